```python
import math
import jax, jax.numpy as jnp
from jax import lax
import numpy as np

D_MODEL = 1024
BATCH = 8
SEQ = 4096
DEPTH = 1

CHUNK = 64
N_META = 16
EPS = 1e-6
SSM_WIDTH = D_MODEL // 4
SSM_GROUP = 16
SSM_GROUPS = SSM_WIDTH // SSM_GROUP
SSM_STATE = 64
DT_MIN = 1e-3
DT_MAX = 1e-1
RET_WIDTH = D_MODEL - SSM_WIDTH
RET_HEAD_DIM = 128
RET_HEADS = RET_WIDTH // RET_HEAD_DIM
ROPE_BASE = 10000.0
IN_WIDTH = SSM_WIDTH + 4 * RET_WIDTH
N_GROUPS = 4
EXPERTS_PER_GROUP = 4
N_EXPERTS = N_GROUPS * EXPERTS_PER_GROUP
TOP_K_INNER = 2
EXPERT_FF = D_MODEL // 4

kernel_name = "hymba_s5_retention_hiermoe_streaming"


def rms_norm(x, g):
    xf = x.astype(jnp.float32)
    y = xf * lax.rsqrt(jnp.mean(xf * xf, axis=-1, keepdims=True) + EPS)
    return (y * g.astype(jnp.float32)).astype(x.dtype)


def s5_mixer(u, lam_re, lam_im, log_dt, b_re, b_im, c_re, c_im, d_skip, w_glu):
    bsz, L, _ = u.shape
    f32 = jnp.float32
    uf = u.astype(f32).reshape(bsz, L, SSM_GROUPS, SSM_GROUP)
    lam = lax.complex(lam_re.astype(f32), lam_im.astype(f32))
    dt = jnp.exp(log_dt.astype(f32))[:, None]
    lam_bar = jnp.exp(lam * dt)
    b = lax.complex(b_re.astype(f32), b_im.astype(f32))
    b_bar = ((lam_bar - 1.0) / lam)[..., None] * b
    bu = jnp.einsum('blgh,gph->lbgp', uf.astype(jnp.complex64), b_bar)
    a = jnp.broadcast_to(lam_bar, (L, 1) + lam_bar.shape)

    def combine(e1, e2):
        a1, x1 = e1
        a2, x2 = e2
        return a1 * a2, a2 * x1 + x2

    _, states = lax.associative_scan(combine, (a, bu), axis=0)
    c = lax.complex(c_re.astype(f32), c_im.astype(f32))
    y = jnp.real(jnp.einsum('lbgp,ghp->blgh', states, c)) + d_skip.astype(f32) * uf
    y = jax.nn.gelu(y.reshape(bsz, L, SSM_WIDTH))
    y = y * jax.nn.sigmoid(y @ w_glu.astype(f32))
    return y.astype(u.dtype)


def retention_mixer(q, k, v, gate):
    bsz, L, _ = q.shape
    f32 = jnp.float32
    pos = jnp.arange(L, dtype=f32)
    inv_freq = ROPE_BASE ** (-jnp.arange(0, RET_HEAD_DIM, 2, dtype=f32) / RET_HEAD_DIM)
    ang = pos[:, None] * inv_freq[None, :]
    ang = jnp.concatenate([ang, ang], axis=-1)[:, None, :]
    cos, sin = jnp.cos(ang), jnp.sin(ang)

    def heads(t):
        return t.astype(f32).reshape(bsz, L, RET_HEADS, RET_HEAD_DIM)

    def rope(t):
        t1, t2 = jnp.split(t, 2, axis=-1)
        return t * cos + jnp.concatenate([-t2, t1], axis=-1) * sin

    qh = rope(heads(q))
    kh = rope(heads(k)) * (RET_HEAD_DIM ** -0.5)
    vh = heads(v)

    pad = CHUNK - N_META
    n_chunks = (L + pad) // CHUNK

    def chunks(t):
        t = jnp.pad(t, ((0, 0), (pad, 0), (0, 0), (0, 0)))
        return t.reshape(bsz, n_chunks, CHUNK, RET_HEADS, RET_HEAD_DIM)

    qc, kc, vc = chunks(qh), chunks(kh), chunks(vh)
    gamma = 1.0 - 2.0 ** (-5.0 - jnp.arange(RET_HEADS, dtype=f32))
    log_g = jnp.log(gamma)
    idx = jnp.arange(CHUNK, dtype=f32)
    intra_decay = jnp.exp(log_g[:, None, None] * jnp.abs(idx[:, None] - idx[None, :]))
    scores = jnp.einsum('bnihd,bnjhd->bnhij', qc, kc) * intra_decay
    intra = jnp.einsum('bnhij,bnjhd->bnihd', scores, vc)
    k_decay = jnp.exp(log_g[None, :] * (CHUNK - 1.0 - idx)[:, None])
    kv = jnp.einsum('bnchd,bnche->nbhde', kc * k_decay[:, :, None], vc)
    chunk_decay = jnp.exp(log_g * CHUNK)[None, :, None, None]

    def step(state, kv_n):
        return state * chunk_decay + kv_n, state

    init = jnp.zeros((bsz, RET_HEADS, RET_HEAD_DIM, RET_HEAD_DIM), f32)
    _, r_prev = lax.scan(step, init, kv)
    q_decay = jnp.exp(log_g[None, :] * (idx + 1.0)[:, None])
    cross = jnp.einsum('bnchd,nbhde->bnche', qc * q_decay[:, :, None], r_prev)
    o = (intra + cross).reshape(bsz, n_chunks * CHUNK, RET_HEADS, RET_HEAD_DIM)[:, pad:]
    mu = jnp.mean(o, axis=-1, keepdims=True)
    var = jnp.mean(jnp.square(o - mu), axis=-1, keepdims=True)
    o = ((o - mu) * lax.rsqrt(var + EPS)).reshape(bsz, L, RET_WIDTH)
    return (jax.nn.silu(gate.astype(f32)) * o).astype(q.dtype)


def hier_moe(t, w_rg, b_rg, w_re, b_re, w_gate, w_up, w_down):
    bsz, L, d = t.shape
    f32 = jnp.float32
    tf = t.reshape(-1, d)
    p_g = jax.nn.softmax((tf @ w_rg + b_rg).astype(f32), axis=-1)
    g_w, g_idx = lax.top_k(p_g, 1)
    logits_all = jnp.einsum('td,gde->tge', tf, w_re) + b_re
    logits_e = jnp.take_along_axis(logits_all, g_idx[:, :, None], axis=1)[:, 0].astype(f32)
    e_val, e_idx = lax.top_k(logits_e, TOP_K_INNER)
    e_w = jax.nn.softmax(e_val, axis=-1)
    within = jnp.einsum('tk,tke->te', e_w, jax.nn.one_hot(e_idx, EXPERTS_PER_GROUP, dtype=f32))
    group_sel = jax.nn.one_hot(g_idx[:, 0], N_GROUPS, dtype=f32) * g_w
    combine = (group_sel[:, :, None] * within[:, None, :]).reshape(-1, N_EXPERTS)
    hg = jnp.einsum('td,edf->tef', tf, w_gate)
    hu = jnp.einsum('td,edf->tef', tf, w_up)
    act = jax.nn.silu(hg) * hu * combine.astype(hg.dtype)[:, :, None]
    out = jnp.einsum('tef,efd->td', act, w_down)
    return out.reshape(bsz, L, d).astype(t.dtype)


def setup_inputs(seed: int = 0) -> dict:
    key = jax.random.key(seed)
    ks = jax.random.split(key, 24)
    f32 = jnp.float32
    n = lambda k, s, sc: jax.random.normal(k, s, f32) * sc
    G, P, H = SSM_GROUPS, SSM_STATE, SSM_GROUP
    lam_re = -0.5 + 0.01 * jax.random.normal(ks[3], (DEPTH, G, P), f32)
    lam_im = math.pi * jnp.arange(P, dtype=f32)[None, None, :] + 0.01 * jax.random.normal(ks[4], (DEPTH, G, P), f32)
    log_dt = jax.random.uniform(ks[5], (DEPTH, G), f32, math.log(DT_MIN), math.log(DT_MAX))
    return {
        "x": n(ks[0], (BATCH, SEQ, D_MODEL), 1.0),
        "meta_tokens": n(ks[1], (N_META, D_MODEL), 1.0),
        "norm_mix_g": 1.0 + n(ks[2], (DEPTH, D_MODEL), 0.02),
        "w_in": n(ks[6], (DEPTH, D_MODEL, IN_WIDTH), D_MODEL ** -0.5),
        "ssm_lambda_re": lam_re,
        "ssm_lambda_im": lam_im,
        "ssm_log_dt": log_dt,
        "ssm_b_re": n(ks[7], (DEPTH, G, P, H), (2 * H) ** -0.5),
        "ssm_b_im": n(ks[8], (DEPTH, G, P, H), (2 * H) ** -0.5),
        "ssm_c_re": n(ks[9], (DEPTH, G, H, P), (2 * P) ** -0.5),
        "ssm_c_im": n(ks[10], (DEPTH, G, H, P), (2 * P) ** -0.5),
        "ssm_d": n(ks[11], (DEPTH, G, H), 1.0),
        "w_glu": n(ks[12], (DEPTH, SSM_WIDTH, SSM_WIDTH), SSM_WIDTH ** -0.5),
        "w_out": n(ks[13], (DEPTH, D_MODEL, D_MODEL), D_MODEL ** -0.5),
        "norm_ffn_g": 1.0 + n(ks[14], (DEPTH, D_MODEL), 0.02),
        "w_router_group": n(ks[15], (DEPTH, D_MODEL, N_GROUPS), D_MODEL ** -0.5),
        "b_router_group": n(ks[16], (DEPTH, N_GROUPS), 0.01),
        "w_router_expert": n(ks[17], (DEPTH, N_GROUPS, D_MODEL, EXPERTS_PER_GROUP), D_MODEL ** -0.5),
        "b_router_expert": n(ks[18], (DEPTH, N_GROUPS, EXPERTS_PER_GROUP), 0.01),
        "w_gate": n(ks[19], (DEPTH, N_EXPERTS, D_MODEL, EXPERT_FF), D_MODEL ** -0.5),
        "w_up": n(ks[20], (DEPTH, N_EXPERTS, D_MODEL, EXPERT_FF), D_MODEL ** -0.5),
        "w_down": n(ks[21], (DEPTH, N_EXPERTS, EXPERT_FF, D_MODEL), EXPERT_FF ** -0.5),
        "norm_final_g": 1.0 + n(ks[22], (D_MODEL,), 0.02),
    }


def reference(x, meta_tokens, norm_mix_g, w_in, ssm_lambda_re, ssm_lambda_im, ssm_log_dt,
              ssm_b_re, ssm_b_im, ssm_c_re, ssm_c_im, ssm_d, w_glu, w_out, norm_ffn_g,
              w_router_group, b_router_group, w_router_expert, b_router_expert,
              w_gate, w_up, w_down, norm_final_g):
    bsz = x.shape[0]
    meta = jnp.broadcast_to(meta_tokens[None].astype(x.dtype), (bsz, N_META, D_MODEL))
    h = jnp.concatenate([meta, x], axis=1)
    splits = [SSM_WIDTH, SSM_WIDTH + RET_WIDTH, SSM_WIDTH + 2 * RET_WIDTH, SSM_WIDTH + 3 * RET_WIDTH]
    for layer in range(DEPTH):
        a = rms_norm(h, norm_mix_g[layer])
        z = a @ w_in[layer]
        u, q, k, v, g = jnp.split(z, splits, axis=-1)
        y_ssm = s5_mixer(u, ssm_lambda_re[layer], ssm_lambda_im[layer], ssm_log_dt[layer],
                         ssm_b_re[layer], ssm_b_im[layer], ssm_c_re[layer], ssm_c_im[layer],
                         ssm_d[layer], w_glu[layer])
        y_ret = retention_mixer(q, k, v, g)
        mixed = jnp.concatenate([y_ssm, y_ret], axis=-1)
        h = h + (mixed @ w_out[layer]).astype(h.dtype)
        h = h + hier_moe(rms_norm(h, norm_ffn_g[layer]), w_router_group[layer], b_router_group[layer],
                         w_router_expert[layer], b_router_expert[layer],
                         w_gate[layer], w_up[layer], w_down[layer])
    return rms_norm(h, norm_final_g)[:, N_META:]
```

```python
import functools
import math

import jax
import jax.numpy as jnp
from jax import lax
from jax.experimental import pallas as pl
from jax.experimental.pallas import tpu as pltpu

D_MODEL = 1024
N_META = 16
CHUNK = 64
EPS = 1e-6
SSM_WIDTH = 256
SSM_GROUP = 16
SSM_GROUPS = 16
SSM_STATE = 64
RET_WIDTH = 768
HEAD_DIM = 128
HEADS = 6
ROPE_BASE = 10000.0
IN_WIDTH = SSM_WIDTH + 4 * RET_WIDTH
N_GROUPS = 4
EXPERTS_PER_GROUP = 4
N_EXPERTS = 16
EXPERT_FF = 256

S5_BLOCK = 16
S5_LANES = S5_BLOCK * SSM_GROUP
RET_BLOCK = 256
TILE = 512
ROUTE_LANES = 128
VMEM_LIMIT = 56 * 1024 * 1024

F32 = jnp.float32
BF16 = jnp.bfloat16


def _dot(a, b):
    return jnp.dot(a, b, preferred_element_type=F32)


def _sigmoid(x):
    return 1.0 / (1.0 + jnp.exp(-x))


def _rms_norm(x, g):
    return x * lax.rsqrt(jnp.mean(x * x, axis=-1, keepdims=True) + EPS) * g


def _rope(t, cos, sin_signed):
    return t * cos + pltpu.roll(t, HEAD_DIM // 2, axis=1) * sin_signed


def _meta_kernel(meta_ref, g_ref, w_ref, cos_ref, sin_ref, kdec_ref, u_ref, r0_ref):
    a = _rms_norm(meta_ref[...], g_ref[...]).astype(BF16)
    u_ref[...] = _dot(a, w_ref[:, :SSM_WIDTH])
    k_off = SSM_WIDTH + RET_WIDTH
    v_off = SSM_WIDTH + 2 * RET_WIDTH
    cos = cos_ref[...]
    sin = sin_ref[...]
    for h in range(HEADS):
        k = _dot(a, w_ref[:, k_off + h * HEAD_DIM:k_off + (h + 1) * HEAD_DIM])
        v = _dot(a, w_ref[:, v_off + h * HEAD_DIM:v_off + (h + 1) * HEAD_DIM])
        kd = (_rope(k, cos, sin) * kdec_ref[h]).astype(BF16)
        r0_ref[h] = lax.dot_general(kd, v.astype(BF16), (((0,), (0,)), ((), ())),
                                    preferred_element_type=F32)


def _meta_call(meta, g, w_in, cos, sin, kdec):
    return pl.pallas_call(
        _meta_kernel,
        out_shape=(jax.ShapeDtypeStruct((N_META, SSM_WIDTH), F32),
                   jax.ShapeDtypeStruct((HEADS, HEAD_DIM, HEAD_DIM), F32)),
        compiler_params=pltpu.CompilerParams(vmem_limit_bytes=VMEM_LIMIT),
        name="meta_call",
    )(meta, g, w_in, cos, sin, kdec)


def _mixer_kernel(x_ref, g_ref, w_ref, cos_ref, sin_ref, mask_ref, qdec_ref, kdec_ref,
                  bdec_ref, r0_ref, u_ref, y_ref, r_ref):
    @pl.when(pl.program_id(1) == 0)
    def _():
        r_ref[...] = r0_ref[...]

    a = _rms_norm(x_ref[...], g_ref[...]).astype(BF16)
    u_ref[...] = _dot(a, w_ref[:, :SSM_WIDTH])
    off = SSM_WIDTH
    q_all = _dot(a, w_ref[:, off:off + RET_WIDTH])
    k_all = _dot(a, w_ref[:, off + RET_WIDTH:off + 2 * RET_WIDTH])
    v_all = _dot(a, w_ref[:, off + 2 * RET_WIDTH:off + 3 * RET_WIDTH])
    gate = _dot(a, w_ref[:, off + 3 * RET_WIDTH:off + 4 * RET_WIDTH])
    cos = cos_ref[...]
    sin = sin_ref[...]
    for h in range(HEADS):
        hs = slice(h * HEAD_DIM, (h + 1) * HEAD_DIM)
        q = _rope(q_all[:, hs], cos, sin)
        k = _rope(k_all[:, hs], cos, sin)
        v = v_all[:, hs].astype(BF16)
        outs = []
        for b in range(TILE // RET_BLOCK):
            bs = slice(b * RET_BLOCK, (b + 1) * RET_BLOCK)
            qb = q[bs]
            kb = k[bs]
            vb = v[bs]
            s = lax.dot_general(qb.astype(BF16), kb.astype(BF16), (((1,), (1,)), ((), ())),
                                preferred_element_type=F32)
            p = (s * mask_ref[h]).astype(BF16)
            state = r_ref[h]
            o = _dot(p, vb) + _dot((qb * qdec_ref[h]).astype(BF16), state.astype(BF16))
            kv = lax.dot_general((kb * kdec_ref[h]).astype(BF16), vb, (((0,), (0,)), ((), ())),
                                 preferred_element_type=F32)
            r_ref[h] = state * bdec_ref[h] + kv
            outs.append(o)
        o = jnp.concatenate(outs, axis=0)
        mu = jnp.mean(o, axis=-1, keepdims=True)
        d = o - mu
        var = jnp.mean(d * d, axis=-1, keepdims=True)
        gt = gate[:, hs]
        y_ref[:, hs] = (gt * _sigmoid(gt) * d * lax.rsqrt(var + EPS)).astype(BF16)


def _mixer_call(x, g, w_in, cos, sin, mask, qdec, kdec, bdec, r0):
    bsz, seq, _ = x.shape
    const = lambda *shape: pl.BlockSpec(shape, lambda b, i: (0,) * len(shape))
    return pl.pallas_call(
        _mixer_kernel,
        grid=(bsz, seq // TILE),
        in_specs=[
            pl.BlockSpec((None, TILE, D_MODEL), lambda b, i: (b, i, 0)),
            const(1, D_MODEL),
            const(D_MODEL, IN_WIDTH),
            pl.BlockSpec((TILE, HEAD_DIM), lambda b, i: (i, 0)),
            pl.BlockSpec((TILE, HEAD_DIM), lambda b, i: (i, 0)),
            const(HEADS, RET_BLOCK, RET_BLOCK),
            const(HEADS, RET_BLOCK, HEAD_DIM),
            const(HEADS, RET_BLOCK, HEAD_DIM),
            const(HEADS, 1, HEAD_DIM),
            const(HEADS, HEAD_DIM, HEAD_DIM),
        ],
        out_specs=(
            pl.BlockSpec((None, TILE, SSM_WIDTH), lambda b, i: (b, i, 0)),
            pl.BlockSpec((None, TILE, RET_WIDTH), lambda b, i: (b, i, 0)),
        ),
        out_shape=(jax.ShapeDtypeStruct((bsz, seq, SSM_WIDTH), F32),
                   jax.ShapeDtypeStruct((bsz, seq, RET_WIDTH), BF16)),
        scratch_shapes=[pltpu.VMEM((HEADS, HEAD_DIM, HEAD_DIM), F32)],
        compiler_params=pltpu.CompilerParams(
            dimension_semantics=("arbitrary", "arbitrary"), vmem_limit_bytes=VMEM_LIMIT),
        name="mixer_call",
    )(x, g, w_in, cos, sin, mask, qdec, kdec, bdec, r0)


def _cmul(a_same, a_cross, x):
    return a_same * x + a_cross * pltpu.roll(x, SSM_STATE, axis=1)


def _s5_kernel(u_ref, um_ref, t0_ref, bmat_ref, cmat_ref, asame_ref, across_ref, y_ref):
    n_blocks = u_ref.shape[0]
    levels = asame_ref.shape[0]
    u = u_ref[...].astype(BF16)
    bmat = bmat_ref[...]
    s0 = _dot(um_ref[...].astype(BF16), bmat)
    v = _dot(u, bmat)
    row = lax.broadcasted_iota(jnp.int32, v.shape, 0)
    s = v + jnp.where(row == 0, _cmul(asame_ref[0:1], across_ref[0:1], s0), 0.0)
    for lv in range(levels):
        d = 1 << lv
        shifted = jnp.where(row >= d, pltpu.roll(s, d, axis=0), 0.0)
        s = s + _cmul(asame_ref[lv:lv + 1], across_ref[lv:lv + 1], shifted)
    s_prev = jnp.where(row == 0, s0, pltpu.roll(s, 1, axis=0))
    y_ref[...] = _dot(u, t0_ref[...]) + _dot(s_prev.astype(BF16), cmat_ref[...])


def _s5_call(ut, um, t0, bmat, cmat, asame, across):
    n_groups, bsz, n_blocks, _ = ut.shape
    levels = asame.shape[1]
    per_group = lambda *shape: pl.BlockSpec((None,) + shape, lambda g, b: (g,) + (0,) * len(shape))
    return pl.pallas_call(
        _s5_kernel,
        grid=(n_groups, bsz),
        in_specs=[
            pl.BlockSpec((None, None, n_blocks, S5_LANES), lambda g, b: (g, b, 0, 0)),
            per_group(1, S5_LANES),
            per_group(S5_LANES, S5_LANES),
            per_group(S5_LANES, 2 * SSM_STATE),
            per_group(2 * SSM_STATE, S5_LANES),
            per_group(levels, 2 * SSM_STATE),
            per_group(levels, 2 * SSM_STATE),
        ],
        out_specs=pl.BlockSpec((None, None, n_blocks, S5_LANES), lambda g, b: (g, b, 0, 0)),
        out_shape=jax.ShapeDtypeStruct(ut.shape, F32),
        compiler_params=pltpu.CompilerParams(
            dimension_semantics=("arbitrary", "arbitrary"), vmem_limit_bytes=VMEM_LIMIT),
        name="s5_call",
    )(ut, um, t0, bmat, cmat, asame, across)


def _s5_operators(lam_re, lam_im, log_dt, b_re, b_im, c_re, c_im, d_skip, levels):
    n_groups = lam_re.shape[0]
    lam = lax.complex(lam_re, lam_im)
    lam_dt = lam * jnp.exp(log_dt)[:, None]
    lam_bar = jnp.exp(lam_dt)
    b_bar = ((lam_bar - 1.0) / lam)[..., None] * lax.complex(b_re, b_im)
    c = lax.complex(c_re, c_im)
    tau = jnp.arange(S5_BLOCK + 1, dtype=F32)
    pows = jnp.exp(lam_dt[None] * tau[:, None, None])
    kern = jnp.real(jnp.einsum('ghp,tgp,gpk->tgkh', c, pows[:S5_BLOCK], b_bar))
    lag = jnp.arange(S5_BLOCK)[None, :] - jnp.arange(S5_BLOCK)[:, None]
    t0 = jnp.where((lag >= 0)[:, :, None, None, None], kern[jnp.clip(lag, 0, S5_BLOCK - 1)], 0.0)
    skip = (jnp.eye(S5_BLOCK, dtype=F32)[:, :, None, None, None]
            * (jnp.eye(SSM_GROUP, dtype=F32)[None, None, None] * d_skip[None, None, :, None, :]))
    t0 = (t0 + skip).transpose(2, 0, 3, 1, 4).reshape(n_groups, S5_LANES, S5_LANES)
    bm = pows[S5_BLOCK - 1 - jnp.arange(S5_BLOCK)][:, :, None, :] * b_bar.transpose(0, 2, 1)[None]
    bm = bm.transpose(1, 0, 2, 3).reshape(n_groups, S5_LANES, SSM_STATE)
    bmat = jnp.concatenate([jnp.real(bm), jnp.imag(bm)], axis=-1)
    cm = c.transpose(0, 2, 1)[:, :, None, :] * pows[1:].transpose(1, 2, 0)[:, :, :, None]
    cm = cm.reshape(n_groups, SSM_STATE, S5_LANES)
    cmat = jnp.concatenate([jnp.real(cm), -jnp.imag(cm)], axis=1)
    step = (S5_BLOCK * (2.0 ** jnp.arange(levels, dtype=F32)))
    adec = jnp.exp(lam_dt[:, None, :] * step[None, :, None])
    asame = jnp.concatenate([jnp.real(adec), jnp.real(adec)], axis=-1)
    across = jnp.concatenate([-jnp.imag(adec), jnp.imag(adec)], axis=-1)
    return t0.astype(BF16), bmat.astype(BF16), cmat.astype(BF16), asame, across


def _proj_kernel(x_ref, ys_ref, yr_ref, wglu_ref, wout_ref, g_ref, wr_hi_ref, wr_lo_ref, br_ref,
                 h_ref, t_ref, c_ref):
    y = jax.nn.gelu(ys_ref[...], approximate=True)
    y = y * _sigmoid(_dot(y.astype(BF16), wglu_ref[...]))
    h = (x_ref[...] + _dot(y.astype(BF16), wout_ref[:SSM_WIDTH, :])
         + _dot(yr_ref[...], wout_ref[SSM_WIDTH:, :]))
    h_ref[...] = h
    t = _rms_norm(h, g_ref[...])
    t_hi = t.astype(BF16)
    t_ref[...] = t_hi
    t_lo = (t - t_hi.astype(F32)).astype(BF16)
    wr_hi = wr_hi_ref[...]
    logits = _dot(t_hi, wr_hi) + _dot(t_lo, wr_hi) + _dot(t_hi, wr_lo_ref[...]) + br_ref[...]

    gl = [logits[:, g:g + 1] for g in range(N_GROUPS)]
    gmax = functools.reduce(jnp.maximum, gl)
    g_w = 1.0 / functools.reduce(lambda a, b: a + b, [jnp.exp(l - gmax) for l in gl])
    sel, taken = [], None
    for l in gl:
        hit = (l >= gmax) if taken is None else jnp.logical_and(l >= gmax, jnp.logical_not(taken))
        taken = hit if taken is None else jnp.logical_or(taken, hit)
        sel.append(hit)
    ev = []
    for e in range(EXPERTS_PER_GROUP):
        acc = jnp.zeros_like(gmax)
        for g in range(N_GROUPS):
            k = N_GROUPS + g * EXPERTS_PER_GROUP + e
            acc = jnp.where(sel[g], logits[:, k:k + 1], acc)
        ev.append(acc)
    m1 = functools.reduce(jnp.maximum, ev)
    first, taken = [], None
    for v in ev:
        hit = (v >= m1) if taken is None else jnp.logical_and(v >= m1, jnp.logical_not(taken))
        taken = hit if taken is None else jnp.logical_or(taken, hit)
        first.append(hit)
    rest = [jnp.where(f, -jnp.inf, v) for f, v in zip(first, ev)]
    m2 = functools.reduce(jnp.maximum, rest)
    second, taken = [], None
    for f, v in zip(first, rest):
        ok = jnp.logical_and(v >= m2, jnp.logical_not(f))
        hit = ok if taken is None else jnp.logical_and(ok, jnp.logical_not(taken))
        taken = hit if taken is None else jnp.logical_or(taken, hit)
        second.append(hit)
    e2 = jnp.exp(m2 - m1)
    w1 = 1.0 / (1.0 + e2)
    w2 = e2 * w1
    within = [jnp.where(f, w1, 0.0) + jnp.where(s, w2, 0.0) for f, s in zip(first, second)]
    lane = lax.broadcasted_iota(jnp.int32, (1, ROUTE_LANES), 1)
    w_lane = jnp.zeros((t.shape[0], ROUTE_LANES), F32)
    for e in range(EXPERTS_PER_GROUP):
        w_lane = jnp.where(lane == e, within[e], w_lane)
    for g in range(N_GROUPS):
        c_ref[:, g * ROUTE_LANES:(g + 1) * ROUTE_LANES] = jnp.where(sel[g], g_w, 0.0) * w_lane


def _proj_call(x, ys, yr, wglu, wout, g, wr_hi, wr_lo, br):
    n_tok = x.shape[0]
    const = lambda *shape: pl.BlockSpec(shape, lambda i: (0,) * len(shape))
    rows = lambda width: pl.BlockSpec((TILE, width), lambda i: (i, 0))
    return pl.pallas_call(
        _proj_kernel,
        grid=(n_tok // TILE,),
        in_specs=[
            rows(D_MODEL), rows(SSM_WIDTH), rows(RET_WIDTH),
            const(SSM_WIDTH, SSM_WIDTH), const(D_MODEL, D_MODEL), const(1, D_MODEL),
            const(D_MODEL, ROUTE_LANES), const(D_MODEL, ROUTE_LANES), const(1, ROUTE_LANES),
        ],
        out_specs=(rows(D_MODEL), rows(D_MODEL), rows(N_GROUPS * ROUTE_LANES)),
        out_shape=(jax.ShapeDtypeStruct((n_tok, D_MODEL), F32),
                   jax.ShapeDtypeStruct((n_tok, D_MODEL), BF16),
                   jax.ShapeDtypeStruct((n_tok, N_GROUPS * ROUTE_LANES), F32)),
        compiler_params=pltpu.CompilerParams(
            dimension_semantics=("arbitrary",), vmem_limit_bytes=VMEM_LIMIT),
        name="proj_call",
    )(x, ys, yr, wglu, wout, g, wr_hi, wr_lo, br)


def _moe_kernel(t_ref, c_ref, h_ref, wg_ref, wu_ref, wd_ref, g_ref, o_ref, acc_ref):
    grp = pl.program_id(1)

    @pl.when(grp == 0)
    def _():
        acc_ref[...] = h_ref[...]

    t = t_ref[...]
    c = c_ref[...]
    acc = acc_ref[...]
    for e in range(EXPERTS_PER_GROUP):
        hg = _dot(t, wg_ref[e])
        hu = _dot(t, wu_ref[e])
        act = hg * _sigmoid(hg) * hu * c[:, e:e + 1]
        acc = acc + _dot(act.astype(BF16), wd_ref[e])
    acc_ref[...] = acc

    @pl.when(grp == N_GROUPS - 1)
    def _():
        o_ref[...] = _rms_norm(acc, g_ref[...])


def _moe_call(t, c, h, wg, wu, wd, g):
    n_tok = t.shape[0]
    return pl.pallas_call(
        _moe_kernel,
        grid=(n_tok // TILE, N_GROUPS),
        in_specs=[
            pl.BlockSpec((TILE, D_MODEL), lambda i, k: (i, 0)),
            pl.BlockSpec((TILE, ROUTE_LANES), lambda i, k: (i, k)),
            pl.BlockSpec((TILE, D_MODEL), lambda i, k: (i, 0)),
            pl.BlockSpec((EXPERTS_PER_GROUP, D_MODEL, EXPERT_FF), lambda i, k: (k, 0, 0)),
            pl.BlockSpec((EXPERTS_PER_GROUP, D_MODEL, EXPERT_FF), lambda i, k: (k, 0, 0)),
            pl.BlockSpec((EXPERTS_PER_GROUP, EXPERT_FF, D_MODEL), lambda i, k: (k, 0, 0)),
            pl.BlockSpec((1, D_MODEL), lambda i, k: (0, 0)),
        ],
        out_specs=pl.BlockSpec((TILE, D_MODEL), lambda i, k: (i, 0)),
        out_shape=jax.ShapeDtypeStruct((n_tok, D_MODEL), F32),
        scratch_shapes=[pltpu.VMEM((TILE, D_MODEL), F32)],
        compiler_params=pltpu.CompilerParams(
            dimension_semantics=("arbitrary", "arbitrary"), vmem_limit_bytes=VMEM_LIMIT),
        name="moe_call",
    )(t, c, h, wg, wu, wd, g)


def _rope_tables(length):
    pos = jnp.arange(length, dtype=F32)
    inv_freq = ROPE_BASE ** (-jnp.arange(0, HEAD_DIM, 2, dtype=F32) / HEAD_DIM)
    ang = pos[:, None] * inv_freq[None, :]
    cos, sin = jnp.cos(ang), jnp.sin(ang)
    return jnp.concatenate([cos, cos], axis=-1), jnp.concatenate([-sin, sin], axis=-1)


def _retention_tables():
    gamma = 1.0 - 2.0 ** (-5.0 - jnp.arange(HEADS, dtype=F32))
    log_g = jnp.log(gamma)[:, None, None]
    scale = HEAD_DIM ** -0.5
    idx = jnp.arange(RET_BLOCK)
    dist = jnp.abs(idx[:, None] - idx[None, :]).astype(F32)
    visible = (idx[None, :] // CHUNK) <= (idx[:, None] // CHUNK)
    mask = jnp.where(visible[None], jnp.exp(log_g * dist[None]), 0.0) * scale
    ones = jnp.ones((1, 1, HEAD_DIM), F32)
    idx_f = idx.astype(F32)[None, :, None]
    qdec = jnp.exp(log_g * (idx_f + 1.0)) * ones
    kdec = jnp.exp(log_g * (RET_BLOCK - 1.0 - idx_f)) * scale * ones
    bdec = jnp.exp(log_g * float(RET_BLOCK)) * ones
    meta_idx = jnp.arange(N_META, dtype=F32)[None, :, None]
    kdec_meta = jnp.exp(log_g * (N_META - 1.0 - meta_idx)) * scale * ones
    return mask, qdec, kdec, bdec, kdec_meta


def kernel(x, meta_tokens, norm_mix_g, w_in, ssm_lambda_re, ssm_lambda_im, ssm_log_dt, ssm_b_re, ssm_b_im, ssm_c_re, ssm_c_im, ssm_d, w_glu, w_out, norm_ffn_g, w_router_group, b_router_group, w_router_expert, b_router_expert, w_gate, w_up, w_down, norm_final_g):
    bsz, seq, _ = x.shape
    assert seq % TILE == 0 and TILE % RET_BLOCK == 0 and RET_BLOCK % CHUNK == 0
    n_blocks = seq // S5_BLOCK
    levels = int(math.log2(n_blocks))
    assert 1 << levels == n_blocks
    n_tok = bsz * seq

    cos, sin = _rope_tables(N_META + seq)
    mask, qdec, kdec, bdec, kdec_meta = _retention_tables()
    g_mix = norm_mix_g[0][None, :]
    w_in_b = w_in[0].astype(BF16)

    u_meta, r0 = _meta_call(meta_tokens, g_mix, w_in_b, cos[:N_META], sin[:N_META], kdec_meta)
    u, y_ret = _mixer_call(x, g_mix, w_in_b, cos[N_META:], sin[N_META:], mask, qdec, kdec, bdec, r0)

    t0, bmat, cmat, asame, across = _s5_operators(
        ssm_lambda_re[0], ssm_lambda_im[0], ssm_log_dt[0], ssm_b_re[0], ssm_b_im[0],
        ssm_c_re[0], ssm_c_im[0], ssm_d[0], levels)
    ut = u.reshape(bsz, n_blocks, S5_BLOCK, SSM_GROUPS, SSM_GROUP).transpose(3, 0, 1, 2, 4)
    ut = ut.reshape(SSM_GROUPS, bsz, n_blocks, S5_LANES)
    um = u_meta.reshape(S5_BLOCK, SSM_GROUPS, SSM_GROUP).transpose(1, 0, 2).reshape(SSM_GROUPS, 1, S5_LANES)
    yt = _s5_call(ut, um, t0, bmat, cmat, asame, across)
    y_ssm = yt.reshape(SSM_GROUPS, bsz, n_blocks, S5_BLOCK, SSM_GROUP).transpose(1, 2, 3, 0, 4)
    y_ssm = y_ssm.reshape(n_tok, SSM_WIDTH)

    w_r = jnp.concatenate(
        [w_router_group[0], w_router_expert[0].transpose(1, 0, 2).reshape(D_MODEL, N_EXPERTS)], axis=1)
    w_r = jnp.pad(w_r, ((0, 0), (0, ROUTE_LANES - w_r.shape[1])))
    b_r = jnp.concatenate([b_router_group[0], b_router_expert[0].reshape(-1)])
    b_r = jnp.pad(b_r, (0, ROUTE_LANES - b_r.shape[0]))[None, :]
    w_r_hi = w_r.astype(BF16)
    w_r_lo = (w_r - w_r_hi.astype(F32)).astype(BF16)

    h, t, c = _proj_call(x.reshape(n_tok, D_MODEL), y_ssm, y_ret.reshape(n_tok, RET_WIDTH),
                         w_glu[0].astype(BF16), w_out[0].astype(BF16), norm_ffn_g[0][None, :],
                         w_r_hi, w_r_lo, b_r)
    out = _moe_call(t, c, h, w_gate[0].astype(BF16), w_up[0].astype(BF16), w_down[0].astype(BF16),
                    norm_final_g[None, :])
    return out.reshape(bsz, seq, D_MODEL)
```

```python
import functools
import math

import jax
import jax.numpy as jnp
from jax import lax
from jax.experimental import pallas as pl
from jax.experimental.pallas import tpu as pltpu

D_MODEL = 1024
N_META = 16
CHUNK = 64
EPS = 1e-6
SSM_WIDTH = 256
SSM_GROUP = 16
SSM_GROUPS = 16
SSM_STATE = 64
RET_WIDTH = 768
HEAD_DIM = 128
HEADS = 6
ROPE_BASE = 10000.0
IN_WIDTH = SSM_WIDTH + 4 * RET_WIDTH
N_GROUPS = 4
EXPERTS_PER_GROUP = 4
N_EXPERTS = 16
EXPERT_FF = 256

S5_BLOCK = 16
S5_LANES = S5_BLOCK * SSM_GROUP
RET_BLOCK = 256
TILE = 512
LANES = 128
PIECE = 16
TILE_PIECES = TILE // PIECE
SORTED_PIECES = TILE_PIECES + N_GROUPS
SORTED_ROWS = SORTED_PIECES * PIECE
EXT_WIDTH = D_MODEL + LANES
VMEM_LIMIT = 56 * 1024 * 1024

F32 = jnp.float32
BF16 = jnp.bfloat16
I32 = jnp.int32


def _dot(a, b):
    return jnp.dot(a, b, preferred_element_type=F32)


def _sigmoid(x):
    return 1.0 / (1.0 + jnp.exp(-x))


def _rms_norm(x, g):
    return x * lax.rsqrt(jnp.mean(x * x, axis=-1, keepdims=True) + EPS) * g


def _rope(t, cos, sin_signed):
    return t * cos + pltpu.roll(t, HEAD_DIM // 2, axis=1) * sin_signed


def _meta_kernel(meta_ref, g_ref, w_ref, cos_ref, sin_ref, kdec_ref, u_ref, r0_ref):
    a = _rms_norm(meta_ref[...], g_ref[...]).astype(BF16)
    u_ref[...] = _dot(a, w_ref[:, :SSM_WIDTH])
    k_off = SSM_WIDTH + RET_WIDTH
    v_off = SSM_WIDTH + 2 * RET_WIDTH
    cos = cos_ref[...]
    sin = sin_ref[...]
    for h in range(HEADS):
        k = _dot(a, w_ref[:, k_off + h * HEAD_DIM:k_off + (h + 1) * HEAD_DIM])
        v = _dot(a, w_ref[:, v_off + h * HEAD_DIM:v_off + (h + 1) * HEAD_DIM])
        kd = (_rope(k, cos, sin) * kdec_ref[h]).astype(BF16)
        r0_ref[h] = lax.dot_general(kd, v.astype(BF16), (((0,), (0,)), ((), ())),
                                    preferred_element_type=F32)


def _meta_call(meta, g, w_in, cos, sin, kdec):
    return pl.pallas_call(
        _meta_kernel,
        out_shape=(jax.ShapeDtypeStruct((N_META, SSM_WIDTH), F32),
                   jax.ShapeDtypeStruct((HEADS, HEAD_DIM, HEAD_DIM), F32)),
        compiler_params=pltpu.CompilerParams(vmem_limit_bytes=VMEM_LIMIT),
        name="meta_call",
    )(meta, g, w_in, cos, sin, kdec)


def _mixer_kernel(x_ref, g_ref, w_ref, cos_ref, sin_ref, mask_ref, qdec_ref, kdec_ref,
                  bdec_ref, r0_ref, u_ref, y_ref, r_ref):
    @pl.when(pl.program_id(1) == 0)
    def _():
        r_ref[...] = r0_ref[...]

    a = _rms_norm(x_ref[...], g_ref[...]).astype(BF16)
    u_ref[...] = _dot(a, w_ref[:, :SSM_WIDTH])
    off = SSM_WIDTH
    q_all = _dot(a, w_ref[:, off:off + RET_WIDTH])
    k_all = _dot(a, w_ref[:, off + RET_WIDTH:off + 2 * RET_WIDTH])
    v_all = _dot(a, w_ref[:, off + 2 * RET_WIDTH:off + 3 * RET_WIDTH])
    gate = _dot(a, w_ref[:, off + 3 * RET_WIDTH:off + 4 * RET_WIDTH])
    cos = cos_ref[...]
    sin = sin_ref[...]
    for h in range(HEADS):
        hs = slice(h * HEAD_DIM, (h + 1) * HEAD_DIM)
        q = _rope(q_all[:, hs], cos, sin)
        k = _rope(k_all[:, hs], cos, sin)
        v = v_all[:, hs].astype(BF16)
        outs = []
        for b in range(TILE // RET_BLOCK):
            bs = slice(b * RET_BLOCK, (b + 1) * RET_BLOCK)
            qb = q[bs]
            kb = k[bs]
            vb = v[bs]
            s = lax.dot_general(qb.astype(BF16), kb.astype(BF16), (((1,), (1,)), ((), ())),
                                preferred_element_type=F32)
            p = (s * mask_ref[h]).astype(BF16)
            state = r_ref[h]
            o = _dot(p, vb) + _dot((qb * qdec_ref[h]).astype(BF16), state.astype(BF16))
            kv = lax.dot_general((kb * kdec_ref[h]).astype(BF16), vb, (((0,), (0,)), ((), ())),
                                 preferred_element_type=F32)
            r_ref[h] = state * bdec_ref[h] + kv
            outs.append(o)
        o = jnp.concatenate(outs, axis=0)
        mu = jnp.mean(o, axis=-1, keepdims=True)
        d = o - mu
        var = jnp.mean(d * d, axis=-1, keepdims=True)
        gt = gate[:, hs]
        y_ref[:, hs] = (gt * _sigmoid(gt) * d * lax.rsqrt(var + EPS)).astype(BF16)


def _mixer_call(x, g, w_in, cos, sin, mask, qdec, kdec, bdec, r0):
    bsz, seq, _ = x.shape
    const = lambda *shape: pl.BlockSpec(shape, lambda b, i: (0,) * len(shape))
    return pl.pallas_call(
        _mixer_kernel,
        grid=(bsz, seq // TILE),
        in_specs=[
            pl.BlockSpec((None, TILE, D_MODEL), lambda b, i: (b, i, 0)),
            const(1, D_MODEL),
            const(D_MODEL, IN_WIDTH),
            pl.BlockSpec((TILE, HEAD_DIM), lambda b, i: (i, 0)),
            pl.BlockSpec((TILE, HEAD_DIM), lambda b, i: (i, 0)),
            const(HEADS, RET_BLOCK, RET_BLOCK),
            const(HEADS, RET_BLOCK, HEAD_DIM),
            const(HEADS, RET_BLOCK, HEAD_DIM),
            const(HEADS, 1, HEAD_DIM),
            const(HEADS, HEAD_DIM, HEAD_DIM),
        ],
        out_specs=(
            pl.BlockSpec((None, TILE, SSM_WIDTH), lambda b, i: (b, i, 0)),
            pl.BlockSpec((None, TILE, RET_WIDTH), lambda b, i: (b, i, 0)),
        ),
        out_shape=(jax.ShapeDtypeStruct((bsz, seq, SSM_WIDTH), F32),
                   jax.ShapeDtypeStruct((bsz, seq, RET_WIDTH), BF16)),
        scratch_shapes=[pltpu.VMEM((HEADS, HEAD_DIM, HEAD_DIM), F32)],
        compiler_params=pltpu.CompilerParams(
            dimension_semantics=("arbitrary", "arbitrary"), vmem_limit_bytes=VMEM_LIMIT),
        name="mixer_call",
    )(x, g, w_in, cos, sin, mask, qdec, kdec, bdec, r0)


def _cmul(a_same, a_cross, x):
    return a_same * x + a_cross * pltpu.roll(x, SSM_STATE, axis=1)


def _s5_kernel(u_ref, um_ref, t0_ref, bmat_ref, cmat_ref, asame_ref, across_ref, y_ref):
    levels = asame_ref.shape[0]
    u = u_ref[...].astype(BF16)
    bmat = bmat_ref[...]
    s0 = _dot(um_ref[...].astype(BF16), bmat)
    v = _dot(u, bmat)
    row = lax.broadcasted_iota(I32, v.shape, 0)
    s = v + jnp.where(row == 0, _cmul(asame_ref[0:1], across_ref[0:1], s0), 0.0)
    for lv in range(levels):
        d = 1 << lv
        shifted = jnp.where(row >= d, pltpu.roll(s, d, axis=0), 0.0)
        s = s + _cmul(asame_ref[lv:lv + 1], across_ref[lv:lv + 1], shifted)
    s_prev = jnp.where(row == 0, s0, pltpu.roll(s, 1, axis=0))
    y_ref[...] = _dot(u, t0_ref[...]) + _dot(s_prev.astype(BF16), cmat_ref[...])


def _s5_call(ut, um, t0, bmat, cmat, asame, across):
    n_groups, bsz, n_blocks, _ = ut.shape
    levels = asame.shape[1]
    per_group = lambda *shape: pl.BlockSpec((None,) + shape, lambda g, b: (g,) + (0,) * len(shape))
    return pl.pallas_call(
        _s5_kernel,
        grid=(n_groups, bsz),
        in_specs=[
            pl.BlockSpec((None, None, n_blocks, S5_LANES), lambda g, b: (g, b, 0, 0)),
            per_group(1, S5_LANES),
            per_group(S5_LANES, S5_LANES),
            per_group(S5_LANES, 2 * SSM_STATE),
            per_group(2 * SSM_STATE, S5_LANES),
            per_group(levels, 2 * SSM_STATE),
            per_group(levels, 2 * SSM_STATE),
        ],
        out_specs=pl.BlockSpec((None, None, n_blocks, S5_LANES), lambda g, b: (g, b, 0, 0)),
        out_shape=jax.ShapeDtypeStruct(ut.shape, F32),
        compiler_params=pltpu.CompilerParams(
            dimension_semantics=("arbitrary", "arbitrary"), vmem_limit_bytes=VMEM_LIMIT),
        name="s5_call",
    )(ut, um, t0, bmat, cmat, asame, across)


def _s5_operators(lam_re, lam_im, log_dt, b_re, b_im, c_re, c_im, d_skip, levels):
    n_groups = lam_re.shape[0]
    lam = lax.complex(lam_re, lam_im)
    lam_dt = lam * jnp.exp(log_dt)[:, None]
    lam_bar = jnp.exp(lam_dt)
    b_bar = ((lam_bar - 1.0) / lam)[..., None] * lax.complex(b_re, b_im)
    c = lax.complex(c_re, c_im)
    tau = jnp.arange(S5_BLOCK + 1, dtype=F32)
    pows = jnp.exp(lam_dt[None] * tau[:, None, None])
    kern = jnp.real(jnp.einsum('ghp,tgp,gpk->tgkh', c, pows[:S5_BLOCK], b_bar))
    lag = jnp.arange(S5_BLOCK)[None, :] - jnp.arange(S5_BLOCK)[:, None]
    t0 = jnp.where((lag >= 0)[:, :, None, None, None], kern[jnp.clip(lag, 0, S5_BLOCK - 1)], 0.0)
    skip = (jnp.eye(S5_BLOCK, dtype=F32)[:, :, None, None, None]
            * (jnp.eye(SSM_GROUP, dtype=F32)[None, None, None] * d_skip[None, None, :, None, :]))
    t0 = (t0 + skip).transpose(2, 0, 3, 1, 4).reshape(n_groups, S5_LANES, S5_LANES)
    bm = pows[S5_BLOCK - 1 - jnp.arange(S5_BLOCK)][:, :, None, :] * b_bar.transpose(0, 2, 1)[None]
    bm = bm.transpose(1, 0, 2, 3).reshape(n_groups, S5_LANES, SSM_STATE)
    bmat = jnp.concatenate([jnp.real(bm), jnp.imag(bm)], axis=-1)
    cm = c.transpose(0, 2, 1)[:, :, None, :] * pows[1:].transpose(1, 2, 0)[:, :, :, None]
    cm = cm.reshape(n_groups, SSM_STATE, S5_LANES)
    cmat = jnp.concatenate([jnp.real(cm), -jnp.imag(cm)], axis=1)
    step = (S5_BLOCK * (2.0 ** jnp.arange(levels, dtype=F32)))
    adec = jnp.exp(lam_dt[:, None, :] * step[None, :, None])
    asame = jnp.concatenate([jnp.real(adec), jnp.real(adec)], axis=-1)
    across = jnp.concatenate([-jnp.imag(adec), jnp.imag(adec)], axis=-1)
    return t0.astype(BF16), bmat.astype(BF16), cmat.astype(BF16), asame, across


def _first_hit(values, target):
    hits, taken = [], None
    for v in values:
        hit = (v >= target) if taken is None else jnp.logical_and(v >= target, jnp.logical_not(taken))
        taken = hit if taken is None else jnp.logical_or(taken, hit)
        hits.append(hit)
    return hits


def _stack_rows(rows, n_rows):
    idx = lax.broadcasted_iota(I32, (n_rows, rows[0].shape[1]), 0)
    out = jnp.zeros((n_rows, rows[0].shape[1]), F32)
    for k, r in enumerate(rows):
        out = jnp.where(idx == k, r, out)
    return out


def _proj_kernel(x_ref, ys_ref, yr_ref, wglu_ref, wout_ref, g_ref, wr_hi_ref, wr_lo_ref, br_ref,
                 tri_ref, h_ref, stage_ref, dest_ref, cnt_ref):
    y = jax.nn.gelu(ys_ref[...], approximate=True)
    y = y * _sigmoid(_dot(y.astype(BF16), wglu_ref[...]))
    h = (x_ref[...] + _dot(y.astype(BF16), wout_ref[:SSM_WIDTH, :])
         + _dot(yr_ref[...], wout_ref[SSM_WIDTH:, :]))
    h_ref[...] = h
    t = _rms_norm(h, g_ref[...])
    t_hi = t.astype(BF16)
    t_lo = (t - t_hi.astype(F32)).astype(BF16)
    wr_hi = wr_hi_ref[...]
    logits = _dot(t_hi, wr_hi) + _dot(t_lo, wr_hi) + _dot(t_hi, wr_lo_ref[...]) + br_ref[...]
    lt = logits.T

    gl = [lt[g:g + 1, :] for g in range(N_GROUPS)]
    gmax = functools.reduce(jnp.maximum, gl)
    g_w = 1.0 / functools.reduce(lambda a, b: a + b, [jnp.exp(l - gmax) for l in gl])
    sel = _first_hit(gl, gmax)
    ev = []
    for e in range(EXPERTS_PER_GROUP):
        acc = jnp.zeros_like(gmax)
        for g in range(N_GROUPS):
            k = N_GROUPS + g * EXPERTS_PER_GROUP + e
            acc = jnp.where(sel[g], lt[k:k + 1, :], acc)
        ev.append(acc)
    m1 = functools.reduce(jnp.maximum, ev)
    first = _first_hit(ev, m1)
    rest = [jnp.where(f, -jnp.inf, v) for f, v in zip(first, ev)]
    m2 = functools.reduce(jnp.maximum, rest)
    second = _first_hit(rest, m2)
    e2 = jnp.exp(m2 - m1)
    w1 = g_w / (1.0 + e2)
    w2 = e2 * w1
    combine = [jnp.where(f, w1, 0.0) + jnp.where(s, w2, 0.0) for f, s in zip(first, second)]

    sel_f = [jnp.where(s, 1.0, 0.0) for s in sel]
    incl = _dot(_stack_rows(sel_f, 8).astype(BF16), tri_ref[...])
    dest = jnp.zeros_like(gmax)
    seg_start = jnp.zeros((1, 1), F32)
    counts = []
    for g in range(N_GROUPS):
        run = incl[g:g + 1, :]
        cnt = run[:, TILE - 1:TILE]
        counts.append(cnt)
        dest = dest + sel_f[g] * (seg_start + run - 1.0)
        seg_start = seg_start + PIECE * jnp.floor((cnt + (PIECE - 1.0)) * (1.0 / PIECE))
    dest_i = dest.astype(I32)
    perm = jnp.where(lax.broadcasted_iota(I32, (SORTED_ROWS, TILE), 0) == dest_i, 1.0, 0.0).astype(BF16)

    c_hi = [c.astype(BF16).astype(F32) for c in combine]
    c_lo = [c - hi for c, hi in zip(combine, c_hi)]
    cw = _stack_rows(c_hi + c_lo, LANES).T
    t_ext = jnp.concatenate([t_hi, cw.astype(BF16)], axis=1)
    stage_ref[...] = _dot(perm, t_ext).astype(BF16)
    dest_ref[...] = _stack_rows([dest], 8)
    cnt_ref[...] = _stack_rows([c + jnp.zeros((1, LANES), F32) for c in counts], 8)


def _proj_call(x, ys, yr, wglu, wout, g, wr_hi, wr_lo, br, tri):
    n_tok = x.shape[0]
    n_tiles = n_tok // TILE
    const = lambda *shape: pl.BlockSpec(shape, lambda i: (0,) * len(shape))
    rows = lambda width: pl.BlockSpec((TILE, width), lambda i: (i, 0))
    return pl.pallas_call(
        _proj_kernel,
        grid=(n_tiles,),
        in_specs=[
            rows(D_MODEL), rows(SSM_WIDTH), rows(RET_WIDTH),
            const(SSM_WIDTH, SSM_WIDTH), const(D_MODEL, D_MODEL), const(1, D_MODEL),
            const(D_MODEL, LANES), const(D_MODEL, LANES), const(1, LANES), const(TILE, TILE),
        ],
        out_specs=(rows(D_MODEL),
                   pl.BlockSpec((SORTED_ROWS, EXT_WIDTH), lambda i: (i, 0)),
                   pl.BlockSpec((None, 8, TILE), lambda i: (i, 0, 0)),
                   pl.BlockSpec((None, 8, LANES), lambda i: (i, 0, 0))),
        out_shape=(jax.ShapeDtypeStruct((n_tok, D_MODEL), F32),
                   jax.ShapeDtypeStruct((n_tiles * SORTED_ROWS, EXT_WIDTH), BF16),
                   jax.ShapeDtypeStruct((n_tiles, 8, TILE), F32),
                   jax.ShapeDtypeStruct((n_tiles, 8, LANES), F32)),
        compiler_params=pltpu.CompilerParams(
            dimension_semantics=("arbitrary",), vmem_limit_bytes=VMEM_LIMIT),
        name="proj_call",
    )(x, ys, yr, wglu, wout, g, wr_hi, wr_lo, br, tri)


def _sort_tables(cnt, n_steps):
    n_tiles = cnt.shape[0]
    npc = (cnt + PIECE - 1) // PIECE
    seg = jnp.cumsum(npc, axis=1) - npc
    before = jnp.cumsum(npc, axis=0) - npc
    n_tile_g = (jnp.sum(npc, axis=0) + TILE_PIECES - 1) // TILE_PIECES
    t_off = jnp.cumsum(n_tile_g) - n_tile_g
    j = jnp.arange(SORTED_PIECES, dtype=I32)[None, :, None]
    in_g = jnp.logical_and(j >= seg[:, None, :], j < (seg + npc)[:, None, :])
    pos = jnp.sum(jnp.where(in_g, TILE_PIECES * t_off[None, None, :] + before[:, None, :] + j - seg[:, None, :], 0),
                  axis=-1)
    valid = jnp.any(in_g, axis=-1)
    stage_piece = jnp.arange(n_tiles * SORTED_PIECES, dtype=I32).reshape(n_tiles, SORTED_PIECES)
    n_slots = n_steps * TILE_PIECES
    src = jnp.zeros((n_slots,), I32).at[jnp.where(valid, pos, n_slots).reshape(-1)].set(
        stage_piece.reshape(-1), mode='drop')
    steps = jnp.arange(n_steps, dtype=I32)
    g_step = jnp.minimum(jnp.sum(steps[:, None] >= (t_off + n_tile_g)[None, :], axis=1), N_GROUPS - 1)
    n_live = jnp.sum(n_tile_g).reshape(1)
    back = jnp.where(valid, pos, 0).reshape(-1)
    return src, g_step.astype(I32), n_live.astype(I32), back.astype(I32)


def _piece_copy(src_ref, piece, buf_ref, slot, j, sem_ref):
    start = piece * PIECE if isinstance(piece, int) else pl.multiple_of(piece * PIECE, PIECE)
    return pltpu.make_async_copy(src_ref.at[pl.ds(start, PIECE)],
                                 buf_ref.at[slot, pl.ds(j * PIECE, PIECE)], sem_ref.at[slot])


def _gather_pieces(table_ref, src_ref, buf_ref, sem_ref, step, n_steps, n_pieces):
    def fetch(s):
        for j in range(n_pieces):
            _piece_copy(src_ref, table_ref[s * n_pieces + j], buf_ref, s % 2, j, sem_ref).start()

    @pl.when(step == 0)
    def _():
        fetch(step)

    @pl.when(step + 1 < n_steps)
    def _():
        fetch(step + 1)

    for j in range(n_pieces):
        _piece_copy(src_ref, 0, buf_ref, step % 2, j, sem_ref).wait()


def _moe_kernel(src_ref, gstep_ref, nlive_ref, stage_ref, wg_ref, wu_ref, wd_ref, y_ref, buf_ref, sem_ref):
    step = pl.program_id(0)
    _gather_pieces(src_ref, stage_ref, buf_ref, sem_ref, step, pl.num_programs(0), TILE_PIECES)

    @pl.when(step < nlive_ref[0])
    def _():
        rows = buf_ref[step % 2]
        t = rows[:, :D_MODEL]
        cw = rows[:, D_MODEL:].astype(F32)
        acc = jnp.zeros((TILE, D_MODEL), F32)
        for e in range(EXPERTS_PER_GROUP):
            c = cw[:, e:e + 1] + cw[:, EXPERTS_PER_GROUP + e:EXPERTS_PER_GROUP + e + 1]
            hg = _dot(t, wg_ref[e])
            hu = _dot(t, wu_ref[e])
            act = hg * _sigmoid(hg) * hu * c
            acc = acc + _dot(act.astype(BF16), wd_ref[e])
        y_ref[...] = acc.astype(BF16)

    @pl.when(step >= nlive_ref[0])
    def _():
        y_ref[...] = jnp.zeros_like(y_ref)


def _moe_call(src, g_step, n_live, stage, wg, wu, wd, n_steps):
    grp = lambda shape: pl.BlockSpec((EXPERTS_PER_GROUP,) + shape, lambda s, src, gs, nl: (gs[s], 0, 0))
    return pl.pallas_call(
        _moe_kernel,
        grid_spec=pltpu.PrefetchScalarGridSpec(
            num_scalar_prefetch=3,
            grid=(n_steps,),
            in_specs=[pl.BlockSpec(memory_space=pl.ANY),
                      grp((D_MODEL, EXPERT_FF)), grp((D_MODEL, EXPERT_FF)), grp((EXPERT_FF, D_MODEL))],
            out_specs=pl.BlockSpec((TILE, D_MODEL), lambda s, src, gs, nl: (s, 0)),
            scratch_shapes=[pltpu.VMEM((2, TILE, EXT_WIDTH), BF16), pltpu.SemaphoreType.DMA((2,))],
        ),
        out_shape=jax.ShapeDtypeStruct((n_steps * TILE, D_MODEL), BF16),
        compiler_params=pltpu.CompilerParams(
            dimension_semantics=("arbitrary",), vmem_limit_bytes=VMEM_LIMIT),
        name="moe_call",
    )(src, g_step, n_live, stage, wg, wu, wd)


def _final_kernel(back_ref, ysort_ref, h_ref, dest_ref, g_ref, o_ref, buf_ref, sem_ref):
    step = pl.program_id(0)
    _gather_pieces(back_ref, ysort_ref, buf_ref, sem_ref, step, pl.num_programs(0), SORTED_PIECES)
    dest = _stack_rows([dest_ref[0:1, :]], LANES).T[:, 0:1].astype(I32)
    unperm = jnp.where(lax.broadcasted_iota(I32, (TILE, SORTED_ROWS), 1) == dest, 1.0, 0.0).astype(BF16)
    o_ref[...] = _rms_norm(h_ref[...] + _dot(unperm, buf_ref[step % 2]), g_ref[...])


def _final_call(back, ysort, h, dest, g):
    n_tok = h.shape[0]
    return pl.pallas_call(
        _final_kernel,
        grid_spec=pltpu.PrefetchScalarGridSpec(
            num_scalar_prefetch=1,
            grid=(n_tok // TILE,),
            in_specs=[pl.BlockSpec(memory_space=pl.ANY),
                      pl.BlockSpec((TILE, D_MODEL), lambda i, back: (i, 0)),
                      pl.BlockSpec((None, 8, TILE), lambda i, back: (i, 0, 0)),
                      pl.BlockSpec((1, D_MODEL), lambda i, back: (0, 0))],
            out_specs=pl.BlockSpec((TILE, D_MODEL), lambda i, back: (i, 0)),
            scratch_shapes=[pltpu.VMEM((2, SORTED_ROWS, D_MODEL), BF16), pltpu.SemaphoreType.DMA((2,))],
        ),
        out_shape=jax.ShapeDtypeStruct((n_tok, D_MODEL), F32),
        compiler_params=pltpu.CompilerParams(
            dimension_semantics=("arbitrary",), vmem_limit_bytes=VMEM_LIMIT),
        name="final_call",
    )(back, ysort, h, dest, g)


def _rope_tables(length):
    pos = jnp.arange(length, dtype=F32)
    inv_freq = ROPE_BASE ** (-jnp.arange(0, HEAD_DIM, 2, dtype=F32) / HEAD_DIM)
    ang = pos[:, None] * inv_freq[None, :]
    cos, sin = jnp.cos(ang), jnp.sin(ang)
    return jnp.concatenate([cos, cos], axis=-1), jnp.concatenate([-sin, sin], axis=-1)


def _retention_tables():
    gamma = 1.0 - 2.0 ** (-5.0 - jnp.arange(HEADS, dtype=F32))
    log_g = jnp.log(gamma)[:, None, None]
    scale = HEAD_DIM ** -0.5
    idx = jnp.arange(RET_BLOCK)
    dist = jnp.abs(idx[:, None] - idx[None, :]).astype(F32)
    visible = (idx[None, :] // CHUNK) <= (idx[:, None] // CHUNK)
    mask = jnp.where(visible[None], jnp.exp(log_g * dist[None]), 0.0) * scale
    ones = jnp.ones((1, 1, HEAD_DIM), F32)
    idx_f = idx.astype(F32)[None, :, None]
    qdec = jnp.exp(log_g * (idx_f + 1.0)) * ones
    kdec = jnp.exp(log_g * (RET_BLOCK - 1.0 - idx_f)) * scale * ones
    bdec = jnp.exp(log_g * float(RET_BLOCK)) * ones
    meta_idx = jnp.arange(N_META, dtype=F32)[None, :, None]
    kdec_meta = jnp.exp(log_g * (N_META - 1.0 - meta_idx)) * scale * ones
    return mask, qdec, kdec, bdec, kdec_meta


def kernel(x, meta_tokens, norm_mix_g, w_in, ssm_lambda_re, ssm_lambda_im, ssm_log_dt, ssm_b_re, ssm_b_im, ssm_c_re, ssm_c_im, ssm_d, w_glu, w_out, norm_ffn_g, w_router_group, b_router_group, w_router_expert, b_router_expert, w_gate, w_up, w_down, norm_final_g):
    bsz, seq, _ = x.shape
    assert seq % TILE == 0 and TILE % RET_BLOCK == 0 and RET_BLOCK % CHUNK == 0
    n_blocks = seq // S5_BLOCK
    levels = int(math.log2(n_blocks))
    assert 1 << levels == n_blocks
    n_tok = bsz * seq
    n_tiles = n_tok // TILE
    n_steps = -(-n_tiles * (TILE_PIECES + N_GROUPS - 1) // TILE_PIECES) + N_GROUPS

    cos, sin = _rope_tables(N_META + seq)
    mask, qdec, kdec, bdec, kdec_meta = _retention_tables()
    g_mix = norm_mix_g[0][None, :]
    w_in_b = w_in[0].astype(BF16)

    u_meta, r0 = _meta_call(meta_tokens, g_mix, w_in_b, cos[:N_META], sin[:N_META], kdec_meta)
    u, y_ret = _mixer_call(x, g_mix, w_in_b, cos[N_META:], sin[N_META:], mask, qdec, kdec, bdec, r0)

    t0, bmat, cmat, asame, across = _s5_operators(
        ssm_lambda_re[0], ssm_lambda_im[0], ssm_log_dt[0], ssm_b_re[0], ssm_b_im[0],
        ssm_c_re[0], ssm_c_im[0], ssm_d[0], levels)
    ut = u.reshape(bsz, n_blocks, S5_BLOCK, SSM_GROUPS, SSM_GROUP).transpose(3, 0, 1, 2, 4)
    ut = ut.reshape(SSM_GROUPS, bsz, n_blocks, S5_LANES)
    um = u_meta.reshape(S5_BLOCK, SSM_GROUPS, SSM_GROUP).transpose(1, 0, 2).reshape(SSM_GROUPS, 1, S5_LANES)
    yt = _s5_call(ut, um, t0, bmat, cmat, asame, across)
    y_ssm = yt.reshape(SSM_GROUPS, bsz, n_blocks, S5_BLOCK, SSM_GROUP).transpose(1, 2, 3, 0, 4)
    y_ssm = y_ssm.reshape(n_tok, SSM_WIDTH)

    w_r = jnp.concatenate(
        [w_router_group[0], w_router_expert[0].transpose(1, 0, 2).reshape(D_MODEL, N_EXPERTS)], axis=1)
    w_r = jnp.pad(w_r, ((0, 0), (0, LANES - w_r.shape[1])))
    b_r = jnp.concatenate([b_router_group[0], b_router_expert[0].reshape(-1)])
    b_r = jnp.pad(b_r, (0, LANES - b_r.shape[0]))[None, :]
    w_r_hi = w_r.astype(BF16)
    w_r_lo = (w_r - w_r_hi.astype(F32)).astype(BF16)
    tri = (jnp.arange(TILE)[:, None] <= jnp.arange(TILE)[None, :]).astype(BF16)

    h, stage, dest, cnt = _proj_call(
        x.reshape(n_tok, D_MODEL), y_ssm, y_ret.reshape(n_tok, RET_WIDTH),
        w_glu[0].astype(BF16), w_out[0].astype(BF16), norm_ffn_g[0][None, :], w_r_hi, w_r_lo, b_r, tri)
    src, g_step, n_live, back = _sort_tables(cnt[:, :N_GROUPS, 0].astype(I32), n_steps)
    y_sorted = _moe_call(src, g_step, n_live, stage, w_gate[0].astype(BF16), w_up[0].astype(BF16),
                         w_down[0].astype(BF16), n_steps)
    out = _final_call(back, y_sorted, h, dest, norm_final_g[None, :])
    return out.reshape(bsz, seq, D_MODEL)
```

```python
import functools
import math

import jax
import jax.numpy as jnp
from jax import lax
from jax.experimental import pallas as pl
from jax.experimental.pallas import tpu as pltpu

D_MODEL = 1024
N_META = 16
CHUNK = 64
EPS = 1e-6
SSM_WIDTH = 256
SSM_GROUP = 16
SSM_GROUPS = 16
SSM_STATE = 64
RET_WIDTH = 768
HEAD_DIM = 128
HEADS = 6
ROPE_BASE = 10000.0
IN_WIDTH = SSM_WIDTH + 4 * RET_WIDTH
N_GROUPS = 4
EXPERTS_PER_GROUP = 4
N_EXPERTS = 16
EXPERT_FF = 256

S5_BLOCK = 16
S5_LANES = S5_BLOCK * SSM_GROUP
RET_BLOCK = 256
TILE = 512
LANES = 128
PIECE = 16
TILE_PIECES = TILE // PIECE
SORTED_PIECES = TILE_PIECES + N_GROUPS
SORTED_ROWS = SORTED_PIECES * PIECE
EXT_WIDTH = D_MODEL + LANES
VMEM_LIMIT = 56 * 1024 * 1024

F32 = jnp.float32
BF16 = jnp.bfloat16
I32 = jnp.int32


def _dot(a, b):
    return jnp.dot(a, b, preferred_element_type=F32)


def _sigmoid(x):
    return 1.0 / (1.0 + jnp.exp(-x))


def _rms_norm(x, g):
    return x * lax.rsqrt(jnp.mean(x * x, axis=-1, keepdims=True) + EPS) * g


def _rope(t, cos, sin_signed):
    return t * cos + pltpu.roll(t, HEAD_DIM // 2, axis=1) * sin_signed


def _meta_kernel(meta_ref, g_ref, w_ref, cos_ref, sin_ref, kdec_ref, u_ref, r0_ref):
    a = _rms_norm(meta_ref[...], g_ref[...]).astype(BF16)
    u_ref[...] = _dot(a, w_ref[:, :SSM_WIDTH])
    k_off = SSM_WIDTH + RET_WIDTH
    v_off = SSM_WIDTH + 2 * RET_WIDTH
    cos = cos_ref[...]
    sin = sin_ref[...]
    for h in range(HEADS):
        k = _dot(a, w_ref[:, k_off + h * HEAD_DIM:k_off + (h + 1) * HEAD_DIM])
        v = _dot(a, w_ref[:, v_off + h * HEAD_DIM:v_off + (h + 1) * HEAD_DIM])
        kd = (_rope(k, cos, sin) * kdec_ref[h]).astype(BF16)
        r0_ref[h] = lax.dot_general(kd, v.astype(BF16), (((0,), (0,)), ((), ())),
                                    preferred_element_type=F32)


def _meta_call(meta, g, w_in, cos, sin, kdec):
    return pl.pallas_call(
        _meta_kernel,
        out_shape=(jax.ShapeDtypeStruct((N_META, SSM_WIDTH), F32),
                   jax.ShapeDtypeStruct((HEADS, HEAD_DIM, HEAD_DIM), F32)),
        compiler_params=pltpu.CompilerParams(vmem_limit_bytes=VMEM_LIMIT),
        name="meta_call",
    )(meta, g, w_in, cos, sin, kdec)


def _mixer_kernel(x_ref, g_ref, w_ref, cos_ref, sin_ref, mask_ref, qdec_ref, kdec_ref,
                  bdec_ref, r0_ref, u_ref, y_ref, r_ref):
    @pl.when(pl.program_id(1) == 0)
    def _():
        r_ref[...] = r0_ref[...]

    a = _rms_norm(x_ref[...], g_ref[...]).astype(BF16)
    u_ref[...] = _dot(a, w_ref[:, :SSM_WIDTH])
    off = SSM_WIDTH
    q_all = _dot(a, w_ref[:, off:off + RET_WIDTH])
    k_all = _dot(a, w_ref[:, off + RET_WIDTH:off + 2 * RET_WIDTH])
    v_all = _dot(a, w_ref[:, off + 2 * RET_WIDTH:off + 3 * RET_WIDTH])
    gate = _dot(a, w_ref[:, off + 3 * RET_WIDTH:off + 4 * RET_WIDTH])
    cos = cos_ref[...]
    sin = sin_ref[...]
    for h in range(HEADS):
        hs = slice(h * HEAD_DIM, (h + 1) * HEAD_DIM)
        q = _rope(q_all[:, hs], cos, sin)
        k = _rope(k_all[:, hs], cos, sin)
        v = v_all[:, hs].astype(BF16)
        outs = []
        for b in range(TILE // RET_BLOCK):
            bs = slice(b * RET_BLOCK, (b + 1) * RET_BLOCK)
            qb = q[bs]
            kb = k[bs]
            vb = v[bs]
            s = lax.dot_general(qb.astype(BF16), kb.astype(BF16), (((1,), (1,)), ((), ())),
                                preferred_element_type=F32)
            p = (s * mask_ref[h]).astype(BF16)
            state = r_ref[h]
            o = _dot(p, vb) + _dot((qb * qdec_ref[h]).astype(BF16), state.astype(BF16))
            kv = lax.dot_general((kb * kdec_ref[h]).astype(BF16), vb, (((0,), (0,)), ((), ())),
                                 preferred_element_type=F32)
            r_ref[h] = state * bdec_ref[h] + kv
            outs.append(o)
        o = jnp.concatenate(outs, axis=0)
        mu = jnp.mean(o, axis=-1, keepdims=True)
        d = o - mu
        var = jnp.mean(d * d, axis=-1, keepdims=True)
        gt = gate[:, hs]
        y_ref[:, hs] = (gt * _sigmoid(gt) * d * lax.rsqrt(var + EPS)).astype(BF16)


def _mixer_call(x, g, w_in, cos, sin, mask, qdec, kdec, bdec, r0):
    bsz, seq, _ = x.shape
    const = lambda *shape: pl.BlockSpec(shape, lambda b, i: (0,) * len(shape))
    return pl.pallas_call(
        _mixer_kernel,
        grid=(bsz, seq // TILE),
        in_specs=[
            pl.BlockSpec((None, TILE, D_MODEL), lambda b, i: (b, i, 0)),
            const(1, D_MODEL),
            const(D_MODEL, IN_WIDTH),
            pl.BlockSpec((TILE, HEAD_DIM), lambda b, i: (i, 0)),
            pl.BlockSpec((TILE, HEAD_DIM), lambda b, i: (i, 0)),
            const(HEADS, RET_BLOCK, RET_BLOCK),
            const(HEADS, RET_BLOCK, HEAD_DIM),
            const(HEADS, RET_BLOCK, HEAD_DIM),
            const(HEADS, 1, HEAD_DIM),
            const(HEADS, HEAD_DIM, HEAD_DIM),
        ],
        out_specs=(
            pl.BlockSpec((None, TILE, SSM_WIDTH), lambda b, i: (b, i, 0)),
            pl.BlockSpec((None, TILE, RET_WIDTH), lambda b, i: (b, i, 0)),
        ),
        out_shape=(jax.ShapeDtypeStruct((bsz, seq, SSM_WIDTH), F32),
                   jax.ShapeDtypeStruct((bsz, seq, RET_WIDTH), BF16)),
        scratch_shapes=[pltpu.VMEM((HEADS, HEAD_DIM, HEAD_DIM), F32)],
        compiler_params=pltpu.CompilerParams(
            dimension_semantics=("arbitrary", "arbitrary"), vmem_limit_bytes=VMEM_LIMIT),
        name="mixer_call",
    )(x, g, w_in, cos, sin, mask, qdec, kdec, bdec, r0)


def _dot_t(a, b):
    return lax.dot_general(a, b, (((1,), (1,)), ((), ())), preferred_element_type=F32)


def _s5_kernel(u_lo_ref, u_hi_ref, um_ref, krow_ref, bmat_ref, cre_ref, cim_ref, ar_ref, ai_ref, wglu_ref,
               y_lo_ref, y_hi_ref, t0_ref, ug_ref, yg_ref):
    n_blocks = u_lo_ref.shape[0] // S5_BLOCK
    levels = ar_ref.shape[0]

    @pl.when(pl.program_id(0) == 0)
    def _():
        lane = lax.broadcasted_iota(I32, (SSM_GROUP, S5_LANES), 1)
        for g in range(SSM_GROUPS):
            k0 = krow_ref[g]
            for a in range(S5_BLOCK):
                blk = k0 if a == 0 else jnp.where(lane >= a * SSM_GROUP,
                                                  pltpu.roll(k0, a * SSM_GROUP, axis=1), 0.0)
                t0_ref[g, a * SSM_GROUP:(a + 1) * SSM_GROUP, :] = blk.astype(BF16)

    per_tile = LANES // SSM_GROUP
    for half, uh_ref in enumerate((u_lo_ref, u_hi_ref)):
        xs = [uh_ref[pl.ds(j, n_blocks, stride=S5_BLOCK), :] for j in range(S5_BLOCK)]
        for k in range(per_tile):
            gs = slice(k * SSM_GROUP, (k + 1) * SSM_GROUP)
            ug_ref[half * per_tile + k] = jnp.concatenate([x[:, gs] for x in xs], axis=1).astype(BF16)

    re, im, re0, im0 = [], [], [], []
    for p in range(SSM_GROUPS // 2):
        g0, g1 = 2 * p, 2 * p + 1
        v = _dot(ug_ref[g0], bmat_ref[g0]) + _dot(ug_ref[g1], bmat_ref[g1])
        v0 = (_dot(um_ref[g0].astype(BF16), bmat_ref[g0]) + _dot(um_ref[g1].astype(BF16), bmat_ref[g1]))[0:1]
        re.append(v[:, :LANES])
        im.append(v[:, LANES:])
        re0.append(v0[:, :LANES])
        im0.append(v0[:, LANES:])
    re, im, re0, im0 = (jnp.concatenate(parts, axis=1) for parts in (re, im, re0, im0))

    row = lax.broadcasted_iota(I32, re.shape, 0)
    ar, ai = ar_ref[0:1], ai_ref[0:1]
    re = re + jnp.where(row == 0, ar * re0 - ai * im0, 0.0)
    im = im + jnp.where(row == 0, ar * im0 + ai * re0, 0.0)
    for lv in range(levels):
        d = 1 << lv
        ar, ai = ar_ref[lv:lv + 1], ai_ref[lv:lv + 1]
        sre = jnp.where(row >= d, pltpu.roll(re, d, axis=0), 0.0)
        sim = jnp.where(row >= d, pltpu.roll(im, d, axis=0), 0.0)
        re, im = re + ar * sre - ai * sim, im + ar * sim + ai * sre
    pre = jnp.where(row == 0, re0, pltpu.roll(re, 1, axis=0)).astype(BF16)
    pim = jnp.where(row == 0, im0, pltpu.roll(im, 1, axis=0)).astype(BF16)

    for g in range(SSM_GROUPS):
        ps = slice((g // 2) * LANES, (g // 2 + 1) * LANES)
        yg_ref[g] = (_dot(ug_ref[g], t0_ref[g]) + _dot_t(pre[:, ps], cre_ref[g])
                     + _dot_t(pim[:, ps], cim_ref[g]))

    wglu = wglu_ref[...]
    for i in range(S5_BLOCK):
        cs = slice(i * SSM_GROUP, (i + 1) * SSM_GROUP)
        y = jnp.concatenate([yg_ref[g, :, cs] for g in range(SSM_GROUPS)], axis=1)
        y = jax.nn.gelu(y, approximate=True)
        y = y * _sigmoid(_dot(y.astype(BF16), wglu))
        y_lo_ref[pl.ds(i, n_blocks, stride=S5_BLOCK), :] = y[:, :LANES]
        y_hi_ref[pl.ds(i, n_blocks, stride=S5_BLOCK), :] = y[:, LANES:]


def _s5_call(u, um, krow, bmat, cre, cim, ar, ai, wglu):
    bsz, seq, _ = u.shape
    n_blocks = seq // S5_BLOCK
    const = lambda a: pl.BlockSpec(a.shape, lambda b: (0,) * a.ndim)
    return pl.pallas_call(
        _s5_kernel,
        grid=(bsz,),
        in_specs=[pl.BlockSpec((None, seq, LANES), lambda b: (b, 0, 0)),
                  pl.BlockSpec((None, seq, LANES), lambda b: (b, 0, 1)),
                  const(um), const(krow), const(bmat), const(cre), const(cim), const(ar), const(ai),
                  const(wglu)],
        out_specs=(pl.BlockSpec((None, seq, LANES), lambda b: (b, 0, 0)),
                   pl.BlockSpec((None, seq, LANES), lambda b: (b, 0, 0))),
        out_shape=(jax.ShapeDtypeStruct((bsz, seq, LANES), F32),
                   jax.ShapeDtypeStruct((bsz, seq, LANES), F32)),
        scratch_shapes=[pltpu.VMEM((SSM_GROUPS, S5_LANES, S5_LANES), BF16),
                        pltpu.VMEM((SSM_GROUPS, n_blocks, S5_LANES), BF16),
                        pltpu.VMEM((SSM_GROUPS, n_blocks, S5_LANES), F32)],
        compiler_params=pltpu.CompilerParams(
            dimension_semantics=("arbitrary",), vmem_limit_bytes=VMEM_LIMIT),
        name="s5_call",
    )(u, u, um, krow, bmat, cre, cim, ar, ai, wglu)


def _s5_operators(lam_re, lam_im, log_dt, b_re, b_im, c_re, c_im, d_skip, levels):
    n_groups = lam_re.shape[0]
    lam = lax.complex(lam_re, lam_im)
    lam_dt = lam * jnp.exp(log_dt)[:, None]
    lam_bar = jnp.exp(lam_dt)
    b_bar = ((lam_bar - 1.0) / lam)[..., None] * lax.complex(b_re, b_im)
    c = lax.complex(c_re, c_im)
    tau = jnp.arange(S5_BLOCK + 1, dtype=F32)
    pows = jnp.exp(lam_dt[None] * tau[:, None, None])
    kern = jnp.real(jnp.einsum('ghp,tgp,gpk->gkth', c, pows[:S5_BLOCK], b_bar))
    skip = (jnp.eye(SSM_GROUP, dtype=F32)[None, :, None, :] * d_skip[:, None, None, :]
            * (tau[:S5_BLOCK] == 0).astype(F32)[None, None, :, None])
    krow = (kern + skip).reshape(n_groups, SSM_GROUP, S5_LANES)
    even = (jnp.arange(n_groups) % 2 == 0)[:, None, None]
    pair_pad = lambda m: jnp.concatenate([jnp.where(even, m, 0.0), jnp.where(even, 0.0, m)], axis=-1)
    bm = (pows[S5_BLOCK - 1 - jnp.arange(S5_BLOCK)].transpose(1, 0, 2)[:, :, None, :]
          * b_bar.transpose(0, 2, 1)[:, None, :, :]).reshape(n_groups, S5_LANES, SSM_STATE)
    bmat = jnp.concatenate([pair_pad(jnp.real(bm)), pair_pad(jnp.imag(bm))], axis=-1)
    cm = (pows[1:].transpose(1, 0, 2)[:, :, None, :] * c[:, None, :, :]).reshape(n_groups, S5_LANES, SSM_STATE)
    cre, cim = pair_pad(jnp.real(cm)), pair_pad(-jnp.imag(cm))
    step = (S5_BLOCK * (2.0 ** jnp.arange(levels, dtype=F32)))
    adec = jnp.exp(lam_dt[None, :, :] * step[:, None, None]).reshape(levels, n_groups * SSM_STATE)
    return krow, bmat.astype(BF16), cre.astype(BF16), cim.astype(BF16), jnp.real(adec), jnp.imag(adec)


def _first_hit(values, target):
    hits, taken = [], None
    for v in values:
        hit = (v >= target) if taken is None else jnp.logical_and(v >= target, jnp.logical_not(taken))
        taken = hit if taken is None else jnp.logical_or(taken, hit)
        hits.append(hit)
    return hits


def _stack_rows(rows, n_rows):
    idx = lax.broadcasted_iota(I32, (n_rows, rows[0].shape[1]), 0)
    out = jnp.zeros((n_rows, rows[0].shape[1]), F32)
    for k, r in enumerate(rows):
        out = jnp.where(idx == k, r, out)
    return out


def _proj_kernel(x_ref, ys_lo_ref, ys_hi_ref, yr_ref, wout_ref, g_ref, wr_hi_ref, wr_lo_ref, br_ref,
                 tri_ref, h_ref, stage_ref, dest_ref, cnt_ref):
    h = (x_ref[...] + _dot(ys_lo_ref[...].astype(BF16), wout_ref[:LANES, :])
         + _dot(ys_hi_ref[...].astype(BF16), wout_ref[LANES:SSM_WIDTH, :])
         + _dot(yr_ref[...], wout_ref[SSM_WIDTH:, :]))
    h_ref[...] = h
    t = _rms_norm(h, g_ref[...])
    t_hi = t.astype(BF16)
    t_lo = (t - t_hi.astype(F32)).astype(BF16)
    wr_hi = wr_hi_ref[...]
    logits = _dot(t_hi, wr_hi) + _dot(t_lo, wr_hi) + _dot(t_hi, wr_lo_ref[...]) + br_ref[...]
    lt = logits.T

    gl = [lt[g:g + 1, :] for g in range(N_GROUPS)]
    gmax = functools.reduce(jnp.maximum, gl)
    g_w = 1.0 / functools.reduce(lambda a, b: a + b, [jnp.exp(l - gmax) for l in gl])
    sel = _first_hit(gl, gmax)
    ev = []
    for e in range(EXPERTS_PER_GROUP):
        acc = jnp.zeros_like(gmax)
        for g in range(N_GROUPS):
            k = N_GROUPS + g * EXPERTS_PER_GROUP + e
            acc = jnp.where(sel[g], lt[k:k + 1, :], acc)
        ev.append(acc)
    m1 = functools.reduce(jnp.maximum, ev)
    first = _first_hit(ev, m1)
    rest = [jnp.where(f, -jnp.inf, v) for f, v in zip(first, ev)]
    m2 = functools.reduce(jnp.maximum, rest)
    second = _first_hit(rest, m2)
    e2 = jnp.exp(m2 - m1)
    w1 = g_w / (1.0 + e2)
    w2 = e2 * w1
    combine = [jnp.where(f, w1, 0.0) + jnp.where(s, w2, 0.0) for f, s in zip(first, second)]

    sel_f = [jnp.where(s, 1.0, 0.0) for s in sel]
    incl = _dot(_stack_rows(sel_f, 8).astype(BF16), tri_ref[...])
    dest = jnp.zeros_like(gmax)
    seg_start = jnp.zeros((1, 1), F32)
    counts = []
    for g in range(N_GROUPS):
        run = incl[g:g + 1, :]
        cnt = run[:, TILE - 1:TILE]
        counts.append(cnt)
        dest = dest + sel_f[g] * (seg_start + run - 1.0)
        seg_start = seg_start + PIECE * jnp.floor((cnt + (PIECE - 1.0)) * (1.0 / PIECE))
    dest_i = dest.astype(I32)
    perm = jnp.where(lax.broadcasted_iota(I32, (SORTED_ROWS, TILE), 0) == dest_i, 1.0, 0.0).astype(BF16)

    c_hi = [c.astype(BF16).astype(F32) for c in combine]
    c_lo = [c - hi for c, hi in zip(combine, c_hi)]
    cw = _stack_rows(c_hi + c_lo, LANES).T
    t_ext = jnp.concatenate([t_hi, cw.astype(BF16)], axis=1)
    stage_ref[...] = _dot(perm, t_ext).astype(BF16)
    dest_ref[...] = _stack_rows([dest], 8)
    cnt_ref[...] = _stack_rows([c + jnp.zeros((1, LANES), F32) for c in counts], 8)


def _proj_call(x, ys_lo, ys_hi, yr, wout, g, wr_hi, wr_lo, br, tri):
    n_tok = x.shape[0]
    n_tiles = n_tok // TILE
    const = lambda *shape: pl.BlockSpec(shape, lambda i: (0,) * len(shape))
    rows = lambda width: pl.BlockSpec((TILE, width), lambda i: (i, 0))
    return pl.pallas_call(
        _proj_kernel,
        grid=(n_tiles,),
        in_specs=[
            rows(D_MODEL), rows(LANES), rows(LANES), rows(RET_WIDTH),
            const(D_MODEL, D_MODEL), const(1, D_MODEL),
            const(D_MODEL, LANES), const(D_MODEL, LANES), const(1, LANES), const(TILE, TILE),
        ],
        out_specs=(rows(D_MODEL),
                   pl.BlockSpec((SORTED_ROWS, EXT_WIDTH), lambda i: (i, 0)),
                   pl.BlockSpec((None, 8, TILE), lambda i: (i, 0, 0)),
                   pl.BlockSpec((None, 8, LANES), lambda i: (i, 0, 0))),
        out_shape=(jax.ShapeDtypeStruct((n_tok, D_MODEL), F32),
                   jax.ShapeDtypeStruct((n_tiles * SORTED_ROWS, EXT_WIDTH), BF16),
                   jax.ShapeDtypeStruct((n_tiles, 8, TILE), F32),
                   jax.ShapeDtypeStruct((n_tiles, 8, LANES), F32)),
        compiler_params=pltpu.CompilerParams(
            dimension_semantics=("arbitrary",), vmem_limit_bytes=VMEM_LIMIT),
        name="proj_call",
    )(x, ys_lo, ys_hi, yr, wout, g, wr_hi, wr_lo, br, tri)


def _sort_tables(cnt, n_steps):
    n_tiles = cnt.shape[0]
    npc = (cnt + PIECE - 1) // PIECE
    seg = jnp.cumsum(npc, axis=1) - npc
    before = jnp.cumsum(npc, axis=0) - npc
    n_tile_g = (jnp.sum(npc, axis=0) + TILE_PIECES - 1) // TILE_PIECES
    t_off = jnp.cumsum(n_tile_g) - n_tile_g
    j = jnp.arange(SORTED_PIECES, dtype=I32)[None, :, None]
    in_g = jnp.logical_and(j >= seg[:, None, :], j < (seg + npc)[:, None, :])
    pos = jnp.sum(jnp.where(in_g, TILE_PIECES * t_off[None, None, :] + before[:, None, :] + j - seg[:, None, :], 0),
                  axis=-1)
    valid = jnp.any(in_g, axis=-1)
    stage_piece = jnp.arange(n_tiles * SORTED_PIECES, dtype=I32).reshape(n_tiles, SORTED_PIECES)
    n_slots = n_steps * TILE_PIECES
    src = jnp.zeros((n_slots,), I32).at[jnp.where(valid, pos, n_slots).reshape(-1)].set(
        stage_piece.reshape(-1), mode='drop')
    steps = jnp.arange(n_steps, dtype=I32)
    g_step = jnp.minimum(jnp.sum(steps[:, None] >= (t_off + n_tile_g)[None, :], axis=1), N_GROUPS - 1)
    n_live = jnp.sum(n_tile_g).reshape(1)
    back = jnp.where(valid, pos, 0).reshape(-1)
    return src, g_step.astype(I32), n_live.astype(I32), back.astype(I32)


def _piece_copy(src_ref, piece, buf_ref, slot, j, sem_ref):
    start = piece * PIECE if isinstance(piece, int) else pl.multiple_of(piece * PIECE, PIECE)
    return pltpu.make_async_copy(src_ref.at[pl.ds(start, PIECE)],
                                 buf_ref.at[slot, pl.ds(j * PIECE, PIECE)], sem_ref.at[slot])


def _gather_pieces(table_ref, src_ref, buf_ref, sem_ref, step, n_steps, n_pieces):
    def fetch(s):
        for j in range(n_pieces):
            _piece_copy(src_ref, table_ref[s * n_pieces + j], buf_ref, s % 2, j, sem_ref).start()

    @pl.when(step == 0)
    def _():
        fetch(step)

    @pl.when(step + 1 < n_steps)
    def _():
        fetch(step + 1)

    for j in range(n_pieces):
        _piece_copy(src_ref, 0, buf_ref, step % 2, j, sem_ref).wait()


def _moe_kernel(src_ref, gstep_ref, nlive_ref, stage_ref, wg_ref, wu_ref, wd_ref, y_ref, buf_ref, sem_ref):
    step = pl.program_id(0)
    _gather_pieces(src_ref, stage_ref, buf_ref, sem_ref, step, pl.num_programs(0), TILE_PIECES)

    @pl.when(step < nlive_ref[0])
    def _():
        rows = buf_ref[step % 2]
        t = rows[:, :D_MODEL]
        cw = rows[:, D_MODEL:].astype(F32)
        acc = jnp.zeros((TILE, D_MODEL), F32)
        for e in range(EXPERTS_PER_GROUP):
            c = cw[:, e:e + 1] + cw[:, EXPERTS_PER_GROUP + e:EXPERTS_PER_GROUP + e + 1]
            hg = _dot(t, wg_ref[e])
            hu = _dot(t, wu_ref[e])
            act = hg * _sigmoid(hg) * hu * c
            acc = acc + _dot(act.astype(BF16), wd_ref[e])
        y_ref[...] = acc.astype(BF16)

    @pl.when(step >= nlive_ref[0])
    def _():
        y_ref[...] = jnp.zeros_like(y_ref)


def _moe_call(src, g_step, n_live, stage, wg, wu, wd, n_steps):
    grp = lambda shape: pl.BlockSpec((EXPERTS_PER_GROUP,) + shape, lambda s, src, gs, nl: (gs[s], 0, 0))
    return pl.pallas_call(
        _moe_kernel,
        grid_spec=pltpu.PrefetchScalarGridSpec(
            num_scalar_prefetch=3,
            grid=(n_steps,),
            in_specs=[pl.BlockSpec(memory_space=pl.ANY),
                      grp((D_MODEL, EXPERT_FF)), grp((D_MODEL, EXPERT_FF)), grp((EXPERT_FF, D_MODEL))],
            out_specs=pl.BlockSpec((TILE, D_MODEL), lambda s, src, gs, nl: (s, 0)),
            scratch_shapes=[pltpu.VMEM((2, TILE, EXT_WIDTH), BF16), pltpu.SemaphoreType.DMA((2,))],
        ),
        out_shape=jax.ShapeDtypeStruct((n_steps * TILE, D_MODEL), BF16),
        compiler_params=pltpu.CompilerParams(
            dimension_semantics=("arbitrary",), vmem_limit_bytes=VMEM_LIMIT),
        name="moe_call",
    )(src, g_step, n_live, stage, wg, wu, wd)


def _final_kernel(back_ref, ysort_ref, h_ref, dest_ref, g_ref, o_ref, buf_ref, sem_ref):
    step = pl.program_id(0)
    _gather_pieces(back_ref, ysort_ref, buf_ref, sem_ref, step, pl.num_programs(0), SORTED_PIECES)
    dest = _stack_rows([dest_ref[0:1, :]], LANES).T[:, 0:1].astype(I32)
    unperm = jnp.where(lax.broadcasted_iota(I32, (TILE, SORTED_ROWS), 1) == dest, 1.0, 0.0).astype(BF16)
    o_ref[...] = _rms_norm(h_ref[...] + _dot(unperm, buf_ref[step % 2]), g_ref[...])


def _final_call(back, ysort, h, dest, g):
    n_tok = h.shape[0]
    return pl.pallas_call(
        _final_kernel,
        grid_spec=pltpu.PrefetchScalarGridSpec(
            num_scalar_prefetch=1,
            grid=(n_tok // TILE,),
            in_specs=[pl.BlockSpec(memory_space=pl.ANY),
                      pl.BlockSpec((TILE, D_MODEL), lambda i, back: (i, 0)),
                      pl.BlockSpec((None, 8, TILE), lambda i, back: (i, 0, 0)),
                      pl.BlockSpec((1, D_MODEL), lambda i, back: (0, 0))],
            out_specs=pl.BlockSpec((TILE, D_MODEL), lambda i, back: (i, 0)),
            scratch_shapes=[pltpu.VMEM((2, SORTED_ROWS, D_MODEL), BF16), pltpu.SemaphoreType.DMA((2,))],
        ),
        out_shape=jax.ShapeDtypeStruct((n_tok, D_MODEL), F32),
        compiler_params=pltpu.CompilerParams(
            dimension_semantics=("arbitrary",), vmem_limit_bytes=VMEM_LIMIT),
        name="final_call",
    )(back, ysort, h, dest, g)


def _rope_tables(length):
    pos = jnp.arange(length, dtype=F32)
    inv_freq = ROPE_BASE ** (-jnp.arange(0, HEAD_DIM, 2, dtype=F32) / HEAD_DIM)
    ang = pos[:, None] * inv_freq[None, :]
    cos, sin = jnp.cos(ang), jnp.sin(ang)
    return jnp.concatenate([cos, cos], axis=-1), jnp.concatenate([-sin, sin], axis=-1)


def _retention_tables():
    gamma = 1.0 - 2.0 ** (-5.0 - jnp.arange(HEADS, dtype=F32))
    log_g = jnp.log(gamma)[:, None, None]
    scale = HEAD_DIM ** -0.5
    idx = jnp.arange(RET_BLOCK)
    dist = jnp.abs(idx[:, None] - idx[None, :]).astype(F32)
    visible = (idx[None, :] // CHUNK) <= (idx[:, None] // CHUNK)
    mask = jnp.where(visible[None], jnp.exp(log_g * dist[None]), 0.0) * scale
    ones = jnp.ones((1, 1, HEAD_DIM), F32)
    idx_f = idx.astype(F32)[None, :, None]
    qdec = jnp.exp(log_g * (idx_f + 1.0)) * ones
    kdec = jnp.exp(log_g * (RET_BLOCK - 1.0 - idx_f)) * scale * ones
    bdec = jnp.exp(log_g * float(RET_BLOCK)) * ones
    meta_idx = jnp.arange(N_META, dtype=F32)[None, :, None]
    kdec_meta = jnp.exp(log_g * (N_META - 1.0 - meta_idx)) * scale * ones
    return mask, qdec, kdec, bdec, kdec_meta


def kernel(x, meta_tokens, norm_mix_g, w_in, ssm_lambda_re, ssm_lambda_im, ssm_log_dt, ssm_b_re, ssm_b_im, ssm_c_re, ssm_c_im, ssm_d, w_glu, w_out, norm_ffn_g, w_router_group, b_router_group, w_router_expert, b_router_expert, w_gate, w_up, w_down, norm_final_g):
    bsz, seq, _ = x.shape
    assert seq % TILE == 0 and TILE % RET_BLOCK == 0 and RET_BLOCK % CHUNK == 0
    n_blocks = seq // S5_BLOCK
    levels = int(math.log2(n_blocks))
    assert 1 << levels == n_blocks
    n_tok = bsz * seq
    n_tiles = n_tok // TILE
    n_steps = -(-n_tiles * (TILE_PIECES + N_GROUPS - 1) // TILE_PIECES) + N_GROUPS

    cos, sin = _rope_tables(N_META + seq)
    mask, qdec, kdec, bdec, kdec_meta = _retention_tables()
    g_mix = norm_mix_g[0][None, :]
    w_in_b = w_in[0].astype(BF16)

    u_meta, r0 = _meta_call(meta_tokens, g_mix, w_in_b, cos[:N_META], sin[:N_META], kdec_meta)
    u, y_ret = _mixer_call(x, g_mix, w_in_b, cos[N_META:], sin[N_META:], mask, qdec, kdec, bdec, r0)

    s5_ops = _s5_operators(
        ssm_lambda_re[0], ssm_lambda_im[0], ssm_log_dt[0], ssm_b_re[0], ssm_b_im[0],
        ssm_c_re[0], ssm_c_im[0], ssm_d[0], levels)
    um = u_meta.reshape(S5_BLOCK, SSM_GROUPS, SSM_GROUP).transpose(1, 0, 2).reshape(SSM_GROUPS, 1, S5_LANES)
    um = jnp.pad(um, ((0, 0), (0, 7), (0, 0)))
    y_lo, y_hi = _s5_call(u, um, *s5_ops, w_glu[0].astype(BF16))

    w_r = jnp.concatenate(
        [w_router_group[0], w_router_expert[0].transpose(1, 0, 2).reshape(D_MODEL, N_EXPERTS)], axis=1)
    w_r = jnp.pad(w_r, ((0, 0), (0, LANES - w_r.shape[1])))
    b_r = jnp.concatenate([b_router_group[0], b_router_expert[0].reshape(-1)])
    b_r = jnp.pad(b_r, (0, LANES - b_r.shape[0]))[None, :]
    w_r_hi = w_r.astype(BF16)
    w_r_lo = (w_r - w_r_hi.astype(F32)).astype(BF16)
    tri = (jnp.arange(TILE)[:, None] <= jnp.arange(TILE)[None, :]).astype(BF16)

    h, stage, dest, cnt = _proj_call(
        x.reshape(n_tok, D_MODEL), y_lo.reshape(n_tok, LANES), y_hi.reshape(n_tok, LANES),
        y_ret.reshape(n_tok, RET_WIDTH),
        w_out[0].astype(BF16), norm_ffn_g[0][None, :], w_r_hi, w_r_lo, b_r, tri)
    src, g_step, n_live, back = _sort_tables(cnt[:, :N_GROUPS, 0].astype(I32), n_steps)
    y_sorted = _moe_call(src, g_step, n_live, stage, w_gate[0].astype(BF16), w_up[0].astype(BF16),
                         w_down[0].astype(BF16), n_steps)
    out = _final_call(back, y_sorted, h, dest, norm_final_g[None, :])
    return out.reshape(bsz, seq, D_MODEL)
```

```python
import functools
import math

import jax
import jax.numpy as jnp
from jax import lax
from jax.experimental import pallas as pl
from jax.experimental.pallas import tpu as pltpu

D_MODEL = 1024
N_META = 16
CHUNK = 64
EPS = 1e-6
SSM_WIDTH = 256
SSM_GROUP = 16
SSM_GROUPS = 16
SSM_STATE = 64
RET_WIDTH = 768
HEAD_DIM = 128
HEADS = 6
ROPE_BASE = 10000.0
IN_WIDTH = SSM_WIDTH + 4 * RET_WIDTH
N_GROUPS = 4
EXPERTS_PER_GROUP = 4
N_EXPERTS = 16
EXPERT_FF = 256

S5_BLOCK = 16
S5_LANES = S5_BLOCK * SSM_GROUP
RET_BLOCK = 256
TILE = 512
LANES = 128
ROUTE_ROWS = 32
PIECE = 16
TILE_PIECES = TILE // PIECE
SORTED_PIECES = TILE_PIECES + N_GROUPS
SORTED_ROWS = SORTED_PIECES * PIECE
EXT_WIDTH = D_MODEL + LANES
VMEM_LIMIT = 56 * 1024 * 1024

F32 = jnp.float32
BF16 = jnp.bfloat16
I32 = jnp.int32


def _dot(a, b):
    return jnp.dot(a, b, preferred_element_type=F32)


def _sigmoid(x):
    return 1.0 / (1.0 + jnp.exp(-x))


def _rms_norm(x, g):
    return x * lax.rsqrt(jnp.mean(x * x, axis=-1, keepdims=True) + EPS) * g


def _rope(t, cos, sin_signed):
    return t * cos + pltpu.roll(t, HEAD_DIM // 2, axis=1) * sin_signed


def _meta_kernel(meta_ref, g_ref, w_ref, cos_ref, sin_ref, kdec_ref, u_ref, r0_ref):
    a = _rms_norm(meta_ref[...], g_ref[...]).astype(BF16)
    u_ref[...] = _dot(a, w_ref[:, :SSM_WIDTH])
    k_off = SSM_WIDTH + RET_WIDTH
    v_off = SSM_WIDTH + 2 * RET_WIDTH
    cos = cos_ref[...]
    sin = sin_ref[...]
    for h in range(HEADS):
        k = _dot(a, w_ref[:, k_off + h * HEAD_DIM:k_off + (h + 1) * HEAD_DIM])
        v = _dot(a, w_ref[:, v_off + h * HEAD_DIM:v_off + (h + 1) * HEAD_DIM])
        kd = (_rope(k, cos, sin) * kdec_ref[h]).astype(BF16)
        r0_ref[h] = lax.dot_general(kd, v.astype(BF16), (((0,), (0,)), ((), ())),
                                    preferred_element_type=F32)


def _meta_call(meta, g, w_in, cos, sin, kdec):
    return pl.pallas_call(
        _meta_kernel,
        out_shape=(jax.ShapeDtypeStruct((N_META, SSM_WIDTH), F32),
                   jax.ShapeDtypeStruct((HEADS, HEAD_DIM, HEAD_DIM), F32)),
        compiler_params=pltpu.CompilerParams(vmem_limit_bytes=VMEM_LIMIT),
        name="meta_call",
    )(meta, g, w_in, cos, sin, kdec)


def _mixer_kernel(x_ref, g_ref, w_ref, cos_ref, sin_ref, mask_ref, qdec_ref, kdec_ref,
                  bdec_ref, r0_ref, u_ref, y_ref, r_ref):
    @pl.when(pl.program_id(1) == 0)
    def _():
        r_ref[...] = r0_ref[...]

    off = SSM_WIDTH
    for b in range(TILE // RET_BLOCK):
        bs = slice(b * RET_BLOCK, (b + 1) * RET_BLOCK)
        a = _rms_norm(x_ref[bs, :], g_ref[...]).astype(BF16)
        u_ref[bs, :] = _dot(a, w_ref[:, :SSM_WIDTH])
        q_all = _dot(a, w_ref[:, off:off + RET_WIDTH])
        k_all = _dot(a, w_ref[:, off + RET_WIDTH:off + 2 * RET_WIDTH])
        v_all = _dot(a, w_ref[:, off + 2 * RET_WIDTH:off + 3 * RET_WIDTH])
        gate = _dot(a, w_ref[:, off + 3 * RET_WIDTH:off + 4 * RET_WIDTH])
        cos = cos_ref[bs, :]
        sin = sin_ref[bs, :]
        for h in range(HEADS):
            hs = slice(h * HEAD_DIM, (h + 1) * HEAD_DIM)
            q = _rope(q_all[:, hs], cos, sin)
            k = _rope(k_all[:, hs], cos, sin)
            v = v_all[:, hs].astype(BF16)
            s = lax.dot_general(q.astype(BF16), k.astype(BF16), (((1,), (1,)), ((), ())),
                                preferred_element_type=F32)
            p = (s * mask_ref[h]).astype(BF16)
            state = r_ref[h]
            o = _dot(p, v) + _dot((q * qdec_ref[h]).astype(BF16), state.astype(BF16))
            kv = lax.dot_general((k * kdec_ref[h]).astype(BF16), v, (((0,), (0,)), ((), ())),
                                 preferred_element_type=F32)
            r_ref[h] = state * bdec_ref[h] + kv
            mu = jnp.mean(o, axis=-1, keepdims=True)
            d = o - mu
            var = jnp.mean(d * d, axis=-1, keepdims=True)
            gt = gate[:, hs]
            y_ref[bs, hs] = (gt * _sigmoid(gt) * d * lax.rsqrt(var + EPS)).astype(BF16)


def _mixer_call(x, g, w_in, cos, sin, mask, qdec, kdec, bdec, r0):
    bsz, seq, _ = x.shape
    const = lambda *shape: pl.BlockSpec(shape, lambda b, i: (0,) * len(shape))
    return pl.pallas_call(
        _mixer_kernel,
        grid=(bsz, seq // TILE),
        in_specs=[
            pl.BlockSpec((None, TILE, D_MODEL), lambda b, i: (b, i, 0)),
            const(1, D_MODEL),
            const(D_MODEL, IN_WIDTH),
            pl.BlockSpec((TILE, HEAD_DIM), lambda b, i: (i, 0)),
            pl.BlockSpec((TILE, HEAD_DIM), lambda b, i: (i, 0)),
            const(HEADS, RET_BLOCK, RET_BLOCK),
            const(HEADS, RET_BLOCK, HEAD_DIM),
            const(HEADS, RET_BLOCK, HEAD_DIM),
            const(HEADS, 1, HEAD_DIM),
            const(HEADS, HEAD_DIM, HEAD_DIM),
        ],
        out_specs=(
            pl.BlockSpec((None, TILE, SSM_WIDTH), lambda b, i: (b, i, 0)),
            pl.BlockSpec((None, TILE, RET_WIDTH), lambda b, i: (b, i, 0)),
        ),
        out_shape=(jax.ShapeDtypeStruct((bsz, seq, SSM_WIDTH), F32),
                   jax.ShapeDtypeStruct((bsz, seq, RET_WIDTH), BF16)),
        scratch_shapes=[pltpu.VMEM((HEADS, HEAD_DIM, HEAD_DIM), F32)],
        compiler_params=pltpu.CompilerParams(
            dimension_semantics=("arbitrary", "arbitrary"), vmem_limit_bytes=VMEM_LIMIT),
        name="mixer_call",
    )(x, g, w_in, cos, sin, mask, qdec, kdec, bdec, r0)


def _dot_t(a, b):
    return lax.dot_general(a, b, (((1,), (1,)), ((), ())), preferred_element_type=F32)


def _s5_kernel(u_lo_ref, u_hi_ref, um_ref, krow_ref, bmat_ref, cre_ref, cim_ref, ar_ref, ai_ref, wglu_ref,
               y_lo_ref, y_hi_ref, t0_ref, ug_ref, yg_ref):
    n_blocks = u_lo_ref.shape[0] // S5_BLOCK
    levels = ar_ref.shape[0]

    @pl.when(pl.program_id(0) == 0)
    def _():
        lane = lax.broadcasted_iota(I32, (SSM_GROUP, S5_LANES), 1)
        for g in range(SSM_GROUPS):
            k0 = krow_ref[g]
            for a in range(S5_BLOCK):
                blk = k0 if a == 0 else jnp.where(lane >= a * SSM_GROUP,
                                                  pltpu.roll(k0, a * SSM_GROUP, axis=1), 0.0)
                t0_ref[g, a * SSM_GROUP:(a + 1) * SSM_GROUP, :] = blk.astype(BF16)

    per_tile = LANES // SSM_GROUP
    for half, uh_ref in enumerate((u_lo_ref, u_hi_ref)):
        xs = [uh_ref[pl.ds(j, n_blocks, stride=S5_BLOCK), :] for j in range(S5_BLOCK)]
        for k in range(per_tile):
            gs = slice(k * SSM_GROUP, (k + 1) * SSM_GROUP)
            ug_ref[half * per_tile + k] = jnp.concatenate([x[:, gs] for x in xs], axis=1).astype(BF16)

    re, im, re0, im0 = [], [], [], []
    for p in range(SSM_GROUPS // 2):
        g0, g1 = 2 * p, 2 * p + 1
        v = _dot(ug_ref[g0], bmat_ref[g0]) + _dot(ug_ref[g1], bmat_ref[g1])
        v0 = (_dot(um_ref[g0].astype(BF16), bmat_ref[g0]) + _dot(um_ref[g1].astype(BF16), bmat_ref[g1]))[0:1]
        re.append(v[:, :LANES])
        im.append(v[:, LANES:])
        re0.append(v0[:, :LANES])
        im0.append(v0[:, LANES:])
    re, im, re0, im0 = (jnp.concatenate(parts, axis=1) for parts in (re, im, re0, im0))

    row = lax.broadcasted_iota(I32, re.shape, 0)
    ar, ai = ar_ref[0:1], ai_ref[0:1]
    re = re + jnp.where(row == 0, ar * re0 - ai * im0, 0.0)
    im = im + jnp.where(row == 0, ar * im0 + ai * re0, 0.0)
    for lv in range(levels):
        d = 1 << lv
        ar, ai = ar_ref[lv:lv + 1], ai_ref[lv:lv + 1]
        sre = jnp.where(row >= d, pltpu.roll(re, d, axis=0), 0.0)
        sim = jnp.where(row >= d, pltpu.roll(im, d, axis=0), 0.0)
        re, im = re + ar * sre - ai * sim, im + ar * sim + ai * sre
    pre = jnp.where(row == 0, re0, pltpu.roll(re, 1, axis=0)).astype(BF16)
    pim = jnp.where(row == 0, im0, pltpu.roll(im, 1, axis=0)).astype(BF16)

    for g in range(SSM_GROUPS):
        ps = slice((g // 2) * LANES, (g // 2 + 1) * LANES)
        yg_ref[g] = (_dot(ug_ref[g], t0_ref[g]) + _dot_t(pre[:, ps], cre_ref[g])
                     + _dot_t(pim[:, ps], cim_ref[g]))

    wglu = wglu_ref[...]
    for i in range(S5_BLOCK):
        cs = slice(i * SSM_GROUP, (i + 1) * SSM_GROUP)
        y = jnp.concatenate([yg_ref[g, :, cs] for g in range(SSM_GROUPS)], axis=1)
        y = jax.nn.gelu(y, approximate=True)
        y = y * _sigmoid(_dot(y.astype(BF16), wglu))
        y_lo_ref[pl.ds(i, n_blocks, stride=S5_BLOCK), :] = y[:, :LANES]
        y_hi_ref[pl.ds(i, n_blocks, stride=S5_BLOCK), :] = y[:, LANES:]


def _s5_call(u, um, krow, bmat, cre, cim, ar, ai, wglu):
    bsz, seq, _ = u.shape
    n_blocks = seq // S5_BLOCK
    const = lambda a: pl.BlockSpec(a.shape, lambda b: (0,) * a.ndim)
    return pl.pallas_call(
        _s5_kernel,
        grid=(bsz,),
        in_specs=[pl.BlockSpec((None, seq, LANES), lambda b: (b, 0, 0)),
                  pl.BlockSpec((None, seq, LANES), lambda b: (b, 0, 1)),
                  const(um), const(krow), const(bmat), const(cre), const(cim), const(ar), const(ai),
                  const(wglu)],
        out_specs=(pl.BlockSpec((None, seq, LANES), lambda b: (b, 0, 0)),
                   pl.BlockSpec((None, seq, LANES), lambda b: (b, 0, 0))),
        out_shape=(jax.ShapeDtypeStruct((bsz, seq, LANES), F32),
                   jax.ShapeDtypeStruct((bsz, seq, LANES), F32)),
        scratch_shapes=[pltpu.VMEM((SSM_GROUPS, S5_LANES, S5_LANES), BF16),
                        pltpu.VMEM((SSM_GROUPS, n_blocks, S5_LANES), BF16),
                        pltpu.VMEM((SSM_GROUPS, n_blocks, S5_LANES), F32)],
        compiler_params=pltpu.CompilerParams(
            dimension_semantics=("arbitrary",), vmem_limit_bytes=VMEM_LIMIT),
        name="s5_call",
    )(u, u, um, krow, bmat, cre, cim, ar, ai, wglu)


def _s5_operators(lam_re, lam_im, log_dt, b_re, b_im, c_re, c_im, d_skip, levels):
    n_groups = lam_re.shape[0]
    lam = lax.complex(lam_re, lam_im)
    lam_dt = lam * jnp.exp(log_dt)[:, None]
    lam_bar = jnp.exp(lam_dt)
    b_bar = ((lam_bar - 1.0) / lam)[..., None] * lax.complex(b_re, b_im)
    c = lax.complex(c_re, c_im)
    tau = jnp.arange(S5_BLOCK + 1, dtype=F32)
    pows = jnp.exp(lam_dt[None] * tau[:, None, None])
    kern = jnp.real(jnp.einsum('ghp,tgp,gpk->gkth', c, pows[:S5_BLOCK], b_bar))
    skip = (jnp.eye(SSM_GROUP, dtype=F32)[None, :, None, :] * d_skip[:, None, None, :]
            * (tau[:S5_BLOCK] == 0).astype(F32)[None, None, :, None])
    krow = (kern + skip).reshape(n_groups, SSM_GROUP, S5_LANES)
    even = (jnp.arange(n_groups) % 2 == 0)[:, None, None]
    pair_pad = lambda m: jnp.concatenate([jnp.where(even, m, 0.0), jnp.where(even, 0.0, m)], axis=-1)
    bm = (pows[S5_BLOCK - 1 - jnp.arange(S5_BLOCK)].transpose(1, 0, 2)[:, :, None, :]
          * b_bar.transpose(0, 2, 1)[:, None, :, :]).reshape(n_groups, S5_LANES, SSM_STATE)
    bmat = jnp.concatenate([pair_pad(jnp.real(bm)), pair_pad(jnp.imag(bm))], axis=-1)
    cm = (pows[1:].transpose(1, 0, 2)[:, :, None, :] * c[:, None, :, :]).reshape(n_groups, S5_LANES, SSM_STATE)
    cre, cim = pair_pad(jnp.real(cm)), pair_pad(-jnp.imag(cm))
    step = (S5_BLOCK * (2.0 ** jnp.arange(levels, dtype=F32)))
    adec = jnp.exp(lam_dt[None, :, :] * step[:, None, None]).reshape(levels, n_groups * SSM_STATE)
    return krow, bmat.astype(BF16), cre.astype(BF16), cim.astype(BF16), jnp.real(adec), jnp.imag(adec)


def _first_hit(values, target):
    hits, taken = [], None
    for v in values:
        hit = (v >= target) if taken is None else jnp.logical_and(v >= target, jnp.logical_not(taken))
        taken = hit if taken is None else jnp.logical_or(taken, hit)
        hits.append(hit)
    return hits


def _stack_rows(rows, n_rows):
    idx = lax.broadcasted_iota(I32, (n_rows, rows[0].shape[1]), 0)
    out = jnp.zeros((n_rows, rows[0].shape[1]), F32)
    for k, r in enumerate(rows):
        out = jnp.where(idx == k, r, out)
    return out


def _proj_kernel(x_ref, ys_lo_ref, ys_hi_ref, yr_ref, wout_ref, g_ref, wr_hi_ref, wr_lo_ref, br_ref,
                 tri_ref, h_ref, stage_ref, dest_ref, cnt_ref):
    ys = jnp.concatenate([ys_lo_ref[...], ys_hi_ref[...]], axis=1).astype(BF16)
    h = x_ref[...] + _dot(ys, wout_ref[:SSM_WIDTH, :]) + _dot(yr_ref[...], wout_ref[SSM_WIDTH:, :])
    h_ref[...] = h.astype(BF16)
    t = _rms_norm(h, g_ref[...])
    t_hi = t.astype(BF16)
    t_lo = (t - t_hi.astype(F32)).astype(BF16)
    wr_hi = wr_hi_ref[...]
    lt = _dot_t(wr_hi, t_hi) + _dot_t(wr_hi, t_lo) + _dot_t(wr_lo_ref[...], t_hi) + br_ref[...]

    gl = [lt[g:g + 1, :] for g in range(N_GROUPS)]
    gmax = functools.reduce(jnp.maximum, gl)
    g_w = 1.0 / functools.reduce(lambda a, b: a + b, [jnp.exp(l - gmax) for l in gl])
    sel = _first_hit(gl, gmax)
    ev = []
    for e in range(EXPERTS_PER_GROUP):
        acc = jnp.zeros_like(gmax)
        for g in range(N_GROUPS):
            k = N_GROUPS + g * EXPERTS_PER_GROUP + e
            acc = jnp.where(sel[g], lt[k:k + 1, :], acc)
        ev.append(acc)
    m1 = functools.reduce(jnp.maximum, ev)
    first = _first_hit(ev, m1)
    rest = [jnp.where(f, -jnp.inf, v) for f, v in zip(first, ev)]
    m2 = functools.reduce(jnp.maximum, rest)
    second = _first_hit(rest, m2)
    e2 = jnp.exp(m2 - m1)
    w1 = g_w / (1.0 + e2)
    w2 = e2 * w1
    combine = [jnp.where(f, w1, 0.0) + jnp.where(s, w2, 0.0) for f, s in zip(first, second)]

    sel_f = [jnp.where(s, 1.0, 0.0) for s in sel]
    incl = _dot(_stack_rows(sel_f, 8).astype(BF16), tri_ref[...])
    dest = jnp.zeros_like(gmax)
    seg_start = jnp.zeros((1, 1), F32)
    counts = []
    for g in range(N_GROUPS):
        run = incl[g:g + 1, :]
        cnt = run[:, TILE - 1:TILE]
        counts.append(cnt)
        dest = dest + sel_f[g] * (seg_start + run - 1.0)
        seg_start = seg_start + PIECE * jnp.floor((cnt + (PIECE - 1.0)) * (1.0 / PIECE))
    dest_i = dest.astype(I32)
    perm = jnp.where(lax.broadcasted_iota(I32, (SORTED_ROWS, TILE), 0) == dest_i, 1.0, 0.0).astype(BF16)

    c_hi = [c.astype(BF16).astype(F32) for c in combine]
    c_lo = [c - hi for c, hi in zip(combine, c_hi)]
    cw = _stack_rows(c_hi + c_lo, LANES).T
    t_ext = jnp.concatenate([t_hi, cw.astype(BF16)], axis=1)
    stage_ref[...] = _dot(perm, t_ext).astype(BF16)
    dest_ref[...] = _stack_rows([dest], 8)
    cnt_ref[...] = _stack_rows([c + jnp.zeros((1, LANES), F32) for c in counts], 8)


def _proj_call(x, ys_lo, ys_hi, yr, wout, g, wr_hi, wr_lo, br, tri):
    n_tok = x.shape[0]
    n_tiles = n_tok // TILE
    const = lambda *shape: pl.BlockSpec(shape, lambda i: (0,) * len(shape))
    rows = lambda width: pl.BlockSpec((TILE, width), lambda i: (i, 0))
    return pl.pallas_call(
        _proj_kernel,
        grid=(n_tiles,),
        in_specs=[
            rows(D_MODEL), rows(LANES), rows(LANES), rows(RET_WIDTH),
            const(D_MODEL, D_MODEL), const(1, D_MODEL),
            const(ROUTE_ROWS, D_MODEL), const(ROUTE_ROWS, D_MODEL), const(ROUTE_ROWS, 1), const(TILE, TILE),
        ],
        out_specs=(rows(D_MODEL),
                   pl.BlockSpec((SORTED_ROWS, EXT_WIDTH), lambda i: (i, 0)),
                   pl.BlockSpec((None, 8, TILE), lambda i: (i, 0, 0)),
                   pl.BlockSpec((None, 8, LANES), lambda i: (i, 0, 0))),
        out_shape=(jax.ShapeDtypeStruct((n_tok, D_MODEL), BF16),
                   jax.ShapeDtypeStruct((n_tiles * SORTED_ROWS, EXT_WIDTH), BF16),
                   jax.ShapeDtypeStruct((n_tiles, 8, TILE), F32),
                   jax.ShapeDtypeStruct((n_tiles, 8, LANES), F32)),
        compiler_params=pltpu.CompilerParams(
            dimension_semantics=("arbitrary",), vmem_limit_bytes=VMEM_LIMIT),
        name="proj_call",
    )(x, ys_lo, ys_hi, yr, wout, g, wr_hi, wr_lo, br, tri)


def _sort_tables(cnt, n_steps):
    n_tiles = cnt.shape[0]
    npc = (cnt + PIECE - 1) // PIECE
    seg = jnp.cumsum(npc, axis=1) - npc
    before = jnp.cumsum(npc, axis=0) - npc
    n_tile_g = (jnp.sum(npc, axis=0) + TILE_PIECES - 1) // TILE_PIECES
    t_off = jnp.cumsum(n_tile_g) - n_tile_g
    j = jnp.arange(SORTED_PIECES, dtype=I32)[None, :, None]
    in_g = jnp.logical_and(j >= seg[:, None, :], j < (seg + npc)[:, None, :])
    pos = jnp.sum(jnp.where(in_g, TILE_PIECES * t_off[None, None, :] + before[:, None, :] + j - seg[:, None, :], 0),
                  axis=-1)
    valid = jnp.any(in_g, axis=-1)
    stage_piece = jnp.arange(n_tiles * SORTED_PIECES, dtype=I32).reshape(n_tiles, SORTED_PIECES)
    n_slots = n_steps * TILE_PIECES
    src = jnp.zeros((n_slots,), I32).at[jnp.where(valid, pos, n_slots).reshape(-1)].set(
        stage_piece.reshape(-1), mode='drop')
    steps = jnp.arange(n_steps, dtype=I32)
    g_step = jnp.minimum(jnp.sum(steps[:, None] >= (t_off + n_tile_g)[None, :], axis=1), N_GROUPS - 1)
    n_live = jnp.sum(n_tile_g).reshape(1)
    back = jnp.where(valid, pos, 0).reshape(-1)
    return src, g_step.astype(I32), n_live.astype(I32), back.astype(I32)


def _piece_copy(src_ref, piece, buf_ref, slot, j, sem_ref):
    start = piece * PIECE if isinstance(piece, int) else pl.multiple_of(piece * PIECE, PIECE)
    return pltpu.make_async_copy(src_ref.at[pl.ds(start, PIECE)],
                                 buf_ref.at[slot, pl.ds(j * PIECE, PIECE)], sem_ref.at[slot])


def _gather_pieces(table_ref, src_ref, buf_ref, sem_ref, step, n_steps, n_pieces):
    def fetch(s):
        for j in range(n_pieces):
            _piece_copy(src_ref, table_ref[s * n_pieces + j], buf_ref, s % 2, j, sem_ref).start()

    @pl.when(step == 0)
    def _():
        fetch(step)

    @pl.when(step + 1 < n_steps)
    def _():
        fetch(step + 1)

    for j in range(n_pieces):
        _piece_copy(src_ref, 0, buf_ref, step % 2, j, sem_ref).wait()


def _moe_kernel(src_ref, gstep_ref, nlive_ref, stage_ref, wg_ref, wu_ref, wd_ref, y_ref,
                buf_ref, sem_ref, wgb_ref, wub_ref, wdb_ref):
    step = pl.program_id(0)
    _gather_pieces(src_ref, stage_ref, buf_ref, sem_ref, step, pl.num_programs(0), TILE_PIECES)

    @pl.when(jnp.logical_or(step == 0, gstep_ref[step] != gstep_ref[jnp.maximum(step - 1, 0)]))
    def _():
        for e in range(EXPERTS_PER_GROUP):
            wgb_ref[e] = wg_ref[e].astype(BF16)
            wub_ref[e] = wu_ref[e].astype(BF16)
            wdb_ref[e] = wd_ref[e].astype(BF16)

    @pl.when(step < nlive_ref[0])
    def _():
        rows = buf_ref[step % 2]
        t = rows[:, :D_MODEL]
        cw = rows[:, D_MODEL:].astype(F32)
        acc = jnp.zeros((TILE, D_MODEL), F32)
        for e in range(EXPERTS_PER_GROUP):
            c = cw[:, e:e + 1] + cw[:, EXPERTS_PER_GROUP + e:EXPERTS_PER_GROUP + e + 1]
            hg = _dot(t, wgb_ref[e])
            hu = _dot(t, wub_ref[e])
            act = hg * _sigmoid(hg) * hu * c
            acc = acc + _dot(act.astype(BF16), wdb_ref[e])
        y_ref[...] = acc.astype(BF16)

    @pl.when(step >= nlive_ref[0])
    def _():
        y_ref[...] = jnp.zeros_like(y_ref)


def _moe_call(src, g_step, n_live, stage, wg, wu, wd, n_steps):
    grp = lambda shape: pl.BlockSpec((EXPERTS_PER_GROUP,) + shape, lambda s, src, gs, nl: (gs[s], 0, 0))
    return pl.pallas_call(
        _moe_kernel,
        grid_spec=pltpu.PrefetchScalarGridSpec(
            num_scalar_prefetch=3,
            grid=(n_steps,),
            in_specs=[pl.BlockSpec(memory_space=pl.ANY),
                      grp((D_MODEL, EXPERT_FF)), grp((D_MODEL, EXPERT_FF)), grp((EXPERT_FF, D_MODEL))],
            out_specs=pl.BlockSpec((TILE, D_MODEL), lambda s, src, gs, nl: (s, 0)),
            scratch_shapes=[pltpu.VMEM((2, TILE, EXT_WIDTH), BF16), pltpu.SemaphoreType.DMA((2,)),
                            pltpu.VMEM((EXPERTS_PER_GROUP, D_MODEL, EXPERT_FF), BF16),
                            pltpu.VMEM((EXPERTS_PER_GROUP, D_MODEL, EXPERT_FF), BF16),
                            pltpu.VMEM((EXPERTS_PER_GROUP, EXPERT_FF, D_MODEL), BF16)],
        ),
        out_shape=jax.ShapeDtypeStruct((n_steps * TILE, D_MODEL), BF16),
        compiler_params=pltpu.CompilerParams(
            dimension_semantics=("arbitrary",), vmem_limit_bytes=VMEM_LIMIT),
        name="moe_call",
    )(src, g_step, n_live, stage, wg, wu, wd)


def _final_kernel(back_ref, ysort_ref, h_ref, dest_ref, g_ref, o_ref, buf_ref, sem_ref):
    step = pl.program_id(0)
    _gather_pieces(back_ref, ysort_ref, buf_ref, sem_ref, step, pl.num_programs(0), SORTED_PIECES)
    dest = _stack_rows([dest_ref[0:1, :]], LANES).T[:, 0:1].astype(I32)
    unperm = jnp.where(lax.broadcasted_iota(I32, (TILE, SORTED_ROWS), 1) == dest, 1.0, 0.0).astype(BF16)
    o_ref[...] = _rms_norm(h_ref[...].astype(F32) + _dot(unperm, buf_ref[step % 2]), g_ref[...])


def _final_call(back, ysort, h, dest, g):
    n_tok = h.shape[0]
    return pl.pallas_call(
        _final_kernel,
        grid_spec=pltpu.PrefetchScalarGridSpec(
            num_scalar_prefetch=1,
            grid=(n_tok // TILE,),
            in_specs=[pl.BlockSpec(memory_space=pl.ANY),
                      pl.BlockSpec((TILE, D_MODEL), lambda i, back: (i, 0)),
                      pl.BlockSpec((None, 8, TILE), lambda i, back: (i, 0, 0)),
                      pl.BlockSpec((1, D_MODEL), lambda i, back: (0, 0))],
            out_specs=pl.BlockSpec((TILE, D_MODEL), lambda i, back: (i, 0)),
            scratch_shapes=[pltpu.VMEM((2, SORTED_ROWS, D_MODEL), BF16), pltpu.SemaphoreType.DMA((2,))],
        ),
        out_shape=jax.ShapeDtypeStruct((n_tok, D_MODEL), F32),
        compiler_params=pltpu.CompilerParams(
            dimension_semantics=("arbitrary",), vmem_limit_bytes=VMEM_LIMIT),
        name="final_call",
    )(back, ysort, h, dest, g)


def _rope_tables(length):
    pos = jnp.arange(length, dtype=F32)
    inv_freq = ROPE_BASE ** (-jnp.arange(0, HEAD_DIM, 2, dtype=F32) / HEAD_DIM)
    ang = pos[:, None] * inv_freq[None, :]
    cos, sin = jnp.cos(ang), jnp.sin(ang)
    return jnp.concatenate([cos, cos], axis=-1), jnp.concatenate([-sin, sin], axis=-1)


def _retention_tables():
    gamma = 1.0 - 2.0 ** (-5.0 - jnp.arange(HEADS, dtype=F32))
    log_g = jnp.log(gamma)[:, None, None]
    scale = HEAD_DIM ** -0.5
    idx = jnp.arange(RET_BLOCK)
    dist = jnp.abs(idx[:, None] - idx[None, :]).astype(F32)
    visible = (idx[None, :] // CHUNK) <= (idx[:, None] // CHUNK)
    mask = jnp.where(visible[None], jnp.exp(log_g * dist[None]), 0.0) * scale
    ones = jnp.ones((1, 1, HEAD_DIM), F32)
    idx_f = idx.astype(F32)[None, :, None]
    qdec = jnp.exp(log_g * (idx_f + 1.0)) * ones
    kdec = jnp.exp(log_g * (RET_BLOCK - 1.0 - idx_f)) * scale * ones
    bdec = jnp.exp(log_g * float(RET_BLOCK)) * ones
    meta_idx = jnp.arange(N_META, dtype=F32)[None, :, None]
    kdec_meta = jnp.exp(log_g * (N_META - 1.0 - meta_idx)) * scale * ones
    return mask, qdec, kdec, bdec, kdec_meta


def kernel(x, meta_tokens, norm_mix_g, w_in, ssm_lambda_re, ssm_lambda_im, ssm_log_dt, ssm_b_re, ssm_b_im, ssm_c_re, ssm_c_im, ssm_d, w_glu, w_out, norm_ffn_g, w_router_group, b_router_group, w_router_expert, b_router_expert, w_gate, w_up, w_down, norm_final_g):
    bsz, seq, _ = x.shape
    assert seq % TILE == 0 and TILE % RET_BLOCK == 0 and RET_BLOCK % CHUNK == 0
    n_blocks = seq // S5_BLOCK
    levels = int(math.log2(n_blocks))
    assert 1 << levels == n_blocks
    n_tok = bsz * seq
    n_tiles = n_tok // TILE
    n_steps = -(-n_tiles * (TILE_PIECES + N_GROUPS - 1) // TILE_PIECES) + N_GROUPS

    cos, sin = _rope_tables(N_META + seq)
    mask, qdec, kdec, bdec, kdec_meta = _retention_tables()
    g_mix = norm_mix_g[0][None, :]
    w_in_b = w_in[0].astype(BF16)

    u_meta, r0 = _meta_call(meta_tokens, g_mix, w_in_b, cos[:N_META], sin[:N_META], kdec_meta)
    u, y_ret = _mixer_call(x, g_mix, w_in_b, cos[N_META:], sin[N_META:], mask, qdec, kdec, bdec, r0)

    s5_ops = _s5_operators(
        ssm_lambda_re[0], ssm_lambda_im[0], ssm_log_dt[0], ssm_b_re[0], ssm_b_im[0],
        ssm_c_re[0], ssm_c_im[0], ssm_d[0], levels)
    um = u_meta.reshape(S5_BLOCK, SSM_GROUPS, SSM_GROUP).transpose(1, 0, 2).reshape(SSM_GROUPS, 1, S5_LANES)
    um = jnp.pad(um, ((0, 0), (0, 7), (0, 0)))
    y_lo, y_hi = _s5_call(u, um, *s5_ops, w_glu[0].astype(BF16))

    w_r = jnp.concatenate(
        [w_router_group[0].T, w_router_expert[0].transpose(0, 2, 1).reshape(N_EXPERTS, D_MODEL)], axis=0)
    w_r = jnp.pad(w_r, ((0, ROUTE_ROWS - w_r.shape[0]), (0, 0)))
    b_r = jnp.concatenate([b_router_group[0], b_router_expert[0].reshape(-1)])
    b_r = jnp.pad(b_r, (0, ROUTE_ROWS - b_r.shape[0]))[:, None]
    w_r_hi = w_r.astype(BF16)
    w_r_lo = (w_r - w_r_hi.astype(F32)).astype(BF16)
    tri = (jnp.arange(TILE)[:, None] <= jnp.arange(TILE)[None, :]).astype(BF16)

    h, stage, dest, cnt = _proj_call(
        x.reshape(n_tok, D_MODEL), y_lo.reshape(n_tok, LANES), y_hi.reshape(n_tok, LANES),
        y_ret.reshape(n_tok, RET_WIDTH),
        w_out[0].astype(BF16), norm_ffn_g[0][None, :], w_r_hi, w_r_lo, b_r, tri)
    src, g_step, n_live, back = _sort_tables(cnt[:, :N_GROUPS, 0].astype(I32), n_steps)
    y_sorted = _moe_call(src, g_step, n_live, stage, w_gate[0], w_up[0], w_down[0], n_steps)
    out = _final_call(back, y_sorted, h, dest, norm_final_g[None, :])
    return out.reshape(bsz, seq, D_MODEL)
```

```python
import functools
import math

import jax
import jax.numpy as jnp
from jax import lax
from jax.experimental import pallas as pl
from jax.experimental.pallas import tpu as pltpu

D_MODEL = 1024
N_META = 16
CHUNK = 64
EPS = 1e-6
SSM_WIDTH = 256
SSM_GROUP = 16
SSM_GROUPS = 16
SSM_STATE = 64
RET_WIDTH = 768
HEAD_DIM = 128
HEADS = 6
ROPE_BASE = 10000.0
IN_WIDTH = SSM_WIDTH + 4 * RET_WIDTH
N_GROUPS = 4
EXPERTS_PER_GROUP = 4
N_EXPERTS = 16
EXPERT_FF = 256

S5_BLOCK = 16
S5_LANES = S5_BLOCK * SSM_GROUP
RET_BLOCK = 256
TILE = 512
LANES = 128
ROUTE_ROWS = 32
PROJ_ROWS = 256
PIECE = 16
TILE_PIECES = TILE // PIECE
SORTED_PIECES = TILE_PIECES + N_GROUPS
SORTED_ROWS = SORTED_PIECES * PIECE
EXT_WIDTH = D_MODEL + LANES
VMEM_LIMIT = 56 * 1024 * 1024

F32 = jnp.float32
BF16 = jnp.bfloat16
I32 = jnp.int32


def _dot(a, b):
    return jnp.dot(a, b, preferred_element_type=F32)


def _sigmoid(x):
    return 1.0 / (1.0 + jnp.exp(-x))


def _rms_norm(x, g):
    return x * lax.rsqrt(jnp.mean(x * x, axis=-1, keepdims=True) + EPS) * g


def _rope(t, cos, sin_signed):
    return t * cos + pltpu.roll(t, HEAD_DIM // 2, axis=1) * sin_signed


def _meta_kernel(meta_ref, g_ref, w_ref, cos_ref, sin_ref, kdec_ref, u_ref, r0_ref):
    a = _rms_norm(meta_ref[...], g_ref[...]).astype(BF16)
    u_ref[...] = _dot(a, w_ref[:, :SSM_WIDTH])
    k_off = SSM_WIDTH + RET_WIDTH
    v_off = SSM_WIDTH + 2 * RET_WIDTH
    cos = cos_ref[...]
    sin = sin_ref[...]
    for h in range(HEADS):
        k = _dot(a, w_ref[:, k_off + h * HEAD_DIM:k_off + (h + 1) * HEAD_DIM])
        v = _dot(a, w_ref[:, v_off + h * HEAD_DIM:v_off + (h + 1) * HEAD_DIM])
        kd = (_rope(k, cos, sin) * kdec_ref[h]).astype(BF16)
        r0_ref[h] = lax.dot_general(kd, v.astype(BF16), (((0,), (0,)), ((), ())),
                                    preferred_element_type=F32)


def _meta_call(meta, g, w_in, cos, sin, kdec):
    return pl.pallas_call(
        _meta_kernel,
        out_shape=(jax.ShapeDtypeStruct((N_META, SSM_WIDTH), F32),
                   jax.ShapeDtypeStruct((HEADS, HEAD_DIM, HEAD_DIM), F32)),
        compiler_params=pltpu.CompilerParams(vmem_limit_bytes=VMEM_LIMIT),
        name="meta_call",
    )(meta, g, w_in, cos, sin, kdec)


def _mixer_kernel(x_ref, g_ref, w_ref, cos_ref, sin_ref, mask_ref, qdec_ref, kdec_ref,
                  bdec_ref, r0_ref, u_ref, y_ref, r_ref):
    @pl.when(pl.program_id(1) == 0)
    def _():
        r_ref[...] = r0_ref[...]

    off = SSM_WIDTH
    for b in range(TILE // RET_BLOCK):
        bs = slice(b * RET_BLOCK, (b + 1) * RET_BLOCK)
        a = _rms_norm(x_ref[bs, :], g_ref[...]).astype(BF16)
        u_ref[bs, :] = _dot(a, w_ref[:, :SSM_WIDTH])
        q_all = _dot(a, w_ref[:, off:off + RET_WIDTH])
        k_all = _dot(a, w_ref[:, off + RET_WIDTH:off + 2 * RET_WIDTH])
        v_all = _dot(a, w_ref[:, off + 2 * RET_WIDTH:off + 3 * RET_WIDTH])
        gate = _dot(a, w_ref[:, off + 3 * RET_WIDTH:off + 4 * RET_WIDTH])
        cos = cos_ref[bs, :]
        sin = sin_ref[bs, :]
        def scores(h):
            hs = slice(h * HEAD_DIM, (h + 1) * HEAD_DIM)
            q = _rope(q_all[:, hs], cos, sin)
            k = _rope(k_all[:, hs], cos, sin)
            return q, k, _dot_t(q.astype(BF16), k.astype(BF16))

        ahead = scores(0)
        for h in range(HEADS):
            hs = slice(h * HEAD_DIM, (h + 1) * HEAD_DIM)
            q, k, s = ahead
            if h + 1 < HEADS:
                ahead = scores(h + 1)
            v = v_all[:, hs].astype(BF16)
            p = (s * mask_ref[h]).astype(BF16)
            state = r_ref[h]
            o = _dot(p, v) + _dot((q * qdec_ref[h]).astype(BF16), state.astype(BF16))
            kv = lax.dot_general((k * kdec_ref[h]).astype(BF16), v, (((0,), (0,)), ((), ())),
                                 preferred_element_type=F32)
            r_ref[h] = state * bdec_ref[h] + kv
            mu = jnp.mean(o, axis=-1, keepdims=True)
            d = o - mu
            var = jnp.mean(d * d, axis=-1, keepdims=True)
            gt = gate[:, hs]
            y_ref[bs, hs] = (gt * _sigmoid(gt) * d * lax.rsqrt(var + EPS)).astype(BF16)


def _mixer_call(x, g, w_in, cos, sin, mask, qdec, kdec, bdec, r0):
    bsz, seq, _ = x.shape
    const = lambda *shape: pl.BlockSpec(shape, lambda b, i: (0,) * len(shape))
    return pl.pallas_call(
        _mixer_kernel,
        grid=(bsz, seq // TILE),
        in_specs=[
            pl.BlockSpec((None, TILE, D_MODEL), lambda b, i: (b, i, 0)),
            const(1, D_MODEL),
            const(D_MODEL, IN_WIDTH),
            pl.BlockSpec((TILE, HEAD_DIM), lambda b, i: (i, 0)),
            pl.BlockSpec((TILE, HEAD_DIM), lambda b, i: (i, 0)),
            const(HEADS, RET_BLOCK, RET_BLOCK),
            const(HEADS, RET_BLOCK, HEAD_DIM),
            const(HEADS, RET_BLOCK, HEAD_DIM),
            const(HEADS, 1, HEAD_DIM),
            const(HEADS, HEAD_DIM, HEAD_DIM),
        ],
        out_specs=(
            pl.BlockSpec((None, TILE, SSM_WIDTH), lambda b, i: (b, i, 0)),
            pl.BlockSpec((None, TILE, RET_WIDTH), lambda b, i: (b, i, 0)),
        ),
        out_shape=(jax.ShapeDtypeStruct((bsz, seq, SSM_WIDTH), F32),
                   jax.ShapeDtypeStruct((bsz, seq, RET_WIDTH), BF16)),
        scratch_shapes=[pltpu.VMEM((HEADS, HEAD_DIM, HEAD_DIM), F32)],
        compiler_params=pltpu.CompilerParams(
            dimension_semantics=("arbitrary", "arbitrary"), vmem_limit_bytes=VMEM_LIMIT),
        name="mixer_call",
    )(x, g, w_in, cos, sin, mask, qdec, kdec, bdec, r0)


def _dot_t(a, b):
    return lax.dot_general(a, b, (((1,), (1,)), ((), ())), preferred_element_type=F32)


def _s5_kernel(u_lo_ref, u_hi_ref, um_ref, krow_ref, bmat_ref, cre_ref, cim_ref, ar_ref, ai_ref, wglu_ref,
               y_lo_ref, y_hi_ref, t0_ref, ug_ref, yg_ref):
    n_blocks = u_lo_ref.shape[0] // S5_BLOCK
    levels = ar_ref.shape[0]

    @pl.when(pl.program_id(0) == 0)
    def _():
        lane = lax.broadcasted_iota(I32, (SSM_GROUP, S5_LANES), 1)
        for g in range(SSM_GROUPS):
            k0 = krow_ref[g]
            for a in range(S5_BLOCK):
                blk = k0 if a == 0 else jnp.where(lane >= a * SSM_GROUP,
                                                  pltpu.roll(k0, a * SSM_GROUP, axis=1), 0.0)
                t0_ref[g, a * SSM_GROUP:(a + 1) * SSM_GROUP, :] = blk.astype(BF16)

    per_tile = LANES // SSM_GROUP
    for half, uh_ref in enumerate((u_lo_ref, u_hi_ref)):
        xs = [uh_ref[pl.ds(j, n_blocks, stride=S5_BLOCK), :] for j in range(S5_BLOCK)]
        for k in range(per_tile):
            gs = slice(k * SSM_GROUP, (k + 1) * SSM_GROUP)
            ug_ref[half * per_tile + k] = jnp.concatenate([x[:, gs] for x in xs], axis=1).astype(BF16)

    re, im, re0, im0 = [], [], [], []
    for p in range(SSM_GROUPS // 2):
        g0, g1 = 2 * p, 2 * p + 1
        v = _dot(ug_ref[g0], bmat_ref[g0]) + _dot(ug_ref[g1], bmat_ref[g1])
        v0 = (_dot(um_ref[g0].astype(BF16), bmat_ref[g0]) + _dot(um_ref[g1].astype(BF16), bmat_ref[g1]))[0:1]
        re.append(v[:, :LANES])
        im.append(v[:, LANES:])
        re0.append(v0[:, :LANES])
        im0.append(v0[:, LANES:])
    re, im, re0, im0 = (jnp.concatenate(parts, axis=1) for parts in (re, im, re0, im0))

    row = lax.broadcasted_iota(I32, re.shape, 0)
    ar, ai = ar_ref[0:1], ai_ref[0:1]
    re = re + jnp.where(row == 0, ar * re0 - ai * im0, 0.0)
    im = im + jnp.where(row == 0, ar * im0 + ai * re0, 0.0)
    for lv in range(levels):
        d = 1 << lv
        ar, ai = ar_ref[lv:lv + 1], ai_ref[lv:lv + 1]
        sre = jnp.where(row >= d, pltpu.roll(re, d, axis=0), 0.0)
        sim = jnp.where(row >= d, pltpu.roll(im, d, axis=0), 0.0)
        re, im = re + ar * sre - ai * sim, im + ar * sim + ai * sre
    pre = jnp.where(row == 0, re0, pltpu.roll(re, 1, axis=0)).astype(BF16)
    pim = jnp.where(row == 0, im0, pltpu.roll(im, 1, axis=0)).astype(BF16)

    for g in range(SSM_GROUPS):
        ps = slice((g // 2) * LANES, (g // 2 + 1) * LANES)
        yg_ref[g] = (_dot(ug_ref[g], t0_ref[g]) + _dot_t(pre[:, ps], cre_ref[g])
                     + _dot_t(pim[:, ps], cim_ref[g]))

    wglu = wglu_ref[...]
    for i in range(S5_BLOCK):
        cs = slice(i * SSM_GROUP, (i + 1) * SSM_GROUP)
        y = jnp.concatenate([yg_ref[g, :, cs] for g in range(SSM_GROUPS)], axis=1)
        y = jax.nn.gelu(y, approximate=True)
        y = y * _sigmoid(_dot(y.astype(BF16), wglu))
        y_lo_ref[pl.ds(i, n_blocks, stride=S5_BLOCK), :] = y[:, :LANES]
        y_hi_ref[pl.ds(i, n_blocks, stride=S5_BLOCK), :] = y[:, LANES:]


def _s5_call(u, um, krow, bmat, cre, cim, ar, ai, wglu):
    bsz, seq, _ = u.shape
    n_blocks = seq // S5_BLOCK
    const = lambda a: pl.BlockSpec(a.shape, lambda b: (0,) * a.ndim)
    return pl.pallas_call(
        _s5_kernel,
        grid=(bsz,),
        in_specs=[pl.BlockSpec((None, seq, LANES), lambda b: (b, 0, 0)),
                  pl.BlockSpec((None, seq, LANES), lambda b: (b, 0, 1)),
                  const(um), const(krow), const(bmat), const(cre), const(cim), const(ar), const(ai),
                  const(wglu)],
        out_specs=(pl.BlockSpec((None, seq, LANES), lambda b: (b, 0, 0)),
                   pl.BlockSpec((None, seq, LANES), lambda b: (b, 0, 0))),
        out_shape=(jax.ShapeDtypeStruct((bsz, seq, LANES), F32),
                   jax.ShapeDtypeStruct((bsz, seq, LANES), F32)),
        scratch_shapes=[pltpu.VMEM((SSM_GROUPS, S5_LANES, S5_LANES), BF16),
                        pltpu.VMEM((SSM_GROUPS, n_blocks, S5_LANES), BF16),
                        pltpu.VMEM((SSM_GROUPS, n_blocks, S5_LANES), F32)],
        compiler_params=pltpu.CompilerParams(
            dimension_semantics=("arbitrary",), vmem_limit_bytes=VMEM_LIMIT),
        name="s5_call",
    )(u, u, um, krow, bmat, cre, cim, ar, ai, wglu)


def _s5_operators(lam_re, lam_im, log_dt, b_re, b_im, c_re, c_im, d_skip, levels):
    n_groups = lam_re.shape[0]
    lam = lax.complex(lam_re, lam_im)
    lam_dt = lam * jnp.exp(log_dt)[:, None]
    lam_bar = jnp.exp(lam_dt)
    b_bar = ((lam_bar - 1.0) / lam)[..., None] * lax.complex(b_re, b_im)
    c = lax.complex(c_re, c_im)
    tau = jnp.arange(S5_BLOCK + 1, dtype=F32)
    pows = jnp.exp(lam_dt[None] * tau[:, None, None])
    kern = jnp.real(jnp.einsum('ghp,tgp,gpk->gkth', c, pows[:S5_BLOCK], b_bar))
    skip = (jnp.eye(SSM_GROUP, dtype=F32)[None, :, None, :] * d_skip[:, None, None, :]
            * (tau[:S5_BLOCK] == 0).astype(F32)[None, None, :, None])
    krow = (kern + skip).reshape(n_groups, SSM_GROUP, S5_LANES)
    even = (jnp.arange(n_groups) % 2 == 0)[:, None, None]
    pair_pad = lambda m: jnp.concatenate([jnp.where(even, m, 0.0), jnp.where(even, 0.0, m)], axis=-1)
    bm = (pows[S5_BLOCK - 1 - jnp.arange(S5_BLOCK)].transpose(1, 0, 2)[:, :, None, :]
          * b_bar.transpose(0, 2, 1)[:, None, :, :]).reshape(n_groups, S5_LANES, SSM_STATE)
    bmat = jnp.concatenate([pair_pad(jnp.real(bm)), pair_pad(jnp.imag(bm))], axis=-1)
    cm = (pows[1:].transpose(1, 0, 2)[:, :, None, :] * c[:, None, :, :]).reshape(n_groups, S5_LANES, SSM_STATE)
    cre, cim = pair_pad(jnp.real(cm)), pair_pad(-jnp.imag(cm))
    step = (S5_BLOCK * (2.0 ** jnp.arange(levels, dtype=F32)))
    adec = jnp.exp(lam_dt[None, :, :] * step[:, None, None]).reshape(levels, n_groups * SSM_STATE)
    return krow, bmat.astype(BF16), cre.astype(BF16), cim.astype(BF16), jnp.real(adec), jnp.imag(adec)


def _first_hit(values, target):
    hits, taken = [], None
    for v in values:
        hit = (v >= target) if taken is None else jnp.logical_and(v >= target, jnp.logical_not(taken))
        taken = hit if taken is None else jnp.logical_or(taken, hit)
        hits.append(hit)
    return hits


def _stack_rows(rows, n_rows):
    idx = lax.broadcasted_iota(I32, (n_rows, rows[0].shape[1]), 0)
    out = jnp.zeros((n_rows, rows[0].shape[1]), F32)
    for k, r in enumerate(rows):
        out = jnp.where(idx == k, r, out)
    return out


def _proj_kernel(x_ref, ys_lo_ref, ys_hi_ref, yr_ref, wout_ref, g_ref, wr_ref, br_ref,
                 tri_ref, h_ref, stage_ref, dest_ref, cnt_ref):
    chunks = [slice(b * PROJ_ROWS, (b + 1) * PROJ_ROWS) for b in range(TILE // PROJ_ROWS)]
    h_parts = []
    for rs in chunks:
        ys = jnp.concatenate([ys_lo_ref[rs, :], ys_hi_ref[rs, :]], axis=1).astype(BF16)
        h = x_ref[rs, :] + _dot(ys, wout_ref[:SSM_WIDTH, :]) + _dot(yr_ref[rs, :], wout_ref[SSM_WIDTH:, :])
        h_ref[rs, :] = h.astype(BF16)
        h_parts.append(h)
    t_parts, lt_parts = [], []
    for h in h_parts:
        t = _rms_norm(h, g_ref[...])
        t_hi = t.astype(BF16)
        t_parts.append((t_hi, (t - t_hi.astype(F32)).astype(BF16)))
    for t_hi, t_lo in t_parts:
        both = _dot_t(wr_ref[...], t_hi)
        lt_parts.append(both[:ROUTE_ROWS] + both[ROUTE_ROWS:] + _dot_t(wr_ref[:ROUTE_ROWS, :], t_lo))
    t_hi = jnp.concatenate([t for t, _ in t_parts], axis=0)
    lt = jnp.concatenate(lt_parts, axis=1) + br_ref[...]

    gl = [lt[g:g + 1, :] for g in range(N_GROUPS)]
    gmax = functools.reduce(jnp.maximum, gl)
    g_w = 1.0 / functools.reduce(lambda a, b: a + b, [jnp.exp(l - gmax) for l in gl])
    sel = _first_hit(gl, gmax)
    ev = []
    for e in range(EXPERTS_PER_GROUP):
        acc = jnp.zeros_like(gmax)
        for g in range(N_GROUPS):
            k = N_GROUPS + g * EXPERTS_PER_GROUP + e
            acc = jnp.where(sel[g], lt[k:k + 1, :], acc)
        ev.append(acc)
    m1 = functools.reduce(jnp.maximum, ev)
    first = _first_hit(ev, m1)
    rest = [jnp.where(f, -jnp.inf, v) for f, v in zip(first, ev)]
    m2 = functools.reduce(jnp.maximum, rest)
    second = _first_hit(rest, m2)
    e2 = jnp.exp(m2 - m1)
    w1 = g_w / (1.0 + e2)
    w2 = e2 * w1
    combine = [jnp.where(f, w1, 0.0) + jnp.where(s, w2, 0.0) for f, s in zip(first, second)]

    sel_f = [jnp.where(s, 1.0, 0.0) for s in sel]
    incl = _dot(_stack_rows(sel_f, 8).astype(BF16), tri_ref[...])
    dest = jnp.zeros_like(gmax)
    seg_start = jnp.zeros((1, 1), F32)
    counts = []
    for g in range(N_GROUPS):
        run = incl[g:g + 1, :]
        cnt = run[:, TILE - 1:TILE]
        counts.append(cnt)
        dest = dest + sel_f[g] * (seg_start + run - 1.0)
        seg_start = seg_start + PIECE * jnp.floor((cnt + (PIECE - 1.0)) * (1.0 / PIECE))
    dest_i = dest.astype(I32)
    perm = jnp.where(lax.broadcasted_iota(I32, (SORTED_ROWS, TILE), 0) == dest_i, 1.0, 0.0).astype(BF16)

    c_hi = [c.astype(BF16).astype(F32) for c in combine]
    c_lo = [c - hi for c, hi in zip(combine, c_hi)]
    cw = _stack_rows(c_hi + c_lo, LANES).T
    t_ext = jnp.concatenate([t_hi, cw.astype(BF16)], axis=1)
    stage_ref[...] = _dot(perm, t_ext).astype(BF16)
    dest_ref[...] = _stack_rows([dest], 8)
    cnt_ref[...] = _stack_rows([c + jnp.zeros((1, LANES), F32) for c in counts], 8)


def _proj_call(x, ys_lo, ys_hi, yr, wout, g, wr, br, tri):
    n_tok = x.shape[0]
    n_tiles = n_tok // TILE
    const = lambda *shape: pl.BlockSpec(shape, lambda i: (0,) * len(shape))
    rows = lambda width: pl.BlockSpec((TILE, width), lambda i: (i, 0))
    return pl.pallas_call(
        _proj_kernel,
        grid=(n_tiles,),
        in_specs=[
            rows(D_MODEL), rows(LANES), rows(LANES), rows(RET_WIDTH),
            const(D_MODEL, D_MODEL), const(1, D_MODEL),
            const(2 * ROUTE_ROWS, D_MODEL), const(ROUTE_ROWS, 1), const(TILE, TILE),
        ],
        out_specs=(rows(D_MODEL),
                   pl.BlockSpec((SORTED_ROWS, EXT_WIDTH), lambda i: (i, 0)),
                   pl.BlockSpec((None, 8, TILE), lambda i: (i, 0, 0)),
                   pl.BlockSpec((None, 8, LANES), lambda i: (i, 0, 0))),
        out_shape=(jax.ShapeDtypeStruct((n_tok, D_MODEL), BF16),
                   jax.ShapeDtypeStruct((n_tiles * SORTED_ROWS, EXT_WIDTH), BF16),
                   jax.ShapeDtypeStruct((n_tiles, 8, TILE), F32),
                   jax.ShapeDtypeStruct((n_tiles, 8, LANES), F32)),
        compiler_params=pltpu.CompilerParams(
            dimension_semantics=("arbitrary",), vmem_limit_bytes=VMEM_LIMIT),
        name="proj_call",
    )(x, ys_lo, ys_hi, yr, wout, g, wr, br, tri)


def _sort_tables(cnt, n_steps):
    n_tiles = cnt.shape[0]
    npc = (cnt + PIECE - 1) // PIECE
    seg = jnp.cumsum(npc, axis=1) - npc
    before = jnp.cumsum(npc, axis=0) - npc
    n_tile_g = (jnp.sum(npc, axis=0) + TILE_PIECES - 1) // TILE_PIECES
    t_off = jnp.cumsum(n_tile_g) - n_tile_g
    j = jnp.arange(SORTED_PIECES, dtype=I32)[None, :, None]
    in_g = jnp.logical_and(j >= seg[:, None, :], j < (seg + npc)[:, None, :])
    pos = jnp.sum(jnp.where(in_g, TILE_PIECES * t_off[None, None, :] + before[:, None, :] + j - seg[:, None, :], 0),
                  axis=-1)
    valid = jnp.any(in_g, axis=-1)
    stage_piece = jnp.arange(n_tiles * SORTED_PIECES, dtype=I32).reshape(n_tiles, SORTED_PIECES)
    n_slots = n_steps * TILE_PIECES
    src = jnp.zeros((n_slots,), I32).at[jnp.where(valid, pos, n_slots).reshape(-1)].set(
        stage_piece.reshape(-1), mode='drop')
    steps = jnp.arange(n_steps, dtype=I32)
    g_step = jnp.minimum(jnp.sum(steps[:, None] >= (t_off + n_tile_g)[None, :], axis=1), N_GROUPS - 1)
    n_live = jnp.sum(n_tile_g).reshape(1)
    back = jnp.where(valid, pos, 0).reshape(-1)
    return src, g_step.astype(I32), n_live.astype(I32), back.astype(I32)


def _piece_copy(src_ref, piece, buf_ref, slot, j, sem_ref):
    start = piece * PIECE if isinstance(piece, int) else pl.multiple_of(piece * PIECE, PIECE)
    return pltpu.make_async_copy(src_ref.at[pl.ds(start, PIECE)],
                                 buf_ref.at[slot, pl.ds(j * PIECE, PIECE)], sem_ref.at[slot])


def _gather_pieces(table_ref, src_ref, buf_ref, sem_ref, step, n_steps, n_pieces):
    def fetch(s):
        for j in range(n_pieces):
            _piece_copy(src_ref, table_ref[s * n_pieces + j], buf_ref, s % 2, j, sem_ref).start()

    @pl.when(step == 0)
    def _():
        fetch(step)

    @pl.when(step + 1 < n_steps)
    def _():
        fetch(step + 1)

    for j in range(n_pieces):
        _piece_copy(src_ref, 0, buf_ref, step % 2, j, sem_ref).wait()


def _moe_kernel(src_ref, gstep_ref, nlive_ref, stage_ref, wg_ref, wu_ref, wd_ref, y_ref,
                buf_ref, sem_ref, wgb_ref, wub_ref, wdb_ref):
    step = pl.program_id(0)
    _gather_pieces(src_ref, stage_ref, buf_ref, sem_ref, step, pl.num_programs(0), TILE_PIECES)

    @pl.when(jnp.logical_or(step == 0, gstep_ref[step] != gstep_ref[jnp.maximum(step - 1, 0)]))
    def _():
        for e in range(EXPERTS_PER_GROUP):
            wgb_ref[e] = wg_ref[e].astype(BF16)
            wub_ref[e] = wu_ref[e].astype(BF16)
            wdb_ref[e] = wd_ref[e].astype(BF16)

    @pl.when(step < nlive_ref[0])
    def _():
        rows = buf_ref[step % 2]
        t = rows[:, :D_MODEL]
        cw = rows[:, D_MODEL:].astype(F32)
        def up(e):
            c = cw[:, e:e + 1] + cw[:, EXPERTS_PER_GROUP + e:EXPERTS_PER_GROUP + e + 1]
            hg = _dot(t, wgb_ref[e])
            hu = _dot(t, wub_ref[e])
            return (hg * _sigmoid(hg) * hu * c).astype(BF16)

        acc = jnp.zeros((TILE, D_MODEL), F32)
        act = up(0)
        for e in range(EXPERTS_PER_GROUP):
            nxt = up(e + 1) if e + 1 < EXPERTS_PER_GROUP else None
            acc = acc + _dot(act, wdb_ref[e])
            act = nxt
        y_ref[...] = acc.astype(BF16)

    @pl.when(step >= nlive_ref[0])
    def _():
        y_ref[...] = jnp.zeros_like(y_ref)


def _moe_call(src, g_step, n_live, stage, wg, wu, wd, n_steps):
    grp = lambda shape: pl.BlockSpec((EXPERTS_PER_GROUP,) + shape, lambda s, src, gs, nl: (gs[s], 0, 0))
    return pl.pallas_call(
        _moe_kernel,
        grid_spec=pltpu.PrefetchScalarGridSpec(
            num_scalar_prefetch=3,
            grid=(n_steps,),
            in_specs=[pl.BlockSpec(memory_space=pl.ANY),
                      grp((D_MODEL, EXPERT_FF)), grp((D_MODEL, EXPERT_FF)), grp((EXPERT_FF, D_MODEL))],
            out_specs=pl.BlockSpec((TILE, D_MODEL), lambda s, src, gs, nl: (s, 0)),
            scratch_shapes=[pltpu.VMEM((2, TILE, EXT_WIDTH), BF16), pltpu.SemaphoreType.DMA((2,)),
                            pltpu.VMEM((EXPERTS_PER_GROUP, D_MODEL, EXPERT_FF), BF16),
                            pltpu.VMEM((EXPERTS_PER_GROUP, D_MODEL, EXPERT_FF), BF16),
                            pltpu.VMEM((EXPERTS_PER_GROUP, EXPERT_FF, D_MODEL), BF16)],
        ),
        out_shape=jax.ShapeDtypeStruct((n_steps * TILE, D_MODEL), BF16),
        compiler_params=pltpu.CompilerParams(
            dimension_semantics=("arbitrary",), vmem_limit_bytes=VMEM_LIMIT),
        name="moe_call",
    )(src, g_step, n_live, stage, wg, wu, wd)


def _final_kernel(back_ref, ysort_ref, h_ref, dest_ref, g_ref, o_ref, buf_ref, sem_ref):
    step = pl.program_id(0)
    _gather_pieces(back_ref, ysort_ref, buf_ref, sem_ref, step, pl.num_programs(0), SORTED_PIECES)
    dest = _stack_rows([dest_ref[0:1, :]], LANES).T[:, 0:1].astype(I32)
    unperm = jnp.where(lax.broadcasted_iota(I32, (TILE, SORTED_ROWS), 1) == dest, 1.0, 0.0).astype(BF16)
    rows = buf_ref[step % 2]
    for b in range(TILE // PROJ_ROWS):
        rs = slice(b * PROJ_ROWS, (b + 1) * PROJ_ROWS)
        o_ref[rs, :] = _rms_norm(h_ref[rs, :].astype(F32) + _dot(unperm[rs], rows), g_ref[...])


def _final_call(back, ysort, h, dest, g):
    n_tok = h.shape[0]
    return pl.pallas_call(
        _final_kernel,
        grid_spec=pltpu.PrefetchScalarGridSpec(
            num_scalar_prefetch=1,
            grid=(n_tok // TILE,),
            in_specs=[pl.BlockSpec(memory_space=pl.ANY),
                      pl.BlockSpec((TILE, D_MODEL), lambda i, back: (i, 0)),
                      pl.BlockSpec((None, 8, TILE), lambda i, back: (i, 0, 0)),
                      pl.BlockSpec((1, D_MODEL), lambda i, back: (0, 0))],
            out_specs=pl.BlockSpec((TILE, D_MODEL), lambda i, back: (i, 0)),
            scratch_shapes=[pltpu.VMEM((2, SORTED_ROWS, D_MODEL), BF16), pltpu.SemaphoreType.DMA((2,))],
        ),
        out_shape=jax.ShapeDtypeStruct((n_tok, D_MODEL), F32),
        compiler_params=pltpu.CompilerParams(
            dimension_semantics=("arbitrary",), vmem_limit_bytes=VMEM_LIMIT),
        name="final_call",
    )(back, ysort, h, dest, g)


def _rope_tables(length):
    pos = jnp.arange(length, dtype=F32)
    inv_freq = ROPE_BASE ** (-jnp.arange(0, HEAD_DIM, 2, dtype=F32) / HEAD_DIM)
    ang = pos[:, None] * inv_freq[None, :]
    cos, sin = jnp.cos(ang), jnp.sin(ang)
    return jnp.concatenate([cos, cos], axis=-1), jnp.concatenate([-sin, sin], axis=-1)


def _retention_tables():
    gamma = 1.0 - 2.0 ** (-5.0 - jnp.arange(HEADS, dtype=F32))
    log_g = jnp.log(gamma)[:, None, None]
    scale = HEAD_DIM ** -0.5
    idx = jnp.arange(RET_BLOCK)
    dist = jnp.abs(idx[:, None] - idx[None, :]).astype(F32)
    visible = (idx[None, :] // CHUNK) <= (idx[:, None] // CHUNK)
    mask = jnp.where(visible[None], jnp.exp(log_g * dist[None]), 0.0) * scale
    ones = jnp.ones((1, 1, HEAD_DIM), F32)
    idx_f = idx.astype(F32)[None, :, None]
    qdec = jnp.exp(log_g * (idx_f + 1.0)) * ones
    kdec = jnp.exp(log_g * (RET_BLOCK - 1.0 - idx_f)) * scale * ones
    bdec = jnp.exp(log_g * float(RET_BLOCK)) * ones
    meta_idx = jnp.arange(N_META, dtype=F32)[None, :, None]
    kdec_meta = jnp.exp(log_g * (N_META - 1.0 - meta_idx)) * scale * ones
    return mask, qdec, kdec, bdec, kdec_meta


def kernel(x, meta_tokens, norm_mix_g, w_in, ssm_lambda_re, ssm_lambda_im, ssm_log_dt, ssm_b_re, ssm_b_im, ssm_c_re, ssm_c_im, ssm_d, w_glu, w_out, norm_ffn_g, w_router_group, b_router_group, w_router_expert, b_router_expert, w_gate, w_up, w_down, norm_final_g):
    bsz, seq, _ = x.shape
    assert seq % TILE == 0 and TILE % RET_BLOCK == 0 and RET_BLOCK % CHUNK == 0
    n_blocks = seq // S5_BLOCK
    levels = int(math.log2(n_blocks))
    assert 1 << levels == n_blocks
    n_tok = bsz * seq
    n_tiles = n_tok // TILE
    n_steps = -(-n_tiles * (TILE_PIECES + N_GROUPS - 1) // TILE_PIECES) + N_GROUPS

    cos, sin = _rope_tables(N_META + seq)
    mask, qdec, kdec, bdec, kdec_meta = _retention_tables()
    g_mix = norm_mix_g[0][None, :]
    w_in_b = w_in[0].astype(BF16)

    u_meta, r0 = _meta_call(meta_tokens, g_mix, w_in_b, cos[:N_META], sin[:N_META], kdec_meta)
    u, y_ret = _mixer_call(x, g_mix, w_in_b, cos[N_META:], sin[N_META:], mask, qdec, kdec, bdec, r0)

    s5_ops = _s5_operators(
        ssm_lambda_re[0], ssm_lambda_im[0], ssm_log_dt[0], ssm_b_re[0], ssm_b_im[0],
        ssm_c_re[0], ssm_c_im[0], ssm_d[0], levels)
    um = u_meta.reshape(S5_BLOCK, SSM_GROUPS, SSM_GROUP).transpose(1, 0, 2).reshape(SSM_GROUPS, 1, S5_LANES)
    um = jnp.pad(um, ((0, 0), (0, 7), (0, 0)))
    y_lo, y_hi = _s5_call(u, um, *s5_ops, w_glu[0].astype(BF16))

    w_r = jnp.concatenate(
        [w_router_group[0].T, w_router_expert[0].transpose(0, 2, 1).reshape(N_EXPERTS, D_MODEL)], axis=0)
    w_r = jnp.pad(w_r, ((0, ROUTE_ROWS - w_r.shape[0]), (0, 0)))
    b_r = jnp.concatenate([b_router_group[0], b_router_expert[0].reshape(-1)])
    b_r = jnp.pad(b_r, (0, ROUTE_ROWS - b_r.shape[0]))[:, None]
    w_r_hi = w_r.astype(BF16)
    w_r = jnp.concatenate([w_r_hi, (w_r - w_r_hi.astype(F32)).astype(BF16)], axis=0)
    tri = (jnp.arange(TILE)[:, None] <= jnp.arange(TILE)[None, :]).astype(BF16)

    h, stage, dest, cnt = _proj_call(
        x.reshape(n_tok, D_MODEL), y_lo.reshape(n_tok, LANES), y_hi.reshape(n_tok, LANES),
        y_ret.reshape(n_tok, RET_WIDTH),
        w_out[0].astype(BF16), norm_ffn_g[0][None, :], w_r, b_r, tri)
    src, g_step, n_live, back = _sort_tables(cnt[:, :N_GROUPS, 0].astype(I32), n_steps)
    y_sorted = _moe_call(src, g_step, n_live, stage, w_gate[0], w_up[0], w_down[0], n_steps)
    out = _final_call(back, y_sorted, h, dest, norm_final_g[None, :])
    return out.reshape(bsz, seq, D_MODEL)
```

```python
import functools
import math

import jax
import jax.numpy as jnp
from jax import lax
from jax.experimental import pallas as pl
from jax.experimental.pallas import tpu as pltpu

D_MODEL = 1024
N_META = 16
CHUNK = 64
EPS = 1e-6
SSM_WIDTH = 256
SSM_GROUP = 16
SSM_GROUPS = 16
SSM_STATE = 64
RET_WIDTH = 768
HEAD_DIM = 128
HEADS = 6
ROPE_BASE = 10000.0
IN_WIDTH = SSM_WIDTH + 4 * RET_WIDTH
N_GROUPS = 4
EXPERTS_PER_GROUP = 4
N_EXPERTS = 16
EXPERT_FF = 256

S5_BLOCK = 16
S5_LANES = S5_BLOCK * SSM_GROUP
RET_BLOCK = 256
TILE = 512
LANES = 128
ROUTE_ROWS = 32
PROJ_ROWS = 256
PIECE = 16
TILE_PIECES = TILE // PIECE
SORTED_PIECES = TILE_PIECES + N_GROUPS
SORTED_ROWS = SORTED_PIECES * PIECE
EXT_WIDTH = D_MODEL + LANES
VMEM_LIMIT = 56 * 1024 * 1024

F32 = jnp.float32
BF16 = jnp.bfloat16
I32 = jnp.int32


def _dot(a, b):
    return jnp.dot(a, b, preferred_element_type=F32)


def _sigmoid(x):
    return 1.0 / (1.0 + jnp.exp(-x))


def _rms_norm(x, g):
    return x * lax.rsqrt(jnp.mean(x * x, axis=-1, keepdims=True) + EPS) * g


def _rope(t, cos, sin_signed):
    return t * cos + pltpu.roll(t, HEAD_DIM // 2, axis=1) * sin_signed


def _meta_kernel(meta_ref, g_ref, w_ref, cos_ref, sin_ref, kdec_ref, u_ref, r0_ref):
    a = _rms_norm(meta_ref[...], g_ref[...]).astype(BF16)
    u_ref[...] = _dot(a, w_ref[:, :SSM_WIDTH])
    k_off = SSM_WIDTH + RET_WIDTH
    v_off = SSM_WIDTH + 2 * RET_WIDTH
    cos = cos_ref[...]
    sin = sin_ref[...]
    for h in range(HEADS):
        k = _dot(a, w_ref[:, k_off + h * HEAD_DIM:k_off + (h + 1) * HEAD_DIM])
        v = _dot(a, w_ref[:, v_off + h * HEAD_DIM:v_off + (h + 1) * HEAD_DIM])
        kd = (_rope(k, cos, sin) * kdec_ref[h]).astype(BF16)
        r0_ref[h] = lax.dot_general(kd, v.astype(BF16), (((0,), (0,)), ((), ())),
                                    preferred_element_type=F32)


def _meta_call(meta, g, w_in, cos, sin, kdec):
    return pl.pallas_call(
        _meta_kernel,
        out_shape=(jax.ShapeDtypeStruct((N_META, SSM_WIDTH), F32),
                   jax.ShapeDtypeStruct((HEADS, HEAD_DIM, HEAD_DIM), F32)),
        compiler_params=pltpu.CompilerParams(vmem_limit_bytes=VMEM_LIMIT),
        name="meta_call",
    )(meta, g, w_in, cos, sin, kdec)


def _mixer_kernel(x_ref, g_ref, w_ref, cos_ref, sin_ref, mask_ref, qdec_ref, kdec_ref,
                  bdec_ref, r0_ref, u_ref, y_ref, r_ref):
    @pl.when(pl.program_id(1) == 0)
    def _():
        r_ref[...] = r0_ref[...]

    off = SSM_WIDTH
    for b in range(TILE // RET_BLOCK):
        bs = slice(b * RET_BLOCK, (b + 1) * RET_BLOCK)
        a = _rms_norm(x_ref[bs, :], g_ref[...]).astype(BF16)
        u_ref[bs, :] = _dot(a, w_ref[:, :SSM_WIDTH])
        q_all = _dot(a, w_ref[:, off:off + RET_WIDTH])
        k_all = _dot(a, w_ref[:, off + RET_WIDTH:off + 2 * RET_WIDTH])
        v_all = _dot(a, w_ref[:, off + 2 * RET_WIDTH:off + 3 * RET_WIDTH])
        gate = _dot(a, w_ref[:, off + 3 * RET_WIDTH:off + 4 * RET_WIDTH])
        cos = cos_ref[bs, :]
        sin = sin_ref[bs, :]
        def scores(h):
            hs = slice(h * HEAD_DIM, (h + 1) * HEAD_DIM)
            q = _rope(q_all[:, hs], cos, sin)
            k = _rope(k_all[:, hs], cos, sin)
            return q, k, _dot_t(q.astype(BF16), k.astype(BF16))

        ahead = scores(0)
        for h in range(HEADS):
            hs = slice(h * HEAD_DIM, (h + 1) * HEAD_DIM)
            q, k, s = ahead
            if h + 1 < HEADS:
                ahead = scores(h + 1)
            v = v_all[:, hs].astype(BF16)
            p = (s * mask_ref[h]).astype(BF16)
            state = r_ref[h]
            o = _dot(p, v) + _dot((q * qdec_ref[h]).astype(BF16), state.astype(BF16))
            kv = lax.dot_general((k * kdec_ref[h]).astype(BF16), v, (((0,), (0,)), ((), ())),
                                 preferred_element_type=F32)
            r_ref[h] = state * bdec_ref[h] + kv
            mu = jnp.mean(o, axis=-1, keepdims=True)
            d = o - mu
            var = jnp.mean(d * d, axis=-1, keepdims=True)
            gt = gate[:, hs]
            y_ref[bs, hs] = (gt * _sigmoid(gt) * d * lax.rsqrt(var + EPS)).astype(BF16)


def _mixer_call(x, g, w_in, cos, sin, mask, qdec, kdec, bdec, r0):
    bsz, seq, _ = x.shape
    const = lambda *shape: pl.BlockSpec(shape, lambda b, i: (0,) * len(shape))
    return pl.pallas_call(
        _mixer_kernel,
        grid=(bsz, seq // TILE),
        in_specs=[
            pl.BlockSpec((None, TILE, D_MODEL), lambda b, i: (b, i, 0)),
            const(1, D_MODEL),
            const(D_MODEL, IN_WIDTH),
            pl.BlockSpec((TILE, HEAD_DIM), lambda b, i: (i, 0)),
            pl.BlockSpec((TILE, HEAD_DIM), lambda b, i: (i, 0)),
            const(HEADS, RET_BLOCK, RET_BLOCK),
            const(HEADS, RET_BLOCK, HEAD_DIM),
            const(HEADS, RET_BLOCK, HEAD_DIM),
            const(HEADS, 1, HEAD_DIM),
            const(HEADS, HEAD_DIM, HEAD_DIM),
        ],
        out_specs=(
            pl.BlockSpec((None, TILE, SSM_WIDTH), lambda b, i: (b, i, 0)),
            pl.BlockSpec((None, TILE, RET_WIDTH), lambda b, i: (b, i, 0)),
        ),
        out_shape=(jax.ShapeDtypeStruct((bsz, seq, SSM_WIDTH), F32),
                   jax.ShapeDtypeStruct((bsz, seq, RET_WIDTH), BF16)),
        scratch_shapes=[pltpu.VMEM((HEADS, HEAD_DIM, HEAD_DIM), F32)],
        compiler_params=pltpu.CompilerParams(
            dimension_semantics=("arbitrary", "arbitrary"), vmem_limit_bytes=VMEM_LIMIT),
        name="mixer_call",
    )(x, g, w_in, cos, sin, mask, qdec, kdec, bdec, r0)


def _dot_t(a, b):
    return lax.dot_general(a, b, (((1,), (1,)), ((), ())), preferred_element_type=F32)


def _s5_kernel(u_lo_ref, u_hi_ref, um_ref, krow_ref, bmat_ref, cre_ref, cim_ref, ar_ref, ai_ref, wglu_ref,
               y_lo_ref, y_hi_ref, t0_ref, ug_ref, yg_ref):
    n_blocks = u_lo_ref.shape[0] // S5_BLOCK
    levels = ar_ref.shape[0]

    @pl.when(pl.program_id(0) == 0)
    def _():
        lane = lax.broadcasted_iota(I32, (SSM_GROUP, S5_LANES), 1)
        for g in range(SSM_GROUPS):
            k0 = krow_ref[g]
            for a in range(S5_BLOCK):
                blk = k0 if a == 0 else jnp.where(lane >= a * SSM_GROUP,
                                                  pltpu.roll(k0, a * SSM_GROUP, axis=1), 0.0)
                t0_ref[g, a * SSM_GROUP:(a + 1) * SSM_GROUP, :] = blk.astype(BF16)

    per_tile = LANES // SSM_GROUP
    for half, uh_ref in enumerate((u_lo_ref, u_hi_ref)):
        xs = [uh_ref[pl.ds(j, n_blocks, stride=S5_BLOCK), :] for j in range(S5_BLOCK)]
        for k in range(per_tile):
            gs = slice(k * SSM_GROUP, (k + 1) * SSM_GROUP)
            ug_ref[half * per_tile + k] = jnp.concatenate([x[:, gs] for x in xs], axis=1).astype(BF16)

    re, im, re0, im0 = [], [], [], []
    for p in range(SSM_GROUPS // 2):
        g0, g1 = 2 * p, 2 * p + 1
        v = _dot(ug_ref[g0], bmat_ref[g0]) + _dot(ug_ref[g1], bmat_ref[g1])
        v0 = (_dot(um_ref[g0].astype(BF16), bmat_ref[g0]) + _dot(um_ref[g1].astype(BF16), bmat_ref[g1]))[0:1]
        re.append(v[:, :LANES])
        im.append(v[:, LANES:])
        re0.append(v0[:, :LANES])
        im0.append(v0[:, LANES:])
    re, im, re0, im0 = (jnp.concatenate(parts, axis=1) for parts in (re, im, re0, im0))

    row = lax.broadcasted_iota(I32, re.shape, 0)
    ar, ai = ar_ref[0:1], ai_ref[0:1]
    re = re + jnp.where(row == 0, ar * re0 - ai * im0, 0.0)
    im = im + jnp.where(row == 0, ar * im0 + ai * re0, 0.0)
    for lv in range(levels):
        d = 1 << lv
        ar, ai = ar_ref[lv:lv + 1], ai_ref[lv:lv + 1]
        sre = jnp.where(row >= d, pltpu.roll(re, d, axis=0), 0.0)
        sim = jnp.where(row >= d, pltpu.roll(im, d, axis=0), 0.0)
        re, im = re + ar * sre - ai * sim, im + ar * sim + ai * sre
    pre = jnp.where(row == 0, re0, pltpu.roll(re, 1, axis=0)).astype(BF16)
    pim = jnp.where(row == 0, im0, pltpu.roll(im, 1, axis=0)).astype(BF16)

    for g in range(SSM_GROUPS):
        ps = slice((g // 2) * LANES, (g // 2 + 1) * LANES)
        yg_ref[g] = (_dot(ug_ref[g], t0_ref[g]) + _dot_t(pre[:, ps], cre_ref[g])
                     + _dot_t(pim[:, ps], cim_ref[g]))

    wglu = wglu_ref[...]
    for i in range(S5_BLOCK):
        cs = slice(i * SSM_GROUP, (i + 1) * SSM_GROUP)
        y = jnp.concatenate([yg_ref[g, :, cs] for g in range(SSM_GROUPS)], axis=1)
        y = jax.nn.gelu(y, approximate=True)
        y = y * _sigmoid(_dot(y.astype(BF16), wglu))
        y_lo_ref[pl.ds(i, n_blocks, stride=S5_BLOCK), :] = y[:, :LANES]
        y_hi_ref[pl.ds(i, n_blocks, stride=S5_BLOCK), :] = y[:, LANES:]


def _s5_call(u, um, krow, bmat, cre, cim, ar, ai, wglu):
    bsz, seq, _ = u.shape
    n_blocks = seq // S5_BLOCK
    const = lambda a: pl.BlockSpec(a.shape, lambda b: (0,) * a.ndim)
    return pl.pallas_call(
        _s5_kernel,
        grid=(bsz,),
        in_specs=[pl.BlockSpec((None, seq, LANES), lambda b: (b, 0, 0)),
                  pl.BlockSpec((None, seq, LANES), lambda b: (b, 0, 1)),
                  const(um), const(krow), const(bmat), const(cre), const(cim), const(ar), const(ai),
                  const(wglu)],
        out_specs=(pl.BlockSpec((None, seq, LANES), lambda b: (b, 0, 0)),
                   pl.BlockSpec((None, seq, LANES), lambda b: (b, 0, 0))),
        out_shape=(jax.ShapeDtypeStruct((bsz, seq, LANES), F32),
                   jax.ShapeDtypeStruct((bsz, seq, LANES), F32)),
        scratch_shapes=[pltpu.VMEM((SSM_GROUPS, S5_LANES, S5_LANES), BF16),
                        pltpu.VMEM((SSM_GROUPS, n_blocks, S5_LANES), BF16),
                        pltpu.VMEM((SSM_GROUPS, n_blocks, S5_LANES), F32)],
        compiler_params=pltpu.CompilerParams(
            dimension_semantics=("arbitrary",), vmem_limit_bytes=VMEM_LIMIT),
        name="s5_call",
    )(u, u, um, krow, bmat, cre, cim, ar, ai, wglu)


def _s5_operators(lam_re, lam_im, log_dt, b_re, b_im, c_re, c_im, d_skip, levels):
    n_groups = lam_re.shape[0]
    lam = lax.complex(lam_re, lam_im)
    lam_dt = lam * jnp.exp(log_dt)[:, None]
    lam_bar = jnp.exp(lam_dt)
    b_bar = ((lam_bar - 1.0) / lam)[..., None] * lax.complex(b_re, b_im)
    c = lax.complex(c_re, c_im)
    tau = jnp.arange(S5_BLOCK + 1, dtype=F32)
    pows = jnp.exp(lam_dt[None] * tau[:, None, None])
    kern = jnp.real(jnp.einsum('ghp,tgp,gpk->gkth', c, pows[:S5_BLOCK], b_bar))
    skip = (jnp.eye(SSM_GROUP, dtype=F32)[None, :, None, :] * d_skip[:, None, None, :]
            * (tau[:S5_BLOCK] == 0).astype(F32)[None, None, :, None])
    krow = (kern + skip).reshape(n_groups, SSM_GROUP, S5_LANES)
    even = (jnp.arange(n_groups) % 2 == 0)[:, None, None]
    pair_pad = lambda m: jnp.concatenate([jnp.where(even, m, 0.0), jnp.where(even, 0.0, m)], axis=-1)
    bm = (pows[S5_BLOCK - 1 - jnp.arange(S5_BLOCK)].transpose(1, 0, 2)[:, :, None, :]
          * b_bar.transpose(0, 2, 1)[:, None, :, :]).reshape(n_groups, S5_LANES, SSM_STATE)
    bmat = jnp.concatenate([pair_pad(jnp.real(bm)), pair_pad(jnp.imag(bm))], axis=-1)
    cm = (pows[1:].transpose(1, 0, 2)[:, :, None, :] * c[:, None, :, :]).reshape(n_groups, S5_LANES, SSM_STATE)
    cre, cim = pair_pad(jnp.real(cm)), pair_pad(-jnp.imag(cm))
    step = (S5_BLOCK * (2.0 ** jnp.arange(levels, dtype=F32)))
    adec = jnp.exp(lam_dt[None, :, :] * step[:, None, None]).reshape(levels, n_groups * SSM_STATE)
    return krow, bmat.astype(BF16), cre.astype(BF16), cim.astype(BF16), jnp.real(adec), jnp.imag(adec)


def _first_hit(values, target):
    hits, taken = [], None
    for v in values:
        hit = (v >= target) if taken is None else jnp.logical_and(v >= target, jnp.logical_not(taken))
        taken = hit if taken is None else jnp.logical_or(taken, hit)
        hits.append(hit)
    return hits


def _stack_rows(rows, n_rows):
    idx = lax.broadcasted_iota(I32, (n_rows, rows[0].shape[1]), 0)
    out = jnp.zeros((n_rows, rows[0].shape[1]), F32)
    for k, r in enumerate(rows):
        out = jnp.where(idx == k, r, out)
    return out


def _project(rs, x_ref, ys_lo_ref, ys_hi_ref, yr_ref, wout_ref, h_ref):
    ys = jnp.concatenate([ys_lo_ref[rs, :], ys_hi_ref[rs, :]], axis=1).astype(BF16)
    h = x_ref[rs, :] + _dot(ys, wout_ref[:SSM_WIDTH, :]) + _dot(yr_ref[rs, :], wout_ref[SSM_WIDTH:, :])
    h_ref[rs, :] = h.astype(BF16)
    return h


def _route(lt, tri_ref):
    gl = [lt[g:g + 1, :] for g in range(N_GROUPS)]
    gmax = functools.reduce(jnp.maximum, gl)
    g_w = 1.0 / functools.reduce(lambda a, b: a + b, [jnp.exp(l - gmax) for l in gl])
    sel = _first_hit(gl, gmax)
    ev = []
    for e in range(EXPERTS_PER_GROUP):
        acc = jnp.zeros_like(gmax)
        for g in range(N_GROUPS):
            k = N_GROUPS + g * EXPERTS_PER_GROUP + e
            acc = jnp.where(sel[g], lt[k:k + 1, :], acc)
        ev.append(acc)
    m1 = functools.reduce(jnp.maximum, ev)
    first = _first_hit(ev, m1)
    rest = [jnp.where(f, -jnp.inf, v) for f, v in zip(first, ev)]
    m2 = functools.reduce(jnp.maximum, rest)
    second = _first_hit(rest, m2)
    e2 = jnp.exp(m2 - m1)
    w1 = g_w / (1.0 + e2)
    w2 = e2 * w1
    combine = [jnp.where(f, w1, 0.0) + jnp.where(s, w2, 0.0) for f, s in zip(first, second)]

    sel_f = [jnp.where(s, 1.0, 0.0) for s in sel]
    incl = _dot(_stack_rows(sel_f, 8).astype(BF16), tri_ref[...])
    dest = jnp.zeros_like(gmax)
    seg_start = jnp.zeros((1, 1), F32)
    counts = []
    for g in range(N_GROUPS):
        run = incl[g:g + 1, :]
        cnt = run[:, TILE - 1:TILE]
        counts.append(cnt)
        dest = dest + sel_f[g] * (seg_start + run - 1.0)
        seg_start = seg_start + PIECE * jnp.floor((cnt + (PIECE - 1.0)) * (1.0 / PIECE))
    return combine, dest, counts


def _proj_kernel(x_ref, ys_lo_ref, ys_hi_ref, yr_ref, wout_ref, g_ref, wr_ref, br_ref,
                 tri_ref, h_ref, stage_ref, dest_ref, cnt_ref, t_scr, lt_scr):
    step = pl.program_id(0)
    cur = step % 2
    prev = 1 - cur

    @pl.when(step == 0)
    def _():
        t_scr[prev] = jnp.zeros(t_scr.shape[1:], BF16)
        lt_scr[prev] = jnp.zeros(lt_scr.shape[1:], F32)

    chunks = [slice(b * PROJ_ROWS, (b + 1) * PROJ_ROWS) for b in range(TILE // PROJ_ROWS)]
    proj_refs = (x_ref, ys_lo_ref, ys_hi_ref, yr_ref, wout_ref, h_ref)
    h_parts = [_project(chunks[0], *proj_refs)]

    combine, dest, counts = _route(lt_scr[prev], tri_ref)
    dest_ref[...] = _stack_rows([dest], 8)
    cnt_ref[...] = _stack_rows([c + jnp.zeros((1, LANES), F32) for c in counts], 8)
    perm = jnp.where(lax.broadcasted_iota(I32, (SORTED_ROWS, TILE), 0) == dest.astype(I32),
                     1.0, 0.0).astype(BF16)
    c_hi = [c.astype(BF16).astype(F32) for c in combine]
    c_lo = [c - hi for c, hi in zip(combine, c_hi)]
    cw = _stack_rows(c_hi + c_lo, LANES).T.astype(BF16)

    h_parts += [_project(rs, *proj_refs) for rs in chunks[1:]]

    t_parts = []
    for rs, h in zip(chunks, h_parts):
        t = _rms_norm(h, g_ref[...])
        t_hi = t.astype(BF16)
        t_parts.append((rs, t_hi, (t - t_hi.astype(F32)).astype(BF16)))

    def logits_hi(rs, t_hi):
        both = _dot_t(wr_ref[...], t_hi)
        lt_scr[cur, :, rs] = both[:ROUTE_ROWS] + both[ROUTE_ROWS:] + br_ref[...]

    def logits_lo(rs, t_lo):
        lt_scr[cur, :, rs] += _dot_t(wr_ref[:ROUTE_ROWS, :], t_lo)

    router = [functools.partial(logits_hi, rs, t_hi) for rs, t_hi, _ in t_parts]
    router += [functools.partial(logits_lo, rs, t_lo) for rs, _, t_lo in t_parts]
    n_col = D_MODEL // PROJ_ROWS
    for c in range(n_col):
        cs = slice(c * PROJ_ROWS, (c + 1) * PROJ_ROWS)
        stage_ref[:, cs] = _dot(perm, t_scr[prev, :, cs]).astype(BF16)
        if c < len(router):
            router[c]()
    for rest in router[n_col:]:
        rest()
    stage_ref[:, D_MODEL:] = _dot(perm, cw).astype(BF16)
    for rs, t_hi, _ in t_parts:
        t_scr[cur, rs, :] = t_hi


def _proj_call(x, ys_lo, ys_hi, yr, wout, g, wr, br, tri):
    n_tok = x.shape[0]
    n_tiles = n_tok // TILE
    const = lambda *shape: pl.BlockSpec(shape, lambda i: (0,) * len(shape))
    front = lambda i: jnp.minimum(i, n_tiles - 1)
    back = lambda i: jnp.maximum(i - 1, 0)
    rows = lambda width: pl.BlockSpec((TILE, width), lambda i: (front(i), 0))
    return pl.pallas_call(
        _proj_kernel,
        grid=(n_tiles + 1,),
        in_specs=[
            rows(D_MODEL), rows(LANES), rows(LANES), rows(RET_WIDTH),
            const(D_MODEL, D_MODEL), const(1, D_MODEL),
            const(2 * ROUTE_ROWS, D_MODEL), const(ROUTE_ROWS, 1), const(TILE, TILE),
        ],
        out_specs=(rows(D_MODEL),
                   pl.BlockSpec((SORTED_ROWS, EXT_WIDTH), lambda i: (back(i), 0)),
                   pl.BlockSpec((None, 8, TILE), lambda i: (back(i), 0, 0)),
                   pl.BlockSpec((None, 8, LANES), lambda i: (back(i), 0, 0))),
        scratch_shapes=[pltpu.VMEM((2, TILE, D_MODEL), BF16), pltpu.VMEM((2, ROUTE_ROWS, TILE), F32)],
        out_shape=(jax.ShapeDtypeStruct((n_tok, D_MODEL), BF16),
                   jax.ShapeDtypeStruct((n_tiles * SORTED_ROWS, EXT_WIDTH), BF16),
                   jax.ShapeDtypeStruct((n_tiles, 8, TILE), F32),
                   jax.ShapeDtypeStruct((n_tiles, 8, LANES), F32)),
        compiler_params=pltpu.CompilerParams(
            dimension_semantics=("arbitrary",), vmem_limit_bytes=VMEM_LIMIT),
        name="proj_call",
    )(x, ys_lo, ys_hi, yr, wout, g, wr, br, tri)


def _sort_tables(cnt, n_steps):
    n_tiles = cnt.shape[0]
    npc = (cnt + PIECE - 1) // PIECE
    seg = jnp.cumsum(npc, axis=1) - npc
    before = jnp.cumsum(npc, axis=0) - npc
    n_tile_g = (jnp.sum(npc, axis=0) + TILE_PIECES - 1) // TILE_PIECES
    t_off = jnp.cumsum(n_tile_g) - n_tile_g
    j = jnp.arange(SORTED_PIECES, dtype=I32)[None, :, None]
    in_g = jnp.logical_and(j >= seg[:, None, :], j < (seg + npc)[:, None, :])
    pos = jnp.sum(jnp.where(in_g, TILE_PIECES * t_off[None, None, :] + before[:, None, :] + j - seg[:, None, :], 0),
                  axis=-1)
    valid = jnp.any(in_g, axis=-1)
    stage_piece = jnp.arange(n_tiles * SORTED_PIECES, dtype=I32).reshape(n_tiles, SORTED_PIECES)
    n_slots = n_steps * TILE_PIECES
    src = jnp.zeros((n_slots,), I32).at[jnp.where(valid, pos, n_slots).reshape(-1)].set(
        stage_piece.reshape(-1), mode='drop')
    steps = jnp.arange(n_steps, dtype=I32)
    g_step = jnp.minimum(jnp.sum(steps[:, None] >= (t_off + n_tile_g)[None, :], axis=1), N_GROUPS - 1)
    n_live = jnp.sum(n_tile_g).reshape(1)
    back = jnp.where(valid, pos, 0).reshape(-1)
    return src, g_step.astype(I32), n_live.astype(I32), back.astype(I32)


def _piece_copy(src_ref, piece, buf_ref, slot, j, sem_ref):
    start = piece * PIECE if isinstance(piece, int) else pl.multiple_of(piece * PIECE, PIECE)
    return pltpu.make_async_copy(src_ref.at[pl.ds(start, PIECE)],
                                 buf_ref.at[slot, pl.ds(j * PIECE, PIECE)], sem_ref.at[slot])


def _fetch_pieces(table_ref, src_ref, buf_ref, sem_ref, row, slot, n_pieces):
    for j in range(n_pieces):
        _piece_copy(src_ref, table_ref[row * n_pieces + j], buf_ref, slot, j, sem_ref).start()


def _wait_pieces(src_ref, buf_ref, sem_ref, slot, n_pieces):
    for j in range(n_pieces):
        _piece_copy(src_ref, 0, buf_ref, slot, j, sem_ref).wait()


def _gather_pieces(table_ref, src_ref, buf_ref, sem_ref, n_pieces):
    step = pl.program_id(0)
    last = pl.num_programs(0) - 1
    args = (src_ref, buf_ref, sem_ref)

    @pl.when(step == 0)
    def _():
        _fetch_pieces(table_ref, *args, step, 0, n_pieces)

    _wait_pieces(*args, step % 2, n_pieces)

    def fetch_next():
        _fetch_pieces(table_ref, *args, jnp.minimum(step + 1, last), (step + 1) % 2, n_pieces)

    def drain():
        @pl.when(step == last)
        def _():
            _wait_pieces(*args, (step + 1) % 2, n_pieces)

    return fetch_next, drain


def _moe_kernel(src_ref, gstep_ref, nlive_ref, stage_ref, wg_ref, wu_ref, wd_ref, y_ref,
                buf_ref, sem_ref, wgb_ref, wub_ref, wdb_ref):
    step = pl.program_id(0)
    fetch_next, drain = _gather_pieces(src_ref, stage_ref, buf_ref, sem_ref, TILE_PIECES)

    @pl.when(jnp.logical_or(step == 0, gstep_ref[step] != gstep_ref[jnp.maximum(step - 1, 0)]))
    def _():
        for e in range(EXPERTS_PER_GROUP):
            wgb_ref[e] = wg_ref[e].astype(BF16)
            wub_ref[e] = wu_ref[e].astype(BF16)
            wdb_ref[e * EXPERT_FF:(e + 1) * EXPERT_FF, :] = wd_ref[e].astype(BF16)

    @pl.when(step < nlive_ref[0])
    def _():
        slot = step % 2
        fetch_next()

        def up(rs):
            t = buf_ref[slot, rs, :D_MODEL]
            return [(_dot(t, wgb_ref[e]), _dot(t, wub_ref[e])) for e in range(EXPERTS_PER_GROUP)]

        def down(rs, hidden):
            cw = buf_ref[slot, rs, D_MODEL:].astype(F32)
            acts = []
            for e, (hg, hu) in enumerate(hidden):
                c = cw[:, e:e + 1] + cw[:, EXPERTS_PER_GROUP + e:EXPERTS_PER_GROUP + e + 1]
                acts.append((hg * _sigmoid(hg) * hu * c).astype(BF16))
            y_ref[rs, :] = _dot(jnp.concatenate(acts, axis=1), wdb_ref[...]).astype(BF16)

        halves = [slice(b * PROJ_ROWS, (b + 1) * PROJ_ROWS) for b in range(TILE // PROJ_ROWS)]
        hidden = [up(rs) for rs in halves]
        for rs, hid in zip(halves, hidden):
            down(rs, hid)

    @pl.when(step >= nlive_ref[0])
    def _():
        fetch_next()
        y_ref[...] = jnp.zeros_like(y_ref)

    drain()


def _moe_call(src, g_step, n_live, stage, wg, wu, wd, n_steps):
    grp = lambda shape: pl.BlockSpec((EXPERTS_PER_GROUP,) + shape, lambda s, src, gs, nl: (gs[s], 0, 0))
    return pl.pallas_call(
        _moe_kernel,
        grid_spec=pltpu.PrefetchScalarGridSpec(
            num_scalar_prefetch=3,
            grid=(n_steps,),
            in_specs=[pl.BlockSpec(memory_space=pl.ANY),
                      grp((D_MODEL, EXPERT_FF)), grp((D_MODEL, EXPERT_FF)), grp((EXPERT_FF, D_MODEL))],
            out_specs=pl.BlockSpec((TILE, D_MODEL), lambda s, src, gs, nl: (s, 0)),
            scratch_shapes=[pltpu.VMEM((2, TILE, EXT_WIDTH), BF16), pltpu.SemaphoreType.DMA((2,)),
                            pltpu.VMEM((EXPERTS_PER_GROUP, D_MODEL, EXPERT_FF), BF16),
                            pltpu.VMEM((EXPERTS_PER_GROUP, D_MODEL, EXPERT_FF), BF16),
                            pltpu.VMEM((EXPERTS_PER_GROUP * EXPERT_FF, D_MODEL), BF16)],
        ),
        out_shape=jax.ShapeDtypeStruct((n_steps * TILE, D_MODEL), BF16),
        compiler_params=pltpu.CompilerParams(
            dimension_semantics=("arbitrary",), vmem_limit_bytes=VMEM_LIMIT),
        name="moe_call",
    )(src, g_step, n_live, stage, wg, wu, wd)


def _final_kernel(back_ref, ysort_ref, h_ref, dest_ref, g_ref, o_ref, buf_ref, sem_ref):
    step = pl.program_id(0)
    fetch_next, drain = _gather_pieces(back_ref, ysort_ref, buf_ref, sem_ref, SORTED_PIECES)
    fetch_next()
    dest = _stack_rows([dest_ref[0:1, :]], LANES).T[:, 0:1].astype(I32)
    unperm = jnp.where(lax.broadcasted_iota(I32, (TILE, SORTED_ROWS), 1) == dest, 1.0, 0.0).astype(BF16)
    rows = buf_ref[step % 2]
    for b in range(TILE // PROJ_ROWS):
        rs = slice(b * PROJ_ROWS, (b + 1) * PROJ_ROWS)
        o_ref[rs, :] = _rms_norm(h_ref[rs, :].astype(F32) + _dot(unperm[rs], rows), g_ref[...])
    drain()


def _final_call(back, ysort, h, dest, g):
    n_tok = h.shape[0]
    return pl.pallas_call(
        _final_kernel,
        grid_spec=pltpu.PrefetchScalarGridSpec(
            num_scalar_prefetch=1,
            grid=(n_tok // TILE,),
            in_specs=[pl.BlockSpec(memory_space=pl.ANY),
                      pl.BlockSpec((TILE, D_MODEL), lambda i, back: (i, 0)),
                      pl.BlockSpec((None, 8, TILE), lambda i, back: (i, 0, 0)),
                      pl.BlockSpec((1, D_MODEL), lambda i, back: (0, 0))],
            out_specs=pl.BlockSpec((TILE, D_MODEL), lambda i, back: (i, 0)),
            scratch_shapes=[pltpu.VMEM((2, SORTED_ROWS, D_MODEL), BF16), pltpu.SemaphoreType.DMA((2,))],
        ),
        out_shape=jax.ShapeDtypeStruct((n_tok, D_MODEL), F32),
        compiler_params=pltpu.CompilerParams(
            dimension_semantics=("arbitrary",), vmem_limit_bytes=VMEM_LIMIT),
        name="final_call",
    )(back, ysort, h, dest, g)


def _rope_tables(length):
    pos = jnp.arange(length, dtype=F32)
    inv_freq = ROPE_BASE ** (-jnp.arange(0, HEAD_DIM, 2, dtype=F32) / HEAD_DIM)
    ang = pos[:, None] * inv_freq[None, :]
    cos, sin = jnp.cos(ang), jnp.sin(ang)
    return jnp.concatenate([cos, cos], axis=-1), jnp.concatenate([-sin, sin], axis=-1)


def _retention_tables():
    gamma = 1.0 - 2.0 ** (-5.0 - jnp.arange(HEADS, dtype=F32))
    log_g = jnp.log(gamma)[:, None, None]
    scale = HEAD_DIM ** -0.5
    idx = jnp.arange(RET_BLOCK)
    dist = jnp.abs(idx[:, None] - idx[None, :]).astype(F32)
    visible = (idx[None, :] // CHUNK) <= (idx[:, None] // CHUNK)
    mask = jnp.where(visible[None], jnp.exp(log_g * dist[None]), 0.0) * scale
    ones = jnp.ones((1, 1, HEAD_DIM), F32)
    idx_f = idx.astype(F32)[None, :, None]
    qdec = jnp.exp(log_g * (idx_f + 1.0)) * ones
    kdec = jnp.exp(log_g * (RET_BLOCK - 1.0 - idx_f)) * scale * ones
    bdec = jnp.exp(log_g * float(RET_BLOCK)) * ones
    meta_idx = jnp.arange(N_META, dtype=F32)[None, :, None]
    kdec_meta = jnp.exp(log_g * (N_META - 1.0 - meta_idx)) * scale * ones
    return mask, qdec, kdec, bdec, kdec_meta


def kernel(x, meta_tokens, norm_mix_g, w_in, ssm_lambda_re, ssm_lambda_im, ssm_log_dt, ssm_b_re, ssm_b_im, ssm_c_re, ssm_c_im, ssm_d, w_glu, w_out, norm_ffn_g, w_router_group, b_router_group, w_router_expert, b_router_expert, w_gate, w_up, w_down, norm_final_g):
    bsz, seq, _ = x.shape
    assert seq % TILE == 0 and TILE % RET_BLOCK == 0 and RET_BLOCK % CHUNK == 0
    n_blocks = seq // S5_BLOCK
    levels = int(math.log2(n_blocks))
    assert 1 << levels == n_blocks
    n_tok = bsz * seq
    n_tiles = n_tok // TILE
    n_steps = -(-n_tiles * (TILE_PIECES + N_GROUPS - 1) // TILE_PIECES) + N_GROUPS

    cos, sin = _rope_tables(N_META + seq)
    mask, qdec, kdec, bdec, kdec_meta = _retention_tables()
    g_mix = norm_mix_g[0][None, :]
    w_in_b = w_in[0].astype(BF16)

    u_meta, r0 = _meta_call(meta_tokens, g_mix, w_in_b, cos[:N_META], sin[:N_META], kdec_meta)
    u, y_ret = _mixer_call(x, g_mix, w_in_b, cos[N_META:], sin[N_META:], mask, qdec, kdec, bdec, r0)

    s5_ops = _s5_operators(
        ssm_lambda_re[0], ssm_lambda_im[0], ssm_log_dt[0], ssm_b_re[0], ssm_b_im[0],
        ssm_c_re[0], ssm_c_im[0], ssm_d[0], levels)
    um = u_meta.reshape(S5_BLOCK, SSM_GROUPS, SSM_GROUP).transpose(1, 0, 2).reshape(SSM_GROUPS, 1, S5_LANES)
    um = jnp.pad(um, ((0, 0), (0, 7), (0, 0)))
    y_lo, y_hi = _s5_call(u, um, *s5_ops, w_glu[0].astype(BF16))

    w_r = jnp.concatenate(
        [w_router_group[0].T, w_router_expert[0].transpose(0, 2, 1).reshape(N_EXPERTS, D_MODEL)], axis=0)
    w_r = jnp.pad(w_r, ((0, ROUTE_ROWS - w_r.shape[0]), (0, 0)))
    b_r = jnp.concatenate([b_router_group[0], b_router_expert[0].reshape(-1)])
    b_r = jnp.pad(b_r, (0, ROUTE_ROWS - b_r.shape[0]))[:, None]
    w_r_hi = w_r.astype(BF16)
    w_r = jnp.concatenate([w_r_hi, (w_r - w_r_hi.astype(F32)).astype(BF16)], axis=0)
    tri = (jnp.arange(TILE)[:, None] <= jnp.arange(TILE)[None, :]).astype(BF16)

    h, stage, dest, cnt = _proj_call(
        x.reshape(n_tok, D_MODEL), y_lo.reshape(n_tok, LANES), y_hi.reshape(n_tok, LANES),
        y_ret.reshape(n_tok, RET_WIDTH),
        w_out[0].astype(BF16), norm_ffn_g[0][None, :], w_r, b_r, tri)
    src, g_step, n_live, back = _sort_tables(cnt[:, :N_GROUPS, 0].astype(I32), n_steps)
    y_sorted = _moe_call(src, g_step, n_live, stage, w_gate[0], w_up[0], w_down[0], n_steps)
    out = _final_call(back, y_sorted, h, dest, norm_final_g[None, :])
    return out.reshape(bsz, seq, D_MODEL)
```

```python
import functools

import jax
import jax.numpy as jnp
from jax import lax
from jax.experimental import pallas as pl
from jax.experimental.pallas import tpu as pltpu

D_MODEL = 1024
N_META = 16
CHUNK = 64
EPS = 1e-6
SSM_WIDTH = 256
SSM_GROUP = 16
SSM_GROUPS = 16
SSM_STATE = 64
RET_WIDTH = 768
HEAD_DIM = 128
HEADS = 6
ROPE_BASE = 10000.0
IN_WIDTH = SSM_WIDTH + 4 * RET_WIDTH
N_GROUPS = 4
EXPERTS_PER_GROUP = 4
N_EXPERTS = 16
EXPERT_FF = 256

S5_BLOCK = 16
S5_LANES = S5_BLOCK * SSM_GROUP
SCAN_ROWS = 8
RET_BLOCK = 256
TILE = 512
LANES = 128
ROUTE_ROWS = 32
PROJ_ROWS = 256
PIECE = 16
TILE_PIECES = TILE // PIECE
SORTED_PIECES = TILE_PIECES + N_GROUPS
SORTED_ROWS = SORTED_PIECES * PIECE
EXT_WIDTH = D_MODEL + LANES
VMEM_LIMIT = 56 * 1024 * 1024

F32 = jnp.float32
BF16 = jnp.bfloat16
I32 = jnp.int32


def _dot(a, b):
    return jnp.dot(a, b, preferred_element_type=F32)


def _sigmoid(x):
    return 1.0 / (1.0 + jnp.exp(-x))


def _rms_norm(x, g):
    return x * lax.rsqrt(jnp.mean(x * x, axis=-1, keepdims=True) + EPS) * g


def _rope(t, cos, sin_signed):
    return t * cos + pltpu.roll(t, HEAD_DIM // 2, axis=1) * sin_signed


def _meta_kernel(meta_ref, g_ref, w_ref, cos_ref, sin_ref, kdec_ref, u_ref, r0_ref):
    a = _rms_norm(meta_ref[...], g_ref[...]).astype(BF16)
    u_ref[...] = _dot(a, w_ref[:, :SSM_WIDTH])
    k_off = SSM_WIDTH + RET_WIDTH
    v_off = SSM_WIDTH + 2 * RET_WIDTH
    cos = cos_ref[...]
    sin = sin_ref[...]
    for h in range(HEADS):
        k = _dot(a, w_ref[:, k_off + h * HEAD_DIM:k_off + (h + 1) * HEAD_DIM])
        v = _dot(a, w_ref[:, v_off + h * HEAD_DIM:v_off + (h + 1) * HEAD_DIM])
        kd = (_rope(k, cos, sin) * kdec_ref[h]).astype(BF16)
        r0_ref[h] = lax.dot_general(kd, v.astype(BF16), (((0,), (0,)), ((), ())),
                                    preferred_element_type=F32)


def _meta_call(meta, g, w_in, cos, sin, kdec):
    return pl.pallas_call(
        _meta_kernel,
        out_shape=(jax.ShapeDtypeStruct((N_META, SSM_WIDTH), F32),
                   jax.ShapeDtypeStruct((HEADS, HEAD_DIM, HEAD_DIM), F32)),
        compiler_params=pltpu.CompilerParams(vmem_limit_bytes=VMEM_LIMIT),
        name="meta_call",
    )(meta, g, w_in, cos, sin, kdec)


def _mixer_kernel(x_ref, g_ref, w_ref, cos_ref, sin_ref, mask_ref, qdec_ref, kdec_ref,
                  bdec_ref, r0_ref, u_ref, y_ref, r_ref):
    @pl.when(pl.program_id(1) == 0)
    def _():
        r_ref[...] = r0_ref[...]

    off = SSM_WIDTH
    for b in range(TILE // RET_BLOCK):
        bs = slice(b * RET_BLOCK, (b + 1) * RET_BLOCK)
        a = _rms_norm(x_ref[bs, :], g_ref[...]).astype(BF16)
        u_ref[bs, :] = _dot(a, w_ref[:, :SSM_WIDTH])
        q_all = _dot(a, w_ref[:, off:off + RET_WIDTH])
        k_all = _dot(a, w_ref[:, off + RET_WIDTH:off + 2 * RET_WIDTH])
        v_all = _dot(a, w_ref[:, off + 2 * RET_WIDTH:off + 3 * RET_WIDTH])
        gate = _dot(a, w_ref[:, off + 3 * RET_WIDTH:off + 4 * RET_WIDTH])
        cos = cos_ref[bs, :]
        sin = sin_ref[bs, :]
        def scores(h):
            hs = slice(h * HEAD_DIM, (h + 1) * HEAD_DIM)
            q = _rope(q_all[:, hs], cos, sin)
            k = _rope(k_all[:, hs], cos, sin)
            return q, k, _dot_t(q.astype(BF16), k.astype(BF16))

        ahead = scores(0)
        for h in range(HEADS):
            hs = slice(h * HEAD_DIM, (h + 1) * HEAD_DIM)
            q, k, s = ahead
            if h + 1 < HEADS:
                ahead = scores(h + 1)
            v = v_all[:, hs].astype(BF16)
            p = (s * mask_ref[h]).astype(BF16)
            state = r_ref[h]
            o = _dot(p, v) + _dot((q * qdec_ref[h]).astype(BF16), state.astype(BF16))
            kv = lax.dot_general((k * kdec_ref[h]).astype(BF16), v, (((0,), (0,)), ((), ())),
                                 preferred_element_type=F32)
            r_ref[h] = state * bdec_ref[h] + kv
            mu = jnp.mean(o, axis=-1, keepdims=True)
            d = o - mu
            var = jnp.mean(d * d, axis=-1, keepdims=True)
            gt = gate[:, hs]
            y_ref[bs, hs] = (gt * _sigmoid(gt) * d * lax.rsqrt(var + EPS)).astype(BF16)


def _mixer_call(x, g, w_in, cos, sin, mask, qdec, kdec, bdec, r0):
    bsz, seq, _ = x.shape
    const = lambda *shape: pl.BlockSpec(shape, lambda b, i: (0,) * len(shape))
    return pl.pallas_call(
        _mixer_kernel,
        grid=(bsz, seq // TILE),
        in_specs=[
            pl.BlockSpec((None, TILE, D_MODEL), lambda b, i: (b, i, 0)),
            const(1, D_MODEL),
            const(D_MODEL, IN_WIDTH),
            pl.BlockSpec((TILE, HEAD_DIM), lambda b, i: (i, 0)),
            pl.BlockSpec((TILE, HEAD_DIM), lambda b, i: (i, 0)),
            const(HEADS, RET_BLOCK, RET_BLOCK),
            const(HEADS, RET_BLOCK, HEAD_DIM),
            const(HEADS, RET_BLOCK, HEAD_DIM),
            const(HEADS, 1, HEAD_DIM),
            const(HEADS, HEAD_DIM, HEAD_DIM),
        ],
        out_specs=(
            pl.BlockSpec((None, TILE, SSM_WIDTH), lambda b, i: (b, i, 0)),
            pl.BlockSpec((None, TILE, RET_WIDTH), lambda b, i: (b, i, 0)),
        ),
        out_shape=(jax.ShapeDtypeStruct((bsz, seq, SSM_WIDTH), F32),
                   jax.ShapeDtypeStruct((bsz, seq, RET_WIDTH), BF16)),
        scratch_shapes=[pltpu.VMEM((HEADS, HEAD_DIM, HEAD_DIM), F32)],
        compiler_params=pltpu.CompilerParams(
            dimension_semantics=("arbitrary", "arbitrary"), vmem_limit_bytes=VMEM_LIMIT),
        name="mixer_call",
    )(x, g, w_in, cos, sin, mask, qdec, kdec, bdec, r0)


def _dot_t(a, b):
    return lax.dot_general(a, b, (((1,), (1,)), ((), ())), preferred_element_type=F32)


def _chunk_transpose(arrs):
    n = len(arrs)
    chunk = lax.broadcasted_iota(I32, (1, LANES), 1) // SSM_GROUP
    arrs = list(arrs)
    s = n // 2
    while s:
        keep = (chunk & s) == 0
        nxt = list(arrs)
        for i in range(n):
            if i & s == 0:
                lo, hi = arrs[i], arrs[i + s]
                nxt[i] = jnp.where(keep, lo, pltpu.roll(hi, s * SSM_GROUP, axis=1))
                nxt[i + s] = jnp.where(keep, pltpu.roll(lo, LANES - s * SSM_GROUP, axis=1), hi)
        arrs = nxt
        s //= 2
    return arrs


def _s5_kernel(u_lo_ref, u_hi_ref, um_ref, krow_ref, bmat_ref, cre_ref, cim_ref, ar_ref, ai_ref, wglu_ref,
               y_lo_ref, y_hi_ref, t0_ref, ug_ref, yg_ref):
    n_blocks = u_lo_ref.shape[0] // S5_BLOCK

    @pl.when(pl.program_id(0) == 0)
    def _():
        lane = lax.broadcasted_iota(I32, (SSM_GROUP, S5_LANES), 1)
        for g in range(SSM_GROUPS):
            k0 = krow_ref[g]
            for a in range(S5_BLOCK):
                blk = k0 if a == 0 else jnp.where(lane >= a * SSM_GROUP,
                                                  pltpu.roll(k0, a * SSM_GROUP, axis=1), 0.0)
                t0_ref[g, a * SSM_GROUP:(a + 1) * SSM_GROUP, :] = blk.astype(BF16)

    per_tile = LANES // SSM_GROUP
    for half, uh_ref in enumerate((u_lo_ref, u_hi_ref)):
        for t in range(S5_BLOCK // per_tile):
            words = [pltpu.bitcast(uh_ref[pl.ds(t * per_tile + k, n_blocks, stride=S5_BLOCK), :].astype(BF16),
                                   jnp.uint32) for k in range(per_tile)]
            for m, w in enumerate(_chunk_transpose(words)):
                ug_ref[half * per_tile + m, :, t * LANES:(t + 1) * LANES] = pltpu.bitcast(w, BF16)

    re, im, re0, im0 = [], [], [], []
    for p in range(SSM_GROUPS // 2):
        g0, g1 = 2 * p, 2 * p + 1
        v = _dot(ug_ref[g0], bmat_ref[g0]) + _dot(ug_ref[g1], bmat_ref[g1])
        v0 = (_dot(um_ref[g0].astype(BF16), bmat_ref[g0]) + _dot(um_ref[g1].astype(BF16), bmat_ref[g1]))[0:1]
        re.append(v[:, :LANES])
        im.append(v[:, LANES:])
        re0.append(v0[:, :LANES])
        im0.append(v0[:, LANES:])
    re, im, re0, im0 = (jnp.concatenate(parts, axis=1) for parts in (re, im, re0, im0))

    row = lax.broadcasted_iota(I32, re.shape, 0)
    ar, ai = ar_ref[0:1], ai_ref[0:1]
    re = re + jnp.where(row == 0, ar * re0 - ai * im0, 0.0)
    im = im + jnp.where(row == 0, ar * im0 + ai * re0, 0.0)
    d = 1
    while d < SCAN_ROWS:
        ar, ai = ar_ref[d - 1:d], ai_ref[d - 1:d]
        inside = row % SCAN_ROWS >= d
        sre = jnp.where(inside, pltpu.roll(re, d, axis=0), 0.0)
        sim = jnp.where(inside, pltpu.roll(im, d, axis=0), 0.0)
        re, im = re + ar * sre - ai * sim, im + ar * sim + ai * sre
        d *= 2
    ar, ai = ar_ref[...], ai_ref[...]
    re_tiles, im_tiles = [re[:SCAN_ROWS]], [im[:SCAN_ROWS]]
    for r in range(1, n_blocks // SCAN_ROWS):
        cre_ = re_tiles[-1][SCAN_ROWS - 1:SCAN_ROWS]
        cim_ = im_tiles[-1][SCAN_ROWS - 1:SCAN_ROWS]
        rs = slice(r * SCAN_ROWS, (r + 1) * SCAN_ROWS)
        re_tiles.append(re[rs] + ar * cre_ - ai * cim_)
        im_tiles.append(im[rs] + ar * cim_ + ai * cre_)
    re = jnp.concatenate(re_tiles, axis=0)
    im = jnp.concatenate(im_tiles, axis=0)
    pre =jnp.where(row == 0, re0, pltpu.roll(re, 1, axis=0)).astype(BF16)
    pim = jnp.where(row == 0, im0, pltpu.roll(im, 1, axis=0)).astype(BF16)

    for g in range(SSM_GROUPS):
        ps = slice((g // 2) * LANES, (g // 2 + 1) * LANES)
        yg_ref[g] = (_dot(ug_ref[g], t0_ref[g]) + _dot_t(pre[:, ps], cre_ref[g])
                     + _dot_t(pim[:, ps], cim_ref[g]))

    wglu = wglu_ref[...]
    for t in range(S5_BLOCK // per_tile):
        ts = slice(t * LANES, (t + 1) * LANES)
        halves = [_chunk_transpose([yg_ref[half * per_tile + m, :, ts] for m in range(per_tile)])
                  for half in range(SSM_GROUPS // per_tile)]
        for k in range(per_tile):
            y = jnp.concatenate([h[k] for h in halves], axis=1)
            y = jax.nn.gelu(y, approximate=True)
            y = y * _sigmoid(_dot(y.astype(BF16), wglu))
            i = t * per_tile + k
            y_lo_ref[pl.ds(i, n_blocks, stride=S5_BLOCK), :] = y[:, :LANES]
            y_hi_ref[pl.ds(i, n_blocks, stride=S5_BLOCK), :] = y[:, LANES:]


def _s5_call(u, um, krow, bmat, cre, cim, ar, ai, wglu):
    bsz, seq, _ = u.shape
    n_blocks = seq // S5_BLOCK
    const = lambda a: pl.BlockSpec(a.shape, lambda b: (0,) * a.ndim)
    return pl.pallas_call(
        _s5_kernel,
        grid=(bsz,),
        in_specs=[pl.BlockSpec((None, seq, LANES), lambda b: (b, 0, 0)),
                  pl.BlockSpec((None, seq, LANES), lambda b: (b, 0, 1)),
                  const(um), const(krow), const(bmat), const(cre), const(cim), const(ar), const(ai),
                  const(wglu)],
        out_specs=(pl.BlockSpec((None, seq, LANES), lambda b: (b, 0, 0)),
                   pl.BlockSpec((None, seq, LANES), lambda b: (b, 0, 0))),
        out_shape=(jax.ShapeDtypeStruct((bsz, seq, LANES), F32),
                   jax.ShapeDtypeStruct((bsz, seq, LANES), F32)),
        scratch_shapes=[pltpu.VMEM((SSM_GROUPS, S5_LANES, S5_LANES), BF16),
                        pltpu.VMEM((SSM_GROUPS, n_blocks, S5_LANES), BF16),
                        pltpu.VMEM((SSM_GROUPS, n_blocks, S5_LANES), F32)],
        compiler_params=pltpu.CompilerParams(
            dimension_semantics=("arbitrary",), vmem_limit_bytes=VMEM_LIMIT),
        name="s5_call",
    )(u, u, um, krow, bmat, cre, cim, ar, ai, wglu)


def _s5_operators(lam_re, lam_im, log_dt, b_re, b_im, c_re, c_im, d_skip):
    n_groups = lam_re.shape[0]
    lam = lax.complex(lam_re, lam_im)
    lam_dt = lam * jnp.exp(log_dt)[:, None]
    lam_bar = jnp.exp(lam_dt)
    b_bar = ((lam_bar - 1.0) / lam)[..., None] * lax.complex(b_re, b_im)
    c = lax.complex(c_re, c_im)
    tau = jnp.arange(S5_BLOCK + 1, dtype=F32)
    pows = jnp.exp(lam_dt[None] * tau[:, None, None])
    kern = jnp.real(jnp.einsum('ghp,tgp,gpk->gkth', c, pows[:S5_BLOCK], b_bar))
    skip = (jnp.eye(SSM_GROUP, dtype=F32)[None, :, None, :] * d_skip[:, None, None, :]
            * (tau[:S5_BLOCK] == 0).astype(F32)[None, None, :, None])
    krow = (kern + skip).reshape(n_groups, SSM_GROUP, S5_LANES)
    even = (jnp.arange(n_groups) % 2 == 0)[:, None, None]
    pair_pad = lambda m: jnp.concatenate([jnp.where(even, m, 0.0), jnp.where(even, 0.0, m)], axis=-1)
    bm = (pows[S5_BLOCK - 1 - jnp.arange(S5_BLOCK)].transpose(1, 0, 2)[:, :, None, :]
          * b_bar.transpose(0, 2, 1)[:, None, :, :]).reshape(n_groups, S5_LANES, SSM_STATE)
    bmat = jnp.concatenate([pair_pad(jnp.real(bm)), pair_pad(jnp.imag(bm))], axis=-1)
    cm = (pows[1:].transpose(1, 0, 2)[:, :, None, :] * c[:, None, :, :]).reshape(n_groups, S5_LANES, SSM_STATE)
    cre, cim = pair_pad(jnp.real(cm)), pair_pad(-jnp.imag(cm))
    step = S5_BLOCK * (1.0 + jnp.arange(SCAN_ROWS, dtype=F32))
    adec = jnp.exp(lam_dt[None, :, :] * step[:, None, None]).reshape(SCAN_ROWS, n_groups * SSM_STATE)
    return krow, bmat.astype(BF16), cre.astype(BF16), cim.astype(BF16), jnp.real(adec), jnp.imag(adec)


def _first_hit(values, target):
    hits, taken = [], None
    for v in values:
        hit = (v >= target) if taken is None else jnp.logical_and(v >= target, jnp.logical_not(taken))
        taken = hit if taken is None else jnp.logical_or(taken, hit)
        hits.append(hit)
    return hits


def _stack_rows(rows, n_rows):
    idx = lax.broadcasted_iota(I32, (n_rows, rows[0].shape[1]), 0)
    out = jnp.zeros((n_rows, rows[0].shape[1]), F32)
    for k, r in enumerate(rows):
        out = jnp.where(idx == k, r, out)
    return out


def _project(rs, x_ref, ys_lo_ref, ys_hi_ref, yr_ref, wout_ref, h_ref):
    ys = jnp.concatenate([ys_lo_ref[rs, :], ys_hi_ref[rs, :]], axis=1).astype(BF16)
    h = x_ref[rs, :] + _dot(ys, wout_ref[:SSM_WIDTH, :]) + _dot(yr_ref[rs, :], wout_ref[SSM_WIDTH:, :])
    h_ref[rs, :] = h.astype(BF16)
    return h


def _route(lt, tri_ref):
    gl = [lt[g:g + 1, :] for g in range(N_GROUPS)]
    gmax = functools.reduce(jnp.maximum, gl)
    g_w = 1.0 / functools.reduce(lambda a, b: a + b, [jnp.exp(l - gmax) for l in gl])
    sel = _first_hit(gl, gmax)
    ev = []
    for e in range(EXPERTS_PER_GROUP):
        acc = jnp.zeros_like(gmax)
        for g in range(N_GROUPS):
            k = N_GROUPS + g * EXPERTS_PER_GROUP + e
            acc = jnp.where(sel[g], lt[k:k + 1, :], acc)
        ev.append(acc)
    m1 = functools.reduce(jnp.maximum, ev)
    first = _first_hit(ev, m1)
    rest = [jnp.where(f, -jnp.inf, v) for f, v in zip(first, ev)]
    m2 = functools.reduce(jnp.maximum, rest)
    second = _first_hit(rest, m2)
    e2 = jnp.exp(m2 - m1)
    w1 = g_w / (1.0 + e2)
    w2 = e2 * w1
    combine = [jnp.where(f, w1, 0.0) + jnp.where(s, w2, 0.0) for f, s in zip(first, second)]

    sel_f = [jnp.where(s, 1.0, 0.0) for s in sel]
    incl = _dot(_stack_rows(sel_f, 8).astype(BF16), tri_ref[...])
    dest = jnp.zeros_like(gmax)
    seg_start = jnp.zeros((1, 1), F32)
    counts = []
    for g in range(N_GROUPS):
        run = incl[g:g + 1, :]
        cnt = run[:, TILE - 1:TILE]
        counts.append(cnt)
        dest = dest + sel_f[g] * (seg_start + run - 1.0)
        seg_start = seg_start + PIECE * jnp.floor((cnt + (PIECE - 1.0)) * (1.0 / PIECE))
    return combine, dest, counts


def _proj_kernel(x_ref, ys_lo_ref, ys_hi_ref, yr_ref, wout_ref, g_ref, wr_ref, br_ref,
                 tri_ref, h_ref, stage_ref, dest_ref, cnt_ref, t_scr, lt_scr):
    step = pl.program_id(0)
    cur = step % 2
    prev = 1 - cur

    @pl.when(step == 0)
    def _():
        t_scr[prev] = jnp.zeros(t_scr.shape[1:], BF16)
        lt_scr[prev] = jnp.zeros(lt_scr.shape[1:], F32)

    chunks = [slice(b * PROJ_ROWS, (b + 1) * PROJ_ROWS) for b in range(TILE // PROJ_ROWS)]
    proj_refs = (x_ref, ys_lo_ref, ys_hi_ref, yr_ref, wout_ref, h_ref)
    h_parts = [_project(chunks[0], *proj_refs)]

    combine, dest, counts = _route(lt_scr[prev], tri_ref)
    dest_ref[...] = _stack_rows([dest], 8)
    cnt_ref[...] = _stack_rows([c + jnp.zeros((1, LANES), F32) for c in counts], 8)
    perm = jnp.where(lax.broadcasted_iota(I32, (SORTED_ROWS, TILE), 0) == dest.astype(I32),
                     1.0, 0.0).astype(BF16)
    c_hi = [c.astype(BF16).astype(F32) for c in combine]
    c_lo = [c - hi for c, hi in zip(combine, c_hi)]
    cw = _stack_rows(c_hi + c_lo, LANES).T.astype(BF16)

    h_parts += [_project(rs, *proj_refs) for rs in chunks[1:]]

    t_parts = []
    for rs, h in zip(chunks, h_parts):
        t = _rms_norm(h, g_ref[...])
        t_hi = t.astype(BF16)
        t_parts.append((rs, t_hi, (t - t_hi.astype(F32)).astype(BF16)))

    def logits_hi(rs, t_hi):
        both = _dot_t(wr_ref[...], t_hi)
        lt_scr[cur, :, rs] = both[:ROUTE_ROWS] + both[ROUTE_ROWS:] + br_ref[...]

    def logits_lo(rs, t_lo):
        lt_scr[cur, :, rs] += _dot_t(wr_ref[:ROUTE_ROWS, :], t_lo)

    router = [functools.partial(logits_hi, rs, t_hi) for rs, t_hi, _ in t_parts]
    router += [functools.partial(logits_lo, rs, t_lo) for rs, _, t_lo in t_parts]
    n_col = D_MODEL // PROJ_ROWS
    for c in range(n_col):
        cs = slice(c * PROJ_ROWS, (c + 1) * PROJ_ROWS)
        stage_ref[:, cs] = _dot(perm, t_scr[prev, :, cs]).astype(BF16)
        if c < len(router):
            router[c]()
    for rest in router[n_col:]:
        rest()
    stage_ref[:, D_MODEL:] = _dot(perm, cw).astype(BF16)
    for rs, t_hi, _ in t_parts:
        t_scr[cur, rs, :] = t_hi


def _proj_call(x, ys_lo, ys_hi, yr, wout, g, wr, br, tri):
    n_tok = x.shape[0]
    n_tiles = n_tok // TILE
    const = lambda *shape: pl.BlockSpec(shape, lambda i: (0,) * len(shape))
    front = lambda i: jnp.minimum(i, n_tiles - 1)
    back = lambda i: jnp.maximum(i - 1, 0)
    rows = lambda width: pl.BlockSpec((TILE, width), lambda i: (front(i), 0))
    return pl.pallas_call(
        _proj_kernel,
        grid=(n_tiles + 1,),
        in_specs=[
            rows(D_MODEL), rows(LANES), rows(LANES), rows(RET_WIDTH),
            const(D_MODEL, D_MODEL), const(1, D_MODEL),
            const(2 * ROUTE_ROWS, D_MODEL), const(ROUTE_ROWS, 1), const(TILE, TILE),
        ],
        out_specs=(rows(D_MODEL),
                   pl.BlockSpec((SORTED_ROWS, EXT_WIDTH), lambda i: (back(i), 0)),
                   pl.BlockSpec((None, 8, TILE), lambda i: (back(i), 0, 0)),
                   pl.BlockSpec((None, 8, LANES), lambda i: (back(i), 0, 0))),
        scratch_shapes=[pltpu.VMEM((2, TILE, D_MODEL), BF16), pltpu.VMEM((2, ROUTE_ROWS, TILE), F32)],
        out_shape=(jax.ShapeDtypeStruct((n_tok, D_MODEL), BF16),
                   jax.ShapeDtypeStruct((n_tiles * SORTED_ROWS, EXT_WIDTH), BF16),
                   jax.ShapeDtypeStruct((n_tiles, 8, TILE), F32),
                   jax.ShapeDtypeStruct((n_tiles, 8, LANES), F32)),
        compiler_params=pltpu.CompilerParams(
            dimension_semantics=("arbitrary",), vmem_limit_bytes=VMEM_LIMIT),
        name="proj_call",
    )(x, ys_lo, ys_hi, yr, wout, g, wr, br, tri)


def _sort_tables(cnt, n_steps):
    n_tiles = cnt.shape[0]
    npc = (cnt + PIECE - 1) // PIECE
    seg = jnp.cumsum(npc, axis=1) - npc
    before = jnp.cumsum(npc, axis=0) - npc
    n_tile_g = (jnp.sum(npc, axis=0) + TILE_PIECES - 1) // TILE_PIECES
    t_off = jnp.cumsum(n_tile_g) - n_tile_g
    j = jnp.arange(SORTED_PIECES, dtype=I32)[None, :, None]
    in_g = jnp.logical_and(j >= seg[:, None, :], j < (seg + npc)[:, None, :])
    pos = jnp.sum(jnp.where(in_g, TILE_PIECES * t_off[None, None, :] + before[:, None, :] + j - seg[:, None, :], 0),
                  axis=-1)
    valid = jnp.any(in_g, axis=-1)
    stage_piece = jnp.arange(n_tiles * SORTED_PIECES, dtype=I32).reshape(n_tiles, SORTED_PIECES)
    n_slots = n_steps * TILE_PIECES
    src = jnp.zeros((n_slots,), I32).at[jnp.where(valid, pos, n_slots).reshape(-1)].set(
        stage_piece.reshape(-1), mode='drop')
    steps = jnp.arange(n_steps, dtype=I32)
    g_step = jnp.minimum(jnp.sum(steps[:, None] >= (t_off + n_tile_g)[None, :], axis=1), N_GROUPS - 1)
    n_live = jnp.sum(n_tile_g).reshape(1)
    back = jnp.where(valid, pos, 0).reshape(-1)
    return src, g_step.astype(I32), n_live.astype(I32), back.astype(I32)


def _piece_copy(src_ref, piece, buf_ref, slot, j, sem_ref):
    start = piece * PIECE if isinstance(piece, int) else pl.multiple_of(piece * PIECE, PIECE)
    return pltpu.make_async_copy(src_ref.at[pl.ds(start, PIECE)],
                                 buf_ref.at[slot, pl.ds(j * PIECE, PIECE)], sem_ref.at[slot])


def _fetch_pieces(table_ref, src_ref, buf_ref, sem_ref, row, slot, n_pieces):
    for j in range(n_pieces):
        _piece_copy(src_ref, table_ref[row * n_pieces + j], buf_ref, slot, j, sem_ref).start()


def _wait_pieces(src_ref, buf_ref, sem_ref, slot, n_pieces):
    for j in range(n_pieces):
        _piece_copy(src_ref, 0, buf_ref, slot, j, sem_ref).wait()


def _gather_pieces(table_ref, src_ref, buf_ref, sem_ref, n_pieces):
    step = pl.program_id(0)
    last = pl.num_programs(0) - 1
    args = (src_ref, buf_ref, sem_ref)

    @pl.when(step == 0)
    def _():
        _fetch_pieces(table_ref, *args, step, 0, n_pieces)

    _wait_pieces(*args, step % 2, n_pieces)

    def fetch_next():
        _fetch_pieces(table_ref, *args, jnp.minimum(step + 1, last), (step + 1) % 2, n_pieces)

    def drain():
        @pl.when(step == last)
        def _():
            _wait_pieces(*args, (step + 1) % 2, n_pieces)

    return fetch_next, drain


def _moe_kernel(src_ref, gstep_ref, nlive_ref, stage_ref, wg_ref, wu_ref, wd_ref, y_ref,
                buf_ref, sem_ref, wgb_ref, wub_ref, wdb_ref):
    step = pl.program_id(0)
    fetch_next, drain = _gather_pieces(src_ref, stage_ref, buf_ref, sem_ref, TILE_PIECES)

    @pl.when(jnp.logical_or(step == 0, gstep_ref[step] != gstep_ref[jnp.maximum(step - 1, 0)]))
    def _():
        for e in range(EXPERTS_PER_GROUP):
            wgb_ref[e] = wg_ref[e].astype(BF16)
            wub_ref[e] = wu_ref[e].astype(BF16)
            wdb_ref[e * EXPERT_FF:(e + 1) * EXPERT_FF, :] = wd_ref[e].astype(BF16)

    @pl.when(step < nlive_ref[0])
    def _():
        slot = step % 2
        fetch_next()

        def up(rs):
            t = buf_ref[slot, rs, :D_MODEL]
            return [(_dot(t, wgb_ref[e]), _dot(t, wub_ref[e])) for e in range(EXPERTS_PER_GROUP)]

        def down(rs, hidden):
            cw = buf_ref[slot, rs, D_MODEL:].astype(F32)
            acts = []
            for e, (hg, hu) in enumerate(hidden):
                c = cw[:, e:e + 1] + cw[:, EXPERTS_PER_GROUP + e:EXPERTS_PER_GROUP + e + 1]
                acts.append((hg * _sigmoid(hg) * hu * c).astype(BF16))
            y_ref[rs, :] = _dot(jnp.concatenate(acts, axis=1), wdb_ref[...]).astype(BF16)

        halves = [slice(b * PROJ_ROWS, (b + 1) * PROJ_ROWS) for b in range(TILE // PROJ_ROWS)]
        hidden = [up(rs) for rs in halves]
        for rs, hid in zip(halves, hidden):
            down(rs, hid)

    @pl.when(step >= nlive_ref[0])
    def _():
        fetch_next()
        y_ref[...] = jnp.zeros_like(y_ref)

    drain()


def _moe_call(src, g_step, n_live, stage, wg, wu, wd, n_steps):
    grp = lambda shape: pl.BlockSpec((EXPERTS_PER_GROUP,) + shape, lambda s, src, gs, nl: (gs[s], 0, 0))
    return pl.pallas_call(
        _moe_kernel,
        grid_spec=pltpu.PrefetchScalarGridSpec(
            num_scalar_prefetch=3,
            grid=(n_steps,),
            in_specs=[pl.BlockSpec(memory_space=pl.ANY),
                      grp((D_MODEL, EXPERT_FF)), grp((D_MODEL, EXPERT_FF)), grp((EXPERT_FF, D_MODEL))],
            out_specs=pl.BlockSpec((TILE, D_MODEL), lambda s, src, gs, nl: (s, 0)),
            scratch_shapes=[pltpu.VMEM((2, TILE, EXT_WIDTH), BF16), pltpu.SemaphoreType.DMA((2,)),
                            pltpu.VMEM((EXPERTS_PER_GROUP, D_MODEL, EXPERT_FF), BF16),
                            pltpu.VMEM((EXPERTS_PER_GROUP, D_MODEL, EXPERT_FF), BF16),
                            pltpu.VMEM((EXPERTS_PER_GROUP * EXPERT_FF, D_MODEL), BF16)],
        ),
        out_shape=jax.ShapeDtypeStruct((n_steps * TILE, D_MODEL), BF16),
        compiler_params=pltpu.CompilerParams(
            dimension_semantics=("arbitrary",), vmem_limit_bytes=VMEM_LIMIT),
        name="moe_call",
    )(src, g_step, n_live, stage, wg, wu, wd)


def _final_kernel(back_ref, ysort_ref, h_ref, dest_ref, g_ref, o_ref, buf_ref, sem_ref):
    step = pl.program_id(0)
    fetch_next, drain = _gather_pieces(back_ref, ysort_ref, buf_ref, sem_ref, SORTED_PIECES)
    fetch_next()
    dest = _stack_rows([dest_ref[0:1, :]], LANES).T[:, 0:1].astype(I32)
    unperm = jnp.where(lax.broadcasted_iota(I32, (TILE, SORTED_ROWS), 1) == dest, 1.0, 0.0).astype(BF16)
    rows = buf_ref[step % 2]
    for b in range(TILE // PROJ_ROWS):
        rs = slice(b * PROJ_ROWS, (b + 1) * PROJ_ROWS)
        o_ref[rs, :] = _rms_norm(h_ref[rs, :].astype(F32) + _dot(unperm[rs], rows), g_ref[...])
    drain()


def _final_call(back, ysort, h, dest, g):
    n_tok = h.shape[0]
    return pl.pallas_call(
        _final_kernel,
        grid_spec=pltpu.PrefetchScalarGridSpec(
            num_scalar_prefetch=1,
            grid=(n_tok // TILE,),
            in_specs=[pl.BlockSpec(memory_space=pl.ANY),
                      pl.BlockSpec((TILE, D_MODEL), lambda i, back: (i, 0)),
                      pl.BlockSpec((None, 8, TILE), lambda i, back: (i, 0, 0)),
                      pl.BlockSpec((1, D_MODEL), lambda i, back: (0, 0))],
            out_specs=pl.BlockSpec((TILE, D_MODEL), lambda i, back: (i, 0)),
            scratch_shapes=[pltpu.VMEM((2, SORTED_ROWS, D_MODEL), BF16), pltpu.SemaphoreType.DMA((2,))],
        ),
        out_shape=jax.ShapeDtypeStruct((n_tok, D_MODEL), F32),
        compiler_params=pltpu.CompilerParams(
            dimension_semantics=("arbitrary",), vmem_limit_bytes=VMEM_LIMIT),
        name="final_call",
    )(back, ysort, h, dest, g)


def _rope_tables(length):
    pos = jnp.arange(length, dtype=F32)
    inv_freq = ROPE_BASE ** (-jnp.arange(0, HEAD_DIM, 2, dtype=F32) / HEAD_DIM)
    ang = pos[:, None] * inv_freq[None, :]
    cos, sin = jnp.cos(ang), jnp.sin(ang)
    return jnp.concatenate([cos, cos], axis=-1), jnp.concatenate([-sin, sin], axis=-1)


def _retention_tables():
    gamma = 1.0 - 2.0 ** (-5.0 - jnp.arange(HEADS, dtype=F32))
    log_g = jnp.log(gamma)[:, None, None]
    scale = HEAD_DIM ** -0.5
    idx = jnp.arange(RET_BLOCK)
    dist = jnp.abs(idx[:, None] - idx[None, :]).astype(F32)
    visible = (idx[None, :] // CHUNK) <= (idx[:, None] // CHUNK)
    mask = jnp.where(visible[None], jnp.exp(log_g * dist[None]), 0.0) * scale
    ones = jnp.ones((1, 1, HEAD_DIM), F32)
    idx_f = idx.astype(F32)[None, :, None]
    qdec = jnp.exp(log_g * (idx_f + 1.0)) * ones
    kdec = jnp.exp(log_g * (RET_BLOCK - 1.0 - idx_f)) * scale * ones
    bdec = jnp.exp(log_g * float(RET_BLOCK)) * ones
    meta_idx = jnp.arange(N_META, dtype=F32)[None, :, None]
    kdec_meta = jnp.exp(log_g * (N_META - 1.0 - meta_idx)) * scale * ones
    return mask, qdec, kdec, bdec, kdec_meta


def kernel(x, meta_tokens, norm_mix_g, w_in, ssm_lambda_re, ssm_lambda_im, ssm_log_dt, ssm_b_re, ssm_b_im, ssm_c_re, ssm_c_im, ssm_d, w_glu, w_out, norm_ffn_g, w_router_group, b_router_group, w_router_expert, b_router_expert, w_gate, w_up, w_down, norm_final_g):
    bsz, seq, _ = x.shape
    assert seq % TILE == 0 and TILE % RET_BLOCK == 0 and RET_BLOCK % CHUNK == 0
    n_blocks = seq // S5_BLOCK
    assert n_blocks % SCAN_ROWS == 0
    n_tok = bsz * seq
    n_tiles = n_tok // TILE
    n_steps = -(-n_tiles * (TILE_PIECES + N_GROUPS - 1) // TILE_PIECES) + N_GROUPS

    cos, sin = _rope_tables(N_META + seq)
    mask, qdec, kdec, bdec, kdec_meta = _retention_tables()
    g_mix = norm_mix_g[0][None, :]
    w_in_b = w_in[0].astype(BF16)

    u_meta, r0 = _meta_call(meta_tokens, g_mix, w_in_b, cos[:N_META], sin[:N_META], kdec_meta)
    u, y_ret = _mixer_call(x, g_mix, w_in_b, cos[N_META:], sin[N_META:], mask, qdec, kdec, bdec, r0)

    s5_ops = _s5_operators(
        ssm_lambda_re[0], ssm_lambda_im[0], ssm_log_dt[0], ssm_b_re[0], ssm_b_im[0],
        ssm_c_re[0], ssm_c_im[0], ssm_d[0])
    um = u_meta.reshape(S5_BLOCK, SSM_GROUPS, SSM_GROUP).transpose(1, 0, 2).reshape(SSM_GROUPS, 1, S5_LANES)
    um = jnp.pad(um, ((0, 0), (0, 7), (0, 0)))
    y_lo, y_hi = _s5_call(u, um, *s5_ops, w_glu[0].astype(BF16))

    w_r = jnp.concatenate(
        [w_router_group[0].T, w_router_expert[0].transpose(0, 2, 1).reshape(N_EXPERTS, D_MODEL)], axis=0)
    w_r = jnp.pad(w_r, ((0, ROUTE_ROWS - w_r.shape[0]), (0, 0)))
    b_r = jnp.concatenate([b_router_group[0], b_router_expert[0].reshape(-1)])
    b_r = jnp.pad(b_r, (0, ROUTE_ROWS - b_r.shape[0]))[:, None]
    w_r_hi = w_r.astype(BF16)
    w_r = jnp.concatenate([w_r_hi, (w_r - w_r_hi.astype(F32)).astype(BF16)], axis=0)
    tri = (jnp.arange(TILE)[:, None] <= jnp.arange(TILE)[None, :]).astype(BF16)

    h, stage, dest, cnt = _proj_call(
        x.reshape(n_tok, D_MODEL), y_lo.reshape(n_tok, LANES), y_hi.reshape(n_tok, LANES),
        y_ret.reshape(n_tok, RET_WIDTH),
        w_out[0].astype(BF16), norm_ffn_g[0][None, :], w_r, b_r, tri)
    src, g_step, n_live, back = _sort_tables(cnt[:, :N_GROUPS, 0].astype(I32), n_steps)
    y_sorted = _moe_call(src, g_step, n_live, stage, w_gate[0], w_up[0], w_down[0], n_steps)
    out = _final_call(back, y_sorted, h, dest, norm_final_g[None, :])
    return out.reshape(bsz, seq, D_MODEL)
```

```python
import functools

import jax
import jax.numpy as jnp
import numpy as np
from jax import lax
from jax.experimental import pallas as pl
from jax.experimental.pallas import tpu as pltpu

D_MODEL = 1024
N_META = 16
CHUNK = 64
EPS = 1e-6
SSM_WIDTH = 256
SSM_GROUP = 16
SSM_GROUPS = 16
SSM_STATE = 64
RET_WIDTH = 768
HEAD_DIM = 128
HEADS = 6
ROPE_BASE = 10000.0
IN_WIDTH = SSM_WIDTH + 4 * RET_WIDTH
N_GROUPS = 4
EXPERTS_PER_GROUP = 4
N_EXPERTS = 16
EXPERT_FF = 256

S5_BLOCK = 16
S5_LANES = S5_BLOCK * SSM_GROUP
SCAN_ROWS = 8
RET_BLOCK = 256
TILE = 512
LANES = 128
ROUTE_ROWS = 32
PROJ_ROWS = 256
PIECE = 16
TILE_PIECES = TILE // PIECE
SORTED_PIECES = TILE_PIECES + N_GROUPS
SORTED_ROWS = SORTED_PIECES * PIECE
EXT_WIDTH = D_MODEL + LANES
VMEM_LIMIT = 56 * 1024 * 1024

F32 = jnp.float32
BF16 = jnp.bfloat16
I32 = jnp.int32


def _dot(a, b):
    return jnp.dot(a, b, preferred_element_type=F32)


def _sigmoid(x):
    return 1.0 / (1.0 + jnp.exp(-x))


def _rms_norm(x, g):
    return x * lax.rsqrt(jnp.mean(x * x, axis=-1, keepdims=True) + EPS) * g


def _rope(t, cos, sin_signed):
    return t * cos + pltpu.roll(t, HEAD_DIM // 2, axis=1) * sin_signed


def _meta_state(meta_ref, g_ref, w_ref, cos_ref, sin_ref, kdec_ref, u_ref, r0_ref):
    a = _rms_norm(meta_ref[...], g_ref[...]).astype(BF16)
    u_ref[...] = _dot(a, w_ref[:, :SSM_WIDTH])
    k_off = SSM_WIDTH + RET_WIDTH
    v_off = SSM_WIDTH + 2 * RET_WIDTH
    cos = cos_ref[...]
    sin = sin_ref[...]
    for h in range(HEADS):
        k = _dot(a, w_ref[:, k_off + h * HEAD_DIM:k_off + (h + 1) * HEAD_DIM])
        v = _dot(a, w_ref[:, v_off + h * HEAD_DIM:v_off + (h + 1) * HEAD_DIM])
        kd = (_rope(k, cos, sin) * kdec_ref[h]).astype(BF16)
        r0_ref[h] = lax.dot_general(kd, v.astype(BF16), (((0,), (0,)), ((), ())),
                                    preferred_element_type=F32)


def _mixer_kernel(x_ref, g_ref, w32_ref, cos_ref, sin_ref, mask_ref, qdec_ref, kdec_ref, bdec_ref,
                  meta_ref, cos_m_ref, sin_m_ref, kdec_m_ref, u_ref, y_ref, um_ref, w_ref, r0_ref, r_ref):
    first_tile = pl.program_id(1) == 0

    @pl.when(jnp.logical_and(pl.program_id(0) == 0, first_tile))
    def _():
        for c in range(0, IN_WIDTH, SSM_WIDTH):
            w_ref[:, c:c + SSM_WIDTH] = w32_ref[:, c:c + SSM_WIDTH].astype(BF16)
        _meta_state(meta_ref, g_ref, w_ref, cos_m_ref, sin_m_ref, kdec_m_ref, um_ref, r0_ref)

    @pl.when(first_tile)
    def _():
        r_ref[...] = r0_ref[...]

    off = SSM_WIDTH
    for b in range(TILE // RET_BLOCK):
        bs = slice(b * RET_BLOCK, (b + 1) * RET_BLOCK)
        a = _rms_norm(x_ref[bs, :], g_ref[...]).astype(BF16)
        u_ref[bs, :] = _dot(a, w_ref[:, :SSM_WIDTH])
        q_all = _dot(a, w_ref[:, off:off + RET_WIDTH])
        k_all = _dot(a, w_ref[:, off + RET_WIDTH:off + 2 * RET_WIDTH])
        v_all = _dot(a, w_ref[:, off + 2 * RET_WIDTH:off + 3 * RET_WIDTH])
        gate = _dot(a, w_ref[:, off + 3 * RET_WIDTH:off + 4 * RET_WIDTH])
        cos = cos_ref[bs, :]
        sin = sin_ref[bs, :]
        def scores(h):
            hs = slice(h * HEAD_DIM, (h + 1) * HEAD_DIM)
            q = _rope(q_all[:, hs], cos, sin)
            k = _rope(k_all[:, hs], cos, sin)
            return q, k, _dot_t(q.astype(BF16), k.astype(BF16))

        ahead = scores(0)
        for h in range(HEADS):
            hs = slice(h * HEAD_DIM, (h + 1) * HEAD_DIM)
            q, k, s = ahead
            if h + 1 < HEADS:
                ahead = scores(h + 1)
            v = v_all[:, hs].astype(BF16)
            p = (s * mask_ref[h]).astype(BF16)
            state = r_ref[h]
            o = _dot(p, v) + _dot((q * qdec_ref[h]).astype(BF16), state.astype(BF16))
            kv = lax.dot_general((k * kdec_ref[h]).astype(BF16), v, (((0,), (0,)), ((), ())),
                                 preferred_element_type=F32)
            r_ref[h] = state * bdec_ref[h] + kv
            mu = jnp.mean(o, axis=-1, keepdims=True)
            d = o - mu
            var = jnp.mean(d * d, axis=-1, keepdims=True)
            gt = gate[:, hs]
            y_ref[bs, hs] = (gt * _sigmoid(gt) * d * lax.rsqrt(var + EPS)).astype(BF16)


def _mixer_call(x, g, w_in, cos, sin, mask, qdec, kdec, bdec, meta, cos_m, sin_m, kdec_m):
    bsz, seq, _ = x.shape
    const = lambda a, **kw: pl.BlockSpec(a.shape, lambda b, i: (0,) * a.ndim, **kw)
    return pl.pallas_call(
        _mixer_kernel,
        grid=(bsz, seq // TILE),
        in_specs=[
            pl.BlockSpec((None, TILE, D_MODEL), lambda b, i: (b, i, 0)),
            const(g),
            const(w_in, pipeline_mode=pl.Buffered(1)),
            pl.BlockSpec((TILE, HEAD_DIM), lambda b, i: (i, 0)),
            pl.BlockSpec((TILE, HEAD_DIM), lambda b, i: (i, 0)),
            const(mask), const(qdec), const(kdec), const(bdec),
            const(meta), const(cos_m), const(sin_m), const(kdec_m),
        ],
        out_specs=(
            pl.BlockSpec((None, TILE, SSM_WIDTH), lambda b, i: (b, i, 0)),
            pl.BlockSpec((None, TILE, RET_WIDTH), lambda b, i: (b, i, 0)),
            pl.BlockSpec((N_META, SSM_WIDTH), lambda b, i: (0, 0)),
        ),
        out_shape=(jax.ShapeDtypeStruct((bsz, seq, SSM_WIDTH), F32),
                   jax.ShapeDtypeStruct((bsz, seq, RET_WIDTH), BF16),
                   jax.ShapeDtypeStruct((N_META, SSM_WIDTH), F32)),
        scratch_shapes=[pltpu.VMEM((D_MODEL, IN_WIDTH), BF16),
                        pltpu.VMEM((HEADS, HEAD_DIM, HEAD_DIM), F32),
                        pltpu.VMEM((HEADS, HEAD_DIM, HEAD_DIM), F32)],
        compiler_params=pltpu.CompilerParams(
            dimension_semantics=("arbitrary", "arbitrary"), vmem_limit_bytes=VMEM_LIMIT),
        name="mixer_call",
    )(x, g, w_in, cos, sin, mask, qdec, kdec, bdec, meta, cos_m, sin_m, kdec_m)


def _dot_t(a, b):
    return lax.dot_general(a, b, (((1,), (1,)), ((), ())), preferred_element_type=F32)


def _chunk_transpose(arrs):
    n = len(arrs)
    chunk = lax.broadcasted_iota(I32, (1, LANES), 1) // SSM_GROUP
    arrs = list(arrs)
    s = n // 2
    while s:
        keep = (chunk & s) == 0
        nxt = list(arrs)
        for i in range(n):
            if i & s == 0:
                lo, hi = arrs[i], arrs[i + s]
                nxt[i] = jnp.where(keep, lo, pltpu.roll(hi, s * SSM_GROUP, axis=1))
                nxt[i + s] = jnp.where(keep, pltpu.roll(lo, LANES - s * SSM_GROUP, axis=1), hi)
        arrs = nxt
        s //= 2
    return arrs


def _s5_kernel(u_lo_ref, u_hi_ref, um_ref, krow_ref, bmat_ref, cre_ref, cim_ref, ar_ref, ai_ref, wglu_ref,
               y_lo_ref, y_hi_ref, t0_ref, ug_ref, yg_ref):
    n_blocks = u_lo_ref.shape[0] // S5_BLOCK

    @pl.when(pl.program_id(0) == 0)
    def _():
        lane = lax.broadcasted_iota(I32, (SSM_GROUP, S5_LANES), 1)
        for g in range(SSM_GROUPS):
            k0 = krow_ref[g]
            for a in range(S5_BLOCK):
                blk = k0 if a == 0 else jnp.where(lane >= a * SSM_GROUP,
                                                  pltpu.roll(k0, a * SSM_GROUP, axis=1), 0.0)
                t0_ref[g, a * SSM_GROUP:(a + 1) * SSM_GROUP, :] = blk.astype(BF16)

    per_tile = LANES // SSM_GROUP
    for half, uh_ref in enumerate((u_lo_ref, u_hi_ref)):
        for t in range(S5_BLOCK // per_tile):
            words = [pltpu.bitcast(uh_ref[pl.ds(t * per_tile + k, n_blocks, stride=S5_BLOCK), :].astype(BF16),
                                   jnp.uint32) for k in range(per_tile)]
            for m, w in enumerate(_chunk_transpose(words)):
                ug_ref[half * per_tile + m, :, t * LANES:(t + 1) * LANES] = pltpu.bitcast(w, BF16)

    re, im, re0, im0 = [], [], [], []
    for p in range(SSM_GROUPS // 2):
        g0, g1 = 2 * p, 2 * p + 1
        v = _dot(ug_ref[g0], bmat_ref[g0]) + _dot(ug_ref[g1], bmat_ref[g1])
        v0 = (_dot(um_ref[g0].astype(BF16), bmat_ref[g0]) + _dot(um_ref[g1].astype(BF16), bmat_ref[g1]))[0:1]
        re.append(v[:, :LANES])
        im.append(v[:, LANES:])
        re0.append(v0[:, :LANES])
        im0.append(v0[:, LANES:])
    re, im, re0, im0 = (jnp.concatenate(parts, axis=1) for parts in (re, im, re0, im0))

    row = lax.broadcasted_iota(I32, re.shape, 0)
    ar, ai = ar_ref[0:1], ai_ref[0:1]
    re = re + jnp.where(row == 0, ar * re0 - ai * im0, 0.0)
    im = im + jnp.where(row == 0, ar * im0 + ai * re0, 0.0)
    d = 1
    while d < SCAN_ROWS:
        ar, ai = ar_ref[d - 1:d], ai_ref[d - 1:d]
        inside = row % SCAN_ROWS >= d
        sre = jnp.where(inside, pltpu.roll(re, d, axis=0), 0.0)
        sim = jnp.where(inside, pltpu.roll(im, d, axis=0), 0.0)
        re, im = re + ar * sre - ai * sim, im + ar * sim + ai * sre
        d *= 2
    ar, ai = ar_ref[...], ai_ref[...]
    re_tiles, im_tiles = [re[:SCAN_ROWS]], [im[:SCAN_ROWS]]
    for r in range(1, n_blocks // SCAN_ROWS):
        cre_ = re_tiles[-1][SCAN_ROWS - 1:SCAN_ROWS]
        cim_ = im_tiles[-1][SCAN_ROWS - 1:SCAN_ROWS]
        rs = slice(r * SCAN_ROWS, (r + 1) * SCAN_ROWS)
        re_tiles.append(re[rs] + ar * cre_ - ai * cim_)
        im_tiles.append(im[rs] + ar * cim_ + ai * cre_)
    re = jnp.concatenate(re_tiles, axis=0)
    im = jnp.concatenate(im_tiles, axis=0)
    pre = jnp.where(row == 0, re0, pltpu.roll(re, 1, axis=0)).astype(BF16)
    pim = jnp.where(row == 0, im0, pltpu.roll(im, 1, axis=0)).astype(BF16)

    for g in range(SSM_GROUPS):
        ps = slice((g // 2) * LANES, (g // 2 + 1) * LANES)
        yg_ref[g] = (_dot(ug_ref[g], t0_ref[g]) + _dot_t(pre[:, ps], cre_ref[g])
                     + _dot_t(pim[:, ps], cim_ref[g]))

    wglu = wglu_ref[...]
    for t in range(S5_BLOCK // per_tile):
        ts = slice(t * LANES, (t + 1) * LANES)
        halves = [_chunk_transpose([yg_ref[half * per_tile + m, :, ts] for m in range(per_tile)])
                  for half in range(SSM_GROUPS // per_tile)]
        for k in range(per_tile):
            y = jnp.concatenate([h[k] for h in halves], axis=1)
            y = jax.nn.gelu(y, approximate=True)
            y = y * _sigmoid(_dot(y.astype(BF16), wglu))
            i = t * per_tile + k
            y_lo_ref[pl.ds(i, n_blocks, stride=S5_BLOCK), :] = y[:, :LANES]
            y_hi_ref[pl.ds(i, n_blocks, stride=S5_BLOCK), :] = y[:, LANES:]


def _s5_call(u, um, krow, bmat, cre, cim, ar, ai, wglu):
    bsz, seq, _ = u.shape
    n_blocks = seq // S5_BLOCK
    const = lambda a: pl.BlockSpec(a.shape, lambda b: (0,) * a.ndim)
    return pl.pallas_call(
        _s5_kernel,
        grid=(bsz,),
        in_specs=[pl.BlockSpec((None, seq, LANES), lambda b: (b, 0, 0)),
                  pl.BlockSpec((None, seq, LANES), lambda b: (b, 0, 1)),
                  const(um), const(krow), const(bmat), const(cre), const(cim), const(ar), const(ai),
                  const(wglu)],
        out_specs=(pl.BlockSpec((None, seq, LANES), lambda b: (b, 0, 0)),
                   pl.BlockSpec((None, seq, LANES), lambda b: (b, 0, 0))),
        out_shape=(jax.ShapeDtypeStruct((bsz, seq, LANES), F32),
                   jax.ShapeDtypeStruct((bsz, seq, LANES), F32)),
        scratch_shapes=[pltpu.VMEM((SSM_GROUPS, S5_LANES, S5_LANES), BF16),
                        pltpu.VMEM((SSM_GROUPS, n_blocks, S5_LANES), BF16),
                        pltpu.VMEM((SSM_GROUPS, n_blocks, S5_LANES), F32)],
        compiler_params=pltpu.CompilerParams(
            dimension_semantics=("arbitrary",), vmem_limit_bytes=VMEM_LIMIT),
        name="s5_call",
    )(u, u, um, krow, bmat, cre, cim, ar, ai, wglu)


def _s5_operators(lam_re, lam_im, log_dt, b_re, b_im, c_re, c_im, d_skip):
    n_groups = lam_re.shape[0]
    lam = lax.complex(lam_re, lam_im)
    lam_dt = lam * jnp.exp(log_dt)[:, None]
    lam_bar = jnp.exp(lam_dt)
    b_bar = ((lam_bar - 1.0) / lam)[..., None] * lax.complex(b_re, b_im)
    c = lax.complex(c_re, c_im)
    tau = jnp.arange(S5_BLOCK + 1, dtype=F32)
    pows = jnp.exp(lam_dt[None] * tau[:, None, None])
    kern = jnp.real(jnp.einsum('ghp,tgp,gpk->gkth', c, pows[:S5_BLOCK], b_bar))
    skip = (jnp.eye(SSM_GROUP, dtype=F32)[None, :, None, :] * d_skip[:, None, None, :]
            * (tau[:S5_BLOCK] == 0).astype(F32)[None, None, :, None])
    krow = (kern + skip).reshape(n_groups, SSM_GROUP, S5_LANES)
    even = (jnp.arange(n_groups) % 2 == 0)[:, None, None]
    pair_pad = lambda m: jnp.concatenate([jnp.where(even, m, 0.0), jnp.where(even, 0.0, m)], axis=-1)
    bm = (pows[S5_BLOCK - 1 - jnp.arange(S5_BLOCK)].transpose(1, 0, 2)[:, :, None, :]
          * b_bar.transpose(0, 2, 1)[:, None, :, :]).reshape(n_groups, S5_LANES, SSM_STATE)
    bmat = jnp.concatenate([pair_pad(jnp.real(bm)), pair_pad(jnp.imag(bm))], axis=-1)
    cm = (pows[1:].transpose(1, 0, 2)[:, :, None, :] * c[:, None, :, :]).reshape(n_groups, S5_LANES, SSM_STATE)
    cre, cim = pair_pad(jnp.real(cm)), pair_pad(-jnp.imag(cm))
    step = S5_BLOCK * (1.0 + jnp.arange(SCAN_ROWS, dtype=F32))
    adec = jnp.exp(lam_dt[None, :, :] * step[:, None, None]).reshape(SCAN_ROWS, n_groups * SSM_STATE)
    return krow, bmat.astype(BF16), cre.astype(BF16), cim.astype(BF16), jnp.real(adec), jnp.imag(adec)


def _first_hit(values, target):
    hits, taken = [], None
    for v in values:
        hit = (v >= target) if taken is None else jnp.logical_and(v >= target, jnp.logical_not(taken))
        taken = hit if taken is None else jnp.logical_or(taken, hit)
        hits.append(hit)
    return hits


def _stack_rows(rows, n_rows):
    idx = lax.broadcasted_iota(I32, (n_rows, rows[0].shape[1]), 0)
    out = jnp.zeros((n_rows, rows[0].shape[1]), F32)
    for k, r in enumerate(rows):
        out = jnp.where(idx == k, r, out)
    return out


def _project(rs, x_ref, ys_lo_ref, ys_hi_ref, yr_ref, wout_ref, h_ref):
    ys = jnp.concatenate([ys_lo_ref[rs, :], ys_hi_ref[rs, :]], axis=1).astype(BF16)
    h = x_ref[rs, :] + _dot(ys, wout_ref[:SSM_WIDTH, :]) + _dot(yr_ref[rs, :], wout_ref[SSM_WIDTH:, :])
    h_ref[rs, :] = h.astype(BF16)
    return h


def _route(lt, tri_ref):
    gl = [lt[g:g + 1, :] for g in range(N_GROUPS)]
    gmax = functools.reduce(jnp.maximum, gl)
    g_w = 1.0 / functools.reduce(lambda a, b: a + b, [jnp.exp(l - gmax) for l in gl])
    sel = _first_hit(gl, gmax)
    ev = []
    for e in range(EXPERTS_PER_GROUP):
        acc = jnp.zeros_like(gmax)
        for g in range(N_GROUPS):
            k = N_GROUPS + g * EXPERTS_PER_GROUP + e
            acc = jnp.where(sel[g], lt[k:k + 1, :], acc)
        ev.append(acc)
    m1 = functools.reduce(jnp.maximum, ev)
    first = _first_hit(ev, m1)
    rest = [jnp.where(f, -jnp.inf, v) for f, v in zip(first, ev)]
    m2 = functools.reduce(jnp.maximum, rest)
    second = _first_hit(rest, m2)
    e2 = jnp.exp(m2 - m1)
    w1 = g_w / (1.0 + e2)
    w2 = e2 * w1
    combine = [jnp.where(f, w1, 0.0) + jnp.where(s, w2, 0.0) for f, s in zip(first, second)]

    sel_f = [jnp.where(s, 1.0, 0.0) for s in sel]
    incl = _dot(_stack_rows(sel_f, 8).astype(BF16), tri_ref[...])
    dest = jnp.zeros_like(gmax)
    seg_start = jnp.zeros((1, 1), F32)
    counts = []
    for g in range(N_GROUPS):
        run = incl[g:g + 1, :]
        cnt = run[:, TILE - 1:TILE]
        counts.append(cnt)
        dest = dest + sel_f[g] * (seg_start + run - 1.0)
        seg_start = seg_start + PIECE * jnp.floor((cnt + (PIECE - 1.0)) * (1.0 / PIECE))
    return combine, dest, counts


def _proj_kernel(x_ref, ys_lo_ref, ys_hi_ref, yr_ref, wout32_ref, g_ref, wr_ref, br_ref,
                 tri_ref, h_ref, stage_ref, dest_ref, cnt_ref, t_scr, lt_scr, wout_ref):
    step = pl.program_id(0)
    cur = step % 2
    prev = 1 - cur

    @pl.when(step == 0)
    def _():
        t_scr[prev] = jnp.zeros(t_scr.shape[1:], BF16)
        lt_scr[prev] = jnp.zeros(lt_scr.shape[1:], F32)
        wout_ref[...] = wout32_ref[...].astype(BF16)

    chunks = [slice(b * PROJ_ROWS, (b + 1) * PROJ_ROWS) for b in range(TILE // PROJ_ROWS)]
    proj_refs = (x_ref, ys_lo_ref, ys_hi_ref, yr_ref, wout_ref, h_ref)
    h_parts = [_project(chunks[0], *proj_refs)]

    combine, dest, counts = _route(lt_scr[prev], tri_ref)
    dest_ref[...] = _stack_rows([dest], 8)
    cnt_ref[...] = _stack_rows([c + jnp.zeros((1, LANES), F32) for c in counts], 8)
    perm = jnp.where(lax.broadcasted_iota(I32, (SORTED_ROWS, TILE), 0) == dest.astype(I32),
                     1.0, 0.0).astype(BF16)
    c_hi = [c.astype(BF16).astype(F32) for c in combine]
    c_lo = [c - hi for c, hi in zip(combine, c_hi)]
    cw = _stack_rows(c_hi + c_lo, LANES).T.astype(BF16)

    h_parts += [_project(rs, *proj_refs) for rs in chunks[1:]]

    t_parts = []
    for rs, h in zip(chunks, h_parts):
        t = _rms_norm(h, g_ref[...])
        t_hi = t.astype(BF16)
        t_parts.append((rs, t_hi, (t - t_hi.astype(F32)).astype(BF16)))

    def logits_hi(rs, t_hi):
        both = _dot_t(wr_ref[...], t_hi)
        lt_scr[cur, :, rs] = both[:ROUTE_ROWS] + both[ROUTE_ROWS:] + br_ref[...]

    def logits_lo(rs, t_lo):
        lt_scr[cur, :, rs] += _dot_t(wr_ref[:ROUTE_ROWS, :], t_lo)

    router = [functools.partial(logits_hi, rs, t_hi) for rs, t_hi, _ in t_parts]
    router += [functools.partial(logits_lo, rs, t_lo) for rs, _, t_lo in t_parts]
    n_col = D_MODEL // PROJ_ROWS
    for c in range(n_col):
        cs = slice(c * PROJ_ROWS, (c + 1) * PROJ_ROWS)
        stage_ref[:, cs] = _dot(perm, t_scr[prev, :, cs]).astype(BF16)
        if c < len(router):
            router[c]()
    for rest in router[n_col:]:
        rest()
    stage_ref[:, D_MODEL:] = _dot(perm, cw).astype(BF16)
    for rs, t_hi, _ in t_parts:
        t_scr[cur, rs, :] = t_hi


def _proj_call(x, ys_lo, ys_hi, yr, wout, g, wr, br, tri):
    n_tok = x.shape[0]
    n_tiles = n_tok // TILE
    const = lambda *shape: pl.BlockSpec(shape, lambda i: (0,) * len(shape))
    front = lambda i: jnp.minimum(i, n_tiles - 1)
    back = lambda i: jnp.maximum(i - 1, 0)
    rows = lambda width: pl.BlockSpec((TILE, width), lambda i: (front(i), 0))
    return pl.pallas_call(
        _proj_kernel,
        grid=(n_tiles + 1,),
        in_specs=[
            rows(D_MODEL), rows(LANES), rows(LANES), rows(RET_WIDTH),
            pl.BlockSpec((D_MODEL, D_MODEL), lambda i: (0, 0), pipeline_mode=pl.Buffered(1)), const(1, D_MODEL),
            const(2 * ROUTE_ROWS, D_MODEL), const(ROUTE_ROWS, 1), const(TILE, TILE),
        ],
        out_specs=(rows(D_MODEL),
                   pl.BlockSpec((SORTED_ROWS, EXT_WIDTH), lambda i: (back(i), 0)),
                   pl.BlockSpec((None, 8, TILE), lambda i: (back(i), 0, 0)),
                   pl.BlockSpec((None, 8, LANES), lambda i: (back(i), 0, 0))),
        scratch_shapes=[pltpu.VMEM((2, TILE, D_MODEL), BF16), pltpu.VMEM((2, ROUTE_ROWS, TILE), F32),
                        pltpu.VMEM((D_MODEL, D_MODEL), BF16)],
        out_shape=(jax.ShapeDtypeStruct((n_tok, D_MODEL), BF16),
                   jax.ShapeDtypeStruct((n_tiles * SORTED_ROWS, EXT_WIDTH), BF16),
                   jax.ShapeDtypeStruct((n_tiles, 8, TILE), F32),
                   jax.ShapeDtypeStruct((n_tiles, 8, LANES), F32)),
        compiler_params=pltpu.CompilerParams(
            dimension_semantics=("arbitrary",), vmem_limit_bytes=VMEM_LIMIT),
        name="proj_call",
    )(x, ys_lo, ys_hi, yr, wout, g, wr, br, tri)


def _sort_tables(cnt, n_steps):
    n_tiles = cnt.shape[0]
    npc = (cnt + PIECE - 1) // PIECE
    seg = jnp.cumsum(npc, axis=1) - npc
    before = jnp.cumsum(npc, axis=0) - npc
    n_tile_g = (jnp.sum(npc, axis=0) + TILE_PIECES - 1) // TILE_PIECES
    t_off = jnp.cumsum(n_tile_g) - n_tile_g
    j = jnp.arange(SORTED_PIECES, dtype=I32)[None, :, None]
    in_g = jnp.logical_and(j >= seg[:, None, :], j < (seg + npc)[:, None, :])
    pos = jnp.sum(jnp.where(in_g, TILE_PIECES * t_off[None, None, :] + before[:, None, :] + j - seg[:, None, :], 0),
                  axis=-1)
    valid = jnp.any(in_g, axis=-1)
    stage_piece = jnp.arange(n_tiles * SORTED_PIECES, dtype=I32).reshape(n_tiles, SORTED_PIECES)
    n_slots = n_steps * TILE_PIECES
    src = jnp.zeros((n_slots,), I32).at[jnp.where(valid, pos, n_slots).reshape(-1)].set(
        stage_piece.reshape(-1), mode='drop')
    steps = jnp.arange(n_steps, dtype=I32)
    g_step = jnp.minimum(jnp.sum(steps[:, None] >= (t_off + n_tile_g)[None, :], axis=1), N_GROUPS - 1)
    n_live = jnp.sum(n_tile_g).reshape(1)
    back = jnp.where(valid, pos, 0).reshape(-1)
    return src, g_step.astype(I32), n_live.astype(I32), back.astype(I32)


def _piece_copy(src_ref, piece, buf_ref, slot, j, sem_ref):
    start = piece * PIECE if isinstance(piece, int) else pl.multiple_of(piece * PIECE, PIECE)
    return pltpu.make_async_copy(src_ref.at[pl.ds(start, PIECE)],
                                 buf_ref.at[slot, pl.ds(j * PIECE, PIECE)], sem_ref.at[slot])


def _fetch_pieces(table_ref, src_ref, buf_ref, sem_ref, row, slot, n_pieces):
    for j in range(n_pieces):
        _piece_copy(src_ref, table_ref[row * n_pieces + j], buf_ref, slot, j, sem_ref).start()


def _wait_pieces(src_ref, buf_ref, sem_ref, slot, n_pieces):
    for j in range(n_pieces):
        _piece_copy(src_ref, 0, buf_ref, slot, j, sem_ref).wait()


def _gather_pieces(table_ref, src_ref, buf_ref, sem_ref, n_pieces):
    step = pl.program_id(0)
    last = pl.num_programs(0) - 1
    args = (src_ref, buf_ref, sem_ref)

    @pl.when(step == 0)
    def _():
        _fetch_pieces(table_ref, *args, step, 0, n_pieces)

    _wait_pieces(*args, step % 2, n_pieces)

    def fetch_next():
        _fetch_pieces(table_ref, *args, jnp.minimum(step + 1, last), (step + 1) % 2, n_pieces)

    def drain():
        @pl.when(step == last)
        def _():
            _wait_pieces(*args, (step + 1) % 2, n_pieces)

    return fetch_next, drain


def _moe_kernel(src_ref, gstep_ref, nlive_ref, stage_ref, wg_ref, wu_ref, wd_ref, y_ref,
                buf_ref, sem_ref, wgb_ref, wub_ref, wdb_ref):
    step = pl.program_id(0)
    fetch_next, drain = _gather_pieces(src_ref, stage_ref, buf_ref, sem_ref, TILE_PIECES)

    @pl.when(jnp.logical_or(step == 0, gstep_ref[step] != gstep_ref[jnp.maximum(step - 1, 0)]))
    def _():
        for e in range(EXPERTS_PER_GROUP):
            wgb_ref[e] = wg_ref[e].astype(BF16)
            wub_ref[e] = wu_ref[e].astype(BF16)
            wdb_ref[e * EXPERT_FF:(e + 1) * EXPERT_FF, :] = wd_ref[e].astype(BF16)

    @pl.when(step < nlive_ref[0])
    def _():
        slot = step % 2
        fetch_next()

        def up(rs):
            t = buf_ref[slot, rs, :D_MODEL]
            return [(_dot(t, wgb_ref[e]), _dot(t, wub_ref[e])) for e in range(EXPERTS_PER_GROUP)]

        def down(rs, hidden):
            cw = buf_ref[slot, rs, D_MODEL:].astype(F32)
            acts = []
            for e, (hg, hu) in enumerate(hidden):
                c = cw[:, e:e + 1] + cw[:, EXPERTS_PER_GROUP + e:EXPERTS_PER_GROUP + e + 1]
                acts.append((hg * _sigmoid(hg) * hu * c).astype(BF16))
            y_ref[rs, :] = _dot(jnp.concatenate(acts, axis=1), wdb_ref[...]).astype(BF16)

        halves = [slice(b * PROJ_ROWS, (b + 1) * PROJ_ROWS) for b in range(TILE // PROJ_ROWS)]
        hidden = [up(rs) for rs in halves]
        for rs, hid in zip(halves, hidden):
            down(rs, hid)

    @pl.when(step >= nlive_ref[0])
    def _():
        fetch_next()
        y_ref[...] = jnp.zeros_like(y_ref)

    drain()


def _moe_call(src, g_step, n_live, stage, wg, wu, wd, n_steps):
    grp = lambda shape: pl.BlockSpec((EXPERTS_PER_GROUP,) + shape, lambda s, src, gs, nl: (gs[s], 0, 0))
    return pl.pallas_call(
        _moe_kernel,
        grid_spec=pltpu.PrefetchScalarGridSpec(
            num_scalar_prefetch=3,
            grid=(n_steps,),
            in_specs=[pl.BlockSpec(memory_space=pl.ANY),
                      grp((D_MODEL, EXPERT_FF)), grp((D_MODEL, EXPERT_FF)), grp((EXPERT_FF, D_MODEL))],
            out_specs=pl.BlockSpec((TILE, D_MODEL), lambda s, src, gs, nl: (s, 0)),
            scratch_shapes=[pltpu.VMEM((2, TILE, EXT_WIDTH), BF16), pltpu.SemaphoreType.DMA((2,)),
                            pltpu.VMEM((EXPERTS_PER_GROUP, D_MODEL, EXPERT_FF), BF16),
                            pltpu.VMEM((EXPERTS_PER_GROUP, D_MODEL, EXPERT_FF), BF16),
                            pltpu.VMEM((EXPERTS_PER_GROUP * EXPERT_FF, D_MODEL), BF16)],
        ),
        out_shape=jax.ShapeDtypeStruct((n_steps * TILE, D_MODEL), BF16),
        compiler_params=pltpu.CompilerParams(
            dimension_semantics=("arbitrary",), vmem_limit_bytes=VMEM_LIMIT),
        name="moe_call",
    )(src, g_step, n_live, stage, wg, wu, wd)


def _final_kernel(back_ref, ysort_ref, h_ref, dest_ref, g_ref, o_ref, buf_ref, sem_ref):
    step = pl.program_id(0)
    fetch_next, drain = _gather_pieces(back_ref, ysort_ref, buf_ref, sem_ref, SORTED_PIECES)
    fetch_next()
    dest = _stack_rows([dest_ref[0:1, :]], LANES).T[:, 0:1].astype(I32)
    unperm = jnp.where(lax.broadcasted_iota(I32, (TILE, SORTED_ROWS), 1) == dest, 1.0, 0.0).astype(BF16)
    rows = buf_ref[step % 2]
    for b in range(TILE // PROJ_ROWS):
        rs = slice(b * PROJ_ROWS, (b + 1) * PROJ_ROWS)
        o_ref[rs, :] = _rms_norm(h_ref[rs, :].astype(F32) + _dot(unperm[rs], rows), g_ref[...])
    drain()


def _final_call(back, ysort, h, dest, g):
    n_tok = h.shape[0]
    return pl.pallas_call(
        _final_kernel,
        grid_spec=pltpu.PrefetchScalarGridSpec(
            num_scalar_prefetch=1,
            grid=(n_tok // TILE,),
            in_specs=[pl.BlockSpec(memory_space=pl.ANY),
                      pl.BlockSpec((TILE, D_MODEL), lambda i, back: (i, 0)),
                      pl.BlockSpec((None, 8, TILE), lambda i, back: (i, 0, 0)),
                      pl.BlockSpec((1, D_MODEL), lambda i, back: (0, 0))],
            out_specs=pl.BlockSpec((TILE, D_MODEL), lambda i, back: (i, 0)),
            scratch_shapes=[pltpu.VMEM((2, SORTED_ROWS, D_MODEL), BF16), pltpu.SemaphoreType.DMA((2,))],
        ),
        out_shape=jax.ShapeDtypeStruct((n_tok, D_MODEL), F32),
        compiler_params=pltpu.CompilerParams(
            dimension_semantics=("arbitrary",), vmem_limit_bytes=VMEM_LIMIT),
        name="final_call",
    )(back, ysort, h, dest, g)


def _rope_tables(length):
    pos = np.arange(length, dtype=np.float32)
    inv_freq = np.float32(ROPE_BASE) ** (-np.arange(0, HEAD_DIM, 2, dtype=np.float32) / np.float32(HEAD_DIM))
    ang = pos[:, None] * inv_freq[None, :]
    cos, sin = np.cos(ang), np.sin(ang)
    return np.concatenate([cos, cos], axis=-1), np.concatenate([-sin, sin], axis=-1)


def _retention_tables():
    f32 = np.float32
    gamma = f32(1.0) - f32(2.0) ** (f32(-5.0) - np.arange(HEADS, dtype=f32))
    log_g = np.log(gamma)[:, None, None]
    scale = f32(HEAD_DIM ** -0.5)
    idx = np.arange(RET_BLOCK)
    dist = np.abs(idx[:, None] - idx[None, :]).astype(f32)
    visible = (idx[None, :] // CHUNK) <= (idx[:, None] // CHUNK)
    mask = np.where(visible[None], np.exp(log_g * dist[None]), f32(0.0)) * scale
    ones = np.ones((1, 1, HEAD_DIM), f32)
    idx_f = idx.astype(f32)[None, :, None]
    qdec = np.exp(log_g * (idx_f + f32(1.0))) * ones
    kdec = np.exp(log_g * (f32(RET_BLOCK - 1.0) - idx_f)) * scale * ones
    bdec = np.exp(log_g * f32(RET_BLOCK)) * ones
    meta_idx = np.arange(N_META, dtype=f32)[None, :, None]
    kdec_meta = np.exp(log_g * (f32(N_META - 1.0) - meta_idx)) * scale * ones
    return tuple(a.astype(f32) for a in (mask, qdec, kdec, bdec, kdec_meta))


def kernel(x, meta_tokens, norm_mix_g, w_in, ssm_lambda_re, ssm_lambda_im, ssm_log_dt, ssm_b_re, ssm_b_im, ssm_c_re, ssm_c_im, ssm_d, w_glu, w_out, norm_ffn_g, w_router_group, b_router_group, w_router_expert, b_router_expert, w_gate, w_up, w_down, norm_final_g):
    bsz, seq, _ = x.shape
    assert seq % TILE == 0 and TILE % RET_BLOCK == 0 and RET_BLOCK % CHUNK == 0
    n_blocks = seq // S5_BLOCK
    assert n_blocks % SCAN_ROWS == 0
    n_tok = bsz * seq
    n_tiles = n_tok // TILE
    n_steps = -(-n_tiles * (TILE_PIECES + N_GROUPS - 1) // TILE_PIECES) + N_GROUPS

    cos, sin = _rope_tables(N_META + seq)
    mask, qdec, kdec, bdec, kdec_meta = _retention_tables()
    g_mix = norm_mix_g[0][None, :]
    u, y_ret, u_meta = _mixer_call(x, g_mix, w_in[0], cos[N_META:], sin[N_META:], mask, qdec, kdec, bdec,
                                   meta_tokens, cos[:N_META], sin[:N_META], kdec_meta)

    s5_ops = _s5_operators(
        ssm_lambda_re[0], ssm_lambda_im[0], ssm_log_dt[0], ssm_b_re[0], ssm_b_im[0],
        ssm_c_re[0], ssm_c_im[0], ssm_d[0])
    um = u_meta.reshape(S5_BLOCK, SSM_GROUPS, SSM_GROUP).transpose(1, 0, 2).reshape(SSM_GROUPS, 1, S5_LANES)
    um = jnp.pad(um, ((0, 0), (0, 7), (0, 0)))
    y_lo, y_hi = _s5_call(u, um, *s5_ops, w_glu[0].astype(BF16))

    w_r = jnp.concatenate(
        [w_router_group[0].T, w_router_expert[0].transpose(0, 2, 1).reshape(N_EXPERTS, D_MODEL)], axis=0)
    w_r = jnp.pad(w_r, ((0, ROUTE_ROWS - w_r.shape[0]), (0, 0)))
    b_r = jnp.concatenate([b_router_group[0], b_router_expert[0].reshape(-1)])
    b_r = jnp.pad(b_r, (0, ROUTE_ROWS - b_r.shape[0]))[:, None]
    w_r_hi = w_r.astype(BF16)
    w_r = jnp.concatenate([w_r_hi, (w_r - w_r_hi.astype(F32)).astype(BF16)], axis=0)
    tri = jnp.asarray(np.arange(TILE)[:, None] <= np.arange(TILE)[None, :], BF16)

    h, stage, dest, cnt = _proj_call(
        x.reshape(n_tok, D_MODEL), y_lo.reshape(n_tok, LANES), y_hi.reshape(n_tok, LANES),
        y_ret.reshape(n_tok, RET_WIDTH),
        w_out[0], norm_ffn_g[0][None, :], w_r, b_r, tri)
    src, g_step, n_live, back = _sort_tables(cnt[:, :N_GROUPS, 0].astype(I32), n_steps)
    y_sorted = _moe_call(src, g_step, n_live, stage, w_gate[0], w_up[0], w_down[0], n_steps)
    out = _final_call(back, y_sorted, h, dest, norm_final_g[None, :])
    return out.reshape(bsz, seq, D_MODEL)
```

```python
import functools

import jax
import jax.numpy as jnp
import numpy as np
from jax import lax
from jax.experimental import pallas as pl
from jax.experimental.pallas import tpu as pltpu

D_MODEL = 1024
N_META = 16
CHUNK = 64
EPS = 1e-6
SSM_WIDTH = 256
SSM_GROUP = 16
SSM_GROUPS = 16
SSM_STATE = 64
RET_WIDTH = 768
HEAD_DIM = 128
HEADS = 6
HEAD_PAIRS = HEADS // 2
PAIR_DIM = 2 * HEAD_DIM
ROPE_BASE = 10000.0
IN_WIDTH = SSM_WIDTH + 4 * RET_WIDTH
N_GROUPS = 4
EXPERTS_PER_GROUP = 4
N_EXPERTS = 16
EXPERT_FF = 256

S5_BLOCK = 16
S5_LANES = S5_BLOCK * SSM_GROUP
SCAN_ROWS = 8
RET_BLOCK = 256
TILE = 512
MIXER_TILE = 1024
LANES = 128
ROUTE_ROWS = 32
PROJ_ROWS = 256
PIECE = 16
TILE_PIECES = TILE // PIECE
SORTED_PIECES = TILE_PIECES + N_GROUPS
SORTED_ROWS = SORTED_PIECES * PIECE
EXT_WIDTH = D_MODEL + LANES
VMEM_LIMIT = 56 * 1024 * 1024

F32 = jnp.float32
BF16 = jnp.bfloat16
I32 = jnp.int32


def _dot(a, b):
    return jnp.dot(a, b, preferred_element_type=F32)


def _sigmoid(x):
    return 1.0 / (1.0 + jnp.exp(-x))


def _rms_norm(x, g):
    return x * lax.rsqrt(jnp.mean(x * x, axis=-1, keepdims=True) + EPS) * g


def _rope(t, cos, sin_signed):
    return t * cos + pltpu.roll(t, HEAD_DIM // 2, axis=1) * sin_signed


def _meta_state(meta_ref, g_ref, w_ref, cos_ref, sin_ref, kdec_ref, blk_ref, u_ref, r0_ref):
    a = _rms_norm(meta_ref[...], g_ref[...]).astype(BF16)
    u_ref[...] = _dot(a, w_ref[:, :SSM_WIDTH])
    k_off = SSM_WIDTH + RET_WIDTH
    v_off = SSM_WIDTH + 2 * RET_WIDTH
    cos = cos_ref[...]
    sin = sin_ref[...]
    for p in range(HEAD_PAIRS):
        k = _dot(a, w_ref[:, k_off + p * PAIR_DIM:k_off + (p + 1) * PAIR_DIM])
        v = _dot(a, w_ref[:, v_off + p * PAIR_DIM:v_off + (p + 1) * PAIR_DIM])
        k = jnp.concatenate([_rope(k[:, :HEAD_DIM], cos, sin), _rope(k[:, HEAD_DIM:], cos, sin)], axis=1)
        kd = (k * kdec_ref[p]).astype(BF16)
        r0_ref[p] = _dot_rows(kd, v.astype(BF16)) * blk_ref[...]


def _dot_rows(a, b):
    return lax.dot_general(a, b, (((0,), (0,)), ((), ())), preferred_element_type=F32)


def _mixer_kernel(x_ref, g_ref, w32_ref, cos_ref, sin_ref, mask_ref, qdec_ref, kdec_ref, bdec_ref, blk_ref,
                  meta_ref, cos_m_ref, sin_m_ref, kdec_m_ref, u_ref, y_ref, um_ref, w_ref, r0_ref, r_ref):
    first_tile = pl.program_id(1) == 0

    @pl.when(jnp.logical_and(pl.program_id(0) == 0, first_tile))
    def _():
        for c in range(0, IN_WIDTH, SSM_WIDTH):
            w_ref[:, c:c + SSM_WIDTH] = w32_ref[:, c:c + SSM_WIDTH].astype(BF16)
        _meta_state(meta_ref, g_ref, w_ref, cos_m_ref, sin_m_ref, kdec_m_ref, blk_ref, um_ref, r0_ref)

    @pl.when(first_tile)
    def _():
        r_ref[...] = r0_ref[...]

    off = SSM_WIDTH
    for b in range(MIXER_TILE // RET_BLOCK):
        bs = slice(b * RET_BLOCK, (b + 1) * RET_BLOCK)
        a = _rms_norm(x_ref[bs, :], g_ref[...]).astype(BF16)
        u_ref[bs, :] = _dot(a, w_ref[:, :SSM_WIDTH])
        q_all = _dot(a, w_ref[:, off:off + RET_WIDTH])
        k_all = _dot(a, w_ref[:, off + RET_WIDTH:off + 2 * RET_WIDTH])
        v_all = _dot(a, w_ref[:, off + 2 * RET_WIDTH:off + 3 * RET_WIDTH])
        gate = _dot(a, w_ref[:, off + 3 * RET_WIDTH:off + 4 * RET_WIDTH])
        cos = cos_ref[bs, :]
        sin = sin_ref[bs, :]
        def scores(h):
            hs = slice(h * HEAD_DIM, (h + 1) * HEAD_DIM)
            q = _rope(q_all[:, hs], cos, sin)
            k = _rope(k_all[:, hs], cos, sin)
            return q, k, _dot_t(q.astype(BF16), k.astype(BF16))

        ahead = scores(0)
        for p in range(HEAD_PAIRS):
            ps = slice(p * PAIR_DIM, (p + 1) * PAIR_DIM)
            pair = [ahead, scores(2 * p + 1)]
            if p + 1 < HEAD_PAIRS:
                ahead = scores(2 * p + 2)
            q = jnp.concatenate([pair[0][0], pair[1][0]], axis=1)
            k = jnp.concatenate([pair[0][1], pair[1][1]], axis=1)
            v = v_all[:, ps].astype(BF16)
            state = r_ref[p]
            cross = _dot((q * qdec_ref[p]).astype(BF16), state.astype(BF16))
            kv = _dot_rows((k * kdec_ref[p]).astype(BF16), v)
            r_ref[p] = state * bdec_ref[p] + kv * blk_ref[...]
            for half, (_, _, s) in enumerate(pair):
                h = 2 * p + half
                hs = slice(h * HEAD_DIM, (h + 1) * HEAD_DIM)
                ls = slice(half * HEAD_DIM, (half + 1) * HEAD_DIM)
                o = _dot((s * mask_ref[h]).astype(BF16), v[:, ls]) + cross[:, ls]
                mu = jnp.mean(o, axis=-1, keepdims=True)
                d = o - mu
                var = jnp.mean(d * d, axis=-1, keepdims=True)
                gt = gate[:, hs]
                y_ref[bs, hs] = (gt * _sigmoid(gt) * d * lax.rsqrt(var + EPS)).astype(BF16)


def _mixer_call(x, g, w_in, cos, sin, mask, qdec, kdec, bdec, blk, meta, cos_m, sin_m, kdec_m):
    bsz, seq, _ = x.shape
    const = lambda a, **kw: pl.BlockSpec(a.shape, lambda b, i: (0,) * a.ndim, **kw)
    return pl.pallas_call(
        _mixer_kernel,
        grid=(bsz, seq // MIXER_TILE),
        in_specs=[
            pl.BlockSpec((None, MIXER_TILE, D_MODEL), lambda b, i: (b, i, 0)),
            const(g),
            const(w_in, pipeline_mode=pl.Buffered(1)),
            pl.BlockSpec((MIXER_TILE, HEAD_DIM), lambda b, i: (i, 0)),
            pl.BlockSpec((MIXER_TILE, HEAD_DIM), lambda b, i: (i, 0)),
            const(mask), const(qdec), const(kdec), const(bdec), const(blk),
            const(meta), const(cos_m), const(sin_m), const(kdec_m),
        ],
        out_specs=(
            pl.BlockSpec((None, MIXER_TILE, SSM_WIDTH), lambda b, i: (b, i, 0)),
            pl.BlockSpec((None, MIXER_TILE, RET_WIDTH), lambda b, i: (b, i, 0)),
            pl.BlockSpec((N_META, SSM_WIDTH), lambda b, i: (0, 0)),
        ),
        out_shape=(jax.ShapeDtypeStruct((bsz, seq, SSM_WIDTH), F32),
                   jax.ShapeDtypeStruct((bsz, seq, RET_WIDTH), BF16),
                   jax.ShapeDtypeStruct((N_META, SSM_WIDTH), F32)),
        scratch_shapes=[pltpu.VMEM((D_MODEL, IN_WIDTH), BF16),
                        pltpu.VMEM((HEAD_PAIRS, PAIR_DIM, PAIR_DIM), F32),
                        pltpu.VMEM((HEAD_PAIRS, PAIR_DIM, PAIR_DIM), F32)],
        compiler_params=pltpu.CompilerParams(
            dimension_semantics=("arbitrary", "arbitrary"), vmem_limit_bytes=VMEM_LIMIT),
        name="mixer_call",
    )(x, g, w_in, cos, sin, mask, qdec, kdec, bdec, blk, meta, cos_m, sin_m, kdec_m)


def _dot_t(a, b):
    return lax.dot_general(a, b, (((1,), (1,)), ((), ())), preferred_element_type=F32)


def _chunk_transpose(arrs):
    n = len(arrs)
    chunk = lax.broadcasted_iota(I32, (1, LANES), 1) // SSM_GROUP
    arrs = list(arrs)
    s = n // 2
    while s:
        keep = (chunk & s) == 0
        nxt = list(arrs)
        for i in range(n):
            if i & s == 0:
                lo, hi = arrs[i], arrs[i + s]
                nxt[i] = jnp.where(keep, lo, pltpu.roll(hi, s * SSM_GROUP, axis=1))
                nxt[i + s] = jnp.where(keep, pltpu.roll(lo, LANES - s * SSM_GROUP, axis=1), hi)
        arrs = nxt
        s //= 2
    return arrs


def _s5_kernel(u_lo_ref, u_hi_ref, um_ref, krow_ref, bmat_ref, cre_ref, cim_ref, ar_ref, ai_ref, wglu_ref,
               y_lo_ref, y_hi_ref, t0_ref, ug_ref, yg_ref):
    n_blocks = u_lo_ref.shape[0] // S5_BLOCK

    @pl.when(pl.program_id(0) == 0)
    def _():
        lane = lax.broadcasted_iota(I32, (SSM_GROUP, S5_LANES), 1)
        for g in range(SSM_GROUPS):
            k0 = krow_ref[g]
            for a in range(S5_BLOCK):
                blk = k0 if a == 0 else jnp.where(lane >= a * SSM_GROUP,
                                                  pltpu.roll(k0, a * SSM_GROUP, axis=1), 0.0)
                t0_ref[g, a * SSM_GROUP:(a + 1) * SSM_GROUP, :] = blk.astype(BF16)

    per_tile = LANES // SSM_GROUP
    for half, uh_ref in enumerate((u_lo_ref, u_hi_ref)):
        for t in range(S5_BLOCK // per_tile):
            words = [pltpu.bitcast(uh_ref[pl.ds(t * per_tile + k, n_blocks, stride=S5_BLOCK), :].astype(BF16),
                                   jnp.uint32) for k in range(per_tile)]
            for m, w in enumerate(_chunk_transpose(words)):
                ug_ref[half * per_tile + m, :, t * LANES:(t + 1) * LANES] = pltpu.bitcast(w, BF16)

    re, im, re0, im0 = [], [], [], []
    for p in range(SSM_GROUPS // 2):
        g0, g1 = 2 * p, 2 * p + 1
        v = _dot(ug_ref[g0], bmat_ref[g0]) + _dot(ug_ref[g1], bmat_ref[g1])
        v0 = (_dot(um_ref[g0].astype(BF16), bmat_ref[g0]) + _dot(um_ref[g1].astype(BF16), bmat_ref[g1]))[0:1]
        re.append(v[:, :LANES])
        im.append(v[:, LANES:])
        re0.append(v0[:, :LANES])
        im0.append(v0[:, LANES:])
    re, im, re0, im0 = (jnp.concatenate(parts, axis=1) for parts in (re, im, re0, im0))

    row = lax.broadcasted_iota(I32, re.shape, 0)
    ar, ai = ar_ref[0:1], ai_ref[0:1]
    re = re + jnp.where(row == 0, ar * re0 - ai * im0, 0.0)
    im = im + jnp.where(row == 0, ar * im0 + ai * re0, 0.0)
    d = 1
    while d < SCAN_ROWS:
        ar, ai = ar_ref[d - 1:d], ai_ref[d - 1:d]
        inside = row % SCAN_ROWS >= d
        sre = jnp.where(inside, pltpu.roll(re, d, axis=0), 0.0)
        sim = jnp.where(inside, pltpu.roll(im, d, axis=0), 0.0)
        re, im = re + ar * sre - ai * sim, im + ar * sim + ai * sre
        d *= 2
    ar, ai = ar_ref[...], ai_ref[...]
    re_tiles, im_tiles = [re[:SCAN_ROWS]], [im[:SCAN_ROWS]]
    for r in range(1, n_blocks // SCAN_ROWS):
        cre_ = re_tiles[-1][SCAN_ROWS - 1:SCAN_ROWS]
        cim_ = im_tiles[-1][SCAN_ROWS - 1:SCAN_ROWS]
        rs = slice(r * SCAN_ROWS, (r + 1) * SCAN_ROWS)
        re_tiles.append(re[rs] + ar * cre_ - ai * cim_)
        im_tiles.append(im[rs] + ar * cim_ + ai * cre_)
    re = jnp.concatenate(re_tiles, axis=0)
    im = jnp.concatenate(im_tiles, axis=0)
    pre = jnp.where(row == 0, re0, pltpu.roll(re, 1, axis=0)).astype(BF16)
    pim = jnp.where(row == 0, im0, pltpu.roll(im, 1, axis=0)).astype(BF16)

    for g in range(SSM_GROUPS):
        ps = slice((g // 2) * LANES, (g // 2 + 1) * LANES)
        yg_ref[g] = (_dot(ug_ref[g], t0_ref[g]) + _dot_t(pre[:, ps], cre_ref[g])
                     + _dot_t(pim[:, ps], cim_ref[g]))

    wglu = wglu_ref[...]
    for t in range(S5_BLOCK // per_tile):
        ts = slice(t * LANES, (t + 1) * LANES)
        halves = [_chunk_transpose([yg_ref[half * per_tile + m, :, ts] for m in range(per_tile)])
                  for half in range(SSM_GROUPS // per_tile)]
        for k in range(per_tile):
            y = jnp.concatenate([h[k] for h in halves], axis=1)
            y = jax.nn.gelu(y, approximate=True)
            y = y * _sigmoid(_dot(y.astype(BF16), wglu))
            i = t * per_tile + k
            y_lo_ref[pl.ds(i, n_blocks, stride=S5_BLOCK), :] = y[:, :LANES]
            y_hi_ref[pl.ds(i, n_blocks, stride=S5_BLOCK), :] = y[:, LANES:]


def _s5_call(u, um, krow, bmat, cre, cim, ar, ai, wglu):
    bsz, seq, _ = u.shape
    n_blocks = seq // S5_BLOCK
    const = lambda a: pl.BlockSpec(a.shape, lambda b: (0,) * a.ndim)
    return pl.pallas_call(
        _s5_kernel,
        grid=(bsz,),
        in_specs=[pl.BlockSpec((None, seq, LANES), lambda b: (b, 0, 0)),
                  pl.BlockSpec((None, seq, LANES), lambda b: (b, 0, 1)),
                  const(um), const(krow), const(bmat), const(cre), const(cim), const(ar), const(ai),
                  const(wglu)],
        out_specs=(pl.BlockSpec((None, seq, LANES), lambda b: (b, 0, 0)),
                   pl.BlockSpec((None, seq, LANES), lambda b: (b, 0, 0))),
        out_shape=(jax.ShapeDtypeStruct((bsz, seq, LANES), F32),
                   jax.ShapeDtypeStruct((bsz, seq, LANES), F32)),
        scratch_shapes=[pltpu.VMEM((SSM_GROUPS, S5_LANES, S5_LANES), BF16),
                        pltpu.VMEM((SSM_GROUPS, n_blocks, S5_LANES), BF16),
                        pltpu.VMEM((SSM_GROUPS, n_blocks, S5_LANES), F32)],
        compiler_params=pltpu.CompilerParams(
            dimension_semantics=("arbitrary",), vmem_limit_bytes=VMEM_LIMIT),
        name="s5_call",
    )(u, u, um, krow, bmat, cre, cim, ar, ai, wglu)


def _s5_operators(lam_re, lam_im, log_dt, b_re, b_im, c_re, c_im, d_skip):
    n_groups = lam_re.shape[0]
    lam = lax.complex(lam_re, lam_im)
    lam_dt = lam * jnp.exp(log_dt)[:, None]
    lam_bar = jnp.exp(lam_dt)
    b_bar = ((lam_bar - 1.0) / lam)[..., None] * lax.complex(b_re, b_im)
    c = lax.complex(c_re, c_im)
    tau = jnp.arange(S5_BLOCK + 1, dtype=F32)
    pows = jnp.exp(lam_dt[None] * tau[:, None, None])
    kern = jnp.real(jnp.einsum('ghp,tgp,gpk->gkth', c, pows[:S5_BLOCK], b_bar))
    skip = (jnp.eye(SSM_GROUP, dtype=F32)[None, :, None, :] * d_skip[:, None, None, :]
            * (tau[:S5_BLOCK] == 0).astype(F32)[None, None, :, None])
    krow = (kern + skip).reshape(n_groups, SSM_GROUP, S5_LANES)
    even = (jnp.arange(n_groups) % 2 == 0)[:, None, None]
    pair_pad = lambda m: jnp.concatenate([jnp.where(even, m, 0.0), jnp.where(even, 0.0, m)], axis=-1)
    bm = (pows[S5_BLOCK - 1 - jnp.arange(S5_BLOCK)].transpose(1, 0, 2)[:, :, None, :]
          * b_bar.transpose(0, 2, 1)[:, None, :, :]).reshape(n_groups, S5_LANES, SSM_STATE)
    bmat = jnp.concatenate([pair_pad(jnp.real(bm)), pair_pad(jnp.imag(bm))], axis=-1)
    cm = (pows[1:].transpose(1, 0, 2)[:, :, None, :] * c[:, None, :, :]).reshape(n_groups, S5_LANES, SSM_STATE)
    cre, cim = pair_pad(jnp.real(cm)), pair_pad(-jnp.imag(cm))
    step = S5_BLOCK * (1.0 + jnp.arange(SCAN_ROWS, dtype=F32))
    adec = jnp.exp(lam_dt[None, :, :] * step[:, None, None]).reshape(SCAN_ROWS, n_groups * SSM_STATE)
    return krow, bmat.astype(BF16), cre.astype(BF16), cim.astype(BF16), jnp.real(adec), jnp.imag(adec)


def _first_hit(values, target):
    hits, taken = [], None
    for v in values:
        hit = (v >= target) if taken is None else jnp.logical_and(v >= target, jnp.logical_not(taken))
        taken = hit if taken is None else jnp.logical_or(taken, hit)
        hits.append(hit)
    return hits


def _stack_rows(rows, n_rows):
    idx = lax.broadcasted_iota(I32, (n_rows, rows[0].shape[1]), 0)
    out = jnp.zeros((n_rows, rows[0].shape[1]), F32)
    for k, r in enumerate(rows):
        out = jnp.where(idx == k, r, out)
    return out


def _project(rs, x_ref, ys_lo_ref, ys_hi_ref, yr_ref, wout_ref, h_ref):
    ys = jnp.concatenate([ys_lo_ref[rs, :], ys_hi_ref[rs, :]], axis=1).astype(BF16)
    h = x_ref[rs, :] + _dot(ys, wout_ref[:SSM_WIDTH, :]) + _dot(yr_ref[rs, :], wout_ref[SSM_WIDTH:, :])
    h_ref[rs, :] = h.astype(BF16)
    return h


def _route(lt, tri_ref):
    gl = [lt[g:g + 1, :] for g in range(N_GROUPS)]
    gmax = functools.reduce(jnp.maximum, gl)
    g_w = 1.0 / functools.reduce(lambda a, b: a + b, [jnp.exp(l - gmax) for l in gl])
    sel = _first_hit(gl, gmax)
    ev = []
    for e in range(EXPERTS_PER_GROUP):
        acc = jnp.zeros_like(gmax)
        for g in range(N_GROUPS):
            k = N_GROUPS + g * EXPERTS_PER_GROUP + e
            acc = jnp.where(sel[g], lt[k:k + 1, :], acc)
        ev.append(acc)
    m1 = functools.reduce(jnp.maximum, ev)
    first = _first_hit(ev, m1)
    rest = [jnp.where(f, -jnp.inf, v) for f, v in zip(first, ev)]
    m2 = functools.reduce(jnp.maximum, rest)
    second = _first_hit(rest, m2)
    e2 = jnp.exp(m2 - m1)
    w1 = g_w / (1.0 + e2)
    w2 = e2 * w1
    combine = [jnp.where(f, w1, 0.0) + jnp.where(s, w2, 0.0) for f, s in zip(first, second)]

    sel_f = [jnp.where(s, 1.0, 0.0) for s in sel]
    incl = _dot(_stack_rows(sel_f, 8).astype(BF16), tri_ref[...])
    dest = jnp.zeros_like(gmax)
    seg_start = jnp.zeros((1, 1), F32)
    counts = []
    for g in range(N_GROUPS):
        run = incl[g:g + 1, :]
        cnt = run[:, TILE - 1:TILE]
        counts.append(cnt)
        dest = dest + sel_f[g] * (seg_start + run - 1.0)
        seg_start = seg_start + PIECE * jnp.floor((cnt + (PIECE - 1.0)) * (1.0 / PIECE))
    return combine, dest, counts


def _proj_kernel(x_ref, ys_lo_ref, ys_hi_ref, yr_ref, wout32_ref, g_ref, wr_ref, br_ref,
                 tri_ref, h_ref, stage_ref, dest_ref, cnt_ref, t_scr, lt_scr, wout_ref):
    step = pl.program_id(0)
    cur = step % 2
    prev = 1 - cur

    @pl.when(step == 0)
    def _():
        t_scr[prev] = jnp.zeros(t_scr.shape[1:], BF16)
        lt_scr[prev] = jnp.zeros(lt_scr.shape[1:], F32)
        wout_ref[...] = wout32_ref[...].astype(BF16)

    chunks = [slice(b * PROJ_ROWS, (b + 1) * PROJ_ROWS) for b in range(TILE // PROJ_ROWS)]
    proj_refs = (x_ref, ys_lo_ref, ys_hi_ref, yr_ref, wout_ref, h_ref)
    h_parts = [_project(chunks[0], *proj_refs)]

    combine, dest, counts = _route(lt_scr[prev], tri_ref)
    dest_ref[...] = _stack_rows([dest], 8)
    cnt_ref[...] = _stack_rows([c + jnp.zeros((1, LANES), F32) for c in counts], 8)
    perm = jnp.where(lax.broadcasted_iota(I32, (SORTED_ROWS, TILE), 0) == dest.astype(I32),
                     1.0, 0.0).astype(BF16)
    c_hi = [c.astype(BF16).astype(F32) for c in combine]
    c_lo = [c - hi for c, hi in zip(combine, c_hi)]
    cw = _stack_rows(c_hi + c_lo, LANES).T.astype(BF16)

    h_parts += [_project(rs, *proj_refs) for rs in chunks[1:]]

    t_parts = []
    for rs, h in zip(chunks, h_parts):
        t = _rms_norm(h, g_ref[...])
        t_hi = t.astype(BF16)
        t_parts.append((rs, t_hi, (t - t_hi.astype(F32)).astype(BF16)))

    def logits_hi(rs, t_hi):
        both = _dot_t(wr_ref[...], t_hi)
        lt_scr[cur, :, rs] = both[:ROUTE_ROWS] + both[ROUTE_ROWS:] + br_ref[...]

    def logits_lo(rs, t_lo):
        lt_scr[cur, :, rs] += _dot_t(wr_ref[:ROUTE_ROWS, :], t_lo)

    router = [functools.partial(logits_hi, rs, t_hi) for rs, t_hi, _ in t_parts]
    router += [functools.partial(logits_lo, rs, t_lo) for rs, _, t_lo in t_parts]
    n_col = D_MODEL // PROJ_ROWS
    for c in range(n_col):
        cs = slice(c * PROJ_ROWS, (c + 1) * PROJ_ROWS)
        stage_ref[:, cs] = _dot(perm, t_scr[prev, :, cs]).astype(BF16)
        if c < len(router):
            router[c]()
    for rest in router[n_col:]:
        rest()
    stage_ref[:, D_MODEL:] = _dot(perm, cw).astype(BF16)
    for rs, t_hi, _ in t_parts:
        t_scr[cur, rs, :] = t_hi


def _proj_call(x, ys_lo, ys_hi, yr, wout, g, wr, br, tri):
    n_tok = x.shape[0]
    n_tiles = n_tok // TILE
    const = lambda *shape: pl.BlockSpec(shape, lambda i: (0,) * len(shape))
    front = lambda i: jnp.minimum(i, n_tiles - 1)
    back = lambda i: jnp.maximum(i - 1, 0)
    rows = lambda width: pl.BlockSpec((TILE, width), lambda i: (front(i), 0))
    return pl.pallas_call(
        _proj_kernel,
        grid=(n_tiles + 1,),
        in_specs=[
            rows(D_MODEL), rows(LANES), rows(LANES), rows(RET_WIDTH),
            pl.BlockSpec((D_MODEL, D_MODEL), lambda i: (0, 0), pipeline_mode=pl.Buffered(1)), const(1, D_MODEL),
            const(2 * ROUTE_ROWS, D_MODEL), const(ROUTE_ROWS, 1), const(TILE, TILE),
        ],
        out_specs=(rows(D_MODEL),
                   pl.BlockSpec((SORTED_ROWS, EXT_WIDTH), lambda i: (back(i), 0)),
                   pl.BlockSpec((None, 8, TILE), lambda i: (back(i), 0, 0)),
                   pl.BlockSpec((None, 8, LANES), lambda i: (back(i), 0, 0))),
        scratch_shapes=[pltpu.VMEM((2, TILE, D_MODEL), BF16), pltpu.VMEM((2, ROUTE_ROWS, TILE), F32),
                        pltpu.VMEM((D_MODEL, D_MODEL), BF16)],
        out_shape=(jax.ShapeDtypeStruct((n_tok, D_MODEL), BF16),
                   jax.ShapeDtypeStruct((n_tiles * SORTED_ROWS, EXT_WIDTH), BF16),
                   jax.ShapeDtypeStruct((n_tiles, 8, TILE), F32),
                   jax.ShapeDtypeStruct((n_tiles, 8, LANES), F32)),
        compiler_params=pltpu.CompilerParams(
            dimension_semantics=("arbitrary",), vmem_limit_bytes=VMEM_LIMIT),
        name="proj_call",
    )(x, ys_lo, ys_hi, yr, wout, g, wr, br, tri)


def _sort_tables(cnt, n_steps):
    n_tiles = cnt.shape[0]
    npc = (cnt + PIECE - 1) // PIECE
    seg = jnp.cumsum(npc, axis=1) - npc
    before = jnp.cumsum(npc, axis=0) - npc
    n_tile_g = (jnp.sum(npc, axis=0) + TILE_PIECES - 1) // TILE_PIECES
    t_off = jnp.cumsum(n_tile_g) - n_tile_g
    j = jnp.arange(SORTED_PIECES, dtype=I32)[None, :, None]
    in_g = jnp.logical_and(j >= seg[:, None, :], j < (seg + npc)[:, None, :])
    pos = jnp.sum(jnp.where(in_g, TILE_PIECES * t_off[None, None, :] + before[:, None, :] + j - seg[:, None, :], 0),
                  axis=-1)
    valid = jnp.any(in_g, axis=-1)
    stage_piece = jnp.arange(n_tiles * SORTED_PIECES, dtype=I32).reshape(n_tiles, SORTED_PIECES)
    n_slots = n_steps * TILE_PIECES
    src = jnp.zeros((n_slots,), I32).at[jnp.where(valid, pos, n_slots).reshape(-1)].set(
        stage_piece.reshape(-1), mode='drop')
    steps = jnp.arange(n_steps, dtype=I32)
    g_step = jnp.minimum(jnp.sum(steps[:, None] >= (t_off + n_tile_g)[None, :], axis=1), N_GROUPS - 1)
    n_live = jnp.sum(n_tile_g).reshape(1)
    back = jnp.where(valid, pos, 0).reshape(-1)
    return src, g_step.astype(I32), n_live.astype(I32), back.astype(I32)


def _piece_copy(src_ref, piece, buf_ref, slot, j, sem_ref):
    start = piece * PIECE if isinstance(piece, int) else pl.multiple_of(piece * PIECE, PIECE)
    return pltpu.make_async_copy(src_ref.at[pl.ds(start, PIECE)],
                                 buf_ref.at[slot, pl.ds(j * PIECE, PIECE)], sem_ref.at[slot])


def _fetch_pieces(table_ref, src_ref, buf_ref, sem_ref, row, slot, n_pieces):
    for j in range(n_pieces):
        _piece_copy(src_ref, table_ref[row * n_pieces + j], buf_ref, slot, j, sem_ref).start()


def _wait_pieces(src_ref, buf_ref, sem_ref, slot, n_pieces):
    for j in range(n_pieces):
        _piece_copy(src_ref, 0, buf_ref, slot, j, sem_ref).wait()


def _gather_pieces(table_ref, src_ref, buf_ref, sem_ref, n_pieces):
    step = pl.program_id(0)
    last = pl.num_programs(0) - 1
    args = (src_ref, buf_ref, sem_ref)

    @pl.when(step == 0)
    def _():
        _fetch_pieces(table_ref, *args, step, 0, n_pieces)

    _wait_pieces(*args, step % 2, n_pieces)

    def fetch_next():
        _fetch_pieces(table_ref, *args, jnp.minimum(step + 1, last), (step + 1) % 2, n_pieces)

    def drain():
        @pl.when(step == last)
        def _():
            _wait_pieces(*args, (step + 1) % 2, n_pieces)

    return fetch_next, drain


def _moe_kernel(src_ref, gstep_ref, nlive_ref, stage_ref, wg_ref, wu_ref, wd_ref, y_ref,
                buf_ref, sem_ref, wgb_ref, wub_ref, wdb_ref):
    step = pl.program_id(0)
    fetch_next, drain = _gather_pieces(src_ref, stage_ref, buf_ref, sem_ref, TILE_PIECES)

    @pl.when(jnp.logical_or(step == 0, gstep_ref[step] != gstep_ref[jnp.maximum(step - 1, 0)]))
    def _():
        for e in range(EXPERTS_PER_GROUP):
            wgb_ref[e] = wg_ref[e].astype(BF16)
            wub_ref[e] = wu_ref[e].astype(BF16)
            wdb_ref[e * EXPERT_FF:(e + 1) * EXPERT_FF, :] = wd_ref[e].astype(BF16)

    @pl.when(step < nlive_ref[0])
    def _():
        slot = step % 2
        fetch_next()

        def up(rs):
            t = buf_ref[slot, rs, :D_MODEL]
            return [(_dot(t, wgb_ref[e]), _dot(t, wub_ref[e])) for e in range(EXPERTS_PER_GROUP)]

        def down(rs, hidden):
            cw = buf_ref[slot, rs, D_MODEL:].astype(F32)
            acts = []
            for e, (hg, hu) in enumerate(hidden):
                c = cw[:, e:e + 1] + cw[:, EXPERTS_PER_GROUP + e:EXPERTS_PER_GROUP + e + 1]
                acts.append((hg * _sigmoid(hg) * hu * c).astype(BF16))
            y_ref[rs, :] = _dot(jnp.concatenate(acts, axis=1), wdb_ref[...]).astype(BF16)

        halves = [slice(b * PROJ_ROWS, (b + 1) * PROJ_ROWS) for b in range(TILE // PROJ_ROWS)]
        hidden = [up(rs) for rs in halves]
        for rs, hid in zip(halves, hidden):
            down(rs, hid)

    @pl.when(step >= nlive_ref[0])
    def _():
        fetch_next()
        y_ref[...] = jnp.zeros_like(y_ref)

    drain()


def _moe_call(src, g_step, n_live, stage, wg, wu, wd, n_steps):
    grp = lambda shape: pl.BlockSpec((EXPERTS_PER_GROUP,) + shape, lambda s, src, gs, nl: (gs[s], 0, 0))
    return pl.pallas_call(
        _moe_kernel,
        grid_spec=pltpu.PrefetchScalarGridSpec(
            num_scalar_prefetch=3,
            grid=(n_steps,),
            in_specs=[pl.BlockSpec(memory_space=pl.ANY),
                      grp((D_MODEL, EXPERT_FF)), grp((D_MODEL, EXPERT_FF)), grp((EXPERT_FF, D_MODEL))],
            out_specs=pl.BlockSpec((TILE, D_MODEL), lambda s, src, gs, nl: (s, 0)),
            scratch_shapes=[pltpu.VMEM((2, TILE, EXT_WIDTH), BF16), pltpu.SemaphoreType.DMA((2,)),
                            pltpu.VMEM((EXPERTS_PER_GROUP, D_MODEL, EXPERT_FF), BF16),
                            pltpu.VMEM((EXPERTS_PER_GROUP, D_MODEL, EXPERT_FF), BF16),
                            pltpu.VMEM((EXPERTS_PER_GROUP * EXPERT_FF, D_MODEL), BF16)],
        ),
        out_shape=jax.ShapeDtypeStruct((n_steps * TILE, D_MODEL), BF16),
        compiler_params=pltpu.CompilerParams(
            dimension_semantics=("arbitrary",), vmem_limit_bytes=VMEM_LIMIT),
        name="moe_call",
    )(src, g_step, n_live, stage, wg, wu, wd)


def _final_kernel(back_ref, ysort_ref, h_ref, dest_ref, g_ref, o_ref, buf_ref, sem_ref):
    step = pl.program_id(0)
    fetch_next, drain = _gather_pieces(back_ref, ysort_ref, buf_ref, sem_ref, SORTED_PIECES)
    fetch_next()
    dest = _stack_rows([dest_ref[0:1, :]], LANES).T[:, 0:1].astype(I32)
    unperm = jnp.where(lax.broadcasted_iota(I32, (TILE, SORTED_ROWS), 1) == dest, 1.0, 0.0).astype(BF16)
    rows = buf_ref[step % 2]
    for b in range(TILE // PROJ_ROWS):
        rs = slice(b * PROJ_ROWS, (b + 1) * PROJ_ROWS)
        o_ref[rs, :] = _rms_norm(h_ref[rs, :].astype(F32) + _dot(unperm[rs], rows), g_ref[...])
    drain()


def _final_call(back, ysort, h, dest, g):
    n_tok = h.shape[0]
    return pl.pallas_call(
        _final_kernel,
        grid_spec=pltpu.PrefetchScalarGridSpec(
            num_scalar_prefetch=1,
            grid=(n_tok // TILE,),
            in_specs=[pl.BlockSpec(memory_space=pl.ANY),
                      pl.BlockSpec((TILE, D_MODEL), lambda i, back: (i, 0)),
                      pl.BlockSpec((None, 8, TILE), lambda i, back: (i, 0, 0)),
                      pl.BlockSpec((1, D_MODEL), lambda i, back: (0, 0))],
            out_specs=pl.BlockSpec((TILE, D_MODEL), lambda i, back: (i, 0)),
            scratch_shapes=[pltpu.VMEM((2, SORTED_ROWS, D_MODEL), BF16), pltpu.SemaphoreType.DMA((2,))],
        ),
        out_shape=jax.ShapeDtypeStruct((n_tok, D_MODEL), F32),
        compiler_params=pltpu.CompilerParams(
            dimension_semantics=("arbitrary",), vmem_limit_bytes=VMEM_LIMIT),
        name="final_call",
    )(back, ysort, h, dest, g)


def _rope_tables(length):
    pos = np.arange(length, dtype=np.float32)
    inv_freq = np.float32(ROPE_BASE) ** (-np.arange(0, HEAD_DIM, 2, dtype=np.float32) / np.float32(HEAD_DIM))
    ang = pos[:, None] * inv_freq[None, :]
    cos, sin = np.cos(ang), np.sin(ang)
    return np.concatenate([cos, cos], axis=-1), np.concatenate([-sin, sin], axis=-1)


def _retention_tables():
    f32 = np.float32
    gamma = f32(1.0) - f32(2.0) ** (f32(-5.0) - np.arange(HEADS, dtype=f32))
    log_g = np.log(gamma)[:, None, None]
    scale = f32(HEAD_DIM ** -0.5)
    idx = np.arange(RET_BLOCK)
    dist = np.abs(idx[:, None] - idx[None, :]).astype(f32)
    visible = (idx[None, :] // CHUNK) <= (idx[:, None] // CHUNK)
    mask = np.where(visible[None], np.exp(log_g * dist[None]), f32(0.0)) * scale
    ones = np.ones((1, 1, HEAD_DIM), f32)
    idx_f = idx.astype(f32)[None, :, None]
    pair = lambda a: a.reshape(HEAD_PAIRS, 2, a.shape[1], HEAD_DIM).transpose(0, 2, 1, 3).reshape(
        HEAD_PAIRS, a.shape[1], PAIR_DIM)
    qdec = pair(np.exp(log_g * (idx_f + f32(1.0))) * ones)
    kdec = pair(np.exp(log_g * (f32(RET_BLOCK - 1.0) - idx_f)) * scale * ones)
    meta_idx = np.arange(N_META, dtype=f32)[None, :, None]
    kdec_meta = pair(np.exp(log_g * (f32(N_META - 1.0) - meta_idx)) * scale * ones)
    blk = np.kron(np.eye(2, dtype=f32), np.ones((HEAD_DIM, HEAD_DIM), f32))
    bdec = np.exp(log_g * f32(RET_BLOCK)).reshape(HEAD_PAIRS, 2)
    bdec = np.stack([np.kron(np.diag(b), np.ones((HEAD_DIM, HEAD_DIM), f32)) for b in bdec])
    return tuple(a.astype(f32) for a in (mask, qdec, kdec, bdec, blk, kdec_meta))


def kernel(x, meta_tokens, norm_mix_g, w_in, ssm_lambda_re, ssm_lambda_im, ssm_log_dt, ssm_b_re, ssm_b_im, ssm_c_re, ssm_c_im, ssm_d, w_glu, w_out, norm_ffn_g, w_router_group, b_router_group, w_router_expert, b_router_expert, w_gate, w_up, w_down, norm_final_g):
    bsz, seq, _ = x.shape
    assert seq % TILE == 0 and seq % MIXER_TILE == 0 and MIXER_TILE % RET_BLOCK == 0 and RET_BLOCK % CHUNK == 0
    n_blocks = seq // S5_BLOCK
    assert n_blocks % SCAN_ROWS == 0
    n_tok = bsz * seq
    n_tiles = n_tok // TILE
    n_steps = -(-n_tiles * (TILE_PIECES + N_GROUPS - 1) // TILE_PIECES) + N_GROUPS

    cos, sin = _rope_tables(N_META + seq)
    mask, qdec, kdec, bdec, blk, kdec_meta = _retention_tables()
    g_mix = norm_mix_g[0][None, :]
    u, y_ret, u_meta = _mixer_call(x, g_mix, w_in[0], cos[N_META:], sin[N_META:], mask, qdec, kdec, bdec, blk,
                                   meta_tokens, cos[:N_META], sin[:N_META], kdec_meta)

    s5_ops = _s5_operators(
        ssm_lambda_re[0], ssm_lambda_im[0], ssm_log_dt[0], ssm_b_re[0], ssm_b_im[0],
        ssm_c_re[0], ssm_c_im[0], ssm_d[0])
    um = u_meta.reshape(S5_BLOCK, SSM_GROUPS, SSM_GROUP).transpose(1, 0, 2).reshape(SSM_GROUPS, 1, S5_LANES)
    um = jnp.pad(um, ((0, 0), (0, 7), (0, 0)))
    y_lo, y_hi = _s5_call(u, um, *s5_ops, w_glu[0].astype(BF16))

    w_r = jnp.concatenate(
        [w_router_group[0].T, w_router_expert[0].transpose(0, 2, 1).reshape(N_EXPERTS, D_MODEL)], axis=0)
    w_r = jnp.pad(w_r, ((0, ROUTE_ROWS - w_r.shape[0]), (0, 0)))
    b_r = jnp.concatenate([b_router_group[0], b_router_expert[0].reshape(-1)])
    b_r = jnp.pad(b_r, (0, ROUTE_ROWS - b_r.shape[0]))[:, None]
    w_r_hi = w_r.astype(BF16)
    w_r = jnp.concatenate([w_r_hi, (w_r - w_r_hi.astype(F32)).astype(BF16)], axis=0)
    tri = jnp.asarray(np.arange(TILE)[:, None] <= np.arange(TILE)[None, :], BF16)

    h, stage, dest, cnt = _proj_call(
        x.reshape(n_tok, D_MODEL), y_lo.reshape(n_tok, LANES), y_hi.reshape(n_tok, LANES),
        y_ret.reshape(n_tok, RET_WIDTH),
        w_out[0], norm_ffn_g[0][None, :], w_r, b_r, tri)
    src, g_step, n_live, back = _sort_tables(cnt[:, :N_GROUPS, 0].astype(I32), n_steps)
    y_sorted = _moe_call(src, g_step, n_live, stage, w_gate[0], w_up[0], w_down[0], n_steps)
    out = _final_call(back, y_sorted, h, dest, norm_final_g[None, :])
    return out.reshape(bsz, seq, D_MODEL)
```

```python
import functools

import jax
import jax.numpy as jnp
import numpy as np
from jax import lax
from jax.experimental import pallas as pl
from jax.experimental.pallas import tpu as pltpu

D_MODEL = 1024
N_META = 16
CHUNK = 64
EPS = 1e-6
SSM_WIDTH = 256
SSM_GROUP = 16
SSM_GROUPS = 16
SSM_STATE = 64
RET_WIDTH = 768
HEAD_DIM = 128
HEADS = 6
HEAD_PAIRS = HEADS // 2
PAIR_DIM = 2 * HEAD_DIM
ROPE_BASE = 10000.0
IN_WIDTH = SSM_WIDTH + 4 * RET_WIDTH
N_GROUPS = 4
EXPERTS_PER_GROUP = 4
N_EXPERTS = 16
EXPERT_FF = 256

S5_BLOCK = 16
S5_LANES = S5_BLOCK * SSM_GROUP
SCAN_ROWS = 8
RET_BLOCK = 256
TILE = 512
MIXER_TILE = 1024
LANES = 128
ROUTE_ROWS = 32
PROJ_ROWS = 256
PIECE = 16
TILE_PIECES = TILE // PIECE
SORTED_PIECES = TILE_PIECES + N_GROUPS
SORTED_ROWS = SORTED_PIECES * PIECE
EXT_WIDTH = D_MODEL + LANES
VMEM_LIMIT = 56 * 1024 * 1024

F32 = jnp.float32
BF16 = jnp.bfloat16
FP8 = jnp.float8_e4m3fn
FP8_HEADROOM = 256.0
I32 = jnp.int32


def _dot(a, b):
    return jnp.dot(a, b, preferred_element_type=F32)


def _sigmoid(x):
    return 1.0 / (1.0 + jnp.exp(-x))


def _rms_norm(x, g):
    return x * lax.rsqrt(jnp.mean(x * x, axis=-1, keepdims=True) + EPS) * g


def _rope(t, cos, sin_signed):
    return t * cos + pltpu.roll(t, HEAD_DIM // 2, axis=1) * sin_signed


def _meta_state(meta_ref, g_ref, w_ref, cos_ref, sin_ref, kdec_ref, blk_ref, u_ref, r0_ref):
    a = _rms_norm(meta_ref[...], g_ref[...]).astype(BF16)
    u_ref[...] = _dot(a, w_ref[:, :SSM_WIDTH])
    k_off = SSM_WIDTH + RET_WIDTH
    v_off = SSM_WIDTH + 2 * RET_WIDTH
    cos = cos_ref[...]
    sin = sin_ref[...]
    for p in range(HEAD_PAIRS):
        k = _dot(a, w_ref[:, k_off + p * PAIR_DIM:k_off + (p + 1) * PAIR_DIM])
        v = _dot(a, w_ref[:, v_off + p * PAIR_DIM:v_off + (p + 1) * PAIR_DIM])
        k = jnp.concatenate([_rope(k[:, :HEAD_DIM], cos, sin), _rope(k[:, HEAD_DIM:], cos, sin)], axis=1)
        kd = (k * kdec_ref[p]).astype(BF16)
        r0_ref[p] = _dot_rows(kd, v.astype(BF16)) * blk_ref[...]


def _dot_rows(a, b):
    return lax.dot_general(a, b, (((0,), (0,)), ((), ())), preferred_element_type=F32)


def _mixer_kernel(x_ref, g_ref, w32_ref, cos_ref, sin_ref, mask_ref, qdec_ref, kdec_ref, bdec_ref, blk_ref,
                  meta_ref, cos_m_ref, sin_m_ref, kdec_m_ref, u_ref, y_ref, um_ref, w_ref, r0_ref, r_ref):
    first_tile = pl.program_id(1) == 0

    @pl.when(jnp.logical_and(pl.program_id(0) == 0, first_tile))
    def _():
        for c in range(0, IN_WIDTH, SSM_WIDTH):
            w_ref[:, c:c + SSM_WIDTH] = w32_ref[:, c:c + SSM_WIDTH].astype(BF16)
        _meta_state(meta_ref, g_ref, w_ref, cos_m_ref, sin_m_ref, kdec_m_ref, blk_ref, um_ref, r0_ref)

    @pl.when(first_tile)
    def _():
        r_ref[...] = r0_ref[...]

    off = SSM_WIDTH
    for b in range(MIXER_TILE // RET_BLOCK):
        bs = slice(b * RET_BLOCK, (b + 1) * RET_BLOCK)
        a = _rms_norm(x_ref[bs, :], g_ref[...]).astype(BF16)
        u_ref[bs, :] = _dot(a, w_ref[:, :SSM_WIDTH])
        q_all = _dot(a, w_ref[:, off:off + RET_WIDTH])
        k_all = _dot(a, w_ref[:, off + RET_WIDTH:off + 2 * RET_WIDTH])
        v_all = _dot(a, w_ref[:, off + 2 * RET_WIDTH:off + 3 * RET_WIDTH])
        gate = _dot(a, w_ref[:, off + 3 * RET_WIDTH:off + 4 * RET_WIDTH])
        cos = cos_ref[bs, :]
        sin = sin_ref[bs, :]
        def scores(h):
            hs = slice(h * HEAD_DIM, (h + 1) * HEAD_DIM)
            q = _rope(q_all[:, hs], cos, sin)
            k = _rope(k_all[:, hs], cos, sin)
            return q, k, _dot_t(q.astype(BF16), k.astype(BF16))

        ahead = scores(0)
        for p in range(HEAD_PAIRS):
            ps = slice(p * PAIR_DIM, (p + 1) * PAIR_DIM)
            pair = [ahead, scores(2 * p + 1)]
            if p + 1 < HEAD_PAIRS:
                ahead = scores(2 * p + 2)
            q = jnp.concatenate([pair[0][0], pair[1][0]], axis=1)
            k = jnp.concatenate([pair[0][1], pair[1][1]], axis=1)
            v = v_all[:, ps].astype(BF16)
            state = r_ref[p]
            cross = _dot((q * qdec_ref[p]).astype(BF16), state.astype(BF16))
            kv = _dot_rows((k * kdec_ref[p]).astype(BF16), v)
            r_ref[p] = state * bdec_ref[p] + kv * blk_ref[...]
            for half, (_, _, s) in enumerate(pair):
                h = 2 * p + half
                hs = slice(h * HEAD_DIM, (h + 1) * HEAD_DIM)
                ls = slice(half * HEAD_DIM, (half + 1) * HEAD_DIM)
                o = _dot((s * mask_ref[h]).astype(BF16), v[:, ls]) + cross[:, ls]
                mu = jnp.mean(o, axis=-1, keepdims=True)
                d = o - mu
                var = jnp.mean(d * d, axis=-1, keepdims=True)
                gt = gate[:, hs]
                y_ref[bs, hs] = (gt * _sigmoid(gt) * d * lax.rsqrt(var + EPS)).astype(BF16)


def _mixer_call(x, g, w_in, cos, sin, mask, qdec, kdec, bdec, blk, meta, cos_m, sin_m, kdec_m):
    bsz, seq, _ = x.shape
    const = lambda a, **kw: pl.BlockSpec(a.shape, lambda b, i: (0,) * a.ndim, **kw)
    return pl.pallas_call(
        _mixer_kernel,
        grid=(bsz, seq // MIXER_TILE),
        in_specs=[
            pl.BlockSpec((None, MIXER_TILE, D_MODEL), lambda b, i: (b, i, 0)),
            const(g),
            const(w_in, pipeline_mode=pl.Buffered(1)),
            pl.BlockSpec((MIXER_TILE, HEAD_DIM), lambda b, i: (i, 0)),
            pl.BlockSpec((MIXER_TILE, HEAD_DIM), lambda b, i: (i, 0)),
            const(mask), const(qdec), const(kdec), const(bdec), const(blk),
            const(meta), const(cos_m), const(sin_m), const(kdec_m),
        ],
        out_specs=(
            pl.BlockSpec((None, MIXER_TILE, SSM_WIDTH), lambda b, i: (b, i, 0)),
            pl.BlockSpec((None, MIXER_TILE, RET_WIDTH), lambda b, i: (b, i, 0)),
            pl.BlockSpec((N_META, SSM_WIDTH), lambda b, i: (0, 0)),
        ),
        out_shape=(jax.ShapeDtypeStruct((bsz, seq, SSM_WIDTH), F32),
                   jax.ShapeDtypeStruct((bsz, seq, RET_WIDTH), BF16),
                   jax.ShapeDtypeStruct((N_META, SSM_WIDTH), F32)),
        scratch_shapes=[pltpu.VMEM((D_MODEL, IN_WIDTH), BF16),
                        pltpu.VMEM((HEAD_PAIRS, PAIR_DIM, PAIR_DIM), F32),
                        pltpu.VMEM((HEAD_PAIRS, PAIR_DIM, PAIR_DIM), F32)],
        compiler_params=pltpu.CompilerParams(
            dimension_semantics=("arbitrary", "arbitrary"), vmem_limit_bytes=VMEM_LIMIT),
        name="mixer_call",
    )(x, g, w_in, cos, sin, mask, qdec, kdec, bdec, blk, meta, cos_m, sin_m, kdec_m)


def _dot_t(a, b):
    return lax.dot_general(a, b, (((1,), (1,)), ((), ())), preferred_element_type=F32)


def _chunk_transpose(arrs):
    n = len(arrs)
    chunk = lax.broadcasted_iota(I32, (1, LANES), 1) // SSM_GROUP
    arrs = list(arrs)
    s = n // 2
    while s:
        keep = (chunk & s) == 0
        nxt = list(arrs)
        for i in range(n):
            if i & s == 0:
                lo, hi = arrs[i], arrs[i + s]
                nxt[i] = jnp.where(keep, lo, pltpu.roll(hi, s * SSM_GROUP, axis=1))
                nxt[i + s] = jnp.where(keep, pltpu.roll(lo, LANES - s * SSM_GROUP, axis=1), hi)
        arrs = nxt
        s //= 2
    return arrs


def _s5_kernel(u_lo_ref, u_hi_ref, um_ref, krow_ref, bmat_ref, cre_ref, cim_ref, ar_ref, ai_ref, wglu_ref,
               y_lo_ref, y_hi_ref, t0_ref, ug_ref, yg_ref):
    n_blocks = u_lo_ref.shape[0] // S5_BLOCK

    @pl.when(pl.program_id(0) == 0)
    def _():
        lane = lax.broadcasted_iota(I32, (SSM_GROUP, S5_LANES), 1)
        for g in range(SSM_GROUPS):
            k0 = krow_ref[g]
            for a in range(S5_BLOCK):
                blk = k0 if a == 0 else jnp.where(lane >= a * SSM_GROUP,
                                                  pltpu.roll(k0, a * SSM_GROUP, axis=1), 0.0)
                t0_ref[g, a * SSM_GROUP:(a + 1) * SSM_GROUP, :] = blk.astype(BF16)

    per_tile = LANES // SSM_GROUP
    for half, uh_ref in enumerate((u_lo_ref, u_hi_ref)):
        for t in range(S5_BLOCK // per_tile):
            words = [pltpu.bitcast(uh_ref[pl.ds(t * per_tile + k, n_blocks, stride=S5_BLOCK), :].astype(BF16),
                                   jnp.uint32) for k in range(per_tile)]
            for m, w in enumerate(_chunk_transpose(words)):
                ug_ref[half * per_tile + m, :, t * LANES:(t + 1) * LANES] = pltpu.bitcast(w, BF16)

    re, im, re0, im0 = [], [], [], []
    for p in range(SSM_GROUPS // 2):
        g0, g1 = 2 * p, 2 * p + 1
        v = _dot(ug_ref[g0], bmat_ref[g0]) + _dot(ug_ref[g1], bmat_ref[g1])
        v0 = (_dot(um_ref[g0].astype(BF16), bmat_ref[g0]) + _dot(um_ref[g1].astype(BF16), bmat_ref[g1]))[0:1]
        re.append(v[:, :LANES])
        im.append(v[:, LANES:])
        re0.append(v0[:, :LANES])
        im0.append(v0[:, LANES:])
    re, im, re0, im0 = (jnp.concatenate(parts, axis=1) for parts in (re, im, re0, im0))

    row = lax.broadcasted_iota(I32, re.shape, 0)
    ar, ai = ar_ref[0:1], ai_ref[0:1]
    re = re + jnp.where(row == 0, ar * re0 - ai * im0, 0.0)
    im = im + jnp.where(row == 0, ar * im0 + ai * re0, 0.0)
    d = 1
    while d < SCAN_ROWS:
        ar, ai = ar_ref[d - 1:d], ai_ref[d - 1:d]
        inside = row % SCAN_ROWS >= d
        sre = jnp.where(inside, pltpu.roll(re, d, axis=0), 0.0)
        sim = jnp.where(inside, pltpu.roll(im, d, axis=0), 0.0)
        re, im = re + ar * sre - ai * sim, im + ar * sim + ai * sre
        d *= 2
    ar, ai = ar_ref[...], ai_ref[...]
    re_tiles, im_tiles = [re[:SCAN_ROWS]], [im[:SCAN_ROWS]]
    for r in range(1, n_blocks // SCAN_ROWS):
        cre_ = re_tiles[-1][SCAN_ROWS - 1:SCAN_ROWS]
        cim_ = im_tiles[-1][SCAN_ROWS - 1:SCAN_ROWS]
        rs = slice(r * SCAN_ROWS, (r + 1) * SCAN_ROWS)
        re_tiles.append(re[rs] + ar * cre_ - ai * cim_)
        im_tiles.append(im[rs] + ar * cim_ + ai * cre_)
    re = jnp.concatenate(re_tiles, axis=0)
    im = jnp.concatenate(im_tiles, axis=0)
    pre = jnp.where(row == 0, re0, pltpu.roll(re, 1, axis=0)).astype(BF16)
    pim = jnp.where(row == 0, im0, pltpu.roll(im, 1, axis=0)).astype(BF16)

    for g in range(SSM_GROUPS):
        ps = slice((g // 2) * LANES, (g // 2 + 1) * LANES)
        yg_ref[g] = (_dot(ug_ref[g], t0_ref[g]) + _dot_t(pre[:, ps], cre_ref[g])
                     + _dot_t(pim[:, ps], cim_ref[g]))

    wglu = wglu_ref[...]
    for t in range(S5_BLOCK // per_tile):
        ts = slice(t * LANES, (t + 1) * LANES)
        halves = [_chunk_transpose([yg_ref[half * per_tile + m, :, ts] for m in range(per_tile)])
                  for half in range(SSM_GROUPS // per_tile)]
        for k in range(per_tile):
            y = jnp.concatenate([h[k] for h in halves], axis=1)
            y = jax.nn.gelu(y, approximate=True)
            y = y * _sigmoid(_dot(y.astype(BF16), wglu))
            i = t * per_tile + k
            y_lo_ref[pl.ds(i, n_blocks, stride=S5_BLOCK), :] = y[:, :LANES]
            y_hi_ref[pl.ds(i, n_blocks, stride=S5_BLOCK), :] = y[:, LANES:]


def _s5_call(u, um, krow, bmat, cre, cim, ar, ai, wglu):
    bsz, seq, _ = u.shape
    n_blocks = seq // S5_BLOCK
    const = lambda a: pl.BlockSpec(a.shape, lambda b: (0,) * a.ndim)
    return pl.pallas_call(
        _s5_kernel,
        grid=(bsz,),
        in_specs=[pl.BlockSpec((None, seq, LANES), lambda b: (b, 0, 0)),
                  pl.BlockSpec((None, seq, LANES), lambda b: (b, 0, 1)),
                  const(um), const(krow), const(bmat), const(cre), const(cim), const(ar), const(ai),
                  const(wglu)],
        out_specs=(pl.BlockSpec((None, seq, LANES), lambda b: (b, 0, 0)),
                   pl.BlockSpec((None, seq, LANES), lambda b: (b, 0, 0))),
        out_shape=(jax.ShapeDtypeStruct((bsz, seq, LANES), F32),
                   jax.ShapeDtypeStruct((bsz, seq, LANES), F32)),
        scratch_shapes=[pltpu.VMEM((SSM_GROUPS, S5_LANES, S5_LANES), BF16),
                        pltpu.VMEM((SSM_GROUPS, n_blocks, S5_LANES), BF16),
                        pltpu.VMEM((SSM_GROUPS, n_blocks, S5_LANES), F32)],
        compiler_params=pltpu.CompilerParams(
            dimension_semantics=("arbitrary",), vmem_limit_bytes=VMEM_LIMIT),
        name="s5_call",
    )(u, u, um, krow, bmat, cre, cim, ar, ai, wglu)


def _s5_operators(lam_re, lam_im, log_dt, b_re, b_im, c_re, c_im, d_skip):
    n_groups = lam_re.shape[0]
    lam = lax.complex(lam_re, lam_im)
    lam_dt = lam * jnp.exp(log_dt)[:, None]
    lam_bar = jnp.exp(lam_dt)
    b_bar = ((lam_bar - 1.0) / lam)[..., None] * lax.complex(b_re, b_im)
    c = lax.complex(c_re, c_im)
    tau = jnp.arange(S5_BLOCK + 1, dtype=F32)
    pows = jnp.exp(lam_dt[None] * tau[:, None, None])
    kern = jnp.real(jnp.einsum('ghp,tgp,gpk->gkth', c, pows[:S5_BLOCK], b_bar))
    skip = (jnp.eye(SSM_GROUP, dtype=F32)[None, :, None, :] * d_skip[:, None, None, :]
            * (tau[:S5_BLOCK] == 0).astype(F32)[None, None, :, None])
    krow = (kern + skip).reshape(n_groups, SSM_GROUP, S5_LANES)
    even = (jnp.arange(n_groups) % 2 == 0)[:, None, None]
    pair_pad = lambda m: jnp.concatenate([jnp.where(even, m, 0.0), jnp.where(even, 0.0, m)], axis=-1)
    bm = (pows[S5_BLOCK - 1 - jnp.arange(S5_BLOCK)].transpose(1, 0, 2)[:, :, None, :]
          * b_bar.transpose(0, 2, 1)[:, None, :, :]).reshape(n_groups, S5_LANES, SSM_STATE)
    bmat = jnp.concatenate([pair_pad(jnp.real(bm)), pair_pad(jnp.imag(bm))], axis=-1)
    cm = (pows[1:].transpose(1, 0, 2)[:, :, None, :] * c[:, None, :, :]).reshape(n_groups, S5_LANES, SSM_STATE)
    cre, cim = pair_pad(jnp.real(cm)), pair_pad(-jnp.imag(cm))
    step = S5_BLOCK * (1.0 + jnp.arange(SCAN_ROWS, dtype=F32))
    adec = jnp.exp(lam_dt[None, :, :] * step[:, None, None]).reshape(SCAN_ROWS, n_groups * SSM_STATE)
    return krow, bmat.astype(BF16), cre.astype(BF16), cim.astype(BF16), jnp.real(adec), jnp.imag(adec)


def _first_hit(values, target):
    hits, taken = [], None
    for v in values:
        hit = (v >= target) if taken is None else jnp.logical_and(v >= target, jnp.logical_not(taken))
        taken = hit if taken is None else jnp.logical_or(taken, hit)
        hits.append(hit)
    return hits


def _stack_rows(rows, n_rows):
    idx = lax.broadcasted_iota(I32, (n_rows, rows[0].shape[1]), 0)
    out = jnp.zeros((n_rows, rows[0].shape[1]), F32)
    for k, r in enumerate(rows):
        out = jnp.where(idx == k, r, out)
    return out


def _project(rs, x_ref, ys_lo_ref, ys_hi_ref, yr_ref, wout_ref, h_ref):
    ys = jnp.concatenate([ys_lo_ref[rs, :], ys_hi_ref[rs, :]], axis=1).astype(BF16)
    h = x_ref[rs, :] + _dot(ys, wout_ref[:SSM_WIDTH, :]) + _dot(yr_ref[rs, :], wout_ref[SSM_WIDTH:, :])
    h_ref[rs, :] = h.astype(BF16)
    return h


def _route(lt, tri_ref):
    gl = [lt[g:g + 1, :] for g in range(N_GROUPS)]
    gmax = functools.reduce(jnp.maximum, gl)
    g_w = 1.0 / functools.reduce(lambda a, b: a + b, [jnp.exp(l - gmax) for l in gl])
    sel = _first_hit(gl, gmax)
    ev = []
    for e in range(EXPERTS_PER_GROUP):
        acc = jnp.zeros_like(gmax)
        for g in range(N_GROUPS):
            k = N_GROUPS + g * EXPERTS_PER_GROUP + e
            acc = jnp.where(sel[g], lt[k:k + 1, :], acc)
        ev.append(acc)
    m1 = functools.reduce(jnp.maximum, ev)
    first = _first_hit(ev, m1)
    rest = [jnp.where(f, -jnp.inf, v) for f, v in zip(first, ev)]
    m2 = functools.reduce(jnp.maximum, rest)
    second = _first_hit(rest, m2)
    e2 = jnp.exp(m2 - m1)
    w1 = g_w / (1.0 + e2)
    w2 = e2 * w1
    combine = [jnp.where(f, w1, 0.0) + jnp.where(s, w2, 0.0) for f, s in zip(first, second)]

    sel_f = [jnp.where(s, 1.0, 0.0) for s in sel]
    incl = _dot(_stack_rows(sel_f, 8).astype(BF16), tri_ref[...])
    dest = jnp.zeros_like(gmax)
    seg_start = jnp.zeros((1, 1), F32)
    counts = []
    for g in range(N_GROUPS):
        run = incl[g:g + 1, :]
        cnt = run[:, TILE - 1:TILE]
        counts.append(cnt)
        dest = dest + sel_f[g] * (seg_start + run - 1.0)
        seg_start = seg_start + PIECE * jnp.floor((cnt + (PIECE - 1.0)) * (1.0 / PIECE))
    return combine, dest, counts


def _proj_kernel(x_ref, ys_lo_ref, ys_hi_ref, yr_ref, wout32_ref, g_ref, wr_ref, br_ref,
                 tri_ref, h_ref, stage_ref, dest_ref, cnt_ref, t_scr, lt_scr, wout_ref):
    step = pl.program_id(0)
    cur = step % 2
    prev = 1 - cur

    @pl.when(step == 0)
    def _():
        t_scr[prev] = jnp.zeros(t_scr.shape[1:], BF16)
        lt_scr[prev] = jnp.zeros(lt_scr.shape[1:], F32)
        wout_ref[...] = wout32_ref[...].astype(BF16)

    chunks = [slice(b * PROJ_ROWS, (b + 1) * PROJ_ROWS) for b in range(TILE // PROJ_ROWS)]
    proj_refs = (x_ref, ys_lo_ref, ys_hi_ref, yr_ref, wout_ref, h_ref)
    h_parts = [_project(chunks[0], *proj_refs)]

    combine, dest, counts = _route(lt_scr[prev], tri_ref)
    dest_ref[...] = _stack_rows([dest], 8)
    cnt_ref[...] = _stack_rows([c + jnp.zeros((1, LANES), F32) for c in counts], 8)
    perm = jnp.where(lax.broadcasted_iota(I32, (SORTED_ROWS, TILE), 0) == dest.astype(I32),
                     1.0, 0.0).astype(BF16)
    c_hi = [c.astype(BF16).astype(F32) for c in combine]
    c_lo = [c - hi for c, hi in zip(combine, c_hi)]
    cw = _stack_rows(c_hi + c_lo, LANES).T.astype(BF16)

    h_parts += [_project(rs, *proj_refs) for rs in chunks[1:]]

    t_parts = []
    for rs, h in zip(chunks, h_parts):
        t = _rms_norm(h, g_ref[...])
        t_hi = t.astype(BF16)
        t_parts.append((rs, t_hi, (t - t_hi.astype(F32)).astype(BF16)))

    def logits_hi(rs, t_hi):
        both = _dot_t(wr_ref[...], t_hi)
        lt_scr[cur, :, rs] = both[:ROUTE_ROWS] + both[ROUTE_ROWS:] + br_ref[...]

    def logits_lo(rs, t_lo):
        lt_scr[cur, :, rs] += _dot_t(wr_ref[:ROUTE_ROWS, :], t_lo)

    router = [functools.partial(logits_hi, rs, t_hi) for rs, t_hi, _ in t_parts]
    router += [functools.partial(logits_lo, rs, t_lo) for rs, _, t_lo in t_parts]
    n_col = D_MODEL // PROJ_ROWS
    for c in range(n_col):
        cs = slice(c * PROJ_ROWS, (c + 1) * PROJ_ROWS)
        stage_ref[:, cs] = _dot(perm, t_scr[prev, :, cs]).astype(BF16)
        if c < len(router):
            router[c]()
    for rest in router[n_col:]:
        rest()
    stage_ref[:, D_MODEL:] = _dot(perm, cw).astype(BF16)
    for rs, t_hi, _ in t_parts:
        t_scr[cur, rs, :] = t_hi


def _proj_call(x, ys_lo, ys_hi, yr, wout, g, wr, br, tri):
    n_tok = x.shape[0]
    n_tiles = n_tok // TILE
    const = lambda *shape: pl.BlockSpec(shape, lambda i: (0,) * len(shape))
    front = lambda i: jnp.minimum(i, n_tiles - 1)
    back = lambda i: jnp.maximum(i - 1, 0)
    rows = lambda width: pl.BlockSpec((TILE, width), lambda i: (front(i), 0))
    return pl.pallas_call(
        _proj_kernel,
        grid=(n_tiles + 1,),
        in_specs=[
            rows(D_MODEL), rows(LANES), rows(LANES), rows(RET_WIDTH),
            pl.BlockSpec((D_MODEL, D_MODEL), lambda i: (0, 0), pipeline_mode=pl.Buffered(1)), const(1, D_MODEL),
            const(2 * ROUTE_ROWS, D_MODEL), const(ROUTE_ROWS, 1), const(TILE, TILE),
        ],
        out_specs=(rows(D_MODEL),
                   pl.BlockSpec((SORTED_ROWS, EXT_WIDTH), lambda i: (back(i), 0)),
                   pl.BlockSpec((None, 8, TILE), lambda i: (back(i), 0, 0)),
                   pl.BlockSpec((None, 8, LANES), lambda i: (back(i), 0, 0))),
        scratch_shapes=[pltpu.VMEM((2, TILE, D_MODEL), BF16), pltpu.VMEM((2, ROUTE_ROWS, TILE), F32),
                        pltpu.VMEM((D_MODEL, D_MODEL), BF16)],
        out_shape=(jax.ShapeDtypeStruct((n_tok, D_MODEL), BF16),
                   jax.ShapeDtypeStruct((n_tiles * SORTED_ROWS, EXT_WIDTH), BF16),
                   jax.ShapeDtypeStruct((n_tiles, 8, TILE), F32),
                   jax.ShapeDtypeStruct((n_tiles, 8, LANES), F32)),
        compiler_params=pltpu.CompilerParams(
            dimension_semantics=("arbitrary",), vmem_limit_bytes=VMEM_LIMIT),
        name="proj_call",
    )(x, ys_lo, ys_hi, yr, wout, g, wr, br, tri)


def _sort_tables(cnt, n_steps):
    n_tiles = cnt.shape[0]
    npc = (cnt + PIECE - 1) // PIECE
    seg = jnp.cumsum(npc, axis=1) - npc
    before = jnp.cumsum(npc, axis=0) - npc
    n_tile_g = (jnp.sum(npc, axis=0) + TILE_PIECES - 1) // TILE_PIECES
    t_off = jnp.cumsum(n_tile_g) - n_tile_g
    j = jnp.arange(SORTED_PIECES, dtype=I32)[None, :, None]
    in_g = jnp.logical_and(j >= seg[:, None, :], j < (seg + npc)[:, None, :])
    pos = jnp.sum(jnp.where(in_g, TILE_PIECES * t_off[None, None, :] + before[:, None, :] + j - seg[:, None, :], 0),
                  axis=-1)
    valid = jnp.any(in_g, axis=-1)
    stage_piece = jnp.arange(n_tiles * SORTED_PIECES, dtype=I32).reshape(n_tiles, SORTED_PIECES)
    n_slots = n_steps * TILE_PIECES
    src = jnp.zeros((n_slots,), I32).at[jnp.where(valid, pos, n_slots).reshape(-1)].set(
        stage_piece.reshape(-1), mode='drop')
    steps = jnp.arange(n_steps, dtype=I32)
    g_step = jnp.minimum(jnp.sum(steps[:, None] >= (t_off + n_tile_g)[None, :], axis=1), N_GROUPS - 1)
    n_live = jnp.sum(n_tile_g).reshape(1)
    back = jnp.where(valid, pos, 0).reshape(-1)
    return src, g_step.astype(I32), n_live.astype(I32), back.astype(I32)


def _piece_copy(src_ref, piece, buf_ref, slot, j, sem_ref):
    start = piece * PIECE if isinstance(piece, int) else pl.multiple_of(piece * PIECE, PIECE)
    return pltpu.make_async_copy(src_ref.at[pl.ds(start, PIECE)],
                                 buf_ref.at[slot, pl.ds(j * PIECE, PIECE)], sem_ref.at[slot])


def _fetch_pieces(table_ref, src_ref, buf_ref, sem_ref, row, slot, n_pieces):
    for j in range(n_pieces):
        _piece_copy(src_ref, table_ref[row * n_pieces + j], buf_ref, slot, j, sem_ref).start()


def _wait_pieces(src_ref, buf_ref, sem_ref, slot, n_pieces):
    for j in range(n_pieces):
        _piece_copy(src_ref, 0, buf_ref, slot, j, sem_ref).wait()


def _gather_pieces(table_ref, src_ref, buf_ref, sem_ref, n_pieces):
    step = pl.program_id(0)
    last = pl.num_programs(0) - 1
    args = (src_ref, buf_ref, sem_ref)

    @pl.when(step == 0)
    def _():
        _fetch_pieces(table_ref, *args, step, 0, n_pieces)

    _wait_pieces(*args, step % 2, n_pieces)

    def fetch_next():
        _fetch_pieces(table_ref, *args, jnp.minimum(step + 1, last), (step + 1) % 2, n_pieces)

    def drain():
        @pl.when(step == last)
        def _():
            _wait_pieces(*args, (step + 1) % 2, n_pieces)

    return fetch_next, drain


def _to_fp8(x, headroom=FP8_HEADROOM):
    peak = jnp.maximum(jnp.max(jnp.abs(x), axis=(0, 1), keepdims=True), 1e-30)
    return (x * (headroom / peak)).astype(FP8), peak * (1.0 / headroom)


def _moe_kernel(src_ref, gstep_ref, nlive_ref, stage_ref, wg_ref, wu_ref, wd_ref, y_ref,
                buf_ref, sem_ref, wgb_ref, wub_ref, wdb_ref, inv_ref):
    step = pl.program_id(0)
    fetch_next, drain = _gather_pieces(src_ref, stage_ref, buf_ref, sem_ref, TILE_PIECES)

    @pl.when(jnp.logical_or(step == 0, gstep_ref[step] != gstep_ref[jnp.maximum(step - 1, 0)]))
    def _():
        ones = jnp.ones((1, LANES), F32)
        for e in range(EXPERTS_PER_GROUP):
            wgb_ref[e], inv = _to_fp8(wg_ref[e])
            inv_ref[e:e + 1, :] = inv * ones
            wub_ref[e], inv = _to_fp8(wu_ref[e])
            inv_ref[EXPERTS_PER_GROUP + e:EXPERTS_PER_GROUP + e + 1, :] = inv * ones
        peak = functools.reduce(jnp.maximum, [jnp.max(jnp.abs(wd_ref[e]), axis=(0, 1), keepdims=True)
                                              for e in range(EXPERTS_PER_GROUP)])
        peak = jnp.maximum(peak, 1e-30)
        for e in range(EXPERTS_PER_GROUP):
            wdb_ref[e * EXPERT_FF:(e + 1) * EXPERT_FF, :] = (wd_ref[e] * (FP8_HEADROOM / peak)).astype(FP8)
        inv_ref[2 * EXPERTS_PER_GROUP:2 * EXPERTS_PER_GROUP + 1, :] = peak * (1.0 / FP8_HEADROOM) * ones

    @pl.when(step < nlive_ref[0])
    def _():
        slot = step % 2
        fetch_next()

        def up(rs):
            t, t_inv = _to_fp8(buf_ref[slot, rs, :D_MODEL].astype(F32))
            out = []
            for e in range(EXPERTS_PER_GROUP):
                g_inv = inv_ref[e:e + 1, 0:1] * t_inv
                u_inv = inv_ref[EXPERTS_PER_GROUP + e:EXPERTS_PER_GROUP + e + 1, 0:1] * t_inv
                out.append((_dot(t, wgb_ref[e]) * g_inv, _dot(t, wub_ref[e]) * u_inv))
            return out

        def down(rs, hidden):
            cw = buf_ref[slot, rs, D_MODEL:].astype(F32)
            acts = []
            for e, (hg, hu) in enumerate(hidden):
                c = cw[:, e:e + 1] + cw[:, EXPERTS_PER_GROUP + e:EXPERTS_PER_GROUP + e + 1]
                acts.append(hg * _sigmoid(hg) * hu * c)
            act, a_inv = _to_fp8(jnp.concatenate(acts, axis=1))
            d_inv = inv_ref[2 * EXPERTS_PER_GROUP:2 * EXPERTS_PER_GROUP + 1, 0:1] * a_inv
            y_ref[rs, :] = (_dot(act, wdb_ref[...]) * d_inv).astype(BF16)

        halves = [slice(b * PROJ_ROWS, (b + 1) * PROJ_ROWS) for b in range(TILE // PROJ_ROWS)]
        hidden = [up(rs) for rs in halves]
        for rs, hid in zip(halves, hidden):
            down(rs, hid)

    @pl.when(step >= nlive_ref[0])
    def _():
        fetch_next()
        y_ref[...] = jnp.zeros_like(y_ref)

    drain()


def _moe_call(src, g_step, n_live, stage, wg, wu, wd, n_steps):
    grp = lambda shape: pl.BlockSpec((EXPERTS_PER_GROUP,) + shape, lambda s, src, gs, nl: (gs[s], 0, 0))
    return pl.pallas_call(
        _moe_kernel,
        grid_spec=pltpu.PrefetchScalarGridSpec(
            num_scalar_prefetch=3,
            grid=(n_steps,),
            in_specs=[pl.BlockSpec(memory_space=pl.ANY),
                      grp((D_MODEL, EXPERT_FF)), grp((D_MODEL, EXPERT_FF)), grp((EXPERT_FF, D_MODEL))],
            out_specs=pl.BlockSpec((TILE, D_MODEL), lambda s, src, gs, nl: (s, 0)),
            scratch_shapes=[pltpu.VMEM((2, TILE, EXT_WIDTH), BF16), pltpu.SemaphoreType.DMA((2,)),
                            pltpu.VMEM((EXPERTS_PER_GROUP, D_MODEL, EXPERT_FF), FP8),
                            pltpu.VMEM((EXPERTS_PER_GROUP, D_MODEL, EXPERT_FF), FP8),
                            pltpu.VMEM((EXPERTS_PER_GROUP * EXPERT_FF, D_MODEL), FP8),
                            pltpu.VMEM((16, LANES), F32)],
        ),
        out_shape=jax.ShapeDtypeStruct((n_steps * TILE, D_MODEL), BF16),
        compiler_params=pltpu.CompilerParams(
            dimension_semantics=("arbitrary",), vmem_limit_bytes=VMEM_LIMIT),
        name="moe_call",
    )(src, g_step, n_live, stage, wg, wu, wd)


def _final_kernel(back_ref, ysort_ref, h_ref, dest_ref, g_ref, o_ref, buf_ref, sem_ref):
    step = pl.program_id(0)
    fetch_next, drain = _gather_pieces(back_ref, ysort_ref, buf_ref, sem_ref, SORTED_PIECES)
    fetch_next()
    dest = _stack_rows([dest_ref[0:1, :]], LANES).T[:, 0:1].astype(I32)
    unperm = jnp.where(lax.broadcasted_iota(I32, (TILE, SORTED_ROWS), 1) == dest, 1.0, 0.0).astype(BF16)
    rows = buf_ref[step % 2]
    for b in range(TILE // PROJ_ROWS):
        rs = slice(b * PROJ_ROWS, (b + 1) * PROJ_ROWS)
        o_ref[rs, :] = _rms_norm(h_ref[rs, :].astype(F32) + _dot(unperm[rs], rows), g_ref[...])
    drain()


def _final_call(back, ysort, h, dest, g):
    n_tok = h.shape[0]
    return pl.pallas_call(
        _final_kernel,
        grid_spec=pltpu.PrefetchScalarGridSpec(
            num_scalar_prefetch=1,
            grid=(n_tok // TILE,),
            in_specs=[pl.BlockSpec(memory_space=pl.ANY),
                      pl.BlockSpec((TILE, D_MODEL), lambda i, back: (i, 0)),
                      pl.BlockSpec((None, 8, TILE), lambda i, back: (i, 0, 0)),
                      pl.BlockSpec((1, D_MODEL), lambda i, back: (0, 0))],
            out_specs=pl.BlockSpec((TILE, D_MODEL), lambda i, back: (i, 0)),
            scratch_shapes=[pltpu.VMEM((2, SORTED_ROWS, D_MODEL), BF16), pltpu.SemaphoreType.DMA((2,))],
        ),
        out_shape=jax.ShapeDtypeStruct((n_tok, D_MODEL), F32),
        compiler_params=pltpu.CompilerParams(
            dimension_semantics=("arbitrary",), vmem_limit_bytes=VMEM_LIMIT),
        name="final_call",
    )(back, ysort, h, dest, g)


def _rope_tables(length):
    pos = np.arange(length, dtype=np.float32)
    inv_freq = np.float32(ROPE_BASE) ** (-np.arange(0, HEAD_DIM, 2, dtype=np.float32) / np.float32(HEAD_DIM))
    ang = pos[:, None] * inv_freq[None, :]
    cos, sin = np.cos(ang), np.sin(ang)
    return np.concatenate([cos, cos], axis=-1), np.concatenate([-sin, sin], axis=-1)


def _retention_tables():
    f32 = np.float32
    gamma = f32(1.0) - f32(2.0) ** (f32(-5.0) - np.arange(HEADS, dtype=f32))
    log_g = np.log(gamma)[:, None, None]
    scale = f32(HEAD_DIM ** -0.5)
    idx = np.arange(RET_BLOCK)
    dist = np.abs(idx[:, None] - idx[None, :]).astype(f32)
    visible = (idx[None, :] // CHUNK) <= (idx[:, None] // CHUNK)
    mask = np.where(visible[None], np.exp(log_g * dist[None]), f32(0.0)) * scale
    ones = np.ones((1, 1, HEAD_DIM), f32)
    idx_f = idx.astype(f32)[None, :, None]
    pair = lambda a: a.reshape(HEAD_PAIRS, 2, a.shape[1], HEAD_DIM).transpose(0, 2, 1, 3).reshape(
        HEAD_PAIRS, a.shape[1], PAIR_DIM)
    qdec = pair(np.exp(log_g * (idx_f + f32(1.0))) * ones)
    kdec = pair(np.exp(log_g * (f32(RET_BLOCK - 1.0) - idx_f)) * scale * ones)
    meta_idx = np.arange(N_META, dtype=f32)[None, :, None]
    kdec_meta = pair(np.exp(log_g * (f32(N_META - 1.0) - meta_idx)) * scale * ones)
    blk = np.kron(np.eye(2, dtype=f32), np.ones((HEAD_DIM, HEAD_DIM), f32))
    bdec = np.exp(log_g * f32(RET_BLOCK)).reshape(HEAD_PAIRS, 2)
    bdec = np.stack([np.kron(np.diag(b), np.ones((HEAD_DIM, HEAD_DIM), f32)) for b in bdec])
    return tuple(a.astype(f32) for a in (mask, qdec, kdec, bdec, blk, kdec_meta))


def kernel(x, meta_tokens, norm_mix_g, w_in, ssm_lambda_re, ssm_lambda_im, ssm_log_dt, ssm_b_re, ssm_b_im, ssm_c_re, ssm_c_im, ssm_d, w_glu, w_out, norm_ffn_g, w_router_group, b_router_group, w_router_expert, b_router_expert, w_gate, w_up, w_down, norm_final_g):
    bsz, seq, _ = x.shape
    assert seq % TILE == 0 and seq % MIXER_TILE == 0 and MIXER_TILE % RET_BLOCK == 0 and RET_BLOCK % CHUNK == 0
    n_blocks = seq // S5_BLOCK
    assert n_blocks % SCAN_ROWS == 0
    n_tok = bsz * seq
    n_tiles = n_tok // TILE
    n_steps = -(-n_tiles * (TILE_PIECES + N_GROUPS - 1) // TILE_PIECES) + N_GROUPS

    cos, sin = _rope_tables(N_META + seq)
    mask, qdec, kdec, bdec, blk, kdec_meta = _retention_tables()
    g_mix = norm_mix_g[0][None, :]
    u, y_ret, u_meta = _mixer_call(x, g_mix, w_in[0], cos[N_META:], sin[N_META:], mask, qdec, kdec, bdec, blk,
                                   meta_tokens, cos[:N_META], sin[:N_META], kdec_meta)

    s5_ops = _s5_operators(
        ssm_lambda_re[0], ssm_lambda_im[0], ssm_log_dt[0], ssm_b_re[0], ssm_b_im[0],
        ssm_c_re[0], ssm_c_im[0], ssm_d[0])
    um = u_meta.reshape(S5_BLOCK, SSM_GROUPS, SSM_GROUP).transpose(1, 0, 2).reshape(SSM_GROUPS, 1, S5_LANES)
    um = jnp.pad(um, ((0, 0), (0, 7), (0, 0)))
    y_lo, y_hi = _s5_call(u, um, *s5_ops, w_glu[0].astype(BF16))

    w_r = jnp.concatenate(
        [w_router_group[0].T, w_router_expert[0].transpose(0, 2, 1).reshape(N_EXPERTS, D_MODEL)], axis=0)
    w_r = jnp.pad(w_r, ((0, ROUTE_ROWS - w_r.shape[0]), (0, 0)))
    b_r = jnp.concatenate([b_router_group[0], b_router_expert[0].reshape(-1)])
    b_r = jnp.pad(b_r, (0, ROUTE_ROWS - b_r.shape[0]))[:, None]
    w_r_hi = w_r.astype(BF16)
    w_r = jnp.concatenate([w_r_hi, (w_r - w_r_hi.astype(F32)).astype(BF16)], axis=0)
    tri = jnp.asarray(np.arange(TILE)[:, None] <= np.arange(TILE)[None, :], BF16)

    h, stage, dest, cnt = _proj_call(
        x.reshape(n_tok, D_MODEL), y_lo.reshape(n_tok, LANES), y_hi.reshape(n_tok, LANES),
        y_ret.reshape(n_tok, RET_WIDTH),
        w_out[0], norm_ffn_g[0][None, :], w_r, b_r, tri)
    src, g_step, n_live, back = _sort_tables(cnt[:, :N_GROUPS, 0].astype(I32), n_steps)
    y_sorted = _moe_call(src, g_step, n_live, stage, w_gate[0], w_up[0], w_down[0], n_steps)
    out = _final_call(back, y_sorted, h, dest, norm_final_g[None, :])
    return out.reshape(bsz, seq, D_MODEL)
```

```python
import functools

import jax
import jax.numpy as jnp
import numpy as np
from jax import lax
from jax.experimental import pallas as pl
from jax.experimental.pallas import tpu as pltpu

D_MODEL = 1024
N_META = 16
CHUNK = 64
EPS = 1e-6
SSM_WIDTH = 256
SSM_GROUP = 16
SSM_GROUPS = 16
SSM_STATE = 64
RET_WIDTH = 768
HEAD_DIM = 128
HEADS = 6
HEAD_PAIRS = HEADS // 2
PAIR_DIM = 2 * HEAD_DIM
ROPE_BASE = 10000.0
IN_WIDTH = SSM_WIDTH + 4 * RET_WIDTH
N_GROUPS = 4
EXPERTS_PER_GROUP = 4
N_EXPERTS = 16
EXPERT_FF = 256

S5_BLOCK = 16
S5_LANES = S5_BLOCK * SSM_GROUP
SCAN_ROWS = 8
RET_BLOCK = 256
TILE = 512
MIXER_TILE = 1024
LANES = 128
ROUTE_ROWS = 32
PROJ_ROWS = 256
PIECE = 16
TILE_PIECES = TILE // PIECE
MOE_TILE = 1024
MOE_PIECES = MOE_TILE // PIECE
SORTED_PIECES = TILE_PIECES + N_GROUPS
SORTED_ROWS = SORTED_PIECES * PIECE
EXT_WIDTH = D_MODEL + LANES
VMEM_LIMIT = 56 * 1024 * 1024

F32 = jnp.float32
BF16 = jnp.bfloat16
FP8 = jnp.float8_e4m3fn
FP8_HEADROOM = 256.0
I32 = jnp.int32


def _dot(a, b):
    return jnp.dot(a, b, preferred_element_type=F32)


def _sigmoid(x):
    return 1.0 / (1.0 + jnp.exp(-x))


def _rms_norm(x, g):
    return x * lax.rsqrt(jnp.mean(x * x, axis=-1, keepdims=True) + EPS) * g


def _rope(t, cos, sin_signed):
    return t * cos + pltpu.roll(t, HEAD_DIM // 2, axis=1) * sin_signed


def _meta_state(meta_ref, g_ref, w_ref, cos_ref, sin_ref, kdec_ref, blk_ref, u_ref, r0_ref):
    a = _rms_norm(meta_ref[...], g_ref[...]).astype(BF16)
    u_ref[...] = _dot(a, w_ref[:, :SSM_WIDTH])
    k_off = SSM_WIDTH + RET_WIDTH
    v_off = SSM_WIDTH + 2 * RET_WIDTH
    cos = cos_ref[...]
    sin = sin_ref[...]
    for p in range(HEAD_PAIRS):
        k = _dot(a, w_ref[:, k_off + p * PAIR_DIM:k_off + (p + 1) * PAIR_DIM])
        v = _dot(a, w_ref[:, v_off + p * PAIR_DIM:v_off + (p + 1) * PAIR_DIM])
        k = jnp.concatenate([_rope(k[:, :HEAD_DIM], cos, sin), _rope(k[:, HEAD_DIM:], cos, sin)], axis=1)
        kd = (k * kdec_ref[p]).astype(BF16)
        r0_ref[p] = _dot_rows(kd, v.astype(BF16)) * blk_ref[...]


def _dot_rows(a, b):
    return lax.dot_general(a, b, (((0,), (0,)), ((), ())), preferred_element_type=F32)


def _mixer_kernel(x_ref, g_ref, w32_ref, cos_ref, sin_ref, mask_ref, qdec_ref, kdec_ref, bdec_ref, blk_ref,
                  meta_ref, cos_m_ref, sin_m_ref, kdec_m_ref, u_ref, y_ref, xb_ref, um_ref, w_ref, r0_ref, r_ref):
    first_tile = pl.program_id(1) == 0

    @pl.when(jnp.logical_and(pl.program_id(0) == 0, first_tile))
    def _():
        for c in range(0, IN_WIDTH, SSM_WIDTH):
            w_ref[:, c:c + SSM_WIDTH] = w32_ref[:, c:c + SSM_WIDTH].astype(BF16)
        _meta_state(meta_ref, g_ref, w_ref, cos_m_ref, sin_m_ref, kdec_m_ref, blk_ref, um_ref, r0_ref)

    @pl.when(first_tile)
    def _():
        r_ref[...] = r0_ref[...]

    off = SSM_WIDTH
    for b in range(MIXER_TILE // RET_BLOCK):
        bs = slice(b * RET_BLOCK, (b + 1) * RET_BLOCK)
        xb = x_ref[bs, :]
        xb_ref[bs, :] = xb.astype(BF16)
        a = _rms_norm(xb, g_ref[...]).astype(BF16)
        u_ref[bs, :] = _dot(a, w_ref[:, :SSM_WIDTH])
        q_all = _dot(a, w_ref[:, off:off + RET_WIDTH])
        k_all = _dot(a, w_ref[:, off + RET_WIDTH:off + 2 * RET_WIDTH])
        v_all = _dot(a, w_ref[:, off + 2 * RET_WIDTH:off + 3 * RET_WIDTH])
        gate = _dot(a, w_ref[:, off + 3 * RET_WIDTH:off + 4 * RET_WIDTH])
        cos = cos_ref[bs, :]
        sin = sin_ref[bs, :]
        def scores(h):
            hs = slice(h * HEAD_DIM, (h + 1) * HEAD_DIM)
            q = _rope(q_all[:, hs], cos, sin)
            k = _rope(k_all[:, hs], cos, sin)
            return q, k, _dot_t(q.astype(BF16), k.astype(BF16))

        ahead = scores(0)
        for p in range(HEAD_PAIRS):
            ps = slice(p * PAIR_DIM, (p + 1) * PAIR_DIM)
            pair = [ahead, scores(2 * p + 1)]
            if p + 1 < HEAD_PAIRS:
                ahead = scores(2 * p + 2)
            q = jnp.concatenate([pair[0][0], pair[1][0]], axis=1)
            k = jnp.concatenate([pair[0][1], pair[1][1]], axis=1)
            v = v_all[:, ps].astype(BF16)
            state = r_ref[p]
            cross = _dot((q * qdec_ref[p]).astype(BF16), state.astype(BF16))
            kv = _dot_rows((k * kdec_ref[p]).astype(BF16), v)
            r_ref[p] = state * bdec_ref[p] + kv * blk_ref[...]
            for half, (_, _, s) in enumerate(pair):
                h = 2 * p + half
                hs = slice(h * HEAD_DIM, (h + 1) * HEAD_DIM)
                ls = slice(half * HEAD_DIM, (half + 1) * HEAD_DIM)
                o = _dot((s * mask_ref[h]).astype(BF16), v[:, ls]) + cross[:, ls]
                mu = jnp.mean(o, axis=-1, keepdims=True)
                d = o - mu
                var = jnp.mean(d * d, axis=-1, keepdims=True)
                gt = gate[:, hs]
                y_ref[bs, hs] = (gt * _sigmoid(gt) * d * lax.rsqrt(var + EPS)).astype(BF16)


def _mixer_call(x, g, w_in, cos, sin, mask, qdec, kdec, bdec, blk, meta, cos_m, sin_m, kdec_m):
    bsz, seq, _ = x.shape
    const = lambda a, **kw: pl.BlockSpec(a.shape, lambda b, i: (0,) * a.ndim, **kw)
    return pl.pallas_call(
        _mixer_kernel,
        grid=(bsz, seq // MIXER_TILE),
        in_specs=[
            pl.BlockSpec((None, MIXER_TILE, D_MODEL), lambda b, i: (b, i, 0)),
            const(g),
            const(w_in, pipeline_mode=pl.Buffered(1)),
            pl.BlockSpec((MIXER_TILE, HEAD_DIM), lambda b, i: (i, 0)),
            pl.BlockSpec((MIXER_TILE, HEAD_DIM), lambda b, i: (i, 0)),
            const(mask), const(qdec), const(kdec), const(bdec), const(blk),
            const(meta), const(cos_m), const(sin_m), const(kdec_m),
        ],
        out_specs=(
            pl.BlockSpec((None, MIXER_TILE, SSM_WIDTH), lambda b, i: (b, i, 0)),
            pl.BlockSpec((None, MIXER_TILE, RET_WIDTH), lambda b, i: (b, i, 0)),
            pl.BlockSpec((None, MIXER_TILE, D_MODEL), lambda b, i: (b, i, 0)),
            pl.BlockSpec((N_META, SSM_WIDTH), lambda b, i: (0, 0)),
        ),
        out_shape=(jax.ShapeDtypeStruct((bsz, seq, SSM_WIDTH), F32),
                   jax.ShapeDtypeStruct((bsz, seq, RET_WIDTH), BF16),
                   jax.ShapeDtypeStruct((bsz, seq, D_MODEL), BF16),
                   jax.ShapeDtypeStruct((N_META, SSM_WIDTH), F32)),
        scratch_shapes=[pltpu.VMEM((D_MODEL, IN_WIDTH), BF16),
                        pltpu.VMEM((HEAD_PAIRS, PAIR_DIM, PAIR_DIM), F32),
                        pltpu.VMEM((HEAD_PAIRS, PAIR_DIM, PAIR_DIM), F32)],
        compiler_params=pltpu.CompilerParams(
            dimension_semantics=("arbitrary", "arbitrary"), vmem_limit_bytes=VMEM_LIMIT),
        name="mixer_call",
    )(x, g, w_in, cos, sin, mask, qdec, kdec, bdec, blk, meta, cos_m, sin_m, kdec_m)


def _dot_t(a, b):
    return lax.dot_general(a, b, (((1,), (1,)), ((), ())), preferred_element_type=F32)


def _chunk_transpose(arrs):
    n = len(arrs)
    chunk = lax.broadcasted_iota(I32, (1, LANES), 1) // SSM_GROUP
    arrs = list(arrs)
    s = n // 2
    while s:
        keep = (chunk & s) == 0
        nxt = list(arrs)
        for i in range(n):
            if i & s == 0:
                lo, hi = arrs[i], arrs[i + s]
                nxt[i] = jnp.where(keep, lo, pltpu.roll(hi, s * SSM_GROUP, axis=1))
                nxt[i + s] = jnp.where(keep, pltpu.roll(lo, LANES - s * SSM_GROUP, axis=1), hi)
        arrs = nxt
        s //= 2
    return arrs


def _s5_kernel(u_lo_ref, u_hi_ref, um_ref, krow_ref, bmat_ref, cre_ref, cim_ref, ar_ref, ai_ref, wglu_ref,
               y_lo_ref, y_hi_ref, t0_ref, ug_ref, yg_ref):
    n_blocks = u_lo_ref.shape[0] // S5_BLOCK

    @pl.when(pl.program_id(0) == 0)
    def _():
        lane = lax.broadcasted_iota(I32, (SSM_GROUP, S5_LANES), 1)
        for g in range(SSM_GROUPS):
            k0 = krow_ref[g]
            for a in range(S5_BLOCK):
                blk = k0 if a == 0 else jnp.where(lane >= a * SSM_GROUP,
                                                  pltpu.roll(k0, a * SSM_GROUP, axis=1), 0.0)
                t0_ref[g, a * SSM_GROUP:(a + 1) * SSM_GROUP, :] = blk.astype(BF16)

    per_tile = LANES // SSM_GROUP
    for half, uh_ref in enumerate((u_lo_ref, u_hi_ref)):
        for t in range(S5_BLOCK // per_tile):
            words = [pltpu.bitcast(uh_ref[pl.ds(t * per_tile + k, n_blocks, stride=S5_BLOCK), :].astype(BF16),
                                   jnp.uint32) for k in range(per_tile)]
            for m, w in enumerate(_chunk_transpose(words)):
                ug_ref[half * per_tile + m, :, t * LANES:(t + 1) * LANES] = pltpu.bitcast(w, BF16)

    re, im, re0, im0 = [], [], [], []
    for p in range(SSM_GROUPS // 2):
        g0, g1 = 2 * p, 2 * p + 1
        v = _dot(ug_ref[g0], bmat_ref[g0]) + _dot(ug_ref[g1], bmat_ref[g1])
        v0 = (_dot(um_ref[g0].astype(BF16), bmat_ref[g0]) + _dot(um_ref[g1].astype(BF16), bmat_ref[g1]))[0:1]
        re.append(v[:, :LANES])
        im.append(v[:, LANES:])
        re0.append(v0[:, :LANES])
        im0.append(v0[:, LANES:])
    re, im, re0, im0 = (jnp.concatenate(parts, axis=1) for parts in (re, im, re0, im0))

    row = lax.broadcasted_iota(I32, re.shape, 0)
    ar, ai = ar_ref[0:1], ai_ref[0:1]
    re = re + jnp.where(row == 0, ar * re0 - ai * im0, 0.0)
    im = im + jnp.where(row == 0, ar * im0 + ai * re0, 0.0)
    d = 1
    while d < SCAN_ROWS:
        ar, ai = ar_ref[d - 1:d], ai_ref[d - 1:d]
        inside = row % SCAN_ROWS >= d
        sre = jnp.where(inside, pltpu.roll(re, d, axis=0), 0.0)
        sim = jnp.where(inside, pltpu.roll(im, d, axis=0), 0.0)
        re, im = re + ar * sre - ai * sim, im + ar * sim + ai * sre
        d *= 2
    ar, ai = ar_ref[...], ai_ref[...]
    re_tiles, im_tiles = [re[:SCAN_ROWS]], [im[:SCAN_ROWS]]
    for r in range(1, n_blocks // SCAN_ROWS):
        cre_ = re_tiles[-1][SCAN_ROWS - 1:SCAN_ROWS]
        cim_ = im_tiles[-1][SCAN_ROWS - 1:SCAN_ROWS]
        rs = slice(r * SCAN_ROWS, (r + 1) * SCAN_ROWS)
        re_tiles.append(re[rs] + ar * cre_ - ai * cim_)
        im_tiles.append(im[rs] + ar * cim_ + ai * cre_)
    re = jnp.concatenate(re_tiles, axis=0)
    im = jnp.concatenate(im_tiles, axis=0)
    pre = jnp.where(row == 0, re0, pltpu.roll(re, 1, axis=0)).astype(BF16)
    pim = jnp.where(row == 0, im0, pltpu.roll(im, 1, axis=0)).astype(BF16)

    for g in range(SSM_GROUPS):
        ps = slice((g // 2) * LANES, (g // 2 + 1) * LANES)
        yg_ref[g] = (_dot(ug_ref[g], t0_ref[g]) + _dot_t(pre[:, ps], cre_ref[g])
                     + _dot_t(pim[:, ps], cim_ref[g]))

    wglu = wglu_ref[...]
    for t in range(S5_BLOCK // per_tile):
        ts = slice(t * LANES, (t + 1) * LANES)
        halves = [_chunk_transpose([yg_ref[half * per_tile + m, :, ts] for m in range(per_tile)])
                  for half in range(SSM_GROUPS // per_tile)]
        for k in range(per_tile):
            y = jnp.concatenate([h[k] for h in halves], axis=1)
            y = jax.nn.gelu(y, approximate=True)
            y = y * _sigmoid(_dot(y.astype(BF16), wglu))
            i = t * per_tile + k
            y_lo_ref[pl.ds(i, n_blocks, stride=S5_BLOCK), :] = y[:, :LANES]
            y_hi_ref[pl.ds(i, n_blocks, stride=S5_BLOCK), :] = y[:, LANES:]


def _s5_call(u, um, krow, bmat, cre, cim, ar, ai, wglu):
    bsz, seq, _ = u.shape
    n_blocks = seq // S5_BLOCK
    const = lambda a: pl.BlockSpec(a.shape, lambda b: (0,) * a.ndim)
    return pl.pallas_call(
        _s5_kernel,
        grid=(bsz,),
        in_specs=[pl.BlockSpec((None, seq, LANES), lambda b: (b, 0, 0)),
                  pl.BlockSpec((None, seq, LANES), lambda b: (b, 0, 1)),
                  const(um), const(krow), const(bmat), const(cre), const(cim), const(ar), const(ai),
                  const(wglu)],
        out_specs=(pl.BlockSpec((None, seq, LANES), lambda b: (b, 0, 0)),
                   pl.BlockSpec((None, seq, LANES), lambda b: (b, 0, 0))),
        out_shape=(jax.ShapeDtypeStruct((bsz, seq, LANES), F32),
                   jax.ShapeDtypeStruct((bsz, seq, LANES), F32)),
        scratch_shapes=[pltpu.VMEM((SSM_GROUPS, S5_LANES, S5_LANES), BF16),
                        pltpu.VMEM((SSM_GROUPS, n_blocks, S5_LANES), BF16),
                        pltpu.VMEM((SSM_GROUPS, n_blocks, S5_LANES), F32)],
        compiler_params=pltpu.CompilerParams(
            dimension_semantics=("arbitrary",), vmem_limit_bytes=VMEM_LIMIT),
        name="s5_call",
    )(u, u, um, krow, bmat, cre, cim, ar, ai, wglu)


def _s5_operators(lam_re, lam_im, log_dt, b_re, b_im, c_re, c_im, d_skip):
    n_groups = lam_re.shape[0]
    lam = lax.complex(lam_re, lam_im)
    lam_dt = lam * jnp.exp(log_dt)[:, None]
    lam_bar = jnp.exp(lam_dt)
    b_bar = ((lam_bar - 1.0) / lam)[..., None] * lax.complex(b_re, b_im)
    c = lax.complex(c_re, c_im)
    tau = jnp.arange(S5_BLOCK + 1, dtype=F32)
    pows = jnp.exp(lam_dt[None] * tau[:, None, None])
    kern = jnp.real(jnp.einsum('ghp,tgp,gpk->gkth', c, pows[:S5_BLOCK], b_bar))
    skip = (jnp.eye(SSM_GROUP, dtype=F32)[None, :, None, :] * d_skip[:, None, None, :]
            * (tau[:S5_BLOCK] == 0).astype(F32)[None, None, :, None])
    krow = (kern + skip).reshape(n_groups, SSM_GROUP, S5_LANES)
    even = (jnp.arange(n_groups) % 2 == 0)[:, None, None]
    pair_pad = lambda m: jnp.concatenate([jnp.where(even, m, 0.0), jnp.where(even, 0.0, m)], axis=-1)
    bm = (pows[S5_BLOCK - 1 - jnp.arange(S5_BLOCK)].transpose(1, 0, 2)[:, :, None, :]
          * b_bar.transpose(0, 2, 1)[:, None, :, :]).reshape(n_groups, S5_LANES, SSM_STATE)
    bmat = jnp.concatenate([pair_pad(jnp.real(bm)), pair_pad(jnp.imag(bm))], axis=-1)
    cm = (pows[1:].transpose(1, 0, 2)[:, :, None, :] * c[:, None, :, :]).reshape(n_groups, S5_LANES, SSM_STATE)
    cre, cim = pair_pad(jnp.real(cm)), pair_pad(-jnp.imag(cm))
    step = S5_BLOCK * (1.0 + jnp.arange(SCAN_ROWS, dtype=F32))
    adec = jnp.exp(lam_dt[None, :, :] * step[:, None, None]).reshape(SCAN_ROWS, n_groups * SSM_STATE)
    return krow, bmat.astype(BF16), cre.astype(BF16), cim.astype(BF16), jnp.real(adec), jnp.imag(adec)


def _first_hit(values, target):
    hits, taken = [], None
    for v in values:
        hit = (v >= target) if taken is None else jnp.logical_and(v >= target, jnp.logical_not(taken))
        taken = hit if taken is None else jnp.logical_or(taken, hit)
        hits.append(hit)
    return hits


def _stack_rows(rows, n_rows):
    idx = lax.broadcasted_iota(I32, (n_rows, rows[0].shape[1]), 0)
    out = jnp.zeros((n_rows, rows[0].shape[1]), F32)
    for k, r in enumerate(rows):
        out = jnp.where(idx == k, r, out)
    return out


def _project(rs, x_ref, ys_lo_ref, ys_hi_ref, yr_ref, wout_ref, h_ref):
    ys = jnp.concatenate([ys_lo_ref[rs, :], ys_hi_ref[rs, :]], axis=1).astype(BF16)
    h = x_ref[rs, :] + _dot(ys, wout_ref[:SSM_WIDTH, :]) + _dot(yr_ref[rs, :], wout_ref[SSM_WIDTH:, :])
    h_ref[rs, :] = h.astype(BF16)
    return h


def _route(lt, tri_ref):
    gl = [lt[g:g + 1, :] for g in range(N_GROUPS)]
    gmax = functools.reduce(jnp.maximum, gl)
    g_w = 1.0 / functools.reduce(lambda a, b: a + b, [jnp.exp(l - gmax) for l in gl])
    sel = _first_hit(gl, gmax)
    ev = []
    for e in range(EXPERTS_PER_GROUP):
        acc = jnp.zeros_like(gmax)
        for g in range(N_GROUPS):
            k = N_GROUPS + g * EXPERTS_PER_GROUP + e
            acc = jnp.where(sel[g], lt[k:k + 1, :], acc)
        ev.append(acc)
    m1 = functools.reduce(jnp.maximum, ev)
    first = _first_hit(ev, m1)
    rest = [jnp.where(f, -jnp.inf, v) for f, v in zip(first, ev)]
    m2 = functools.reduce(jnp.maximum, rest)
    second = _first_hit(rest, m2)
    e2 = jnp.exp(m2 - m1)
    w1 = g_w / (1.0 + e2)
    w2 = e2 * w1
    combine = [jnp.where(f, w1, 0.0) + jnp.where(s, w2, 0.0) for f, s in zip(first, second)]

    sel_f = [jnp.where(s, 1.0, 0.0) for s in sel]
    incl = _dot(_stack_rows(sel_f, 8).astype(BF16), tri_ref[...])
    dest = jnp.zeros_like(gmax)
    seg_start = jnp.zeros((1, 1), F32)
    counts = []
    for g in range(N_GROUPS):
        run = incl[g:g + 1, :]
        cnt = run[:, TILE - 1:TILE]
        counts.append(cnt)
        dest = dest + sel_f[g] * (seg_start + run - 1.0)
        seg_start = seg_start + PIECE * jnp.floor((cnt + (PIECE - 1.0)) * (1.0 / PIECE))
    return combine, dest, counts


def _proj_kernel(x_ref, ys_lo_ref, ys_hi_ref, yr_ref, wout32_ref, g_ref, wr_ref, br_ref,
                 tri_ref, h_ref, stage_ref, dest_ref, cnt_ref, wout_ref):
    @pl.when(pl.program_id(0) == 0)
    def _():
        wout_ref[...] = wout32_ref[...].astype(BF16)

    chunks = [slice(b * PROJ_ROWS, (b + 1) * PROJ_ROWS) for b in range(TILE // PROJ_ROWS)]
    proj_refs = (x_ref, ys_lo_ref, ys_hi_ref, yr_ref, wout_ref, h_ref)
    h_parts = [_project(rs, *proj_refs) for rs in chunks]
    t_parts = []
    for h in h_parts:
        t = _rms_norm(h, g_ref[...])
        t_hi = t.astype(BF16)
        t_parts.append((t_hi, (t - t_hi.astype(F32)).astype(BF16)))
    lt_parts = []
    for t_hi, t_lo in t_parts:
        both = _dot_t(wr_ref[...], t_hi)
        lt_parts.append(both[:ROUTE_ROWS] + both[ROUTE_ROWS:] + _dot_t(wr_ref[:ROUTE_ROWS, :], t_lo))
    lt = jnp.concatenate(lt_parts, axis=1) + br_ref[...]

    combine, dest, counts = _route(lt, tri_ref)
    dest_ref[...] = _stack_rows([dest], 8)
    cnt_ref[...] = _stack_rows([c + jnp.zeros((1, LANES), F32) for c in counts], 8)
    perm = jnp.where(lax.broadcasted_iota(I32, (SORTED_ROWS, TILE), 0) == dest.astype(I32),
                     1.0, 0.0).astype(BF16)
    c_hi = [c.astype(BF16).astype(F32) for c in combine]
    c_lo = [c - hi for c, hi in zip(combine, c_hi)]
    cw = _stack_rows(c_hi + c_lo, LANES).T.astype(BF16)
    t_ext = jnp.concatenate([jnp.concatenate([t for t, _ in t_parts], axis=0), cw], axis=1)
    stage_ref[...] = _dot(perm, t_ext).astype(BF16)


def _proj_call(x, ys_lo, ys_hi, yr, wout, g, wr, br, tri):
    n_tok = x.shape[0]
    n_tiles = n_tok // TILE
    const = lambda *shape: pl.BlockSpec(shape, lambda i: (0,) * len(shape))
    rows = lambda width: pl.BlockSpec((TILE, width), lambda i: (i, 0))
    return pl.pallas_call(
        _proj_kernel,
        grid=(n_tiles,),
        in_specs=[
            rows(D_MODEL), rows(LANES), rows(LANES), rows(RET_WIDTH),
            pl.BlockSpec((D_MODEL, D_MODEL), lambda i: (0, 0), pipeline_mode=pl.Buffered(1)), const(1, D_MODEL),
            const(2 * ROUTE_ROWS, D_MODEL), const(ROUTE_ROWS, 1), const(TILE, TILE),
        ],
        out_specs=(rows(D_MODEL),
                   pl.BlockSpec((SORTED_ROWS, EXT_WIDTH), lambda i: (i, 0)),
                   pl.BlockSpec((None, 8, TILE), lambda i: (i, 0, 0)),
                   pl.BlockSpec((None, 8, LANES), lambda i: (i, 0, 0))),
        scratch_shapes=[pltpu.VMEM((D_MODEL, D_MODEL), BF16)],
        out_shape=(jax.ShapeDtypeStruct((n_tok, D_MODEL), BF16),
                   jax.ShapeDtypeStruct((n_tiles * SORTED_ROWS, EXT_WIDTH), BF16),
                   jax.ShapeDtypeStruct((n_tiles, 8, TILE), F32),
                   jax.ShapeDtypeStruct((n_tiles, 8, LANES), F32)),
        compiler_params=pltpu.CompilerParams(
            dimension_semantics=("arbitrary",), vmem_limit_bytes=VMEM_LIMIT),
        name="proj_call",
    )(x, ys_lo, ys_hi, yr, wout, g, wr, br, tri)


def _sort_tables(cnt, n_steps):
    n_tiles = cnt.shape[0]
    npc = (cnt + PIECE - 1) // PIECE
    seg = jnp.cumsum(npc, axis=1) - npc
    before = jnp.cumsum(npc, axis=0) - npc
    n_tile_g = (jnp.sum(npc, axis=0) + MOE_PIECES - 1) // MOE_PIECES
    t_off = jnp.cumsum(n_tile_g) - n_tile_g
    j = jnp.arange(SORTED_PIECES, dtype=I32)[None, :, None]
    in_g = jnp.logical_and(j >= seg[:, None, :], j < (seg + npc)[:, None, :])
    pos = jnp.sum(jnp.where(in_g, MOE_PIECES * t_off[None, None, :] + before[:, None, :] + j - seg[:, None, :], 0),
                  axis=-1)
    valid = jnp.any(in_g, axis=-1)
    stage_piece = jnp.arange(n_tiles * SORTED_PIECES, dtype=I32).reshape(n_tiles, SORTED_PIECES)
    n_slots = n_steps * MOE_PIECES
    src = jnp.zeros((n_slots,), I32).at[jnp.where(valid, pos, n_slots).reshape(-1)].set(
        stage_piece.reshape(-1), mode='drop')
    steps = jnp.arange(n_steps, dtype=I32)
    g_step = jnp.minimum(jnp.sum(steps[:, None] >= (t_off + n_tile_g)[None, :], axis=1), N_GROUPS - 1)
    n_live = jnp.sum(n_tile_g).reshape(1)
    back = jnp.where(valid, pos, 0).reshape(-1)
    return src, g_step.astype(I32), n_live.astype(I32), back.astype(I32)


def _piece_copy(src_ref, piece, buf_ref, slot, j, sem_ref):
    start = piece * PIECE if isinstance(piece, int) else pl.multiple_of(piece * PIECE, PIECE)
    return pltpu.make_async_copy(src_ref.at[pl.ds(start, PIECE)],
                                 buf_ref.at[slot, pl.ds(j * PIECE, PIECE)], sem_ref.at[slot])


def _fetch_pieces(table_ref, src_ref, buf_ref, sem_ref, row, slot, n_pieces):
    for j in range(n_pieces):
        _piece_copy(src_ref, table_ref[row * n_pieces + j], buf_ref, slot, j, sem_ref).start()


def _wait_pieces(src_ref, buf_ref, sem_ref, slot, n_pieces):
    for j in range(n_pieces):
        _piece_copy(src_ref, 0, buf_ref, slot, j, sem_ref).wait()


def _gather_pieces(table_ref, src_ref, buf_ref, sem_ref, n_pieces):
    step = pl.program_id(0)
    last = pl.num_programs(0) - 1
    args = (src_ref, buf_ref, sem_ref)

    @pl.when(step == 0)
    def _():
        _fetch_pieces(table_ref, *args, step, 0, n_pieces)

    _wait_pieces(*args, step % 2, n_pieces)

    def fetch_next():
        _fetch_pieces(table_ref, *args, jnp.minimum(step + 1, last), (step + 1) % 2, n_pieces)

    def drain():
        @pl.when(step == last)
        def _():
            _wait_pieces(*args, (step + 1) % 2, n_pieces)

    return fetch_next, drain


def _to_fp8(x, headroom=FP8_HEADROOM):
    peak = jnp.maximum(jnp.max(jnp.abs(x), axis=(0, 1), keepdims=True), 1e-30)
    return (x * (headroom / peak)).astype(FP8), peak * (1.0 / headroom)


def _moe_kernel(src_ref, gstep_ref, nlive_ref, stage_ref, wg_ref, wu_ref, wd_ref, y_ref,
                buf_ref, sem_ref, wgb_ref, wub_ref, wdb_ref, inv_ref):
    step = pl.program_id(0)
    fetch_next, drain = _gather_pieces(src_ref, stage_ref, buf_ref, sem_ref, MOE_PIECES)

    @pl.when(jnp.logical_or(step == 0, gstep_ref[step] != gstep_ref[jnp.maximum(step - 1, 0)]))
    def _():
        ones = jnp.ones((1, LANES), F32)
        for e in range(EXPERTS_PER_GROUP):
            wgb_ref[e], inv = _to_fp8(wg_ref[e])
            inv_ref[e:e + 1, :] = inv * ones
            wub_ref[e], inv = _to_fp8(wu_ref[e])
            inv_ref[EXPERTS_PER_GROUP + e:EXPERTS_PER_GROUP + e + 1, :] = inv * ones
        peak = functools.reduce(jnp.maximum, [jnp.max(jnp.abs(wd_ref[e]), axis=(0, 1), keepdims=True)
                                              for e in range(EXPERTS_PER_GROUP)])
        peak = jnp.maximum(peak, 1e-30)
        for e in range(EXPERTS_PER_GROUP):
            wdb_ref[e * EXPERT_FF:(e + 1) * EXPERT_FF, :] = (wd_ref[e] * (FP8_HEADROOM / peak)).astype(FP8)
        inv_ref[2 * EXPERTS_PER_GROUP:2 * EXPERTS_PER_GROUP + 1, :] = peak * (1.0 / FP8_HEADROOM) * ones

    @pl.when(step < nlive_ref[0])
    def _():
        slot = step % 2
        fetch_next()

        def up(rs):
            t, t_inv = _to_fp8(buf_ref[slot, rs, :D_MODEL].astype(F32))
            out = []
            for e in range(EXPERTS_PER_GROUP):
                g_inv = inv_ref[e:e + 1, 0:1] * t_inv
                u_inv = inv_ref[EXPERTS_PER_GROUP + e:EXPERTS_PER_GROUP + e + 1, 0:1] * t_inv
                out.append((_dot(t, wgb_ref[e]) * g_inv, _dot(t, wub_ref[e]), u_inv))
            return out

        def down(rs, hidden):
            cw = buf_ref[slot, rs, D_MODEL:].astype(F32)
            acts = []
            for e, (hg, hu, u_inv) in enumerate(hidden):
                c = (cw[:, e:e + 1] + cw[:, EXPERTS_PER_GROUP + e:EXPERTS_PER_GROUP + e + 1]) * u_inv
                acts.append(hg * _sigmoid(hg) * hu * c)
            act, a_inv = _to_fp8(jnp.concatenate(acts, axis=1))
            d_inv = inv_ref[2 * EXPERTS_PER_GROUP:2 * EXPERTS_PER_GROUP + 1, 0:1] * a_inv
            y_ref[rs, :] = (_dot(act, wdb_ref[...]) * d_inv).astype(BF16)

        halves = [slice(b * PROJ_ROWS, (b + 1) * PROJ_ROWS) for b in range(MOE_TILE // PROJ_ROWS)]
        hidden = [up(rs) for rs in halves]
        for rs, hid in zip(halves, hidden):
            down(rs, hid)

    @pl.when(step >= nlive_ref[0])
    def _():
        fetch_next()
        y_ref[...] = jnp.zeros_like(y_ref)

    drain()


def _moe_call(src, g_step, n_live, stage, wg, wu, wd, n_steps):
    grp = lambda shape: pl.BlockSpec((EXPERTS_PER_GROUP,) + shape, lambda s, src, gs, nl: (gs[s], 0, 0))
    return pl.pallas_call(
        _moe_kernel,
        grid_spec=pltpu.PrefetchScalarGridSpec(
            num_scalar_prefetch=3,
            grid=(n_steps,),
            in_specs=[pl.BlockSpec(memory_space=pl.ANY),
                      grp((D_MODEL, EXPERT_FF)), grp((D_MODEL, EXPERT_FF)), grp((EXPERT_FF, D_MODEL))],
            out_specs=pl.BlockSpec((MOE_TILE, D_MODEL), lambda s, src, gs, nl: (s, 0)),
            scratch_shapes=[pltpu.VMEM((2, MOE_TILE, EXT_WIDTH), BF16), pltpu.SemaphoreType.DMA((2,)),
                            pltpu.VMEM((EXPERTS_PER_GROUP, D_MODEL, EXPERT_FF), FP8),
                            pltpu.VMEM((EXPERTS_PER_GROUP, D_MODEL, EXPERT_FF), FP8),
                            pltpu.VMEM((EXPERTS_PER_GROUP * EXPERT_FF, D_MODEL), FP8),
                            pltpu.VMEM((16, LANES), F32)],
        ),
        out_shape=jax.ShapeDtypeStruct((n_steps * MOE_TILE, D_MODEL), BF16),
        compiler_params=pltpu.CompilerParams(
            dimension_semantics=("arbitrary",), vmem_limit_bytes=VMEM_LIMIT),
        name="moe_call",
    )(src, g_step, n_live, stage, wg, wu, wd)


def _final_kernel(back_ref, ysort_ref, h_ref, dest_ref, g_ref, o_ref, buf_ref, sem_ref):
    step = pl.program_id(0)
    fetch_next, drain = _gather_pieces(back_ref, ysort_ref, buf_ref, sem_ref, SORTED_PIECES)
    fetch_next()
    dest = _stack_rows([dest_ref[0:1, :]], LANES).T[:, 0:1].astype(I32)
    unperm = jnp.where(lax.broadcasted_iota(I32, (TILE, SORTED_ROWS), 1) == dest, 1.0, 0.0).astype(BF16)
    rows = buf_ref[step % 2]
    for b in range(TILE // PROJ_ROWS):
        rs = slice(b * PROJ_ROWS, (b + 1) * PROJ_ROWS)
        o_ref[rs, :] = _rms_norm(h_ref[rs, :].astype(F32) + _dot(unperm[rs], rows), g_ref[...])
    drain()


def _final_call(back, ysort, h, dest, g):
    n_tok = h.shape[0]
    return pl.pallas_call(
        _final_kernel,
        grid_spec=pltpu.PrefetchScalarGridSpec(
            num_scalar_prefetch=1,
            grid=(n_tok // TILE,),
            in_specs=[pl.BlockSpec(memory_space=pl.ANY),
                      pl.BlockSpec((TILE, D_MODEL), lambda i, back: (i, 0)),
                      pl.BlockSpec((None, 8, TILE), lambda i, back: (i, 0, 0)),
                      pl.BlockSpec((1, D_MODEL), lambda i, back: (0, 0))],
            out_specs=pl.BlockSpec((TILE, D_MODEL), lambda i, back: (i, 0)),
            scratch_shapes=[pltpu.VMEM((2, SORTED_ROWS, D_MODEL), BF16), pltpu.SemaphoreType.DMA((2,))],
        ),
        out_shape=jax.ShapeDtypeStruct((n_tok, D_MODEL), F32),
        compiler_params=pltpu.CompilerParams(
            dimension_semantics=("arbitrary",), vmem_limit_bytes=VMEM_LIMIT),
        name="final_call",
    )(back, ysort, h, dest, g)


def _rope_tables(length):
    pos = np.arange(length, dtype=np.float32)
    inv_freq = np.float32(ROPE_BASE) ** (-np.arange(0, HEAD_DIM, 2, dtype=np.float32) / np.float32(HEAD_DIM))
    ang = pos[:, None] * inv_freq[None, :]
    cos, sin = np.cos(ang), np.sin(ang)
    return np.concatenate([cos, cos], axis=-1), np.concatenate([-sin, sin], axis=-1)


def _retention_tables():
    f32 = np.float32
    gamma = f32(1.0) - f32(2.0) ** (f32(-5.0) - np.arange(HEADS, dtype=f32))
    log_g = np.log(gamma)[:, None, None]
    scale = f32(HEAD_DIM ** -0.5)
    idx = np.arange(RET_BLOCK)
    dist = np.abs(idx[:, None] - idx[None, :]).astype(f32)
    visible = (idx[None, :] // CHUNK) <= (idx[:, None] // CHUNK)
    mask = np.where(visible[None], np.exp(log_g * dist[None]), f32(0.0)) * scale
    ones = np.ones((1, 1, HEAD_DIM), f32)
    idx_f = idx.astype(f32)[None, :, None]
    pair = lambda a: a.reshape(HEAD_PAIRS, 2, a.shape[1], HEAD_DIM).transpose(0, 2, 1, 3).reshape(
        HEAD_PAIRS, a.shape[1], PAIR_DIM)
    qdec = pair(np.exp(log_g * (idx_f + f32(1.0))) * ones)
    kdec = pair(np.exp(log_g * (f32(RET_BLOCK - 1.0) - idx_f)) * scale * ones)
    meta_idx = np.arange(N_META, dtype=f32)[None, :, None]
    kdec_meta = pair(np.exp(log_g * (f32(N_META - 1.0) - meta_idx)) * scale * ones)
    blk = np.kron(np.eye(2, dtype=f32), np.ones((HEAD_DIM, HEAD_DIM), f32))
    bdec = np.exp(log_g * f32(RET_BLOCK)).reshape(HEAD_PAIRS, 2)
    bdec = np.stack([np.kron(np.diag(b), np.ones((HEAD_DIM, HEAD_DIM), f32)) for b in bdec])
    return tuple(a.astype(f32) for a in (mask, qdec, kdec, bdec, blk, kdec_meta))


def kernel(x, meta_tokens, norm_mix_g, w_in, ssm_lambda_re, ssm_lambda_im, ssm_log_dt, ssm_b_re, ssm_b_im, ssm_c_re, ssm_c_im, ssm_d, w_glu, w_out, norm_ffn_g, w_router_group, b_router_group, w_router_expert, b_router_expert, w_gate, w_up, w_down, norm_final_g):
    bsz, seq, _ = x.shape
    assert seq % TILE == 0 and seq % MIXER_TILE == 0 and MIXER_TILE % RET_BLOCK == 0 and RET_BLOCK % CHUNK == 0
    n_blocks = seq // S5_BLOCK
    assert n_blocks % SCAN_ROWS == 0
    n_tok = bsz * seq
    n_tiles = n_tok // TILE
    n_steps = -(-n_tiles * (TILE_PIECES + N_GROUPS - 1) // MOE_PIECES) + N_GROUPS

    cos, sin = _rope_tables(N_META + seq)
    mask, qdec, kdec, bdec, blk, kdec_meta = _retention_tables()
    g_mix = norm_mix_g[0][None, :]
    u, y_ret, x_bf, u_meta = _mixer_call(x, g_mix, w_in[0], cos[N_META:], sin[N_META:], mask, qdec, kdec, bdec, blk,
                                   meta_tokens, cos[:N_META], sin[:N_META], kdec_meta)

    s5_ops = _s5_operators(
        ssm_lambda_re[0], ssm_lambda_im[0], ssm_log_dt[0], ssm_b_re[0], ssm_b_im[0],
        ssm_c_re[0], ssm_c_im[0], ssm_d[0])
    um = u_meta.reshape(S5_BLOCK, SSM_GROUPS, SSM_GROUP).transpose(1, 0, 2).reshape(SSM_GROUPS, 1, S5_LANES)
    um = jnp.pad(um, ((0, 0), (0, 7), (0, 0)))
    y_lo, y_hi = _s5_call(u, um, *s5_ops, w_glu[0].astype(BF16))

    w_r = jnp.concatenate(
        [w_router_group[0].T, w_router_expert[0].transpose(0, 2, 1).reshape(N_EXPERTS, D_MODEL)], axis=0)
    w_r = jnp.pad(w_r, ((0, ROUTE_ROWS - w_r.shape[0]), (0, 0)))
    b_r = jnp.concatenate([b_router_group[0], b_router_expert[0].reshape(-1)])
    b_r = jnp.pad(b_r, (0, ROUTE_ROWS - b_r.shape[0]))[:, None]
    w_r_hi = w_r.astype(BF16)
    w_r = jnp.concatenate([w_r_hi, (w_r - w_r_hi.astype(F32)).astype(BF16)], axis=0)
    tri = jnp.asarray(np.arange(TILE)[:, None] <= np.arange(TILE)[None, :], BF16)

    h, stage, dest, cnt = _proj_call(
        x_bf.reshape(n_tok, D_MODEL), y_lo.reshape(n_tok, LANES), y_hi.reshape(n_tok, LANES),
        y_ret.reshape(n_tok, RET_WIDTH),
        w_out[0], norm_ffn_g[0][None, :], w_r, b_r, tri)
    src, g_step, n_live, back = _sort_tables(cnt[:, :N_GROUPS, 0].astype(I32), n_steps)
    y_sorted = _moe_call(src, g_step, n_live, stage, w_gate[0], w_up[0], w_down[0], n_steps)
    out = _final_call(back, y_sorted, h, dest, norm_final_g[None, :])
    return out.reshape(bsz, seq, D_MODEL)
```

```python
import functools

import jax
import jax.numpy as jnp
import numpy as np
from jax import lax
from jax.experimental import pallas as pl
from jax.experimental.pallas import tpu as pltpu

D_MODEL = 1024
N_META = 16
CHUNK = 64
EPS = 1e-6
SSM_WIDTH = 256
SSM_GROUP = 16
SSM_GROUPS = 16
SSM_STATE = 64
RET_WIDTH = 768
HEAD_DIM = 128
HEADS = 6
HEAD_PAIRS = HEADS // 2
PAIR_DIM = 2 * HEAD_DIM
ROPE_BASE = 10000.0
IN_WIDTH = SSM_WIDTH + 4 * RET_WIDTH
N_GROUPS = 4
EXPERTS_PER_GROUP = 4
N_EXPERTS = 16
EXPERT_FF = 256

S5_BLOCK = 16
S5_LANES = S5_BLOCK * SSM_GROUP
SCAN_ROWS = 8
RET_BLOCK = 256
TILE = 512
MIXER_TILE = 1024
LANES = 128
ROUTE_ROWS = 32
ROW_CHUNK = 256
FINAL_TILES = 2
PIECE = 16
TILE_PIECES = TILE // PIECE
MOE_TILE = 1024
MOE_PIECES = MOE_TILE // PIECE
SORTED_PIECES = TILE_PIECES + N_GROUPS
SORTED_ROWS = SORTED_PIECES * PIECE
EXT_WIDTH = D_MODEL + LANES
VMEM_LIMIT = 56 * 1024 * 1024

F32 = jnp.float32
BF16 = jnp.bfloat16
FP8 = jnp.float8_e4m3fn
FP8_HEADROOM = 256.0
I32 = jnp.int32


def _dot(a, b):
    return jnp.dot(a, b, preferred_element_type=F32)


def _sigmoid(x):
    return 1.0 / (1.0 + jnp.exp(-x))


def _rms_norm(x, g):
    return x * lax.rsqrt(jnp.mean(x * x, axis=-1, keepdims=True) + EPS) * g


def _rope(t, cos, sin_signed):
    return t * cos + pltpu.roll(t, HEAD_DIM // 2, axis=1) * sin_signed


def _meta_state(meta_ref, g_ref, w_ref, cos_ref, sin_ref, kdec_ref, blk_ref, u_ref, r0_ref):
    a = _rms_norm(meta_ref[...], g_ref[...]).astype(BF16)
    u_ref[...] = _dot(a, w_ref[:, :SSM_WIDTH])
    k_off = SSM_WIDTH + RET_WIDTH
    v_off = SSM_WIDTH + 2 * RET_WIDTH
    cos = cos_ref[...]
    sin = sin_ref[...]
    for p in range(HEAD_PAIRS):
        k = _dot(a, w_ref[:, k_off + p * PAIR_DIM:k_off + (p + 1) * PAIR_DIM])
        v = _dot(a, w_ref[:, v_off + p * PAIR_DIM:v_off + (p + 1) * PAIR_DIM])
        k = jnp.concatenate([_rope(k[:, :HEAD_DIM], cos, sin), _rope(k[:, HEAD_DIM:], cos, sin)], axis=1)
        kd = (k * kdec_ref[p]).astype(BF16)
        r0_ref[p] = _dot_rows(kd, v.astype(BF16)) * blk_ref[...]


def _dot_rows(a, b):
    return lax.dot_general(a, b, (((0,), (0,)), ((), ())), preferred_element_type=F32)


def _mixer_kernel(x_ref, g_ref, w32_ref, cos_ref, sin_ref, mask_ref, qdec_ref, kdec_ref, bdec_ref, blk_ref,
                  meta_ref, cos_m_ref, sin_m_ref, kdec_m_ref, u_ref, y_ref, um_ref, w_ref, r0_ref, r_ref):
    first_tile = pl.program_id(1) == 0

    @pl.when(jnp.logical_and(pl.program_id(0) == 0, first_tile))
    def _():
        for c in range(0, IN_WIDTH, SSM_WIDTH):
            w_ref[:, c:c + SSM_WIDTH] = w32_ref[:, c:c + SSM_WIDTH].astype(BF16)
        _meta_state(meta_ref, g_ref, w_ref, cos_m_ref, sin_m_ref, kdec_m_ref, blk_ref, um_ref, r0_ref)

    @pl.when(first_tile)
    def _():
        r_ref[...] = r0_ref[...]

    off = SSM_WIDTH
    for b in range(MIXER_TILE // RET_BLOCK):
        bs = slice(b * RET_BLOCK, (b + 1) * RET_BLOCK)
        a = _rms_norm(x_ref[bs, :], g_ref[...]).astype(BF16)
        u_ref[bs, :] = _dot(a, w_ref[:, :SSM_WIDTH])
        q_all = _dot(a, w_ref[:, off:off + RET_WIDTH])
        k_all = _dot(a, w_ref[:, off + RET_WIDTH:off + 2 * RET_WIDTH])
        v_all = _dot(a, w_ref[:, off + 2 * RET_WIDTH:off + 3 * RET_WIDTH])
        gate = _dot(a, w_ref[:, off + 3 * RET_WIDTH:off + 4 * RET_WIDTH])
        cos = cos_ref[bs, :]
        sin = sin_ref[bs, :]
        def scores(h):
            hs = slice(h * HEAD_DIM, (h + 1) * HEAD_DIM)
            q = _rope(q_all[:, hs], cos, sin)
            k = _rope(k_all[:, hs], cos, sin)
            return q, k, _dot_t(q.astype(BF16), k.astype(BF16))

        ahead = scores(0)
        for p in range(HEAD_PAIRS):
            ps = slice(p * PAIR_DIM, (p + 1) * PAIR_DIM)
            pair = [ahead, scores(2 * p + 1)]
            if p + 1 < HEAD_PAIRS:
                ahead = scores(2 * p + 2)
            q = jnp.concatenate([pair[0][0], pair[1][0]], axis=1)
            k = jnp.concatenate([pair[0][1], pair[1][1]], axis=1)
            v = v_all[:, ps].astype(BF16)
            state = r_ref[p]
            cross = _dot((q * qdec_ref[p]).astype(BF16), state.astype(BF16))
            kv = _dot_rows((k * kdec_ref[p]).astype(BF16), v)
            r_ref[p] = state * bdec_ref[p] + kv * blk_ref[...]
            for half, (_, _, s) in enumerate(pair):
                h = 2 * p + half
                hs = slice(h * HEAD_DIM, (h + 1) * HEAD_DIM)
                ls = slice(half * HEAD_DIM, (half + 1) * HEAD_DIM)
                o = _dot((s * mask_ref[h]).astype(BF16), v[:, ls]) + cross[:, ls]
                mu = jnp.mean(o, axis=-1, keepdims=True)
                d = o - mu
                var = jnp.mean(d * d, axis=-1, keepdims=True)
                gt = gate[:, hs]
                y_ref[bs, hs] = (gt * _sigmoid(gt) * d * lax.rsqrt(var + EPS)).astype(BF16)


def _mixer_call(x, g, w_in, cos, sin, mask, qdec, kdec, bdec, blk, meta, cos_m, sin_m, kdec_m):
    bsz, seq, _ = x.shape
    const = lambda a, **kw: pl.BlockSpec(a.shape, lambda b, i: (0,) * a.ndim, **kw)
    return pl.pallas_call(
        _mixer_kernel,
        grid=(bsz, seq // MIXER_TILE),
        in_specs=[
            pl.BlockSpec((None, MIXER_TILE, D_MODEL), lambda b, i: (b, i, 0)),
            const(g),
            const(w_in, pipeline_mode=pl.Buffered(1)),
            pl.BlockSpec((MIXER_TILE, HEAD_DIM), lambda b, i: (i, 0)),
            pl.BlockSpec((MIXER_TILE, HEAD_DIM), lambda b, i: (i, 0)),
            const(mask), const(qdec), const(kdec), const(bdec), const(blk),
            const(meta), const(cos_m), const(sin_m), const(kdec_m),
        ],
        out_specs=(
            pl.BlockSpec((None, MIXER_TILE, SSM_WIDTH), lambda b, i: (b, i, 0)),
            pl.BlockSpec((None, MIXER_TILE, RET_WIDTH), lambda b, i: (b, i, 0)),
            pl.BlockSpec((N_META, SSM_WIDTH), lambda b, i: (0, 0)),
        ),
        out_shape=(jax.ShapeDtypeStruct((bsz, seq, SSM_WIDTH), F32),
                   jax.ShapeDtypeStruct((bsz, seq, RET_WIDTH), BF16),
                   jax.ShapeDtypeStruct((N_META, SSM_WIDTH), F32)),
        scratch_shapes=[pltpu.VMEM((D_MODEL, IN_WIDTH), BF16),
                        pltpu.VMEM((HEAD_PAIRS, PAIR_DIM, PAIR_DIM), F32),
                        pltpu.VMEM((HEAD_PAIRS, PAIR_DIM, PAIR_DIM), F32)],
        compiler_params=pltpu.CompilerParams(
            dimension_semantics=("arbitrary", "arbitrary"), vmem_limit_bytes=VMEM_LIMIT),
        name="mixer_call",
    )(x, g, w_in, cos, sin, mask, qdec, kdec, bdec, blk, meta, cos_m, sin_m, kdec_m)


def _dot_t(a, b):
    return lax.dot_general(a, b, (((1,), (1,)), ((), ())), preferred_element_type=F32)


def _chunk_transpose(arrs):
    n = len(arrs)
    chunk = lax.broadcasted_iota(I32, (1, LANES), 1) // SSM_GROUP
    arrs = list(arrs)
    s = n // 2
    while s:
        keep = (chunk & s) == 0
        nxt = list(arrs)
        for i in range(n):
            if i & s == 0:
                lo, hi = arrs[i], arrs[i + s]
                nxt[i] = jnp.where(keep, lo, pltpu.roll(hi, s * SSM_GROUP, axis=1))
                nxt[i + s] = jnp.where(keep, pltpu.roll(lo, LANES - s * SSM_GROUP, axis=1), hi)
        arrs = nxt
        s //= 2
    return arrs


def _s5_kernel(u_lo_ref, u_hi_ref, um_ref, krow_ref, bmat_ref, cre_ref, cim_ref, ar_ref, ai_ref, wglu_ref,
               y_lo_ref, y_hi_ref, t0_ref, ug_ref, yg_ref):
    n_blocks = u_lo_ref.shape[0] // S5_BLOCK

    @pl.when(pl.program_id(0) == 0)
    def _():
        lane = lax.broadcasted_iota(I32, (SSM_GROUP, S5_LANES), 1)
        for g in range(SSM_GROUPS):
            k0 = krow_ref[g]
            for a in range(S5_BLOCK):
                blk = k0 if a == 0 else jnp.where(lane >= a * SSM_GROUP,
                                                  pltpu.roll(k0, a * SSM_GROUP, axis=1), 0.0)
                t0_ref[g, a * SSM_GROUP:(a + 1) * SSM_GROUP, :] = blk.astype(BF16)

    per_tile = LANES // SSM_GROUP
    for half, uh_ref in enumerate((u_lo_ref, u_hi_ref)):
        for t in range(S5_BLOCK // per_tile):
            words = [pltpu.bitcast(uh_ref[pl.ds(t * per_tile + k, n_blocks, stride=S5_BLOCK), :].astype(BF16),
                                   jnp.uint32) for k in range(per_tile)]
            for m, w in enumerate(_chunk_transpose(words)):
                ug_ref[half * per_tile + m, :, t * LANES:(t + 1) * LANES] = pltpu.bitcast(w, BF16)

    re, im, re0, im0 = [], [], [], []
    for p in range(SSM_GROUPS // 2):
        g0, g1 = 2 * p, 2 * p + 1
        v = _dot(ug_ref[g0], bmat_ref[g0]) + _dot(ug_ref[g1], bmat_ref[g1])
        v0 = (_dot(um_ref[g0].astype(BF16), bmat_ref[g0]) + _dot(um_ref[g1].astype(BF16), bmat_ref[g1]))[0:1]
        re.append(v[:, :LANES])
        im.append(v[:, LANES:])
        re0.append(v0[:, :LANES])
        im0.append(v0[:, LANES:])
    re, im, re0, im0 = (jnp.concatenate(parts, axis=1) for parts in (re, im, re0, im0))

    row = lax.broadcasted_iota(I32, re.shape, 0)
    ar, ai = ar_ref[0:1], ai_ref[0:1]
    re = re + jnp.where(row == 0, ar * re0 - ai * im0, 0.0)
    im = im + jnp.where(row == 0, ar * im0 + ai * re0, 0.0)
    d = 1
    while d < SCAN_ROWS:
        ar, ai = ar_ref[d - 1:d], ai_ref[d - 1:d]
        inside = row % SCAN_ROWS >= d
        sre = jnp.where(inside, pltpu.roll(re, d, axis=0), 0.0)
        sim = jnp.where(inside, pltpu.roll(im, d, axis=0), 0.0)
        re, im = re + ar * sre - ai * sim, im + ar * sim + ai * sre
        d *= 2
    ar, ai = ar_ref[...], ai_ref[...]
    re_tiles, im_tiles = [re[:SCAN_ROWS]], [im[:SCAN_ROWS]]
    for r in range(1, n_blocks // SCAN_ROWS):
        cre_ = re_tiles[-1][SCAN_ROWS - 1:SCAN_ROWS]
        cim_ = im_tiles[-1][SCAN_ROWS - 1:SCAN_ROWS]
        rs = slice(r * SCAN_ROWS, (r + 1) * SCAN_ROWS)
        re_tiles.append(re[rs] + ar * cre_ - ai * cim_)
        im_tiles.append(im[rs] + ar * cim_ + ai * cre_)
    re = jnp.concatenate(re_tiles, axis=0)
    im = jnp.concatenate(im_tiles, axis=0)
    pre = jnp.where(row == 0, re0, pltpu.roll(re, 1, axis=0)).astype(BF16)
    pim = jnp.where(row == 0, im0, pltpu.roll(im, 1, axis=0)).astype(BF16)

    for g in range(SSM_GROUPS):
        ps = slice((g // 2) * LANES, (g // 2 + 1) * LANES)
        yg_ref[g] = (_dot(ug_ref[g], t0_ref[g]) + _dot_t(pre[:, ps], cre_ref[g])
                     + _dot_t(pim[:, ps], cim_ref[g]))

    wglu = wglu_ref[...]
    for t in range(S5_BLOCK // per_tile):
        ts = slice(t * LANES, (t + 1) * LANES)
        halves = [_chunk_transpose([yg_ref[half * per_tile + m, :, ts] for m in range(per_tile)])
                  for half in range(SSM_GROUPS // per_tile)]
        for k in range(per_tile):
            y = jnp.concatenate([h[k] for h in halves], axis=1)
            y = jax.nn.gelu(y, approximate=True)
            y = y * _sigmoid(_dot(y.astype(BF16), wglu))
            i = t * per_tile + k
            y_lo_ref[pl.ds(i, n_blocks, stride=S5_BLOCK), :] = y[:, :LANES]
            y_hi_ref[pl.ds(i, n_blocks, stride=S5_BLOCK), :] = y[:, LANES:]


def _s5_call(u, um, krow, bmat, cre, cim, ar, ai, wglu):
    bsz, seq, _ = u.shape
    n_blocks = seq // S5_BLOCK
    const = lambda a: pl.BlockSpec(a.shape, lambda b: (0,) * a.ndim)
    return pl.pallas_call(
        _s5_kernel,
        grid=(bsz,),
        in_specs=[pl.BlockSpec((None, seq, LANES), lambda b: (b, 0, 0)),
                  pl.BlockSpec((None, seq, LANES), lambda b: (b, 0, 1)),
                  const(um), const(krow), const(bmat), const(cre), const(cim), const(ar), const(ai),
                  const(wglu)],
        out_specs=(pl.BlockSpec((None, seq, LANES), lambda b: (b, 0, 0)),
                   pl.BlockSpec((None, seq, LANES), lambda b: (b, 0, 0))),
        out_shape=(jax.ShapeDtypeStruct((bsz, seq, LANES), F32),
                   jax.ShapeDtypeStruct((bsz, seq, LANES), F32)),
        scratch_shapes=[pltpu.VMEM((SSM_GROUPS, S5_LANES, S5_LANES), BF16),
                        pltpu.VMEM((SSM_GROUPS, n_blocks, S5_LANES), BF16),
                        pltpu.VMEM((SSM_GROUPS, n_blocks, S5_LANES), F32)],
        compiler_params=pltpu.CompilerParams(
            dimension_semantics=("arbitrary",), vmem_limit_bytes=VMEM_LIMIT),
        name="s5_call",
    )(u, u, um, krow, bmat, cre, cim, ar, ai, wglu)


def _s5_operators(lam_re, lam_im, log_dt, b_re, b_im, c_re, c_im, d_skip):
    n_groups = lam_re.shape[0]
    lam = lax.complex(lam_re, lam_im)
    lam_dt = lam * jnp.exp(log_dt)[:, None]
    lam_bar = jnp.exp(lam_dt)
    b_bar = ((lam_bar - 1.0) / lam)[..., None] * lax.complex(b_re, b_im)
    c = lax.complex(c_re, c_im)
    tau = jnp.arange(S5_BLOCK + 1, dtype=F32)
    pows = jnp.exp(lam_dt[None] * tau[:, None, None])
    kern = jnp.real(jnp.einsum('ghp,tgp,gpk->gkth', c, pows[:S5_BLOCK], b_bar))
    skip = (jnp.eye(SSM_GROUP, dtype=F32)[None, :, None, :] * d_skip[:, None, None, :]
            * (tau[:S5_BLOCK] == 0).astype(F32)[None, None, :, None])
    krow = (kern + skip).reshape(n_groups, SSM_GROUP, S5_LANES)
    even = (jnp.arange(n_groups) % 2 == 0)[:, None, None]
    pair_pad = lambda m: jnp.concatenate([jnp.where(even, m, 0.0), jnp.where(even, 0.0, m)], axis=-1)
    bm = (pows[S5_BLOCK - 1 - jnp.arange(S5_BLOCK)].transpose(1, 0, 2)[:, :, None, :]
          * b_bar.transpose(0, 2, 1)[:, None, :, :]).reshape(n_groups, S5_LANES, SSM_STATE)
    bmat = jnp.concatenate([pair_pad(jnp.real(bm)), pair_pad(jnp.imag(bm))], axis=-1)
    cm = (pows[1:].transpose(1, 0, 2)[:, :, None, :] * c[:, None, :, :]).reshape(n_groups, S5_LANES, SSM_STATE)
    cre, cim = pair_pad(jnp.real(cm)), pair_pad(-jnp.imag(cm))
    step = S5_BLOCK * (1.0 + jnp.arange(SCAN_ROWS, dtype=F32))
    adec = jnp.exp(lam_dt[None, :, :] * step[:, None, None]).reshape(SCAN_ROWS, n_groups * SSM_STATE)
    return krow, bmat.astype(BF16), cre.astype(BF16), cim.astype(BF16), jnp.real(adec), jnp.imag(adec)


def _first_hit(values, target):
    hits, taken = [], None
    for v in values:
        hit = (v >= target) if taken is None else jnp.logical_and(v >= target, jnp.logical_not(taken))
        taken = hit if taken is None else jnp.logical_or(taken, hit)
        hits.append(hit)
    return hits


def _stack_rows(rows, n_rows):
    idx = lax.broadcasted_iota(I32, (n_rows, rows[0].shape[1]), 0)
    out = jnp.zeros((n_rows, rows[0].shape[1]), F32)
    for k, r in enumerate(rows):
        out = jnp.where(idx == k, r, out)
    return out


def _project(rs, x_ref, ys_lo_ref, ys_hi_ref, yr_ref, wout_ref, h_ref):
    ys = jnp.concatenate([ys_lo_ref[rs, :], ys_hi_ref[rs, :]], axis=1).astype(BF16)
    h = x_ref[rs, :] + _dot(ys, wout_ref[:SSM_WIDTH, :]) + _dot(yr_ref[rs, :], wout_ref[SSM_WIDTH:, :])
    h_ref[rs, :] = h.astype(BF16)
    return h


def _route(lt, tri_ref):
    gl = [lt[g:g + 1, :] for g in range(N_GROUPS)]
    gmax = functools.reduce(jnp.maximum, gl)
    g_w = 1.0 / functools.reduce(lambda a, b: a + b, [jnp.exp(l - gmax) for l in gl])
    sel = _first_hit(gl, gmax)
    ev = []
    for e in range(EXPERTS_PER_GROUP):
        acc = jnp.zeros_like(gmax)
        for g in range(N_GROUPS):
            k = N_GROUPS + g * EXPERTS_PER_GROUP + e
            acc = jnp.where(sel[g], lt[k:k + 1, :], acc)
        ev.append(acc)
    m1 = functools.reduce(jnp.maximum, ev)
    first = _first_hit(ev, m1)
    rest = [jnp.where(f, -jnp.inf, v) for f, v in zip(first, ev)]
    m2 = functools.reduce(jnp.maximum, rest)
    second = _first_hit(rest, m2)
    e2 = jnp.exp(m2 - m1)
    w1 = g_w / (1.0 + e2)
    w2 = e2 * w1
    combine = [jnp.where(f, w1, 0.0) + jnp.where(s, w2, 0.0) for f, s in zip(first, second)]

    sel_f = [jnp.where(s, 1.0, 0.0) for s in sel]
    incl = _dot(_stack_rows(sel_f, 8).astype(BF16), tri_ref[...])
    dest = jnp.zeros_like(gmax)
    seg_start = jnp.zeros((1, 1), F32)
    counts = []
    for g in range(N_GROUPS):
        run = incl[g:g + 1, :]
        cnt = run[:, TILE - 1:TILE]
        counts.append(cnt)
        dest = dest + sel_f[g] * (seg_start + run - 1.0)
        seg_start = seg_start + PIECE * jnp.floor((cnt + (PIECE - 1.0)) * (1.0 / PIECE))
    return combine, dest, counts


def _proj_kernel(x_ref, ys_lo_ref, ys_hi_ref, yr_ref, wout32_ref, g_ref, wr_ref, br_ref,
                 tri_ref, h_ref, stage_ref, dest_ref, cnt_ref, wout_ref):
    @pl.when(pl.program_id(0) == 0)
    def _():
        wout_ref[...] = wout32_ref[...].astype(BF16)

    chunks = [slice(b * ROW_CHUNK, (b + 1) * ROW_CHUNK) for b in range(TILE // ROW_CHUNK)]
    proj_refs = (x_ref, ys_lo_ref, ys_hi_ref, yr_ref, wout_ref, h_ref)
    h_parts = [_project(rs, *proj_refs) for rs in chunks]
    t_parts = []
    for h in h_parts:
        t = _rms_norm(h, g_ref[...])
        t_hi = t.astype(BF16)
        t_parts.append((t_hi, (t - t_hi.astype(F32)).astype(BF16)))
    lt_parts = []
    for t_hi, t_lo in t_parts:
        both = _dot_t(wr_ref[...], t_hi)
        lt_parts.append(both[:ROUTE_ROWS] + both[ROUTE_ROWS:] + _dot_t(wr_ref[:ROUTE_ROWS, :], t_lo))
    lt = jnp.concatenate(lt_parts, axis=1) + br_ref[...]

    combine, dest, counts = _route(lt, tri_ref)
    dest_ref[...] = _stack_rows([dest], 8)
    cnt_ref[...] = _stack_rows([c + jnp.zeros((1, LANES), F32) for c in counts], 8)
    perm = jnp.where(lax.broadcasted_iota(I32, (SORTED_ROWS, TILE), 0) == dest.astype(I32),
                     1.0, 0.0).astype(BF16)
    c_hi = [c.astype(BF16).astype(F32) for c in combine]
    c_lo = [c - hi for c, hi in zip(combine, c_hi)]
    cw = _stack_rows(c_hi + c_lo, LANES).T.astype(BF16)
    t_ext = jnp.concatenate([jnp.concatenate([t for t, _ in t_parts], axis=0), cw], axis=1)
    stage_ref[...] = _dot(perm, t_ext).astype(BF16)


def _proj_call(x, ys_lo, ys_hi, yr, wout, g, wr, br, tri):
    n_tok = x.shape[0]
    n_tiles = n_tok // TILE
    const = lambda *shape: pl.BlockSpec(shape, lambda i: (0,) * len(shape))
    rows = lambda width: pl.BlockSpec((TILE, width), lambda i: (i, 0))
    return pl.pallas_call(
        _proj_kernel,
        grid=(n_tiles,),
        in_specs=[
            rows(D_MODEL), rows(LANES), rows(LANES), rows(RET_WIDTH),
            pl.BlockSpec((D_MODEL, D_MODEL), lambda i: (0, 0), pipeline_mode=pl.Buffered(1)), const(1, D_MODEL),
            const(2 * ROUTE_ROWS, D_MODEL), const(ROUTE_ROWS, 1), const(TILE, TILE),
        ],
        out_specs=(rows(D_MODEL),
                   pl.BlockSpec((SORTED_ROWS, EXT_WIDTH), lambda i: (i, 0)),
                   pl.BlockSpec((None, 8, TILE), lambda i: (i, 0, 0)),
                   pl.BlockSpec((None, 8, LANES), lambda i: (i, 0, 0))),
        scratch_shapes=[pltpu.VMEM((D_MODEL, D_MODEL), BF16)],
        out_shape=(jax.ShapeDtypeStruct((n_tok, D_MODEL), BF16),
                   jax.ShapeDtypeStruct((n_tiles * SORTED_ROWS, EXT_WIDTH), BF16),
                   jax.ShapeDtypeStruct((n_tiles, 8, TILE), F32),
                   jax.ShapeDtypeStruct((n_tiles, 8, LANES), F32)),
        compiler_params=pltpu.CompilerParams(
            dimension_semantics=("arbitrary",), vmem_limit_bytes=VMEM_LIMIT),
        name="proj_call",
    )(x, ys_lo, ys_hi, yr, wout, g, wr, br, tri)


def _sort_tables(cnt, n_steps):
    n_tiles = cnt.shape[0]
    npc = (cnt + PIECE - 1) // PIECE
    seg = jnp.cumsum(npc, axis=1) - npc
    before = jnp.cumsum(npc, axis=0) - npc
    n_tile_g = (jnp.sum(npc, axis=0) + MOE_PIECES - 1) // MOE_PIECES
    t_off = jnp.cumsum(n_tile_g) - n_tile_g
    j = jnp.arange(SORTED_PIECES, dtype=I32)[None, :, None]
    in_g = jnp.logical_and(j >= seg[:, None, :], j < (seg + npc)[:, None, :])
    pos = jnp.sum(jnp.where(in_g, MOE_PIECES * t_off[None, None, :] + before[:, None, :] + j - seg[:, None, :], 0),
                  axis=-1)
    valid = jnp.any(in_g, axis=-1)
    stage_piece = jnp.arange(n_tiles * SORTED_PIECES, dtype=I32).reshape(n_tiles, SORTED_PIECES)
    n_slots = n_steps * MOE_PIECES
    src = jnp.zeros((n_slots,), I32).at[jnp.where(valid, pos, n_slots).reshape(-1)].set(
        stage_piece.reshape(-1), mode='drop')
    steps = jnp.arange(n_steps, dtype=I32)
    g_step = jnp.minimum(jnp.sum(steps[:, None] >= (t_off + n_tile_g)[None, :], axis=1), N_GROUPS - 1)
    n_live = jnp.sum(n_tile_g).reshape(1)
    back = jnp.where(valid, pos, 0).reshape(-1)
    return src, g_step.astype(I32), n_live.astype(I32), back.astype(I32)


def _piece_copy(src_ref, piece, buf_ref, slot, j, sem_ref):
    start = piece * PIECE if isinstance(piece, int) else pl.multiple_of(piece * PIECE, PIECE)
    return pltpu.make_async_copy(src_ref.at[pl.ds(start, PIECE)],
                                 buf_ref.at[slot, pl.ds(j * PIECE, PIECE)], sem_ref.at[slot])


def _fetch_pieces(table_ref, src_ref, buf_ref, sem_ref, row, slot, n_pieces):
    for j in range(n_pieces):
        _piece_copy(src_ref, table_ref[row * n_pieces + j], buf_ref, slot, j, sem_ref).start()


def _wait_pieces(src_ref, buf_ref, sem_ref, slot, n_pieces):
    for j in range(n_pieces):
        _piece_copy(src_ref, 0, buf_ref, slot, j, sem_ref).wait()


def _gather_pieces(table_ref, src_ref, buf_ref, sem_ref, n_pieces):
    step = pl.program_id(0)
    last = pl.num_programs(0) - 1
    args = (src_ref, buf_ref, sem_ref)

    @pl.when(step == 0)
    def _():
        _fetch_pieces(table_ref, *args, step, 0, n_pieces)

    _wait_pieces(*args, step % 2, n_pieces)

    def fetch_next():
        _fetch_pieces(table_ref, *args, jnp.minimum(step + 1, last), (step + 1) % 2, n_pieces)

    def drain():
        @pl.when(step == last)
        def _():
            _wait_pieces(*args, (step + 1) % 2, n_pieces)

    return fetch_next, drain


def _to_fp8(x, headroom=FP8_HEADROOM):
    peak = jnp.maximum(jnp.max(jnp.abs(x), axis=(0, 1), keepdims=True), 1e-30)
    return (x * (headroom / peak)).astype(FP8), peak * (1.0 / headroom)


def _moe_kernel(src_ref, gstep_ref, nlive_ref, stage_ref, wg_ref, wu_ref, wd_ref, y_ref,
                buf_ref, sem_ref, wgb_ref, wub_ref, wdb_ref, inv_ref):
    step = pl.program_id(0)
    fetch_next, drain = _gather_pieces(src_ref, stage_ref, buf_ref, sem_ref, MOE_PIECES)

    @pl.when(jnp.logical_or(step == 0, gstep_ref[step] != gstep_ref[jnp.maximum(step - 1, 0)]))
    def _():
        ones = jnp.ones((1, LANES), F32)
        for e in range(EXPERTS_PER_GROUP):
            wgb_ref[e], inv = _to_fp8(wg_ref[e])
            inv_ref[e:e + 1, :] = inv * ones
            wub_ref[e], inv = _to_fp8(wu_ref[e])
            inv_ref[EXPERTS_PER_GROUP + e:EXPERTS_PER_GROUP + e + 1, :] = inv * ones
        peak = functools.reduce(jnp.maximum, [jnp.max(jnp.abs(wd_ref[e]), axis=(0, 1), keepdims=True)
                                              for e in range(EXPERTS_PER_GROUP)])
        peak = jnp.maximum(peak, 1e-30)
        for e in range(EXPERTS_PER_GROUP):
            wdb_ref[e * EXPERT_FF:(e + 1) * EXPERT_FF, :] = (wd_ref[e] * (FP8_HEADROOM / peak)).astype(FP8)
        inv_ref[2 * EXPERTS_PER_GROUP:2 * EXPERTS_PER_GROUP + 1, :] = peak * (1.0 / FP8_HEADROOM) * ones

    @pl.when(step < nlive_ref[0])
    def _():
        slot = step % 2
        fetch_next()

        def up(rs):
            t, t_inv = _to_fp8(buf_ref[slot, rs, :D_MODEL].astype(F32))
            out = []
            for e in range(EXPERTS_PER_GROUP):
                g_inv = inv_ref[e:e + 1, 0:1] * t_inv
                u_inv = inv_ref[EXPERTS_PER_GROUP + e:EXPERTS_PER_GROUP + e + 1, 0:1] * t_inv
                out.append((_dot(t, wgb_ref[e]) * g_inv, _dot(t, wub_ref[e]), u_inv))
            return out

        def down(rs, hidden):
            cw = buf_ref[slot, rs, D_MODEL:].astype(F32)
            acts = []
            for e, (hg, hu, u_inv) in enumerate(hidden):
                c = (cw[:, e:e + 1] + cw[:, EXPERTS_PER_GROUP + e:EXPERTS_PER_GROUP + e + 1]) * u_inv
                acts.append(hg * _sigmoid(hg) * hu * c)
            act, a_inv = _to_fp8(jnp.concatenate(acts, axis=1))
            d_inv = inv_ref[2 * EXPERTS_PER_GROUP:2 * EXPERTS_PER_GROUP + 1, 0:1] * a_inv
            y_ref[rs, :] = (_dot(act, wdb_ref[...]) * d_inv).astype(BF16)

        chunks = [slice(b * ROW_CHUNK, (b + 1) * ROW_CHUNK) for b in range(MOE_TILE // ROW_CHUNK)]
        hidden = [up(rs) for rs in chunks]
        for rs, hid in zip(chunks, hidden):
            down(rs, hid)

    @pl.when(step >= nlive_ref[0])
    def _():
        fetch_next()
        y_ref[...] = jnp.zeros_like(y_ref)

    drain()


def _moe_call(src, g_step, n_live, stage, wg, wu, wd, n_steps):
    grp = lambda shape: pl.BlockSpec((EXPERTS_PER_GROUP,) + shape, lambda s, src, gs, nl: (gs[s], 0, 0))
    return pl.pallas_call(
        _moe_kernel,
        grid_spec=pltpu.PrefetchScalarGridSpec(
            num_scalar_prefetch=3,
            grid=(n_steps,),
            in_specs=[pl.BlockSpec(memory_space=pl.ANY),
                      grp((D_MODEL, EXPERT_FF)), grp((D_MODEL, EXPERT_FF)), grp((EXPERT_FF, D_MODEL))],
            out_specs=pl.BlockSpec((MOE_TILE, D_MODEL), lambda s, src, gs, nl: (s, 0)),
            scratch_shapes=[pltpu.VMEM((2, MOE_TILE, EXT_WIDTH), BF16), pltpu.SemaphoreType.DMA((2,)),
                            pltpu.VMEM((EXPERTS_PER_GROUP, D_MODEL, EXPERT_FF), FP8),
                            pltpu.VMEM((EXPERTS_PER_GROUP, D_MODEL, EXPERT_FF), FP8),
                            pltpu.VMEM((EXPERTS_PER_GROUP * EXPERT_FF, D_MODEL), FP8),
                            pltpu.VMEM((16, LANES), F32)],
        ),
        out_shape=jax.ShapeDtypeStruct((n_steps * MOE_TILE, D_MODEL), BF16),
        compiler_params=pltpu.CompilerParams(
            dimension_semantics=("arbitrary",), vmem_limit_bytes=VMEM_LIMIT),
        name="moe_call",
    )(src, g_step, n_live, stage, wg, wu, wd)


def _final_kernel(back_ref, ysort_ref, h_ref, dest_ref, g_ref, o_ref, buf_ref, sem_ref):
    step = pl.program_id(0)
    fetch_next, drain = _gather_pieces(back_ref, ysort_ref, buf_ref, sem_ref, FINAL_TILES * SORTED_PIECES)
    fetch_next()
    for t in range(FINAL_TILES):
        dest = _stack_rows([dest_ref[t, 0:1, :]], LANES).T[:, 0:1].astype(I32)
        unperm = jnp.where(lax.broadcasted_iota(I32, (TILE, SORTED_ROWS), 1) == dest, 1.0, 0.0).astype(BF16)
        rows = buf_ref[step % 2, t * SORTED_ROWS:(t + 1) * SORTED_ROWS, :]
        for b in range(TILE // ROW_CHUNK):
            rs = slice(b * ROW_CHUNK, (b + 1) * ROW_CHUNK)
            os = slice(t * TILE + b * ROW_CHUNK, t * TILE + (b + 1) * ROW_CHUNK)
            o_ref[os, :] = _rms_norm(h_ref[os, :].astype(F32) + _dot(unperm[rs], rows), g_ref[...])
    drain()


def _final_call(back, ysort, h, dest, g):
    n_tok = h.shape[0]
    return pl.pallas_call(
        _final_kernel,
        grid_spec=pltpu.PrefetchScalarGridSpec(
            num_scalar_prefetch=1,
            grid=(n_tok // (FINAL_TILES * TILE),),
            in_specs=[pl.BlockSpec(memory_space=pl.ANY),
                      pl.BlockSpec((FINAL_TILES * TILE, D_MODEL), lambda i, back: (i, 0)),
                      pl.BlockSpec((FINAL_TILES, 8, TILE), lambda i, back: (i, 0, 0)),
                      pl.BlockSpec((1, D_MODEL), lambda i, back: (0, 0))],
            out_specs=pl.BlockSpec((FINAL_TILES * TILE, D_MODEL), lambda i, back: (i, 0)),
            scratch_shapes=[pltpu.VMEM((2, FINAL_TILES * SORTED_ROWS, D_MODEL), BF16),
                            pltpu.SemaphoreType.DMA((2,))],
        ),
        out_shape=jax.ShapeDtypeStruct((n_tok, D_MODEL), F32),
        compiler_params=pltpu.CompilerParams(
            dimension_semantics=("arbitrary",), vmem_limit_bytes=VMEM_LIMIT),
        name="final_call",
    )(back, ysort, h, dest, g)


def _rope_tables(length):
    pos = np.arange(length, dtype=np.float32)
    inv_freq = np.float32(ROPE_BASE) ** (-np.arange(0, HEAD_DIM, 2, dtype=np.float32) / np.float32(HEAD_DIM))
    ang = pos[:, None] * inv_freq[None, :]
    cos, sin = np.cos(ang), np.sin(ang)
    return np.concatenate([cos, cos], axis=-1), np.concatenate([-sin, sin], axis=-1)


def _retention_tables():
    f32 = np.float32
    gamma = f32(1.0) - f32(2.0) ** (f32(-5.0) - np.arange(HEADS, dtype=f32))
    log_g = np.log(gamma)[:, None, None]
    scale = f32(HEAD_DIM ** -0.5)
    idx = np.arange(RET_BLOCK)
    dist = np.abs(idx[:, None] - idx[None, :]).astype(f32)
    visible = (idx[None, :] // CHUNK) <= (idx[:, None] // CHUNK)
    mask = np.where(visible[None], np.exp(log_g * dist[None]), f32(0.0)) * scale
    ones = np.ones((1, 1, HEAD_DIM), f32)
    idx_f = idx.astype(f32)[None, :, None]
    pair = lambda a: a.reshape(HEAD_PAIRS, 2, a.shape[1], HEAD_DIM).transpose(0, 2, 1, 3).reshape(
        HEAD_PAIRS, a.shape[1], PAIR_DIM)
    qdec = pair(np.exp(log_g * (idx_f + f32(1.0))) * ones)
    kdec = pair(np.exp(log_g * (f32(RET_BLOCK - 1.0) - idx_f)) * scale * ones)
    meta_idx = np.arange(N_META, dtype=f32)[None, :, None]
    kdec_meta = pair(np.exp(log_g * (f32(N_META - 1.0) - meta_idx)) * scale * ones)
    blk = np.kron(np.eye(2, dtype=f32), np.ones((HEAD_DIM, HEAD_DIM), f32))
    bdec = np.exp(log_g * f32(RET_BLOCK)).reshape(HEAD_PAIRS, 2)
    bdec = np.stack([np.kron(np.diag(b), np.ones((HEAD_DIM, HEAD_DIM), f32)) for b in bdec])
    return tuple(a.astype(f32) for a in (mask, qdec, kdec, bdec, blk, kdec_meta))


def kernel(x, meta_tokens, norm_mix_g, w_in, ssm_lambda_re, ssm_lambda_im, ssm_log_dt, ssm_b_re, ssm_b_im, ssm_c_re, ssm_c_im, ssm_d, w_glu, w_out, norm_ffn_g, w_router_group, b_router_group, w_router_expert, b_router_expert, w_gate, w_up, w_down, norm_final_g):
    bsz, seq, _ = x.shape
    assert seq % TILE == 0 and seq % MIXER_TILE == 0 and MIXER_TILE % RET_BLOCK == 0 and RET_BLOCK % CHUNK == 0
    n_blocks = seq // S5_BLOCK
    assert n_blocks % SCAN_ROWS == 0
    n_tok = bsz * seq
    assert n_tok % (FINAL_TILES * TILE) == 0
    n_tiles = n_tok // TILE
    n_steps = -(-n_tiles * (TILE_PIECES + N_GROUPS - 1) // MOE_PIECES) + N_GROUPS

    cos, sin = _rope_tables(N_META + seq)
    mask, qdec, kdec, bdec, blk, kdec_meta = _retention_tables()
    g_mix = norm_mix_g[0][None, :]
    u, y_ret, u_meta = _mixer_call(x, g_mix, w_in[0], cos[N_META:], sin[N_META:], mask, qdec, kdec, bdec, blk,
                                   meta_tokens, cos[:N_META], sin[:N_META], kdec_meta)

    s5_ops = _s5_operators(
        ssm_lambda_re[0], ssm_lambda_im[0], ssm_log_dt[0], ssm_b_re[0], ssm_b_im[0],
        ssm_c_re[0], ssm_c_im[0], ssm_d[0])
    um = u_meta.reshape(S5_BLOCK, SSM_GROUPS, SSM_GROUP).transpose(1, 0, 2).reshape(SSM_GROUPS, 1, S5_LANES)
    um = jnp.pad(um, ((0, 0), (0, 7), (0, 0)))
    y_lo, y_hi = _s5_call(u, um, *s5_ops, w_glu[0].astype(BF16))

    w_r = jnp.concatenate(
        [w_router_group[0].T, w_router_expert[0].transpose(0, 2, 1).reshape(N_EXPERTS, D_MODEL)], axis=0)
    w_r = jnp.pad(w_r, ((0, ROUTE_ROWS - w_r.shape[0]), (0, 0)))
    b_r = jnp.concatenate([b_router_group[0], b_router_expert[0].reshape(-1)])
    b_r = jnp.pad(b_r, (0, ROUTE_ROWS - b_r.shape[0]))[:, None]
    w_r_hi = w_r.astype(BF16)
    w_r = jnp.concatenate([w_r_hi, (w_r - w_r_hi.astype(F32)).astype(BF16)], axis=0)
    tri = jnp.asarray(np.arange(TILE)[:, None] <= np.arange(TILE)[None, :], BF16)

    h, stage, dest, cnt = _proj_call(
        x.reshape(n_tok, D_MODEL), y_lo.reshape(n_tok, LANES), y_hi.reshape(n_tok, LANES),
        y_ret.reshape(n_tok, RET_WIDTH),
        w_out[0], norm_ffn_g[0][None, :], w_r, b_r, tri)
    src, g_step, n_live, back = _sort_tables(cnt[:, :N_GROUPS, 0].astype(I32), n_steps)
    y_sorted = _moe_call(src, g_step, n_live, stage, w_gate[0], w_up[0], w_down[0], n_steps)
    out = _final_call(back, y_sorted, h, dest, norm_final_g[None, :])
    return out.reshape(bsz, seq, D_MODEL)
```

```python
import functools

import jax
import jax.numpy as jnp
import numpy as np
from jax import lax
from jax.experimental import pallas as pl
from jax.experimental.pallas import tpu as pltpu

D_MODEL = 1024
N_META = 16
CHUNK = 64
EPS = 1e-6
SSM_WIDTH = 256
SSM_GROUP = 16
SSM_GROUPS = 16
SSM_STATE = 64
RET_WIDTH = 768
HEAD_DIM = 128
HEADS = 6
HEAD_PAIRS = HEADS // 2
PAIR_DIM = 2 * HEAD_DIM
ROPE_BASE = 10000.0
IN_WIDTH = SSM_WIDTH + 4 * RET_WIDTH
N_GROUPS = 4
EXPERTS_PER_GROUP = 4
N_EXPERTS = 16
EXPERT_FF = 256

S5_BLOCK = 16
S5_LANES = S5_BLOCK * SSM_GROUP
SCAN_ROWS = 8
RET_BLOCK = 256
TILE = 512
MIXER_TILE = 1024
LANES = 128
ROUTE_ROWS = 32
ROW_CHUNK = 256
PROJ_TILES = 2
FINAL_TILES = 4
PIECE = 16
TILE_PIECES = TILE // PIECE
MOE_TILE = 1024
MOE_PIECES = MOE_TILE // PIECE
SORTED_PIECES = TILE_PIECES + N_GROUPS
SORTED_ROWS = SORTED_PIECES * PIECE
EXT_WIDTH = D_MODEL + LANES
VMEM_LIMIT = 56 * 1024 * 1024

F32 = jnp.float32
BF16 = jnp.bfloat16
FP8 = jnp.float8_e4m3fn
FP8_HEADROOM = 256.0
I32 = jnp.int32


def _dot(a, b):
    return jnp.dot(a, b, preferred_element_type=F32)


def _sigmoid(x):
    return 1.0 / (1.0 + jnp.exp(-x))


def _rms_norm(x, g):
    return x * lax.rsqrt(jnp.mean(x * x, axis=-1, keepdims=True) + EPS) * g


def _rope(t, cos, sin_signed):
    return t * cos + pltpu.roll(t, HEAD_DIM // 2, axis=1) * sin_signed


def _meta_state(meta_ref, g_ref, w_ref, cos_ref, sin_ref, kdec_ref, blk_ref, u_ref, r0_ref):
    a = _rms_norm(meta_ref[...], g_ref[...]).astype(BF16)
    u_ref[...] = _dot(a, w_ref[:, :SSM_WIDTH])
    k_off = SSM_WIDTH + RET_WIDTH
    v_off = SSM_WIDTH + 2 * RET_WIDTH
    cos = cos_ref[...]
    sin = sin_ref[...]
    for p in range(HEAD_PAIRS):
        k = _dot(a, w_ref[:, k_off + p * PAIR_DIM:k_off + (p + 1) * PAIR_DIM])
        v = _dot(a, w_ref[:, v_off + p * PAIR_DIM:v_off + (p + 1) * PAIR_DIM])
        k = jnp.concatenate([_rope(k[:, :HEAD_DIM], cos, sin), _rope(k[:, HEAD_DIM:], cos, sin)], axis=1)
        kd = (k * kdec_ref[p]).astype(BF16)
        r0_ref[p] = _dot_rows(kd, v.astype(BF16)) * blk_ref[...]


def _dot_rows(a, b):
    return lax.dot_general(a, b, (((0,), (0,)), ((), ())), preferred_element_type=F32)


def _mixer_kernel(x_ref, g_ref, w32_ref, cos_ref, sin_ref, mask_ref, qdec_ref, kdec_ref, bdec_ref, blk_ref,
                  meta_ref, cos_m_ref, sin_m_ref, kdec_m_ref, u_ref, y_ref, um_ref, w_ref, r0_ref, r_ref):
    first_tile = pl.program_id(1) == 0

    @pl.when(jnp.logical_and(pl.program_id(0) == 0, first_tile))
    def _():
        for c in range(0, IN_WIDTH, SSM_WIDTH):
            w_ref[:, c:c + SSM_WIDTH] = w32_ref[:, c:c + SSM_WIDTH].astype(BF16)
        _meta_state(meta_ref, g_ref, w_ref, cos_m_ref, sin_m_ref, kdec_m_ref, blk_ref, um_ref, r0_ref)

    @pl.when(first_tile)
    def _():
        r_ref[...] = r0_ref[...]

    off = SSM_WIDTH
    for b in range(MIXER_TILE // RET_BLOCK):
        bs = slice(b * RET_BLOCK, (b + 1) * RET_BLOCK)
        a = _rms_norm(x_ref[bs, :], g_ref[...]).astype(BF16)
        u_ref[bs, :] = _dot(a, w_ref[:, :SSM_WIDTH])
        q_all = _dot(a, w_ref[:, off:off + RET_WIDTH])
        k_all = _dot(a, w_ref[:, off + RET_WIDTH:off + 2 * RET_WIDTH])
        v_all = _dot(a, w_ref[:, off + 2 * RET_WIDTH:off + 3 * RET_WIDTH])
        gate = _dot(a, w_ref[:, off + 3 * RET_WIDTH:off + 4 * RET_WIDTH])
        cos = cos_ref[bs, :]
        sin = sin_ref[bs, :]
        def scores(h):
            hs = slice(h * HEAD_DIM, (h + 1) * HEAD_DIM)
            q = _rope(q_all[:, hs], cos, sin)
            k = _rope(k_all[:, hs], cos, sin)
            return q, k, _dot_t(q.astype(BF16), k.astype(BF16))

        ahead = scores(0)
        for p in range(HEAD_PAIRS):
            ps = slice(p * PAIR_DIM, (p + 1) * PAIR_DIM)
            pair = [ahead, scores(2 * p + 1)]
            if p + 1 < HEAD_PAIRS:
                ahead = scores(2 * p + 2)
            q = jnp.concatenate([pair[0][0], pair[1][0]], axis=1)
            k = jnp.concatenate([pair[0][1], pair[1][1]], axis=1)
            v = v_all[:, ps].astype(BF16)
            state = r_ref[p]
            cross = _dot((q * qdec_ref[p]).astype(BF16), state.astype(BF16))
            kv = _dot_rows((k * kdec_ref[p]).astype(BF16), v)
            r_ref[p] = state * bdec_ref[p] + kv * blk_ref[...]
            for half, (_, _, s) in enumerate(pair):
                h = 2 * p + half
                hs = slice(h * HEAD_DIM, (h + 1) * HEAD_DIM)
                ls = slice(half * HEAD_DIM, (half + 1) * HEAD_DIM)
                o = _dot((s * mask_ref[h]).astype(BF16), v[:, ls]) + cross[:, ls]
                mu = jnp.mean(o, axis=-1, keepdims=True)
                d = o - mu
                var = jnp.mean(d * d, axis=-1, keepdims=True)
                gt = gate[:, hs]
                y_ref[bs, hs] = (gt * _sigmoid(gt) * d * lax.rsqrt(var + EPS)).astype(BF16)


def _mixer_call(x, g, w_in, cos, sin, mask, qdec, kdec, bdec, blk, meta, cos_m, sin_m, kdec_m):
    bsz, seq, _ = x.shape
    const = lambda a, **kw: pl.BlockSpec(a.shape, lambda b, i: (0,) * a.ndim, **kw)
    return pl.pallas_call(
        _mixer_kernel,
        grid=(bsz, seq // MIXER_TILE),
        in_specs=[
            pl.BlockSpec((None, MIXER_TILE, D_MODEL), lambda b, i: (b, i, 0)),
            const(g),
            const(w_in, pipeline_mode=pl.Buffered(1)),
            pl.BlockSpec((MIXER_TILE, HEAD_DIM), lambda b, i: (i, 0)),
            pl.BlockSpec((MIXER_TILE, HEAD_DIM), lambda b, i: (i, 0)),
            const(mask), const(qdec), const(kdec), const(bdec), const(blk),
            const(meta), const(cos_m), const(sin_m), const(kdec_m),
        ],
        out_specs=(
            pl.BlockSpec((None, MIXER_TILE, SSM_WIDTH), lambda b, i: (b, i, 0)),
            pl.BlockSpec((None, MIXER_TILE, RET_WIDTH), lambda b, i: (b, i, 0)),
            pl.BlockSpec((N_META, SSM_WIDTH), lambda b, i: (0, 0)),
        ),
        out_shape=(jax.ShapeDtypeStruct((bsz, seq, SSM_WIDTH), F32),
                   jax.ShapeDtypeStruct((bsz, seq, RET_WIDTH), BF16),
                   jax.ShapeDtypeStruct((N_META, SSM_WIDTH), F32)),
        scratch_shapes=[pltpu.VMEM((D_MODEL, IN_WIDTH), BF16),
                        pltpu.VMEM((HEAD_PAIRS, PAIR_DIM, PAIR_DIM), F32),
                        pltpu.VMEM((HEAD_PAIRS, PAIR_DIM, PAIR_DIM), F32)],
        compiler_params=pltpu.CompilerParams(
            dimension_semantics=("arbitrary", "arbitrary"), vmem_limit_bytes=VMEM_LIMIT),
        name="mixer_call",
    )(x, g, w_in, cos, sin, mask, qdec, kdec, bdec, blk, meta, cos_m, sin_m, kdec_m)


def _dot_t(a, b):
    return lax.dot_general(a, b, (((1,), (1,)), ((), ())), preferred_element_type=F32)


def _chunk_transpose(arrs):
    n = len(arrs)
    chunk = lax.broadcasted_iota(I32, (1, LANES), 1) // SSM_GROUP
    arrs = list(arrs)
    s = n // 2
    while s:
        keep = (chunk & s) == 0
        nxt = list(arrs)
        for i in range(n):
            if i & s == 0:
                lo, hi = arrs[i], arrs[i + s]
                nxt[i] = jnp.where(keep, lo, pltpu.roll(hi, s * SSM_GROUP, axis=1))
                nxt[i + s] = jnp.where(keep, pltpu.roll(lo, LANES - s * SSM_GROUP, axis=1), hi)
        arrs = nxt
        s //= 2
    return arrs


def _s5_kernel(u_lo_ref, u_hi_ref, um_ref, krow_ref, bmat_ref, cre_ref, cim_ref, ar_ref, ai_ref, wglu_ref,
               y_lo_ref, y_hi_ref, t0_ref, ug_ref, yg_ref):
    n_blocks = u_lo_ref.shape[0] // S5_BLOCK

    @pl.when(pl.program_id(0) == 0)
    def _():
        lane = lax.broadcasted_iota(I32, (SSM_GROUP, S5_LANES), 1)
        for g in range(SSM_GROUPS):
            k0 = krow_ref[g]
            for a in range(S5_BLOCK):
                blk = k0 if a == 0 else jnp.where(lane >= a * SSM_GROUP,
                                                  pltpu.roll(k0, a * SSM_GROUP, axis=1), 0.0)
                t0_ref[g, a * SSM_GROUP:(a + 1) * SSM_GROUP, :] = blk.astype(BF16)

    per_tile = LANES // SSM_GROUP
    for half, uh_ref in enumerate((u_lo_ref, u_hi_ref)):
        for t in range(S5_BLOCK // per_tile):
            words = [pltpu.bitcast(uh_ref[pl.ds(t * per_tile + k, n_blocks, stride=S5_BLOCK), :].astype(BF16),
                                   jnp.uint32) for k in range(per_tile)]
            for m, w in enumerate(_chunk_transpose(words)):
                ug_ref[half * per_tile + m, :, t * LANES:(t + 1) * LANES] = pltpu.bitcast(w, BF16)

    re, im, re0, im0 = [], [], [], []
    for p in range(SSM_GROUPS // 2):
        g0, g1 = 2 * p, 2 * p + 1
        v = _dot(ug_ref[g0], bmat_ref[g0]) + _dot(ug_ref[g1], bmat_ref[g1])
        v0 = (_dot(um_ref[g0].astype(BF16), bmat_ref[g0]) + _dot(um_ref[g1].astype(BF16), bmat_ref[g1]))[0:1]
        re.append(v[:, :LANES])
        im.append(v[:, LANES:])
        re0.append(v0[:, :LANES])
        im0.append(v0[:, LANES:])
    re, im, re0, im0 = (jnp.concatenate(parts, axis=1) for parts in (re, im, re0, im0))

    row = lax.broadcasted_iota(I32, re.shape, 0)
    ar, ai = ar_ref[0:1], ai_ref[0:1]
    re = re + jnp.where(row == 0, ar * re0 - ai * im0, 0.0)
    im = im + jnp.where(row == 0, ar * im0 + ai * re0, 0.0)
    d = 1
    while d < SCAN_ROWS:
        ar, ai = ar_ref[d - 1:d], ai_ref[d - 1:d]
        inside = row % SCAN_ROWS >= d
        sre = jnp.where(inside, pltpu.roll(re, d, axis=0), 0.0)
        sim = jnp.where(inside, pltpu.roll(im, d, axis=0), 0.0)
        re, im = re + ar * sre - ai * sim, im + ar * sim + ai * sre
        d *= 2
    ar, ai = ar_ref[...], ai_ref[...]
    re_tiles, im_tiles = [re[:SCAN_ROWS]], [im[:SCAN_ROWS]]
    for r in range(1, n_blocks // SCAN_ROWS):
        cre_ = re_tiles[-1][SCAN_ROWS - 1:SCAN_ROWS]
        cim_ = im_tiles[-1][SCAN_ROWS - 1:SCAN_ROWS]
        rs = slice(r * SCAN_ROWS, (r + 1) * SCAN_ROWS)
        re_tiles.append(re[rs] + ar * cre_ - ai * cim_)
        im_tiles.append(im[rs] + ar * cim_ + ai * cre_)
    re = jnp.concatenate(re_tiles, axis=0)
    im = jnp.concatenate(im_tiles, axis=0)
    pre = jnp.where(row == 0, re0, pltpu.roll(re, 1, axis=0)).astype(BF16)
    pim = jnp.where(row == 0, im0, pltpu.roll(im, 1, axis=0)).astype(BF16)

    for g in range(SSM_GROUPS):
        ps = slice((g // 2) * LANES, (g // 2 + 1) * LANES)
        yg_ref[g] = (_dot(ug_ref[g], t0_ref[g]) + _dot_t(pre[:, ps], cre_ref[g])
                     + _dot_t(pim[:, ps], cim_ref[g]))

    wglu = wglu_ref[...]
    for t in range(S5_BLOCK // per_tile):
        ts = slice(t * LANES, (t + 1) * LANES)
        halves = [_chunk_transpose([yg_ref[half * per_tile + m, :, ts] for m in range(per_tile)])
                  for half in range(SSM_GROUPS // per_tile)]
        for k in range(per_tile):
            y = jnp.concatenate([h[k] for h in halves], axis=1)
            y = jax.nn.gelu(y, approximate=True)
            y = y * _sigmoid(_dot(y.astype(BF16), wglu))
            i = t * per_tile + k
            y_lo_ref[pl.ds(i, n_blocks, stride=S5_BLOCK), :] = y[:, :LANES]
            y_hi_ref[pl.ds(i, n_blocks, stride=S5_BLOCK), :] = y[:, LANES:]


def _s5_call(u, um, krow, bmat, cre, cim, ar, ai, wglu):
    bsz, seq, _ = u.shape
    n_blocks = seq // S5_BLOCK
    const = lambda a: pl.BlockSpec(a.shape, lambda b: (0,) * a.ndim)
    return pl.pallas_call(
        _s5_kernel,
        grid=(bsz,),
        in_specs=[pl.BlockSpec((None, seq, LANES), lambda b: (b, 0, 0)),
                  pl.BlockSpec((None, seq, LANES), lambda b: (b, 0, 1)),
                  const(um), const(krow), const(bmat), const(cre), const(cim), const(ar), const(ai),
                  const(wglu)],
        out_specs=(pl.BlockSpec((None, seq, LANES), lambda b: (b, 0, 0)),
                   pl.BlockSpec((None, seq, LANES), lambda b: (b, 0, 0))),
        out_shape=(jax.ShapeDtypeStruct((bsz, seq, LANES), F32),
                   jax.ShapeDtypeStruct((bsz, seq, LANES), F32)),
        scratch_shapes=[pltpu.VMEM((SSM_GROUPS, S5_LANES, S5_LANES), BF16),
                        pltpu.VMEM((SSM_GROUPS, n_blocks, S5_LANES), BF16),
                        pltpu.VMEM((SSM_GROUPS, n_blocks, S5_LANES), F32)],
        compiler_params=pltpu.CompilerParams(
            dimension_semantics=("arbitrary",), vmem_limit_bytes=VMEM_LIMIT),
        name="s5_call",
    )(u, u, um, krow, bmat, cre, cim, ar, ai, wglu)


def _s5_operators(lam_re, lam_im, log_dt, b_re, b_im, c_re, c_im, d_skip):
    n_groups = lam_re.shape[0]
    lam = lax.complex(lam_re, lam_im)
    lam_dt = lam * jnp.exp(log_dt)[:, None]
    lam_bar = jnp.exp(lam_dt)
    b_bar = ((lam_bar - 1.0) / lam)[..., None] * lax.complex(b_re, b_im)
    c = lax.complex(c_re, c_im)
    tau = jnp.arange(S5_BLOCK + 1, dtype=F32)
    pows = jnp.exp(lam_dt[None] * tau[:, None, None])
    kern = jnp.real(jnp.einsum('ghp,tgp,gpk->gkth', c, pows[:S5_BLOCK], b_bar))
    skip = (jnp.eye(SSM_GROUP, dtype=F32)[None, :, None, :] * d_skip[:, None, None, :]
            * (tau[:S5_BLOCK] == 0).astype(F32)[None, None, :, None])
    krow = (kern + skip).reshape(n_groups, SSM_GROUP, S5_LANES)
    even = (jnp.arange(n_groups) % 2 == 0)[:, None, None]
    pair_pad = lambda m: jnp.concatenate([jnp.where(even, m, 0.0), jnp.where(even, 0.0, m)], axis=-1)
    bm = (pows[S5_BLOCK - 1 - jnp.arange(S5_BLOCK)].transpose(1, 0, 2)[:, :, None, :]
          * b_bar.transpose(0, 2, 1)[:, None, :, :]).reshape(n_groups, S5_LANES, SSM_STATE)
    bmat = jnp.concatenate([pair_pad(jnp.real(bm)), pair_pad(jnp.imag(bm))], axis=-1)
    cm = (pows[1:].transpose(1, 0, 2)[:, :, None, :] * c[:, None, :, :]).reshape(n_groups, S5_LANES, SSM_STATE)
    cre, cim = pair_pad(jnp.real(cm)), pair_pad(-jnp.imag(cm))
    step = S5_BLOCK * (1.0 + jnp.arange(SCAN_ROWS, dtype=F32))
    adec = jnp.exp(lam_dt[None, :, :] * step[:, None, None]).reshape(SCAN_ROWS, n_groups * SSM_STATE)
    return krow, bmat.astype(BF16), cre.astype(BF16), cim.astype(BF16), jnp.real(adec), jnp.imag(adec)


def _first_hit(values, target):
    hits, taken = [], None
    for v in values:
        hit = (v >= target) if taken is None else jnp.logical_and(v >= target, jnp.logical_not(taken))
        taken = hit if taken is None else jnp.logical_or(taken, hit)
        hits.append(hit)
    return hits


def _stack_rows(rows, n_rows):
    idx = lax.broadcasted_iota(I32, (n_rows, rows[0].shape[1]), 0)
    out = jnp.zeros((n_rows, rows[0].shape[1]), F32)
    for k, r in enumerate(rows):
        out = jnp.where(idx == k, r, out)
    return out


def _project(rs, x_ref, ys_lo_ref, ys_hi_ref, yr_ref, wout_ref, h_ref):
    ys = jnp.concatenate([ys_lo_ref[rs, :], ys_hi_ref[rs, :]], axis=1).astype(BF16)
    h = x_ref[rs, :] + _dot(ys, wout_ref[:SSM_WIDTH, :]) + _dot(yr_ref[rs, :], wout_ref[SSM_WIDTH:, :])
    h_ref[rs, :] = h.astype(BF16)
    return h


def _route(lt, tri_ref):
    gl = [lt[g:g + 1, :] for g in range(N_GROUPS)]
    gmax = functools.reduce(jnp.maximum, gl)
    g_w = 1.0 / functools.reduce(lambda a, b: a + b, [jnp.exp(l - gmax) for l in gl])
    sel = _first_hit(gl, gmax)
    ev = []
    for e in range(EXPERTS_PER_GROUP):
        acc = jnp.zeros_like(gmax)
        for g in range(N_GROUPS):
            k = N_GROUPS + g * EXPERTS_PER_GROUP + e
            acc = jnp.where(sel[g], lt[k:k + 1, :], acc)
        ev.append(acc)
    m1 = functools.reduce(jnp.maximum, ev)
    first = _first_hit(ev, m1)
    rest = [jnp.where(f, -jnp.inf, v) for f, v in zip(first, ev)]
    m2 = functools.reduce(jnp.maximum, rest)
    second = _first_hit(rest, m2)
    e2 = jnp.exp(m2 - m1)
    w1 = g_w / (1.0 + e2)
    w2 = e2 * w1
    combine = [jnp.where(f, w1, 0.0) + jnp.where(s, w2, 0.0) for f, s in zip(first, second)]

    sel_f = [jnp.where(s, 1.0, 0.0) for s in sel]
    incl = _dot(_stack_rows(sel_f, 8).astype(BF16), tri_ref[...])
    dest = jnp.zeros_like(gmax)
    seg_start = jnp.zeros((1, 1), F32)
    counts = []
    for g in range(N_GROUPS):
        run = incl[g:g + 1, :]
        cnt = run[:, TILE - 1:TILE]
        counts.append(cnt)
        dest = dest + sel_f[g] * (seg_start + run - 1.0)
        seg_start = seg_start + PIECE * jnp.floor((cnt + (PIECE - 1.0)) * (1.0 / PIECE))
    return combine, dest, counts


def _proj_kernel(x_ref, ys_lo_ref, ys_hi_ref, yr_ref, wout32_ref, g_ref, wr_ref, br_ref,
                 tri_ref, h_ref, stage_ref, dest_ref, cnt_ref, wout_ref):
    @pl.when(pl.program_id(0) == 0)
    def _():
        wout_ref[...] = wout32_ref[...].astype(BF16)

    proj_refs = (x_ref, ys_lo_ref, ys_hi_ref, yr_ref, wout_ref, h_ref)
    for tile in range(PROJ_TILES):
        _proj_tile(tile, proj_refs, g_ref, wr_ref, br_ref, tri_ref, stage_ref, dest_ref, cnt_ref)


def _proj_tile(tile, proj_refs, g_ref, wr_ref, br_ref, tri_ref, stage_ref, dest_ref, cnt_ref):
    chunks = [slice(tile * TILE + b * ROW_CHUNK, tile * TILE + (b + 1) * ROW_CHUNK)
              for b in range(TILE // ROW_CHUNK)]
    h_parts = [_project(rs, *proj_refs) for rs in chunks]
    t_parts = []
    for h in h_parts:
        t = _rms_norm(h, g_ref[...])
        t_hi = t.astype(BF16)
        t_parts.append((t_hi, (t - t_hi.astype(F32)).astype(BF16)))
    lt_parts = []
    for t_hi, t_lo in t_parts:
        both = _dot_t(wr_ref[...], t_hi)
        lt_parts.append(both[:ROUTE_ROWS] + both[ROUTE_ROWS:] + _dot_t(wr_ref[:ROUTE_ROWS, :], t_lo))
    lt = jnp.concatenate(lt_parts, axis=1) + br_ref[...]

    combine, dest, counts = _route(lt, tri_ref)
    dest_ref[tile] = _stack_rows([dest], 8)
    cnt_ref[tile] = _stack_rows([c + jnp.zeros((1, LANES), F32) for c in counts], 8)
    perm = jnp.where(lax.broadcasted_iota(I32, (SORTED_ROWS, TILE), 0) == dest.astype(I32),
                     1.0, 0.0).astype(BF16)
    c_hi = [c.astype(BF16).astype(F32) for c in combine]
    c_lo = [c - hi for c, hi in zip(combine, c_hi)]
    cw = _stack_rows(c_hi + c_lo, LANES).T.astype(BF16)
    t_ext = jnp.concatenate([jnp.concatenate([t for t, _ in t_parts], axis=0), cw], axis=1)
    stage_ref[tile * SORTED_ROWS:(tile + 1) * SORTED_ROWS, :] = _dot(perm, t_ext).astype(BF16)


def _proj_call(x, ys_lo, ys_hi, yr, wout, g, wr, br, tri):
    n_tok = x.shape[0]
    n_tiles = n_tok // TILE
    const = lambda *shape: pl.BlockSpec(shape, lambda i: (0,) * len(shape))
    rows = lambda width: pl.BlockSpec((PROJ_TILES * TILE, width), lambda i: (i, 0))
    return pl.pallas_call(
        _proj_kernel,
        grid=(n_tiles // PROJ_TILES,),
        in_specs=[
            rows(D_MODEL), rows(LANES), rows(LANES), rows(RET_WIDTH),
            pl.BlockSpec((D_MODEL, D_MODEL), lambda i: (0, 0), pipeline_mode=pl.Buffered(1)), const(1, D_MODEL),
            const(2 * ROUTE_ROWS, D_MODEL), const(ROUTE_ROWS, 1), const(TILE, TILE),
        ],
        out_specs=(rows(D_MODEL),
                   pl.BlockSpec((PROJ_TILES * SORTED_ROWS, EXT_WIDTH), lambda i: (i, 0)),
                   pl.BlockSpec((PROJ_TILES, 8, TILE), lambda i: (i, 0, 0)),
                   pl.BlockSpec((PROJ_TILES, 8, LANES), lambda i: (i, 0, 0))),
        scratch_shapes=[pltpu.VMEM((D_MODEL, D_MODEL), BF16)],
        out_shape=(jax.ShapeDtypeStruct((n_tok, D_MODEL), BF16),
                   jax.ShapeDtypeStruct((n_tiles * SORTED_ROWS, EXT_WIDTH), BF16),
                   jax.ShapeDtypeStruct((n_tiles, 8, TILE), F32),
                   jax.ShapeDtypeStruct((n_tiles, 8, LANES), F32)),
        compiler_params=pltpu.CompilerParams(
            dimension_semantics=("arbitrary",), vmem_limit_bytes=VMEM_LIMIT),
        name="proj_call",
    )(x, ys_lo, ys_hi, yr, wout, g, wr, br, tri)


def _sort_tables(cnt, n_steps):
    n_tiles = cnt.shape[0]
    npc = (cnt + PIECE - 1) // PIECE
    seg = jnp.cumsum(npc, axis=1) - npc
    before = jnp.cumsum(npc, axis=0) - npc
    n_tile_g = (jnp.sum(npc, axis=0) + MOE_PIECES - 1) // MOE_PIECES
    t_off = jnp.cumsum(n_tile_g) - n_tile_g
    j = jnp.arange(SORTED_PIECES, dtype=I32)[None, :, None]
    in_g = jnp.logical_and(j >= seg[:, None, :], j < (seg + npc)[:, None, :])
    pos = jnp.sum(jnp.where(in_g, MOE_PIECES * t_off[None, None, :] + before[:, None, :] + j - seg[:, None, :], 0),
                  axis=-1)
    valid = jnp.any(in_g, axis=-1)
    stage_piece = jnp.arange(n_tiles * SORTED_PIECES, dtype=I32).reshape(n_tiles, SORTED_PIECES)
    n_slots = n_steps * MOE_PIECES
    src = jnp.zeros((n_slots,), I32).at[jnp.where(valid, pos, n_slots).reshape(-1)].set(
        stage_piece.reshape(-1), mode='drop')
    steps = jnp.arange(n_steps, dtype=I32)
    g_step = jnp.minimum(jnp.sum(steps[:, None] >= (t_off + n_tile_g)[None, :], axis=1), N_GROUPS - 1)
    n_live = jnp.sum(n_tile_g).reshape(1)
    back = jnp.where(valid, pos, 0).reshape(-1)
    return src, g_step.astype(I32), n_live.astype(I32), back.astype(I32)


def _piece_copy(src_ref, piece, buf_ref, slot, j, sem_ref):
    start = piece * PIECE if isinstance(piece, int) else pl.multiple_of(piece * PIECE, PIECE)
    return pltpu.make_async_copy(src_ref.at[pl.ds(start, PIECE)],
                                 buf_ref.at[slot, pl.ds(j * PIECE, PIECE)], sem_ref.at[slot])


def _fetch_pieces(table_ref, src_ref, buf_ref, sem_ref, row, slot, n_pieces):
    for j in range(n_pieces):
        _piece_copy(src_ref, table_ref[row * n_pieces + j], buf_ref, slot, j, sem_ref).start()


def _wait_pieces(src_ref, buf_ref, sem_ref, slot, n_pieces):
    for j in range(n_pieces):
        _piece_copy(src_ref, 0, buf_ref, slot, j, sem_ref).wait()


def _gather_pieces(table_ref, src_ref, buf_ref, sem_ref, n_pieces):
    step = pl.program_id(0)
    last = pl.num_programs(0) - 1
    args = (src_ref, buf_ref, sem_ref)

    @pl.when(step == 0)
    def _():
        _fetch_pieces(table_ref, *args, step, 0, n_pieces)

    _wait_pieces(*args, step % 2, n_pieces)

    def fetch_next():
        _fetch_pieces(table_ref, *args, jnp.minimum(step + 1, last), (step + 1) % 2, n_pieces)

    def drain():
        @pl.when(step == last)
        def _():
            _wait_pieces(*args, (step + 1) % 2, n_pieces)

    return fetch_next, drain


def _to_fp8(x, headroom=FP8_HEADROOM):
    peak = jnp.maximum(jnp.max(jnp.abs(x), axis=(0, 1), keepdims=True), 1e-30)
    return (x * (headroom / peak)).astype(FP8), peak * (1.0 / headroom)


def _moe_kernel(src_ref, gstep_ref, nlive_ref, stage_ref, wg_ref, wu_ref, wd_ref, y_ref,
                buf_ref, sem_ref, wgb_ref, wub_ref, wdb_ref, inv_ref):
    step = pl.program_id(0)
    fetch_next, drain = _gather_pieces(src_ref, stage_ref, buf_ref, sem_ref, MOE_PIECES)

    @pl.when(jnp.logical_or(step == 0, gstep_ref[step] != gstep_ref[jnp.maximum(step - 1, 0)]))
    def _():
        ones = jnp.ones((1, LANES), F32)
        for e in range(EXPERTS_PER_GROUP):
            wgb_ref[e], inv = _to_fp8(wg_ref[e])
            inv_ref[e:e + 1, :] = inv * ones
            wub_ref[e], inv = _to_fp8(wu_ref[e])
            inv_ref[EXPERTS_PER_GROUP + e:EXPERTS_PER_GROUP + e + 1, :] = inv * ones
        peak = functools.reduce(jnp.maximum, [jnp.max(jnp.abs(wd_ref[e]), axis=(0, 1), keepdims=True)
                                              for e in range(EXPERTS_PER_GROUP)])
        peak = jnp.maximum(peak, 1e-30)
        for e in range(EXPERTS_PER_GROUP):
            wdb_ref[e * EXPERT_FF:(e + 1) * EXPERT_FF, :] = (wd_ref[e] * (FP8_HEADROOM / peak)).astype(FP8)
        inv_ref[2 * EXPERTS_PER_GROUP:2 * EXPERTS_PER_GROUP + 1, :] = peak * (1.0 / FP8_HEADROOM) * ones

    @pl.when(step < nlive_ref[0])
    def _():
        slot = step % 2
        fetch_next()

        def up(rs):
            t, t_inv = _to_fp8(buf_ref[slot, rs, :D_MODEL].astype(F32))
            out = []
            for e in range(EXPERTS_PER_GROUP):
                g_inv = inv_ref[e:e + 1, 0:1] * t_inv
                u_inv = inv_ref[EXPERTS_PER_GROUP + e:EXPERTS_PER_GROUP + e + 1, 0:1] * t_inv
                out.append((_dot(t, wgb_ref[e]) * g_inv, _dot(t, wub_ref[e]), u_inv))
            return out

        def down(rs, hidden):
            cw = buf_ref[slot, rs, D_MODEL:].astype(F32)
            acts = []
            for e, (hg, hu, u_inv) in enumerate(hidden):
                c = (cw[:, e:e + 1] + cw[:, EXPERTS_PER_GROUP + e:EXPERTS_PER_GROUP + e + 1]) * u_inv
                acts.append(hg * _sigmoid(hg) * hu * c)
            act, a_inv = _to_fp8(jnp.concatenate(acts, axis=1))
            d_inv = inv_ref[2 * EXPERTS_PER_GROUP:2 * EXPERTS_PER_GROUP + 1, 0:1] * a_inv
            y_ref[rs, :] = (_dot(act, wdb_ref[...]) * d_inv).astype(BF16)

        chunks = [slice(b * ROW_CHUNK, (b + 1) * ROW_CHUNK) for b in range(MOE_TILE // ROW_CHUNK)]
        hidden = [up(rs) for rs in chunks]
        for rs, hid in zip(chunks, hidden):
            down(rs, hid)

    @pl.when(step >= nlive_ref[0])
    def _():
        fetch_next()
        y_ref[...] = jnp.zeros_like(y_ref)

    drain()


def _moe_call(src, g_step, n_live, stage, wg, wu, wd, n_steps):
    grp = lambda shape: pl.BlockSpec((EXPERTS_PER_GROUP,) + shape, lambda s, src, gs, nl: (gs[s], 0, 0))
    return pl.pallas_call(
        _moe_kernel,
        grid_spec=pltpu.PrefetchScalarGridSpec(
            num_scalar_prefetch=3,
            grid=(n_steps,),
            in_specs=[pl.BlockSpec(memory_space=pl.ANY),
                      grp((D_MODEL, EXPERT_FF)), grp((D_MODEL, EXPERT_FF)), grp((EXPERT_FF, D_MODEL))],
            out_specs=pl.BlockSpec((MOE_TILE, D_MODEL), lambda s, src, gs, nl: (s, 0)),
            scratch_shapes=[pltpu.VMEM((2, MOE_TILE, EXT_WIDTH), BF16), pltpu.SemaphoreType.DMA((2,)),
                            pltpu.VMEM((EXPERTS_PER_GROUP, D_MODEL, EXPERT_FF), FP8),
                            pltpu.VMEM((EXPERTS_PER_GROUP, D_MODEL, EXPERT_FF), FP8),
                            pltpu.VMEM((EXPERTS_PER_GROUP * EXPERT_FF, D_MODEL), FP8),
                            pltpu.VMEM((16, LANES), F32)],
        ),
        out_shape=jax.ShapeDtypeStruct((n_steps * MOE_TILE, D_MODEL), BF16),
        compiler_params=pltpu.CompilerParams(
            dimension_semantics=("arbitrary",), vmem_limit_bytes=VMEM_LIMIT),
        name="moe_call",
    )(src, g_step, n_live, stage, wg, wu, wd)


def _final_kernel(back_ref, ysort_ref, h_ref, dest_ref, g_ref, o_ref, buf_ref, sem_ref):
    step = pl.program_id(0)
    fetch_next, drain = _gather_pieces(back_ref, ysort_ref, buf_ref, sem_ref, FINAL_TILES * SORTED_PIECES)
    fetch_next()
    for t in range(FINAL_TILES):
        dest = _stack_rows([dest_ref[t, 0:1, :]], LANES).T[:, 0:1].astype(I32)
        unperm = jnp.where(lax.broadcasted_iota(I32, (TILE, SORTED_ROWS), 1) == dest, 1.0, 0.0).astype(BF16)
        rows = buf_ref[step % 2, t * SORTED_ROWS:(t + 1) * SORTED_ROWS, :]
        for b in range(TILE // ROW_CHUNK):
            rs = slice(b * ROW_CHUNK, (b + 1) * ROW_CHUNK)
            os = slice(t * TILE + b * ROW_CHUNK, t * TILE + (b + 1) * ROW_CHUNK)
            o_ref[os, :] = _rms_norm(h_ref[os, :].astype(F32) + _dot(unperm[rs], rows), g_ref[...])
    drain()


def _final_call(back, ysort, h, dest, g):
    n_tok = h.shape[0]
    return pl.pallas_call(
        _final_kernel,
        grid_spec=pltpu.PrefetchScalarGridSpec(
            num_scalar_prefetch=1,
            grid=(n_tok // (FINAL_TILES * TILE),),
            in_specs=[pl.BlockSpec(memory_space=pl.ANY),
                      pl.BlockSpec((FINAL_TILES * TILE, D_MODEL), lambda i, back: (i, 0)),
                      pl.BlockSpec((FINAL_TILES, 8, TILE), lambda i, back: (i, 0, 0)),
                      pl.BlockSpec((1, D_MODEL), lambda i, back: (0, 0))],
            out_specs=pl.BlockSpec((FINAL_TILES * TILE, D_MODEL), lambda i, back: (i, 0)),
            scratch_shapes=[pltpu.VMEM((2, FINAL_TILES * SORTED_ROWS, D_MODEL), BF16),
                            pltpu.SemaphoreType.DMA((2,))],
        ),
        out_shape=jax.ShapeDtypeStruct((n_tok, D_MODEL), F32),
        compiler_params=pltpu.CompilerParams(
            dimension_semantics=("arbitrary",), vmem_limit_bytes=VMEM_LIMIT),
        name="final_call",
    )(back, ysort, h, dest, g)


def _rope_tables(length):
    pos = np.arange(length, dtype=np.float32)
    inv_freq = np.float32(ROPE_BASE) ** (-np.arange(0, HEAD_DIM, 2, dtype=np.float32) / np.float32(HEAD_DIM))
    ang = pos[:, None] * inv_freq[None, :]
    cos, sin = np.cos(ang), np.sin(ang)
    return np.concatenate([cos, cos], axis=-1), np.concatenate([-sin, sin], axis=-1)


def _retention_tables():
    f32 = np.float32
    gamma = f32(1.0) - f32(2.0) ** (f32(-5.0) - np.arange(HEADS, dtype=f32))
    log_g = np.log(gamma)[:, None, None]
    scale = f32(HEAD_DIM ** -0.5)
    idx = np.arange(RET_BLOCK)
    dist = np.abs(idx[:, None] - idx[None, :]).astype(f32)
    visible = (idx[None, :] // CHUNK) <= (idx[:, None] // CHUNK)
    mask = np.where(visible[None], np.exp(log_g * dist[None]), f32(0.0)) * scale
    ones = np.ones((1, 1, HEAD_DIM), f32)
    idx_f = idx.astype(f32)[None, :, None]
    pair = lambda a: a.reshape(HEAD_PAIRS, 2, a.shape[1], HEAD_DIM).transpose(0, 2, 1, 3).reshape(
        HEAD_PAIRS, a.shape[1], PAIR_DIM)
    qdec = pair(np.exp(log_g * (idx_f + f32(1.0))) * ones)
    kdec = pair(np.exp(log_g * (f32(RET_BLOCK - 1.0) - idx_f)) * scale * ones)
    meta_idx = np.arange(N_META, dtype=f32)[None, :, None]
    kdec_meta = pair(np.exp(log_g * (f32(N_META - 1.0) - meta_idx)) * scale * ones)
    blk = np.kron(np.eye(2, dtype=f32), np.ones((HEAD_DIM, HEAD_DIM), f32))
    bdec = np.exp(log_g * f32(RET_BLOCK)).reshape(HEAD_PAIRS, 2)
    bdec = np.stack([np.kron(np.diag(b), np.ones((HEAD_DIM, HEAD_DIM), f32)) for b in bdec])
    return tuple(a.astype(f32) for a in (mask, qdec, kdec, bdec, blk, kdec_meta))


def kernel(x, meta_tokens, norm_mix_g, w_in, ssm_lambda_re, ssm_lambda_im, ssm_log_dt, ssm_b_re, ssm_b_im, ssm_c_re, ssm_c_im, ssm_d, w_glu, w_out, norm_ffn_g, w_router_group, b_router_group, w_router_expert, b_router_expert, w_gate, w_up, w_down, norm_final_g):
    bsz, seq, _ = x.shape
    assert seq % TILE == 0 and seq % MIXER_TILE == 0 and MIXER_TILE % RET_BLOCK == 0 and RET_BLOCK % CHUNK == 0
    n_blocks = seq // S5_BLOCK
    assert n_blocks % SCAN_ROWS == 0
    n_tok = bsz * seq
    assert n_tok % (FINAL_TILES * TILE) == 0 and n_tok % (PROJ_TILES * TILE) == 0
    n_tiles = n_tok // TILE
    n_steps = -(-n_tiles * (TILE_PIECES + N_GROUPS - 1) // MOE_PIECES) + N_GROUPS

    cos, sin = _rope_tables(N_META + seq)
    mask, qdec, kdec, bdec, blk, kdec_meta = _retention_tables()
    g_mix = norm_mix_g[0][None, :]
    u, y_ret, u_meta = _mixer_call(x, g_mix, w_in[0], cos[N_META:], sin[N_META:], mask, qdec, kdec, bdec, blk,
                                   meta_tokens, cos[:N_META], sin[:N_META], kdec_meta)

    s5_ops = _s5_operators(
        ssm_lambda_re[0], ssm_lambda_im[0], ssm_log_dt[0], ssm_b_re[0], ssm_b_im[0],
        ssm_c_re[0], ssm_c_im[0], ssm_d[0])
    um = u_meta.reshape(S5_BLOCK, SSM_GROUPS, SSM_GROUP).transpose(1, 0, 2).reshape(SSM_GROUPS, 1, S5_LANES)
    um = jnp.pad(um, ((0, 0), (0, 7), (0, 0)))
    y_lo, y_hi = _s5_call(u, um, *s5_ops, w_glu[0].astype(BF16))

    w_r = jnp.concatenate(
        [w_router_group[0].T, w_router_expert[0].transpose(0, 2, 1).reshape(N_EXPERTS, D_MODEL)], axis=0)
    w_r = jnp.pad(w_r, ((0, ROUTE_ROWS - w_r.shape[0]), (0, 0)))
    b_r = jnp.concatenate([b_router_group[0], b_router_expert[0].reshape(-1)])
    b_r = jnp.pad(b_r, (0, ROUTE_ROWS - b_r.shape[0]))[:, None]
    w_r_hi = w_r.astype(BF16)
    w_r = jnp.concatenate([w_r_hi, (w_r - w_r_hi.astype(F32)).astype(BF16)], axis=0)
    tri = jnp.asarray(np.arange(TILE)[:, None] <= np.arange(TILE)[None, :], BF16)

    h, stage, dest, cnt = _proj_call(
        x.reshape(n_tok, D_MODEL), y_lo.reshape(n_tok, LANES), y_hi.reshape(n_tok, LANES),
        y_ret.reshape(n_tok, RET_WIDTH),
        w_out[0], norm_ffn_g[0][None, :], w_r, b_r, tri)
    src, g_step, n_live, back = _sort_tables(cnt[:, :N_GROUPS, 0].astype(I32), n_steps)
    y_sorted = _moe_call(src, g_step, n_live, stage, w_gate[0], w_up[0], w_down[0], n_steps)
    out = _final_call(back, y_sorted, h, dest, norm_final_g[None, :])
    return out.reshape(bsz, seq, D_MODEL)
```

```python
import functools

import jax
import jax.numpy as jnp
import numpy as np
from jax import lax
from jax.experimental import pallas as pl
from jax.experimental.pallas import tpu as pltpu

D_MODEL = 1024
N_META = 16
CHUNK = 64
EPS = 1e-6
SSM_WIDTH = 256
SSM_GROUP = 16
SSM_GROUPS = 16
SSM_STATE = 64
RET_WIDTH = 768
HEAD_DIM = 128
HEADS = 6
HEAD_PAIRS = HEADS // 2
PAIR_DIM = 2 * HEAD_DIM
ROPE_BASE = 10000.0
IN_WIDTH = SSM_WIDTH + 4 * RET_WIDTH
N_GROUPS = 4
EXPERTS_PER_GROUP = 4
N_EXPERTS = 16
EXPERT_FF = 256

S5_BLOCK = 16
S5_LANES = S5_BLOCK * SSM_GROUP
SCAN_ROWS = 8
RET_BLOCK = 256
TILE = 512
MIXER_TILE = 1024
LANES = 128
ROUTE_ROWS = 32
ROW_CHUNK = 256
PROJ_TILES = 2
FINAL_TILES = 4
PIECE = 16
TILE_PIECES = TILE // PIECE
MOE_TILE = 1024
MOE_PIECES = MOE_TILE // PIECE
SORTED_PIECES = TILE_PIECES + N_GROUPS
SORTED_ROWS = SORTED_PIECES * PIECE
EXT_WIDTH = D_MODEL + LANES
VMEM_LIMIT = 56 * 1024 * 1024

F32 = jnp.float32
BF16 = jnp.bfloat16
FP8 = jnp.float8_e4m3fn
FP8_HEADROOM = 256.0
I32 = jnp.int32


def _dot(a, b):
    return jnp.dot(a, b, preferred_element_type=F32)


def _sigmoid(x):
    return 1.0 / (1.0 + jnp.exp(-x))


def _rms_norm(x, g):
    return x * lax.rsqrt(jnp.mean(x * x, axis=-1, keepdims=True) + EPS) * g


def _rope(t, cos, sin_signed):
    return t * cos + pltpu.roll(t, HEAD_DIM // 2, axis=1) * sin_signed


def _meta_state(meta_ref, g_ref, w_ref, cos_ref, sin_ref, kdec_ref, blk_ref, u_ref, r0_ref):
    a = _rms_norm(meta_ref[...], g_ref[...]).astype(BF16)
    u_ref[...] = _dot(a, w_ref[:, :SSM_WIDTH])
    k_off = SSM_WIDTH + RET_WIDTH
    v_off = SSM_WIDTH + 2 * RET_WIDTH
    cos = cos_ref[...]
    sin = sin_ref[...]
    for p in range(HEAD_PAIRS):
        k = _dot(a, w_ref[:, k_off + p * PAIR_DIM:k_off + (p + 1) * PAIR_DIM])
        v = _dot(a, w_ref[:, v_off + p * PAIR_DIM:v_off + (p + 1) * PAIR_DIM])
        k = jnp.concatenate([_rope(k[:, :HEAD_DIM], cos, sin), _rope(k[:, HEAD_DIM:], cos, sin)], axis=1)
        kd = (k * kdec_ref[p]).astype(BF16)
        r0_ref[p] = _dot_rows(kd, v.astype(BF16)) * blk_ref[...]


def _dot_rows(a, b):
    return lax.dot_general(a, b, (((0,), (0,)), ((), ())), preferred_element_type=F32)


def _mixer_kernel(x_ref, g_ref, w32_ref, cos_ref, sin_ref, mask_ref, qdec_ref, kdec_ref, bdec_ref, blk_ref,
                  meta_ref, cos_m_ref, sin_m_ref, kdec_m_ref, u_ref, y_ref, um_ref, w_ref, r0_ref, r_ref):
    first_tile = pl.program_id(1) == 0

    @pl.when(jnp.logical_and(pl.program_id(0) == 0, first_tile))
    def _():
        for c in range(0, IN_WIDTH, SSM_WIDTH):
            w_ref[:, c:c + SSM_WIDTH] = w32_ref[:, c:c + SSM_WIDTH].astype(BF16)
        _meta_state(meta_ref, g_ref, w_ref, cos_m_ref, sin_m_ref, kdec_m_ref, blk_ref, um_ref, r0_ref)

    @pl.when(first_tile)
    def _():
        r_ref[...] = r0_ref[...]

    off = SSM_WIDTH
    for b in range(MIXER_TILE // RET_BLOCK):
        bs = slice(b * RET_BLOCK, (b + 1) * RET_BLOCK)
        a = _rms_norm(x_ref[bs, :], g_ref[...]).astype(BF16)
        u_ref[bs, :] = _dot(a, w_ref[:, :SSM_WIDTH])
        q_all = _dot(a, w_ref[:, off:off + RET_WIDTH])
        k_all = _dot(a, w_ref[:, off + RET_WIDTH:off + 2 * RET_WIDTH])
        v_all = _dot(a, w_ref[:, off + 2 * RET_WIDTH:off + 3 * RET_WIDTH])
        gate = _dot(a, w_ref[:, off + 3 * RET_WIDTH:off + 4 * RET_WIDTH])
        cos = cos_ref[bs, :]
        sin = sin_ref[bs, :]
        def scores(h):
            hs = slice(h * HEAD_DIM, (h + 1) * HEAD_DIM)
            q = _rope(q_all[:, hs], cos, sin)
            k = _rope(k_all[:, hs], cos, sin)
            return q, k, _dot_t(q.astype(BF16), k.astype(BF16))

        ahead = scores(0)
        for p in range(HEAD_PAIRS):
            ps = slice(p * PAIR_DIM, (p + 1) * PAIR_DIM)
            pair = [ahead, scores(2 * p + 1)]
            if p + 1 < HEAD_PAIRS:
                ahead = scores(2 * p + 2)
            q = jnp.concatenate([pair[0][0], pair[1][0]], axis=1)
            k = jnp.concatenate([pair[0][1], pair[1][1]], axis=1)
            v = v_all[:, ps].astype(BF16)
            state = r_ref[p]
            cross = _dot((q * qdec_ref[p]).astype(BF16), state.astype(BF16))
            kv = _dot_rows((k * kdec_ref[p]).astype(BF16), v)
            r_ref[p] = state * bdec_ref[p] + kv * blk_ref[...]
            for half, (_, _, s) in enumerate(pair):
                h = 2 * p + half
                hs = slice(h * HEAD_DIM, (h + 1) * HEAD_DIM)
                ls = slice(half * HEAD_DIM, (half + 1) * HEAD_DIM)
                o = _dot((s * mask_ref[h]).astype(BF16), v[:, ls]) + cross[:, ls]
                mu = jnp.mean(o, axis=-1, keepdims=True)
                d = o - mu
                var = jnp.mean(d * d, axis=-1, keepdims=True)
                gt = gate[:, hs]
                y_ref[bs, hs] = (gt * _sigmoid(gt) * d * lax.rsqrt(var + EPS)).astype(BF16)


def _mixer_call(x, g, w_in, cos, sin, mask, qdec, kdec, bdec, blk, meta, cos_m, sin_m, kdec_m):
    bsz, seq, _ = x.shape
    const = lambda a, **kw: pl.BlockSpec(a.shape, lambda b, i: (0,) * a.ndim, **kw)
    return pl.pallas_call(
        _mixer_kernel,
        grid=(bsz, seq // MIXER_TILE),
        in_specs=[
            pl.BlockSpec((None, MIXER_TILE, D_MODEL), lambda b, i: (b, i, 0)),
            const(g),
            const(w_in, pipeline_mode=pl.Buffered(1)),
            pl.BlockSpec((MIXER_TILE, HEAD_DIM), lambda b, i: (i, 0)),
            pl.BlockSpec((MIXER_TILE, HEAD_DIM), lambda b, i: (i, 0)),
            const(mask), const(qdec), const(kdec), const(bdec), const(blk),
            const(meta), const(cos_m), const(sin_m), const(kdec_m),
        ],
        out_specs=(
            pl.BlockSpec((None, MIXER_TILE, SSM_WIDTH), lambda b, i: (b, i, 0)),
            pl.BlockSpec((None, MIXER_TILE, RET_WIDTH), lambda b, i: (b, i, 0)),
            pl.BlockSpec((N_META, SSM_WIDTH), lambda b, i: (0, 0)),
        ),
        out_shape=(jax.ShapeDtypeStruct((bsz, seq, SSM_WIDTH), F32),
                   jax.ShapeDtypeStruct((bsz, seq, RET_WIDTH), BF16),
                   jax.ShapeDtypeStruct((N_META, SSM_WIDTH), F32)),
        scratch_shapes=[pltpu.VMEM((D_MODEL, IN_WIDTH), BF16),
                        pltpu.VMEM((HEAD_PAIRS, PAIR_DIM, PAIR_DIM), F32),
                        pltpu.VMEM((HEAD_PAIRS, PAIR_DIM, PAIR_DIM), F32)],
        compiler_params=pltpu.CompilerParams(
            dimension_semantics=("arbitrary", "arbitrary"), vmem_limit_bytes=VMEM_LIMIT),
        name="mixer_call",
    )(x, g, w_in, cos, sin, mask, qdec, kdec, bdec, blk, meta, cos_m, sin_m, kdec_m)


def _dot_t(a, b):
    return lax.dot_general(a, b, (((1,), (1,)), ((), ())), preferred_element_type=F32)


def _chunk_transpose(arrs):
    n = len(arrs)
    chunk = lax.broadcasted_iota(I32, (1, LANES), 1) // SSM_GROUP
    arrs = list(arrs)
    s = n // 2
    while s:
        keep = (chunk & s) == 0
        nxt = list(arrs)
        for i in range(n):
            if i & s == 0:
                lo, hi = arrs[i], arrs[i + s]
                nxt[i] = jnp.where(keep, lo, pltpu.roll(hi, s * SSM_GROUP, axis=1))
                nxt[i + s] = jnp.where(keep, pltpu.roll(lo, LANES - s * SSM_GROUP, axis=1), hi)
        arrs = nxt
        s //= 2
    return arrs


def _s5_kernel(u_lo_ref, u_hi_ref, um_ref, krow_ref, bmat_ref, cre_ref, cim_ref, ar_ref, ai_ref, wglu_ref,
               y_lo_ref, y_hi_ref, t0_ref, ug_ref, yg_ref):
    n_blocks = u_lo_ref.shape[0] // S5_BLOCK

    @pl.when(pl.program_id(0) == 0)
    def _():
        lane = lax.broadcasted_iota(I32, (SSM_GROUP, S5_LANES), 1)
        for g in range(SSM_GROUPS):
            k0 = krow_ref[g]
            for a in range(S5_BLOCK):
                blk = k0 if a == 0 else jnp.where(lane >= a * SSM_GROUP,
                                                  pltpu.roll(k0, a * SSM_GROUP, axis=1), 0.0)
                t0_ref[g, a * SSM_GROUP:(a + 1) * SSM_GROUP, :] = blk.astype(BF16)

    per_tile = LANES // SSM_GROUP
    for half, uh_ref in enumerate((u_lo_ref, u_hi_ref)):
        for t in range(S5_BLOCK // per_tile):
            words = [pltpu.bitcast(uh_ref[pl.ds(t * per_tile + k, n_blocks, stride=S5_BLOCK), :].astype(BF16),
                                   jnp.uint32) for k in range(per_tile)]
            for m, w in enumerate(_chunk_transpose(words)):
                ug_ref[half * per_tile + m, :, t * LANES:(t + 1) * LANES] = pltpu.bitcast(w, BF16)

    re, im, re0, im0 = [], [], [], []
    for p in range(SSM_GROUPS // 2):
        g0, g1 = 2 * p, 2 * p + 1
        v = _dot(ug_ref[g0], bmat_ref[g0]) + _dot(ug_ref[g1], bmat_ref[g1])
        v0 = (_dot(um_ref[g0].astype(BF16), bmat_ref[g0]) + _dot(um_ref[g1].astype(BF16), bmat_ref[g1]))[0:1]
        re.append(v[:, :LANES])
        im.append(v[:, LANES:])
        re0.append(v0[:, :LANES])
        im0.append(v0[:, LANES:])
    re, im, re0, im0 = (jnp.concatenate(parts, axis=1) for parts in (re, im, re0, im0))

    row = lax.broadcasted_iota(I32, re.shape, 0)
    ar, ai = ar_ref[0:1], ai_ref[0:1]
    re = re + jnp.where(row == 0, ar * re0 - ai * im0, 0.0)
    im = im + jnp.where(row == 0, ar * im0 + ai * re0, 0.0)
    d = 1
    while d < SCAN_ROWS:
        ar, ai = ar_ref[d - 1:d], ai_ref[d - 1:d]
        inside = row % SCAN_ROWS >= d
        sre = jnp.where(inside, pltpu.roll(re, d, axis=0), 0.0)
        sim = jnp.where(inside, pltpu.roll(im, d, axis=0), 0.0)
        re, im = re + ar * sre - ai * sim, im + ar * sim + ai * sre
        d *= 2
    ar, ai = ar_ref[...], ai_ref[...]
    re_tiles, im_tiles = [re[:SCAN_ROWS]], [im[:SCAN_ROWS]]
    for r in range(1, n_blocks // SCAN_ROWS):
        cre_ = re_tiles[-1][SCAN_ROWS - 1:SCAN_ROWS]
        cim_ = im_tiles[-1][SCAN_ROWS - 1:SCAN_ROWS]
        rs = slice(r * SCAN_ROWS, (r + 1) * SCAN_ROWS)
        re_tiles.append(re[rs] + ar * cre_ - ai * cim_)
        im_tiles.append(im[rs] + ar * cim_ + ai * cre_)
    re = jnp.concatenate(re_tiles, axis=0)
    im = jnp.concatenate(im_tiles, axis=0)
    pre = jnp.where(row == 0, re0, pltpu.roll(re, 1, axis=0)).astype(BF16)
    pim = jnp.where(row == 0, im0, pltpu.roll(im, 1, axis=0)).astype(BF16)

    for g in range(SSM_GROUPS):
        ps = slice((g // 2) * LANES, (g // 2 + 1) * LANES)
        yg_ref[g] = (_dot(ug_ref[g], t0_ref[g]) + _dot_t(pre[:, ps], cre_ref[g])
                     + _dot_t(pim[:, ps], cim_ref[g]))

    wglu = wglu_ref[...]
    for t in range(S5_BLOCK // per_tile):
        ts = slice(t * LANES, (t + 1) * LANES)
        halves = [_chunk_transpose([yg_ref[half * per_tile + m, :, ts] for m in range(per_tile)])
                  for half in range(SSM_GROUPS // per_tile)]
        for k in range(per_tile):
            y = jnp.concatenate([h[k] for h in halves], axis=1)
            y = jax.nn.gelu(y, approximate=True)
            y = y * _sigmoid(_dot(y.astype(BF16), wglu))
            i = t * per_tile + k
            y_lo_ref[pl.ds(i, n_blocks, stride=S5_BLOCK), :] = y[:, :LANES]
            y_hi_ref[pl.ds(i, n_blocks, stride=S5_BLOCK), :] = y[:, LANES:]


def _s5_call(u, um, krow, bmat, cre, cim, ar, ai, wglu):
    bsz, seq, _ = u.shape
    n_blocks = seq // S5_BLOCK
    const = lambda a: pl.BlockSpec(a.shape, lambda b: (0,) * a.ndim)
    return pl.pallas_call(
        _s5_kernel,
        grid=(bsz,),
        in_specs=[pl.BlockSpec((None, seq, LANES), lambda b: (b, 0, 0)),
                  pl.BlockSpec((None, seq, LANES), lambda b: (b, 0, 1)),
                  const(um), const(krow), const(bmat), const(cre), const(cim), const(ar), const(ai),
                  const(wglu)],
        out_specs=(pl.BlockSpec((None, seq, LANES), lambda b: (b, 0, 0)),
                   pl.BlockSpec((None, seq, LANES), lambda b: (b, 0, 0))),
        out_shape=(jax.ShapeDtypeStruct((bsz, seq, LANES), F32),
                   jax.ShapeDtypeStruct((bsz, seq, LANES), F32)),
        scratch_shapes=[pltpu.VMEM((SSM_GROUPS, S5_LANES, S5_LANES), BF16),
                        pltpu.VMEM((SSM_GROUPS, n_blocks, S5_LANES), BF16),
                        pltpu.VMEM((SSM_GROUPS, n_blocks, S5_LANES), F32)],
        compiler_params=pltpu.CompilerParams(
            dimension_semantics=("arbitrary",), vmem_limit_bytes=VMEM_LIMIT),
        name="s5_call",
    )(u, u, um, krow, bmat, cre, cim, ar, ai, wglu)


def _s5_operators(lam_re, lam_im, log_dt, b_re, b_im, c_re, c_im, d_skip):
    n_groups = lam_re.shape[0]
    lam = lax.complex(lam_re, lam_im)
    lam_dt = lam * jnp.exp(log_dt)[:, None]
    lam_bar = jnp.exp(lam_dt)
    b_bar = ((lam_bar - 1.0) / lam)[..., None] * lax.complex(b_re, b_im)
    c = lax.complex(c_re, c_im)
    tau = jnp.arange(S5_BLOCK + 1, dtype=F32)
    pows = jnp.exp(lam_dt[None] * tau[:, None, None])
    kern = jnp.real(jnp.einsum('ghp,tgp,gpk->gkth', c, pows[:S5_BLOCK], b_bar))
    skip = (jnp.eye(SSM_GROUP, dtype=F32)[None, :, None, :] * d_skip[:, None, None, :]
            * (tau[:S5_BLOCK] == 0).astype(F32)[None, None, :, None])
    krow = (kern + skip).reshape(n_groups, SSM_GROUP, S5_LANES)
    even = (jnp.arange(n_groups) % 2 == 0)[:, None, None]
    pair_pad = lambda m: jnp.concatenate([jnp.where(even, m, 0.0), jnp.where(even, 0.0, m)], axis=-1)
    bm = (pows[S5_BLOCK - 1 - jnp.arange(S5_BLOCK)].transpose(1, 0, 2)[:, :, None, :]
          * b_bar.transpose(0, 2, 1)[:, None, :, :]).reshape(n_groups, S5_LANES, SSM_STATE)
    bmat = jnp.concatenate([pair_pad(jnp.real(bm)), pair_pad(jnp.imag(bm))], axis=-1)
    cm = (pows[1:].transpose(1, 0, 2)[:, :, None, :] * c[:, None, :, :]).reshape(n_groups, S5_LANES, SSM_STATE)
    cre, cim = pair_pad(jnp.real(cm)), pair_pad(-jnp.imag(cm))
    step = S5_BLOCK * (1.0 + jnp.arange(SCAN_ROWS, dtype=F32))
    adec = jnp.exp(lam_dt[None, :, :] * step[:, None, None]).reshape(SCAN_ROWS, n_groups * SSM_STATE)
    return krow, bmat.astype(BF16), cre.astype(BF16), cim.astype(BF16), jnp.real(adec), jnp.imag(adec)


def _first_hit(values, target):
    hits, taken = [], None
    for v in values:
        hit = (v >= target) if taken is None else jnp.logical_and(v >= target, jnp.logical_not(taken))
        taken = hit if taken is None else jnp.logical_or(taken, hit)
        hits.append(hit)
    return hits


def _stack_rows(rows, n_rows):
    idx = lax.broadcasted_iota(I32, (n_rows, rows[0].shape[1]), 0)
    out = jnp.zeros((n_rows, rows[0].shape[1]), F32)
    for k, r in enumerate(rows):
        out = jnp.where(idx == k, r, out)
    return out


def _project(rs, x_ref, ys_lo_ref, ys_hi_ref, yr_ref, wout_ref, h_ref):
    ys = jnp.concatenate([ys_lo_ref[rs, :], ys_hi_ref[rs, :]], axis=1).astype(BF16)
    h = x_ref[rs, :] + _dot(ys, wout_ref[:SSM_WIDTH, :]) + _dot(yr_ref[rs, :], wout_ref[SSM_WIDTH:, :])
    h_ref[rs, :] = h.astype(BF16)
    return h


def _route(lt, tri_ref):
    gl = [lt[g:g + 1, :] for g in range(N_GROUPS)]
    gmax = functools.reduce(jnp.maximum, gl)
    g_w = 1.0 / functools.reduce(lambda a, b: a + b, [jnp.exp(l - gmax) for l in gl])
    sel = _first_hit(gl, gmax)
    ev = []
    for e in range(EXPERTS_PER_GROUP):
        acc = jnp.zeros_like(gmax)
        for g in range(N_GROUPS):
            k = N_GROUPS + g * EXPERTS_PER_GROUP + e
            acc = jnp.where(sel[g], lt[k:k + 1, :], acc)
        ev.append(acc)
    m1 = functools.reduce(jnp.maximum, ev)
    first = _first_hit(ev, m1)
    rest = [jnp.where(f, -jnp.inf, v) for f, v in zip(first, ev)]
    m2 = functools.reduce(jnp.maximum, rest)
    second = _first_hit(rest, m2)
    e2 = jnp.exp(m2 - m1)
    w1 = g_w / (1.0 + e2)
    w2 = e2 * w1
    combine = [jnp.where(f, w1, 0.0) + jnp.where(s, w2, 0.0) for f, s in zip(first, second)]

    sel_f = [jnp.where(s, 1.0, 0.0) for s in sel]
    incl = _dot(_stack_rows(sel_f, 8).astype(BF16), tri_ref[...])
    dest = jnp.zeros_like(gmax)
    seg_start = jnp.zeros((1, 1), F32)
    counts = []
    for g in range(N_GROUPS):
        run = incl[g:g + 1, :]
        cnt = run[:, TILE - 1:TILE]
        counts.append(cnt)
        dest = dest + sel_f[g] * (seg_start + run - 1.0)
        seg_start = seg_start + PIECE * jnp.floor((cnt + (PIECE - 1.0)) * (1.0 / PIECE))
    return combine, dest, counts


def _proj_kernel(x_ref, ys_lo_ref, ys_hi_ref, yr_ref, wout32_ref, g_ref, wr_ref, br_ref,
                 tri_ref, h_ref, stage_ref, dest_ref, cnt_ref, wout_ref):
    @pl.when(pl.program_id(0) == 0)
    def _():
        wout_ref[...] = wout32_ref[...].astype(BF16)

    tiles = range(PROJ_TILES)
    proj_refs = (x_ref, ys_lo_ref, ys_hi_ref, yr_ref, wout_ref, h_ref)
    chunks = [[slice(tile * TILE + b * ROW_CHUNK, tile * TILE + (b + 1) * ROW_CHUNK)
               for b in range(TILE // ROW_CHUNK)] for tile in tiles]
    h_parts = [[_project(rs, *proj_refs) for rs in chunks[tile]] for tile in tiles]

    t_parts, logits = [], []
    for tile in tiles:
        parts = []
        for h in h_parts[tile]:
            t = _rms_norm(h, g_ref[...])
            t_hi = t.astype(BF16)
            parts.append((t_hi, (t - t_hi.astype(F32)).astype(BF16)))
        lt_parts = []
        for t_hi, t_lo in parts:
            both = _dot_t(wr_ref[...], t_hi)
            lt_parts.append(both[:ROUTE_ROWS] + both[ROUTE_ROWS:] + _dot_t(wr_ref[:ROUTE_ROWS, :], t_lo))
        t_parts.append(parts)
        logits.append(jnp.concatenate(lt_parts, axis=1) + br_ref[...])

    routed = [_route(logits[tile], tri_ref) for tile in tiles]

    for tile in tiles:
        combine, dest, counts = routed[tile]
        dest_ref[tile] = _stack_rows([dest], 8)
        cnt_ref[tile] = _stack_rows([c + jnp.zeros((1, LANES), F32) for c in counts], 8)
        perm = jnp.where(lax.broadcasted_iota(I32, (SORTED_ROWS, TILE), 0) == dest.astype(I32),
                         1.0, 0.0).astype(BF16)
        c_hi = [c.astype(BF16).astype(F32) for c in combine]
        c_lo = [c - hi for c, hi in zip(combine, c_hi)]
        cw = _stack_rows(c_hi + c_lo, LANES).T.astype(BF16)
        t_ext = jnp.concatenate([jnp.concatenate([t for t, _ in t_parts[tile]], axis=0), cw], axis=1)
        stage_ref[tile * SORTED_ROWS:(tile + 1) * SORTED_ROWS, :] = _dot(perm, t_ext).astype(BF16)


def _proj_call(x, ys_lo, ys_hi, yr, wout, g, wr, br, tri):
    n_tok = x.shape[0]
    n_tiles = n_tok // TILE
    const = lambda *shape: pl.BlockSpec(shape, lambda i: (0,) * len(shape))
    rows = lambda width: pl.BlockSpec((PROJ_TILES * TILE, width), lambda i: (i, 0))
    return pl.pallas_call(
        _proj_kernel,
        grid=(n_tiles // PROJ_TILES,),
        in_specs=[
            rows(D_MODEL), rows(LANES), rows(LANES), rows(RET_WIDTH),
            pl.BlockSpec((D_MODEL, D_MODEL), lambda i: (0, 0), pipeline_mode=pl.Buffered(1)), const(1, D_MODEL),
            const(2 * ROUTE_ROWS, D_MODEL), const(ROUTE_ROWS, 1), const(TILE, TILE),
        ],
        out_specs=(rows(D_MODEL),
                   pl.BlockSpec((PROJ_TILES * SORTED_ROWS, EXT_WIDTH), lambda i: (i, 0)),
                   pl.BlockSpec((PROJ_TILES, 8, TILE), lambda i: (i, 0, 0)),
                   pl.BlockSpec((PROJ_TILES, 8, LANES), lambda i: (i, 0, 0))),
        scratch_shapes=[pltpu.VMEM((D_MODEL, D_MODEL), BF16)],
        out_shape=(jax.ShapeDtypeStruct((n_tok, D_MODEL), BF16),
                   jax.ShapeDtypeStruct((n_tiles * SORTED_ROWS, EXT_WIDTH), BF16),
                   jax.ShapeDtypeStruct((n_tiles, 8, TILE), F32),
                   jax.ShapeDtypeStruct((n_tiles, 8, LANES), F32)),
        compiler_params=pltpu.CompilerParams(
            dimension_semantics=("arbitrary",), vmem_limit_bytes=VMEM_LIMIT),
        name="proj_call",
    )(x, ys_lo, ys_hi, yr, wout, g, wr, br, tri)


def _sort_tables(cnt, n_steps):
    n_tiles = cnt.shape[0]
    npc = (cnt + PIECE - 1) // PIECE
    seg = jnp.cumsum(npc, axis=1) - npc
    before = jnp.cumsum(npc, axis=0) - npc
    n_tile_g = (jnp.sum(npc, axis=0) + MOE_PIECES - 1) // MOE_PIECES
    t_off = jnp.cumsum(n_tile_g) - n_tile_g
    j = jnp.arange(SORTED_PIECES, dtype=I32)[None, :, None]
    in_g = jnp.logical_and(j >= seg[:, None, :], j < (seg + npc)[:, None, :])
    pos = jnp.sum(jnp.where(in_g, MOE_PIECES * t_off[None, None, :] + before[:, None, :] + j - seg[:, None, :], 0),
                  axis=-1)
    valid = jnp.any(in_g, axis=-1)
    stage_piece = jnp.arange(n_tiles * SORTED_PIECES, dtype=I32).reshape(n_tiles, SORTED_PIECES)
    n_slots = n_steps * MOE_PIECES
    src = jnp.zeros((n_slots,), I32).at[jnp.where(valid, pos, n_slots).reshape(-1)].set(
        stage_piece.reshape(-1), mode='drop')
    steps = jnp.arange(n_steps, dtype=I32)
    g_step = jnp.minimum(jnp.sum(steps[:, None] >= (t_off + n_tile_g)[None, :], axis=1), N_GROUPS - 1)
    n_live = jnp.sum(n_tile_g).reshape(1)
    back = jnp.where(valid, pos, 0).reshape(-1)
    return src, g_step.astype(I32), n_live.astype(I32), back.astype(I32)


def _piece_copy(src_ref, piece, buf_ref, slot, j, sem_ref):
    start = piece * PIECE if isinstance(piece, int) else pl.multiple_of(piece * PIECE, PIECE)
    return pltpu.make_async_copy(src_ref.at[pl.ds(start, PIECE)],
                                 buf_ref.at[slot, pl.ds(j * PIECE, PIECE)], sem_ref.at[slot])


def _fetch_pieces(table_ref, src_ref, buf_ref, sem_ref, row, slot, n_pieces):
    for j in range(n_pieces):
        _piece_copy(src_ref, table_ref[row * n_pieces + j], buf_ref, slot, j, sem_ref).start()


def _wait_pieces(src_ref, buf_ref, sem_ref, slot, n_pieces):
    for j in range(n_pieces):
        _piece_copy(src_ref, 0, buf_ref, slot, j, sem_ref).wait()


def _gather_pieces(table_ref, src_ref, buf_ref, sem_ref, n_pieces):
    step = pl.program_id(0)
    last = pl.num_programs(0) - 1
    args = (src_ref, buf_ref, sem_ref)

    @pl.when(step == 0)
    def _():
        _fetch_pieces(table_ref, *args, step, 0, n_pieces)

    _wait_pieces(*args, step % 2, n_pieces)

    def fetch_next():
        _fetch_pieces(table_ref, *args, jnp.minimum(step + 1, last), (step + 1) % 2, n_pieces)

    def drain():
        @pl.when(step == last)
        def _():
            _wait_pieces(*args, (step + 1) % 2, n_pieces)

    return fetch_next, drain


def _to_fp8(x, headroom=FP8_HEADROOM):
    peak = jnp.maximum(jnp.max(jnp.abs(x), axis=(0, 1), keepdims=True), 1e-30)
    return (x * (headroom / peak)).astype(FP8), peak * (1.0 / headroom)


def _moe_kernel(src_ref, gstep_ref, nlive_ref, stage_ref, wg_ref, wu_ref, wd_ref, y_ref,
                buf_ref, sem_ref, wgb_ref, wub_ref, wdb_ref, inv_ref):
    step = pl.program_id(0)
    fetch_next, drain = _gather_pieces(src_ref, stage_ref, buf_ref, sem_ref, MOE_PIECES)

    @pl.when(jnp.logical_or(step == 0, gstep_ref[step] != gstep_ref[jnp.maximum(step - 1, 0)]))
    def _():
        ones = jnp.ones((1, LANES), F32)
        for e in range(EXPERTS_PER_GROUP):
            wgb_ref[e], inv = _to_fp8(wg_ref[e])
            inv_ref[e:e + 1, :] = inv * ones
            wub_ref[e], inv = _to_fp8(wu_ref[e])
            inv_ref[EXPERTS_PER_GROUP + e:EXPERTS_PER_GROUP + e + 1, :] = inv * ones
        peak = functools.reduce(jnp.maximum, [jnp.max(jnp.abs(wd_ref[e]), axis=(0, 1), keepdims=True)
                                              for e in range(EXPERTS_PER_GROUP)])
        peak = jnp.maximum(peak, 1e-30)
        for e in range(EXPERTS_PER_GROUP):
            wdb_ref[e * EXPERT_FF:(e + 1) * EXPERT_FF, :] = (wd_ref[e] * (FP8_HEADROOM / peak)).astype(FP8)
        inv_ref[2 * EXPERTS_PER_GROUP:2 * EXPERTS_PER_GROUP + 1, :] = peak * (1.0 / FP8_HEADROOM) * ones

    @pl.when(step < nlive_ref[0])
    def _():
        slot = step % 2
        fetch_next()

        def up(rs):
            t, t_inv = _to_fp8(buf_ref[slot, rs, :D_MODEL].astype(F32))
            out = []
            for e in range(EXPERTS_PER_GROUP):
                g_inv = inv_ref[e:e + 1, 0:1] * t_inv
                u_inv = inv_ref[EXPERTS_PER_GROUP + e:EXPERTS_PER_GROUP + e + 1, 0:1] * t_inv
                out.append((_dot(t, wgb_ref[e]) * g_inv, _dot(t, wub_ref[e]), u_inv))
            return out

        def down(rs, hidden):
            cw = buf_ref[slot, rs, D_MODEL:].astype(F32)
            acts = []
            for e, (hg, hu, u_inv) in enumerate(hidden):
                c = (cw[:, e:e + 1] + cw[:, EXPERTS_PER_GROUP + e:EXPERTS_PER_GROUP + e + 1]) * u_inv
                acts.append(hg * _sigmoid(hg) * hu * c)
            act, a_inv = _to_fp8(jnp.concatenate(acts, axis=1))
            d_inv = inv_ref[2 * EXPERTS_PER_GROUP:2 * EXPERTS_PER_GROUP + 1, 0:1] * a_inv
            y_ref[rs, :] = (_dot(act, wdb_ref[...]) * d_inv).astype(BF16)

        chunks = [slice(b * ROW_CHUNK, (b + 1) * ROW_CHUNK) for b in range(MOE_TILE // ROW_CHUNK)]
        hidden = [up(rs) for rs in chunks]
        for rs, hid in zip(chunks, hidden):
            down(rs, hid)

    @pl.when(step >= nlive_ref[0])
    def _():
        fetch_next()
        y_ref[...] = jnp.zeros_like(y_ref)

    drain()


def _moe_call(src, g_step, n_live, stage, wg, wu, wd, n_steps):
    grp = lambda shape: pl.BlockSpec((EXPERTS_PER_GROUP,) + shape, lambda s, src, gs, nl: (gs[s], 0, 0))
    return pl.pallas_call(
        _moe_kernel,
        grid_spec=pltpu.PrefetchScalarGridSpec(
            num_scalar_prefetch=3,
            grid=(n_steps,),
            in_specs=[pl.BlockSpec(memory_space=pl.ANY),
                      grp((D_MODEL, EXPERT_FF)), grp((D_MODEL, EXPERT_FF)), grp((EXPERT_FF, D_MODEL))],
            out_specs=pl.BlockSpec((MOE_TILE, D_MODEL), lambda s, src, gs, nl: (s, 0)),
            scratch_shapes=[pltpu.VMEM((2, MOE_TILE, EXT_WIDTH), BF16), pltpu.SemaphoreType.DMA((2,)),
                            pltpu.VMEM((EXPERTS_PER_GROUP, D_MODEL, EXPERT_FF), FP8),
                            pltpu.VMEM((EXPERTS_PER_GROUP, D_MODEL, EXPERT_FF), FP8),
                            pltpu.VMEM((EXPERTS_PER_GROUP * EXPERT_FF, D_MODEL), FP8),
                            pltpu.VMEM((16, LANES), F32)],
        ),
        out_shape=jax.ShapeDtypeStruct((n_steps * MOE_TILE, D_MODEL), BF16),
        compiler_params=pltpu.CompilerParams(
            dimension_semantics=("arbitrary",), vmem_limit_bytes=VMEM_LIMIT),
        name="moe_call",
    )(src, g_step, n_live, stage, wg, wu, wd)


def _final_kernel(back_ref, ysort_ref, h_ref, dest_ref, g_ref, o_ref, buf_ref, sem_ref):
    step = pl.program_id(0)
    fetch_next, drain = _gather_pieces(back_ref, ysort_ref, buf_ref, sem_ref, FINAL_TILES * SORTED_PIECES)
    fetch_next()
    for t in range(FINAL_TILES):
        dest = _stack_rows([dest_ref[t, 0:1, :]], LANES).T[:, 0:1].astype(I32)
        unperm = jnp.where(lax.broadcasted_iota(I32, (TILE, SORTED_ROWS), 1) == dest, 1.0, 0.0).astype(BF16)
        rows = buf_ref[step % 2, t * SORTED_ROWS:(t + 1) * SORTED_ROWS, :]
        for b in range(TILE // ROW_CHUNK):
            rs = slice(b * ROW_CHUNK, (b + 1) * ROW_CHUNK)
            os = slice(t * TILE + b * ROW_CHUNK, t * TILE + (b + 1) * ROW_CHUNK)
            o_ref[os, :] = _rms_norm(h_ref[os, :].astype(F32) + _dot(unperm[rs], rows), g_ref[...])
    drain()


def _final_call(back, ysort, h, dest, g):
    n_tok = h.shape[0]
    return pl.pallas_call(
        _final_kernel,
        grid_spec=pltpu.PrefetchScalarGridSpec(
            num_scalar_prefetch=1,
            grid=(n_tok // (FINAL_TILES * TILE),),
            in_specs=[pl.BlockSpec(memory_space=pl.ANY),
                      pl.BlockSpec((FINAL_TILES * TILE, D_MODEL), lambda i, back: (i, 0)),
                      pl.BlockSpec((FINAL_TILES, 8, TILE), lambda i, back: (i, 0, 0)),
                      pl.BlockSpec((1, D_MODEL), lambda i, back: (0, 0))],
            out_specs=pl.BlockSpec((FINAL_TILES * TILE, D_MODEL), lambda i, back: (i, 0)),
            scratch_shapes=[pltpu.VMEM((2, FINAL_TILES * SORTED_ROWS, D_MODEL), BF16),
                            pltpu.SemaphoreType.DMA((2,))],
        ),
        out_shape=jax.ShapeDtypeStruct((n_tok, D_MODEL), F32),
        compiler_params=pltpu.CompilerParams(
            dimension_semantics=("arbitrary",), vmem_limit_bytes=VMEM_LIMIT),
        name="final_call",
    )(back, ysort, h, dest, g)


def _rope_tables(length):
    pos = np.arange(length, dtype=np.float32)
    inv_freq = np.float32(ROPE_BASE) ** (-np.arange(0, HEAD_DIM, 2, dtype=np.float32) / np.float32(HEAD_DIM))
    ang = pos[:, None] * inv_freq[None, :]
    cos, sin = np.cos(ang), np.sin(ang)
    return np.concatenate([cos, cos], axis=-1), np.concatenate([-sin, sin], axis=-1)


def _retention_tables():
    f32 = np.float32
    gamma = f32(1.0) - f32(2.0) ** (f32(-5.0) - np.arange(HEADS, dtype=f32))
    log_g = np.log(gamma)[:, None, None]
    scale = f32(HEAD_DIM ** -0.5)
    idx = np.arange(RET_BLOCK)
    dist = np.abs(idx[:, None] - idx[None, :]).astype(f32)
    visible = (idx[None, :] // CHUNK) <= (idx[:, None] // CHUNK)
    mask = np.where(visible[None], np.exp(log_g * dist[None]), f32(0.0)) * scale
    ones = np.ones((1, 1, HEAD_DIM), f32)
    idx_f = idx.astype(f32)[None, :, None]
    pair = lambda a: a.reshape(HEAD_PAIRS, 2, a.shape[1], HEAD_DIM).transpose(0, 2, 1, 3).reshape(
        HEAD_PAIRS, a.shape[1], PAIR_DIM)
    qdec = pair(np.exp(log_g * (idx_f + f32(1.0))) * ones)
    kdec = pair(np.exp(log_g * (f32(RET_BLOCK - 1.0) - idx_f)) * scale * ones)
    meta_idx = np.arange(N_META, dtype=f32)[None, :, None]
    kdec_meta = pair(np.exp(log_g * (f32(N_META - 1.0) - meta_idx)) * scale * ones)
    blk = np.kron(np.eye(2, dtype=f32), np.ones((HEAD_DIM, HEAD_DIM), f32))
    bdec = np.exp(log_g * f32(RET_BLOCK)).reshape(HEAD_PAIRS, 2)
    bdec = np.stack([np.kron(np.diag(b), np.ones((HEAD_DIM, HEAD_DIM), f32)) for b in bdec])
    return tuple(a.astype(f32) for a in (mask, qdec, kdec, bdec, blk, kdec_meta))


def kernel(x, meta_tokens, norm_mix_g, w_in, ssm_lambda_re, ssm_lambda_im, ssm_log_dt, ssm_b_re, ssm_b_im, ssm_c_re, ssm_c_im, ssm_d, w_glu, w_out, norm_ffn_g, w_router_group, b_router_group, w_router_expert, b_router_expert, w_gate, w_up, w_down, norm_final_g):
    bsz, seq, _ = x.shape
    assert seq % TILE == 0 and seq % MIXER_TILE == 0 and MIXER_TILE % RET_BLOCK == 0 and RET_BLOCK % CHUNK == 0
    n_blocks = seq // S5_BLOCK
    assert n_blocks % SCAN_ROWS == 0
    n_tok = bsz * seq
    assert n_tok % (FINAL_TILES * TILE) == 0 and n_tok % (PROJ_TILES * TILE) == 0
    n_tiles = n_tok // TILE
    n_steps = -(-n_tiles * (TILE_PIECES + N_GROUPS - 1) // MOE_PIECES) + N_GROUPS

    cos, sin = _rope_tables(N_META + seq)
    mask, qdec, kdec, bdec, blk, kdec_meta = _retention_tables()
    g_mix = norm_mix_g[0][None, :]
    u, y_ret, u_meta = _mixer_call(x, g_mix, w_in[0], cos[N_META:], sin[N_META:], mask, qdec, kdec, bdec, blk,
                                   meta_tokens, cos[:N_META], sin[:N_META], kdec_meta)

    s5_ops = _s5_operators(
        ssm_lambda_re[0], ssm_lambda_im[0], ssm_log_dt[0], ssm_b_re[0], ssm_b_im[0],
        ssm_c_re[0], ssm_c_im[0], ssm_d[0])
    um = u_meta.reshape(S5_BLOCK, SSM_GROUPS, SSM_GROUP).transpose(1, 0, 2).reshape(SSM_GROUPS, 1, S5_LANES)
    um = jnp.pad(um, ((0, 0), (0, 7), (0, 0)))
    y_lo, y_hi = _s5_call(u, um, *s5_ops, w_glu[0].astype(BF16))

    w_r = jnp.concatenate(
        [w_router_group[0].T, w_router_expert[0].transpose(0, 2, 1).reshape(N_EXPERTS, D_MODEL)], axis=0)
    w_r = jnp.pad(w_r, ((0, ROUTE_ROWS - w_r.shape[0]), (0, 0)))
    b_r = jnp.concatenate([b_router_group[0], b_router_expert[0].reshape(-1)])
    b_r = jnp.pad(b_r, (0, ROUTE_ROWS - b_r.shape[0]))[:, None]
    w_r_hi = w_r.astype(BF16)
    w_r = jnp.concatenate([w_r_hi, (w_r - w_r_hi.astype(F32)).astype(BF16)], axis=0)
    tri = jnp.asarray(np.arange(TILE)[:, None] <= np.arange(TILE)[None, :], BF16)

    h, stage, dest, cnt = _proj_call(
        x.reshape(n_tok, D_MODEL), y_lo.reshape(n_tok, LANES), y_hi.reshape(n_tok, LANES),
        y_ret.reshape(n_tok, RET_WIDTH),
        w_out[0], norm_ffn_g[0][None, :], w_r, b_r, tri)
    src, g_step, n_live, back = _sort_tables(cnt[:, :N_GROUPS, 0].astype(I32), n_steps)
    y_sorted = _moe_call(src, g_step, n_live, stage, w_gate[0], w_up[0], w_down[0], n_steps)
    out = _final_call(back, y_sorted, h, dest, norm_final_g[None, :])
    return out.reshape(bsz, seq, D_MODEL)
```

```python
import functools

import jax
import jax.numpy as jnp
import numpy as np
from jax import lax
from jax.experimental import pallas as pl
from jax.experimental.pallas import tpu as pltpu

D_MODEL = 1024
N_META = 16
CHUNK = 64
EPS = 1e-6
SSM_WIDTH = 256
SSM_GROUP = 16
SSM_GROUPS = 16
SSM_STATE = 64
RET_WIDTH = 768
HEAD_DIM = 128
HEADS = 6
HEAD_PAIRS = HEADS // 2
PAIR_DIM = 2 * HEAD_DIM
ROPE_BASE = 10000.0
IN_WIDTH = SSM_WIDTH + 4 * RET_WIDTH
N_GROUPS = 4
EXPERTS_PER_GROUP = 4
N_EXPERTS = 16
EXPERT_FF = 256

S5_BLOCK = 16
S5_LANES = S5_BLOCK * SSM_GROUP
SCAN_ROWS = 8
RET_BLOCK = 256
TILE = 512
MIXER_TILE = 1024
LANES = 128
ROUTE_ROWS = 32
ROW_CHUNK = 256
PROJ_TILES = 2
FINAL_TILES = 4
PIECE = 16
TILE_PIECES = TILE // PIECE
MOE_TILE = 1024
MOE_PIECES = MOE_TILE // PIECE
SORTED_PIECES = TILE_PIECES + N_GROUPS
SORTED_ROWS = SORTED_PIECES * PIECE
EXT_WIDTH = D_MODEL + LANES
INV_LANE = 2 * EXPERTS_PER_GROUP
VMEM_LIMIT = 56 * 1024 * 1024

F32 = jnp.float32
BF16 = jnp.bfloat16
FP8 = jnp.float8_e4m3fn
FP8_HEADROOM = 256.0
I32 = jnp.int32


def _dot(a, b):
    return jnp.dot(a, b, preferred_element_type=F32)


def _sigmoid(x):
    return 1.0 / (1.0 + jnp.exp(-x))


def _rms_norm(x, g):
    return x * lax.rsqrt(jnp.mean(x * x, axis=-1, keepdims=True) + EPS) * g


def _rope(t, cos, sin_signed):
    return t * cos + pltpu.roll(t, HEAD_DIM // 2, axis=1) * sin_signed


def _meta_state(meta_ref, g_ref, w_ref, cos_ref, sin_ref, kdec_ref, blk_ref, u_ref, r0_ref):
    a = _rms_norm(meta_ref[...], g_ref[...]).astype(BF16)
    u_ref[...] = _dot(a, w_ref[:, :SSM_WIDTH])
    k_off = SSM_WIDTH + RET_WIDTH
    v_off = SSM_WIDTH + 2 * RET_WIDTH
    cos = cos_ref[...]
    sin = sin_ref[...]
    for p in range(HEAD_PAIRS):
        k = _dot(a, w_ref[:, k_off + p * PAIR_DIM:k_off + (p + 1) * PAIR_DIM])
        v = _dot(a, w_ref[:, v_off + p * PAIR_DIM:v_off + (p + 1) * PAIR_DIM])
        k = jnp.concatenate([_rope(k[:, :HEAD_DIM], cos, sin), _rope(k[:, HEAD_DIM:], cos, sin)], axis=1)
        kd = (k * kdec_ref[p]).astype(BF16)
        r0_ref[p] = _dot_rows(kd, v.astype(BF16)) * blk_ref[...]


def _dot_rows(a, b):
    return lax.dot_general(a, b, (((0,), (0,)), ((), ())), preferred_element_type=F32)


def _mixer_kernel(x_ref, g_ref, w32_ref, cos_ref, sin_ref, mask_ref, qdec_ref, kdec_ref, bdec_ref, blk_ref,
                  meta_ref, cos_m_ref, sin_m_ref, kdec_m_ref, u_ref, y_ref, um_ref, w_ref, r0_ref, r_ref):
    first_tile = pl.program_id(1) == 0

    @pl.when(jnp.logical_and(pl.program_id(0) == 0, first_tile))
    def _():
        for c in range(0, IN_WIDTH, SSM_WIDTH):
            w_ref[:, c:c + SSM_WIDTH] = w32_ref[:, c:c + SSM_WIDTH].astype(BF16)
        _meta_state(meta_ref, g_ref, w_ref, cos_m_ref, sin_m_ref, kdec_m_ref, blk_ref, um_ref, r0_ref)

    @pl.when(first_tile)
    def _():
        r_ref[...] = r0_ref[...]

    off = SSM_WIDTH
    for b in range(MIXER_TILE // RET_BLOCK):
        bs = slice(b * RET_BLOCK, (b + 1) * RET_BLOCK)
        a = _rms_norm(x_ref[bs, :], g_ref[...]).astype(BF16)
        u_ref[bs, :] = _dot(a, w_ref[:, :SSM_WIDTH])
        q_all = _dot(a, w_ref[:, off:off + RET_WIDTH])
        k_all = _dot(a, w_ref[:, off + RET_WIDTH:off + 2 * RET_WIDTH])
        v_all = _dot(a, w_ref[:, off + 2 * RET_WIDTH:off + 3 * RET_WIDTH])
        gate = _dot(a, w_ref[:, off + 3 * RET_WIDTH:off + 4 * RET_WIDTH])
        cos = cos_ref[bs, :]
        sin = sin_ref[bs, :]
        def scores(h):
            hs = slice(h * HEAD_DIM, (h + 1) * HEAD_DIM)
            q = _rope(q_all[:, hs], cos, sin)
            k = _rope(k_all[:, hs], cos, sin)
            return q, k, _dot_t(q.astype(BF16), k.astype(BF16))

        ahead = scores(0)
        for p in range(HEAD_PAIRS):
            ps = slice(p * PAIR_DIM, (p + 1) * PAIR_DIM)
            pair = [ahead, scores(2 * p + 1)]
            if p + 1 < HEAD_PAIRS:
                ahead = scores(2 * p + 2)
            q = jnp.concatenate([pair[0][0], pair[1][0]], axis=1)
            k = jnp.concatenate([pair[0][1], pair[1][1]], axis=1)
            v = v_all[:, ps].astype(BF16)
            state = r_ref[p]
            cross = _dot((q * qdec_ref[p]).astype(BF16), state.astype(BF16))
            kv = _dot_rows((k * kdec_ref[p]).astype(BF16), v)
            r_ref[p] = state * bdec_ref[p] + kv * blk_ref[...]
            for half, (_, _, s) in enumerate(pair):
                h = 2 * p + half
                hs = slice(h * HEAD_DIM, (h + 1) * HEAD_DIM)
                ls = slice(half * HEAD_DIM, (half + 1) * HEAD_DIM)
                o = _dot((s * mask_ref[h]).astype(BF16), v[:, ls]) + cross[:, ls]
                mu = jnp.mean(o, axis=-1, keepdims=True)
                d = o - mu
                var = jnp.mean(d * d, axis=-1, keepdims=True)
                gt = gate[:, hs]
                y_ref[bs, hs] = (gt * _sigmoid(gt) * d * lax.rsqrt(var + EPS)).astype(BF16)


def _mixer_call(x, g, w_in, cos, sin, mask, qdec, kdec, bdec, blk, meta, cos_m, sin_m, kdec_m):
    bsz, seq, _ = x.shape
    const = lambda a, **kw: pl.BlockSpec(a.shape, lambda b, i: (0,) * a.ndim, **kw)
    return pl.pallas_call(
        _mixer_kernel,
        grid=(bsz, seq // MIXER_TILE),
        in_specs=[
            pl.BlockSpec((None, MIXER_TILE, D_MODEL), lambda b, i: (b, i, 0)),
            const(g),
            const(w_in, pipeline_mode=pl.Buffered(1)),
            pl.BlockSpec((MIXER_TILE, HEAD_DIM), lambda b, i: (i, 0)),
            pl.BlockSpec((MIXER_TILE, HEAD_DIM), lambda b, i: (i, 0)),
            const(mask), const(qdec), const(kdec), const(bdec), const(blk),
            const(meta), const(cos_m), const(sin_m), const(kdec_m),
        ],
        out_specs=(
            pl.BlockSpec((None, MIXER_TILE, SSM_WIDTH), lambda b, i: (b, i, 0)),
            pl.BlockSpec((None, MIXER_TILE, RET_WIDTH), lambda b, i: (b, i, 0)),
            pl.BlockSpec((N_META, SSM_WIDTH), lambda b, i: (0, 0)),
        ),
        out_shape=(jax.ShapeDtypeStruct((bsz, seq, SSM_WIDTH), F32),
                   jax.ShapeDtypeStruct((bsz, seq, RET_WIDTH), BF16),
                   jax.ShapeDtypeStruct((N_META, SSM_WIDTH), F32)),
        scratch_shapes=[pltpu.VMEM((D_MODEL, IN_WIDTH), BF16),
                        pltpu.VMEM((HEAD_PAIRS, PAIR_DIM, PAIR_DIM), F32),
                        pltpu.VMEM((HEAD_PAIRS, PAIR_DIM, PAIR_DIM), F32)],
        compiler_params=pltpu.CompilerParams(
            dimension_semantics=("arbitrary", "arbitrary"), vmem_limit_bytes=VMEM_LIMIT),
        name="mixer_call",
    )(x, g, w_in, cos, sin, mask, qdec, kdec, bdec, blk, meta, cos_m, sin_m, kdec_m)


def _dot_t(a, b):
    return lax.dot_general(a, b, (((1,), (1,)), ((), ())), preferred_element_type=F32)


def _chunk_transpose(arrs):
    n = len(arrs)
    chunk = lax.broadcasted_iota(I32, (1, LANES), 1) // SSM_GROUP
    arrs = list(arrs)
    s = n // 2
    while s:
        keep = (chunk & s) == 0
        nxt = list(arrs)
        for i in range(n):
            if i & s == 0:
                lo, hi = arrs[i], arrs[i + s]
                nxt[i] = jnp.where(keep, lo, pltpu.roll(hi, s * SSM_GROUP, axis=1))
                nxt[i + s] = jnp.where(keep, pltpu.roll(lo, LANES - s * SSM_GROUP, axis=1), hi)
        arrs = nxt
        s //= 2
    return arrs


def _s5_kernel(u_lo_ref, u_hi_ref, um_ref, krow_ref, bmat_ref, cre_ref, cim_ref, ar_ref, ai_ref, wglu_ref,
               y_lo_ref, y_hi_ref, t0_ref, ug_ref, yg_ref):
    n_blocks = u_lo_ref.shape[0] // S5_BLOCK

    @pl.when(pl.program_id(0) == 0)
    def _():
        lane = lax.broadcasted_iota(I32, (SSM_GROUP, S5_LANES), 1)
        for g in range(SSM_GROUPS):
            k0 = krow_ref[g]
            for a in range(S5_BLOCK):
                blk = k0 if a == 0 else jnp.where(lane >= a * SSM_GROUP,
                                                  pltpu.roll(k0, a * SSM_GROUP, axis=1), 0.0)
                t0_ref[g, a * SSM_GROUP:(a + 1) * SSM_GROUP, :] = blk.astype(BF16)

    per_tile = LANES // SSM_GROUP
    for half, uh_ref in enumerate((u_lo_ref, u_hi_ref)):
        for t in range(S5_BLOCK // per_tile):
            words = [pltpu.bitcast(uh_ref[pl.ds(t * per_tile + k, n_blocks, stride=S5_BLOCK), :].astype(BF16),
                                   jnp.uint32) for k in range(per_tile)]
            for m, w in enumerate(_chunk_transpose(words)):
                ug_ref[half * per_tile + m, :, t * LANES:(t + 1) * LANES] = pltpu.bitcast(w, BF16)

    re, im, re0, im0 = [], [], [], []
    for p in range(SSM_GROUPS // 2):
        g0, g1 = 2 * p, 2 * p + 1
        v = _dot(ug_ref[g0], bmat_ref[g0]) + _dot(ug_ref[g1], bmat_ref[g1])
        v0 = (_dot(um_ref[g0].astype(BF16), bmat_ref[g0]) + _dot(um_ref[g1].astype(BF16), bmat_ref[g1]))[0:1]
        re.append(v[:, :LANES])
        im.append(v[:, LANES:])
        re0.append(v0[:, :LANES])
        im0.append(v0[:, LANES:])
    re, im, re0, im0 = (jnp.concatenate(parts, axis=1) for parts in (re, im, re0, im0))

    row = lax.broadcasted_iota(I32, re.shape, 0)
    ar, ai = ar_ref[0:1], ai_ref[0:1]
    re = re + jnp.where(row == 0, ar * re0 - ai * im0, 0.0)
    im = im + jnp.where(row == 0, ar * im0 + ai * re0, 0.0)
    d = 1
    while d < SCAN_ROWS:
        ar, ai = ar_ref[d - 1:d], ai_ref[d - 1:d]
        inside = row % SCAN_ROWS >= d
        sre = jnp.where(inside, pltpu.roll(re, d, axis=0), 0.0)
        sim = jnp.where(inside, pltpu.roll(im, d, axis=0), 0.0)
        re, im = re + ar * sre - ai * sim, im + ar * sim + ai * sre
        d *= 2
    ar, ai = ar_ref[...], ai_ref[...]
    re_tiles, im_tiles = [re[:SCAN_ROWS]], [im[:SCAN_ROWS]]
    for r in range(1, n_blocks // SCAN_ROWS):
        cre_ = re_tiles[-1][SCAN_ROWS - 1:SCAN_ROWS]
        cim_ = im_tiles[-1][SCAN_ROWS - 1:SCAN_ROWS]
        rs = slice(r * SCAN_ROWS, (r + 1) * SCAN_ROWS)
        re_tiles.append(re[rs] + ar * cre_ - ai * cim_)
        im_tiles.append(im[rs] + ar * cim_ + ai * cre_)
    re = jnp.concatenate(re_tiles, axis=0)
    im = jnp.concatenate(im_tiles, axis=0)
    pre = jnp.where(row == 0, re0, pltpu.roll(re, 1, axis=0)).astype(BF16)
    pim = jnp.where(row == 0, im0, pltpu.roll(im, 1, axis=0)).astype(BF16)

    for g in range(SSM_GROUPS):
        ps = slice((g // 2) * LANES, (g // 2 + 1) * LANES)
        yg_ref[g] = (_dot(ug_ref[g], t0_ref[g]) + _dot_t(pre[:, ps], cre_ref[g])
                     + _dot_t(pim[:, ps], cim_ref[g]))

    wglu = wglu_ref[...]
    for t in range(S5_BLOCK // per_tile):
        ts = slice(t * LANES, (t + 1) * LANES)
        halves = [_chunk_transpose([yg_ref[half * per_tile + m, :, ts] for m in range(per_tile)])
                  for half in range(SSM_GROUPS // per_tile)]
        for k in range(per_tile):
            y = jnp.concatenate([h[k] for h in halves], axis=1)
            y = jax.nn.gelu(y, approximate=True)
            y = y * _sigmoid(_dot(y.astype(BF16), wglu))
            i = t * per_tile + k
            y_lo_ref[pl.ds(i, n_blocks, stride=S5_BLOCK), :] = y[:, :LANES]
            y_hi_ref[pl.ds(i, n_blocks, stride=S5_BLOCK), :] = y[:, LANES:]


def _s5_call(u, um, krow, bmat, cre, cim, ar, ai, wglu):
    bsz, seq, _ = u.shape
    n_blocks = seq // S5_BLOCK
    const = lambda a: pl.BlockSpec(a.shape, lambda b: (0,) * a.ndim)
    return pl.pallas_call(
        _s5_kernel,
        grid=(bsz,),
        in_specs=[pl.BlockSpec((None, seq, LANES), lambda b: (b, 0, 0)),
                  pl.BlockSpec((None, seq, LANES), lambda b: (b, 0, 1)),
                  const(um), const(krow), const(bmat), const(cre), const(cim), const(ar), const(ai),
                  const(wglu)],
        out_specs=(pl.BlockSpec((None, seq, LANES), lambda b: (b, 0, 0)),
                   pl.BlockSpec((None, seq, LANES), lambda b: (b, 0, 0))),
        out_shape=(jax.ShapeDtypeStruct((bsz, seq, LANES), F32),
                   jax.ShapeDtypeStruct((bsz, seq, LANES), F32)),
        scratch_shapes=[pltpu.VMEM((SSM_GROUPS, S5_LANES, S5_LANES), BF16),
                        pltpu.VMEM((SSM_GROUPS, n_blocks, S5_LANES), BF16),
                        pltpu.VMEM((SSM_GROUPS, n_blocks, S5_LANES), F32)],
        compiler_params=pltpu.CompilerParams(
            dimension_semantics=("arbitrary",), vmem_limit_bytes=VMEM_LIMIT),
        name="s5_call",
    )(u, u, um, krow, bmat, cre, cim, ar, ai, wglu)


def _s5_operators(lam_re, lam_im, log_dt, b_re, b_im, c_re, c_im, d_skip):
    n_groups = lam_re.shape[0]
    lam = lax.complex(lam_re, lam_im)
    lam_dt = lam * jnp.exp(log_dt)[:, None]
    lam_bar = jnp.exp(lam_dt)
    b_bar = ((lam_bar - 1.0) / lam)[..., None] * lax.complex(b_re, b_im)
    c = lax.complex(c_re, c_im)
    tau = jnp.arange(S5_BLOCK + 1, dtype=F32)
    pows = jnp.exp(lam_dt[None] * tau[:, None, None])
    kern = jnp.real(jnp.einsum('ghp,tgp,gpk->gkth', c, pows[:S5_BLOCK], b_bar))
    skip = (jnp.eye(SSM_GROUP, dtype=F32)[None, :, None, :] * d_skip[:, None, None, :]
            * (tau[:S5_BLOCK] == 0).astype(F32)[None, None, :, None])
    krow = (kern + skip).reshape(n_groups, SSM_GROUP, S5_LANES)
    even = (jnp.arange(n_groups) % 2 == 0)[:, None, None]
    pair_pad = lambda m: jnp.concatenate([jnp.where(even, m, 0.0), jnp.where(even, 0.0, m)], axis=-1)
    bm = (pows[S5_BLOCK - 1 - jnp.arange(S5_BLOCK)].transpose(1, 0, 2)[:, :, None, :]
          * b_bar.transpose(0, 2, 1)[:, None, :, :]).reshape(n_groups, S5_LANES, SSM_STATE)
    bmat = jnp.concatenate([pair_pad(jnp.real(bm)), pair_pad(jnp.imag(bm))], axis=-1)
    cm = (pows[1:].transpose(1, 0, 2)[:, :, None, :] * c[:, None, :, :]).reshape(n_groups, S5_LANES, SSM_STATE)
    cre, cim = pair_pad(jnp.real(cm)), pair_pad(-jnp.imag(cm))
    step = S5_BLOCK * (1.0 + jnp.arange(SCAN_ROWS, dtype=F32))
    adec = jnp.exp(lam_dt[None, :, :] * step[:, None, None]).reshape(SCAN_ROWS, n_groups * SSM_STATE)
    return krow, bmat.astype(BF16), cre.astype(BF16), cim.astype(BF16), jnp.real(adec), jnp.imag(adec)


def _first_hit(values, target):
    hits, taken = [], None
    for v in values:
        hit = (v >= target) if taken is None else jnp.logical_and(v >= target, jnp.logical_not(taken))
        taken = hit if taken is None else jnp.logical_or(taken, hit)
        hits.append(hit)
    return hits


def _stack_rows(rows, n_rows):
    idx = lax.broadcasted_iota(I32, (n_rows, rows[0].shape[1]), 0)
    out = jnp.zeros((n_rows, rows[0].shape[1]), F32)
    for k, r in enumerate(rows):
        out = jnp.where(idx == k, r, out)
    return out


def _project(rs, x_ref, ys_lo_ref, ys_hi_ref, yr_ref, wout_ref, h_ref):
    ys = jnp.concatenate([ys_lo_ref[rs, :], ys_hi_ref[rs, :]], axis=1).astype(BF16)
    h = x_ref[rs, :] + _dot(ys, wout_ref[:SSM_WIDTH, :]) + _dot(yr_ref[rs, :], wout_ref[SSM_WIDTH:, :])
    h_ref[rs, :] = h.astype(BF16)
    return h


def _route(lt, tri_ref):
    gl = [lt[g:g + 1, :] for g in range(N_GROUPS)]
    gmax = functools.reduce(jnp.maximum, gl)
    g_w = 1.0 / functools.reduce(lambda a, b: a + b, [jnp.exp(l - gmax) for l in gl])
    sel = _first_hit(gl, gmax)
    ev = []
    for e in range(EXPERTS_PER_GROUP):
        acc = jnp.zeros_like(gmax)
        for g in range(N_GROUPS):
            k = N_GROUPS + g * EXPERTS_PER_GROUP + e
            acc = jnp.where(sel[g], lt[k:k + 1, :], acc)
        ev.append(acc)
    m1 = functools.reduce(jnp.maximum, ev)
    first = _first_hit(ev, m1)
    rest = [jnp.where(f, -jnp.inf, v) for f, v in zip(first, ev)]
    m2 = functools.reduce(jnp.maximum, rest)
    second = _first_hit(rest, m2)
    e2 = jnp.exp(m2 - m1)
    w1 = g_w / (1.0 + e2)
    w2 = e2 * w1
    combine = [jnp.where(f, w1, 0.0) + jnp.where(s, w2, 0.0) for f, s in zip(first, second)]

    sel_f = [jnp.where(s, 1.0, 0.0) for s in sel]
    incl = _dot(_stack_rows(sel_f, 8).astype(BF16), tri_ref[...])
    dest = jnp.zeros_like(gmax)
    seg_start = jnp.zeros((1, 1), F32)
    counts = []
    for g in range(N_GROUPS):
        run = incl[g:g + 1, :]
        cnt = run[:, TILE - 1:TILE]
        counts.append(cnt)
        dest = dest + sel_f[g] * (seg_start + run - 1.0)
        seg_start = seg_start + PIECE * jnp.floor((cnt + (PIECE - 1.0)) * (1.0 / PIECE))
    return combine, dest, counts


def _proj_kernel(x_ref, ys_lo_ref, ys_hi_ref, yr_ref, wout32_ref, g_ref, wr_ref, br_ref,
                 tri_ref, h_ref, stage_ref, dest_ref, cnt_ref, wout_ref):
    @pl.when(pl.program_id(0) == 0)
    def _():
        wout_ref[...] = wout32_ref[...].astype(BF16)

    tiles = range(PROJ_TILES)
    proj_refs = (x_ref, ys_lo_ref, ys_hi_ref, yr_ref, wout_ref, h_ref)
    chunks = [[slice(tile * TILE + b * ROW_CHUNK, tile * TILE + (b + 1) * ROW_CHUNK)
               for b in range(TILE // ROW_CHUNK)] for tile in tiles]
    h_parts = [[_project(rs, *proj_refs) for rs in chunks[tile]] for tile in tiles]

    t_parts, logits, staged = [], [], []
    for tile in tiles:
        parts, rows, inv = [], [], []
        for h in h_parts[tile]:
            t = _rms_norm(h, g_ref[...])
            t_hi = t.astype(BF16)
            parts.append((t_hi, (t - t_hi.astype(F32)).astype(BF16)))
            peak = jnp.maximum(jnp.max(jnp.abs(t), axis=1, keepdims=True), 1e-30)
            rows.append((t * (FP8_HEADROOM / peak)).astype(BF16))
            inv.append(peak * (1.0 / FP8_HEADROOM))
        staged.append((jnp.concatenate(rows, axis=0), jnp.concatenate(inv, axis=0)))
        lt_parts = []
        for t_hi, t_lo in parts:
            both = _dot_t(wr_ref[...], t_hi)
            lt_parts.append(both[:ROUTE_ROWS] + both[ROUTE_ROWS:] + _dot_t(wr_ref[:ROUTE_ROWS, :], t_lo))
        t_parts.append(parts)
        logits.append(jnp.concatenate(lt_parts, axis=1) + br_ref[...])

    routed = [_route(logits[tile], tri_ref) for tile in tiles]

    for tile in tiles:
        combine, dest, counts = routed[tile]
        dest_ref[tile] = _stack_rows([dest], 8)
        cnt_ref[tile] = _stack_rows([c + jnp.zeros((1, LANES), F32) for c in counts], 8)
        perm = jnp.where(lax.broadcasted_iota(I32, (SORTED_ROWS, TILE), 0) == dest.astype(I32),
                         1.0, 0.0).astype(BF16)
        c_hi = [c.astype(BF16).astype(F32) for c in combine]
        c_lo = [c - hi for c, hi in zip(combine, c_hi)]
        cw = _stack_rows(c_hi + c_lo, LANES).T
        rows, inv = staged[tile]
        inv_hi = inv.astype(BF16).astype(F32)
        lane = lax.broadcasted_iota(I32, (1, LANES), 1)
        cw = jnp.where(lane == INV_LANE, inv_hi, jnp.where(lane == INV_LANE + 1, inv - inv_hi, cw))
        t_ext = jnp.concatenate([rows, cw.astype(BF16)], axis=1)
        stage_ref[tile * SORTED_ROWS:(tile + 1) * SORTED_ROWS, :] = _dot(perm, t_ext).astype(BF16)


def _proj_call(x, ys_lo, ys_hi, yr, wout, g, wr, br, tri):
    n_tok = x.shape[0]
    n_tiles = n_tok // TILE
    const = lambda *shape: pl.BlockSpec(shape, lambda i: (0,) * len(shape))
    rows = lambda width: pl.BlockSpec((PROJ_TILES * TILE, width), lambda i: (i, 0))
    return pl.pallas_call(
        _proj_kernel,
        grid=(n_tiles // PROJ_TILES,),
        in_specs=[
            rows(D_MODEL), rows(LANES), rows(LANES), rows(RET_WIDTH),
            pl.BlockSpec((D_MODEL, D_MODEL), lambda i: (0, 0), pipeline_mode=pl.Buffered(1)), const(1, D_MODEL),
            const(2 * ROUTE_ROWS, D_MODEL), const(ROUTE_ROWS, 1), const(TILE, TILE),
        ],
        out_specs=(rows(D_MODEL),
                   pl.BlockSpec((PROJ_TILES * SORTED_ROWS, EXT_WIDTH), lambda i: (i, 0)),
                   pl.BlockSpec((PROJ_TILES, 8, TILE), lambda i: (i, 0, 0)),
                   pl.BlockSpec((PROJ_TILES, 8, LANES), lambda i: (i, 0, 0))),
        scratch_shapes=[pltpu.VMEM((D_MODEL, D_MODEL), BF16)],
        out_shape=(jax.ShapeDtypeStruct((n_tok, D_MODEL), BF16),
                   jax.ShapeDtypeStruct((n_tiles * SORTED_ROWS, EXT_WIDTH), BF16),
                   jax.ShapeDtypeStruct((n_tiles, 8, TILE), F32),
                   jax.ShapeDtypeStruct((n_tiles, 8, LANES), F32)),
        compiler_params=pltpu.CompilerParams(
            dimension_semantics=("arbitrary",), vmem_limit_bytes=VMEM_LIMIT),
        name="proj_call",
    )(x, ys_lo, ys_hi, yr, wout, g, wr, br, tri)


def _sort_tables(cnt, n_steps):
    n_tiles = cnt.shape[0]
    npc = (cnt + PIECE - 1) // PIECE
    seg = jnp.cumsum(npc, axis=1) - npc
    before = jnp.cumsum(npc, axis=0) - npc
    n_tile_g = (jnp.sum(npc, axis=0) + MOE_PIECES - 1) // MOE_PIECES
    t_off = jnp.cumsum(n_tile_g) - n_tile_g
    j = jnp.arange(SORTED_PIECES, dtype=I32)[None, :, None]
    in_g = jnp.logical_and(j >= seg[:, None, :], j < (seg + npc)[:, None, :])
    pos = jnp.sum(jnp.where(in_g, MOE_PIECES * t_off[None, None, :] + before[:, None, :] + j - seg[:, None, :], 0),
                  axis=-1)
    valid = jnp.any(in_g, axis=-1)
    stage_piece = jnp.arange(n_tiles * SORTED_PIECES, dtype=I32).reshape(n_tiles, SORTED_PIECES)
    n_slots = n_steps * MOE_PIECES
    src = jnp.zeros((n_slots,), I32).at[jnp.where(valid, pos, n_slots).reshape(-1)].set(
        stage_piece.reshape(-1), mode='drop')
    steps = jnp.arange(n_steps, dtype=I32)
    g_step = jnp.minimum(jnp.sum(steps[:, None] >= (t_off + n_tile_g)[None, :], axis=1), N_GROUPS - 1)
    n_live = jnp.sum(n_tile_g).reshape(1)
    back = jnp.where(valid, pos, 0).reshape(-1)
    return src, g_step.astype(I32), n_live.astype(I32), back.astype(I32)


def _piece_copy(src_ref, piece, buf_ref, slot, j, sem_ref):
    start = piece * PIECE if isinstance(piece, int) else pl.multiple_of(piece * PIECE, PIECE)
    return pltpu.make_async_copy(src_ref.at[pl.ds(start, PIECE)],
                                 buf_ref.at[slot, pl.ds(j * PIECE, PIECE)], sem_ref.at[slot])


def _fetch_pieces(table_ref, src_ref, buf_ref, sem_ref, row, slot, n_pieces):
    for j in range(n_pieces):
        _piece_copy(src_ref, table_ref[row * n_pieces + j], buf_ref, slot, j, sem_ref).start()


def _wait_pieces(src_ref, buf_ref, sem_ref, slot, n_pieces):
    for j in range(n_pieces):
        _piece_copy(src_ref, 0, buf_ref, slot, j, sem_ref).wait()


def _gather_pieces(table_ref, src_ref, buf_ref, sem_ref, n_pieces):
    step = pl.program_id(0)
    last = pl.num_programs(0) - 1
    args = (src_ref, buf_ref, sem_ref)

    @pl.when(step == 0)
    def _():
        _fetch_pieces(table_ref, *args, step, 0, n_pieces)

    _wait_pieces(*args, step % 2, n_pieces)

    def fetch_next():
        _fetch_pieces(table_ref, *args, jnp.minimum(step + 1, last), (step + 1) % 2, n_pieces)

    def drain():
        @pl.when(step == last)
        def _():
            _wait_pieces(*args, (step + 1) % 2, n_pieces)

    return fetch_next, drain


def _to_fp8(x, headroom=FP8_HEADROOM):
    peak = jnp.maximum(jnp.max(jnp.abs(x), axis=(0, 1), keepdims=True), 1e-30)
    return (x * (headroom / peak)).astype(FP8), peak * (1.0 / headroom)


def _moe_kernel(src_ref, gstep_ref, nlive_ref, stage_ref, wg_ref, wu_ref, wd_ref, y_ref,
                buf_ref, sem_ref, wgb_ref, wub_ref, wdb_ref, inv_ref):
    step = pl.program_id(0)
    fetch_next, drain = _gather_pieces(src_ref, stage_ref, buf_ref, sem_ref, MOE_PIECES)

    @pl.when(jnp.logical_or(step == 0, gstep_ref[step] != gstep_ref[jnp.maximum(step - 1, 0)]))
    def _():
        ones = jnp.ones((1, LANES), F32)
        for e in range(EXPERTS_PER_GROUP):
            wgb_ref[e], inv = _to_fp8(wg_ref[e])
            inv_ref[e:e + 1, :] = inv * ones
            wub_ref[e], inv = _to_fp8(wu_ref[e])
            inv_ref[EXPERTS_PER_GROUP + e:EXPERTS_PER_GROUP + e + 1, :] = inv * ones
        peak = functools.reduce(jnp.maximum, [jnp.max(jnp.abs(wd_ref[e]), axis=(0, 1), keepdims=True)
                                              for e in range(EXPERTS_PER_GROUP)])
        peak = jnp.maximum(peak, 1e-30)
        for e in range(EXPERTS_PER_GROUP):
            wdb_ref[e * EXPERT_FF:(e + 1) * EXPERT_FF, :] = (wd_ref[e] * (FP8_HEADROOM / peak)).astype(FP8)
        inv_ref[2 * EXPERTS_PER_GROUP:2 * EXPERTS_PER_GROUP + 1, :] = peak * (1.0 / FP8_HEADROOM) * ones

    @pl.when(step < nlive_ref[0])
    def _():
        slot = step % 2
        fetch_next()

        def up(rs):
            t = buf_ref[slot, rs, :D_MODEL].astype(FP8)
            t_inv = (buf_ref[slot, rs, D_MODEL + INV_LANE:D_MODEL + INV_LANE + 1].astype(F32)
                     + buf_ref[slot, rs, D_MODEL + INV_LANE + 1:D_MODEL + INV_LANE + 2].astype(F32))
            out = []
            for e in range(EXPERTS_PER_GROUP):
                g_inv = inv_ref[e:e + 1, 0:1] * t_inv
                u_inv = inv_ref[EXPERTS_PER_GROUP + e:EXPERTS_PER_GROUP + e + 1, 0:1] * t_inv
                out.append((_dot(t, wgb_ref[e]) * g_inv, _dot(t, wub_ref[e]), u_inv))
            return out

        def down(rs, hidden):
            cw = buf_ref[slot, rs, D_MODEL:].astype(F32)
            acts = []
            for e, (hg, hu, u_inv) in enumerate(hidden):
                c = (cw[:, e:e + 1] + cw[:, EXPERTS_PER_GROUP + e:EXPERTS_PER_GROUP + e + 1]) * u_inv
                acts.append(hg * _sigmoid(hg) * hu * c)
            act, a_inv = _to_fp8(jnp.concatenate(acts, axis=1))
            d_inv = inv_ref[2 * EXPERTS_PER_GROUP:2 * EXPERTS_PER_GROUP + 1, 0:1] * a_inv
            y_ref[rs, :] = (_dot(act, wdb_ref[...]) * d_inv).astype(BF16)

        chunks = [slice(b * ROW_CHUNK, (b + 1) * ROW_CHUNK) for b in range(MOE_TILE // ROW_CHUNK)]
        hidden = [up(rs) for rs in chunks]
        for rs, hid in zip(chunks, hidden):
            down(rs, hid)

    @pl.when(step >= nlive_ref[0])
    def _():
        fetch_next()
        y_ref[...] = jnp.zeros_like(y_ref)

    drain()


def _moe_call(src, g_step, n_live, stage, wg, wu, wd, n_steps):
    grp = lambda shape: pl.BlockSpec((EXPERTS_PER_GROUP,) + shape, lambda s, src, gs, nl: (gs[s], 0, 0))
    return pl.pallas_call(
        _moe_kernel,
        grid_spec=pltpu.PrefetchScalarGridSpec(
            num_scalar_prefetch=3,
            grid=(n_steps,),
            in_specs=[pl.BlockSpec(memory_space=pl.ANY),
                      grp((D_MODEL, EXPERT_FF)), grp((D_MODEL, EXPERT_FF)), grp((EXPERT_FF, D_MODEL))],
            out_specs=pl.BlockSpec((MOE_TILE, D_MODEL), lambda s, src, gs, nl: (s, 0)),
            scratch_shapes=[pltpu.VMEM((2, MOE_TILE, EXT_WIDTH), BF16), pltpu.SemaphoreType.DMA((2,)),
                            pltpu.VMEM((EXPERTS_PER_GROUP, D_MODEL, EXPERT_FF), FP8),
                            pltpu.VMEM((EXPERTS_PER_GROUP, D_MODEL, EXPERT_FF), FP8),
                            pltpu.VMEM((EXPERTS_PER_GROUP * EXPERT_FF, D_MODEL), FP8),
                            pltpu.VMEM((16, LANES), F32)],
        ),
        out_shape=jax.ShapeDtypeStruct((n_steps * MOE_TILE, D_MODEL), BF16),
        compiler_params=pltpu.CompilerParams(
            dimension_semantics=("arbitrary",), vmem_limit_bytes=VMEM_LIMIT),
        name="moe_call",
    )(src, g_step, n_live, stage, wg, wu, wd)


def _final_kernel(back_ref, ysort_ref, h_ref, dest_ref, g_ref, o_ref, buf_ref, sem_ref):
    step = pl.program_id(0)
    fetch_next, drain = _gather_pieces(back_ref, ysort_ref, buf_ref, sem_ref, FINAL_TILES * SORTED_PIECES)
    fetch_next()
    for t in range(FINAL_TILES):
        dest = _stack_rows([dest_ref[t, 0:1, :]], LANES).T[:, 0:1].astype(I32)
        unperm = jnp.where(lax.broadcasted_iota(I32, (TILE, SORTED_ROWS), 1) == dest, 1.0, 0.0).astype(BF16)
        rows = buf_ref[step % 2, t * SORTED_ROWS:(t + 1) * SORTED_ROWS, :]
        for b in range(TILE // ROW_CHUNK):
            rs = slice(b * ROW_CHUNK, (b + 1) * ROW_CHUNK)
            os = slice(t * TILE + b * ROW_CHUNK, t * TILE + (b + 1) * ROW_CHUNK)
            o_ref[os, :] = _rms_norm(h_ref[os, :].astype(F32) + _dot(unperm[rs], rows), g_ref[...])
    drain()


def _final_call(back, ysort, h, dest, g):
    n_tok = h.shape[0]
    return pl.pallas_call(
        _final_kernel,
        grid_spec=pltpu.PrefetchScalarGridSpec(
            num_scalar_prefetch=1,
            grid=(n_tok // (FINAL_TILES * TILE),),
            in_specs=[pl.BlockSpec(memory_space=pl.ANY),
                      pl.BlockSpec((FINAL_TILES * TILE, D_MODEL), lambda i, back: (i, 0)),
                      pl.BlockSpec((FINAL_TILES, 8, TILE), lambda i, back: (i, 0, 0)),
                      pl.BlockSpec((1, D_MODEL), lambda i, back: (0, 0))],
            out_specs=pl.BlockSpec((FINAL_TILES * TILE, D_MODEL), lambda i, back: (i, 0)),
            scratch_shapes=[pltpu.VMEM((2, FINAL_TILES * SORTED_ROWS, D_MODEL), BF16),
                            pltpu.SemaphoreType.DMA((2,))],
        ),
        out_shape=jax.ShapeDtypeStruct((n_tok, D_MODEL), F32),
        compiler_params=pltpu.CompilerParams(
            dimension_semantics=("arbitrary",), vmem_limit_bytes=VMEM_LIMIT),
        name="final_call",
    )(back, ysort, h, dest, g)


def _rope_tables(length):
    pos = np.arange(length, dtype=np.float32)
    inv_freq = np.float32(ROPE_BASE) ** (-np.arange(0, HEAD_DIM, 2, dtype=np.float32) / np.float32(HEAD_DIM))
    ang = pos[:, None] * inv_freq[None, :]
    cos, sin = np.cos(ang), np.sin(ang)
    return np.concatenate([cos, cos], axis=-1), np.concatenate([-sin, sin], axis=-1)


def _retention_tables():
    f32 = np.float32
    gamma = f32(1.0) - f32(2.0) ** (f32(-5.0) - np.arange(HEADS, dtype=f32))
    log_g = np.log(gamma)[:, None, None]
    scale = f32(HEAD_DIM ** -0.5)
    idx = np.arange(RET_BLOCK)
    dist = np.abs(idx[:, None] - idx[None, :]).astype(f32)
    visible = (idx[None, :] // CHUNK) <= (idx[:, None] // CHUNK)
    mask = np.where(visible[None], np.exp(log_g * dist[None]), f32(0.0)) * scale
    ones = np.ones((1, 1, HEAD_DIM), f32)
    idx_f = idx.astype(f32)[None, :, None]
    pair = lambda a: a.reshape(HEAD_PAIRS, 2, a.shape[1], HEAD_DIM).transpose(0, 2, 1, 3).reshape(
        HEAD_PAIRS, a.shape[1], PAIR_DIM)
    qdec = pair(np.exp(log_g * (idx_f + f32(1.0))) * ones)
    kdec = pair(np.exp(log_g * (f32(RET_BLOCK - 1.0) - idx_f)) * scale * ones)
    meta_idx = np.arange(N_META, dtype=f32)[None, :, None]
    kdec_meta = pair(np.exp(log_g * (f32(N_META - 1.0) - meta_idx)) * scale * ones)
    blk = np.kron(np.eye(2, dtype=f32), np.ones((HEAD_DIM, HEAD_DIM), f32))
    bdec = np.exp(log_g * f32(RET_BLOCK)).reshape(HEAD_PAIRS, 2)
    bdec = np.stack([np.kron(np.diag(b), np.ones((HEAD_DIM, HEAD_DIM), f32)) for b in bdec])
    return tuple(a.astype(f32) for a in (mask, qdec, kdec, bdec, blk, kdec_meta))


def kernel(x, meta_tokens, norm_mix_g, w_in, ssm_lambda_re, ssm_lambda_im, ssm_log_dt, ssm_b_re, ssm_b_im, ssm_c_re, ssm_c_im, ssm_d, w_glu, w_out, norm_ffn_g, w_router_group, b_router_group, w_router_expert, b_router_expert, w_gate, w_up, w_down, norm_final_g):
    bsz, seq, _ = x.shape
    assert seq % TILE == 0 and seq % MIXER_TILE == 0 and MIXER_TILE % RET_BLOCK == 0 and RET_BLOCK % CHUNK == 0
    n_blocks = seq // S5_BLOCK
    assert n_blocks % SCAN_ROWS == 0
    n_tok = bsz * seq
    assert n_tok % (FINAL_TILES * TILE) == 0 and n_tok % (PROJ_TILES * TILE) == 0
    n_tiles = n_tok // TILE
    n_steps = -(-n_tiles * (TILE_PIECES + N_GROUPS - 1) // MOE_PIECES) + N_GROUPS

    cos, sin = _rope_tables(N_META + seq)
    mask, qdec, kdec, bdec, blk, kdec_meta = _retention_tables()
    g_mix = norm_mix_g[0][None, :]
    u, y_ret, u_meta = _mixer_call(x, g_mix, w_in[0], cos[N_META:], sin[N_META:], mask, qdec, kdec, bdec, blk,
                                   meta_tokens, cos[:N_META], sin[:N_META], kdec_meta)

    s5_ops = _s5_operators(
        ssm_lambda_re[0], ssm_lambda_im[0], ssm_log_dt[0], ssm_b_re[0], ssm_b_im[0],
        ssm_c_re[0], ssm_c_im[0], ssm_d[0])
    um = u_meta.reshape(S5_BLOCK, SSM_GROUPS, SSM_GROUP).transpose(1, 0, 2).reshape(SSM_GROUPS, 1, S5_LANES)
    um = jnp.pad(um, ((0, 0), (0, 7), (0, 0)))
    y_lo, y_hi = _s5_call(u, um, *s5_ops, w_glu[0].astype(BF16))

    w_r = jnp.concatenate(
        [w_router_group[0].T, w_router_expert[0].transpose(0, 2, 1).reshape(N_EXPERTS, D_MODEL)], axis=0)
    w_r = jnp.pad(w_r, ((0, ROUTE_ROWS - w_r.shape[0]), (0, 0)))
    b_r = jnp.concatenate([b_router_group[0], b_router_expert[0].reshape(-1)])
    b_r = jnp.pad(b_r, (0, ROUTE_ROWS - b_r.shape[0]))[:, None]
    w_r_hi = w_r.astype(BF16)
    w_r = jnp.concatenate([w_r_hi, (w_r - w_r_hi.astype(F32)).astype(BF16)], axis=0)
    tri = jnp.asarray(np.arange(TILE)[:, None] <= np.arange(TILE)[None, :], BF16)

    h, stage, dest, cnt = _proj_call(
        x.reshape(n_tok, D_MODEL), y_lo.reshape(n_tok, LANES), y_hi.reshape(n_tok, LANES),
        y_ret.reshape(n_tok, RET_WIDTH),
        w_out[0], norm_ffn_g[0][None, :], w_r, b_r, tri)
    src, g_step, n_live, back = _sort_tables(cnt[:, :N_GROUPS, 0].astype(I32), n_steps)
    y_sorted = _moe_call(src, g_step, n_live, stage, w_gate[0], w_up[0], w_down[0], n_steps)
    out = _final_call(back, y_sorted, h, dest, norm_final_g[None, :])
    return out.reshape(bsz, seq, D_MODEL)
```

```python
import functools

import jax
import jax.numpy as jnp
import numpy as np
from jax import lax
from jax.experimental import pallas as pl
from jax.experimental.pallas import tpu as pltpu

D_MODEL = 1024
N_META = 16
CHUNK = 64
EPS = 1e-6
SSM_WIDTH = 256
SSM_GROUP = 16
SSM_GROUPS = 16
SSM_STATE = 64
RET_WIDTH = 768
HEAD_DIM = 128
HEADS = 6
HEAD_PAIRS = HEADS // 2
PAIR_DIM = 2 * HEAD_DIM
ROPE_BASE = 10000.0
IN_WIDTH = SSM_WIDTH + 4 * RET_WIDTH
N_GROUPS = 4
EXPERTS_PER_GROUP = 4
N_EXPERTS = 16
EXPERT_FF = 256

S5_BLOCK = 16
S5_LANES = S5_BLOCK * SSM_GROUP
SCAN_ROWS = 8
RET_BLOCK = 256
TILE = 512
MIXER_TILE = 1024
LANES = 128
ROUTE_ROWS = 32
ROW_CHUNK = 256
PROJ_TILES = 2
FINAL_TILES = 4
PIECE = 16
TILE_PIECES = TILE // PIECE
MOE_TILE = 1024
MOE_PIECES = MOE_TILE // PIECE
SORTED_PIECES = TILE_PIECES + N_GROUPS
SORTED_ROWS = SORTED_PIECES * PIECE
EXT_WIDTH = D_MODEL + LANES
INV_LANE = 2 * EXPERTS_PER_GROUP
V7X_VMEM_BYTES = 64 * 1024 * 1024
VMEM_LIMIT = V7X_VMEM_BYTES * 7 // 8

F32 = jnp.float32
BF16 = jnp.bfloat16
FP8 = jnp.float8_e4m3fn
FP8_HEADROOM = 256.0
I32 = jnp.int32


def _dot(a, b):
    return jnp.dot(a, b, preferred_element_type=F32)


def _sigmoid(x):
    return 1.0 / (1.0 + jnp.exp(-x))


def _rms_norm(x, g):
    return x * lax.rsqrt(jnp.mean(x * x, axis=-1, keepdims=True) + EPS) * g


def _rope(t, cos, sin_signed):
    return t * cos + pltpu.roll(t, HEAD_DIM // 2, axis=1) * sin_signed


def _meta_state(meta_ref, g_ref, w_ref, cos_ref, sin_ref, kdec_ref, blk_ref, u_ref, r0_ref):
    a = _rms_norm(meta_ref[...], g_ref[...]).astype(BF16)
    u_ref[...] = _dot(a, w_ref[:, :SSM_WIDTH])
    k_off = SSM_WIDTH + RET_WIDTH
    v_off = SSM_WIDTH + 2 * RET_WIDTH
    cos = cos_ref[...]
    sin = sin_ref[...]
    for p in range(HEAD_PAIRS):
        k = _dot(a, w_ref[:, k_off + p * PAIR_DIM:k_off + (p + 1) * PAIR_DIM])
        v = _dot(a, w_ref[:, v_off + p * PAIR_DIM:v_off + (p + 1) * PAIR_DIM])
        k = jnp.concatenate([_rope(k[:, :HEAD_DIM], cos, sin), _rope(k[:, HEAD_DIM:], cos, sin)], axis=1)
        kd = (k * kdec_ref[p]).astype(BF16)
        r0_ref[p] = _dot_rows(kd, v.astype(BF16)) * blk_ref[...]


def _dot_rows(a, b):
    return lax.dot_general(a, b, (((0,), (0,)), ((), ())), preferred_element_type=F32)


def _mixer_kernel(x_ref, g_ref, w32_ref, cos_ref, sin_ref, mask_ref, qdec_ref, kdec_ref, bdec_ref, blk_ref,
                  meta_ref, cos_m_ref, sin_m_ref, kdec_m_ref, u_ref, y_ref, um_ref, w_ref, r0_ref, r_ref):
    first_tile = pl.program_id(1) == 0

    @pl.when(jnp.logical_and(pl.program_id(0) == 0, first_tile))
    def _():
        for c in range(0, IN_WIDTH, SSM_WIDTH):
            w_ref[:, c:c + SSM_WIDTH] = w32_ref[:, c:c + SSM_WIDTH].astype(BF16)
        _meta_state(meta_ref, g_ref, w_ref, cos_m_ref, sin_m_ref, kdec_m_ref, blk_ref, um_ref, r0_ref)

    @pl.when(first_tile)
    def _():
        r_ref[...] = r0_ref[...]

    off = SSM_WIDTH
    for b in range(MIXER_TILE // RET_BLOCK):
        bs = slice(b * RET_BLOCK, (b + 1) * RET_BLOCK)
        a = _rms_norm(x_ref[bs, :], g_ref[...]).astype(BF16)
        u_ref[bs, :] = _dot(a, w_ref[:, :SSM_WIDTH])
        q_all = _dot(a, w_ref[:, off:off + RET_WIDTH])
        k_all = _dot(a, w_ref[:, off + RET_WIDTH:off + 2 * RET_WIDTH])
        v_all = _dot(a, w_ref[:, off + 2 * RET_WIDTH:off + 3 * RET_WIDTH])
        gate = _dot(a, w_ref[:, off + 3 * RET_WIDTH:off + 4 * RET_WIDTH])
        cos = cos_ref[bs, :]
        sin = sin_ref[bs, :]
        def scores(h):
            hs = slice(h * HEAD_DIM, (h + 1) * HEAD_DIM)
            q = _rope(q_all[:, hs], cos, sin)
            k = _rope(k_all[:, hs], cos, sin)
            return q, k, _dot_t(q.astype(BF16), k.astype(BF16))

        ahead = scores(0)
        for p in range(HEAD_PAIRS):
            ps = slice(p * PAIR_DIM, (p + 1) * PAIR_DIM)
            pair = [ahead, scores(2 * p + 1)]
            if p + 1 < HEAD_PAIRS:
                ahead = scores(2 * p + 2)
            q = jnp.concatenate([pair[0][0], pair[1][0]], axis=1)
            k = jnp.concatenate([pair[0][1], pair[1][1]], axis=1)
            v = v_all[:, ps].astype(BF16)
            state = r_ref[p]
            cross = _dot((q * qdec_ref[p]).astype(BF16), state.astype(BF16))
            kv = _dot_rows((k * kdec_ref[p]).astype(BF16), v)
            r_ref[p] = state * bdec_ref[p] + kv * blk_ref[...]
            for half, (_, _, s) in enumerate(pair):
                h = 2 * p + half
                hs = slice(h * HEAD_DIM, (h + 1) * HEAD_DIM)
                ls = slice(half * HEAD_DIM, (half + 1) * HEAD_DIM)
                o = _dot((s * mask_ref[h]).astype(BF16), v[:, ls]) + cross[:, ls]
                mu = jnp.mean(o, axis=-1, keepdims=True)
                d = o - mu
                var = jnp.mean(d * d, axis=-1, keepdims=True)
                gt = gate[:, hs]
                y_ref[bs, hs] = (gt * _sigmoid(gt) * d * lax.rsqrt(var + EPS)).astype(BF16)


def _mixer_call(x, g, w_in, cos, sin, mask, qdec, kdec, bdec, blk, meta, cos_m, sin_m, kdec_m):
    bsz, seq, _ = x.shape
    const = lambda a, **kw: pl.BlockSpec(a.shape, lambda b, i: (0,) * a.ndim, **kw)
    return pl.pallas_call(
        _mixer_kernel,
        grid=(bsz, seq // MIXER_TILE),
        in_specs=[
            pl.BlockSpec((None, MIXER_TILE, D_MODEL), lambda b, i: (b, i, 0)),
            const(g),
            const(w_in, pipeline_mode=pl.Buffered(1)),
            pl.BlockSpec((MIXER_TILE, HEAD_DIM), lambda b, i: (i, 0)),
            pl.BlockSpec((MIXER_TILE, HEAD_DIM), lambda b, i: (i, 0)),
            const(mask), const(qdec), const(kdec), const(bdec), const(blk),
            const(meta), const(cos_m), const(sin_m), const(kdec_m),
        ],
        out_specs=(
            pl.BlockSpec((None, MIXER_TILE, SSM_WIDTH), lambda b, i: (b, i, 0)),
            pl.BlockSpec((None, MIXER_TILE, RET_WIDTH), lambda b, i: (b, i, 0)),
            pl.BlockSpec((N_META, SSM_WIDTH), lambda b, i: (0, 0)),
        ),
        out_shape=(jax.ShapeDtypeStruct((bsz, seq, SSM_WIDTH), F32),
                   jax.ShapeDtypeStruct((bsz, seq, RET_WIDTH), BF16),
                   jax.ShapeDtypeStruct((N_META, SSM_WIDTH), F32)),
        scratch_shapes=[pltpu.VMEM((D_MODEL, IN_WIDTH), BF16),
                        pltpu.VMEM((HEAD_PAIRS, PAIR_DIM, PAIR_DIM), F32),
                        pltpu.VMEM((HEAD_PAIRS, PAIR_DIM, PAIR_DIM), F32)],
        compiler_params=pltpu.CompilerParams(
            dimension_semantics=("arbitrary", "arbitrary"), vmem_limit_bytes=VMEM_LIMIT),
        name="mixer_call",
    )(x, g, w_in, cos, sin, mask, qdec, kdec, bdec, blk, meta, cos_m, sin_m, kdec_m)


def _dot_t(a, b):
    return lax.dot_general(a, b, (((1,), (1,)), ((), ())), preferred_element_type=F32)


def _chunk_transpose(arrs):
    n = len(arrs)
    chunk = lax.broadcasted_iota(I32, (1, LANES), 1) // SSM_GROUP
    arrs = list(arrs)
    s = n // 2
    while s:
        keep = (chunk & s) == 0
        nxt = list(arrs)
        for i in range(n):
            if i & s == 0:
                lo, hi = arrs[i], arrs[i + s]
                nxt[i] = jnp.where(keep, lo, pltpu.roll(hi, s * SSM_GROUP, axis=1))
                nxt[i + s] = jnp.where(keep, pltpu.roll(lo, LANES - s * SSM_GROUP, axis=1), hi)
        arrs = nxt
        s //= 2
    return arrs


def _s5_kernel(u_lo_ref, u_hi_ref, um_ref, krow_ref, bmat_ref, cre_ref, cim_ref, ar_ref, ai_ref, wglu_ref,
               y_lo_ref, y_hi_ref, t0_ref, ug_ref, yg_ref):
    n_blocks = u_lo_ref.shape[0] // S5_BLOCK

    @pl.when(pl.program_id(0) == 0)
    def _():
        lane = lax.broadcasted_iota(I32, (SSM_GROUP, S5_LANES), 1)
        for g in range(SSM_GROUPS):
            k0 = krow_ref[g]
            for a in range(S5_BLOCK):
                blk = k0 if a == 0 else jnp.where(lane >= a * SSM_GROUP,
                                                  pltpu.roll(k0, a * SSM_GROUP, axis=1), 0.0)
                t0_ref[g, a * SSM_GROUP:(a + 1) * SSM_GROUP, :] = blk.astype(BF16)

    per_tile = LANES // SSM_GROUP
    for half, uh_ref in enumerate((u_lo_ref, u_hi_ref)):
        for t in range(S5_BLOCK // per_tile):
            words = [pltpu.bitcast(uh_ref[pl.ds(t * per_tile + k, n_blocks, stride=S5_BLOCK), :].astype(BF16),
                                   jnp.uint32) for k in range(per_tile)]
            for m, w in enumerate(_chunk_transpose(words)):
                ug_ref[half * per_tile + m, :, t * LANES:(t + 1) * LANES] = pltpu.bitcast(w, BF16)

    re, im, re0, im0 = [], [], [], []
    for p in range(SSM_GROUPS // 2):
        g0, g1 = 2 * p, 2 * p + 1
        v = _dot(ug_ref[g0], bmat_ref[g0]) + _dot(ug_ref[g1], bmat_ref[g1])
        v0 = (_dot(um_ref[g0].astype(BF16), bmat_ref[g0]) + _dot(um_ref[g1].astype(BF16), bmat_ref[g1]))[0:1]
        re.append(v[:, :LANES])
        im.append(v[:, LANES:])
        re0.append(v0[:, :LANES])
        im0.append(v0[:, LANES:])
    re, im, re0, im0 = (jnp.concatenate(parts, axis=1) for parts in (re, im, re0, im0))

    row = lax.broadcasted_iota(I32, re.shape, 0)
    ar, ai = ar_ref[0:1], ai_ref[0:1]
    re = re + jnp.where(row == 0, ar * re0 - ai * im0, 0.0)
    im = im + jnp.where(row == 0, ar * im0 + ai * re0, 0.0)
    d = 1
    while d < SCAN_ROWS:
        ar, ai = ar_ref[d - 1:d], ai_ref[d - 1:d]
        inside = row % SCAN_ROWS >= d
        sre = jnp.where(inside, pltpu.roll(re, d, axis=0), 0.0)
        sim = jnp.where(inside, pltpu.roll(im, d, axis=0), 0.0)
        re, im = re + ar * sre - ai * sim, im + ar * sim + ai * sre
        d *= 2
    ar, ai = ar_ref[...], ai_ref[...]
    re_tiles, im_tiles = [re[:SCAN_ROWS]], [im[:SCAN_ROWS]]
    for r in range(1, n_blocks // SCAN_ROWS):
        cre_ = re_tiles[-1][SCAN_ROWS - 1:SCAN_ROWS]
        cim_ = im_tiles[-1][SCAN_ROWS - 1:SCAN_ROWS]
        rs = slice(r * SCAN_ROWS, (r + 1) * SCAN_ROWS)
        re_tiles.append(re[rs] + ar * cre_ - ai * cim_)
        im_tiles.append(im[rs] + ar * cim_ + ai * cre_)
    re = jnp.concatenate(re_tiles, axis=0)
    im = jnp.concatenate(im_tiles, axis=0)
    pre = jnp.where(row == 0, re0, pltpu.roll(re, 1, axis=0)).astype(BF16)
    pim = jnp.where(row == 0, im0, pltpu.roll(im, 1, axis=0)).astype(BF16)

    for g in range(SSM_GROUPS):
        ps = slice((g // 2) * LANES, (g // 2 + 1) * LANES)
        yg_ref[g] = (_dot(ug_ref[g], t0_ref[g]) + _dot_t(pre[:, ps], cre_ref[g])
                     + _dot_t(pim[:, ps], cim_ref[g]))

    wglu = wglu_ref[...]
    for t in range(S5_BLOCK // per_tile):
        ts = slice(t * LANES, (t + 1) * LANES)
        halves = [_chunk_transpose([yg_ref[half * per_tile + m, :, ts] for m in range(per_tile)])
                  for half in range(SSM_GROUPS // per_tile)]
        for k in range(per_tile):
            y = jnp.concatenate([h[k] for h in halves], axis=1)
            y = jax.nn.gelu(y, approximate=True)
            y = y * _sigmoid(_dot(y.astype(BF16), wglu))
            i = t * per_tile + k
            y_lo_ref[pl.ds(i, n_blocks, stride=S5_BLOCK), :] = y[:, :LANES]
            y_hi_ref[pl.ds(i, n_blocks, stride=S5_BLOCK), :] = y[:, LANES:]


def _s5_call(u, um, krow, bmat, cre, cim, ar, ai, wglu):
    bsz, seq, _ = u.shape
    n_blocks = seq // S5_BLOCK
    const = lambda a: pl.BlockSpec(a.shape, lambda b: (0,) * a.ndim)
    return pl.pallas_call(
        _s5_kernel,
        grid=(bsz,),
        in_specs=[pl.BlockSpec((None, seq, LANES), lambda b: (b, 0, 0)),
                  pl.BlockSpec((None, seq, LANES), lambda b: (b, 0, 1)),
                  const(um), const(krow), const(bmat), const(cre), const(cim), const(ar), const(ai),
                  const(wglu)],
        out_specs=(pl.BlockSpec((None, seq, LANES), lambda b: (b, 0, 0)),
                   pl.BlockSpec((None, seq, LANES), lambda b: (b, 0, 0))),
        out_shape=(jax.ShapeDtypeStruct((bsz, seq, LANES), F32),
                   jax.ShapeDtypeStruct((bsz, seq, LANES), F32)),
        scratch_shapes=[pltpu.VMEM((SSM_GROUPS, S5_LANES, S5_LANES), BF16),
                        pltpu.VMEM((SSM_GROUPS, n_blocks, S5_LANES), BF16),
                        pltpu.VMEM((SSM_GROUPS, n_blocks, S5_LANES), F32)],
        compiler_params=pltpu.CompilerParams(
            dimension_semantics=("arbitrary",), vmem_limit_bytes=VMEM_LIMIT),
        name="s5_call",
    )(u, u, um, krow, bmat, cre, cim, ar, ai, wglu)


def _s5_operators(lam_re, lam_im, log_dt, b_re, b_im, c_re, c_im, d_skip):
    n_groups = lam_re.shape[0]
    lam = lax.complex(lam_re, lam_im)
    lam_dt = lam * jnp.exp(log_dt)[:, None]
    lam_bar = jnp.exp(lam_dt)
    b_bar = ((lam_bar - 1.0) / lam)[..., None] * lax.complex(b_re, b_im)
    c = lax.complex(c_re, c_im)
    tau = jnp.arange(S5_BLOCK + 1, dtype=F32)
    pows = jnp.exp(lam_dt[None] * tau[:, None, None])
    kern = jnp.real(jnp.einsum('ghp,tgp,gpk->gkth', c, pows[:S5_BLOCK], b_bar))
    skip = (jnp.eye(SSM_GROUP, dtype=F32)[None, :, None, :] * d_skip[:, None, None, :]
            * (tau[:S5_BLOCK] == 0).astype(F32)[None, None, :, None])
    krow = (kern + skip).reshape(n_groups, SSM_GROUP, S5_LANES)
    even = (jnp.arange(n_groups) % 2 == 0)[:, None, None]
    pair_pad = lambda m: jnp.concatenate([jnp.where(even, m, 0.0), jnp.where(even, 0.0, m)], axis=-1)
    bm = (pows[S5_BLOCK - 1 - jnp.arange(S5_BLOCK)].transpose(1, 0, 2)[:, :, None, :]
          * b_bar.transpose(0, 2, 1)[:, None, :, :]).reshape(n_groups, S5_LANES, SSM_STATE)
    bmat = jnp.concatenate([pair_pad(jnp.real(bm)), pair_pad(jnp.imag(bm))], axis=-1)
    cm = (pows[1:].transpose(1, 0, 2)[:, :, None, :] * c[:, None, :, :]).reshape(n_groups, S5_LANES, SSM_STATE)
    cre, cim = pair_pad(jnp.real(cm)), pair_pad(-jnp.imag(cm))
    step = S5_BLOCK * (1.0 + jnp.arange(SCAN_ROWS, dtype=F32))
    adec = jnp.exp(lam_dt[None, :, :] * step[:, None, None]).reshape(SCAN_ROWS, n_groups * SSM_STATE)
    return krow, bmat.astype(BF16), cre.astype(BF16), cim.astype(BF16), jnp.real(adec), jnp.imag(adec)


def _first_hit(values, target):
    hits, taken = [], None
    for v in values:
        hit = (v >= target) if taken is None else jnp.logical_and(v >= target, jnp.logical_not(taken))
        taken = hit if taken is None else jnp.logical_or(taken, hit)
        hits.append(hit)
    return hits


def _stack_rows(rows, n_rows):
    idx = lax.broadcasted_iota(I32, (n_rows, rows[0].shape[1]), 0)
    out = jnp.zeros((n_rows, rows[0].shape[1]), F32)
    for k, r in enumerate(rows):
        out = jnp.where(idx == k, r, out)
    return out


def _project(rs, x_ref, ys_lo_ref, ys_hi_ref, yr_ref, wout_ref, h_ref):
    ys = jnp.concatenate([ys_lo_ref[rs, :], ys_hi_ref[rs, :]], axis=1).astype(BF16)
    h = x_ref[rs, :] + _dot(ys, wout_ref[:SSM_WIDTH, :]) + _dot(yr_ref[rs, :], wout_ref[SSM_WIDTH:, :])
    h_ref[rs, :] = h.astype(BF16)
    return h


def _route(lt, tri_ref):
    gl = [lt[g:g + 1, :] for g in range(N_GROUPS)]
    gmax = functools.reduce(jnp.maximum, gl)
    g_w = 1.0 / functools.reduce(lambda a, b: a + b, [jnp.exp(l - gmax) for l in gl])
    sel = _first_hit(gl, gmax)
    ev = []
    for e in range(EXPERTS_PER_GROUP):
        acc = jnp.zeros_like(gmax)
        for g in range(N_GROUPS):
            k = N_GROUPS + g * EXPERTS_PER_GROUP + e
            acc = jnp.where(sel[g], lt[k:k + 1, :], acc)
        ev.append(acc)
    m1 = functools.reduce(jnp.maximum, ev)
    first = _first_hit(ev, m1)
    rest = [jnp.where(f, -jnp.inf, v) for f, v in zip(first, ev)]
    m2 = functools.reduce(jnp.maximum, rest)
    second = _first_hit(rest, m2)
    e2 = jnp.exp(m2 - m1)
    w1 = g_w / (1.0 + e2)
    w2 = e2 * w1
    combine = [jnp.where(f, w1, 0.0) + jnp.where(s, w2, 0.0) for f, s in zip(first, second)]

    sel_f = [jnp.where(s, 1.0, 0.0) for s in sel]
    incl = _dot(_stack_rows(sel_f, 8).astype(BF16), tri_ref[...])
    dest = jnp.zeros_like(gmax)
    seg_start = jnp.zeros((1, 1), F32)
    counts = []
    for g in range(N_GROUPS):
        run = incl[g:g + 1, :]
        cnt = run[:, TILE - 1:TILE]
        counts.append(cnt)
        dest = dest + sel_f[g] * (seg_start + run - 1.0)
        seg_start = seg_start + PIECE * jnp.floor((cnt + (PIECE - 1.0)) * (1.0 / PIECE))
    return combine, dest, counts


def _proj_kernel(x_ref, ys_lo_ref, ys_hi_ref, yr_ref, wout32_ref, g_ref, wr_ref, br_ref,
                 tri_ref, h_ref, stage_ref, dest_ref, cnt_ref, wout_ref):
    @pl.when(pl.program_id(0) == 0)
    def _():
        wout_ref[...] = wout32_ref[...].astype(BF16)

    tiles = range(PROJ_TILES)
    proj_refs = (x_ref, ys_lo_ref, ys_hi_ref, yr_ref, wout_ref, h_ref)
    chunks = [[slice(tile * TILE + b * ROW_CHUNK, tile * TILE + (b + 1) * ROW_CHUNK)
               for b in range(TILE // ROW_CHUNK)] for tile in tiles]
    h_parts = [[_project(rs, *proj_refs) for rs in chunks[tile]] for tile in tiles]

    t_parts, logits, staged = [], [], []
    for tile in tiles:
        parts, rows, inv = [], [], []
        for h in h_parts[tile]:
            t = _rms_norm(h, g_ref[...])
            t_hi = t.astype(BF16)
            parts.append((t_hi, (t - t_hi.astype(F32)).astype(BF16)))
            peak = jnp.maximum(jnp.max(jnp.abs(t), axis=1, keepdims=True), 1e-30)
            rows.append((t * (FP8_HEADROOM / peak)).astype(BF16))
            inv.append(peak * (1.0 / FP8_HEADROOM))
        staged.append((jnp.concatenate(rows, axis=0), jnp.concatenate(inv, axis=0)))
        lt_parts = []
        for t_hi, t_lo in parts:
            both = _dot_t(wr_ref[...], t_hi)
            lt_parts.append(both[:ROUTE_ROWS] + both[ROUTE_ROWS:] + _dot_t(wr_ref[:ROUTE_ROWS, :], t_lo))
        t_parts.append(parts)
        logits.append(jnp.concatenate(lt_parts, axis=1) + br_ref[...])

    routed = [_route(logits[tile], tri_ref) for tile in tiles]

    for tile in tiles:
        combine, dest, counts = routed[tile]
        dest_ref[tile] = _stack_rows([dest], 8)
        cnt_ref[tile] = _stack_rows([c + jnp.zeros((1, LANES), F32) for c in counts], 8)
        perm = jnp.where(lax.broadcasted_iota(I32, (SORTED_ROWS, TILE), 0) == dest.astype(I32),
                         1.0, 0.0).astype(BF16)
        c_hi = [c.astype(BF16).astype(F32) for c in combine]
        c_lo = [c - hi for c, hi in zip(combine, c_hi)]
        cw = _stack_rows(c_hi + c_lo, LANES).T
        rows, inv = staged[tile]
        inv_hi = inv.astype(BF16).astype(F32)
        lane = lax.broadcasted_iota(I32, (1, LANES), 1)
        cw = jnp.where(lane == INV_LANE, inv_hi, jnp.where(lane == INV_LANE + 1, inv - inv_hi, cw))
        t_ext = jnp.concatenate([rows, cw.astype(BF16)], axis=1)
        stage_ref[tile * SORTED_ROWS:(tile + 1) * SORTED_ROWS, :] = _dot(perm, t_ext).astype(BF16)


def _proj_call(x, ys_lo, ys_hi, yr, wout, g, wr, br, tri):
    n_tok = x.shape[0]
    n_tiles = n_tok // TILE
    const = lambda *shape: pl.BlockSpec(shape, lambda i: (0,) * len(shape))
    rows = lambda width: pl.BlockSpec((PROJ_TILES * TILE, width), lambda i: (i, 0))
    return pl.pallas_call(
        _proj_kernel,
        grid=(n_tiles // PROJ_TILES,),
        in_specs=[
            rows(D_MODEL), rows(LANES), rows(LANES), rows(RET_WIDTH),
            pl.BlockSpec((D_MODEL, D_MODEL), lambda i: (0, 0), pipeline_mode=pl.Buffered(1)), const(1, D_MODEL),
            const(2 * ROUTE_ROWS, D_MODEL), const(ROUTE_ROWS, 1), const(TILE, TILE),
        ],
        out_specs=(rows(D_MODEL),
                   pl.BlockSpec((PROJ_TILES * SORTED_ROWS, EXT_WIDTH), lambda i: (i, 0)),
                   pl.BlockSpec((PROJ_TILES, 8, TILE), lambda i: (i, 0, 0)),
                   pl.BlockSpec((PROJ_TILES, 8, LANES), lambda i: (i, 0, 0))),
        scratch_shapes=[pltpu.VMEM((D_MODEL, D_MODEL), BF16)],
        out_shape=(jax.ShapeDtypeStruct((n_tok, D_MODEL), BF16),
                   jax.ShapeDtypeStruct((n_tiles * SORTED_ROWS, EXT_WIDTH), BF16),
                   jax.ShapeDtypeStruct((n_tiles, 8, TILE), F32),
                   jax.ShapeDtypeStruct((n_tiles, 8, LANES), F32)),
        compiler_params=pltpu.CompilerParams(
            dimension_semantics=("arbitrary",), vmem_limit_bytes=VMEM_LIMIT),
        name="proj_call",
    )(x, ys_lo, ys_hi, yr, wout, g, wr, br, tri)


def _sort_tables(cnt, n_steps):
    n_tiles = cnt.shape[0]
    npc = (cnt + PIECE - 1) // PIECE
    seg = jnp.cumsum(npc, axis=1) - npc
    before = jnp.cumsum(npc, axis=0) - npc
    n_tile_g = (jnp.sum(npc, axis=0) + MOE_PIECES - 1) // MOE_PIECES
    t_off = jnp.cumsum(n_tile_g) - n_tile_g
    j = jnp.arange(SORTED_PIECES, dtype=I32)[None, :, None]
    in_g = jnp.logical_and(j >= seg[:, None, :], j < (seg + npc)[:, None, :])
    pos = jnp.sum(jnp.where(in_g, MOE_PIECES * t_off[None, None, :] + before[:, None, :] + j - seg[:, None, :], 0),
                  axis=-1)
    valid = jnp.any(in_g, axis=-1)
    stage_piece = jnp.arange(n_tiles * SORTED_PIECES, dtype=I32).reshape(n_tiles, SORTED_PIECES)
    n_slots = n_steps * MOE_PIECES
    src = jnp.zeros((n_slots,), I32).at[jnp.where(valid, pos, n_slots).reshape(-1)].set(
        stage_piece.reshape(-1), mode='drop')
    steps = jnp.arange(n_steps, dtype=I32)
    g_step = jnp.minimum(jnp.sum(steps[:, None] >= (t_off + n_tile_g)[None, :], axis=1), N_GROUPS - 1)
    n_live = jnp.sum(n_tile_g).reshape(1)
    back = jnp.where(valid, pos, 0).reshape(-1)
    return src, g_step.astype(I32), n_live.astype(I32), back.astype(I32)


def _piece_copy(src_ref, piece, buf_ref, slot, j, sem_ref):
    start = piece * PIECE if isinstance(piece, int) else pl.multiple_of(piece * PIECE, PIECE)
    return pltpu.make_async_copy(src_ref.at[pl.ds(start, PIECE)],
                                 buf_ref.at[slot, pl.ds(j * PIECE, PIECE)], sem_ref.at[slot])


def _fetch_pieces(table_ref, src_ref, buf_ref, sem_ref, row, slot, n_pieces):
    for j in range(n_pieces):
        _piece_copy(src_ref, table_ref[row * n_pieces + j], buf_ref, slot, j, sem_ref).start()


def _wait_pieces(src_ref, buf_ref, sem_ref, slot, n_pieces):
    for j in range(n_pieces):
        _piece_copy(src_ref, 0, buf_ref, slot, j, sem_ref).wait()


def _gather_pieces(table_ref, src_ref, buf_ref, sem_ref, n_pieces):
    step = pl.program_id(0)
    last = pl.num_programs(0) - 1
    args = (src_ref, buf_ref, sem_ref)

    @pl.when(step == 0)
    def _():
        _fetch_pieces(table_ref, *args, step, 0, n_pieces)

    _wait_pieces(*args, step % 2, n_pieces)

    def fetch_next():
        _fetch_pieces(table_ref, *args, jnp.minimum(step + 1, last), (step + 1) % 2, n_pieces)

    def drain():
        @pl.when(step == last)
        def _():
            _wait_pieces(*args, (step + 1) % 2, n_pieces)

    return fetch_next, drain


def _to_fp8(x, headroom=FP8_HEADROOM):
    peak = jnp.maximum(jnp.max(jnp.abs(x), axis=(0, 1), keepdims=True), 1e-30)
    return (x * (headroom / peak)).astype(FP8), peak * (1.0 / headroom)


def _moe_kernel(src_ref, gstep_ref, nlive_ref, stage_ref, wg_ref, wu_ref, wd_ref, y_ref,
                buf_ref, sem_ref, wgb_ref, wub_ref, wdb_ref, inv_ref):
    step = pl.program_id(0)
    fetch_next, drain = _gather_pieces(src_ref, stage_ref, buf_ref, sem_ref, MOE_PIECES)

    @pl.when(jnp.logical_or(step == 0, gstep_ref[step] != gstep_ref[jnp.maximum(step - 1, 0)]))
    def _():
        ones = jnp.ones((1, LANES), F32)
        for e in range(EXPERTS_PER_GROUP):
            wgb_ref[e], inv = _to_fp8(wg_ref[e])
            inv_ref[e:e + 1, :] = inv * ones
            wub_ref[e], inv = _to_fp8(wu_ref[e])
            inv_ref[EXPERTS_PER_GROUP + e:EXPERTS_PER_GROUP + e + 1, :] = inv * ones
        peak = functools.reduce(jnp.maximum, [jnp.max(jnp.abs(wd_ref[e]), axis=(0, 1), keepdims=True)
                                              for e in range(EXPERTS_PER_GROUP)])
        peak = jnp.maximum(peak, 1e-30)
        for e in range(EXPERTS_PER_GROUP):
            wdb_ref[e * EXPERT_FF:(e + 1) * EXPERT_FF, :] = (wd_ref[e] * (FP8_HEADROOM / peak)).astype(FP8)
        inv_ref[2 * EXPERTS_PER_GROUP:2 * EXPERTS_PER_GROUP + 1, :] = peak * (1.0 / FP8_HEADROOM) * ones

    @pl.when(step < nlive_ref[0])
    def _():
        slot = step % 2
        fetch_next()

        def up(rs):
            t = buf_ref[slot, rs, :D_MODEL].astype(FP8)
            t_inv = (buf_ref[slot, rs, D_MODEL + INV_LANE:D_MODEL + INV_LANE + 1].astype(F32)
                     + buf_ref[slot, rs, D_MODEL + INV_LANE + 1:D_MODEL + INV_LANE + 2].astype(F32))
            out = []
            for e in range(EXPERTS_PER_GROUP):
                g_inv = inv_ref[e:e + 1, 0:1] * t_inv
                u_inv = inv_ref[EXPERTS_PER_GROUP + e:EXPERTS_PER_GROUP + e + 1, 0:1] * t_inv
                out.append((_dot(t, wgb_ref[e]) * g_inv, _dot(t, wub_ref[e]), u_inv))
            return out

        def down(rs, hidden):
            cw = buf_ref[slot, rs, D_MODEL:].astype(F32)
            acts = []
            for e, (hg, hu, u_inv) in enumerate(hidden):
                c = (cw[:, e:e + 1] + cw[:, EXPERTS_PER_GROUP + e:EXPERTS_PER_GROUP + e + 1]) * u_inv
                acts.append(hg * _sigmoid(hg) * hu * c)
            act, a_inv = _to_fp8(jnp.concatenate(acts, axis=1))
            d_inv = inv_ref[2 * EXPERTS_PER_GROUP:2 * EXPERTS_PER_GROUP + 1, 0:1] * a_inv
            y_ref[rs, :] = (_dot(act, wdb_ref[...]) * d_inv).astype(BF16)

        chunks = [slice(b * ROW_CHUNK, (b + 1) * ROW_CHUNK) for b in range(MOE_TILE // ROW_CHUNK)]
        hidden = [up(rs) for rs in chunks]
        for rs, hid in zip(chunks, hidden):
            down(rs, hid)

    @pl.when(step >= nlive_ref[0])
    def _():
        fetch_next()
        y_ref[...] = jnp.zeros_like(y_ref)

    drain()


def _moe_call(src, g_step, n_live, stage, wg, wu, wd, n_steps):
    grp = lambda shape: pl.BlockSpec((EXPERTS_PER_GROUP,) + shape, lambda s, src, gs, nl: (gs[s], 0, 0))
    return pl.pallas_call(
        _moe_kernel,
        grid_spec=pltpu.PrefetchScalarGridSpec(
            num_scalar_prefetch=3,
            grid=(n_steps,),
            in_specs=[pl.BlockSpec(memory_space=pl.ANY),
                      grp((D_MODEL, EXPERT_FF)), grp((D_MODEL, EXPERT_FF)), grp((EXPERT_FF, D_MODEL))],
            out_specs=pl.BlockSpec((MOE_TILE, D_MODEL), lambda s, src, gs, nl: (s, 0)),
            scratch_shapes=[pltpu.VMEM((2, MOE_TILE, EXT_WIDTH), BF16), pltpu.SemaphoreType.DMA((2,)),
                            pltpu.VMEM((EXPERTS_PER_GROUP, D_MODEL, EXPERT_FF), FP8),
                            pltpu.VMEM((EXPERTS_PER_GROUP, D_MODEL, EXPERT_FF), FP8),
                            pltpu.VMEM((EXPERTS_PER_GROUP * EXPERT_FF, D_MODEL), FP8),
                            pltpu.VMEM((16, LANES), F32)],
        ),
        out_shape=jax.ShapeDtypeStruct((n_steps * MOE_TILE, D_MODEL), BF16),
        compiler_params=pltpu.CompilerParams(
            dimension_semantics=("arbitrary",), vmem_limit_bytes=VMEM_LIMIT),
        name="moe_call",
    )(src, g_step, n_live, stage, wg, wu, wd)


def _final_kernel(back_ref, ysort_ref, h_ref, dest_ref, g_ref, o_ref, buf_ref, sem_ref):
    step = pl.program_id(0)
    fetch_next, drain = _gather_pieces(back_ref, ysort_ref, buf_ref, sem_ref, FINAL_TILES * SORTED_PIECES)
    fetch_next()
    for t in range(FINAL_TILES):
        dest = _stack_rows([dest_ref[t, 0:1, :]], LANES).T[:, 0:1].astype(I32)
        unperm = jnp.where(lax.broadcasted_iota(I32, (TILE, SORTED_ROWS), 1) == dest, 1.0, 0.0).astype(BF16)
        rows = buf_ref[step % 2, t * SORTED_ROWS:(t + 1) * SORTED_ROWS, :]
        for b in range(TILE // ROW_CHUNK):
            rs = slice(b * ROW_CHUNK, (b + 1) * ROW_CHUNK)
            os = slice(t * TILE + b * ROW_CHUNK, t * TILE + (b + 1) * ROW_CHUNK)
            o_ref[os, :] = _rms_norm(h_ref[os, :].astype(F32) + _dot(unperm[rs], rows), g_ref[...])
    drain()


def _final_call(back, ysort, h, dest, g):
    n_tok = h.shape[0]
    return pl.pallas_call(
        _final_kernel,
        grid_spec=pltpu.PrefetchScalarGridSpec(
            num_scalar_prefetch=1,
            grid=(n_tok // (FINAL_TILES * TILE),),
            in_specs=[pl.BlockSpec(memory_space=pl.ANY),
                      pl.BlockSpec((FINAL_TILES * TILE, D_MODEL), lambda i, back: (i, 0)),
                      pl.BlockSpec((FINAL_TILES, 8, TILE), lambda i, back: (i, 0, 0)),
                      pl.BlockSpec((1, D_MODEL), lambda i, back: (0, 0))],
            out_specs=pl.BlockSpec((FINAL_TILES * TILE, D_MODEL), lambda i, back: (i, 0)),
            scratch_shapes=[pltpu.VMEM((2, FINAL_TILES * SORTED_ROWS, D_MODEL), BF16),
                            pltpu.SemaphoreType.DMA((2,))],
        ),
        out_shape=jax.ShapeDtypeStruct((n_tok, D_MODEL), F32),
        compiler_params=pltpu.CompilerParams(
            dimension_semantics=("arbitrary",), vmem_limit_bytes=VMEM_LIMIT),
        name="final_call",
    )(back, ysort, h, dest, g)


def _rope_tables(length):
    pos = np.arange(length, dtype=np.float32)
    inv_freq = np.float32(ROPE_BASE) ** (-np.arange(0, HEAD_DIM, 2, dtype=np.float32) / np.float32(HEAD_DIM))
    ang = pos[:, None] * inv_freq[None, :]
    cos, sin = np.cos(ang), np.sin(ang)
    return np.concatenate([cos, cos], axis=-1), np.concatenate([-sin, sin], axis=-1)


def _retention_tables():
    f32 = np.float32
    gamma = f32(1.0) - f32(2.0) ** (f32(-5.0) - np.arange(HEADS, dtype=f32))
    log_g = np.log(gamma)[:, None, None]
    scale = f32(HEAD_DIM ** -0.5)
    idx = np.arange(RET_BLOCK)
    dist = np.abs(idx[:, None] - idx[None, :]).astype(f32)
    visible = (idx[None, :] // CHUNK) <= (idx[:, None] // CHUNK)
    mask = np.where(visible[None], np.exp(log_g * dist[None]), f32(0.0)) * scale
    ones = np.ones((1, 1, HEAD_DIM), f32)
    idx_f = idx.astype(f32)[None, :, None]
    pair = lambda a: a.reshape(HEAD_PAIRS, 2, a.shape[1], HEAD_DIM).transpose(0, 2, 1, 3).reshape(
        HEAD_PAIRS, a.shape[1], PAIR_DIM)
    qdec = pair(np.exp(log_g * (idx_f + f32(1.0))) * ones)
    kdec = pair(np.exp(log_g * (f32(RET_BLOCK - 1.0) - idx_f)) * scale * ones)
    meta_idx = np.arange(N_META, dtype=f32)[None, :, None]
    kdec_meta = pair(np.exp(log_g * (f32(N_META - 1.0) - meta_idx)) * scale * ones)
    blk = np.kron(np.eye(2, dtype=f32), np.ones((HEAD_DIM, HEAD_DIM), f32))
    bdec = np.exp(log_g * f32(RET_BLOCK)).reshape(HEAD_PAIRS, 2)
    bdec = np.stack([np.kron(np.diag(b), np.ones((HEAD_DIM, HEAD_DIM), f32)) for b in bdec])
    return tuple(a.astype(f32) for a in (mask, qdec, kdec, bdec, blk, kdec_meta))


def kernel(x, meta_tokens, norm_mix_g, w_in, ssm_lambda_re, ssm_lambda_im, ssm_log_dt, ssm_b_re, ssm_b_im, ssm_c_re, ssm_c_im, ssm_d, w_glu, w_out, norm_ffn_g, w_router_group, b_router_group, w_router_expert, b_router_expert, w_gate, w_up, w_down, norm_final_g):
    bsz, seq, _ = x.shape
    assert seq % TILE == 0 and seq % MIXER_TILE == 0 and MIXER_TILE % RET_BLOCK == 0 and RET_BLOCK % CHUNK == 0
    n_blocks = seq // S5_BLOCK
    assert n_blocks % SCAN_ROWS == 0
    n_tok = bsz * seq
    assert n_tok % (FINAL_TILES * TILE) == 0 and n_tok % (PROJ_TILES * TILE) == 0
    n_tiles = n_tok // TILE
    n_steps = -(-n_tiles * (TILE_PIECES + N_GROUPS - 1) // MOE_PIECES) + N_GROUPS

    cos, sin = _rope_tables(N_META + seq)
    mask, qdec, kdec, bdec, blk, kdec_meta = _retention_tables()
    g_mix = norm_mix_g[0][None, :]
    u, y_ret, u_meta = _mixer_call(x, g_mix, w_in[0], cos[N_META:], sin[N_META:], mask, qdec, kdec, bdec, blk,
                                   meta_tokens, cos[:N_META], sin[:N_META], kdec_meta)

    s5_ops = _s5_operators(
        ssm_lambda_re[0], ssm_lambda_im[0], ssm_log_dt[0], ssm_b_re[0], ssm_b_im[0],
        ssm_c_re[0], ssm_c_im[0], ssm_d[0])
    um = u_meta.reshape(S5_BLOCK, SSM_GROUPS, SSM_GROUP).transpose(1, 0, 2).reshape(SSM_GROUPS, 1, S5_LANES)
    um = jnp.pad(um, ((0, 0), (0, 7), (0, 0)))
    y_lo, y_hi = _s5_call(u, um, *s5_ops, w_glu[0].astype(BF16))

    w_r = jnp.concatenate(
        [w_router_group[0].T, w_router_expert[0].transpose(0, 2, 1).reshape(N_EXPERTS, D_MODEL)], axis=0)
    w_r = jnp.pad(w_r, ((0, ROUTE_ROWS - w_r.shape[0]), (0, 0)))
    b_r = jnp.concatenate([b_router_group[0], b_router_expert[0].reshape(-1)])
    b_r = jnp.pad(b_r, (0, ROUTE_ROWS - b_r.shape[0]))[:, None]
    w_r_hi = w_r.astype(BF16)
    w_r = jnp.concatenate([w_r_hi, (w_r - w_r_hi.astype(F32)).astype(BF16)], axis=0)
    tri = jnp.asarray(np.arange(TILE)[:, None] <= np.arange(TILE)[None, :], BF16)

    h, stage, dest, cnt = _proj_call(
        x.reshape(n_tok, D_MODEL), y_lo.reshape(n_tok, LANES), y_hi.reshape(n_tok, LANES),
        y_ret.reshape(n_tok, RET_WIDTH),
        w_out[0], norm_ffn_g[0][None, :], w_r, b_r, tri)
    src, g_step, n_live, back = _sort_tables(cnt[:, :N_GROUPS, 0].astype(I32), n_steps)
    y_sorted = _moe_call(src, g_step, n_live, stage, w_gate[0], w_up[0], w_down[0], n_steps)
    out = _final_call(back, y_sorted, h, dest, norm_final_g[None, :])
    return out.reshape(bsz, seq, D_MODEL)
```

```python
import functools

import jax
import jax.numpy as jnp
import numpy as np
from jax import lax
from jax.experimental import pallas as pl
from jax.experimental.pallas import tpu as pltpu

D_MODEL = 1024
N_META = 16
CHUNK = 64
EPS = 1e-6
SSM_WIDTH = 256
SSM_GROUP = 16
SSM_GROUPS = 16
SSM_STATE = 64
RET_WIDTH = 768
HEAD_DIM = 128
HEADS = 6
HEAD_PAIRS = HEADS // 2
PAIR_DIM = 2 * HEAD_DIM
ROPE_BASE = 10000.0
IN_WIDTH = SSM_WIDTH + 4 * RET_WIDTH
N_GROUPS = 4
EXPERTS_PER_GROUP = 4
N_EXPERTS = 16
EXPERT_FF = 256

S5_BLOCK = 16
S5_LANES = S5_BLOCK * SSM_GROUP
SCAN_ROWS = 8
RET_BLOCK = 256
TILE = 512
MIXER_TILE = 1024
LANES = 128
ROUTE_ROWS = 32
ROW_CHUNK = 256
PROJ_TILES = 2
FINAL_TILES = 4
PIECE = 16
TILE_PIECES = TILE // PIECE
MOE_TILE = 1024
MOE_PIECES = MOE_TILE // PIECE
SORTED_PIECES = TILE_PIECES + N_GROUPS
SORTED_ROWS = SORTED_PIECES * PIECE
EXT_WIDTH = D_MODEL + LANES
INV_LANE = 2 * EXPERTS_PER_GROUP
V7X_VMEM_BYTES = 64 * 1024 * 1024
VMEM_LIMIT = V7X_VMEM_BYTES * 7 // 8

F32 = jnp.float32
BF16 = jnp.bfloat16
FP8 = jnp.float8_e4m3fn
FP8_HEADROOM = 256.0
I32 = jnp.int32


def _dot(a, b):
    return jnp.dot(a, b, preferred_element_type=F32)


def _sigmoid(x):
    return 1.0 / (1.0 + jnp.exp(-x))


def _rms_norm(x, g):
    return x * lax.rsqrt(jnp.mean(x * x, axis=-1, keepdims=True) + EPS) * g


def _rope(t, cos, sin_signed):
    return t * cos + pltpu.roll(t, HEAD_DIM // 2, axis=1) * sin_signed


def _meta_state(meta_ref, g_ref, w_ref, cos_ref, sin_ref, kdec_ref, blk_ref, u_ref, r0_ref):
    a = _rms_norm(meta_ref[...], g_ref[...]).astype(BF16)
    u_ref[...] = _dot(a, w_ref[:, :SSM_WIDTH])
    k_off = SSM_WIDTH + RET_WIDTH
    v_off = SSM_WIDTH + 2 * RET_WIDTH
    cos = cos_ref[...]
    sin = sin_ref[...]
    for p in range(HEAD_PAIRS):
        k = _dot(a, w_ref[:, k_off + p * PAIR_DIM:k_off + (p + 1) * PAIR_DIM])
        v = _dot(a, w_ref[:, v_off + p * PAIR_DIM:v_off + (p + 1) * PAIR_DIM])
        k = jnp.concatenate([_rope(k[:, :HEAD_DIM], cos, sin), _rope(k[:, HEAD_DIM:], cos, sin)], axis=1)
        kd = (k * kdec_ref[p]).astype(BF16)
        r0_ref[p] = _dot_rows(kd, v.astype(BF16)) * blk_ref[...]


def _dot_rows(a, b):
    return lax.dot_general(a, b, (((0,), (0,)), ((), ())), preferred_element_type=F32)


def _mixer_kernel(x_ref, g_ref, w32_ref, cos_ref, sin_ref, mask_ref, qdec_ref, kdec_ref, bdec_ref, blk_ref,
                  meta_ref, cos_m_ref, sin_m_ref, kdec_m_ref, u_ref, y_ref, um_ref, w_ref, r0_ref, r_ref):
    first_tile = pl.program_id(1) == 0

    @pl.when(jnp.logical_and(pl.program_id(0) == 0, first_tile))
    def _():
        for c in range(0, IN_WIDTH, SSM_WIDTH):
            w_ref[:, c:c + SSM_WIDTH] = w32_ref[:, c:c + SSM_WIDTH].astype(BF16)
        _meta_state(meta_ref, g_ref, w_ref, cos_m_ref, sin_m_ref, kdec_m_ref, blk_ref, um_ref, r0_ref)

    @pl.when(first_tile)
    def _():
        r_ref[...] = r0_ref[...]

    off = SSM_WIDTH
    for b in range(MIXER_TILE // RET_BLOCK):
        bs = slice(b * RET_BLOCK, (b + 1) * RET_BLOCK)
        a = _rms_norm(x_ref[bs, :], g_ref[...]).astype(BF16)
        u_ref[bs, :] = _dot(a, w_ref[:, :SSM_WIDTH])
        q_all = _dot(a, w_ref[:, off:off + RET_WIDTH])
        k_all = _dot(a, w_ref[:, off + RET_WIDTH:off + 2 * RET_WIDTH])
        v_all = _dot(a, w_ref[:, off + 2 * RET_WIDTH:off + 3 * RET_WIDTH])
        gate = _dot(a, w_ref[:, off + 3 * RET_WIDTH:off + 4 * RET_WIDTH])
        cos = cos_ref[bs, :]
        sin = sin_ref[bs, :]
        def scores(h):
            hs = slice(h * HEAD_DIM, (h + 1) * HEAD_DIM)
            q = _rope(q_all[:, hs], cos, sin)
            k = _rope(k_all[:, hs], cos, sin)
            return q, k, _dot_t(q.astype(BF16), k.astype(BF16))

        ahead = scores(0)
        for p in range(HEAD_PAIRS):
            ps = slice(p * PAIR_DIM, (p + 1) * PAIR_DIM)
            pair = [ahead, scores(2 * p + 1)]
            if p + 1 < HEAD_PAIRS:
                ahead = scores(2 * p + 2)
            q = jnp.concatenate([pair[0][0], pair[1][0]], axis=1)
            k = jnp.concatenate([pair[0][1], pair[1][1]], axis=1)
            v = v_all[:, ps].astype(BF16)
            state = r_ref[p]
            cross = _dot((q * qdec_ref[p]).astype(BF16), state.astype(BF16))
            kv = _dot_rows((k * kdec_ref[p]).astype(BF16), v)
            r_ref[p] = state * bdec_ref[p] + kv * blk_ref[...]
            for half, (_, _, s) in enumerate(pair):
                h = 2 * p + half
                hs = slice(h * HEAD_DIM, (h + 1) * HEAD_DIM)
                ls = slice(half * HEAD_DIM, (half + 1) * HEAD_DIM)
                o = _dot((s * mask_ref[h]).astype(BF16), v[:, ls]) + cross[:, ls]
                mu = jnp.mean(o, axis=-1, keepdims=True)
                d = o - mu
                var = jnp.mean(d * d, axis=-1, keepdims=True)
                gt = gate[:, hs]
                y_ref[bs, hs] = (gt * _sigmoid(gt) * d * lax.rsqrt(var + EPS)).astype(BF16)


def _mixer_call(x, g, w_in, cos, sin, mask, qdec, kdec, bdec, blk, meta, cos_m, sin_m, kdec_m):
    bsz, seq, _ = x.shape
    const = lambda a, **kw: pl.BlockSpec(a.shape, lambda b, i: (0,) * a.ndim, **kw)
    return pl.pallas_call(
        _mixer_kernel,
        grid=(bsz, seq // MIXER_TILE),
        in_specs=[
            pl.BlockSpec((None, MIXER_TILE, D_MODEL), lambda b, i: (b, i, 0)),
            const(g),
            const(w_in, pipeline_mode=pl.Buffered(1)),
            pl.BlockSpec((MIXER_TILE, HEAD_DIM), lambda b, i: (i, 0)),
            pl.BlockSpec((MIXER_TILE, HEAD_DIM), lambda b, i: (i, 0)),
            const(mask), const(qdec), const(kdec), const(bdec), const(blk),
            const(meta), const(cos_m), const(sin_m), const(kdec_m),
        ],
        out_specs=(
            pl.BlockSpec((None, MIXER_TILE, SSM_WIDTH), lambda b, i: (b, i, 0)),
            pl.BlockSpec((None, MIXER_TILE, RET_WIDTH), lambda b, i: (b, i, 0)),
            pl.BlockSpec((N_META, SSM_WIDTH), lambda b, i: (0, 0)),
        ),
        out_shape=(jax.ShapeDtypeStruct((bsz, seq, SSM_WIDTH), F32),
                   jax.ShapeDtypeStruct((bsz, seq, RET_WIDTH), BF16),
                   jax.ShapeDtypeStruct((N_META, SSM_WIDTH), F32)),
        scratch_shapes=[pltpu.VMEM((D_MODEL, IN_WIDTH), BF16),
                        pltpu.VMEM((HEAD_PAIRS, PAIR_DIM, PAIR_DIM), F32),
                        pltpu.VMEM((HEAD_PAIRS, PAIR_DIM, PAIR_DIM), F32)],
        compiler_params=pltpu.CompilerParams(
            dimension_semantics=("arbitrary", "arbitrary"), vmem_limit_bytes=VMEM_LIMIT),
        name="mixer_call",
    )(x, g, w_in, cos, sin, mask, qdec, kdec, bdec, blk, meta, cos_m, sin_m, kdec_m)


def _dot_t(a, b):
    return lax.dot_general(a, b, (((1,), (1,)), ((), ())), preferred_element_type=F32)


def _chunk_transpose(arrs):
    n = len(arrs)
    chunk = lax.broadcasted_iota(I32, (1, LANES), 1) // SSM_GROUP
    arrs = list(arrs)
    s = n // 2
    while s:
        keep = (chunk & s) == 0
        nxt = list(arrs)
        for i in range(n):
            if i & s == 0:
                lo, hi = arrs[i], arrs[i + s]
                nxt[i] = jnp.where(keep, lo, pltpu.roll(hi, s * SSM_GROUP, axis=1))
                nxt[i + s] = jnp.where(keep, pltpu.roll(lo, LANES - s * SSM_GROUP, axis=1), hi)
        arrs = nxt
        s //= 2
    return arrs


def _s5_kernel(u_lo_ref, u_hi_ref, um_ref, krow_ref, bmat_ref, cre_ref, cim_ref, ar_ref, ai_ref, wglu_ref,
               y_lo_ref, y_hi_ref, t0_ref, ug_ref, yg_ref):
    n_blocks = u_lo_ref.shape[0] // S5_BLOCK

    @pl.when(pl.program_id(0) == 0)
    def _():
        lane = lax.broadcasted_iota(I32, (SSM_GROUP, S5_LANES), 1)
        for g in range(SSM_GROUPS):
            k0 = krow_ref[g]
            for a in range(S5_BLOCK):
                blk = k0 if a == 0 else jnp.where(lane >= a * SSM_GROUP,
                                                  pltpu.roll(k0, a * SSM_GROUP, axis=1), 0.0)
                t0_ref[g, a * SSM_GROUP:(a + 1) * SSM_GROUP, :] = blk.astype(BF16)

    per_tile = LANES // SSM_GROUP
    for half, uh_ref in enumerate((u_lo_ref, u_hi_ref)):
        for t in range(S5_BLOCK // per_tile):
            words = [pltpu.bitcast(uh_ref[pl.ds(t * per_tile + k, n_blocks, stride=S5_BLOCK), :].astype(BF16),
                                   jnp.uint32) for k in range(per_tile)]
            for m, w in enumerate(_chunk_transpose(words)):
                ug_ref[half * per_tile + m, :, t * LANES:(t + 1) * LANES] = pltpu.bitcast(w, BF16)

    re, im, re0, im0 = [], [], [], []
    for p in range(SSM_GROUPS // 2):
        g0, g1 = 2 * p, 2 * p + 1
        v = _dot(ug_ref[g0], bmat_ref[g0]) + _dot(ug_ref[g1], bmat_ref[g1])
        v0 = (_dot(um_ref[g0].astype(BF16), bmat_ref[g0]) + _dot(um_ref[g1].astype(BF16), bmat_ref[g1]))[0:1]
        re.append(v[:, :LANES])
        im.append(v[:, LANES:])
        re0.append(v0[:, :LANES])
        im0.append(v0[:, LANES:])
    re, im, re0, im0 = (jnp.concatenate(parts, axis=1) for parts in (re, im, re0, im0))

    row = lax.broadcasted_iota(I32, re.shape, 0)
    ar, ai = ar_ref[0:1], ai_ref[0:1]
    re = re + jnp.where(row == 0, ar * re0 - ai * im0, 0.0)
    im = im + jnp.where(row == 0, ar * im0 + ai * re0, 0.0)
    d = 1
    while d < SCAN_ROWS:
        ar, ai = ar_ref[d - 1:d], ai_ref[d - 1:d]
        inside = row % SCAN_ROWS >= d
        sre = jnp.where(inside, pltpu.roll(re, d, axis=0), 0.0)
        sim = jnp.where(inside, pltpu.roll(im, d, axis=0), 0.0)
        re, im = re + ar * sre - ai * sim, im + ar * sim + ai * sre
        d *= 2
    ar, ai = ar_ref[...], ai_ref[...]
    re_tiles, im_tiles = [re[:SCAN_ROWS]], [im[:SCAN_ROWS]]
    for r in range(1, n_blocks // SCAN_ROWS):
        cre_ = re_tiles[-1][SCAN_ROWS - 1:SCAN_ROWS]
        cim_ = im_tiles[-1][SCAN_ROWS - 1:SCAN_ROWS]
        rs = slice(r * SCAN_ROWS, (r + 1) * SCAN_ROWS)
        re_tiles.append(re[rs] + ar * cre_ - ai * cim_)
        im_tiles.append(im[rs] + ar * cim_ + ai * cre_)
    re = jnp.concatenate(re_tiles, axis=0)
    im = jnp.concatenate(im_tiles, axis=0)
    pre = jnp.where(row == 0, re0, pltpu.roll(re, 1, axis=0)).astype(BF16)
    pim = jnp.where(row == 0, im0, pltpu.roll(im, 1, axis=0)).astype(BF16)

    for g in range(SSM_GROUPS):
        ps = slice((g // 2) * LANES, (g // 2 + 1) * LANES)
        yg_ref[g] = (_dot(ug_ref[g], t0_ref[g]) + _dot_t(pre[:, ps], cre_ref[g])
                     + _dot_t(pim[:, ps], cim_ref[g]))

    wglu = wglu_ref[...]
    for t in range(S5_BLOCK // per_tile):
        ts = slice(t * LANES, (t + 1) * LANES)
        halves = [_chunk_transpose([yg_ref[half * per_tile + m, :, ts] for m in range(per_tile)])
                  for half in range(SSM_GROUPS // per_tile)]
        for k in range(per_tile):
            y = jnp.concatenate([h[k] for h in halves], axis=1)
            y = jax.nn.gelu(y, approximate=True)
            y = y * _sigmoid(_dot(y.astype(BF16), wglu))
            i = t * per_tile + k
            y_lo_ref[pl.ds(i, n_blocks, stride=S5_BLOCK), :] = y[:, :LANES]
            y_hi_ref[pl.ds(i, n_blocks, stride=S5_BLOCK), :] = y[:, LANES:]


def _s5_call(u, um, krow, bmat, cre, cim, ar, ai, wglu):
    bsz, seq, _ = u.shape
    n_blocks = seq // S5_BLOCK
    const = lambda a: pl.BlockSpec(a.shape, lambda b: (0,) * a.ndim)
    return pl.pallas_call(
        _s5_kernel,
        grid=(bsz,),
        in_specs=[pl.BlockSpec((None, seq, LANES), lambda b: (b, 0, 0)),
                  pl.BlockSpec((None, seq, LANES), lambda b: (b, 0, 1)),
                  const(um), const(krow), const(bmat), const(cre), const(cim), const(ar), const(ai),
                  const(wglu)],
        out_specs=(pl.BlockSpec((None, seq, LANES), lambda b: (b, 0, 0)),
                   pl.BlockSpec((None, seq, LANES), lambda b: (b, 0, 0))),
        out_shape=(jax.ShapeDtypeStruct((bsz, seq, LANES), F32),
                   jax.ShapeDtypeStruct((bsz, seq, LANES), F32)),
        scratch_shapes=[pltpu.VMEM((SSM_GROUPS, S5_LANES, S5_LANES), BF16),
                        pltpu.VMEM((SSM_GROUPS, n_blocks, S5_LANES), BF16),
                        pltpu.VMEM((SSM_GROUPS, n_blocks, S5_LANES), F32)],
        compiler_params=pltpu.CompilerParams(
            dimension_semantics=("arbitrary",), vmem_limit_bytes=VMEM_LIMIT),
        name="s5_call",
    )(u, u, um, krow, bmat, cre, cim, ar, ai, wglu)


def _s5_operators(lam_re, lam_im, log_dt, b_re, b_im, c_re, c_im, d_skip):
    n_groups = lam_re.shape[0]
    lam = lax.complex(lam_re, lam_im)
    lam_dt = lam * jnp.exp(log_dt)[:, None]
    lam_bar = jnp.exp(lam_dt)
    b_bar = ((lam_bar - 1.0) / lam)[..., None] * lax.complex(b_re, b_im)
    c = lax.complex(c_re, c_im)
    tau = jnp.arange(S5_BLOCK + 1, dtype=F32)
    pows = jnp.exp(lam_dt[None] * tau[:, None, None])
    kern = jnp.real(jnp.einsum('ghp,tgp,gpk->gkth', c, pows[:S5_BLOCK], b_bar))
    skip = (jnp.eye(SSM_GROUP, dtype=F32)[None, :, None, :] * d_skip[:, None, None, :]
            * (tau[:S5_BLOCK] == 0).astype(F32)[None, None, :, None])
    krow = (kern + skip).reshape(n_groups, SSM_GROUP, S5_LANES)
    even = (jnp.arange(n_groups) % 2 == 0)[:, None, None]
    pair_pad = lambda m: jnp.concatenate([jnp.where(even, m, 0.0), jnp.where(even, 0.0, m)], axis=-1)
    bm = (pows[S5_BLOCK - 1 - jnp.arange(S5_BLOCK)].transpose(1, 0, 2)[:, :, None, :]
          * b_bar.transpose(0, 2, 1)[:, None, :, :]).reshape(n_groups, S5_LANES, SSM_STATE)
    bmat = jnp.concatenate([pair_pad(jnp.real(bm)), pair_pad(jnp.imag(bm))], axis=-1)
    cm = (pows[1:].transpose(1, 0, 2)[:, :, None, :] * c[:, None, :, :]).reshape(n_groups, S5_LANES, SSM_STATE)
    cre, cim = pair_pad(jnp.real(cm)), pair_pad(-jnp.imag(cm))
    step = S5_BLOCK * (1.0 + jnp.arange(SCAN_ROWS, dtype=F32))
    adec = jnp.exp(lam_dt[None, :, :] * step[:, None, None]).reshape(SCAN_ROWS, n_groups * SSM_STATE)
    return krow, bmat.astype(BF16), cre.astype(BF16), cim.astype(BF16), jnp.real(adec), jnp.imag(adec)


def _first_hit(values, target):
    hits, taken = [], None
    for v in values:
        hit = (v >= target) if taken is None else jnp.logical_and(v >= target, jnp.logical_not(taken))
        taken = hit if taken is None else jnp.logical_or(taken, hit)
        hits.append(hit)
    return hits


def _stack_rows(rows, n_rows):
    idx = lax.broadcasted_iota(I32, (n_rows, rows[0].shape[1]), 0)
    out = jnp.zeros((n_rows, rows[0].shape[1]), F32)
    for k, r in enumerate(rows):
        out = jnp.where(idx == k, r, out)
    return out


def _project(rs, x_ref, ys_lo_ref, ys_hi_ref, yr_ref, wout_ref, h_ref):
    ys = jnp.concatenate([ys_lo_ref[rs, :], ys_hi_ref[rs, :]], axis=1).astype(BF16)
    h = x_ref[rs, :] + _dot(ys, wout_ref[:SSM_WIDTH, :]) + _dot(yr_ref[rs, :], wout_ref[SSM_WIDTH:, :])
    h_ref[rs, :] = h.astype(BF16)
    return h


def _route(lt, tri_ref):
    gl = [lt[g:g + 1, :] for g in range(N_GROUPS)]
    gmax = functools.reduce(jnp.maximum, gl)
    g_w = 1.0 / functools.reduce(lambda a, b: a + b, [jnp.exp(l - gmax) for l in gl])
    sel = _first_hit(gl, gmax)
    ev = []
    for e in range(EXPERTS_PER_GROUP):
        acc = jnp.zeros_like(gmax)
        for g in range(N_GROUPS):
            k = N_GROUPS + g * EXPERTS_PER_GROUP + e
            acc = jnp.where(sel[g], lt[k:k + 1, :], acc)
        ev.append(acc)
    m1 = functools.reduce(jnp.maximum, ev)
    first = _first_hit(ev, m1)
    rest = [jnp.where(f, -jnp.inf, v) for f, v in zip(first, ev)]
    m2 = functools.reduce(jnp.maximum, rest)
    second = _first_hit(rest, m2)
    e2 = jnp.exp(m2 - m1)
    w1 = g_w / (1.0 + e2)
    w2 = e2 * w1
    combine = [jnp.where(f, w1, 0.0) + jnp.where(s, w2, 0.0) for f, s in zip(first, second)]

    sel_f = [jnp.where(s, 1.0, 0.0) for s in sel]
    incl = _dot(_stack_rows(sel_f, 8).astype(BF16), tri_ref[...])
    dest = jnp.zeros_like(gmax)
    seg_start = jnp.zeros((1, 1), F32)
    counts = []
    for g in range(N_GROUPS):
        run = incl[g:g + 1, :]
        cnt = run[:, TILE - 1:TILE]
        counts.append(cnt)
        dest = dest + sel_f[g] * (seg_start + run - 1.0)
        seg_start = seg_start + PIECE * jnp.floor((cnt + (PIECE - 1.0)) * (1.0 / PIECE))
    return combine, dest, counts


def _proj_kernel(x_ref, ys_lo_ref, ys_hi_ref, yr_ref, wout32_ref, g_ref, wr_ref, br_ref,
                 tri_ref, h_ref, stage_ref, dest_ref, cnt_ref, wout_ref):
    @pl.when(pl.program_id(0) == 0)
    def _():
        wout_ref[...] = wout32_ref[...].astype(BF16)

    tiles = range(PROJ_TILES)
    proj_refs = (x_ref, ys_lo_ref, ys_hi_ref, yr_ref, wout_ref, h_ref)
    chunks = [[slice(tile * TILE + b * ROW_CHUNK, tile * TILE + (b + 1) * ROW_CHUNK)
               for b in range(TILE // ROW_CHUNK)] for tile in tiles]
    h_parts = [[_project(rs, *proj_refs) for rs in chunks[tile]] for tile in tiles]

    logits, staged = [], []
    for tile in tiles:
        parts, rows, inv = [], [], []
        for h in h_parts[tile]:
            t = _rms_norm(h, g_ref[...])
            parts.append(t.astype(BF16))
            peak = jnp.maximum(jnp.max(jnp.abs(t), axis=1, keepdims=True), 1e-30)
            rows.append((t * (FP8_HEADROOM / peak)).astype(BF16))
            inv.append(peak * (1.0 / FP8_HEADROOM))
        staged.append((jnp.concatenate(rows, axis=0), jnp.concatenate(inv, axis=0)))
        lt_parts = []
        for t_hi in parts:
            both = _dot_t(wr_ref[...], t_hi)
            lt_parts.append(both[:ROUTE_ROWS] + both[ROUTE_ROWS:])
        logits.append(jnp.concatenate(lt_parts, axis=1) + br_ref[...])

    routed = [_route(logits[tile], tri_ref) for tile in tiles]

    for tile in tiles:
        combine, dest, counts = routed[tile]
        dest_ref[tile] = _stack_rows([dest], 8)
        cnt_ref[tile] = _stack_rows([c + jnp.zeros((1, LANES), F32) for c in counts], 8)
        perm = jnp.where(lax.broadcasted_iota(I32, (SORTED_ROWS, TILE), 0) == dest.astype(I32),
                         1.0, 0.0).astype(BF16)
        c_hi = [c.astype(BF16).astype(F32) for c in combine]
        c_lo = [c - hi for c, hi in zip(combine, c_hi)]
        cw = _stack_rows(c_hi + c_lo, LANES).T
        rows, inv = staged[tile]
        inv_hi = inv.astype(BF16).astype(F32)
        lane = lax.broadcasted_iota(I32, (1, LANES), 1)
        cw = jnp.where(lane == INV_LANE, inv_hi, jnp.where(lane == INV_LANE + 1, inv - inv_hi, cw))
        t_ext = jnp.concatenate([rows, cw.astype(BF16)], axis=1)
        stage_ref[tile * SORTED_ROWS:(tile + 1) * SORTED_ROWS, :] = _dot(perm, t_ext).astype(BF16)


def _proj_call(x, ys_lo, ys_hi, yr, wout, g, wr, br, tri):
    n_tok = x.shape[0]
    n_tiles = n_tok // TILE
    const = lambda *shape: pl.BlockSpec(shape, lambda i: (0,) * len(shape))
    rows = lambda width: pl.BlockSpec((PROJ_TILES * TILE, width), lambda i: (i, 0))
    return pl.pallas_call(
        _proj_kernel,
        grid=(n_tiles // PROJ_TILES,),
        in_specs=[
            rows(D_MODEL), rows(LANES), rows(LANES), rows(RET_WIDTH),
            pl.BlockSpec((D_MODEL, D_MODEL), lambda i: (0, 0), pipeline_mode=pl.Buffered(1)), const(1, D_MODEL),
            const(2 * ROUTE_ROWS, D_MODEL), const(ROUTE_ROWS, 1), const(TILE, TILE),
        ],
        out_specs=(rows(D_MODEL),
                   pl.BlockSpec((PROJ_TILES * SORTED_ROWS, EXT_WIDTH), lambda i: (i, 0)),
                   pl.BlockSpec((PROJ_TILES, 8, TILE), lambda i: (i, 0, 0)),
                   pl.BlockSpec((PROJ_TILES, 8, LANES), lambda i: (i, 0, 0))),
        scratch_shapes=[pltpu.VMEM((D_MODEL, D_MODEL), BF16)],
        out_shape=(jax.ShapeDtypeStruct((n_tok, D_MODEL), BF16),
                   jax.ShapeDtypeStruct((n_tiles * SORTED_ROWS, EXT_WIDTH), BF16),
                   jax.ShapeDtypeStruct((n_tiles, 8, TILE), F32),
                   jax.ShapeDtypeStruct((n_tiles, 8, LANES), F32)),
        compiler_params=pltpu.CompilerParams(
            dimension_semantics=("arbitrary",), vmem_limit_bytes=VMEM_LIMIT),
        name="proj_call",
    )(x, ys_lo, ys_hi, yr, wout, g, wr, br, tri)


def _sort_tables(cnt, n_steps):
    n_tiles = cnt.shape[0]
    npc = (cnt + PIECE - 1) // PIECE
    seg = jnp.cumsum(npc, axis=1) - npc
    before = jnp.cumsum(npc, axis=0) - npc
    n_tile_g = (jnp.sum(npc, axis=0) + MOE_PIECES - 1) // MOE_PIECES
    t_off = jnp.cumsum(n_tile_g) - n_tile_g
    j = jnp.arange(SORTED_PIECES, dtype=I32)[None, :, None]
    in_g = jnp.logical_and(j >= seg[:, None, :], j < (seg + npc)[:, None, :])
    pos = jnp.sum(jnp.where(in_g, MOE_PIECES * t_off[None, None, :] + before[:, None, :] + j - seg[:, None, :], 0),
                  axis=-1)
    valid = jnp.any(in_g, axis=-1)
    stage_piece = jnp.arange(n_tiles * SORTED_PIECES, dtype=I32).reshape(n_tiles, SORTED_PIECES)
    n_slots = n_steps * MOE_PIECES
    src = jnp.zeros((n_slots,), I32).at[jnp.where(valid, pos, n_slots).reshape(-1)].set(
        stage_piece.reshape(-1), mode='drop')
    steps = jnp.arange(n_steps, dtype=I32)
    g_step = jnp.minimum(jnp.sum(steps[:, None] >= (t_off + n_tile_g)[None, :], axis=1), N_GROUPS - 1)
    n_live = jnp.sum(n_tile_g).reshape(1)
    back = jnp.where(valid, pos, 0).reshape(-1)
    return src, g_step.astype(I32), n_live.astype(I32), back.astype(I32)


def _piece_copy(src_ref, piece, buf_ref, slot, j, sem_ref):
    start = piece * PIECE if isinstance(piece, int) else pl.multiple_of(piece * PIECE, PIECE)
    return pltpu.make_async_copy(src_ref.at[pl.ds(start, PIECE)],
                                 buf_ref.at[slot, pl.ds(j * PIECE, PIECE)], sem_ref.at[slot])


def _fetch_pieces(table_ref, src_ref, buf_ref, sem_ref, row, slot, n_pieces):
    for j in range(n_pieces):
        _piece_copy(src_ref, table_ref[row * n_pieces + j], buf_ref, slot, j, sem_ref).start()


def _wait_pieces(src_ref, buf_ref, sem_ref, slot, n_pieces):
    for j in range(n_pieces):
        _piece_copy(src_ref, 0, buf_ref, slot, j, sem_ref).wait()


def _gather_pieces(table_ref, src_ref, buf_ref, sem_ref, n_pieces):
    step = pl.program_id(0)
    last = pl.num_programs(0) - 1
    args = (src_ref, buf_ref, sem_ref)

    @pl.when(step == 0)
    def _():
        _fetch_pieces(table_ref, *args, step, 0, n_pieces)

    _wait_pieces(*args, step % 2, n_pieces)

    def fetch_next():
        _fetch_pieces(table_ref, *args, jnp.minimum(step + 1, last), (step + 1) % 2, n_pieces)

    def drain():
        @pl.when(step == last)
        def _():
            _wait_pieces(*args, (step + 1) % 2, n_pieces)

    return fetch_next, drain


def _to_fp8(x, headroom=FP8_HEADROOM):
    peak = jnp.maximum(jnp.max(jnp.abs(x), axis=(0, 1), keepdims=True), 1e-30)
    return (x * (headroom / peak)).astype(FP8), peak * (1.0 / headroom)


def _moe_kernel(src_ref, gstep_ref, nlive_ref, stage_ref, wg_ref, wu_ref, wd_ref, y_ref,
                buf_ref, sem_ref, wgb_ref, wub_ref, wdb_ref, inv_ref):
    step = pl.program_id(0)
    fetch_next, drain = _gather_pieces(src_ref, stage_ref, buf_ref, sem_ref, MOE_PIECES)

    @pl.when(jnp.logical_or(step == 0, gstep_ref[step] != gstep_ref[jnp.maximum(step - 1, 0)]))
    def _():
        ones = jnp.ones((1, LANES), F32)
        for e in range(EXPERTS_PER_GROUP):
            wgb_ref[e], inv = _to_fp8(wg_ref[e])
            inv_ref[e:e + 1, :] = inv * ones
            wub_ref[e], inv = _to_fp8(wu_ref[e])
            inv_ref[EXPERTS_PER_GROUP + e:EXPERTS_PER_GROUP + e + 1, :] = inv * ones
        peak = functools.reduce(jnp.maximum, [jnp.max(jnp.abs(wd_ref[e]), axis=(0, 1), keepdims=True)
                                              for e in range(EXPERTS_PER_GROUP)])
        peak = jnp.maximum(peak, 1e-30)
        for e in range(EXPERTS_PER_GROUP):
            wdb_ref[e * EXPERT_FF:(e + 1) * EXPERT_FF, :] = (wd_ref[e] * (FP8_HEADROOM / peak)).astype(FP8)
        inv_ref[2 * EXPERTS_PER_GROUP:2 * EXPERTS_PER_GROUP + 1, :] = peak * (1.0 / FP8_HEADROOM) * ones

    @pl.when(step < nlive_ref[0])
    def _():
        slot = step % 2
        fetch_next()

        def up(rs):
            t = buf_ref[slot, rs, :D_MODEL].astype(FP8)
            t_inv = (buf_ref[slot, rs, D_MODEL + INV_LANE:D_MODEL + INV_LANE + 1].astype(F32)
                     + buf_ref[slot, rs, D_MODEL + INV_LANE + 1:D_MODEL + INV_LANE + 2].astype(F32))
            out = []
            for e in range(EXPERTS_PER_GROUP):
                g_inv = inv_ref[e:e + 1, 0:1] * t_inv
                u_inv = inv_ref[EXPERTS_PER_GROUP + e:EXPERTS_PER_GROUP + e + 1, 0:1] * t_inv
                out.append((_dot(t, wgb_ref[e]) * g_inv, _dot(t, wub_ref[e]), u_inv))
            return out

        def down(rs, hidden):
            cw = buf_ref[slot, rs, D_MODEL:].astype(F32)
            acts = []
            for e, (hg, hu, u_inv) in enumerate(hidden):
                c = (cw[:, e:e + 1] + cw[:, EXPERTS_PER_GROUP + e:EXPERTS_PER_GROUP + e + 1]) * u_inv
                acts.append(hg * _sigmoid(hg) * hu * c)
            act, a_inv = _to_fp8(jnp.concatenate(acts, axis=1))
            d_inv = inv_ref[2 * EXPERTS_PER_GROUP:2 * EXPERTS_PER_GROUP + 1, 0:1] * a_inv
            y_ref[rs, :] = (_dot(act, wdb_ref[...]) * d_inv).astype(BF16)

        chunks = [slice(b * ROW_CHUNK, (b + 1) * ROW_CHUNK) for b in range(MOE_TILE // ROW_CHUNK)]
        hidden = [up(rs) for rs in chunks]
        for rs, hid in zip(chunks, hidden):
            down(rs, hid)

    @pl.when(step >= nlive_ref[0])
    def _():
        fetch_next()
        y_ref[...] = jnp.zeros_like(y_ref)

    drain()


def _moe_call(src, g_step, n_live, stage, wg, wu, wd, n_steps):
    grp = lambda shape: pl.BlockSpec((EXPERTS_PER_GROUP,) + shape, lambda s, src, gs, nl: (gs[s], 0, 0))
    return pl.pallas_call(
        _moe_kernel,
        grid_spec=pltpu.PrefetchScalarGridSpec(
            num_scalar_prefetch=3,
            grid=(n_steps,),
            in_specs=[pl.BlockSpec(memory_space=pl.ANY),
                      grp((D_MODEL, EXPERT_FF)), grp((D_MODEL, EXPERT_FF)), grp((EXPERT_FF, D_MODEL))],
            out_specs=pl.BlockSpec((MOE_TILE, D_MODEL), lambda s, src, gs, nl: (s, 0)),
            scratch_shapes=[pltpu.VMEM((2, MOE_TILE, EXT_WIDTH), BF16), pltpu.SemaphoreType.DMA((2,)),
                            pltpu.VMEM((EXPERTS_PER_GROUP, D_MODEL, EXPERT_FF), FP8),
                            pltpu.VMEM((EXPERTS_PER_GROUP, D_MODEL, EXPERT_FF), FP8),
                            pltpu.VMEM((EXPERTS_PER_GROUP * EXPERT_FF, D_MODEL), FP8),
                            pltpu.VMEM((16, LANES), F32)],
        ),
        out_shape=jax.ShapeDtypeStruct((n_steps * MOE_TILE, D_MODEL), BF16),
        compiler_params=pltpu.CompilerParams(
            dimension_semantics=("arbitrary",), vmem_limit_bytes=VMEM_LIMIT),
        name="moe_call",
    )(src, g_step, n_live, stage, wg, wu, wd)


def _final_kernel(back_ref, ysort_ref, h_ref, dest_ref, g_ref, o_ref, buf_ref, sem_ref):
    step = pl.program_id(0)
    fetch_next, drain = _gather_pieces(back_ref, ysort_ref, buf_ref, sem_ref, FINAL_TILES * SORTED_PIECES)
    fetch_next()
    for t in range(FINAL_TILES):
        dest = _stack_rows([dest_ref[t, 0:1, :]], LANES).T[:, 0:1].astype(I32)
        unperm = jnp.where(lax.broadcasted_iota(I32, (TILE, SORTED_ROWS), 1) == dest, 1.0, 0.0).astype(BF16)
        rows = buf_ref[step % 2, t * SORTED_ROWS:(t + 1) * SORTED_ROWS, :]
        for b in range(TILE // ROW_CHUNK):
            rs = slice(b * ROW_CHUNK, (b + 1) * ROW_CHUNK)
            os = slice(t * TILE + b * ROW_CHUNK, t * TILE + (b + 1) * ROW_CHUNK)
            o_ref[os, :] = _rms_norm(h_ref[os, :].astype(F32) + _dot(unperm[rs], rows), g_ref[...])
    drain()


def _final_call(back, ysort, h, dest, g):
    n_tok = h.shape[0]
    return pl.pallas_call(
        _final_kernel,
        grid_spec=pltpu.PrefetchScalarGridSpec(
            num_scalar_prefetch=1,
            grid=(n_tok // (FINAL_TILES * TILE),),
            in_specs=[pl.BlockSpec(memory_space=pl.ANY),
                      pl.BlockSpec((FINAL_TILES * TILE, D_MODEL), lambda i, back: (i, 0)),
                      pl.BlockSpec((FINAL_TILES, 8, TILE), lambda i, back: (i, 0, 0)),
                      pl.BlockSpec((1, D_MODEL), lambda i, back: (0, 0))],
            out_specs=pl.BlockSpec((FINAL_TILES * TILE, D_MODEL), lambda i, back: (i, 0)),
            scratch_shapes=[pltpu.VMEM((2, FINAL_TILES * SORTED_ROWS, D_MODEL), BF16),
                            pltpu.SemaphoreType.DMA((2,))],
        ),
        out_shape=jax.ShapeDtypeStruct((n_tok, D_MODEL), F32),
        compiler_params=pltpu.CompilerParams(
            dimension_semantics=("arbitrary",), vmem_limit_bytes=VMEM_LIMIT),
        name="final_call",
    )(back, ysort, h, dest, g)


def _rope_tables(length):
    pos = np.arange(length, dtype=np.float32)
    inv_freq = np.float32(ROPE_BASE) ** (-np.arange(0, HEAD_DIM, 2, dtype=np.float32) / np.float32(HEAD_DIM))
    ang = pos[:, None] * inv_freq[None, :]
    cos, sin = np.cos(ang), np.sin(ang)
    return np.concatenate([cos, cos], axis=-1), np.concatenate([-sin, sin], axis=-1)


def _retention_tables():
    f32 = np.float32
    gamma = f32(1.0) - f32(2.0) ** (f32(-5.0) - np.arange(HEADS, dtype=f32))
    log_g = np.log(gamma)[:, None, None]
    scale = f32(HEAD_DIM ** -0.5)
    idx = np.arange(RET_BLOCK)
    dist = np.abs(idx[:, None] - idx[None, :]).astype(f32)
    visible = (idx[None, :] // CHUNK) <= (idx[:, None] // CHUNK)
    mask = np.where(visible[None], np.exp(log_g * dist[None]), f32(0.0)) * scale
    ones = np.ones((1, 1, HEAD_DIM), f32)
    idx_f = idx.astype(f32)[None, :, None]
    pair = lambda a: a.reshape(HEAD_PAIRS, 2, a.shape[1], HEAD_DIM).transpose(0, 2, 1, 3).reshape(
        HEAD_PAIRS, a.shape[1], PAIR_DIM)
    qdec = pair(np.exp(log_g * (idx_f + f32(1.0))) * ones)
    kdec = pair(np.exp(log_g * (f32(RET_BLOCK - 1.0) - idx_f)) * scale * ones)
    meta_idx = np.arange(N_META, dtype=f32)[None, :, None]
    kdec_meta = pair(np.exp(log_g * (f32(N_META - 1.0) - meta_idx)) * scale * ones)
    blk = np.kron(np.eye(2, dtype=f32), np.ones((HEAD_DIM, HEAD_DIM), f32))
    bdec = np.exp(log_g * f32(RET_BLOCK)).reshape(HEAD_PAIRS, 2)
    bdec = np.stack([np.kron(np.diag(b), np.ones((HEAD_DIM, HEAD_DIM), f32)) for b in bdec])
    return tuple(a.astype(f32) for a in (mask, qdec, kdec, bdec, blk, kdec_meta))


def kernel(x, meta_tokens, norm_mix_g, w_in, ssm_lambda_re, ssm_lambda_im, ssm_log_dt, ssm_b_re, ssm_b_im, ssm_c_re, ssm_c_im, ssm_d, w_glu, w_out, norm_ffn_g, w_router_group, b_router_group, w_router_expert, b_router_expert, w_gate, w_up, w_down, norm_final_g):
    bsz, seq, _ = x.shape
    assert seq % TILE == 0 and seq % MIXER_TILE == 0 and MIXER_TILE % RET_BLOCK == 0 and RET_BLOCK % CHUNK == 0
    n_blocks = seq // S5_BLOCK
    assert n_blocks % SCAN_ROWS == 0
    n_tok = bsz * seq
    assert n_tok % (FINAL_TILES * TILE) == 0 and n_tok % (PROJ_TILES * TILE) == 0
    n_tiles = n_tok // TILE
    n_steps = -(-n_tiles * (TILE_PIECES + N_GROUPS - 1) // MOE_PIECES) + N_GROUPS

    cos, sin = _rope_tables(N_META + seq)
    mask, qdec, kdec, bdec, blk, kdec_meta = _retention_tables()
    g_mix = norm_mix_g[0][None, :]
    u, y_ret, u_meta = _mixer_call(x, g_mix, w_in[0], cos[N_META:], sin[N_META:], mask, qdec, kdec, bdec, blk,
                                   meta_tokens, cos[:N_META], sin[:N_META], kdec_meta)

    s5_ops = _s5_operators(
        ssm_lambda_re[0], ssm_lambda_im[0], ssm_log_dt[0], ssm_b_re[0], ssm_b_im[0],
        ssm_c_re[0], ssm_c_im[0], ssm_d[0])
    um = u_meta.reshape(S5_BLOCK, SSM_GROUPS, SSM_GROUP).transpose(1, 0, 2).reshape(SSM_GROUPS, 1, S5_LANES)
    um = jnp.pad(um, ((0, 0), (0, 7), (0, 0)))
    y_lo, y_hi = _s5_call(u, um, *s5_ops, w_glu[0].astype(BF16))

    w_r = jnp.concatenate(
        [w_router_group[0].T, w_router_expert[0].transpose(0, 2, 1).reshape(N_EXPERTS, D_MODEL)], axis=0)
    w_r = jnp.pad(w_r, ((0, ROUTE_ROWS - w_r.shape[0]), (0, 0)))
    b_r = jnp.concatenate([b_router_group[0], b_router_expert[0].reshape(-1)])
    b_r = jnp.pad(b_r, (0, ROUTE_ROWS - b_r.shape[0]))[:, None]
    w_r_hi = w_r.astype(BF16)
    w_r = jnp.concatenate([w_r_hi, (w_r - w_r_hi.astype(F32)).astype(BF16)], axis=0)
    tri = jnp.asarray(np.arange(TILE)[:, None] <= np.arange(TILE)[None, :], BF16)

    h, stage, dest, cnt = _proj_call(
        x.reshape(n_tok, D_MODEL), y_lo.reshape(n_tok, LANES), y_hi.reshape(n_tok, LANES),
        y_ret.reshape(n_tok, RET_WIDTH),
        w_out[0], norm_ffn_g[0][None, :], w_r, b_r, tri)
    src, g_step, n_live, back = _sort_tables(cnt[:, :N_GROUPS, 0].astype(I32), n_steps)
    y_sorted = _moe_call(src, g_step, n_live, stage, w_gate[0], w_up[0], w_down[0], n_steps)
    out = _final_call(back, y_sorted, h, dest, norm_final_g[None, :])
    return out.reshape(bsz, seq, D_MODEL)
```

```python
import functools

import jax
import jax.numpy as jnp
import numpy as np
from jax import lax
from jax.experimental import pallas as pl
from jax.experimental.pallas import tpu as pltpu

D_MODEL = 1024
N_META = 16
CHUNK = 64
EPS = 1e-6
SSM_WIDTH = 256
SSM_GROUP = 16
SSM_GROUPS = 16
SSM_STATE = 64
RET_WIDTH = 768
HEAD_DIM = 128
HEADS = 6
HEAD_PAIRS = HEADS // 2
PAIR_DIM = 2 * HEAD_DIM
ROPE_BASE = 10000.0
IN_WIDTH = SSM_WIDTH + 4 * RET_WIDTH
N_GROUPS = 4
EXPERTS_PER_GROUP = 4
N_EXPERTS = 16
EXPERT_FF = 256

S5_BLOCK = 16
S5_LANES = S5_BLOCK * SSM_GROUP
SCAN_ROWS = 8
RET_BLOCK = 256
TILE = 512
MIXER_TILE = 1024
LANES = 128
ROUTE_ROWS = 32
ROW_CHUNK = 256
PROJ_TILES = 2
FINAL_TILES = 4
PIECE = 16
TILE_PIECES = TILE // PIECE
MOE_TILE = 1024
MOE_PIECES = MOE_TILE // PIECE
SORTED_PIECES = TILE_PIECES + N_GROUPS
SORTED_ROWS = SORTED_PIECES * PIECE
EXT_WIDTH = D_MODEL + LANES
INV_LANE = 2 * EXPERTS_PER_GROUP
V7X_VMEM_BYTES = 64 * 1024 * 1024
VMEM_LIMIT = V7X_VMEM_BYTES * 7 // 8

F32 = jnp.float32
BF16 = jnp.bfloat16
FP8 = jnp.float8_e4m3fn
FP8_HEADROOM = 256.0
I32 = jnp.int32


def _dot(a, b):
    return jnp.dot(a, b, preferred_element_type=F32)


def _sigmoid(x):
    return 1.0 / (1.0 + jnp.exp(-x))


def _rms_norm(x, g):
    return x * lax.rsqrt(jnp.mean(x * x, axis=-1, keepdims=True) + EPS) * g


def _rope(t, cos, sin_signed):
    return t * cos + pltpu.roll(t, HEAD_DIM // 2, axis=1) * sin_signed


def _meta_state(meta_ref, g_ref, w_ref, cos_ref, sin_ref, kdec_ref, blk_ref, u_ref, r0_ref):
    a = _rms_norm(meta_ref[...], g_ref[...]).astype(BF16)
    u_ref[...] = _dot(a, w_ref[:, :SSM_WIDTH])
    k_off = SSM_WIDTH + RET_WIDTH
    v_off = SSM_WIDTH + 2 * RET_WIDTH
    cos = cos_ref[...]
    sin = sin_ref[...]
    for p in range(HEAD_PAIRS):
        k = _dot(a, w_ref[:, k_off + p * PAIR_DIM:k_off + (p + 1) * PAIR_DIM])
        v = _dot(a, w_ref[:, v_off + p * PAIR_DIM:v_off + (p + 1) * PAIR_DIM])
        k = jnp.concatenate([_rope(k[:, :HEAD_DIM], cos, sin), _rope(k[:, HEAD_DIM:], cos, sin)], axis=1)
        kd = (k * kdec_ref[p]).astype(BF16)
        r0_ref[p] = _dot_rows(kd, v.astype(BF16)) * blk_ref[...]


def _dot_rows(a, b):
    return lax.dot_general(a, b, (((0,), (0,)), ((), ())), preferred_element_type=F32)


def _mixer_kernel(x_ref, g_ref, w32_ref, cos_ref, sin_ref, mask_ref, qdec_ref, kdec_ref, bdec_ref, blk_ref,
                  meta_ref, cos_m_ref, sin_m_ref, kdec_m_ref, u_lo_ref, u_hi_ref, y_ref, um_ref,
                  w_ref, r0_ref, r_ref):
    first_tile = pl.program_id(1) == 0

    @pl.when(jnp.logical_and(pl.program_id(0) == 0, first_tile))
    def _():
        for c in range(0, IN_WIDTH, SSM_WIDTH):
            w_ref[:, c:c + SSM_WIDTH] = w32_ref[:, c:c + SSM_WIDTH].astype(BF16)
        _meta_state(meta_ref, g_ref, w_ref, cos_m_ref, sin_m_ref, kdec_m_ref, blk_ref, um_ref, r0_ref)

    @pl.when(first_tile)
    def _():
        r_ref[...] = r0_ref[...]

    off = SSM_WIDTH
    for b in range(MIXER_TILE // RET_BLOCK):
        bs = slice(b * RET_BLOCK, (b + 1) * RET_BLOCK)
        a = _rms_norm(x_ref[bs, :], g_ref[...]).astype(BF16)
        u = _dot(a, w_ref[:, :SSM_WIDTH])
        u_lo_ref[bs, :] = u[:, :LANES]
        u_hi_ref[bs, :] = u[:, LANES:]
        q_all = _dot(a, w_ref[:, off:off + RET_WIDTH])
        k_all = _dot(a, w_ref[:, off + RET_WIDTH:off + 2 * RET_WIDTH])
        v_all = _dot(a, w_ref[:, off + 2 * RET_WIDTH:off + 3 * RET_WIDTH])
        gate = _dot(a, w_ref[:, off + 3 * RET_WIDTH:off + 4 * RET_WIDTH])
        cos = cos_ref[bs, :]
        sin = sin_ref[bs, :]
        def scores(h):
            hs = slice(h * HEAD_DIM, (h + 1) * HEAD_DIM)
            q = _rope(q_all[:, hs], cos, sin)
            k = _rope(k_all[:, hs], cos, sin)
            return q, k, _dot_t(q.astype(BF16), k.astype(BF16))

        ahead = scores(0)
        for p in range(HEAD_PAIRS):
            ps = slice(p * PAIR_DIM, (p + 1) * PAIR_DIM)
            pair = [ahead, scores(2 * p + 1)]
            if p + 1 < HEAD_PAIRS:
                ahead = scores(2 * p + 2)
            q = jnp.concatenate([pair[0][0], pair[1][0]], axis=1)
            k = jnp.concatenate([pair[0][1], pair[1][1]], axis=1)
            v = v_all[:, ps].astype(BF16)
            state = r_ref[p]
            cross = _dot((q * qdec_ref[p]).astype(BF16), state.astype(BF16))
            kv = _dot_rows((k * kdec_ref[p]).astype(BF16), v)
            r_ref[p] = state * bdec_ref[p] + kv * blk_ref[...]
            for half, (_, _, s) in enumerate(pair):
                h = 2 * p + half
                hs = slice(h * HEAD_DIM, (h + 1) * HEAD_DIM)
                ls = slice(half * HEAD_DIM, (half + 1) * HEAD_DIM)
                o = _dot((s * mask_ref[h]).astype(BF16), v[:, ls]) + cross[:, ls]
                mu = jnp.mean(o, axis=-1, keepdims=True)
                d = o - mu
                var = jnp.mean(d * d, axis=-1, keepdims=True)
                gt = gate[:, hs]
                y_ref[bs, hs] = (gt * _sigmoid(gt) * d * lax.rsqrt(var + EPS)).astype(BF16)


def _mixer_call(x, g, w_in, cos, sin, mask, qdec, kdec, bdec, blk, meta, cos_m, sin_m, kdec_m):
    bsz, seq, _ = x.shape
    const = lambda a, **kw: pl.BlockSpec(a.shape, lambda b, i: (0,) * a.ndim, **kw)
    return pl.pallas_call(
        _mixer_kernel,
        grid=(bsz, seq // MIXER_TILE),
        in_specs=[
            pl.BlockSpec((None, MIXER_TILE, D_MODEL), lambda b, i: (b, i, 0)),
            const(g),
            const(w_in, pipeline_mode=pl.Buffered(1)),
            pl.BlockSpec((MIXER_TILE, HEAD_DIM), lambda b, i: (i, 0)),
            pl.BlockSpec((MIXER_TILE, HEAD_DIM), lambda b, i: (i, 0)),
            const(mask), const(qdec), const(kdec), const(bdec), const(blk),
            const(meta), const(cos_m), const(sin_m), const(kdec_m),
        ],
        out_specs=(
            pl.BlockSpec((None, MIXER_TILE, LANES), lambda b, i: (b, i, 0)),
            pl.BlockSpec((None, MIXER_TILE, LANES), lambda b, i: (b, i, 0)),
            pl.BlockSpec((None, MIXER_TILE, RET_WIDTH), lambda b, i: (b, i, 0)),
            pl.BlockSpec((N_META, SSM_WIDTH), lambda b, i: (0, 0)),
        ),
        out_shape=(jax.ShapeDtypeStruct((bsz, seq, LANES), F32),
                   jax.ShapeDtypeStruct((bsz, seq, LANES), F32),
                   jax.ShapeDtypeStruct((bsz, seq, RET_WIDTH), BF16),
                   jax.ShapeDtypeStruct((N_META, SSM_WIDTH), F32)),
        scratch_shapes=[pltpu.VMEM((D_MODEL, IN_WIDTH), BF16),
                        pltpu.VMEM((HEAD_PAIRS, PAIR_DIM, PAIR_DIM), F32),
                        pltpu.VMEM((HEAD_PAIRS, PAIR_DIM, PAIR_DIM), F32)],
        compiler_params=pltpu.CompilerParams(
            dimension_semantics=("arbitrary", "arbitrary"), vmem_limit_bytes=VMEM_LIMIT),
        name="mixer_call",
    )(x, g, w_in, cos, sin, mask, qdec, kdec, bdec, blk, meta, cos_m, sin_m, kdec_m)


def _dot_t(a, b):
    return lax.dot_general(a, b, (((1,), (1,)), ((), ())), preferred_element_type=F32)


def _chunk_transpose(arrs):
    n = len(arrs)
    chunk = lax.broadcasted_iota(I32, (1, LANES), 1) // SSM_GROUP
    arrs = list(arrs)
    s = n // 2
    while s:
        keep = (chunk & s) == 0
        nxt = list(arrs)
        for i in range(n):
            if i & s == 0:
                lo, hi = arrs[i], arrs[i + s]
                nxt[i] = jnp.where(keep, lo, pltpu.roll(hi, s * SSM_GROUP, axis=1))
                nxt[i + s] = jnp.where(keep, pltpu.roll(lo, LANES - s * SSM_GROUP, axis=1), hi)
        arrs = nxt
        s //= 2
    return arrs


def _s5_kernel(u_lo_ref, u_hi_ref, um_ref, krow_ref, bmat_ref, cre_ref, cim_ref, ar_ref, ai_ref, wglu_ref,
               y_lo_ref, y_hi_ref, t0_ref, ug_ref, yg_ref):
    n_blocks = u_lo_ref.shape[0] // S5_BLOCK

    @pl.when(pl.program_id(0) == 0)
    def _():
        lane = lax.broadcasted_iota(I32, (SSM_GROUP, S5_LANES), 1)
        for g in range(SSM_GROUPS):
            k0 = krow_ref[g]
            for a in range(S5_BLOCK):
                blk = k0 if a == 0 else jnp.where(lane >= a * SSM_GROUP,
                                                  pltpu.roll(k0, a * SSM_GROUP, axis=1), 0.0)
                t0_ref[g, a * SSM_GROUP:(a + 1) * SSM_GROUP, :] = blk.astype(BF16)

    per_tile = LANES // SSM_GROUP
    for half, uh_ref in enumerate((u_lo_ref, u_hi_ref)):
        for t in range(S5_BLOCK // per_tile):
            words = [pltpu.bitcast(uh_ref[pl.ds(t * per_tile + k, n_blocks, stride=S5_BLOCK), :].astype(BF16),
                                   jnp.uint32) for k in range(per_tile)]
            for m, w in enumerate(_chunk_transpose(words)):
                ug_ref[half * per_tile + m, :, t * LANES:(t + 1) * LANES] = pltpu.bitcast(w, BF16)

    re, im, re0, im0 = [], [], [], []
    for p in range(SSM_GROUPS // 2):
        g0, g1 = 2 * p, 2 * p + 1
        v = _dot(ug_ref[g0], bmat_ref[g0]) + _dot(ug_ref[g1], bmat_ref[g1])
        v0 = (_dot(um_ref[g0].astype(BF16), bmat_ref[g0]) + _dot(um_ref[g1].astype(BF16), bmat_ref[g1]))[0:1]
        re.append(v[:, :LANES])
        im.append(v[:, LANES:])
        re0.append(v0[:, :LANES])
        im0.append(v0[:, LANES:])
    re, im, re0, im0 = (jnp.concatenate(parts, axis=1) for parts in (re, im, re0, im0))

    row = lax.broadcasted_iota(I32, re.shape, 0)
    ar, ai = ar_ref[0:1], ai_ref[0:1]
    re = re + jnp.where(row == 0, ar * re0 - ai * im0, 0.0)
    im = im + jnp.where(row == 0, ar * im0 + ai * re0, 0.0)
    d = 1
    while d < SCAN_ROWS:
        ar, ai = ar_ref[d - 1:d], ai_ref[d - 1:d]
        inside = row % SCAN_ROWS >= d
        sre = jnp.where(inside, pltpu.roll(re, d, axis=0), 0.0)
        sim = jnp.where(inside, pltpu.roll(im, d, axis=0), 0.0)
        re, im = re + ar * sre - ai * sim, im + ar * sim + ai * sre
        d *= 2
    ar, ai = ar_ref[...], ai_ref[...]
    re_tiles, im_tiles = [re[:SCAN_ROWS]], [im[:SCAN_ROWS]]
    for r in range(1, n_blocks // SCAN_ROWS):
        cre_ = re_tiles[-1][SCAN_ROWS - 1:SCAN_ROWS]
        cim_ = im_tiles[-1][SCAN_ROWS - 1:SCAN_ROWS]
        rs = slice(r * SCAN_ROWS, (r + 1) * SCAN_ROWS)
        re_tiles.append(re[rs] + ar * cre_ - ai * cim_)
        im_tiles.append(im[rs] + ar * cim_ + ai * cre_)
    re = jnp.concatenate(re_tiles, axis=0)
    im = jnp.concatenate(im_tiles, axis=0)
    pre = jnp.where(row == 0, re0, pltpu.roll(re, 1, axis=0)).astype(BF16)
    pim = jnp.where(row == 0, im0, pltpu.roll(im, 1, axis=0)).astype(BF16)

    for g in range(SSM_GROUPS):
        ps = slice((g // 2) * LANES, (g // 2 + 1) * LANES)
        yg_ref[g] = (_dot(ug_ref[g], t0_ref[g]) + _dot_t(pre[:, ps], cre_ref[g])
                     + _dot_t(pim[:, ps], cim_ref[g]))

    wglu = wglu_ref[...]
    for t in range(S5_BLOCK // per_tile):
        ts = slice(t * LANES, (t + 1) * LANES)
        halves = [_chunk_transpose([yg_ref[half * per_tile + m, :, ts] for m in range(per_tile)])
                  for half in range(SSM_GROUPS // per_tile)]
        for k in range(per_tile):
            y = jnp.concatenate([h[k] for h in halves], axis=1)
            y = jax.nn.gelu(y, approximate=True)
            y = y * _sigmoid(_dot(y.astype(BF16), wglu))
            i = t * per_tile + k
            y_lo_ref[pl.ds(i, n_blocks, stride=S5_BLOCK), :] = y[:, :LANES]
            y_hi_ref[pl.ds(i, n_blocks, stride=S5_BLOCK), :] = y[:, LANES:]


def _s5_call(u_lo, u_hi, um, krow, bmat, cre, cim, ar, ai, wglu):
    bsz, seq, _ = u_lo.shape
    n_blocks = seq // S5_BLOCK
    const = lambda a: pl.BlockSpec(a.shape, lambda b: (0,) * a.ndim)
    return pl.pallas_call(
        _s5_kernel,
        grid=(bsz,),
        in_specs=[pl.BlockSpec((None, seq, LANES), lambda b: (b, 0, 0)),
                  pl.BlockSpec((None, seq, LANES), lambda b: (b, 0, 0)),
                  const(um), const(krow), const(bmat), const(cre), const(cim), const(ar), const(ai),
                  const(wglu)],
        out_specs=(pl.BlockSpec((None, seq, LANES), lambda b: (b, 0, 0)),
                   pl.BlockSpec((None, seq, LANES), lambda b: (b, 0, 0))),
        out_shape=(jax.ShapeDtypeStruct((bsz, seq, LANES), F32),
                   jax.ShapeDtypeStruct((bsz, seq, LANES), F32)),
        scratch_shapes=[pltpu.VMEM((SSM_GROUPS, S5_LANES, S5_LANES), BF16),
                        pltpu.VMEM((SSM_GROUPS, n_blocks, S5_LANES), BF16),
                        pltpu.VMEM((SSM_GROUPS, n_blocks, S5_LANES), F32)],
        compiler_params=pltpu.CompilerParams(
            dimension_semantics=("arbitrary",), vmem_limit_bytes=VMEM_LIMIT),
        name="s5_call",
    )(u_lo, u_hi, um, krow, bmat, cre, cim, ar, ai, wglu)


def _s5_operators(lam_re, lam_im, log_dt, b_re, b_im, c_re, c_im, d_skip):
    n_groups = lam_re.shape[0]
    lam = lax.complex(lam_re, lam_im)
    lam_dt = lam * jnp.exp(log_dt)[:, None]
    lam_bar = jnp.exp(lam_dt)
    b_bar = ((lam_bar - 1.0) / lam)[..., None] * lax.complex(b_re, b_im)
    c = lax.complex(c_re, c_im)
    tau = jnp.arange(S5_BLOCK + 1, dtype=F32)
    pows = jnp.exp(lam_dt[None] * tau[:, None, None])
    kern = jnp.real(jnp.einsum('ghp,tgp,gpk->gkth', c, pows[:S5_BLOCK], b_bar))
    skip = (jnp.eye(SSM_GROUP, dtype=F32)[None, :, None, :] * d_skip[:, None, None, :]
            * (tau[:S5_BLOCK] == 0).astype(F32)[None, None, :, None])
    krow = (kern + skip).reshape(n_groups, SSM_GROUP, S5_LANES)
    even = (jnp.arange(n_groups) % 2 == 0)[:, None, None]
    pair_pad = lambda m: jnp.concatenate([jnp.where(even, m, 0.0), jnp.where(even, 0.0, m)], axis=-1)
    bm = (pows[S5_BLOCK - 1 - jnp.arange(S5_BLOCK)].transpose(1, 0, 2)[:, :, None, :]
          * b_bar.transpose(0, 2, 1)[:, None, :, :]).reshape(n_groups, S5_LANES, SSM_STATE)
    bmat = jnp.concatenate([pair_pad(jnp.real(bm)), pair_pad(jnp.imag(bm))], axis=-1)
    cm = (pows[1:].transpose(1, 0, 2)[:, :, None, :] * c[:, None, :, :]).reshape(n_groups, S5_LANES, SSM_STATE)
    cre, cim = pair_pad(jnp.real(cm)), pair_pad(-jnp.imag(cm))
    step = S5_BLOCK * (1.0 + jnp.arange(SCAN_ROWS, dtype=F32))
    adec = jnp.exp(lam_dt[None, :, :] * step[:, None, None]).reshape(SCAN_ROWS, n_groups * SSM_STATE)
    return krow, bmat.astype(BF16), cre.astype(BF16), cim.astype(BF16), jnp.real(adec), jnp.imag(adec)


def _first_hit(values, target):
    hits, taken = [], None
    for v in values:
        hit = (v >= target) if taken is None else jnp.logical_and(v >= target, jnp.logical_not(taken))
        taken = hit if taken is None else jnp.logical_or(taken, hit)
        hits.append(hit)
    return hits


def _stack_rows(rows, n_rows):
    idx = lax.broadcasted_iota(I32, (n_rows, rows[0].shape[1]), 0)
    out = jnp.zeros((n_rows, rows[0].shape[1]), F32)
    for k, r in enumerate(rows):
        out = jnp.where(idx == k, r, out)
    return out


def _project(rs, x_ref, ys_lo_ref, ys_hi_ref, yr_ref, wout_ref, h_ref):
    ys = jnp.concatenate([ys_lo_ref[rs, :], ys_hi_ref[rs, :]], axis=1).astype(BF16)
    h = x_ref[rs, :] + _dot(ys, wout_ref[:SSM_WIDTH, :]) + _dot(yr_ref[rs, :], wout_ref[SSM_WIDTH:, :])
    h_ref[rs, :] = h.astype(BF16)
    return h


def _route(lt, tri_ref):
    gl = [lt[g:g + 1, :] for g in range(N_GROUPS)]
    gmax = functools.reduce(jnp.maximum, gl)
    g_w = 1.0 / functools.reduce(lambda a, b: a + b, [jnp.exp(l - gmax) for l in gl])
    sel = _first_hit(gl, gmax)
    ev = []
    for e in range(EXPERTS_PER_GROUP):
        acc = jnp.zeros_like(gmax)
        for g in range(N_GROUPS):
            k = N_GROUPS + g * EXPERTS_PER_GROUP + e
            acc = jnp.where(sel[g], lt[k:k + 1, :], acc)
        ev.append(acc)
    m1 = functools.reduce(jnp.maximum, ev)
    first = _first_hit(ev, m1)
    rest = [jnp.where(f, -jnp.inf, v) for f, v in zip(first, ev)]
    m2 = functools.reduce(jnp.maximum, rest)
    second = _first_hit(rest, m2)
    e2 = jnp.exp(m2 - m1)
    w1 = g_w / (1.0 + e2)
    w2 = e2 * w1
    combine = [jnp.where(f, w1, 0.0) + jnp.where(s, w2, 0.0) for f, s in zip(first, second)]

    sel_f = [jnp.where(s, 1.0, 0.0) for s in sel]
    incl = _dot(_stack_rows(sel_f, 8).astype(BF16), tri_ref[...])
    dest = jnp.zeros_like(gmax)
    seg_start = jnp.zeros((1, 1), F32)
    counts = []
    for g in range(N_GROUPS):
        run = incl[g:g + 1, :]
        cnt = run[:, TILE - 1:TILE]
        counts.append(cnt)
        dest = dest + sel_f[g] * (seg_start + run - 1.0)
        seg_start = seg_start + PIECE * jnp.floor((cnt + (PIECE - 1.0)) * (1.0 / PIECE))
    return combine, dest, counts


def _proj_kernel(x_ref, ys_lo_ref, ys_hi_ref, yr_ref, wout32_ref, g_ref, wr_ref, br_ref,
                 tri_ref, h_ref, stage_ref, dest_ref, cnt_ref, wout_ref):
    @pl.when(pl.program_id(0) == 0)
    def _():
        wout_ref[...] = wout32_ref[...].astype(BF16)

    tiles = range(PROJ_TILES)
    proj_refs = (x_ref, ys_lo_ref, ys_hi_ref, yr_ref, wout_ref, h_ref)
    chunks = [[slice(tile * TILE + b * ROW_CHUNK, tile * TILE + (b + 1) * ROW_CHUNK)
               for b in range(TILE // ROW_CHUNK)] for tile in tiles]
    h_parts = [[_project(rs, *proj_refs) for rs in chunks[tile]] for tile in tiles]

    logits, staged = [], []
    for tile in tiles:
        parts, rows, inv = [], [], []
        for h in h_parts[tile]:
            t = _rms_norm(h, g_ref[...])
            parts.append(t.astype(BF16))
            peak = jnp.maximum(jnp.max(jnp.abs(t), axis=1, keepdims=True), 1e-30)
            rows.append((t * (FP8_HEADROOM / peak)).astype(BF16))
            inv.append(peak * (1.0 / FP8_HEADROOM))
        staged.append((jnp.concatenate(rows, axis=0), jnp.concatenate(inv, axis=0)))
        lt_parts = []
        for t_hi in parts:
            both = _dot_t(wr_ref[...], t_hi)
            lt_parts.append(both[:ROUTE_ROWS] + both[ROUTE_ROWS:])
        logits.append(jnp.concatenate(lt_parts, axis=1) + br_ref[...])

    routed = [_route(logits[tile], tri_ref) for tile in tiles]

    for tile in tiles:
        combine, dest, counts = routed[tile]
        dest_ref[tile] = _stack_rows([dest], 8)
        cnt_ref[tile] = _stack_rows([c + jnp.zeros((1, LANES), F32) for c in counts], 8)
        perm = jnp.where(lax.broadcasted_iota(I32, (SORTED_ROWS, TILE), 0) == dest.astype(I32),
                         1.0, 0.0).astype(BF16)
        c_hi = [c.astype(BF16).astype(F32) for c in combine]
        c_lo = [c - hi for c, hi in zip(combine, c_hi)]
        cw = _stack_rows(c_hi + c_lo, LANES).T
        rows, inv = staged[tile]
        inv_hi = inv.astype(BF16).astype(F32)
        lane = lax.broadcasted_iota(I32, (1, LANES), 1)
        cw = jnp.where(lane == INV_LANE, inv_hi, jnp.where(lane == INV_LANE + 1, inv - inv_hi, cw))
        t_ext = jnp.concatenate([rows, cw.astype(BF16)], axis=1)
        stage_ref[tile * SORTED_ROWS:(tile + 1) * SORTED_ROWS, :] = _dot(perm, t_ext).astype(BF16)


def _proj_call(x, ys_lo, ys_hi, yr, wout, g, wr, br, tri):
    n_tok = x.shape[0]
    n_tiles = n_tok // TILE
    const = lambda *shape: pl.BlockSpec(shape, lambda i: (0,) * len(shape))
    rows = lambda width: pl.BlockSpec((PROJ_TILES * TILE, width), lambda i: (i, 0))
    return pl.pallas_call(
        _proj_kernel,
        grid=(n_tiles // PROJ_TILES,),
        in_specs=[
            rows(D_MODEL), rows(LANES), rows(LANES), rows(RET_WIDTH),
            pl.BlockSpec((D_MODEL, D_MODEL), lambda i: (0, 0), pipeline_mode=pl.Buffered(1)), const(1, D_MODEL),
            const(2 * ROUTE_ROWS, D_MODEL), const(ROUTE_ROWS, 1), const(TILE, TILE),
        ],
        out_specs=(rows(D_MODEL),
                   pl.BlockSpec((PROJ_TILES * SORTED_ROWS, EXT_WIDTH), lambda i: (i, 0)),
                   pl.BlockSpec((PROJ_TILES, 8, TILE), lambda i: (i, 0, 0)),
                   pl.BlockSpec((PROJ_TILES, 8, LANES), lambda i: (i, 0, 0))),
        scratch_shapes=[pltpu.VMEM((D_MODEL, D_MODEL), BF16)],
        out_shape=(jax.ShapeDtypeStruct((n_tok, D_MODEL), BF16),
                   jax.ShapeDtypeStruct((n_tiles * SORTED_ROWS, EXT_WIDTH), BF16),
                   jax.ShapeDtypeStruct((n_tiles, 8, TILE), F32),
                   jax.ShapeDtypeStruct((n_tiles, 8, LANES), F32)),
        compiler_params=pltpu.CompilerParams(
            dimension_semantics=("arbitrary",), vmem_limit_bytes=VMEM_LIMIT),
        name="proj_call",
    )(x, ys_lo, ys_hi, yr, wout, g, wr, br, tri)


def _sort_tables(cnt, n_steps):
    n_tiles = cnt.shape[0]
    npc = (cnt + PIECE - 1) // PIECE
    seg = jnp.cumsum(npc, axis=1) - npc
    before = jnp.cumsum(npc, axis=0) - npc
    n_tile_g = (jnp.sum(npc, axis=0) + MOE_PIECES - 1) // MOE_PIECES
    t_off = jnp.cumsum(n_tile_g) - n_tile_g
    j = jnp.arange(SORTED_PIECES, dtype=I32)[None, :, None]
    in_g = jnp.logical_and(j >= seg[:, None, :], j < (seg + npc)[:, None, :])
    pos = jnp.sum(jnp.where(in_g, MOE_PIECES * t_off[None, None, :] + before[:, None, :] + j - seg[:, None, :], 0),
                  axis=-1)
    valid = jnp.any(in_g, axis=-1)
    stage_piece = jnp.arange(n_tiles * SORTED_PIECES, dtype=I32).reshape(n_tiles, SORTED_PIECES)
    n_slots = n_steps * MOE_PIECES
    src = jnp.zeros((n_slots,), I32).at[jnp.where(valid, pos, n_slots).reshape(-1)].set(
        stage_piece.reshape(-1), mode='drop')
    steps = jnp.arange(n_steps, dtype=I32)
    g_step = jnp.minimum(jnp.sum(steps[:, None] >= (t_off + n_tile_g)[None, :], axis=1), N_GROUPS - 1)
    n_live = jnp.sum(n_tile_g).reshape(1)
    back = jnp.where(valid, pos, 0).reshape(-1)
    return src, g_step.astype(I32), n_live.astype(I32), back.astype(I32)


def _piece_copy(src_ref, piece, buf_ref, slot, j, sem_ref):
    start = piece * PIECE if isinstance(piece, int) else pl.multiple_of(piece * PIECE, PIECE)
    return pltpu.make_async_copy(src_ref.at[pl.ds(start, PIECE)],
                                 buf_ref.at[slot, pl.ds(j * PIECE, PIECE)], sem_ref.at[slot])


def _fetch_pieces(table_ref, src_ref, buf_ref, sem_ref, row, slot, n_pieces):
    for j in range(n_pieces):
        _piece_copy(src_ref, table_ref[row * n_pieces + j], buf_ref, slot, j, sem_ref).start()


def _wait_pieces(src_ref, buf_ref, sem_ref, slot, n_pieces):
    for j in range(n_pieces):
        _piece_copy(src_ref, 0, buf_ref, slot, j, sem_ref).wait()


def _gather_pieces(table_ref, src_ref, buf_ref, sem_ref, n_pieces):
    step = pl.program_id(0)
    last = pl.num_programs(0) - 1
    args = (src_ref, buf_ref, sem_ref)

    @pl.when(step == 0)
    def _():
        _fetch_pieces(table_ref, *args, step, 0, n_pieces)

    _wait_pieces(*args, step % 2, n_pieces)

    def fetch_next():
        _fetch_pieces(table_ref, *args, jnp.minimum(step + 1, last), (step + 1) % 2, n_pieces)

    def drain():
        @pl.when(step == last)
        def _():
            _wait_pieces(*args, (step + 1) % 2, n_pieces)

    return fetch_next, drain


def _to_fp8(x, headroom=FP8_HEADROOM):
    peak = jnp.maximum(jnp.max(jnp.abs(x), axis=(0, 1), keepdims=True), 1e-30)
    return (x * (headroom / peak)).astype(FP8), peak * (1.0 / headroom)


def _moe_kernel(src_ref, gstep_ref, nlive_ref, stage_ref, wg_ref, wu_ref, wd_ref, y_ref,
                buf_ref, sem_ref, wgb_ref, wub_ref, wdb_ref, inv_ref):
    step = pl.program_id(0)
    fetch_next, drain = _gather_pieces(src_ref, stage_ref, buf_ref, sem_ref, MOE_PIECES)

    @pl.when(jnp.logical_or(step == 0, gstep_ref[step] != gstep_ref[jnp.maximum(step - 1, 0)]))
    def _():
        ones = jnp.ones((1, LANES), F32)
        for e in range(EXPERTS_PER_GROUP):
            wgb_ref[e], inv = _to_fp8(wg_ref[e])
            inv_ref[e:e + 1, :] = inv * ones
            wub_ref[e], inv = _to_fp8(wu_ref[e])
            inv_ref[EXPERTS_PER_GROUP + e:EXPERTS_PER_GROUP + e + 1, :] = inv * ones
        peak = functools.reduce(jnp.maximum, [jnp.max(jnp.abs(wd_ref[e]), axis=(0, 1), keepdims=True)
                                              for e in range(EXPERTS_PER_GROUP)])
        peak = jnp.maximum(peak, 1e-30)
        for e in range(EXPERTS_PER_GROUP):
            wdb_ref[e * EXPERT_FF:(e + 1) * EXPERT_FF, :] = (wd_ref[e] * (FP8_HEADROOM / peak)).astype(FP8)
        inv_ref[2 * EXPERTS_PER_GROUP:2 * EXPERTS_PER_GROUP + 1, :] = peak * (1.0 / FP8_HEADROOM) * ones

    @pl.when(step < nlive_ref[0])
    def _():
        slot = step % 2
        fetch_next()

        def up(rs):
            t = buf_ref[slot, rs, :D_MODEL].astype(FP8)
            t_inv = (buf_ref[slot, rs, D_MODEL + INV_LANE:D_MODEL + INV_LANE + 1].astype(F32)
                     + buf_ref[slot, rs, D_MODEL + INV_LANE + 1:D_MODEL + INV_LANE + 2].astype(F32))
            out = []
            for e in range(EXPERTS_PER_GROUP):
                g_inv = inv_ref[e:e + 1, 0:1] * t_inv
                u_inv = inv_ref[EXPERTS_PER_GROUP + e:EXPERTS_PER_GROUP + e + 1, 0:1] * t_inv
                out.append((_dot(t, wgb_ref[e]) * g_inv, _dot(t, wub_ref[e]), u_inv))
            return out

        def down(rs, hidden):
            cw = buf_ref[slot, rs, D_MODEL:].astype(F32)
            acts = []
            for e, (hg, hu, u_inv) in enumerate(hidden):
                c = (cw[:, e:e + 1] + cw[:, EXPERTS_PER_GROUP + e:EXPERTS_PER_GROUP + e + 1]) * u_inv
                acts.append(hg * _sigmoid(hg) * hu * c)
            act, a_inv = _to_fp8(jnp.concatenate(acts, axis=1))
            d_inv = inv_ref[2 * EXPERTS_PER_GROUP:2 * EXPERTS_PER_GROUP + 1, 0:1] * a_inv
            y_ref[rs, :] = (_dot(act, wdb_ref[...]) * d_inv).astype(BF16)

        chunks = [slice(b * ROW_CHUNK, (b + 1) * ROW_CHUNK) for b in range(MOE_TILE // ROW_CHUNK)]
        hidden = [up(rs) for rs in chunks]
        for rs, hid in zip(chunks, hidden):
            down(rs, hid)

    @pl.when(step >= nlive_ref[0])
    def _():
        fetch_next()
        y_ref[...] = jnp.zeros_like(y_ref)

    drain()


def _moe_call(src, g_step, n_live, stage, wg, wu, wd, n_steps):
    grp = lambda shape: pl.BlockSpec((EXPERTS_PER_GROUP,) + shape, lambda s, src, gs, nl: (gs[s], 0, 0))
    return pl.pallas_call(
        _moe_kernel,
        grid_spec=pltpu.PrefetchScalarGridSpec(
            num_scalar_prefetch=3,
            grid=(n_steps,),
            in_specs=[pl.BlockSpec(memory_space=pl.ANY),
                      grp((D_MODEL, EXPERT_FF)), grp((D_MODEL, EXPERT_FF)), grp((EXPERT_FF, D_MODEL))],
            out_specs=pl.BlockSpec((MOE_TILE, D_MODEL), lambda s, src, gs, nl: (s, 0)),
            scratch_shapes=[pltpu.VMEM((2, MOE_TILE, EXT_WIDTH), BF16), pltpu.SemaphoreType.DMA((2,)),
                            pltpu.VMEM((EXPERTS_PER_GROUP, D_MODEL, EXPERT_FF), FP8),
                            pltpu.VMEM((EXPERTS_PER_GROUP, D_MODEL, EXPERT_FF), FP8),
                            pltpu.VMEM((EXPERTS_PER_GROUP * EXPERT_FF, D_MODEL), FP8),
                            pltpu.VMEM((16, LANES), F32)],
        ),
        out_shape=jax.ShapeDtypeStruct((n_steps * MOE_TILE, D_MODEL), BF16),
        compiler_params=pltpu.CompilerParams(
            dimension_semantics=("arbitrary",), vmem_limit_bytes=VMEM_LIMIT),
        name="moe_call",
    )(src, g_step, n_live, stage, wg, wu, wd)


def _final_kernel(back_ref, ysort_ref, h_ref, dest_ref, g_ref, o_ref, buf_ref, sem_ref):
    step = pl.program_id(0)
    fetch_next, drain = _gather_pieces(back_ref, ysort_ref, buf_ref, sem_ref, FINAL_TILES * SORTED_PIECES)
    fetch_next()
    for t in range(FINAL_TILES):
        dest = _stack_rows([dest_ref[t, 0:1, :]], LANES).T[:, 0:1].astype(I32)
        unperm = jnp.where(lax.broadcasted_iota(I32, (TILE, SORTED_ROWS), 1) == dest, 1.0, 0.0).astype(BF16)
        rows = buf_ref[step % 2, t * SORTED_ROWS:(t + 1) * SORTED_ROWS, :]
        for b in range(TILE // ROW_CHUNK):
            rs = slice(b * ROW_CHUNK, (b + 1) * ROW_CHUNK)
            os = slice(t * TILE + b * ROW_CHUNK, t * TILE + (b + 1) * ROW_CHUNK)
            o_ref[os, :] = _rms_norm(h_ref[os, :].astype(F32) + _dot(unperm[rs], rows), g_ref[...])
    drain()


def _final_call(back, ysort, h, dest, g):
    n_tok = h.shape[0]
    return pl.pallas_call(
        _final_kernel,
        grid_spec=pltpu.PrefetchScalarGridSpec(
            num_scalar_prefetch=1,
            grid=(n_tok // (FINAL_TILES * TILE),),
            in_specs=[pl.BlockSpec(memory_space=pl.ANY),
                      pl.BlockSpec((FINAL_TILES * TILE, D_MODEL), lambda i, back: (i, 0)),
                      pl.BlockSpec((FINAL_TILES, 8, TILE), lambda i, back: (i, 0, 0)),
                      pl.BlockSpec((1, D_MODEL), lambda i, back: (0, 0))],
            out_specs=pl.BlockSpec((FINAL_TILES * TILE, D_MODEL), lambda i, back: (i, 0)),
            scratch_shapes=[pltpu.VMEM((2, FINAL_TILES * SORTED_ROWS, D_MODEL), BF16),
                            pltpu.SemaphoreType.DMA((2,))],
        ),
        out_shape=jax.ShapeDtypeStruct((n_tok, D_MODEL), F32),
        compiler_params=pltpu.CompilerParams(
            dimension_semantics=("arbitrary",), vmem_limit_bytes=VMEM_LIMIT),
        name="final_call",
    )(back, ysort, h, dest, g)


def _rope_tables(length):
    pos = np.arange(length, dtype=np.float32)
    inv_freq = np.float32(ROPE_BASE) ** (-np.arange(0, HEAD_DIM, 2, dtype=np.float32) / np.float32(HEAD_DIM))
    ang = pos[:, None] * inv_freq[None, :]
    cos, sin = np.cos(ang), np.sin(ang)
    return np.concatenate([cos, cos], axis=-1), np.concatenate([-sin, sin], axis=-1)


def _retention_tables():
    f32 = np.float32
    gamma = f32(1.0) - f32(2.0) ** (f32(-5.0) - np.arange(HEADS, dtype=f32))
    log_g = np.log(gamma)[:, None, None]
    scale = f32(HEAD_DIM ** -0.5)
    idx = np.arange(RET_BLOCK)
    dist = np.abs(idx[:, None] - idx[None, :]).astype(f32)
    visible = (idx[None, :] // CHUNK) <= (idx[:, None] // CHUNK)
    mask = np.where(visible[None], np.exp(log_g * dist[None]), f32(0.0)) * scale
    ones = np.ones((1, 1, HEAD_DIM), f32)
    idx_f = idx.astype(f32)[None, :, None]
    pair = lambda a: a.reshape(HEAD_PAIRS, 2, a.shape[1], HEAD_DIM).transpose(0, 2, 1, 3).reshape(
        HEAD_PAIRS, a.shape[1], PAIR_DIM)
    qdec = pair(np.exp(log_g * (idx_f + f32(1.0))) * ones)
    kdec = pair(np.exp(log_g * (f32(RET_BLOCK - 1.0) - idx_f)) * scale * ones)
    meta_idx = np.arange(N_META, dtype=f32)[None, :, None]
    kdec_meta = pair(np.exp(log_g * (f32(N_META - 1.0) - meta_idx)) * scale * ones)
    blk = np.kron(np.eye(2, dtype=f32), np.ones((HEAD_DIM, HEAD_DIM), f32))
    bdec = np.exp(log_g * f32(RET_BLOCK)).reshape(HEAD_PAIRS, 2)
    bdec = np.stack([np.kron(np.diag(b), np.ones((HEAD_DIM, HEAD_DIM), f32)) for b in bdec])
    return tuple(a.astype(f32) for a in (mask, qdec, kdec, bdec, blk, kdec_meta))


def kernel(x, meta_tokens, norm_mix_g, w_in, ssm_lambda_re, ssm_lambda_im, ssm_log_dt, ssm_b_re, ssm_b_im, ssm_c_re, ssm_c_im, ssm_d, w_glu, w_out, norm_ffn_g, w_router_group, b_router_group, w_router_expert, b_router_expert, w_gate, w_up, w_down, norm_final_g):
    bsz, seq, _ = x.shape
    assert seq % TILE == 0 and seq % MIXER_TILE == 0 and MIXER_TILE % RET_BLOCK == 0 and RET_BLOCK % CHUNK == 0
    n_blocks = seq // S5_BLOCK
    assert n_blocks % SCAN_ROWS == 0
    n_tok = bsz * seq
    assert n_tok % (FINAL_TILES * TILE) == 0 and n_tok % (PROJ_TILES * TILE) == 0
    n_tiles = n_tok // TILE
    n_steps = -(-n_tiles * (TILE_PIECES + N_GROUPS - 1) // MOE_PIECES) + N_GROUPS

    cos, sin = _rope_tables(N_META + seq)
    mask, qdec, kdec, bdec, blk, kdec_meta = _retention_tables()
    g_mix = norm_mix_g[0][None, :]
    u_lo, u_hi, y_ret, u_meta = _mixer_call(x, g_mix, w_in[0], cos[N_META:], sin[N_META:], mask, qdec, kdec, bdec, blk,
                                   meta_tokens, cos[:N_META], sin[:N_META], kdec_meta)

    s5_ops = _s5_operators(
        ssm_lambda_re[0], ssm_lambda_im[0], ssm_log_dt[0], ssm_b_re[0], ssm_b_im[0],
        ssm_c_re[0], ssm_c_im[0], ssm_d[0])
    um = u_meta.reshape(S5_BLOCK, SSM_GROUPS, SSM_GROUP).transpose(1, 0, 2).reshape(SSM_GROUPS, 1, S5_LANES)
    um = jnp.pad(um, ((0, 0), (0, 7), (0, 0)))
    y_lo, y_hi = _s5_call(u_lo, u_hi, um, *s5_ops, w_glu[0].astype(BF16))

    w_r = jnp.concatenate(
        [w_router_group[0].T, w_router_expert[0].transpose(0, 2, 1).reshape(N_EXPERTS, D_MODEL)], axis=0)
    w_r = jnp.pad(w_r, ((0, ROUTE_ROWS - w_r.shape[0]), (0, 0)))
    b_r = jnp.concatenate([b_router_group[0], b_router_expert[0].reshape(-1)])
    b_r = jnp.pad(b_r, (0, ROUTE_ROWS - b_r.shape[0]))[:, None]
    w_r_hi = w_r.astype(BF16)
    w_r = jnp.concatenate([w_r_hi, (w_r - w_r_hi.astype(F32)).astype(BF16)], axis=0)
    tri = jnp.asarray(np.arange(TILE)[:, None] <= np.arange(TILE)[None, :], BF16)

    h, stage, dest, cnt = _proj_call(
        x.reshape(n_tok, D_MODEL), y_lo.reshape(n_tok, LANES), y_hi.reshape(n_tok, LANES),
        y_ret.reshape(n_tok, RET_WIDTH),
        w_out[0], norm_ffn_g[0][None, :], w_r, b_r, tri)
    src, g_step, n_live, back = _sort_tables(cnt[:, :N_GROUPS, 0].astype(I32), n_steps)
    y_sorted = _moe_call(src, g_step, n_live, stage, w_gate[0], w_up[0], w_down[0], n_steps)
    out = _final_call(back, y_sorted, h, dest, norm_final_g[None, :])
    return out.reshape(bsz, seq, D_MODEL)
```

```python
import functools

import jax
import jax.numpy as jnp
import numpy as np
from jax import lax
from jax.experimental import pallas as pl
from jax.experimental.pallas import tpu as pltpu

D_MODEL = 1024
N_META = 16
CHUNK = 64
EPS = 1e-6
SSM_WIDTH = 256
SSM_GROUP = 16
SSM_GROUPS = 16
SSM_STATE = 64
RET_WIDTH = 768
HEAD_DIM = 128
HEADS = 6
HEAD_PAIRS = HEADS // 2
PAIR_DIM = 2 * HEAD_DIM
ROPE_BASE = 10000.0
IN_WIDTH = SSM_WIDTH + 4 * RET_WIDTH
N_GROUPS = 4
EXPERTS_PER_GROUP = 4
N_EXPERTS = 16
EXPERT_FF = 256

S5_BLOCK = 16
S5_LANES = S5_BLOCK * SSM_GROUP
SCAN_ROWS = 8
RET_BLOCK = 256
TILE = 512
MIXER_TILE = 1024
LANES = 128
ROUTE_ROWS = 32
ROW_CHUNK = 256
PROJ_TILES = 2
FINAL_TILES = 4
PIECE = 16
TILE_PIECES = TILE // PIECE
MOE_TILE = 1024
MOE_PIECES = MOE_TILE // PIECE
SORTED_PIECES = TILE_PIECES + N_GROUPS
SORTED_ROWS = SORTED_PIECES * PIECE
EXT_WIDTH = D_MODEL + LANES
INV_LANE = 2 * EXPERTS_PER_GROUP
V7X_VMEM_BYTES = 64 * 1024 * 1024
VMEM_LIMIT = V7X_VMEM_BYTES * 7 // 8

F32 = jnp.float32
BF16 = jnp.bfloat16
FP8 = jnp.float8_e4m3fn
FP8_HEADROOM = 256.0
I32 = jnp.int32


def _dot(a, b):
    return jnp.dot(a, b, preferred_element_type=F32)


def _sigmoid(x):
    return 1.0 / (1.0 + jnp.exp(-x))


def _rms_norm(x, g):
    return x * lax.rsqrt(jnp.mean(x * x, axis=-1, keepdims=True) + EPS) * g


def _rope(t, cos, sin_signed):
    return t * cos + pltpu.roll(t, HEAD_DIM // 2, axis=1) * sin_signed


def _meta_state(meta_ref, g_ref, w_ref, cos_ref, sin_ref, kdec_ref, blk_ref, u_ref, r0_ref):
    a = _rms_norm(meta_ref[...], g_ref[...]).astype(BF16)
    u_ref[...] = _dot(a, w_ref[:, :SSM_WIDTH])
    k_off = SSM_WIDTH + RET_WIDTH
    v_off = SSM_WIDTH + 2 * RET_WIDTH
    cos = cos_ref[...]
    sin = sin_ref[...]
    for p in range(HEAD_PAIRS):
        k = _dot(a, w_ref[:, k_off + p * PAIR_DIM:k_off + (p + 1) * PAIR_DIM])
        v = _dot(a, w_ref[:, v_off + p * PAIR_DIM:v_off + (p + 1) * PAIR_DIM])
        k = jnp.concatenate([_rope(k[:, :HEAD_DIM], cos, sin), _rope(k[:, HEAD_DIM:], cos, sin)], axis=1)
        kd = (k * kdec_ref[p]).astype(BF16)
        r0_ref[p] = _dot_rows(kd, v.astype(BF16)) * blk_ref[...]


def _dot_rows(a, b):
    return lax.dot_general(a, b, (((0,), (0,)), ((), ())), preferred_element_type=F32)


def _mixer_kernel(x_ref, g_ref, w32_ref, cos_ref, sin_ref, mask_ref, qdec_ref, kdec_ref, bdec_ref, blk_ref,
                  meta_ref, cos_m_ref, sin_m_ref, kdec_m_ref, u_ref, y_ref, um_ref, w_ref, r0_ref, r_ref):
    first_tile = pl.program_id(1) == 0

    @pl.when(jnp.logical_and(pl.program_id(0) == 0, first_tile))
    def _():
        for c in range(0, IN_WIDTH, SSM_WIDTH):
            w_ref[:, c:c + SSM_WIDTH] = w32_ref[:, c:c + SSM_WIDTH].astype(BF16)
        _meta_state(meta_ref, g_ref, w_ref, cos_m_ref, sin_m_ref, kdec_m_ref, blk_ref, um_ref, r0_ref)

    @pl.when(first_tile)
    def _():
        r_ref[...] = r0_ref[...]

    off = SSM_WIDTH
    for b in range(MIXER_TILE // RET_BLOCK):
        bs = slice(b * RET_BLOCK, (b + 1) * RET_BLOCK)
        a = _rms_norm(x_ref[bs, :], g_ref[...]).astype(BF16)
        u_ref[bs, :] = _dot(a, w_ref[:, :SSM_WIDTH])
        q_all = _dot(a, w_ref[:, off:off + RET_WIDTH])
        k_all = _dot(a, w_ref[:, off + RET_WIDTH:off + 2 * RET_WIDTH])
        v_all = _dot(a, w_ref[:, off + 2 * RET_WIDTH:off + 3 * RET_WIDTH])
        gate = _dot(a, w_ref[:, off + 3 * RET_WIDTH:off + 4 * RET_WIDTH])
        cos = cos_ref[bs, :]
        sin = sin_ref[bs, :]
        def scores(h):
            hs = slice(h * HEAD_DIM, (h + 1) * HEAD_DIM)
            q = _rope(q_all[:, hs], cos, sin)
            k = _rope(k_all[:, hs], cos, sin)
            return q, k, _dot_t(q.astype(BF16), k.astype(BF16))

        ahead = scores(0)
        for p in range(HEAD_PAIRS):
            ps = slice(p * PAIR_DIM, (p + 1) * PAIR_DIM)
            pair = [ahead, scores(2 * p + 1)]
            if p + 1 < HEAD_PAIRS:
                ahead = scores(2 * p + 2)
            q = jnp.concatenate([pair[0][0], pair[1][0]], axis=1)
            k = jnp.concatenate([pair[0][1], pair[1][1]], axis=1)
            v = v_all[:, ps].astype(BF16)
            state = r_ref[p]
            cross = _dot((q * qdec_ref[p]).astype(BF16), state.astype(BF16))
            kv = _dot_rows((k * kdec_ref[p]).astype(BF16), v)
            r_ref[p] = state * bdec_ref[p] + kv * blk_ref[...]
            for half, (_, _, s) in enumerate(pair):
                h = 2 * p + half
                hs = slice(h * HEAD_DIM, (h + 1) * HEAD_DIM)
                ls = slice(half * HEAD_DIM, (half + 1) * HEAD_DIM)
                o = _dot((s * mask_ref[h]).astype(BF16), v[:, ls]) + cross[:, ls]
                mu = jnp.mean(o, axis=-1, keepdims=True)
                d = o - mu
                var = jnp.mean(d * d, axis=-1, keepdims=True)
                gt = gate[:, hs]
                y_ref[bs, hs] = (gt * _sigmoid(gt) * d * lax.rsqrt(var + EPS)).astype(BF16)


def _mixer_call(x, g, w_in, cos, sin, mask, qdec, kdec, bdec, blk, meta, cos_m, sin_m, kdec_m):
    bsz, seq, _ = x.shape
    const = lambda a, **kw: pl.BlockSpec(a.shape, lambda b, i: (0,) * a.ndim, **kw)
    return pl.pallas_call(
        _mixer_kernel,
        grid=(bsz, seq // MIXER_TILE),
        in_specs=[
            pl.BlockSpec((None, MIXER_TILE, D_MODEL), lambda b, i: (b, i, 0)),
            const(g),
            const(w_in, pipeline_mode=pl.Buffered(1)),
            pl.BlockSpec((MIXER_TILE, HEAD_DIM), lambda b, i: (i, 0)),
            pl.BlockSpec((MIXER_TILE, HEAD_DIM), lambda b, i: (i, 0)),
            const(mask), const(qdec), const(kdec), const(bdec), const(blk),
            const(meta), const(cos_m), const(sin_m), const(kdec_m),
        ],
        out_specs=(
            pl.BlockSpec((None, MIXER_TILE, SSM_WIDTH), lambda b, i: (b, i, 0)),
            pl.BlockSpec((None, MIXER_TILE, RET_WIDTH), lambda b, i: (b, i, 0)),
            pl.BlockSpec((N_META, SSM_WIDTH), lambda b, i: (0, 0)),
        ),
        out_shape=(jax.ShapeDtypeStruct((bsz, seq, SSM_WIDTH), F32),
                   jax.ShapeDtypeStruct((bsz, seq, RET_WIDTH), BF16),
                   jax.ShapeDtypeStruct((N_META, SSM_WIDTH), F32)),
        scratch_shapes=[pltpu.VMEM((D_MODEL, IN_WIDTH), BF16),
                        pltpu.VMEM((HEAD_PAIRS, PAIR_DIM, PAIR_DIM), F32),
                        pltpu.VMEM((HEAD_PAIRS, PAIR_DIM, PAIR_DIM), F32)],
        compiler_params=pltpu.CompilerParams(
            dimension_semantics=("arbitrary", "arbitrary"), vmem_limit_bytes=VMEM_LIMIT),
        name="mixer_call",
    )(x, g, w_in, cos, sin, mask, qdec, kdec, bdec, blk, meta, cos_m, sin_m, kdec_m)


def _dot_t(a, b):
    return lax.dot_general(a, b, (((1,), (1,)), ((), ())), preferred_element_type=F32)


def _chunk_transpose(arrs):
    n = len(arrs)
    chunk = lax.broadcasted_iota(I32, (1, LANES), 1) // SSM_GROUP
    arrs = list(arrs)
    s = n // 2
    while s:
        keep = (chunk & s) == 0
        nxt = list(arrs)
        for i in range(n):
            if i & s == 0:
                lo, hi = arrs[i], arrs[i + s]
                nxt[i] = jnp.where(keep, lo, pltpu.roll(hi, s * SSM_GROUP, axis=1))
                nxt[i + s] = jnp.where(keep, pltpu.roll(lo, LANES - s * SSM_GROUP, axis=1), hi)
        arrs = nxt
        s //= 2
    return arrs


def _s5_kernel(u_lo_ref, u_hi_ref, um_ref, krow_ref, bmat_ref, cre_ref, cim_ref, ar_ref, ai_ref, wglu_ref,
               y_lo_ref, y_hi_ref, t0_ref, ug_ref, yg_ref):
    n_blocks = u_lo_ref.shape[0] // S5_BLOCK

    @pl.when(pl.program_id(0) == 0)
    def _():
        lane = lax.broadcasted_iota(I32, (SSM_GROUP, S5_LANES), 1)
        for g in range(SSM_GROUPS):
            k0 = krow_ref[g]
            for a in range(S5_BLOCK):
                blk = k0 if a == 0 else jnp.where(lane >= a * SSM_GROUP,
                                                  pltpu.roll(k0, a * SSM_GROUP, axis=1), 0.0)
                t0_ref[g, a * SSM_GROUP:(a + 1) * SSM_GROUP, :] = blk.astype(BF16)

    per_tile = LANES // SSM_GROUP
    for half, uh_ref in enumerate((u_lo_ref, u_hi_ref)):
        for t in range(S5_BLOCK // per_tile):
            words = [pltpu.bitcast(uh_ref[pl.ds(t * per_tile + k, n_blocks, stride=S5_BLOCK), :].astype(BF16),
                                   jnp.uint32) for k in range(per_tile)]
            for m, w in enumerate(_chunk_transpose(words)):
                ug_ref[half * per_tile + m, :, t * LANES:(t + 1) * LANES] = pltpu.bitcast(w, BF16)

    re, im, re0, im0 = [], [], [], []
    for p in range(SSM_GROUPS // 2):
        g0, g1 = 2 * p, 2 * p + 1
        v = _dot(ug_ref[g0], bmat_ref[g0]) + _dot(ug_ref[g1], bmat_ref[g1])
        v0 = (_dot(um_ref[g0].astype(BF16), bmat_ref[g0]) + _dot(um_ref[g1].astype(BF16), bmat_ref[g1]))[0:1]
        re.append(v[:, :LANES])
        im.append(v[:, LANES:])
        re0.append(v0[:, :LANES])
        im0.append(v0[:, LANES:])
    re, im, re0, im0 = (jnp.concatenate(parts, axis=1) for parts in (re, im, re0, im0))

    row = lax.broadcasted_iota(I32, re.shape, 0)
    ar, ai = ar_ref[0:1], ai_ref[0:1]
    re = re + jnp.where(row == 0, ar * re0 - ai * im0, 0.0)
    im = im + jnp.where(row == 0, ar * im0 + ai * re0, 0.0)
    d = 1
    while d < SCAN_ROWS:
        ar, ai = ar_ref[d - 1:d], ai_ref[d - 1:d]
        inside = row % SCAN_ROWS >= d
        sre = jnp.where(inside, pltpu.roll(re, d, axis=0), 0.0)
        sim = jnp.where(inside, pltpu.roll(im, d, axis=0), 0.0)
        re, im = re + ar * sre - ai * sim, im + ar * sim + ai * sre
        d *= 2
    ar, ai = ar_ref[...], ai_ref[...]
    re_tiles, im_tiles = [re[:SCAN_ROWS]], [im[:SCAN_ROWS]]
    for r in range(1, n_blocks // SCAN_ROWS):
        cre_ = re_tiles[-1][SCAN_ROWS - 1:SCAN_ROWS]
        cim_ = im_tiles[-1][SCAN_ROWS - 1:SCAN_ROWS]
        rs = slice(r * SCAN_ROWS, (r + 1) * SCAN_ROWS)
        re_tiles.append(re[rs] + ar * cre_ - ai * cim_)
        im_tiles.append(im[rs] + ar * cim_ + ai * cre_)
    re = jnp.concatenate(re_tiles, axis=0)
    im = jnp.concatenate(im_tiles, axis=0)
    pre = jnp.where(row == 0, re0, pltpu.roll(re, 1, axis=0)).astype(BF16)
    pim = jnp.where(row == 0, im0, pltpu.roll(im, 1, axis=0)).astype(BF16)

    for g in range(SSM_GROUPS):
        ps = slice((g // 2) * LANES, (g // 2 + 1) * LANES)
        yg_ref[g] = (_dot(ug_ref[g], t0_ref[g]) + _dot_t(pre[:, ps], cre_ref[g])
                     + _dot_t(pim[:, ps], cim_ref[g])).astype(BF16)

    wglu = wglu_ref[...]
    for t in range(S5_BLOCK // per_tile):
        ts = slice(t * LANES, (t + 1) * LANES)
        halves = [_chunk_transpose([pltpu.bitcast(yg_ref[half * per_tile + m, :, ts], jnp.uint32)
                                    for m in range(per_tile)])
                  for half in range(SSM_GROUPS // per_tile)]
        for k in range(per_tile):
            y = jnp.concatenate([pltpu.bitcast(h[k], BF16) for h in halves], axis=1).astype(F32)
            y = jax.nn.gelu(y, approximate=True)
            y = y * _sigmoid(_dot(y.astype(BF16), wglu))
            i = t * per_tile + k
            y_lo_ref[pl.ds(i, n_blocks, stride=S5_BLOCK), :] = y[:, :LANES]
            y_hi_ref[pl.ds(i, n_blocks, stride=S5_BLOCK), :] = y[:, LANES:]


def _s5_call(u, um, krow, bmat, cre, cim, ar, ai, wglu):
    bsz, seq, _ = u.shape
    n_blocks = seq // S5_BLOCK
    const = lambda a: pl.BlockSpec(a.shape, lambda b: (0,) * a.ndim)
    return pl.pallas_call(
        _s5_kernel,
        grid=(bsz,),
        in_specs=[pl.BlockSpec((None, seq, LANES), lambda b: (b, 0, 0)),
                  pl.BlockSpec((None, seq, LANES), lambda b: (b, 0, 1)),
                  const(um), const(krow), const(bmat), const(cre), const(cim), const(ar), const(ai),
                  const(wglu)],
        out_specs=(pl.BlockSpec((None, seq, LANES), lambda b: (b, 0, 0)),
                   pl.BlockSpec((None, seq, LANES), lambda b: (b, 0, 0))),
        out_shape=(jax.ShapeDtypeStruct((bsz, seq, LANES), F32),
                   jax.ShapeDtypeStruct((bsz, seq, LANES), F32)),
        scratch_shapes=[pltpu.VMEM((SSM_GROUPS, S5_LANES, S5_LANES), BF16),
                        pltpu.VMEM((SSM_GROUPS, n_blocks, S5_LANES), BF16),
                        pltpu.VMEM((SSM_GROUPS, n_blocks, S5_LANES), BF16)],
        compiler_params=pltpu.CompilerParams(
            dimension_semantics=("arbitrary",), vmem_limit_bytes=VMEM_LIMIT),
        name="s5_call",
    )(u, u, um, krow, bmat, cre, cim, ar, ai, wglu)


def _s5_operators(lam_re, lam_im, log_dt, b_re, b_im, c_re, c_im, d_skip):
    n_groups = lam_re.shape[0]
    lam = lax.complex(lam_re, lam_im)
    lam_dt = lam * jnp.exp(log_dt)[:, None]
    lam_bar = jnp.exp(lam_dt)
    b_bar = ((lam_bar - 1.0) / lam)[..., None] * lax.complex(b_re, b_im)
    c = lax.complex(c_re, c_im)
    tau = jnp.arange(S5_BLOCK + 1, dtype=F32)
    pows = jnp.exp(lam_dt[None] * tau[:, None, None])
    kern = jnp.real(jnp.einsum('ghp,tgp,gpk->gkth', c, pows[:S5_BLOCK], b_bar))
    skip = (jnp.eye(SSM_GROUP, dtype=F32)[None, :, None, :] * d_skip[:, None, None, :]
            * (tau[:S5_BLOCK] == 0).astype(F32)[None, None, :, None])
    krow = (kern + skip).reshape(n_groups, SSM_GROUP, S5_LANES)
    even = (jnp.arange(n_groups) % 2 == 0)[:, None, None]
    pair_pad = lambda m: jnp.concatenate([jnp.where(even, m, 0.0), jnp.where(even, 0.0, m)], axis=-1)
    bm = (pows[S5_BLOCK - 1 - jnp.arange(S5_BLOCK)].transpose(1, 0, 2)[:, :, None, :]
          * b_bar.transpose(0, 2, 1)[:, None, :, :]).reshape(n_groups, S5_LANES, SSM_STATE)
    bmat = jnp.concatenate([pair_pad(jnp.real(bm)), pair_pad(jnp.imag(bm))], axis=-1)
    cm = (pows[1:].transpose(1, 0, 2)[:, :, None, :] * c[:, None, :, :]).reshape(n_groups, S5_LANES, SSM_STATE)
    cre, cim = pair_pad(jnp.real(cm)), pair_pad(-jnp.imag(cm))
    step = S5_BLOCK * (1.0 + jnp.arange(SCAN_ROWS, dtype=F32))
    adec = jnp.exp(lam_dt[None, :, :] * step[:, None, None]).reshape(SCAN_ROWS, n_groups * SSM_STATE)
    return krow, bmat.astype(BF16), cre.astype(BF16), cim.astype(BF16), jnp.real(adec), jnp.imag(adec)


def _first_hit(values, target):
    hits, taken = [], None
    for v in values:
        hit = (v >= target) if taken is None else jnp.logical_and(v >= target, jnp.logical_not(taken))
        taken = hit if taken is None else jnp.logical_or(taken, hit)
        hits.append(hit)
    return hits


def _stack_rows(rows, n_rows):
    idx = lax.broadcasted_iota(I32, (n_rows, rows[0].shape[1]), 0)
    out = jnp.zeros((n_rows, rows[0].shape[1]), F32)
    for k, r in enumerate(rows):
        out = jnp.where(idx == k, r, out)
    return out


def _project(rs, x_ref, ys_lo_ref, ys_hi_ref, yr_ref, wout_ref, h_ref):
    ys = jnp.concatenate([ys_lo_ref[rs, :], ys_hi_ref[rs, :]], axis=1).astype(BF16)
    h = x_ref[rs, :] + _dot(ys, wout_ref[:SSM_WIDTH, :]) + _dot(yr_ref[rs, :], wout_ref[SSM_WIDTH:, :])
    h_ref[rs, :] = h.astype(BF16)
    return h


def _route(lt, tri_ref):
    gl = [lt[g:g + 1, :] for g in range(N_GROUPS)]
    gmax = functools.reduce(jnp.maximum, gl)
    g_w = 1.0 / functools.reduce(lambda a, b: a + b, [jnp.exp(l - gmax) for l in gl])
    sel = _first_hit(gl, gmax)
    ev = []
    for e in range(EXPERTS_PER_GROUP):
        acc = jnp.zeros_like(gmax)
        for g in range(N_GROUPS):
            k = N_GROUPS + g * EXPERTS_PER_GROUP + e
            acc = jnp.where(sel[g], lt[k:k + 1, :], acc)
        ev.append(acc)
    m1 = functools.reduce(jnp.maximum, ev)
    first = _first_hit(ev, m1)
    rest = [jnp.where(f, -jnp.inf, v) for f, v in zip(first, ev)]
    m2 = functools.reduce(jnp.maximum, rest)
    second = _first_hit(rest, m2)
    e2 = jnp.exp(m2 - m1)
    w1 = g_w / (1.0 + e2)
    w2 = e2 * w1
    combine = [jnp.where(f, w1, 0.0) + jnp.where(s, w2, 0.0) for f, s in zip(first, second)]

    sel_f = [jnp.where(s, 1.0, 0.0) for s in sel]
    incl = _dot(_stack_rows(sel_f, 8).astype(BF16), tri_ref[...])
    dest = jnp.zeros_like(gmax)
    seg_start = jnp.zeros((1, 1), F32)
    counts = []
    for g in range(N_GROUPS):
        run = incl[g:g + 1, :]
        cnt = run[:, TILE - 1:TILE]
        counts.append(cnt)
        dest = dest + sel_f[g] * (seg_start + run - 1.0)
        seg_start = seg_start + PIECE * jnp.floor((cnt + (PIECE - 1.0)) * (1.0 / PIECE))
    return combine, dest, counts


def _proj_kernel(x_ref, ys_lo_ref, ys_hi_ref, yr_ref, wout32_ref, g_ref, wr_ref, br_ref,
                 tri_ref, h_ref, stage_ref, dest_ref, cnt_ref, wout_ref):
    @pl.when(pl.program_id(0) == 0)
    def _():
        wout_ref[...] = wout32_ref[...].astype(BF16)

    tiles = range(PROJ_TILES)
    proj_refs = (x_ref, ys_lo_ref, ys_hi_ref, yr_ref, wout_ref, h_ref)
    chunks = [[slice(tile * TILE + b * ROW_CHUNK, tile * TILE + (b + 1) * ROW_CHUNK)
               for b in range(TILE // ROW_CHUNK)] for tile in tiles]
    h_parts = [[_project(rs, *proj_refs) for rs in chunks[tile]] for tile in tiles]

    logits, staged = [], []
    for tile in tiles:
        parts, rows, inv = [], [], []
        for h in h_parts[tile]:
            t = _rms_norm(h, g_ref[...])
            parts.append(t.astype(BF16))
            peak = jnp.maximum(jnp.max(jnp.abs(t), axis=1, keepdims=True), 1e-30)
            rows.append((t * (FP8_HEADROOM / peak)).astype(BF16))
            inv.append(peak * (1.0 / FP8_HEADROOM))
        staged.append((jnp.concatenate(rows, axis=0), jnp.concatenate(inv, axis=0)))
        lt_parts = []
        for t_hi in parts:
            both = _dot_t(wr_ref[...], t_hi)
            lt_parts.append(both[:ROUTE_ROWS] + both[ROUTE_ROWS:])
        logits.append(jnp.concatenate(lt_parts, axis=1) + br_ref[...])

    routed = [_route(logits[tile], tri_ref) for tile in tiles]

    for tile in tiles:
        combine, dest, counts = routed[tile]
        dest_ref[tile] = _stack_rows([dest], 8)
        cnt_ref[tile] = _stack_rows([c + jnp.zeros((1, LANES), F32) for c in counts], 8)
        perm = jnp.where(lax.broadcasted_iota(I32, (SORTED_ROWS, TILE), 0) == dest.astype(I32),
                         1.0, 0.0).astype(BF16)
        c_hi = [c.astype(BF16).astype(F32) for c in combine]
        c_lo = [c - hi for c, hi in zip(combine, c_hi)]
        cw = _stack_rows(c_hi + c_lo, LANES).T
        rows, inv = staged[tile]
        inv_hi = inv.astype(BF16).astype(F32)
        lane = lax.broadcasted_iota(I32, (1, LANES), 1)
        cw = jnp.where(lane == INV_LANE, inv_hi, jnp.where(lane == INV_LANE + 1, inv - inv_hi, cw))
        t_ext = jnp.concatenate([rows, cw.astype(BF16)], axis=1)
        stage_ref[tile * SORTED_ROWS:(tile + 1) * SORTED_ROWS, :] = _dot(perm, t_ext).astype(BF16)


def _proj_call(x, ys_lo, ys_hi, yr, wout, g, wr, br, tri):
    n_tok = x.shape[0]
    n_tiles = n_tok // TILE
    const = lambda *shape: pl.BlockSpec(shape, lambda i: (0,) * len(shape))
    rows = lambda width: pl.BlockSpec((PROJ_TILES * TILE, width), lambda i: (i, 0))
    return pl.pallas_call(
        _proj_kernel,
        grid=(n_tiles // PROJ_TILES,),
        in_specs=[
            rows(D_MODEL), rows(LANES), rows(LANES), rows(RET_WIDTH),
            pl.BlockSpec((D_MODEL, D_MODEL), lambda i: (0, 0), pipeline_mode=pl.Buffered(1)), const(1, D_MODEL),
            const(2 * ROUTE_ROWS, D_MODEL), const(ROUTE_ROWS, 1), const(TILE, TILE),
        ],
        out_specs=(rows(D_MODEL),
                   pl.BlockSpec((PROJ_TILES * SORTED_ROWS, EXT_WIDTH), lambda i: (i, 0)),
                   pl.BlockSpec((PROJ_TILES, 8, TILE), lambda i: (i, 0, 0)),
                   pl.BlockSpec((PROJ_TILES, 8, LANES), lambda i: (i, 0, 0))),
        scratch_shapes=[pltpu.VMEM((D_MODEL, D_MODEL), BF16)],
        out_shape=(jax.ShapeDtypeStruct((n_tok, D_MODEL), BF16),
                   jax.ShapeDtypeStruct((n_tiles * SORTED_ROWS, EXT_WIDTH), BF16),
                   jax.ShapeDtypeStruct((n_tiles, 8, TILE), F32),
                   jax.ShapeDtypeStruct((n_tiles, 8, LANES), F32)),
        compiler_params=pltpu.CompilerParams(
            dimension_semantics=("arbitrary",), vmem_limit_bytes=VMEM_LIMIT),
        name="proj_call",
    )(x, ys_lo, ys_hi, yr, wout, g, wr, br, tri)


def _sort_tables(cnt, n_steps):
    n_tiles = cnt.shape[0]
    npc = (cnt + PIECE - 1) // PIECE
    seg = jnp.cumsum(npc, axis=1) - npc
    before = jnp.cumsum(npc, axis=0) - npc
    n_tile_g = (jnp.sum(npc, axis=0) + MOE_PIECES - 1) // MOE_PIECES
    t_off = jnp.cumsum(n_tile_g) - n_tile_g
    j = jnp.arange(SORTED_PIECES, dtype=I32)[None, :, None]
    in_g = jnp.logical_and(j >= seg[:, None, :], j < (seg + npc)[:, None, :])
    pos = jnp.sum(jnp.where(in_g, MOE_PIECES * t_off[None, None, :] + before[:, None, :] + j - seg[:, None, :], 0),
                  axis=-1)
    valid = jnp.any(in_g, axis=-1)
    stage_piece = jnp.arange(n_tiles * SORTED_PIECES, dtype=I32).reshape(n_tiles, SORTED_PIECES)
    n_slots = n_steps * MOE_PIECES
    src = jnp.zeros((n_slots,), I32).at[jnp.where(valid, pos, n_slots).reshape(-1)].set(
        stage_piece.reshape(-1), mode='drop')
    steps = jnp.arange(n_steps, dtype=I32)
    g_step = jnp.minimum(jnp.sum(steps[:, None] >= (t_off + n_tile_g)[None, :], axis=1), N_GROUPS - 1)
    n_live = jnp.sum(n_tile_g).reshape(1)
    back = jnp.where(valid, pos, 0).reshape(-1)
    return src, g_step.astype(I32), n_live.astype(I32), back.astype(I32)


def _piece_copy(src_ref, piece, buf_ref, slot, j, sem_ref):
    start = piece * PIECE if isinstance(piece, int) else pl.multiple_of(piece * PIECE, PIECE)
    return pltpu.make_async_copy(src_ref.at[pl.ds(start, PIECE)],
                                 buf_ref.at[slot, pl.ds(j * PIECE, PIECE)], sem_ref.at[slot])


def _fetch_pieces(table_ref, src_ref, buf_ref, sem_ref, row, slot, n_pieces):
    for j in range(n_pieces):
        _piece_copy(src_ref, table_ref[row * n_pieces + j], buf_ref, slot, j, sem_ref).start()


def _wait_pieces(src_ref, buf_ref, sem_ref, slot, n_pieces):
    for j in range(n_pieces):
        _piece_copy(src_ref, 0, buf_ref, slot, j, sem_ref).wait()


def _gather_pieces(table_ref, src_ref, buf_ref, sem_ref, n_pieces):
    step = pl.program_id(0)
    last = pl.num_programs(0) - 1
    args = (src_ref, buf_ref, sem_ref)

    @pl.when(step == 0)
    def _():
        _fetch_pieces(table_ref, *args, step, 0, n_pieces)

    _wait_pieces(*args, step % 2, n_pieces)

    def fetch_next():
        _fetch_pieces(table_ref, *args, jnp.minimum(step + 1, last), (step + 1) % 2, n_pieces)

    def drain():
        @pl.when(step == last)
        def _():
            _wait_pieces(*args, (step + 1) % 2, n_pieces)

    return fetch_next, drain


def _to_fp8(x, headroom=FP8_HEADROOM):
    peak = jnp.maximum(jnp.max(jnp.abs(x), axis=(0, 1), keepdims=True), 1e-30)
    return (x * (headroom / peak)).astype(FP8), peak * (1.0 / headroom)


def _moe_kernel(src_ref, gstep_ref, nlive_ref, stage_ref, wg_ref, wu_ref, wd_ref, y_ref,
                buf_ref, sem_ref, wgb_ref, wub_ref, wdb_ref, inv_ref):
    step = pl.program_id(0)
    fetch_next, drain = _gather_pieces(src_ref, stage_ref, buf_ref, sem_ref, MOE_PIECES)

    @pl.when(jnp.logical_or(step == 0, gstep_ref[step] != gstep_ref[jnp.maximum(step - 1, 0)]))
    def _():
        ones = jnp.ones((1, LANES), F32)
        for e in range(EXPERTS_PER_GROUP):
            wgb_ref[e], inv = _to_fp8(wg_ref[e])
            inv_ref[e:e + 1, :] = inv * ones
            wub_ref[e], inv = _to_fp8(wu_ref[e])
            inv_ref[EXPERTS_PER_GROUP + e:EXPERTS_PER_GROUP + e + 1, :] = inv * ones
        peak = functools.reduce(jnp.maximum, [jnp.max(jnp.abs(wd_ref[e]), axis=(0, 1), keepdims=True)
                                              for e in range(EXPERTS_PER_GROUP)])
        peak = jnp.maximum(peak, 1e-30)
        for e in range(EXPERTS_PER_GROUP):
            wdb_ref[e * EXPERT_FF:(e + 1) * EXPERT_FF, :] = (wd_ref[e] * (FP8_HEADROOM / peak)).astype(FP8)
        inv_ref[2 * EXPERTS_PER_GROUP:2 * EXPERTS_PER_GROUP + 1, :] = peak * (1.0 / FP8_HEADROOM) * ones

    @pl.when(step < nlive_ref[0])
    def _():
        slot = step % 2
        fetch_next()

        def up(rs):
            t = buf_ref[slot, rs, :D_MODEL].astype(FP8)
            t_inv = (buf_ref[slot, rs, D_MODEL + INV_LANE:D_MODEL + INV_LANE + 1].astype(F32)
                     + buf_ref[slot, rs, D_MODEL + INV_LANE + 1:D_MODEL + INV_LANE + 2].astype(F32))
            out = []
            for e in range(EXPERTS_PER_GROUP):
                g_inv = inv_ref[e:e + 1, 0:1] * t_inv
                u_inv = inv_ref[EXPERTS_PER_GROUP + e:EXPERTS_PER_GROUP + e + 1, 0:1] * t_inv
                out.append((_dot(t, wgb_ref[e]) * g_inv, _dot(t, wub_ref[e]), u_inv))
            return out

        def down(rs, hidden):
            cw = buf_ref[slot, rs, D_MODEL:].astype(F32)
            acts = []
            for e, (hg, hu, u_inv) in enumerate(hidden):
                c = (cw[:, e:e + 1] + cw[:, EXPERTS_PER_GROUP + e:EXPERTS_PER_GROUP + e + 1]) * u_inv
                acts.append(hg * _sigmoid(hg) * hu * c)
            act, a_inv = _to_fp8(jnp.concatenate(acts, axis=1))
            d_inv = inv_ref[2 * EXPERTS_PER_GROUP:2 * EXPERTS_PER_GROUP + 1, 0:1] * a_inv
            y_ref[rs, :] = (_dot(act, wdb_ref[...]) * d_inv).astype(BF16)

        chunks = [slice(b * ROW_CHUNK, (b + 1) * ROW_CHUNK) for b in range(MOE_TILE // ROW_CHUNK)]
        hidden = [up(rs) for rs in chunks]
        for rs, hid in zip(chunks, hidden):
            down(rs, hid)

    @pl.when(step >= nlive_ref[0])
    def _():
        fetch_next()
        y_ref[...] = jnp.zeros_like(y_ref)

    drain()


def _moe_call(src, g_step, n_live, stage, wg, wu, wd, n_steps):
    grp = lambda shape: pl.BlockSpec((EXPERTS_PER_GROUP,) + shape, lambda s, src, gs, nl: (gs[s], 0, 0))
    return pl.pallas_call(
        _moe_kernel,
        grid_spec=pltpu.PrefetchScalarGridSpec(
            num_scalar_prefetch=3,
            grid=(n_steps,),
            in_specs=[pl.BlockSpec(memory_space=pl.ANY),
                      grp((D_MODEL, EXPERT_FF)), grp((D_MODEL, EXPERT_FF)), grp((EXPERT_FF, D_MODEL))],
            out_specs=pl.BlockSpec((MOE_TILE, D_MODEL), lambda s, src, gs, nl: (s, 0)),
            scratch_shapes=[pltpu.VMEM((2, MOE_TILE, EXT_WIDTH), BF16), pltpu.SemaphoreType.DMA((2,)),
                            pltpu.VMEM((EXPERTS_PER_GROUP, D_MODEL, EXPERT_FF), FP8),
                            pltpu.VMEM((EXPERTS_PER_GROUP, D_MODEL, EXPERT_FF), FP8),
                            pltpu.VMEM((EXPERTS_PER_GROUP * EXPERT_FF, D_MODEL), FP8),
                            pltpu.VMEM((16, LANES), F32)],
        ),
        out_shape=jax.ShapeDtypeStruct((n_steps * MOE_TILE, D_MODEL), BF16),
        compiler_params=pltpu.CompilerParams(
            dimension_semantics=("arbitrary",), vmem_limit_bytes=VMEM_LIMIT),
        name="moe_call",
    )(src, g_step, n_live, stage, wg, wu, wd)


def _final_kernel(back_ref, ysort_ref, h_ref, dest_ref, g_ref, o_ref, buf_ref, sem_ref):
    step = pl.program_id(0)
    fetch_next, drain = _gather_pieces(back_ref, ysort_ref, buf_ref, sem_ref, FINAL_TILES * SORTED_PIECES)
    fetch_next()
    for t in range(FINAL_TILES):
        dest = _stack_rows([dest_ref[t, 0:1, :]], LANES).T[:, 0:1].astype(I32)
        unperm = jnp.where(lax.broadcasted_iota(I32, (TILE, SORTED_ROWS), 1) == dest, 1.0, 0.0).astype(BF16)
        rows = buf_ref[step % 2, t * SORTED_ROWS:(t + 1) * SORTED_ROWS, :]
        for b in range(TILE // ROW_CHUNK):
            rs = slice(b * ROW_CHUNK, (b + 1) * ROW_CHUNK)
            os = slice(t * TILE + b * ROW_CHUNK, t * TILE + (b + 1) * ROW_CHUNK)
            o_ref[os, :] = _rms_norm(h_ref[os, :].astype(F32) + _dot(unperm[rs], rows), g_ref[...])
    drain()


def _final_call(back, ysort, h, dest, g):
    n_tok = h.shape[0]
    return pl.pallas_call(
        _final_kernel,
        grid_spec=pltpu.PrefetchScalarGridSpec(
            num_scalar_prefetch=1,
            grid=(n_tok // (FINAL_TILES * TILE),),
            in_specs=[pl.BlockSpec(memory_space=pl.ANY),
                      pl.BlockSpec((FINAL_TILES * TILE, D_MODEL), lambda i, back: (i, 0)),
                      pl.BlockSpec((FINAL_TILES, 8, TILE), lambda i, back: (i, 0, 0)),
                      pl.BlockSpec((1, D_MODEL), lambda i, back: (0, 0))],
            out_specs=pl.BlockSpec((FINAL_TILES * TILE, D_MODEL), lambda i, back: (i, 0)),
            scratch_shapes=[pltpu.VMEM((2, FINAL_TILES * SORTED_ROWS, D_MODEL), BF16),
                            pltpu.SemaphoreType.DMA((2,))],
        ),
        out_shape=jax.ShapeDtypeStruct((n_tok, D_MODEL), F32),
        compiler_params=pltpu.CompilerParams(
            dimension_semantics=("arbitrary",), vmem_limit_bytes=VMEM_LIMIT),
        name="final_call",
    )(back, ysort, h, dest, g)


def _rope_tables(length):
    pos = np.arange(length, dtype=np.float32)
    inv_freq = np.float32(ROPE_BASE) ** (-np.arange(0, HEAD_DIM, 2, dtype=np.float32) / np.float32(HEAD_DIM))
    ang = pos[:, None] * inv_freq[None, :]
    cos, sin = np.cos(ang), np.sin(ang)
    return np.concatenate([cos, cos], axis=-1), np.concatenate([-sin, sin], axis=-1)


def _retention_tables():
    f32 = np.float32
    gamma = f32(1.0) - f32(2.0) ** (f32(-5.0) - np.arange(HEADS, dtype=f32))
    log_g = np.log(gamma)[:, None, None]
    scale = f32(HEAD_DIM ** -0.5)
    idx = np.arange(RET_BLOCK)
    dist = np.abs(idx[:, None] - idx[None, :]).astype(f32)
    visible = (idx[None, :] // CHUNK) <= (idx[:, None] // CHUNK)
    mask = np.where(visible[None], np.exp(log_g * dist[None]), f32(0.0)) * scale
    ones = np.ones((1, 1, HEAD_DIM), f32)
    idx_f = idx.astype(f32)[None, :, None]
    pair = lambda a: a.reshape(HEAD_PAIRS, 2, a.shape[1], HEAD_DIM).transpose(0, 2, 1, 3).reshape(
        HEAD_PAIRS, a.shape[1], PAIR_DIM)
    qdec = pair(np.exp(log_g * (idx_f + f32(1.0))) * ones)
    kdec = pair(np.exp(log_g * (f32(RET_BLOCK - 1.0) - idx_f)) * scale * ones)
    meta_idx = np.arange(N_META, dtype=f32)[None, :, None]
    kdec_meta = pair(np.exp(log_g * (f32(N_META - 1.0) - meta_idx)) * scale * ones)
    blk = np.kron(np.eye(2, dtype=f32), np.ones((HEAD_DIM, HEAD_DIM), f32))
    bdec = np.exp(log_g * f32(RET_BLOCK)).reshape(HEAD_PAIRS, 2)
    bdec = np.stack([np.kron(np.diag(b), np.ones((HEAD_DIM, HEAD_DIM), f32)) for b in bdec])
    return tuple(a.astype(f32) for a in (mask, qdec, kdec, bdec, blk, kdec_meta))


def kernel(x, meta_tokens, norm_mix_g, w_in, ssm_lambda_re, ssm_lambda_im, ssm_log_dt, ssm_b_re, ssm_b_im, ssm_c_re, ssm_c_im, ssm_d, w_glu, w_out, norm_ffn_g, w_router_group, b_router_group, w_router_expert, b_router_expert, w_gate, w_up, w_down, norm_final_g):
    bsz, seq, _ = x.shape
    assert seq % TILE == 0 and seq % MIXER_TILE == 0 and MIXER_TILE % RET_BLOCK == 0 and RET_BLOCK % CHUNK == 0
    n_blocks = seq // S5_BLOCK
    assert n_blocks % SCAN_ROWS == 0
    n_tok = bsz * seq
    assert n_tok % (FINAL_TILES * TILE) == 0 and n_tok % (PROJ_TILES * TILE) == 0
    n_tiles = n_tok // TILE
    n_steps = -(-n_tiles * (TILE_PIECES + N_GROUPS - 1) // MOE_PIECES) + N_GROUPS

    cos, sin = _rope_tables(N_META + seq)
    mask, qdec, kdec, bdec, blk, kdec_meta = _retention_tables()
    g_mix = norm_mix_g[0][None, :]
    u, y_ret, u_meta = _mixer_call(x, g_mix, w_in[0], cos[N_META:], sin[N_META:], mask, qdec, kdec, bdec, blk,
                                   meta_tokens, cos[:N_META], sin[:N_META], kdec_meta)

    s5_ops = _s5_operators(
        ssm_lambda_re[0], ssm_lambda_im[0], ssm_log_dt[0], ssm_b_re[0], ssm_b_im[0],
        ssm_c_re[0], ssm_c_im[0], ssm_d[0])
    um = u_meta.reshape(S5_BLOCK, SSM_GROUPS, SSM_GROUP).transpose(1, 0, 2).reshape(SSM_GROUPS, 1, S5_LANES)
    um = jnp.pad(um, ((0, 0), (0, 7), (0, 0)))
    y_lo, y_hi = _s5_call(u, um, *s5_ops, w_glu[0].astype(BF16))

    w_r = jnp.concatenate(
        [w_router_group[0].T, w_router_expert[0].transpose(0, 2, 1).reshape(N_EXPERTS, D_MODEL)], axis=0)
    w_r = jnp.pad(w_r, ((0, ROUTE_ROWS - w_r.shape[0]), (0, 0)))
    b_r = jnp.concatenate([b_router_group[0], b_router_expert[0].reshape(-1)])
    b_r = jnp.pad(b_r, (0, ROUTE_ROWS - b_r.shape[0]))[:, None]
    w_r_hi = w_r.astype(BF16)
    w_r = jnp.concatenate([w_r_hi, (w_r - w_r_hi.astype(F32)).astype(BF16)], axis=0)
    tri = jnp.asarray(np.arange(TILE)[:, None] <= np.arange(TILE)[None, :], BF16)

    h, stage, dest, cnt = _proj_call(
        x.reshape(n_tok, D_MODEL), y_lo.reshape(n_tok, LANES), y_hi.reshape(n_tok, LANES),
        y_ret.reshape(n_tok, RET_WIDTH),
        w_out[0], norm_ffn_g[0][None, :], w_r, b_r, tri)
    src, g_step, n_live, back = _sort_tables(cnt[:, :N_GROUPS, 0].astype(I32), n_steps)
    y_sorted = _moe_call(src, g_step, n_live, stage, w_gate[0], w_up[0], w_down[0], n_steps)
    out = _final_call(back, y_sorted, h, dest, norm_final_g[None, :])
    return out.reshape(bsz, seq, D_MODEL)
```

```python
import functools

import jax
import jax.numpy as jnp
import numpy as np
from jax import lax
from jax.experimental import pallas as pl
from jax.experimental.pallas import tpu as pltpu

D_MODEL = 1024
N_META = 16
CHUNK = 64
EPS = 1e-6
SSM_WIDTH = 256
SSM_GROUP = 16
SSM_GROUPS = 16
SSM_STATE = 64
RET_WIDTH = 768
HEAD_DIM = 128
HEADS = 6
HEAD_PAIRS = HEADS // 2
PAIR_DIM = 2 * HEAD_DIM
ROPE_BASE = 10000.0
IN_WIDTH = SSM_WIDTH + 4 * RET_WIDTH
N_GROUPS = 4
EXPERTS_PER_GROUP = 4
N_EXPERTS = 16
EXPERT_FF = 256

S5_BLOCK = 16
S5_LANES = S5_BLOCK * SSM_GROUP
SCAN_ROWS = 8
RET_BLOCK = 256
TILE = 512
MIXER_TILE = 1024
LANES = 128
ROUTE_ROWS = 32
ROW_CHUNK = 256
PROJ_TILES = 2
FINAL_TILES = 4
PIECE = 16
TILE_PIECES = TILE // PIECE
MOE_TILE = 1024
MOE_PIECES = MOE_TILE // PIECE
SORTED_PIECES = TILE_PIECES + N_GROUPS
SORTED_ROWS = SORTED_PIECES * PIECE
EXT_WIDTH = D_MODEL + LANES
INV_LANE = 2 * EXPERTS_PER_GROUP
V7X_VMEM_BYTES = 64 * 1024 * 1024
VMEM_LIMIT = V7X_VMEM_BYTES * 7 // 8

F32 = jnp.float32
BF16 = jnp.bfloat16
FP8 = jnp.float8_e4m3fn
FP8_HEADROOM = 256.0
I32 = jnp.int32


def _dot(a, b):
    return jnp.dot(a, b, preferred_element_type=F32)


def _sigmoid(x):
    return 0.5 * jnp.tanh(0.5 * x) + 0.5


def _rms_norm(x, g):
    return x * lax.rsqrt(jnp.mean(x * x, axis=-1, keepdims=True) + EPS) * g


def _rope(t, cos, sin_signed):
    return t * cos + pltpu.roll(t, HEAD_DIM // 2, axis=1) * sin_signed


def _meta_state(meta_ref, g_ref, w_ref, cos_ref, sin_ref, kdec_ref, blk_ref, u_ref, r0_ref):
    a = _rms_norm(meta_ref[...], g_ref[...]).astype(BF16)
    u_ref[...] = _dot(a, w_ref[:, :SSM_WIDTH])
    k_off = SSM_WIDTH + RET_WIDTH
    v_off = SSM_WIDTH + 2 * RET_WIDTH
    cos = cos_ref[...]
    sin = sin_ref[...]
    for p in range(HEAD_PAIRS):
        k = _dot(a, w_ref[:, k_off + p * PAIR_DIM:k_off + (p + 1) * PAIR_DIM])
        v = _dot(a, w_ref[:, v_off + p * PAIR_DIM:v_off + (p + 1) * PAIR_DIM])
        k = jnp.concatenate([_rope(k[:, :HEAD_DIM], cos, sin), _rope(k[:, HEAD_DIM:], cos, sin)], axis=1)
        kd = (k * kdec_ref[p]).astype(BF16)
        r0_ref[p] = _dot_rows(kd, v.astype(BF16)) * blk_ref[...]


def _dot_rows(a, b):
    return lax.dot_general(a, b, (((0,), (0,)), ((), ())), preferred_element_type=F32)


def _mixer_kernel(x_ref, g_ref, w32_ref, cos_ref, sin_ref, mask_ref, qdec_ref, kdec_ref, bdec_ref, blk_ref,
                  meta_ref, cos_m_ref, sin_m_ref, kdec_m_ref, u_ref, y_ref, um_ref, w_ref, r0_ref, r_ref):
    first_tile = pl.program_id(1) == 0

    @pl.when(jnp.logical_and(pl.program_id(0) == 0, first_tile))
    def _():
        for c in range(0, IN_WIDTH, SSM_WIDTH):
            w_ref[:, c:c + SSM_WIDTH] = w32_ref[:, c:c + SSM_WIDTH].astype(BF16)
        _meta_state(meta_ref, g_ref, w_ref, cos_m_ref, sin_m_ref, kdec_m_ref, blk_ref, um_ref, r0_ref)

    @pl.when(first_tile)
    def _():
        r_ref[...] = r0_ref[...]

    off = SSM_WIDTH
    for b in range(MIXER_TILE // RET_BLOCK):
        bs = slice(b * RET_BLOCK, (b + 1) * RET_BLOCK)
        a = _rms_norm(x_ref[bs, :], g_ref[...]).astype(BF16)
        u_ref[bs, :] = _dot(a, w_ref[:, :SSM_WIDTH])
        q_all = _dot(a, w_ref[:, off:off + RET_WIDTH])
        k_all = _dot(a, w_ref[:, off + RET_WIDTH:off + 2 * RET_WIDTH])
        v_all = _dot(a, w_ref[:, off + 2 * RET_WIDTH:off + 3 * RET_WIDTH])
        gate = _dot(a, w_ref[:, off + 3 * RET_WIDTH:off + 4 * RET_WIDTH])
        cos = cos_ref[bs, :]
        sin = sin_ref[bs, :]
        def scores(h):
            hs = slice(h * HEAD_DIM, (h + 1) * HEAD_DIM)
            q = _rope(q_all[:, hs], cos, sin)
            k = _rope(k_all[:, hs], cos, sin)
            return q, k, _dot_t(q.astype(BF16), k.astype(BF16))

        ahead = scores(0)
        for p in range(HEAD_PAIRS):
            ps = slice(p * PAIR_DIM, (p + 1) * PAIR_DIM)
            pair = [ahead, scores(2 * p + 1)]
            if p + 1 < HEAD_PAIRS:
                ahead = scores(2 * p + 2)
            q = jnp.concatenate([pair[0][0], pair[1][0]], axis=1)
            k = jnp.concatenate([pair[0][1], pair[1][1]], axis=1)
            v = v_all[:, ps].astype(BF16)
            state = r_ref[p]
            cross = _dot((q * qdec_ref[p]).astype(BF16), state.astype(BF16))
            kv = _dot_rows((k * kdec_ref[p]).astype(BF16), v)
            r_ref[p] = state * bdec_ref[p] + kv * blk_ref[...]
            for half, (_, _, s) in enumerate(pair):
                h = 2 * p + half
                hs = slice(h * HEAD_DIM, (h + 1) * HEAD_DIM)
                ls = slice(half * HEAD_DIM, (half + 1) * HEAD_DIM)
                o = _dot((s * mask_ref[h]).astype(BF16), v[:, ls]) + cross[:, ls]
                mu = jnp.mean(o, axis=-1, keepdims=True)
                d = o - mu
                var = jnp.mean(d * d, axis=-1, keepdims=True)
                gt = gate[:, hs]
                y_ref[bs, hs] = (gt * _sigmoid(gt) * d * lax.rsqrt(var + EPS)).astype(BF16)


def _mixer_call(x, g, w_in, cos, sin, mask, qdec, kdec, bdec, blk, meta, cos_m, sin_m, kdec_m):
    bsz, seq, _ = x.shape
    const = lambda a, **kw: pl.BlockSpec(a.shape, lambda b, i: (0,) * a.ndim, **kw)
    return pl.pallas_call(
        _mixer_kernel,
        grid=(bsz, seq // MIXER_TILE),
        in_specs=[
            pl.BlockSpec((None, MIXER_TILE, D_MODEL), lambda b, i: (b, i, 0)),
            const(g),
            const(w_in, pipeline_mode=pl.Buffered(1)),
            pl.BlockSpec((MIXER_TILE, HEAD_DIM), lambda b, i: (i, 0)),
            pl.BlockSpec((MIXER_TILE, HEAD_DIM), lambda b, i: (i, 0)),
            const(mask), const(qdec), const(kdec), const(bdec), const(blk),
            const(meta), const(cos_m), const(sin_m), const(kdec_m),
        ],
        out_specs=(
            pl.BlockSpec((None, MIXER_TILE, SSM_WIDTH), lambda b, i: (b, i, 0)),
            pl.BlockSpec((None, MIXER_TILE, RET_WIDTH), lambda b, i: (b, i, 0)),
            pl.BlockSpec((N_META, SSM_WIDTH), lambda b, i: (0, 0)),
        ),
        out_shape=(jax.ShapeDtypeStruct((bsz, seq, SSM_WIDTH), F32),
                   jax.ShapeDtypeStruct((bsz, seq, RET_WIDTH), BF16),
                   jax.ShapeDtypeStruct((N_META, SSM_WIDTH), F32)),
        scratch_shapes=[pltpu.VMEM((D_MODEL, IN_WIDTH), BF16),
                        pltpu.VMEM((HEAD_PAIRS, PAIR_DIM, PAIR_DIM), F32),
                        pltpu.VMEM((HEAD_PAIRS, PAIR_DIM, PAIR_DIM), F32)],
        compiler_params=pltpu.CompilerParams(
            dimension_semantics=("arbitrary", "arbitrary"), vmem_limit_bytes=VMEM_LIMIT),
        name="mixer_call",
    )(x, g, w_in, cos, sin, mask, qdec, kdec, bdec, blk, meta, cos_m, sin_m, kdec_m)


def _dot_t(a, b):
    return lax.dot_general(a, b, (((1,), (1,)), ((), ())), preferred_element_type=F32)


def _chunk_transpose(arrs):
    n = len(arrs)
    chunk = lax.broadcasted_iota(I32, (1, LANES), 1) // SSM_GROUP
    arrs = list(arrs)
    s = n // 2
    while s:
        keep = (chunk & s) == 0
        nxt = list(arrs)
        for i in range(n):
            if i & s == 0:
                lo, hi = arrs[i], arrs[i + s]
                nxt[i] = jnp.where(keep, lo, pltpu.roll(hi, s * SSM_GROUP, axis=1))
                nxt[i + s] = jnp.where(keep, pltpu.roll(lo, LANES - s * SSM_GROUP, axis=1), hi)
        arrs = nxt
        s //= 2
    return arrs


def _s5_kernel(u_lo_ref, u_hi_ref, um_ref, krow_ref, bmat_ref, cre_ref, cim_ref, ar_ref, ai_ref, wglu_ref,
               y_lo_ref, y_hi_ref, t0_ref, ug_ref, yg_ref):
    n_blocks = u_lo_ref.shape[0] // S5_BLOCK

    @pl.when(pl.program_id(0) == 0)
    def _():
        lane = lax.broadcasted_iota(I32, (SSM_GROUP, S5_LANES), 1)
        for g in range(SSM_GROUPS):
            k0 = krow_ref[g]
            for a in range(S5_BLOCK):
                blk = k0 if a == 0 else jnp.where(lane >= a * SSM_GROUP,
                                                  pltpu.roll(k0, a * SSM_GROUP, axis=1), 0.0)
                t0_ref[g, a * SSM_GROUP:(a + 1) * SSM_GROUP, :] = blk.astype(BF16)

    per_tile = LANES // SSM_GROUP
    for half, uh_ref in enumerate((u_lo_ref, u_hi_ref)):
        for t in range(S5_BLOCK // per_tile):
            words = [pltpu.bitcast(uh_ref[pl.ds(t * per_tile + k, n_blocks, stride=S5_BLOCK), :].astype(BF16),
                                   jnp.uint32) for k in range(per_tile)]
            for m, w in enumerate(_chunk_transpose(words)):
                ug_ref[half * per_tile + m, :, t * LANES:(t + 1) * LANES] = pltpu.bitcast(w, BF16)

    re, im, re0, im0 = [], [], [], []
    for p in range(SSM_GROUPS // 2):
        g0, g1 = 2 * p, 2 * p + 1
        v = _dot(ug_ref[g0], bmat_ref[g0]) + _dot(ug_ref[g1], bmat_ref[g1])
        v0 = (_dot(um_ref[g0].astype(BF16), bmat_ref[g0]) + _dot(um_ref[g1].astype(BF16), bmat_ref[g1]))[0:1]
        re.append(v[:, :LANES])
        im.append(v[:, LANES:])
        re0.append(v0[:, :LANES])
        im0.append(v0[:, LANES:])
    re, im, re0, im0 = (jnp.concatenate(parts, axis=1) for parts in (re, im, re0, im0))

    row = lax.broadcasted_iota(I32, re.shape, 0)
    ar, ai = ar_ref[0:1], ai_ref[0:1]
    re = re + jnp.where(row == 0, ar * re0 - ai * im0, 0.0)
    im = im + jnp.where(row == 0, ar * im0 + ai * re0, 0.0)
    d = 1
    while d < SCAN_ROWS:
        ar, ai = ar_ref[d - 1:d], ai_ref[d - 1:d]
        inside = row % SCAN_ROWS >= d
        sre = jnp.where(inside, pltpu.roll(re, d, axis=0), 0.0)
        sim = jnp.where(inside, pltpu.roll(im, d, axis=0), 0.0)
        re, im = re + ar * sre - ai * sim, im + ar * sim + ai * sre
        d *= 2
    ar, ai = ar_ref[...], ai_ref[...]
    re_tiles, im_tiles = [re[:SCAN_ROWS]], [im[:SCAN_ROWS]]
    for r in range(1, n_blocks // SCAN_ROWS):
        cre_ = re_tiles[-1][SCAN_ROWS - 1:SCAN_ROWS]
        cim_ = im_tiles[-1][SCAN_ROWS - 1:SCAN_ROWS]
        rs = slice(r * SCAN_ROWS, (r + 1) * SCAN_ROWS)
        re_tiles.append(re[rs] + ar * cre_ - ai * cim_)
        im_tiles.append(im[rs] + ar * cim_ + ai * cre_)
    re = jnp.concatenate(re_tiles, axis=0)
    im = jnp.concatenate(im_tiles, axis=0)
    pre = jnp.where(row == 0, re0, pltpu.roll(re, 1, axis=0)).astype(BF16)
    pim = jnp.where(row == 0, im0, pltpu.roll(im, 1, axis=0)).astype(BF16)

    for g in range(SSM_GROUPS):
        ps = slice((g // 2) * LANES, (g // 2 + 1) * LANES)
        yg_ref[g] = (_dot(ug_ref[g], t0_ref[g]) + _dot_t(pre[:, ps], cre_ref[g])
                     + _dot_t(pim[:, ps], cim_ref[g])).astype(BF16)

    wglu = wglu_ref[...]
    for t in range(S5_BLOCK // per_tile):
        ts = slice(t * LANES, (t + 1) * LANES)
        halves = [_chunk_transpose([pltpu.bitcast(yg_ref[half * per_tile + m, :, ts], jnp.uint32)
                                    for m in range(per_tile)])
                  for half in range(SSM_GROUPS // per_tile)]
        for k in range(per_tile):
            y = jnp.concatenate([pltpu.bitcast(h[k], BF16) for h in halves], axis=1).astype(F32)
            y = jax.nn.gelu(y, approximate=True)
            y = y * _sigmoid(_dot(y.astype(BF16), wglu))
            i = t * per_tile + k
            y_lo_ref[pl.ds(i, n_blocks, stride=S5_BLOCK), :] = y[:, :LANES]
            y_hi_ref[pl.ds(i, n_blocks, stride=S5_BLOCK), :] = y[:, LANES:]


def _s5_call(u, um, krow, bmat, cre, cim, ar, ai, wglu):
    bsz, seq, _ = u.shape
    n_blocks = seq // S5_BLOCK
    const = lambda a: pl.BlockSpec(a.shape, lambda b: (0,) * a.ndim)
    return pl.pallas_call(
        _s5_kernel,
        grid=(bsz,),
        in_specs=[pl.BlockSpec((None, seq, LANES), lambda b: (b, 0, 0)),
                  pl.BlockSpec((None, seq, LANES), lambda b: (b, 0, 1)),
                  const(um), const(krow), const(bmat), const(cre), const(cim), const(ar), const(ai),
                  const(wglu)],
        out_specs=(pl.BlockSpec((None, seq, LANES), lambda b: (b, 0, 0)),
                   pl.BlockSpec((None, seq, LANES), lambda b: (b, 0, 0))),
        out_shape=(jax.ShapeDtypeStruct((bsz, seq, LANES), F32),
                   jax.ShapeDtypeStruct((bsz, seq, LANES), F32)),
        scratch_shapes=[pltpu.VMEM((SSM_GROUPS, S5_LANES, S5_LANES), BF16),
                        pltpu.VMEM((SSM_GROUPS, n_blocks, S5_LANES), BF16),
                        pltpu.VMEM((SSM_GROUPS, n_blocks, S5_LANES), BF16)],
        compiler_params=pltpu.CompilerParams(
            dimension_semantics=("arbitrary",), vmem_limit_bytes=VMEM_LIMIT),
        name="s5_call",
    )(u, u, um, krow, bmat, cre, cim, ar, ai, wglu)


def _s5_operators(lam_re, lam_im, log_dt, b_re, b_im, c_re, c_im, d_skip):
    n_groups = lam_re.shape[0]
    lam = lax.complex(lam_re, lam_im)
    lam_dt = lam * jnp.exp(log_dt)[:, None]
    lam_bar = jnp.exp(lam_dt)
    b_bar = ((lam_bar - 1.0) / lam)[..., None] * lax.complex(b_re, b_im)
    c = lax.complex(c_re, c_im)
    tau = jnp.arange(S5_BLOCK + 1, dtype=F32)
    pows = jnp.exp(lam_dt[None] * tau[:, None, None])
    kern = jnp.real(jnp.einsum('ghp,tgp,gpk->gkth', c, pows[:S5_BLOCK], b_bar))
    skip = (jnp.eye(SSM_GROUP, dtype=F32)[None, :, None, :] * d_skip[:, None, None, :]
            * (tau[:S5_BLOCK] == 0).astype(F32)[None, None, :, None])
    krow = (kern + skip).reshape(n_groups, SSM_GROUP, S5_LANES)
    even = (jnp.arange(n_groups) % 2 == 0)[:, None, None]
    pair_pad = lambda m: jnp.concatenate([jnp.where(even, m, 0.0), jnp.where(even, 0.0, m)], axis=-1)
    bm = (pows[S5_BLOCK - 1 - jnp.arange(S5_BLOCK)].transpose(1, 0, 2)[:, :, None, :]
          * b_bar.transpose(0, 2, 1)[:, None, :, :]).reshape(n_groups, S5_LANES, SSM_STATE)
    bmat = jnp.concatenate([pair_pad(jnp.real(bm)), pair_pad(jnp.imag(bm))], axis=-1)
    cm = (pows[1:].transpose(1, 0, 2)[:, :, None, :] * c[:, None, :, :]).reshape(n_groups, S5_LANES, SSM_STATE)
    cre, cim = pair_pad(jnp.real(cm)), pair_pad(-jnp.imag(cm))
    step = S5_BLOCK * (1.0 + jnp.arange(SCAN_ROWS, dtype=F32))
    adec = jnp.exp(lam_dt[None, :, :] * step[:, None, None]).reshape(SCAN_ROWS, n_groups * SSM_STATE)
    return krow, bmat.astype(BF16), cre.astype(BF16), cim.astype(BF16), jnp.real(adec), jnp.imag(adec)


def _first_hit(values, target):
    hits, taken = [], None
    for v in values:
        hit = (v >= target) if taken is None else jnp.logical_and(v >= target, jnp.logical_not(taken))
        taken = hit if taken is None else jnp.logical_or(taken, hit)
        hits.append(hit)
    return hits


def _stack_rows(rows, n_rows):
    idx = lax.broadcasted_iota(I32, (n_rows, rows[0].shape[1]), 0)
    out = jnp.zeros((n_rows, rows[0].shape[1]), F32)
    for k, r in enumerate(rows):
        out = jnp.where(idx == k, r, out)
    return out


def _project(rs, x_ref, ys_lo_ref, ys_hi_ref, yr_ref, wout_ref, h_ref):
    ys = jnp.concatenate([ys_lo_ref[rs, :], ys_hi_ref[rs, :]], axis=1).astype(BF16)
    h = x_ref[rs, :] + _dot(ys, wout_ref[:SSM_WIDTH, :]) + _dot(yr_ref[rs, :], wout_ref[SSM_WIDTH:, :])
    h_ref[rs, :] = h.astype(BF16)
    return h


def _route(lt, tri_ref):
    gl = [lt[g:g + 1, :] for g in range(N_GROUPS)]
    gmax = functools.reduce(jnp.maximum, gl)
    g_w = 1.0 / functools.reduce(lambda a, b: a + b, [jnp.exp(l - gmax) for l in gl])
    sel = _first_hit(gl, gmax)
    ev = []
    for e in range(EXPERTS_PER_GROUP):
        acc = jnp.zeros_like(gmax)
        for g in range(N_GROUPS):
            k = N_GROUPS + g * EXPERTS_PER_GROUP + e
            acc = jnp.where(sel[g], lt[k:k + 1, :], acc)
        ev.append(acc)
    m1 = functools.reduce(jnp.maximum, ev)
    first = _first_hit(ev, m1)
    rest = [jnp.where(f, -jnp.inf, v) for f, v in zip(first, ev)]
    m2 = functools.reduce(jnp.maximum, rest)
    second = _first_hit(rest, m2)
    e2 = jnp.exp(m2 - m1)
    w1 = g_w / (1.0 + e2)
    w2 = e2 * w1
    combine = [jnp.where(f, w1, 0.0) + jnp.where(s, w2, 0.0) for f, s in zip(first, second)]

    sel_f = [jnp.where(s, 1.0, 0.0) for s in sel]
    incl = _dot(_stack_rows(sel_f, 8).astype(BF16), tri_ref[...])
    dest = jnp.zeros_like(gmax)
    seg_start = jnp.zeros((1, 1), F32)
    counts = []
    for g in range(N_GROUPS):
        run = incl[g:g + 1, :]
        cnt = run[:, TILE - 1:TILE]
        counts.append(cnt)
        dest = dest + sel_f[g] * (seg_start + run - 1.0)
        seg_start = seg_start + PIECE * jnp.floor((cnt + (PIECE - 1.0)) * (1.0 / PIECE))
    return combine, dest, counts


def _proj_kernel(x_ref, ys_lo_ref, ys_hi_ref, yr_ref, wout32_ref, g_ref, wr_ref, br_ref,
                 tri_ref, h_ref, stage_ref, dest_ref, cnt_ref, wout_ref):
    @pl.when(pl.program_id(0) == 0)
    def _():
        wout_ref[...] = wout32_ref[...].astype(BF16)

    tiles = range(PROJ_TILES)
    proj_refs = (x_ref, ys_lo_ref, ys_hi_ref, yr_ref, wout_ref, h_ref)
    chunks = [[slice(tile * TILE + b * ROW_CHUNK, tile * TILE + (b + 1) * ROW_CHUNK)
               for b in range(TILE // ROW_CHUNK)] for tile in tiles]
    h_parts = [[_project(rs, *proj_refs) for rs in chunks[tile]] for tile in tiles]

    logits, staged = [], []
    for tile in tiles:
        parts, rows, inv = [], [], []
        for h in h_parts[tile]:
            t = _rms_norm(h, g_ref[...])
            parts.append(t.astype(BF16))
            peak = jnp.maximum(jnp.max(jnp.abs(t), axis=1, keepdims=True), 1e-30)
            rows.append((t * (FP8_HEADROOM / peak)).astype(BF16))
            inv.append(peak * (1.0 / FP8_HEADROOM))
        staged.append((jnp.concatenate(rows, axis=0), jnp.concatenate(inv, axis=0)))
        lt_parts = []
        for t_hi in parts:
            both = _dot_t(wr_ref[...], t_hi)
            lt_parts.append(both[:ROUTE_ROWS] + both[ROUTE_ROWS:])
        logits.append(jnp.concatenate(lt_parts, axis=1) + br_ref[...])

    routed = [_route(logits[tile], tri_ref) for tile in tiles]

    for tile in tiles:
        combine, dest, counts = routed[tile]
        dest_ref[tile] = _stack_rows([dest], 8)
        cnt_ref[tile] = _stack_rows([c + jnp.zeros((1, LANES), F32) for c in counts], 8)
        perm = jnp.where(lax.broadcasted_iota(I32, (SORTED_ROWS, TILE), 0) == dest.astype(I32),
                         1.0, 0.0).astype(BF16)
        c_hi = [c.astype(BF16).astype(F32) for c in combine]
        c_lo = [c - hi for c, hi in zip(combine, c_hi)]
        cw = _stack_rows(c_hi + c_lo, LANES).T
        rows, inv = staged[tile]
        inv_hi = inv.astype(BF16).astype(F32)
        lane = lax.broadcasted_iota(I32, (1, LANES), 1)
        cw = jnp.where(lane == INV_LANE, inv_hi, jnp.where(lane == INV_LANE + 1, inv - inv_hi, cw))
        t_ext = jnp.concatenate([rows, cw.astype(BF16)], axis=1)
        stage_ref[tile * SORTED_ROWS:(tile + 1) * SORTED_ROWS, :] = _dot(perm, t_ext).astype(BF16)


def _proj_call(x, ys_lo, ys_hi, yr, wout, g, wr, br, tri):
    n_tok = x.shape[0]
    n_tiles = n_tok // TILE
    const = lambda *shape: pl.BlockSpec(shape, lambda i: (0,) * len(shape))
    rows = lambda width: pl.BlockSpec((PROJ_TILES * TILE, width), lambda i: (i, 0))
    return pl.pallas_call(
        _proj_kernel,
        grid=(n_tiles // PROJ_TILES,),
        in_specs=[
            rows(D_MODEL), rows(LANES), rows(LANES), rows(RET_WIDTH),
            pl.BlockSpec((D_MODEL, D_MODEL), lambda i: (0, 0), pipeline_mode=pl.Buffered(1)), const(1, D_MODEL),
            const(2 * ROUTE_ROWS, D_MODEL), const(ROUTE_ROWS, 1), const(TILE, TILE),
        ],
        out_specs=(rows(D_MODEL),
                   pl.BlockSpec((PROJ_TILES * SORTED_ROWS, EXT_WIDTH), lambda i: (i, 0)),
                   pl.BlockSpec((PROJ_TILES, 8, TILE), lambda i: (i, 0, 0)),
                   pl.BlockSpec((PROJ_TILES, 8, LANES), lambda i: (i, 0, 0))),
        scratch_shapes=[pltpu.VMEM((D_MODEL, D_MODEL), BF16)],
        out_shape=(jax.ShapeDtypeStruct((n_tok, D_MODEL), BF16),
                   jax.ShapeDtypeStruct((n_tiles * SORTED_ROWS, EXT_WIDTH), BF16),
                   jax.ShapeDtypeStruct((n_tiles, 8, TILE), F32),
                   jax.ShapeDtypeStruct((n_tiles, 8, LANES), F32)),
        compiler_params=pltpu.CompilerParams(
            dimension_semantics=("arbitrary",), vmem_limit_bytes=VMEM_LIMIT),
        name="proj_call",
    )(x, ys_lo, ys_hi, yr, wout, g, wr, br, tri)


def _sort_tables(cnt, n_steps):
    n_tiles = cnt.shape[0]
    npc = (cnt + PIECE - 1) // PIECE
    seg = jnp.cumsum(npc, axis=1) - npc
    before = jnp.cumsum(npc, axis=0) - npc
    n_tile_g = (jnp.sum(npc, axis=0) + MOE_PIECES - 1) // MOE_PIECES
    t_off = jnp.cumsum(n_tile_g) - n_tile_g
    j = jnp.arange(SORTED_PIECES, dtype=I32)[None, :, None]
    in_g = jnp.logical_and(j >= seg[:, None, :], j < (seg + npc)[:, None, :])
    pos = jnp.sum(jnp.where(in_g, MOE_PIECES * t_off[None, None, :] + before[:, None, :] + j - seg[:, None, :], 0),
                  axis=-1)
    valid = jnp.any(in_g, axis=-1)
    stage_piece = jnp.arange(n_tiles * SORTED_PIECES, dtype=I32).reshape(n_tiles, SORTED_PIECES)
    n_slots = n_steps * MOE_PIECES
    src = jnp.zeros((n_slots,), I32).at[jnp.where(valid, pos, n_slots).reshape(-1)].set(
        stage_piece.reshape(-1), mode='drop')
    steps = jnp.arange(n_steps, dtype=I32)
    g_step = jnp.minimum(jnp.sum(steps[:, None] >= (t_off + n_tile_g)[None, :], axis=1), N_GROUPS - 1)
    n_live = jnp.sum(n_tile_g).reshape(1)
    back = jnp.where(valid, pos, 0).reshape(-1)
    return src, g_step.astype(I32), n_live.astype(I32), back.astype(I32)


def _piece_copy(src_ref, piece, buf_ref, slot, j, sem_ref):
    start = piece * PIECE if isinstance(piece, int) else pl.multiple_of(piece * PIECE, PIECE)
    return pltpu.make_async_copy(src_ref.at[pl.ds(start, PIECE)],
                                 buf_ref.at[slot, pl.ds(j * PIECE, PIECE)], sem_ref.at[slot])


def _fetch_pieces(table_ref, src_ref, buf_ref, sem_ref, row, slot, n_pieces):
    for j in range(n_pieces):
        _piece_copy(src_ref, table_ref[row * n_pieces + j], buf_ref, slot, j, sem_ref).start()


def _wait_pieces(src_ref, buf_ref, sem_ref, slot, n_pieces):
    for j in range(n_pieces):
        _piece_copy(src_ref, 0, buf_ref, slot, j, sem_ref).wait()


def _gather_pieces(table_ref, src_ref, buf_ref, sem_ref, n_pieces):
    step = pl.program_id(0)
    last = pl.num_programs(0) - 1
    args = (src_ref, buf_ref, sem_ref)

    @pl.when(step == 0)
    def _():
        _fetch_pieces(table_ref, *args, step, 0, n_pieces)

    _wait_pieces(*args, step % 2, n_pieces)

    def fetch_next():
        _fetch_pieces(table_ref, *args, jnp.minimum(step + 1, last), (step + 1) % 2, n_pieces)

    def drain():
        @pl.when(step == last)
        def _():
            _wait_pieces(*args, (step + 1) % 2, n_pieces)

    return fetch_next, drain


def _to_fp8(x, headroom=FP8_HEADROOM):
    peak = jnp.maximum(jnp.max(jnp.abs(x), axis=(0, 1), keepdims=True), 1e-30)
    return (x * (headroom / peak)).astype(FP8), peak * (1.0 / headroom)


def _moe_kernel(src_ref, gstep_ref, nlive_ref, stage_ref, wg_ref, wu_ref, wd_ref, y_ref,
                buf_ref, sem_ref, wgb_ref, wub_ref, wdb_ref, inv_ref):
    step = pl.program_id(0)
    fetch_next, drain = _gather_pieces(src_ref, stage_ref, buf_ref, sem_ref, MOE_PIECES)

    @pl.when(jnp.logical_or(step == 0, gstep_ref[step] != gstep_ref[jnp.maximum(step - 1, 0)]))
    def _():
        ones = jnp.ones((1, LANES), F32)
        for e in range(EXPERTS_PER_GROUP):
            wgb_ref[e], inv = _to_fp8(wg_ref[e])
            inv_ref[e:e + 1, :] = inv * ones
            wub_ref[e], inv = _to_fp8(wu_ref[e])
            inv_ref[EXPERTS_PER_GROUP + e:EXPERTS_PER_GROUP + e + 1, :] = inv * ones
        peak = functools.reduce(jnp.maximum, [jnp.max(jnp.abs(wd_ref[e]), axis=(0, 1), keepdims=True)
                                              for e in range(EXPERTS_PER_GROUP)])
        peak = jnp.maximum(peak, 1e-30)
        for e in range(EXPERTS_PER_GROUP):
            wdb_ref[e * EXPERT_FF:(e + 1) * EXPERT_FF, :] = (wd_ref[e] * (FP8_HEADROOM / peak)).astype(FP8)
        inv_ref[2 * EXPERTS_PER_GROUP:2 * EXPERTS_PER_GROUP + 1, :] = peak * (1.0 / FP8_HEADROOM) * ones

    @pl.when(step < nlive_ref[0])
    def _():
        slot = step % 2
        fetch_next()

        def up(rs):
            t = buf_ref[slot, rs, :D_MODEL].astype(FP8)
            t_inv = (buf_ref[slot, rs, D_MODEL + INV_LANE:D_MODEL + INV_LANE + 1].astype(F32)
                     + buf_ref[slot, rs, D_MODEL + INV_LANE + 1:D_MODEL + INV_LANE + 2].astype(F32))
            out = []
            for e in range(EXPERTS_PER_GROUP):
                g_inv = inv_ref[e:e + 1, 0:1] * t_inv
                u_inv = inv_ref[EXPERTS_PER_GROUP + e:EXPERTS_PER_GROUP + e + 1, 0:1] * t_inv
                out.append((_dot(t, wgb_ref[e]) * g_inv, _dot(t, wub_ref[e]), u_inv))
            return out

        def down(rs, hidden):
            cw = buf_ref[slot, rs, D_MODEL:].astype(F32)
            acts = []
            for e, (hg, hu, u_inv) in enumerate(hidden):
                c = (cw[:, e:e + 1] + cw[:, EXPERTS_PER_GROUP + e:EXPERTS_PER_GROUP + e + 1]) * u_inv
                acts.append(hg * _sigmoid(hg) * hu * c)
            act, a_inv = _to_fp8(jnp.concatenate(acts, axis=1))
            d_inv = inv_ref[2 * EXPERTS_PER_GROUP:2 * EXPERTS_PER_GROUP + 1, 0:1] * a_inv
            y_ref[rs, :] = (_dot(act, wdb_ref[...]) * d_inv).astype(BF16)

        chunks = [slice(b * ROW_CHUNK, (b + 1) * ROW_CHUNK) for b in range(MOE_TILE // ROW_CHUNK)]
        hidden = [up(rs) for rs in chunks]
        for rs, hid in zip(chunks, hidden):
            down(rs, hid)

    @pl.when(step >= nlive_ref[0])
    def _():
        fetch_next()
        y_ref[...] = jnp.zeros_like(y_ref)

    drain()


def _moe_call(src, g_step, n_live, stage, wg, wu, wd, n_steps):
    grp = lambda shape: pl.BlockSpec((EXPERTS_PER_GROUP,) + shape, lambda s, src, gs, nl: (gs[s], 0, 0))
    return pl.pallas_call(
        _moe_kernel,
        grid_spec=pltpu.PrefetchScalarGridSpec(
            num_scalar_prefetch=3,
            grid=(n_steps,),
            in_specs=[pl.BlockSpec(memory_space=pl.ANY),
                      grp((D_MODEL, EXPERT_FF)), grp((D_MODEL, EXPERT_FF)), grp((EXPERT_FF, D_MODEL))],
            out_specs=pl.BlockSpec((MOE_TILE, D_MODEL), lambda s, src, gs, nl: (s, 0)),
            scratch_shapes=[pltpu.VMEM((2, MOE_TILE, EXT_WIDTH), BF16), pltpu.SemaphoreType.DMA((2,)),
                            pltpu.VMEM((EXPERTS_PER_GROUP, D_MODEL, EXPERT_FF), FP8),
                            pltpu.VMEM((EXPERTS_PER_GROUP, D_MODEL, EXPERT_FF), FP8),
                            pltpu.VMEM((EXPERTS_PER_GROUP * EXPERT_FF, D_MODEL), FP8),
                            pltpu.VMEM((16, LANES), F32)],
        ),
        out_shape=jax.ShapeDtypeStruct((n_steps * MOE_TILE, D_MODEL), BF16),
        compiler_params=pltpu.CompilerParams(
            dimension_semantics=("arbitrary",), vmem_limit_bytes=VMEM_LIMIT),
        name="moe_call",
    )(src, g_step, n_live, stage, wg, wu, wd)


def _final_kernel(back_ref, ysort_ref, h_ref, dest_ref, g_ref, o_ref, buf_ref, sem_ref):
    step = pl.program_id(0)
    fetch_next, drain = _gather_pieces(back_ref, ysort_ref, buf_ref, sem_ref, FINAL_TILES * SORTED_PIECES)
    fetch_next()
    for t in range(FINAL_TILES):
        dest = _stack_rows([dest_ref[t, 0:1, :]], LANES).T[:, 0:1].astype(I32)
        unperm = jnp.where(lax.broadcasted_iota(I32, (TILE, SORTED_ROWS), 1) == dest, 1.0, 0.0).astype(BF16)
        rows = buf_ref[step % 2, t * SORTED_ROWS:(t + 1) * SORTED_ROWS, :]
        for b in range(TILE // ROW_CHUNK):
            rs = slice(b * ROW_CHUNK, (b + 1) * ROW_CHUNK)
            os = slice(t * TILE + b * ROW_CHUNK, t * TILE + (b + 1) * ROW_CHUNK)
            o_ref[os, :] = _rms_norm(h_ref[os, :].astype(F32) + _dot(unperm[rs], rows), g_ref[...])
    drain()


def _final_call(back, ysort, h, dest, g):
    n_tok = h.shape[0]
    return pl.pallas_call(
        _final_kernel,
        grid_spec=pltpu.PrefetchScalarGridSpec(
            num_scalar_prefetch=1,
            grid=(n_tok // (FINAL_TILES * TILE),),
            in_specs=[pl.BlockSpec(memory_space=pl.ANY),
                      pl.BlockSpec((FINAL_TILES * TILE, D_MODEL), lambda i, back: (i, 0)),
                      pl.BlockSpec((FINAL_TILES, 8, TILE), lambda i, back: (i, 0, 0)),
                      pl.BlockSpec((1, D_MODEL), lambda i, back: (0, 0))],
            out_specs=pl.BlockSpec((FINAL_TILES * TILE, D_MODEL), lambda i, back: (i, 0)),
            scratch_shapes=[pltpu.VMEM((2, FINAL_TILES * SORTED_ROWS, D_MODEL), BF16),
                            pltpu.SemaphoreType.DMA((2,))],
        ),
        out_shape=jax.ShapeDtypeStruct((n_tok, D_MODEL), F32),
        compiler_params=pltpu.CompilerParams(
            dimension_semantics=("arbitrary",), vmem_limit_bytes=VMEM_LIMIT),
        name="final_call",
    )(back, ysort, h, dest, g)


def _rope_tables(length):
    pos = np.arange(length, dtype=np.float32)
    inv_freq = np.float32(ROPE_BASE) ** (-np.arange(0, HEAD_DIM, 2, dtype=np.float32) / np.float32(HEAD_DIM))
    ang = pos[:, None] * inv_freq[None, :]
    cos, sin = np.cos(ang), np.sin(ang)
    return np.concatenate([cos, cos], axis=-1), np.concatenate([-sin, sin], axis=-1)


def _retention_tables():
    f32 = np.float32
    gamma = f32(1.0) - f32(2.0) ** (f32(-5.0) - np.arange(HEADS, dtype=f32))
    log_g = np.log(gamma)[:, None, None]
    scale = f32(HEAD_DIM ** -0.5)
    idx = np.arange(RET_BLOCK)
    dist = np.abs(idx[:, None] - idx[None, :]).astype(f32)
    visible = (idx[None, :] // CHUNK) <= (idx[:, None] // CHUNK)
    mask = np.where(visible[None], np.exp(log_g * dist[None]), f32(0.0)) * scale
    ones = np.ones((1, 1, HEAD_DIM), f32)
    idx_f = idx.astype(f32)[None, :, None]
    pair = lambda a: a.reshape(HEAD_PAIRS, 2, a.shape[1], HEAD_DIM).transpose(0, 2, 1, 3).reshape(
        HEAD_PAIRS, a.shape[1], PAIR_DIM)
    qdec = pair(np.exp(log_g * (idx_f + f32(1.0))) * ones)
    kdec = pair(np.exp(log_g * (f32(RET_BLOCK - 1.0) - idx_f)) * scale * ones)
    meta_idx = np.arange(N_META, dtype=f32)[None, :, None]
    kdec_meta = pair(np.exp(log_g * (f32(N_META - 1.0) - meta_idx)) * scale * ones)
    blk = np.kron(np.eye(2, dtype=f32), np.ones((HEAD_DIM, HEAD_DIM), f32))
    bdec = np.exp(log_g * f32(RET_BLOCK)).reshape(HEAD_PAIRS, 2)
    bdec = np.stack([np.kron(np.diag(b), np.ones((HEAD_DIM, HEAD_DIM), f32)) for b in bdec])
    return tuple(a.astype(f32) for a in (mask, qdec, kdec, bdec, blk, kdec_meta))


def kernel(x, meta_tokens, norm_mix_g, w_in, ssm_lambda_re, ssm_lambda_im, ssm_log_dt, ssm_b_re, ssm_b_im, ssm_c_re, ssm_c_im, ssm_d, w_glu, w_out, norm_ffn_g, w_router_group, b_router_group, w_router_expert, b_router_expert, w_gate, w_up, w_down, norm_final_g):
    bsz, seq, _ = x.shape
    assert seq % TILE == 0 and seq % MIXER_TILE == 0 and MIXER_TILE % RET_BLOCK == 0 and RET_BLOCK % CHUNK == 0
    n_blocks = seq // S5_BLOCK
    assert n_blocks % SCAN_ROWS == 0
    n_tok = bsz * seq
    assert n_tok % (FINAL_TILES * TILE) == 0 and n_tok % (PROJ_TILES * TILE) == 0
    n_tiles = n_tok // TILE
    n_steps = -(-n_tiles * (TILE_PIECES + N_GROUPS - 1) // MOE_PIECES) + N_GROUPS

    cos, sin = _rope_tables(N_META + seq)
    mask, qdec, kdec, bdec, blk, kdec_meta = _retention_tables()
    g_mix = norm_mix_g[0][None, :]
    u, y_ret, u_meta = _mixer_call(x, g_mix, w_in[0], cos[N_META:], sin[N_META:], mask, qdec, kdec, bdec, blk,
                                   meta_tokens, cos[:N_META], sin[:N_META], kdec_meta)

    s5_ops = _s5_operators(
        ssm_lambda_re[0], ssm_lambda_im[0], ssm_log_dt[0], ssm_b_re[0], ssm_b_im[0],
        ssm_c_re[0], ssm_c_im[0], ssm_d[0])
    um = u_meta.reshape(S5_BLOCK, SSM_GROUPS, SSM_GROUP).transpose(1, 0, 2).reshape(SSM_GROUPS, 1, S5_LANES)
    um = jnp.pad(um, ((0, 0), (0, 7), (0, 0)))
    y_lo, y_hi = _s5_call(u, um, *s5_ops, w_glu[0].astype(BF16))

    w_r = jnp.concatenate(
        [w_router_group[0].T, w_router_expert[0].transpose(0, 2, 1).reshape(N_EXPERTS, D_MODEL)], axis=0)
    w_r = jnp.pad(w_r, ((0, ROUTE_ROWS - w_r.shape[0]), (0, 0)))
    b_r = jnp.concatenate([b_router_group[0], b_router_expert[0].reshape(-1)])
    b_r = jnp.pad(b_r, (0, ROUTE_ROWS - b_r.shape[0]))[:, None]
    w_r_hi = w_r.astype(BF16)
    w_r = jnp.concatenate([w_r_hi, (w_r - w_r_hi.astype(F32)).astype(BF16)], axis=0)
    tri = jnp.asarray(np.arange(TILE)[:, None] <= np.arange(TILE)[None, :], BF16)

    h, stage, dest, cnt = _proj_call(
        x.reshape(n_tok, D_MODEL), y_lo.reshape(n_tok, LANES), y_hi.reshape(n_tok, LANES),
        y_ret.reshape(n_tok, RET_WIDTH),
        w_out[0], norm_ffn_g[0][None, :], w_r, b_r, tri)
    src, g_step, n_live, back = _sort_tables(cnt[:, :N_GROUPS, 0].astype(I32), n_steps)
    y_sorted = _moe_call(src, g_step, n_live, stage, w_gate[0], w_up[0], w_down[0], n_steps)
    out = _final_call(back, y_sorted, h, dest, norm_final_g[None, :])
    return out.reshape(bsz, seq, D_MODEL)
```

```python
import functools

import jax
import jax.numpy as jnp
import numpy as np
from jax import lax
from jax.experimental import pallas as pl
from jax.experimental.pallas import tpu as pltpu

D_MODEL = 1024
N_META = 16
CHUNK = 64
EPS = 1e-6
SSM_WIDTH = 256
SSM_GROUP = 16
SSM_GROUPS = 16
SSM_STATE = 64
RET_WIDTH = 768
HEAD_DIM = 128
HEADS = 6
HEAD_PAIRS = HEADS // 2
PAIR_DIM = 2 * HEAD_DIM
ROPE_BASE = 10000.0
IN_WIDTH = SSM_WIDTH + 4 * RET_WIDTH
N_GROUPS = 4
EXPERTS_PER_GROUP = 4
N_EXPERTS = 16
EXPERT_FF = 256

S5_BLOCK = 16
S5_LANES = S5_BLOCK * SSM_GROUP
SCAN_ROWS = 8
RET_BLOCK = 256
TILE = 512
MIXER_TILE = 1024
LANES = 128
ROUTE_ROWS = 32
ROW_CHUNK = 256
PROJ_TILES = 2
FINAL_TILES = 4
PIECE = 16
TILE_PIECES = TILE // PIECE
MOE_TILE = 1024
MOE_PIECES = MOE_TILE // PIECE
SORTED_PIECES = TILE_PIECES + N_GROUPS
SORTED_ROWS = SORTED_PIECES * PIECE
EXT_WIDTH = D_MODEL + LANES
INV_LANE = 2 * EXPERTS_PER_GROUP
V7X_VMEM_BYTES = 64 * 1024 * 1024
VMEM_LIMIT = V7X_VMEM_BYTES * 7 // 8

F32 = jnp.float32
BF16 = jnp.bfloat16
FP8 = jnp.float8_e4m3fn
FP8_HEADROOM = 256.0
I32 = jnp.int32


def _dot(a, b):
    return jnp.dot(a, b, preferred_element_type=F32)


def _sigmoid(x):
    return 0.5 * jnp.tanh(0.5 * x) + 0.5


def _sigmoid_exp(x):
    return 1.0 / (1.0 + jnp.exp(-x))


def _rms_norm(x, g):
    return x * lax.rsqrt(jnp.mean(x * x, axis=-1, keepdims=True) + EPS) * g


def _rope(t, cos, sin_signed):
    return t * cos + pltpu.roll(t, HEAD_DIM // 2, axis=1) * sin_signed


def _meta_state(meta_ref, g_ref, w_ref, cos_ref, sin_ref, kdec_ref, blk_ref, u_ref, r0_ref):
    a = _rms_norm(meta_ref[...], g_ref[...]).astype(BF16)
    u_ref[...] = _dot(a, w_ref[:, :SSM_WIDTH])
    k_off = SSM_WIDTH + RET_WIDTH
    v_off = SSM_WIDTH + 2 * RET_WIDTH
    cos = cos_ref[...]
    sin = sin_ref[...]
    for p in range(HEAD_PAIRS):
        k = _dot(a, w_ref[:, k_off + p * PAIR_DIM:k_off + (p + 1) * PAIR_DIM])
        v = _dot(a, w_ref[:, v_off + p * PAIR_DIM:v_off + (p + 1) * PAIR_DIM])
        k = jnp.concatenate([_rope(k[:, :HEAD_DIM], cos, sin), _rope(k[:, HEAD_DIM:], cos, sin)], axis=1)
        kd = (k * kdec_ref[p]).astype(BF16)
        r0_ref[p] = _dot_rows(kd, v.astype(BF16)) * blk_ref[...]


def _dot_rows(a, b):
    return lax.dot_general(a, b, (((0,), (0,)), ((), ())), preferred_element_type=F32)


def _mixer_kernel(x_ref, g_ref, w32_ref, cos_ref, sin_ref, mask_ref, qdec_ref, kdec_ref, bdec_ref, blk_ref,
                  meta_ref, cos_m_ref, sin_m_ref, kdec_m_ref, u_ref, y_ref, um_ref, w_ref, r0_ref, r_ref):
    first_tile = pl.program_id(1) == 0

    @pl.when(jnp.logical_and(pl.program_id(0) == 0, first_tile))
    def _():
        for c in range(0, IN_WIDTH, SSM_WIDTH):
            w_ref[:, c:c + SSM_WIDTH] = w32_ref[:, c:c + SSM_WIDTH].astype(BF16)
        _meta_state(meta_ref, g_ref, w_ref, cos_m_ref, sin_m_ref, kdec_m_ref, blk_ref, um_ref, r0_ref)

    @pl.when(first_tile)
    def _():
        r_ref[...] = r0_ref[...]

    off = SSM_WIDTH
    for b in range(MIXER_TILE // RET_BLOCK):
        bs = slice(b * RET_BLOCK, (b + 1) * RET_BLOCK)
        a = _rms_norm(x_ref[bs, :], g_ref[...]).astype(BF16)
        u_ref[bs, :] = _dot(a, w_ref[:, :SSM_WIDTH])
        q_all = _dot(a, w_ref[:, off:off + RET_WIDTH])
        k_all = _dot(a, w_ref[:, off + RET_WIDTH:off + 2 * RET_WIDTH])
        v_all = _dot(a, w_ref[:, off + 2 * RET_WIDTH:off + 3 * RET_WIDTH])
        gate = _dot(a, w_ref[:, off + 3 * RET_WIDTH:off + 4 * RET_WIDTH])
        cos = cos_ref[bs, :]
        sin = sin_ref[bs, :]
        def scores(h):
            hs = slice(h * HEAD_DIM, (h + 1) * HEAD_DIM)
            q = _rope(q_all[:, hs], cos, sin)
            k = _rope(k_all[:, hs], cos, sin)
            return q, k, _dot_t(q.astype(BF16), k.astype(BF16))

        ahead = scores(0)
        for p in range(HEAD_PAIRS):
            ps = slice(p * PAIR_DIM, (p + 1) * PAIR_DIM)
            pair = [ahead, scores(2 * p + 1)]
            if p + 1 < HEAD_PAIRS:
                ahead = scores(2 * p + 2)
            q = jnp.concatenate([pair[0][0], pair[1][0]], axis=1)
            k = jnp.concatenate([pair[0][1], pair[1][1]], axis=1)
            v = v_all[:, ps].astype(BF16)
            state = r_ref[p]
            cross = _dot((q * qdec_ref[p]).astype(BF16), state.astype(BF16))
            kv = _dot_rows((k * kdec_ref[p]).astype(BF16), v)
            r_ref[p] = state * bdec_ref[p] + kv * blk_ref[...]
            for half, (_, _, s) in enumerate(pair):
                h = 2 * p + half
                hs = slice(h * HEAD_DIM, (h + 1) * HEAD_DIM)
                ls = slice(half * HEAD_DIM, (half + 1) * HEAD_DIM)
                o = _dot((s * mask_ref[h]).astype(BF16), v[:, ls]) + cross[:, ls]
                mu = jnp.mean(o, axis=-1, keepdims=True)
                d = o - mu
                var = jnp.mean(d * d, axis=-1, keepdims=True)
                gt = gate[:, hs]
                y_ref[bs, hs] = (gt * _sigmoid_exp(gt) * d * lax.rsqrt(var + EPS)).astype(BF16)


def _mixer_call(x, g, w_in, cos, sin, mask, qdec, kdec, bdec, blk, meta, cos_m, sin_m, kdec_m):
    bsz, seq, _ = x.shape
    const = lambda a, **kw: pl.BlockSpec(a.shape, lambda b, i: (0,) * a.ndim, **kw)
    return pl.pallas_call(
        _mixer_kernel,
        grid=(bsz, seq // MIXER_TILE),
        in_specs=[
            pl.BlockSpec((None, MIXER_TILE, D_MODEL), lambda b, i: (b, i, 0)),
            const(g),
            const(w_in, pipeline_mode=pl.Buffered(1)),
            pl.BlockSpec((MIXER_TILE, HEAD_DIM), lambda b, i: (i, 0)),
            pl.BlockSpec((MIXER_TILE, HEAD_DIM), lambda b, i: (i, 0)),
            const(mask), const(qdec), const(kdec), const(bdec), const(blk),
            const(meta), const(cos_m), const(sin_m), const(kdec_m),
        ],
        out_specs=(
            pl.BlockSpec((None, MIXER_TILE, SSM_WIDTH), lambda b, i: (b, i, 0)),
            pl.BlockSpec((None, MIXER_TILE, RET_WIDTH), lambda b, i: (b, i, 0)),
            pl.BlockSpec((N_META, SSM_WIDTH), lambda b, i: (0, 0)),
        ),
        out_shape=(jax.ShapeDtypeStruct((bsz, seq, SSM_WIDTH), F32),
                   jax.ShapeDtypeStruct((bsz, seq, RET_WIDTH), BF16),
                   jax.ShapeDtypeStruct((N_META, SSM_WIDTH), F32)),
        scratch_shapes=[pltpu.VMEM((D_MODEL, IN_WIDTH), BF16),
                        pltpu.VMEM((HEAD_PAIRS, PAIR_DIM, PAIR_DIM), F32),
                        pltpu.VMEM((HEAD_PAIRS, PAIR_DIM, PAIR_DIM), F32)],
        compiler_params=pltpu.CompilerParams(
            dimension_semantics=("arbitrary", "arbitrary"), vmem_limit_bytes=VMEM_LIMIT),
        name="mixer_call",
    )(x, g, w_in, cos, sin, mask, qdec, kdec, bdec, blk, meta, cos_m, sin_m, kdec_m)


def _dot_t(a, b):
    return lax.dot_general(a, b, (((1,), (1,)), ((), ())), preferred_element_type=F32)


def _chunk_transpose(arrs):
    n = len(arrs)
    chunk = lax.broadcasted_iota(I32, (1, LANES), 1) // SSM_GROUP
    arrs = list(arrs)
    s = n // 2
    while s:
        keep = (chunk & s) == 0
        nxt = list(arrs)
        for i in range(n):
            if i & s == 0:
                lo, hi = arrs[i], arrs[i + s]
                nxt[i] = jnp.where(keep, lo, pltpu.roll(hi, s * SSM_GROUP, axis=1))
                nxt[i + s] = jnp.where(keep, pltpu.roll(lo, LANES - s * SSM_GROUP, axis=1), hi)
        arrs = nxt
        s //= 2
    return arrs


def _s5_kernel(u_lo_ref, u_hi_ref, um_ref, krow_ref, bmat_ref, cre_ref, cim_ref, ar_ref, ai_ref, wglu_ref,
               y_lo_ref, y_hi_ref, t0_ref, ug_ref, yg_ref):
    n_blocks = u_lo_ref.shape[0] // S5_BLOCK

    @pl.when(pl.program_id(0) == 0)
    def _():
        lane = lax.broadcasted_iota(I32, (SSM_GROUP, S5_LANES), 1)
        for g in range(SSM_GROUPS):
            k0 = krow_ref[g]
            for a in range(S5_BLOCK):
                blk = k0 if a == 0 else jnp.where(lane >= a * SSM_GROUP,
                                                  pltpu.roll(k0, a * SSM_GROUP, axis=1), 0.0)
                t0_ref[g, a * SSM_GROUP:(a + 1) * SSM_GROUP, :] = blk.astype(BF16)

    per_tile = LANES // SSM_GROUP
    for half, uh_ref in enumerate((u_lo_ref, u_hi_ref)):
        for t in range(S5_BLOCK // per_tile):
            words = [pltpu.bitcast(uh_ref[pl.ds(t * per_tile + k, n_blocks, stride=S5_BLOCK), :].astype(BF16),
                                   jnp.uint32) for k in range(per_tile)]
            for m, w in enumerate(_chunk_transpose(words)):
                ug_ref[half * per_tile + m, :, t * LANES:(t + 1) * LANES] = pltpu.bitcast(w, BF16)

    re, im, re0, im0 = [], [], [], []
    for p in range(SSM_GROUPS // 2):
        g0, g1 = 2 * p, 2 * p + 1
        v = _dot(ug_ref[g0], bmat_ref[g0]) + _dot(ug_ref[g1], bmat_ref[g1])
        v0 = (_dot(um_ref[g0].astype(BF16), bmat_ref[g0]) + _dot(um_ref[g1].astype(BF16), bmat_ref[g1]))[0:1]
        re.append(v[:, :LANES])
        im.append(v[:, LANES:])
        re0.append(v0[:, :LANES])
        im0.append(v0[:, LANES:])
    re, im, re0, im0 = (jnp.concatenate(parts, axis=1) for parts in (re, im, re0, im0))

    row = lax.broadcasted_iota(I32, re.shape, 0)
    ar, ai = ar_ref[0:1], ai_ref[0:1]
    re = re + jnp.where(row == 0, ar * re0 - ai * im0, 0.0)
    im = im + jnp.where(row == 0, ar * im0 + ai * re0, 0.0)
    d = 1
    while d < SCAN_ROWS:
        ar, ai = ar_ref[d - 1:d], ai_ref[d - 1:d]
        inside = row % SCAN_ROWS >= d
        sre = jnp.where(inside, pltpu.roll(re, d, axis=0), 0.0)
        sim = jnp.where(inside, pltpu.roll(im, d, axis=0), 0.0)
        re, im = re + ar * sre - ai * sim, im + ar * sim + ai * sre
        d *= 2
    ar, ai = ar_ref[...], ai_ref[...]
    re_tiles, im_tiles = [re[:SCAN_ROWS]], [im[:SCAN_ROWS]]
    for r in range(1, n_blocks // SCAN_ROWS):
        cre_ = re_tiles[-1][SCAN_ROWS - 1:SCAN_ROWS]
        cim_ = im_tiles[-1][SCAN_ROWS - 1:SCAN_ROWS]
        rs = slice(r * SCAN_ROWS, (r + 1) * SCAN_ROWS)
        re_tiles.append(re[rs] + ar * cre_ - ai * cim_)
        im_tiles.append(im[rs] + ar * cim_ + ai * cre_)
    re = jnp.concatenate(re_tiles, axis=0)
    im = jnp.concatenate(im_tiles, axis=0)
    pre = jnp.where(row == 0, re0, pltpu.roll(re, 1, axis=0)).astype(BF16)
    pim = jnp.where(row == 0, im0, pltpu.roll(im, 1, axis=0)).astype(BF16)

    for g in range(SSM_GROUPS):
        ps = slice((g // 2) * LANES, (g // 2 + 1) * LANES)
        yg_ref[g] = (_dot(ug_ref[g], t0_ref[g]) + _dot_t(pre[:, ps], cre_ref[g])
                     + _dot_t(pim[:, ps], cim_ref[g])).astype(BF16)

    wglu = wglu_ref[...]
    for t in range(S5_BLOCK // per_tile):
        ts = slice(t * LANES, (t + 1) * LANES)
        halves = [_chunk_transpose([pltpu.bitcast(yg_ref[half * per_tile + m, :, ts], jnp.uint32)
                                    for m in range(per_tile)])
                  for half in range(SSM_GROUPS // per_tile)]
        for k in range(per_tile):
            y = jnp.concatenate([pltpu.bitcast(h[k], BF16) for h in halves], axis=1).astype(F32)
            y = jax.nn.gelu(y, approximate=True)
            y = y * _sigmoid(_dot(y.astype(BF16), wglu))
            i = t * per_tile + k
            y_lo_ref[pl.ds(i, n_blocks, stride=S5_BLOCK), :] = y[:, :LANES]
            y_hi_ref[pl.ds(i, n_blocks, stride=S5_BLOCK), :] = y[:, LANES:]


def _s5_call(u, um, krow, bmat, cre, cim, ar, ai, wglu):
    bsz, seq, _ = u.shape
    n_blocks = seq // S5_BLOCK
    const = lambda a: pl.BlockSpec(a.shape, lambda b: (0,) * a.ndim)
    return pl.pallas_call(
        _s5_kernel,
        grid=(bsz,),
        in_specs=[pl.BlockSpec((None, seq, LANES), lambda b: (b, 0, 0)),
                  pl.BlockSpec((None, seq, LANES), lambda b: (b, 0, 1)),
                  const(um), const(krow), const(bmat), const(cre), const(cim), const(ar), const(ai),
                  const(wglu)],
        out_specs=(pl.BlockSpec((None, seq, LANES), lambda b: (b, 0, 0)),
                   pl.BlockSpec((None, seq, LANES), lambda b: (b, 0, 0))),
        out_shape=(jax.ShapeDtypeStruct((bsz, seq, LANES), F32),
                   jax.ShapeDtypeStruct((bsz, seq, LANES), F32)),
        scratch_shapes=[pltpu.VMEM((SSM_GROUPS, S5_LANES, S5_LANES), BF16),
                        pltpu.VMEM((SSM_GROUPS, n_blocks, S5_LANES), BF16),
                        pltpu.VMEM((SSM_GROUPS, n_blocks, S5_LANES), BF16)],
        compiler_params=pltpu.CompilerParams(
            dimension_semantics=("arbitrary",), vmem_limit_bytes=VMEM_LIMIT),
        name="s5_call",
    )(u, u, um, krow, bmat, cre, cim, ar, ai, wglu)


def _s5_operators(lam_re, lam_im, log_dt, b_re, b_im, c_re, c_im, d_skip):
    n_groups = lam_re.shape[0]
    lam = lax.complex(lam_re, lam_im)
    lam_dt = lam * jnp.exp(log_dt)[:, None]
    lam_bar = jnp.exp(lam_dt)
    b_bar = ((lam_bar - 1.0) / lam)[..., None] * lax.complex(b_re, b_im)
    c = lax.complex(c_re, c_im)
    tau = jnp.arange(S5_BLOCK + 1, dtype=F32)
    pows = jnp.exp(lam_dt[None] * tau[:, None, None])
    kern = jnp.real(jnp.einsum('ghp,tgp,gpk->gkth', c, pows[:S5_BLOCK], b_bar))
    skip = (jnp.eye(SSM_GROUP, dtype=F32)[None, :, None, :] * d_skip[:, None, None, :]
            * (tau[:S5_BLOCK] == 0).astype(F32)[None, None, :, None])
    krow = (kern + skip).reshape(n_groups, SSM_GROUP, S5_LANES)
    even = (jnp.arange(n_groups) % 2 == 0)[:, None, None]
    pair_pad = lambda m: jnp.concatenate([jnp.where(even, m, 0.0), jnp.where(even, 0.0, m)], axis=-1)
    bm = (pows[S5_BLOCK - 1 - jnp.arange(S5_BLOCK)].transpose(1, 0, 2)[:, :, None, :]
          * b_bar.transpose(0, 2, 1)[:, None, :, :]).reshape(n_groups, S5_LANES, SSM_STATE)
    bmat = jnp.concatenate([pair_pad(jnp.real(bm)), pair_pad(jnp.imag(bm))], axis=-1)
    cm = (pows[1:].transpose(1, 0, 2)[:, :, None, :] * c[:, None, :, :]).reshape(n_groups, S5_LANES, SSM_STATE)
    cre, cim = pair_pad(jnp.real(cm)), pair_pad(-jnp.imag(cm))
    step = S5_BLOCK * (1.0 + jnp.arange(SCAN_ROWS, dtype=F32))
    adec = jnp.exp(lam_dt[None, :, :] * step[:, None, None]).reshape(SCAN_ROWS, n_groups * SSM_STATE)
    return krow, bmat.astype(BF16), cre.astype(BF16), cim.astype(BF16), jnp.real(adec), jnp.imag(adec)


def _first_hit(values, target):
    hits, taken = [], None
    for v in values:
        hit = (v >= target) if taken is None else jnp.logical_and(v >= target, jnp.logical_not(taken))
        taken = hit if taken is None else jnp.logical_or(taken, hit)
        hits.append(hit)
    return hits


def _stack_rows(rows, n_rows):
    idx = lax.broadcasted_iota(I32, (n_rows, rows[0].shape[1]), 0)
    out = jnp.zeros((n_rows, rows[0].shape[1]), F32)
    for k, r in enumerate(rows):
        out = jnp.where(idx == k, r, out)
    return out


def _project(rs, x_ref, ys_lo_ref, ys_hi_ref, yr_ref, wout_ref, h_ref):
    ys = jnp.concatenate([ys_lo_ref[rs, :], ys_hi_ref[rs, :]], axis=1).astype(BF16)
    h = x_ref[rs, :] + _dot(ys, wout_ref[:SSM_WIDTH, :]) + _dot(yr_ref[rs, :], wout_ref[SSM_WIDTH:, :])
    h_ref[rs, :] = h.astype(BF16)
    return h


def _route(lt, tri_ref):
    gl = [lt[g:g + 1, :] for g in range(N_GROUPS)]
    gmax = functools.reduce(jnp.maximum, gl)
    g_w = 1.0 / functools.reduce(lambda a, b: a + b, [jnp.exp(l - gmax) for l in gl])
    sel = _first_hit(gl, gmax)
    ev = []
    for e in range(EXPERTS_PER_GROUP):
        acc = jnp.zeros_like(gmax)
        for g in range(N_GROUPS):
            k = N_GROUPS + g * EXPERTS_PER_GROUP + e
            acc = jnp.where(sel[g], lt[k:k + 1, :], acc)
        ev.append(acc)
    m1 = functools.reduce(jnp.maximum, ev)
    first = _first_hit(ev, m1)
    rest = [jnp.where(f, -jnp.inf, v) for f, v in zip(first, ev)]
    m2 = functools.reduce(jnp.maximum, rest)
    second = _first_hit(rest, m2)
    e2 = jnp.exp(m2 - m1)
    w1 = g_w / (1.0 + e2)
    w2 = e2 * w1
    combine = [jnp.where(f, w1, 0.0) + jnp.where(s, w2, 0.0) for f, s in zip(first, second)]

    sel_f = [jnp.where(s, 1.0, 0.0) for s in sel]
    incl = _dot(_stack_rows(sel_f, 8).astype(BF16), tri_ref[...])
    dest = jnp.zeros_like(gmax)
    seg_start = jnp.zeros((1, 1), F32)
    counts = []
    for g in range(N_GROUPS):
        run = incl[g:g + 1, :]
        cnt = run[:, TILE - 1:TILE]
        counts.append(cnt)
        dest = dest + sel_f[g] * (seg_start + run - 1.0)
        seg_start = seg_start + PIECE * jnp.floor((cnt + (PIECE - 1.0)) * (1.0 / PIECE))
    return combine, dest, counts


def _proj_kernel(x_ref, ys_lo_ref, ys_hi_ref, yr_ref, wout32_ref, g_ref, wr_ref, br_ref,
                 tri_ref, h_ref, stage_ref, dest_ref, cnt_ref, wout_ref):
    @pl.when(pl.program_id(0) == 0)
    def _():
        wout_ref[...] = wout32_ref[...].astype(BF16)

    tiles = range(PROJ_TILES)
    proj_refs = (x_ref, ys_lo_ref, ys_hi_ref, yr_ref, wout_ref, h_ref)
    chunks = [[slice(tile * TILE + b * ROW_CHUNK, tile * TILE + (b + 1) * ROW_CHUNK)
               for b in range(TILE // ROW_CHUNK)] for tile in tiles]
    h_parts = [[_project(rs, *proj_refs) for rs in chunks[tile]] for tile in tiles]

    logits, staged = [], []
    for tile in tiles:
        parts, rows, inv = [], [], []
        for h in h_parts[tile]:
            t = _rms_norm(h, g_ref[...])
            parts.append(t.astype(BF16))
            peak = jnp.maximum(jnp.max(jnp.abs(t), axis=1, keepdims=True), 1e-30)
            rows.append((t * (FP8_HEADROOM / peak)).astype(BF16))
            inv.append(peak * (1.0 / FP8_HEADROOM))
        staged.append((jnp.concatenate(rows, axis=0), jnp.concatenate(inv, axis=0)))
        lt_parts = []
        for t_hi in parts:
            both = _dot_t(wr_ref[...], t_hi)
            lt_parts.append(both[:ROUTE_ROWS] + both[ROUTE_ROWS:])
        logits.append(jnp.concatenate(lt_parts, axis=1) + br_ref[...])

    routed = [_route(logits[tile], tri_ref) for tile in tiles]

    for tile in tiles:
        combine, dest, counts = routed[tile]
        dest_ref[tile] = _stack_rows([dest], 8)
        cnt_ref[tile] = _stack_rows([c + jnp.zeros((1, LANES), F32) for c in counts], 8)
        perm = jnp.where(lax.broadcasted_iota(I32, (SORTED_ROWS, TILE), 0) == dest.astype(I32),
                         1.0, 0.0).astype(BF16)
        c_hi = [c.astype(BF16).astype(F32) for c in combine]
        c_lo = [c - hi for c, hi in zip(combine, c_hi)]
        cw = _stack_rows(c_hi + c_lo, LANES).T
        rows, inv = staged[tile]
        inv_hi = inv.astype(BF16).astype(F32)
        lane = lax.broadcasted_iota(I32, (1, LANES), 1)
        cw = jnp.where(lane == INV_LANE, inv_hi, jnp.where(lane == INV_LANE + 1, inv - inv_hi, cw))
        t_ext = jnp.concatenate([rows, cw.astype(BF16)], axis=1)
        stage_ref[tile * SORTED_ROWS:(tile + 1) * SORTED_ROWS, :] = _dot(perm, t_ext).astype(BF16)


def _proj_call(x, ys_lo, ys_hi, yr, wout, g, wr, br, tri):
    n_tok = x.shape[0]
    n_tiles = n_tok // TILE
    const = lambda *shape: pl.BlockSpec(shape, lambda i: (0,) * len(shape))
    rows = lambda width: pl.BlockSpec((PROJ_TILES * TILE, width), lambda i: (i, 0))
    return pl.pallas_call(
        _proj_kernel,
        grid=(n_tiles // PROJ_TILES,),
        in_specs=[
            rows(D_MODEL), rows(LANES), rows(LANES), rows(RET_WIDTH),
            pl.BlockSpec((D_MODEL, D_MODEL), lambda i: (0, 0), pipeline_mode=pl.Buffered(1)), const(1, D_MODEL),
            const(2 * ROUTE_ROWS, D_MODEL), const(ROUTE_ROWS, 1), const(TILE, TILE),
        ],
        out_specs=(rows(D_MODEL),
                   pl.BlockSpec((PROJ_TILES * SORTED_ROWS, EXT_WIDTH), lambda i: (i, 0)),
                   pl.BlockSpec((PROJ_TILES, 8, TILE), lambda i: (i, 0, 0)),
                   pl.BlockSpec((PROJ_TILES, 8, LANES), lambda i: (i, 0, 0))),
        scratch_shapes=[pltpu.VMEM((D_MODEL, D_MODEL), BF16)],
        out_shape=(jax.ShapeDtypeStruct((n_tok, D_MODEL), BF16),
                   jax.ShapeDtypeStruct((n_tiles * SORTED_ROWS, EXT_WIDTH), BF16),
                   jax.ShapeDtypeStruct((n_tiles, 8, TILE), F32),
                   jax.ShapeDtypeStruct((n_tiles, 8, LANES), F32)),
        compiler_params=pltpu.CompilerParams(
            dimension_semantics=("arbitrary",), vmem_limit_bytes=VMEM_LIMIT),
        name="proj_call",
    )(x, ys_lo, ys_hi, yr, wout, g, wr, br, tri)


def _sort_tables(cnt, n_steps):
    n_tiles = cnt.shape[0]
    npc = (cnt + PIECE - 1) // PIECE
    seg = jnp.cumsum(npc, axis=1) - npc
    before = jnp.cumsum(npc, axis=0) - npc
    n_tile_g = (jnp.sum(npc, axis=0) + MOE_PIECES - 1) // MOE_PIECES
    t_off = jnp.cumsum(n_tile_g) - n_tile_g
    j = jnp.arange(SORTED_PIECES, dtype=I32)[None, :, None]
    in_g = jnp.logical_and(j >= seg[:, None, :], j < (seg + npc)[:, None, :])
    pos = jnp.sum(jnp.where(in_g, MOE_PIECES * t_off[None, None, :] + before[:, None, :] + j - seg[:, None, :], 0),
                  axis=-1)
    valid = jnp.any(in_g, axis=-1)
    stage_piece = jnp.arange(n_tiles * SORTED_PIECES, dtype=I32).reshape(n_tiles, SORTED_PIECES)
    n_slots = n_steps * MOE_PIECES
    src = jnp.zeros((n_slots,), I32).at[jnp.where(valid, pos, n_slots).reshape(-1)].set(
        stage_piece.reshape(-1), mode='drop')
    steps = jnp.arange(n_steps, dtype=I32)
    g_step = jnp.minimum(jnp.sum(steps[:, None] >= (t_off + n_tile_g)[None, :], axis=1), N_GROUPS - 1)
    n_live = jnp.sum(n_tile_g).reshape(1)
    back = jnp.where(valid, pos, 0).reshape(-1)
    return src, g_step.astype(I32), n_live.astype(I32), back.astype(I32)


def _piece_copy(src_ref, piece, buf_ref, slot, j, sem_ref):
    start = piece * PIECE if isinstance(piece, int) else pl.multiple_of(piece * PIECE, PIECE)
    return pltpu.make_async_copy(src_ref.at[pl.ds(start, PIECE)],
                                 buf_ref.at[slot, pl.ds(j * PIECE, PIECE)], sem_ref.at[slot])


def _fetch_pieces(table_ref, src_ref, buf_ref, sem_ref, row, slot, n_pieces):
    for j in range(n_pieces):
        _piece_copy(src_ref, table_ref[row * n_pieces + j], buf_ref, slot, j, sem_ref).start()


def _wait_pieces(src_ref, buf_ref, sem_ref, slot, n_pieces):
    for j in range(n_pieces):
        _piece_copy(src_ref, 0, buf_ref, slot, j, sem_ref).wait()


def _gather_pieces(table_ref, src_ref, buf_ref, sem_ref, n_pieces):
    step = pl.program_id(0)
    last = pl.num_programs(0) - 1
    args = (src_ref, buf_ref, sem_ref)

    @pl.when(step == 0)
    def _():
        _fetch_pieces(table_ref, *args, step, 0, n_pieces)

    _wait_pieces(*args, step % 2, n_pieces)

    def fetch_next():
        _fetch_pieces(table_ref, *args, jnp.minimum(step + 1, last), (step + 1) % 2, n_pieces)

    def drain():
        @pl.when(step == last)
        def _():
            _wait_pieces(*args, (step + 1) % 2, n_pieces)

    return fetch_next, drain


def _to_fp8(x, headroom=FP8_HEADROOM):
    peak = jnp.maximum(jnp.max(jnp.abs(x), axis=(0, 1), keepdims=True), 1e-30)
    return (x * (headroom / peak)).astype(FP8), peak * (1.0 / headroom)


def _moe_kernel(src_ref, gstep_ref, nlive_ref, stage_ref, wg_ref, wu_ref, wd_ref, y_ref,
                buf_ref, sem_ref, wgb_ref, wub_ref, wdb_ref, inv_ref):
    step = pl.program_id(0)
    fetch_next, drain = _gather_pieces(src_ref, stage_ref, buf_ref, sem_ref, MOE_PIECES)

    @pl.when(jnp.logical_or(step == 0, gstep_ref[step] != gstep_ref[jnp.maximum(step - 1, 0)]))
    def _():
        ones = jnp.ones((1, LANES), F32)
        for e in range(EXPERTS_PER_GROUP):
            wgb_ref[e], inv = _to_fp8(wg_ref[e])
            inv_ref[e:e + 1, :] = inv * ones
            wub_ref[e], inv = _to_fp8(wu_ref[e])
            inv_ref[EXPERTS_PER_GROUP + e:EXPERTS_PER_GROUP + e + 1, :] = inv * ones
        peak = functools.reduce(jnp.maximum, [jnp.max(jnp.abs(wd_ref[e]), axis=(0, 1), keepdims=True)
                                              for e in range(EXPERTS_PER_GROUP)])
        peak = jnp.maximum(peak, 1e-30)
        for e in range(EXPERTS_PER_GROUP):
            wdb_ref[e * EXPERT_FF:(e + 1) * EXPERT_FF, :] = (wd_ref[e] * (FP8_HEADROOM / peak)).astype(FP8)
        inv_ref[2 * EXPERTS_PER_GROUP:2 * EXPERTS_PER_GROUP + 1, :] = peak * (1.0 / FP8_HEADROOM) * ones

    @pl.when(step < nlive_ref[0])
    def _():
        slot = step % 2
        fetch_next()

        def up(rs):
            t = buf_ref[slot, rs, :D_MODEL].astype(FP8)
            t_inv = (buf_ref[slot, rs, D_MODEL + INV_LANE:D_MODEL + INV_LANE + 1].astype(F32)
                     + buf_ref[slot, rs, D_MODEL + INV_LANE + 1:D_MODEL + INV_LANE + 2].astype(F32))
            out = []
            for e in range(EXPERTS_PER_GROUP):
                g_inv = inv_ref[e:e + 1, 0:1] * t_inv
                u_inv = inv_ref[EXPERTS_PER_GROUP + e:EXPERTS_PER_GROUP + e + 1, 0:1] * t_inv
                out.append((_dot(t, wgb_ref[e]) * g_inv, _dot(t, wub_ref[e]), u_inv))
            return out

        def down(rs, hidden):
            cw = buf_ref[slot, rs, D_MODEL:].astype(F32)
            acts = []
            for e, (hg, hu, u_inv) in enumerate(hidden):
                c = (cw[:, e:e + 1] + cw[:, EXPERTS_PER_GROUP + e:EXPERTS_PER_GROUP + e + 1]) * u_inv
                acts.append(hg * _sigmoid(hg) * hu * c)
            act, a_inv = _to_fp8(jnp.concatenate(acts, axis=1))
            d_inv = inv_ref[2 * EXPERTS_PER_GROUP:2 * EXPERTS_PER_GROUP + 1, 0:1] * a_inv
            y_ref[rs, :] = (_dot(act, wdb_ref[...]) * d_inv).astype(BF16)

        chunks = [slice(b * ROW_CHUNK, (b + 1) * ROW_CHUNK) for b in range(MOE_TILE // ROW_CHUNK)]
        hidden = [up(rs) for rs in chunks]
        for rs, hid in zip(chunks, hidden):
            down(rs, hid)

    @pl.when(step >= nlive_ref[0])
    def _():
        fetch_next()
        y_ref[...] = jnp.zeros_like(y_ref)

    drain()


def _moe_call(src, g_step, n_live, stage, wg, wu, wd, n_steps):
    grp = lambda shape: pl.BlockSpec((EXPERTS_PER_GROUP,) + shape, lambda s, src, gs, nl: (gs[s], 0, 0))
    return pl.pallas_call(
        _moe_kernel,
        grid_spec=pltpu.PrefetchScalarGridSpec(
            num_scalar_prefetch=3,
            grid=(n_steps,),
            in_specs=[pl.BlockSpec(memory_space=pl.ANY),
                      grp((D_MODEL, EXPERT_FF)), grp((D_MODEL, EXPERT_FF)), grp((EXPERT_FF, D_MODEL))],
            out_specs=pl.BlockSpec((MOE_TILE, D_MODEL), lambda s, src, gs, nl: (s, 0)),
            scratch_shapes=[pltpu.VMEM((2, MOE_TILE, EXT_WIDTH), BF16), pltpu.SemaphoreType.DMA((2,)),
                            pltpu.VMEM((EXPERTS_PER_GROUP, D_MODEL, EXPERT_FF), FP8),
                            pltpu.VMEM((EXPERTS_PER_GROUP, D_MODEL, EXPERT_FF), FP8),
                            pltpu.VMEM((EXPERTS_PER_GROUP * EXPERT_FF, D_MODEL), FP8),
                            pltpu.VMEM((16, LANES), F32)],
        ),
        out_shape=jax.ShapeDtypeStruct((n_steps * MOE_TILE, D_MODEL), BF16),
        compiler_params=pltpu.CompilerParams(
            dimension_semantics=("arbitrary",), vmem_limit_bytes=VMEM_LIMIT),
        name="moe_call",
    )(src, g_step, n_live, stage, wg, wu, wd)


def _final_kernel(back_ref, ysort_ref, h_ref, dest_ref, g_ref, o_ref, buf_ref, sem_ref):
    step = pl.program_id(0)
    fetch_next, drain = _gather_pieces(back_ref, ysort_ref, buf_ref, sem_ref, FINAL_TILES * SORTED_PIECES)
    fetch_next()
    for t in range(FINAL_TILES):
        dest = _stack_rows([dest_ref[t, 0:1, :]], LANES).T[:, 0:1].astype(I32)
        unperm = jnp.where(lax.broadcasted_iota(I32, (TILE, SORTED_ROWS), 1) == dest, 1.0, 0.0).astype(BF16)
        rows = buf_ref[step % 2, t * SORTED_ROWS:(t + 1) * SORTED_ROWS, :]
        for b in range(TILE // ROW_CHUNK):
            rs = slice(b * ROW_CHUNK, (b + 1) * ROW_CHUNK)
            os = slice(t * TILE + b * ROW_CHUNK, t * TILE + (b + 1) * ROW_CHUNK)
            o_ref[os, :] = _rms_norm(h_ref[os, :].astype(F32) + _dot(unperm[rs], rows), g_ref[...])
    drain()


def _final_call(back, ysort, h, dest, g):
    n_tok = h.shape[0]
    return pl.pallas_call(
        _final_kernel,
        grid_spec=pltpu.PrefetchScalarGridSpec(
            num_scalar_prefetch=1,
            grid=(n_tok // (FINAL_TILES * TILE),),
            in_specs=[pl.BlockSpec(memory_space=pl.ANY),
                      pl.BlockSpec((FINAL_TILES * TILE, D_MODEL), lambda i, back: (i, 0)),
                      pl.BlockSpec((FINAL_TILES, 8, TILE), lambda i, back: (i, 0, 0)),
                      pl.BlockSpec((1, D_MODEL), lambda i, back: (0, 0))],
            out_specs=pl.BlockSpec((FINAL_TILES * TILE, D_MODEL), lambda i, back: (i, 0)),
            scratch_shapes=[pltpu.VMEM((2, FINAL_TILES * SORTED_ROWS, D_MODEL), BF16),
                            pltpu.SemaphoreType.DMA((2,))],
        ),
        out_shape=jax.ShapeDtypeStruct((n_tok, D_MODEL), F32),
        compiler_params=pltpu.CompilerParams(
            dimension_semantics=("arbitrary",), vmem_limit_bytes=VMEM_LIMIT),
        name="final_call",
    )(back, ysort, h, dest, g)


def _rope_tables(length):
    pos = np.arange(length, dtype=np.float32)
    inv_freq = np.float32(ROPE_BASE) ** (-np.arange(0, HEAD_DIM, 2, dtype=np.float32) / np.float32(HEAD_DIM))
    ang = pos[:, None] * inv_freq[None, :]
    cos, sin = np.cos(ang), np.sin(ang)
    return np.concatenate([cos, cos], axis=-1), np.concatenate([-sin, sin], axis=-1)


def _retention_tables():
    f32 = np.float32
    gamma = f32(1.0) - f32(2.0) ** (f32(-5.0) - np.arange(HEADS, dtype=f32))
    log_g = np.log(gamma)[:, None, None]
    scale = f32(HEAD_DIM ** -0.5)
    idx = np.arange(RET_BLOCK)
    dist = np.abs(idx[:, None] - idx[None, :]).astype(f32)
    visible = (idx[None, :] // CHUNK) <= (idx[:, None] // CHUNK)
    mask = np.where(visible[None], np.exp(log_g * dist[None]), f32(0.0)) * scale
    ones = np.ones((1, 1, HEAD_DIM), f32)
    idx_f = idx.astype(f32)[None, :, None]
    pair = lambda a: a.reshape(HEAD_PAIRS, 2, a.shape[1], HEAD_DIM).transpose(0, 2, 1, 3).reshape(
        HEAD_PAIRS, a.shape[1], PAIR_DIM)
    qdec = pair(np.exp(log_g * (idx_f + f32(1.0))) * ones)
    kdec = pair(np.exp(log_g * (f32(RET_BLOCK - 1.0) - idx_f)) * scale * ones)
    meta_idx = np.arange(N_META, dtype=f32)[None, :, None]
    kdec_meta = pair(np.exp(log_g * (f32(N_META - 1.0) - meta_idx)) * scale * ones)
    blk = np.kron(np.eye(2, dtype=f32), np.ones((HEAD_DIM, HEAD_DIM), f32))
    bdec = np.exp(log_g * f32(RET_BLOCK)).reshape(HEAD_PAIRS, 2)
    bdec = np.stack([np.kron(np.diag(b), np.ones((HEAD_DIM, HEAD_DIM), f32)) for b in bdec])
    return tuple(a.astype(f32) for a in (mask, qdec, kdec, bdec, blk, kdec_meta))


def kernel(x, meta_tokens, norm_mix_g, w_in, ssm_lambda_re, ssm_lambda_im, ssm_log_dt, ssm_b_re, ssm_b_im, ssm_c_re, ssm_c_im, ssm_d, w_glu, w_out, norm_ffn_g, w_router_group, b_router_group, w_router_expert, b_router_expert, w_gate, w_up, w_down, norm_final_g):
    bsz, seq, _ = x.shape
    assert seq % TILE == 0 and seq % MIXER_TILE == 0 and MIXER_TILE % RET_BLOCK == 0 and RET_BLOCK % CHUNK == 0
    n_blocks = seq // S5_BLOCK
    assert n_blocks % SCAN_ROWS == 0
    n_tok = bsz * seq
    assert n_tok % (FINAL_TILES * TILE) == 0 and n_tok % (PROJ_TILES * TILE) == 0
    n_tiles = n_tok // TILE
    n_steps = -(-n_tiles * (TILE_PIECES + N_GROUPS - 1) // MOE_PIECES) + N_GROUPS

    cos, sin = _rope_tables(N_META + seq)
    mask, qdec, kdec, bdec, blk, kdec_meta = _retention_tables()
    g_mix = norm_mix_g[0][None, :]
    u, y_ret, u_meta = _mixer_call(x, g_mix, w_in[0], cos[N_META:], sin[N_META:], mask, qdec, kdec, bdec, blk,
                                   meta_tokens, cos[:N_META], sin[:N_META], kdec_meta)

    s5_ops = _s5_operators(
        ssm_lambda_re[0], ssm_lambda_im[0], ssm_log_dt[0], ssm_b_re[0], ssm_b_im[0],
        ssm_c_re[0], ssm_c_im[0], ssm_d[0])
    um = u_meta.reshape(S5_BLOCK, SSM_GROUPS, SSM_GROUP).transpose(1, 0, 2).reshape(SSM_GROUPS, 1, S5_LANES)
    um = jnp.pad(um, ((0, 0), (0, 7), (0, 0)))
    y_lo, y_hi = _s5_call(u, um, *s5_ops, w_glu[0].astype(BF16))

    w_r = jnp.concatenate(
        [w_router_group[0].T, w_router_expert[0].transpose(0, 2, 1).reshape(N_EXPERTS, D_MODEL)], axis=0)
    w_r = jnp.pad(w_r, ((0, ROUTE_ROWS - w_r.shape[0]), (0, 0)))
    b_r = jnp.concatenate([b_router_group[0], b_router_expert[0].reshape(-1)])
    b_r = jnp.pad(b_r, (0, ROUTE_ROWS - b_r.shape[0]))[:, None]
    w_r_hi = w_r.astype(BF16)
    w_r = jnp.concatenate([w_r_hi, (w_r - w_r_hi.astype(F32)).astype(BF16)], axis=0)
    tri = jnp.asarray(np.arange(TILE)[:, None] <= np.arange(TILE)[None, :], BF16)

    h, stage, dest, cnt = _proj_call(
        x.reshape(n_tok, D_MODEL), y_lo.reshape(n_tok, LANES), y_hi.reshape(n_tok, LANES),
        y_ret.reshape(n_tok, RET_WIDTH),
        w_out[0], norm_ffn_g[0][None, :], w_r, b_r, tri)
    src, g_step, n_live, back = _sort_tables(cnt[:, :N_GROUPS, 0].astype(I32), n_steps)
    y_sorted = _moe_call(src, g_step, n_live, stage, w_gate[0], w_up[0], w_down[0], n_steps)
    out = _final_call(back, y_sorted, h, dest, norm_final_g[None, :])
    return out.reshape(bsz, seq, D_MODEL)
```

```python
import functools

import jax
import jax.numpy as jnp
import numpy as np
from jax import lax
from jax.experimental import pallas as pl
from jax.experimental.pallas import tpu as pltpu

D_MODEL = 1024
N_META = 16
CHUNK = 64
EPS = 1e-6
SSM_WIDTH = 256
SSM_GROUP = 16
SSM_GROUPS = 16
SSM_STATE = 64
RET_WIDTH = 768
HEAD_DIM = 128
HEADS = 6
HEAD_PAIRS = HEADS // 2
PAIR_DIM = 2 * HEAD_DIM
ROPE_BASE = 10000.0
IN_WIDTH = SSM_WIDTH + 4 * RET_WIDTH
N_GROUPS = 4
EXPERTS_PER_GROUP = 4
N_EXPERTS = 16
EXPERT_FF = 256

S5_BLOCK = 16
S5_LANES = S5_BLOCK * SSM_GROUP
SCAN_ROWS = 8
RET_BLOCK = 256
TILE = 512
MIXER_TILE = 1024
LANES = 128
ROUTE_ROWS = 32
ROW_CHUNK = 256
PROJ_TILES = 2
FINAL_TILES = 4
PIECE = 16
TILE_PIECES = TILE // PIECE
MOE_TILE = 1024
MOE_PIECES = MOE_TILE // PIECE
SORTED_PIECES = TILE_PIECES + N_GROUPS
SORTED_ROWS = SORTED_PIECES * PIECE
EXT_WIDTH = D_MODEL + LANES
INV_LANE = 2 * EXPERTS_PER_GROUP
V7X_VMEM_BYTES = 64 * 1024 * 1024
VMEM_LIMIT = V7X_VMEM_BYTES * 7 // 8

F32 = jnp.float32
BF16 = jnp.bfloat16
FP8 = jnp.float8_e4m3fn
FP8_HEADROOM = 256.0
I32 = jnp.int32


def _dot(a, b):
    return jnp.dot(a, b, preferred_element_type=F32)


def _sigmoid(x):
    return 0.5 * jnp.tanh(0.5 * x) + 0.5


def _sigmoid_exp(x):
    return 1.0 / (1.0 + jnp.exp(-x))


def _rms_norm(x, g):
    return x * lax.rsqrt(jnp.mean(x * x, axis=-1, keepdims=True) + EPS) * g


def _rope(t, cos, sin_signed):
    return t * cos + pltpu.roll(t, HEAD_DIM // 2, axis=1) * sin_signed


def _meta_state(meta_ref, g_ref, w_ref, cos_ref, sin_ref, kdec_ref, blk_ref, u_ref, r0_ref):
    a = _rms_norm(meta_ref[...], g_ref[...]).astype(BF16)
    u_ref[...] = _dot(a, w_ref[:, :SSM_WIDTH])
    k_off = SSM_WIDTH + RET_WIDTH
    v_off = SSM_WIDTH + 2 * RET_WIDTH
    cos = cos_ref[...]
    sin = sin_ref[...]
    for p in range(HEAD_PAIRS):
        k = _dot(a, w_ref[:, k_off + p * PAIR_DIM:k_off + (p + 1) * PAIR_DIM])
        v = _dot(a, w_ref[:, v_off + p * PAIR_DIM:v_off + (p + 1) * PAIR_DIM])
        k = jnp.concatenate([_rope(k[:, :HEAD_DIM], cos, sin), _rope(k[:, HEAD_DIM:], cos, sin)], axis=1)
        kd = (k * kdec_ref[p]).astype(BF16)
        r0_ref[p] = _dot_rows(kd, v.astype(BF16)) * blk_ref[...]


def _dot_rows(a, b):
    return lax.dot_general(a, b, (((0,), (0,)), ((), ())), preferred_element_type=F32)


def _mixer_kernel(x_ref, g_ref, w32_ref, cos_ref, sin_ref, mask_ref, qdec_ref, kdec_ref, bdec_ref, blk_ref,
                  meta_ref, cos_m_ref, sin_m_ref, kdec_m_ref, u_ref, y_ref, um_ref, w_ref, r0_ref, r_ref):
    first_tile = pl.program_id(1) == 0

    @pl.when(jnp.logical_and(pl.program_id(0) == 0, first_tile))
    def _():
        for c in range(0, IN_WIDTH, SSM_WIDTH):
            w_ref[:, c:c + SSM_WIDTH] = w32_ref[:, c:c + SSM_WIDTH].astype(BF16)
        _meta_state(meta_ref, g_ref, w_ref, cos_m_ref, sin_m_ref, kdec_m_ref, blk_ref, um_ref, r0_ref)

    @pl.when(first_tile)
    def _():
        r_ref[...] = r0_ref[...]

    off = SSM_WIDTH
    for b in range(MIXER_TILE // RET_BLOCK):
        bs = slice(b * RET_BLOCK, (b + 1) * RET_BLOCK)
        a = _rms_norm(x_ref[bs, :], g_ref[...]).astype(BF16)
        u_ref[bs, :] = _dot(a, w_ref[:, :SSM_WIDTH])
        q_all = _dot(a, w_ref[:, off:off + RET_WIDTH])
        k_all = _dot(a, w_ref[:, off + RET_WIDTH:off + 2 * RET_WIDTH])
        v_all = _dot(a, w_ref[:, off + 2 * RET_WIDTH:off + 3 * RET_WIDTH])
        gate = _dot(a, w_ref[:, off + 3 * RET_WIDTH:off + 4 * RET_WIDTH])
        cos = cos_ref[bs, :]
        sin = sin_ref[bs, :]
        def scores(h):
            hs = slice(h * HEAD_DIM, (h + 1) * HEAD_DIM)
            q = _rope(q_all[:, hs], cos, sin)
            k = _rope(k_all[:, hs], cos, sin)
            return q, k, _dot_t(q.astype(BF16), k.astype(BF16))

        ahead = scores(0)
        for p in range(HEAD_PAIRS):
            ps = slice(p * PAIR_DIM, (p + 1) * PAIR_DIM)
            pair = [ahead, scores(2 * p + 1)]
            if p + 1 < HEAD_PAIRS:
                ahead = scores(2 * p + 2)
            q = jnp.concatenate([pair[0][0], pair[1][0]], axis=1)
            k = jnp.concatenate([pair[0][1], pair[1][1]], axis=1)
            v = v_all[:, ps].astype(BF16)
            state = r_ref[p]
            cross = _dot((q * qdec_ref[p]).astype(BF16), state.astype(BF16))
            kv = _dot_rows((k * kdec_ref[p]).astype(BF16), v)
            r_ref[p] = state * bdec_ref[p] + kv * blk_ref[...]
            for half, (_, _, s) in enumerate(pair):
                h = 2 * p + half
                hs = slice(h * HEAD_DIM, (h + 1) * HEAD_DIM)
                ls = slice(half * HEAD_DIM, (half + 1) * HEAD_DIM)
                o = _dot((s * mask_ref[h]).astype(BF16), v[:, ls]) + cross[:, ls]
                mu = jnp.mean(o, axis=-1, keepdims=True)
                d = o - mu
                var = jnp.mean(d * d, axis=-1, keepdims=True)
                gt = gate[:, hs]
                y_ref[bs, hs] = (gt * _sigmoid_exp(gt) * d * lax.rsqrt(var + EPS)).astype(BF16)


def _mixer_call(x, g, w_in, cos, sin, mask, qdec, kdec, bdec, blk, meta, cos_m, sin_m, kdec_m):
    bsz, seq, _ = x.shape
    const = lambda a, **kw: pl.BlockSpec(a.shape, lambda b, i: (0,) * a.ndim, **kw)
    return pl.pallas_call(
        _mixer_kernel,
        grid=(bsz, seq // MIXER_TILE),
        in_specs=[
            pl.BlockSpec((None, MIXER_TILE, D_MODEL), lambda b, i: (b, i, 0)),
            const(g),
            const(w_in, pipeline_mode=pl.Buffered(1)),
            pl.BlockSpec((MIXER_TILE, HEAD_DIM), lambda b, i: (i, 0)),
            pl.BlockSpec((MIXER_TILE, HEAD_DIM), lambda b, i: (i, 0)),
            const(mask), const(qdec), const(kdec), const(bdec), const(blk),
            const(meta), const(cos_m), const(sin_m), const(kdec_m),
        ],
        out_specs=(
            pl.BlockSpec((None, MIXER_TILE, SSM_WIDTH), lambda b, i: (b, i, 0)),
            pl.BlockSpec((None, MIXER_TILE, RET_WIDTH), lambda b, i: (b, i, 0)),
            pl.BlockSpec((N_META, SSM_WIDTH), lambda b, i: (0, 0)),
        ),
        out_shape=(jax.ShapeDtypeStruct((bsz, seq, SSM_WIDTH), F32),
                   jax.ShapeDtypeStruct((bsz, seq, RET_WIDTH), BF16),
                   jax.ShapeDtypeStruct((N_META, SSM_WIDTH), F32)),
        scratch_shapes=[pltpu.VMEM((D_MODEL, IN_WIDTH), BF16),
                        pltpu.VMEM((HEAD_PAIRS, PAIR_DIM, PAIR_DIM), F32),
                        pltpu.VMEM((HEAD_PAIRS, PAIR_DIM, PAIR_DIM), F32)],
        compiler_params=pltpu.CompilerParams(
            dimension_semantics=("arbitrary", "arbitrary"), vmem_limit_bytes=VMEM_LIMIT),
        name="mixer_call",
    )(x, g, w_in, cos, sin, mask, qdec, kdec, bdec, blk, meta, cos_m, sin_m, kdec_m)


def _dot_t(a, b):
    return lax.dot_general(a, b, (((1,), (1,)), ((), ())), preferred_element_type=F32)


def _chunk_transpose(arrs):
    n = len(arrs)
    chunk = lax.broadcasted_iota(I32, (1, LANES), 1) // SSM_GROUP
    arrs = list(arrs)
    s = n // 2
    while s:
        keep = (chunk & s) == 0
        nxt = list(arrs)
        for i in range(n):
            if i & s == 0:
                lo, hi = arrs[i], arrs[i + s]
                nxt[i] = jnp.where(keep, lo, pltpu.roll(hi, s * SSM_GROUP, axis=1))
                nxt[i + s] = jnp.where(keep, pltpu.roll(lo, LANES - s * SSM_GROUP, axis=1), hi)
        arrs = nxt
        s //= 2
    return arrs


def _s5_kernel(u_lo_ref, u_hi_ref, um_ref, krow_ref, bmat_ref, cre_ref, cim_ref, ar_ref, ai_ref, wglu_ref,
               y_lo_ref, y_hi_ref, t0_ref, ug_ref, yg_ref):
    n_blocks = u_lo_ref.shape[0] // S5_BLOCK

    @pl.when(pl.program_id(0) == 0)
    def _():
        lane = lax.broadcasted_iota(I32, (SSM_GROUP, S5_LANES), 1)
        for g in range(SSM_GROUPS):
            k0 = krow_ref[g]
            for a in range(S5_BLOCK):
                blk = k0 if a == 0 else jnp.where(lane >= a * SSM_GROUP,
                                                  pltpu.roll(k0, a * SSM_GROUP, axis=1), 0.0)
                t0_ref[g, a * SSM_GROUP:(a + 1) * SSM_GROUP, :] = blk.astype(BF16)

    per_tile = LANES // SSM_GROUP
    for half, uh_ref in enumerate((u_lo_ref, u_hi_ref)):
        for t in range(S5_BLOCK // per_tile):
            words = [pltpu.bitcast(uh_ref[pl.ds(t * per_tile + k, n_blocks, stride=S5_BLOCK), :].astype(BF16),
                                   jnp.uint32) for k in range(per_tile)]
            for m, w in enumerate(_chunk_transpose(words)):
                ug_ref[half * per_tile + m, :, t * LANES:(t + 1) * LANES] = pltpu.bitcast(w, BF16)

    re, im, re0, im0 = [], [], [], []
    for p in range(SSM_GROUPS // 2):
        g0, g1 = 2 * p, 2 * p + 1
        v = _dot(ug_ref[g0], bmat_ref[g0]) + _dot(ug_ref[g1], bmat_ref[g1])
        v0 = (_dot(um_ref[g0].astype(BF16), bmat_ref[g0]) + _dot(um_ref[g1].astype(BF16), bmat_ref[g1]))[0:1]
        re.append(v[:, :LANES])
        im.append(v[:, LANES:])
        re0.append(v0[:, :LANES])
        im0.append(v0[:, LANES:])
    re, im, re0, im0 = (jnp.concatenate(parts, axis=1) for parts in (re, im, re0, im0))

    row = lax.broadcasted_iota(I32, re.shape, 0)
    ar, ai = ar_ref[0:1], ai_ref[0:1]
    re = re + jnp.where(row == 0, ar * re0 - ai * im0, 0.0)
    im = im + jnp.where(row == 0, ar * im0 + ai * re0, 0.0)
    d = 1
    while d < SCAN_ROWS:
        ar, ai = ar_ref[d - 1:d], ai_ref[d - 1:d]
        inside = row % SCAN_ROWS >= d
        sre = jnp.where(inside, pltpu.roll(re, d, axis=0), 0.0)
        sim = jnp.where(inside, pltpu.roll(im, d, axis=0), 0.0)
        re, im = re + ar * sre - ai * sim, im + ar * sim + ai * sre
        d *= 2
    ar, ai = ar_ref[...], ai_ref[...]
    re_tiles, im_tiles = [re[:SCAN_ROWS]], [im[:SCAN_ROWS]]
    for r in range(1, n_blocks // SCAN_ROWS):
        cre_ = re_tiles[-1][SCAN_ROWS - 1:SCAN_ROWS]
        cim_ = im_tiles[-1][SCAN_ROWS - 1:SCAN_ROWS]
        rs = slice(r * SCAN_ROWS, (r + 1) * SCAN_ROWS)
        re_tiles.append(re[rs] + ar * cre_ - ai * cim_)
        im_tiles.append(im[rs] + ar * cim_ + ai * cre_)
    re = jnp.concatenate(re_tiles, axis=0)
    im = jnp.concatenate(im_tiles, axis=0)
    pre = jnp.where(row == 0, re0, pltpu.roll(re, 1, axis=0)).astype(BF16)
    pim = jnp.where(row == 0, im0, pltpu.roll(im, 1, axis=0)).astype(BF16)

    for g in range(SSM_GROUPS):
        ps = slice((g // 2) * LANES, (g // 2 + 1) * LANES)
        yg_ref[g] = (_dot(ug_ref[g], t0_ref[g]) + _dot_t(pre[:, ps], cre_ref[g])
                     + _dot_t(pim[:, ps], cim_ref[g])).astype(BF16)

    wglu = wglu_ref[...]
    for t in range(S5_BLOCK // per_tile):
        ts = slice(t * LANES, (t + 1) * LANES)
        halves = [_chunk_transpose([pltpu.bitcast(yg_ref[half * per_tile + m, :, ts], jnp.uint32)
                                    for m in range(per_tile)])
                  for half in range(SSM_GROUPS // per_tile)]
        for k in range(per_tile):
            y = jnp.concatenate([pltpu.bitcast(h[k], BF16) for h in halves], axis=1).astype(F32)
            y = jax.nn.gelu(y, approximate=True)
            y = y * _sigmoid(_dot(y.astype(BF16), wglu))
            i = t * per_tile + k
            y_lo_ref[pl.ds(i, n_blocks, stride=S5_BLOCK), :] = y[:, :LANES]
            y_hi_ref[pl.ds(i, n_blocks, stride=S5_BLOCK), :] = y[:, LANES:]


def _s5_call(u, um, krow, bmat, cre, cim, ar, ai, wglu):
    bsz, seq, _ = u.shape
    n_blocks = seq // S5_BLOCK
    const = lambda a: pl.BlockSpec(a.shape, lambda b: (0,) * a.ndim)
    return pl.pallas_call(
        _s5_kernel,
        grid=(bsz,),
        in_specs=[pl.BlockSpec((None, seq, LANES), lambda b: (b, 0, 0)),
                  pl.BlockSpec((None, seq, LANES), lambda b: (b, 0, 1)),
                  const(um), const(krow), const(bmat), const(cre), const(cim), const(ar), const(ai),
                  const(wglu)],
        out_specs=(pl.BlockSpec((None, seq, LANES), lambda b: (b, 0, 0)),
                   pl.BlockSpec((None, seq, LANES), lambda b: (b, 0, 0))),
        out_shape=(jax.ShapeDtypeStruct((bsz, seq, LANES), F32),
                   jax.ShapeDtypeStruct((bsz, seq, LANES), F32)),
        scratch_shapes=[pltpu.VMEM((SSM_GROUPS, S5_LANES, S5_LANES), BF16),
                        pltpu.VMEM((SSM_GROUPS, n_blocks, S5_LANES), BF16),
                        pltpu.VMEM((SSM_GROUPS, n_blocks, S5_LANES), BF16)],
        compiler_params=pltpu.CompilerParams(
            dimension_semantics=("arbitrary",), vmem_limit_bytes=VMEM_LIMIT),
        name="s5_call",
    )(u, u, um, krow, bmat, cre, cim, ar, ai, wglu)


def _s5_operators(lam_re, lam_im, log_dt, b_re, b_im, c_re, c_im, d_skip):
    n_groups = lam_re.shape[0]
    lam = lax.complex(lam_re, lam_im)
    lam_dt = lam * jnp.exp(log_dt)[:, None]
    lam_bar = jnp.exp(lam_dt)
    b_bar = ((lam_bar - 1.0) / lam)[..., None] * lax.complex(b_re, b_im)
    c = lax.complex(c_re, c_im)
    tau = jnp.arange(S5_BLOCK + 1, dtype=F32)
    pows = jnp.exp(lam_dt[None] * tau[:, None, None])
    kern = jnp.real(jnp.einsum('ghp,tgp,gpk->gkth', c, pows[:S5_BLOCK], b_bar))
    skip = (jnp.eye(SSM_GROUP, dtype=F32)[None, :, None, :] * d_skip[:, None, None, :]
            * (tau[:S5_BLOCK] == 0).astype(F32)[None, None, :, None])
    krow = (kern + skip).reshape(n_groups, SSM_GROUP, S5_LANES)
    even = (jnp.arange(n_groups) % 2 == 0)[:, None, None]
    pair_pad = lambda m: jnp.concatenate([jnp.where(even, m, 0.0), jnp.where(even, 0.0, m)], axis=-1)
    bm = (pows[S5_BLOCK - 1 - jnp.arange(S5_BLOCK)].transpose(1, 0, 2)[:, :, None, :]
          * b_bar.transpose(0, 2, 1)[:, None, :, :]).reshape(n_groups, S5_LANES, SSM_STATE)
    bmat = jnp.concatenate([pair_pad(jnp.real(bm)), pair_pad(jnp.imag(bm))], axis=-1)
    cm = (pows[1:].transpose(1, 0, 2)[:, :, None, :] * c[:, None, :, :]).reshape(n_groups, S5_LANES, SSM_STATE)
    cre, cim = pair_pad(jnp.real(cm)), pair_pad(-jnp.imag(cm))
    step = S5_BLOCK * (1.0 + jnp.arange(SCAN_ROWS, dtype=F32))
    adec = jnp.exp(lam_dt[None, :, :] * step[:, None, None]).reshape(SCAN_ROWS, n_groups * SSM_STATE)
    return krow, bmat.astype(BF16), cre.astype(BF16), cim.astype(BF16), jnp.real(adec), jnp.imag(adec)


def _first_hit(values, target):
    hits, taken = [], None
    for v in values:
        hit = (v >= target) if taken is None else jnp.logical_and(v >= target, jnp.logical_not(taken))
        taken = hit if taken is None else jnp.logical_or(taken, hit)
        hits.append(hit)
    return hits


def _stack_rows(rows, n_rows):
    idx = lax.broadcasted_iota(I32, (n_rows, rows[0].shape[1]), 0)
    out = jnp.zeros((n_rows, rows[0].shape[1]), F32)
    for k, r in enumerate(rows):
        out = jnp.where(idx == k, r, out)
    return out


def _project(rs, x_ref, ys_lo_ref, ys_hi_ref, yr_ref, wout_ref, h_ref):
    ys = jnp.concatenate([ys_lo_ref[rs, :], ys_hi_ref[rs, :]], axis=1).astype(BF16)
    h = x_ref[rs, :] + _dot(ys, wout_ref[:SSM_WIDTH, :]) + _dot(yr_ref[rs, :], wout_ref[SSM_WIDTH:, :])
    h_ref[rs, :] = h.astype(BF16)
    return h


def _route(lt, tri_ref):
    gl = [lt[g:g + 1, :] for g in range(N_GROUPS)]
    gmax = functools.reduce(jnp.maximum, gl)
    g_w = 1.0 / functools.reduce(lambda a, b: a + b, [jnp.exp(l - gmax) for l in gl])
    sel = _first_hit(gl, gmax)
    ev = []
    for e in range(EXPERTS_PER_GROUP):
        acc = jnp.zeros_like(gmax)
        for g in range(N_GROUPS):
            k = N_GROUPS + g * EXPERTS_PER_GROUP + e
            acc = jnp.where(sel[g], lt[k:k + 1, :], acc)
        ev.append(acc)
    m1 = functools.reduce(jnp.maximum, ev)
    first = _first_hit(ev, m1)
    rest = [jnp.where(f, -jnp.inf, v) for f, v in zip(first, ev)]
    m2 = functools.reduce(jnp.maximum, rest)
    second = _first_hit(rest, m2)
    e2 = jnp.exp(m2 - m1)
    w1 = g_w / (1.0 + e2)
    w2 = e2 * w1
    combine = [jnp.where(f, w1, 0.0) + jnp.where(s, w2, 0.0) for f, s in zip(first, second)]

    sel_f = [jnp.where(s, 1.0, 0.0) for s in sel]
    incl = _dot(_stack_rows(sel_f, 8).astype(BF16), tri_ref[...])
    dest = jnp.zeros_like(gmax)
    seg_start = jnp.zeros((1, 1), F32)
    counts = []
    for g in range(N_GROUPS):
        run = incl[g:g + 1, :]
        cnt = run[:, TILE - 1:TILE]
        counts.append(cnt)
        dest = dest + sel_f[g] * (seg_start + run - 1.0)
        seg_start = seg_start + PIECE * jnp.floor((cnt + (PIECE - 1.0)) * (1.0 / PIECE))
    return combine, dest, counts


def _proj_kernel(x_ref, ys_lo_ref, ys_hi_ref, yr_ref, wout32_ref, g_ref, wr_ref, br_ref,
                 tri_ref, h_ref, stage_ref, dest_ref, cnt_ref, wout_ref):
    @pl.when(pl.program_id(0) == 0)
    def _():
        wout_ref[...] = wout32_ref[...].astype(BF16)

    tiles = range(PROJ_TILES)
    proj_refs = (x_ref, ys_lo_ref, ys_hi_ref, yr_ref, wout_ref, h_ref)
    chunks = [[slice(tile * TILE + b * ROW_CHUNK, tile * TILE + (b + 1) * ROW_CHUNK)
               for b in range(TILE // ROW_CHUNK)] for tile in tiles]
    h_parts = [[_project(rs, *proj_refs) for rs in chunks[tile]] for tile in tiles]

    logits, staged = [], []
    for tile in tiles:
        parts, rows, inv = [], [], []
        for h in h_parts[tile]:
            t = _rms_norm(h, g_ref[...])
            parts.append(t.astype(BF16))
            peak = jnp.maximum(jnp.max(jnp.abs(t), axis=1, keepdims=True), 1e-30)
            rows.append((t * (FP8_HEADROOM / peak)).astype(BF16))
            inv.append(peak * (1.0 / FP8_HEADROOM))
        staged.append((jnp.concatenate(rows, axis=0), jnp.concatenate(inv, axis=0)))
        lt_parts = []
        for t_hi in parts:
            both = _dot_t(wr_ref[...], t_hi)
            lt_parts.append(both[:ROUTE_ROWS] + both[ROUTE_ROWS:])
        logits.append(jnp.concatenate(lt_parts, axis=1) + br_ref[...])

    routed = [_route(logits[tile], tri_ref) for tile in tiles]

    for tile in tiles:
        combine, dest, counts = routed[tile]
        dest_ref[tile] = _stack_rows([dest], 8)
        cnt_ref[tile] = _stack_rows([c + jnp.zeros((1, LANES), F32) for c in counts], 8)
        perm = jnp.where(lax.broadcasted_iota(I32, (SORTED_ROWS, TILE), 0) == dest.astype(I32),
                         1.0, 0.0).astype(BF16)
        c_hi = [c.astype(BF16).astype(F32) for c in combine]
        c_lo = [c - hi for c, hi in zip(combine, c_hi)]
        cw = _stack_rows(c_hi + c_lo, LANES).T
        rows, inv = staged[tile]
        inv_hi = inv.astype(BF16).astype(F32)
        lane = lax.broadcasted_iota(I32, (1, LANES), 1)
        cw = jnp.where(lane == INV_LANE, inv_hi, jnp.where(lane == INV_LANE + 1, inv - inv_hi, cw))
        t_ext = jnp.concatenate([rows, cw.astype(BF16)], axis=1)
        stage_ref[tile * SORTED_ROWS:(tile + 1) * SORTED_ROWS, :] = _dot(perm, t_ext).astype(BF16)


def _proj_call(x, ys_lo, ys_hi, yr, wout, g, wr, br, tri):
    n_tok = x.shape[0]
    n_tiles = n_tok // TILE
    const = lambda *shape: pl.BlockSpec(shape, lambda i: (0,) * len(shape))
    rows = lambda width: pl.BlockSpec((PROJ_TILES * TILE, width), lambda i: (i, 0))
    return pl.pallas_call(
        _proj_kernel,
        grid=(n_tiles // PROJ_TILES,),
        in_specs=[
            rows(D_MODEL), rows(LANES), rows(LANES), rows(RET_WIDTH),
            pl.BlockSpec((D_MODEL, D_MODEL), lambda i: (0, 0), pipeline_mode=pl.Buffered(1)), const(1, D_MODEL),
            const(2 * ROUTE_ROWS, D_MODEL), const(ROUTE_ROWS, 1), const(TILE, TILE),
        ],
        out_specs=(rows(D_MODEL),
                   pl.BlockSpec((PROJ_TILES * SORTED_ROWS, EXT_WIDTH), lambda i: (i, 0)),
                   pl.BlockSpec((PROJ_TILES, 8, TILE), lambda i: (i, 0, 0)),
                   pl.BlockSpec((PROJ_TILES, 8, LANES), lambda i: (i, 0, 0))),
        scratch_shapes=[pltpu.VMEM((D_MODEL, D_MODEL), BF16)],
        out_shape=(jax.ShapeDtypeStruct((n_tok, D_MODEL), BF16),
                   jax.ShapeDtypeStruct((n_tiles * SORTED_ROWS, EXT_WIDTH), BF16),
                   jax.ShapeDtypeStruct((n_tiles, 8, TILE), F32),
                   jax.ShapeDtypeStruct((n_tiles, 8, LANES), F32)),
        compiler_params=pltpu.CompilerParams(
            dimension_semantics=("arbitrary",), vmem_limit_bytes=VMEM_LIMIT),
        name="proj_call",
    )(x, ys_lo, ys_hi, yr, wout, g, wr, br, tri)


def _sort_tables(cnt, n_steps):
    n_tiles = cnt.shape[0]
    npc = (cnt + PIECE - 1) // PIECE
    seg = jnp.cumsum(npc, axis=1) - npc
    before = jnp.cumsum(npc, axis=0) - npc
    n_tile_g = (jnp.sum(npc, axis=0) + MOE_PIECES - 1) // MOE_PIECES
    t_off = jnp.cumsum(n_tile_g) - n_tile_g
    j = jnp.arange(SORTED_PIECES, dtype=I32)[None, :, None]
    in_g = jnp.logical_and(j >= seg[:, None, :], j < (seg + npc)[:, None, :])
    pos = jnp.sum(jnp.where(in_g, MOE_PIECES * t_off[None, None, :] + before[:, None, :] + j - seg[:, None, :], 0),
                  axis=-1)
    valid = jnp.any(in_g, axis=-1)
    stage_piece = jnp.arange(n_tiles * SORTED_PIECES, dtype=I32).reshape(n_tiles, SORTED_PIECES)
    n_slots = n_steps * MOE_PIECES
    src = jnp.zeros((n_slots,), I32).at[jnp.where(valid, pos, n_slots).reshape(-1)].set(
        stage_piece.reshape(-1), mode='drop')
    steps = jnp.arange(n_steps, dtype=I32)
    g_step = jnp.minimum(jnp.sum(steps[:, None] >= (t_off + n_tile_g)[None, :], axis=1), N_GROUPS - 1)
    n_live = jnp.sum(n_tile_g).reshape(1)
    back = jnp.where(valid, pos, 0).reshape(-1)
    return src, g_step.astype(I32), n_live.astype(I32), back.astype(I32)


def _piece_copy(src_ref, piece, buf_ref, slot, j, sem_ref):
    start = piece * PIECE if isinstance(piece, int) else pl.multiple_of(piece * PIECE, PIECE)
    return pltpu.make_async_copy(src_ref.at[pl.ds(start, PIECE)],
                                 buf_ref.at[slot, pl.ds(j * PIECE, PIECE)], sem_ref.at[slot])


def _fetch_pieces(table_ref, src_ref, buf_ref, sem_ref, row, slot, n_pieces):
    for j in range(n_pieces):
        _piece_copy(src_ref, table_ref[row * n_pieces + j], buf_ref, slot, j, sem_ref).start(priority=j % 2)


def _wait_pieces(src_ref, buf_ref, sem_ref, slot, n_pieces):
    for j in range(n_pieces):
        _piece_copy(src_ref, 0, buf_ref, slot, j, sem_ref).wait()


def _gather_pieces(table_ref, src_ref, buf_ref, sem_ref, n_pieces):
    step = pl.program_id(0)
    last = pl.num_programs(0) - 1
    args = (src_ref, buf_ref, sem_ref)

    @pl.when(step == 0)
    def _():
        _fetch_pieces(table_ref, *args, step, 0, n_pieces)

    _wait_pieces(*args, step % 2, n_pieces)

    def fetch_next():
        _fetch_pieces(table_ref, *args, jnp.minimum(step + 1, last), (step + 1) % 2, n_pieces)

    def drain():
        @pl.when(step == last)
        def _():
            _wait_pieces(*args, (step + 1) % 2, n_pieces)

    return fetch_next, drain


def _to_fp8(x, headroom=FP8_HEADROOM):
    peak = jnp.maximum(jnp.max(jnp.abs(x), axis=(0, 1), keepdims=True), 1e-30)
    return (x * (headroom / peak)).astype(FP8), peak * (1.0 / headroom)


def _moe_kernel(src_ref, gstep_ref, nlive_ref, stage_ref, wg_ref, wu_ref, wd_ref, y_ref,
                buf_ref, sem_ref, wgb_ref, wub_ref, wdb_ref, inv_ref):
    step = pl.program_id(0)
    fetch_next, drain = _gather_pieces(src_ref, stage_ref, buf_ref, sem_ref, MOE_PIECES)

    @pl.when(jnp.logical_or(step == 0, gstep_ref[step] != gstep_ref[jnp.maximum(step - 1, 0)]))
    def _():
        ones = jnp.ones((1, LANES), F32)
        for e in range(EXPERTS_PER_GROUP):
            wgb_ref[e], inv = _to_fp8(wg_ref[e])
            inv_ref[e:e + 1, :] = inv * ones
            wub_ref[e], inv = _to_fp8(wu_ref[e])
            inv_ref[EXPERTS_PER_GROUP + e:EXPERTS_PER_GROUP + e + 1, :] = inv * ones
        peak = functools.reduce(jnp.maximum, [jnp.max(jnp.abs(wd_ref[e]), axis=(0, 1), keepdims=True)
                                              for e in range(EXPERTS_PER_GROUP)])
        peak = jnp.maximum(peak, 1e-30)
        for e in range(EXPERTS_PER_GROUP):
            wdb_ref[e * EXPERT_FF:(e + 1) * EXPERT_FF, :] = (wd_ref[e] * (FP8_HEADROOM / peak)).astype(FP8)
        inv_ref[2 * EXPERTS_PER_GROUP:2 * EXPERTS_PER_GROUP + 1, :] = peak * (1.0 / FP8_HEADROOM) * ones

    @pl.when(step < nlive_ref[0])
    def _():
        slot = step % 2
        fetch_next()

        def up(rs):
            t = buf_ref[slot, rs, :D_MODEL].astype(FP8)
            t_inv = (buf_ref[slot, rs, D_MODEL + INV_LANE:D_MODEL + INV_LANE + 1].astype(F32)
                     + buf_ref[slot, rs, D_MODEL + INV_LANE + 1:D_MODEL + INV_LANE + 2].astype(F32))
            out = []
            for e in range(EXPERTS_PER_GROUP):
                g_inv = inv_ref[e:e + 1, 0:1] * t_inv
                u_inv = inv_ref[EXPERTS_PER_GROUP + e:EXPERTS_PER_GROUP + e + 1, 0:1] * t_inv
                out.append((_dot(t, wgb_ref[e]) * g_inv, _dot(t, wub_ref[e]), u_inv))
            return out

        def down(rs, hidden):
            cw = buf_ref[slot, rs, D_MODEL:].astype(F32)
            acts = []
            for e, (hg, hu, u_inv) in enumerate(hidden):
                c = (cw[:, e:e + 1] + cw[:, EXPERTS_PER_GROUP + e:EXPERTS_PER_GROUP + e + 1]) * u_inv
                acts.append(hg * _sigmoid(hg) * hu * c)
            act, a_inv = _to_fp8(jnp.concatenate(acts, axis=1))
            d_inv = inv_ref[2 * EXPERTS_PER_GROUP:2 * EXPERTS_PER_GROUP + 1, 0:1] * a_inv
            y_ref[rs, :] = (_dot(act, wdb_ref[...]) * d_inv).astype(BF16)

        chunks = [slice(b * ROW_CHUNK, (b + 1) * ROW_CHUNK) for b in range(MOE_TILE // ROW_CHUNK)]
        hidden = [up(rs) for rs in chunks]
        for rs, hid in zip(chunks, hidden):
            down(rs, hid)

    @pl.when(step >= nlive_ref[0])
    def _():
        fetch_next()
        y_ref[...] = jnp.zeros_like(y_ref)

    drain()


def _moe_call(src, g_step, n_live, stage, wg, wu, wd, n_steps):
    grp = lambda shape: pl.BlockSpec((EXPERTS_PER_GROUP,) + shape, lambda s, src, gs, nl: (gs[s], 0, 0))
    return pl.pallas_call(
        _moe_kernel,
        grid_spec=pltpu.PrefetchScalarGridSpec(
            num_scalar_prefetch=3,
            grid=(n_steps,),
            in_specs=[pl.BlockSpec(memory_space=pl.ANY),
                      grp((D_MODEL, EXPERT_FF)), grp((D_MODEL, EXPERT_FF)), grp((EXPERT_FF, D_MODEL))],
            out_specs=pl.BlockSpec((MOE_TILE, D_MODEL), lambda s, src, gs, nl: (s, 0)),
            scratch_shapes=[pltpu.VMEM((2, MOE_TILE, EXT_WIDTH), BF16), pltpu.SemaphoreType.DMA((2,)),
                            pltpu.VMEM((EXPERTS_PER_GROUP, D_MODEL, EXPERT_FF), FP8),
                            pltpu.VMEM((EXPERTS_PER_GROUP, D_MODEL, EXPERT_FF), FP8),
                            pltpu.VMEM((EXPERTS_PER_GROUP * EXPERT_FF, D_MODEL), FP8),
                            pltpu.VMEM((16, LANES), F32)],
        ),
        out_shape=jax.ShapeDtypeStruct((n_steps * MOE_TILE, D_MODEL), BF16),
        compiler_params=pltpu.CompilerParams(
            dimension_semantics=("arbitrary",), vmem_limit_bytes=VMEM_LIMIT),
        name="moe_call",
    )(src, g_step, n_live, stage, wg, wu, wd)


def _final_kernel(back_ref, ysort_ref, h_ref, dest_ref, g_ref, o_ref, buf_ref, sem_ref):
    step = pl.program_id(0)
    fetch_next, drain = _gather_pieces(back_ref, ysort_ref, buf_ref, sem_ref, FINAL_TILES * SORTED_PIECES)
    fetch_next()
    for t in range(FINAL_TILES):
        dest = _stack_rows([dest_ref[t, 0:1, :]], LANES).T[:, 0:1].astype(I32)
        unperm = jnp.where(lax.broadcasted_iota(I32, (TILE, SORTED_ROWS), 1) == dest, 1.0, 0.0).astype(BF16)
        rows = buf_ref[step % 2, t * SORTED_ROWS:(t + 1) * SORTED_ROWS, :]
        for b in range(TILE // ROW_CHUNK):
            rs = slice(b * ROW_CHUNK, (b + 1) * ROW_CHUNK)
            os = slice(t * TILE + b * ROW_CHUNK, t * TILE + (b + 1) * ROW_CHUNK)
            o_ref[os, :] = _rms_norm(h_ref[os, :].astype(F32) + _dot(unperm[rs], rows), g_ref[...])
    drain()


def _final_call(back, ysort, h, dest, g):
    n_tok = h.shape[0]
    return pl.pallas_call(
        _final_kernel,
        grid_spec=pltpu.PrefetchScalarGridSpec(
            num_scalar_prefetch=1,
            grid=(n_tok // (FINAL_TILES * TILE),),
            in_specs=[pl.BlockSpec(memory_space=pl.ANY),
                      pl.BlockSpec((FINAL_TILES * TILE, D_MODEL), lambda i, back: (i, 0)),
                      pl.BlockSpec((FINAL_TILES, 8, TILE), lambda i, back: (i, 0, 0)),
                      pl.BlockSpec((1, D_MODEL), lambda i, back: (0, 0))],
            out_specs=pl.BlockSpec((FINAL_TILES * TILE, D_MODEL), lambda i, back: (i, 0)),
            scratch_shapes=[pltpu.VMEM((2, FINAL_TILES * SORTED_ROWS, D_MODEL), BF16),
                            pltpu.SemaphoreType.DMA((2,))],
        ),
        out_shape=jax.ShapeDtypeStruct((n_tok, D_MODEL), F32),
        compiler_params=pltpu.CompilerParams(
            dimension_semantics=("arbitrary",), vmem_limit_bytes=VMEM_LIMIT),
        name="final_call",
    )(back, ysort, h, dest, g)


def _rope_tables(length):
    pos = np.arange(length, dtype=np.float32)
    inv_freq = np.float32(ROPE_BASE) ** (-np.arange(0, HEAD_DIM, 2, dtype=np.float32) / np.float32(HEAD_DIM))
    ang = pos[:, None] * inv_freq[None, :]
    cos, sin = np.cos(ang), np.sin(ang)
    return np.concatenate([cos, cos], axis=-1), np.concatenate([-sin, sin], axis=-1)


def _retention_tables():
    f32 = np.float32
    gamma = f32(1.0) - f32(2.0) ** (f32(-5.0) - np.arange(HEADS, dtype=f32))
    log_g = np.log(gamma)[:, None, None]
    scale = f32(HEAD_DIM ** -0.5)
    idx = np.arange(RET_BLOCK)
    dist = np.abs(idx[:, None] - idx[None, :]).astype(f32)
    visible = (idx[None, :] // CHUNK) <= (idx[:, None] // CHUNK)
    mask = np.where(visible[None], np.exp(log_g * dist[None]), f32(0.0)) * scale
    ones = np.ones((1, 1, HEAD_DIM), f32)
    idx_f = idx.astype(f32)[None, :, None]
    pair = lambda a: a.reshape(HEAD_PAIRS, 2, a.shape[1], HEAD_DIM).transpose(0, 2, 1, 3).reshape(
        HEAD_PAIRS, a.shape[1], PAIR_DIM)
    qdec = pair(np.exp(log_g * (idx_f + f32(1.0))) * ones)
    kdec = pair(np.exp(log_g * (f32(RET_BLOCK - 1.0) - idx_f)) * scale * ones)
    meta_idx = np.arange(N_META, dtype=f32)[None, :, None]
    kdec_meta = pair(np.exp(log_g * (f32(N_META - 1.0) - meta_idx)) * scale * ones)
    blk = np.kron(np.eye(2, dtype=f32), np.ones((HEAD_DIM, HEAD_DIM), f32))
    bdec = np.exp(log_g * f32(RET_BLOCK)).reshape(HEAD_PAIRS, 2)
    bdec = np.stack([np.kron(np.diag(b), np.ones((HEAD_DIM, HEAD_DIM), f32)) for b in bdec])
    return tuple(a.astype(f32) for a in (mask, qdec, kdec, bdec, blk, kdec_meta))


def kernel(x, meta_tokens, norm_mix_g, w_in, ssm_lambda_re, ssm_lambda_im, ssm_log_dt, ssm_b_re, ssm_b_im, ssm_c_re, ssm_c_im, ssm_d, w_glu, w_out, norm_ffn_g, w_router_group, b_router_group, w_router_expert, b_router_expert, w_gate, w_up, w_down, norm_final_g):
    bsz, seq, _ = x.shape
    assert seq % TILE == 0 and seq % MIXER_TILE == 0 and MIXER_TILE % RET_BLOCK == 0 and RET_BLOCK % CHUNK == 0
    n_blocks = seq // S5_BLOCK
    assert n_blocks % SCAN_ROWS == 0
    n_tok = bsz * seq
    assert n_tok % (FINAL_TILES * TILE) == 0 and n_tok % (PROJ_TILES * TILE) == 0
    n_tiles = n_tok // TILE
    n_steps = -(-n_tiles * (TILE_PIECES + N_GROUPS - 1) // MOE_PIECES) + N_GROUPS

    cos, sin = _rope_tables(N_META + seq)
    mask, qdec, kdec, bdec, blk, kdec_meta = _retention_tables()
    g_mix = norm_mix_g[0][None, :]
    u, y_ret, u_meta = _mixer_call(x, g_mix, w_in[0], cos[N_META:], sin[N_META:], mask, qdec, kdec, bdec, blk,
                                   meta_tokens, cos[:N_META], sin[:N_META], kdec_meta)

    s5_ops = _s5_operators(
        ssm_lambda_re[0], ssm_lambda_im[0], ssm_log_dt[0], ssm_b_re[0], ssm_b_im[0],
        ssm_c_re[0], ssm_c_im[0], ssm_d[0])
    um = u_meta.reshape(S5_BLOCK, SSM_GROUPS, SSM_GROUP).transpose(1, 0, 2).reshape(SSM_GROUPS, 1, S5_LANES)
    um = jnp.pad(um, ((0, 0), (0, 7), (0, 0)))
    y_lo, y_hi = _s5_call(u, um, *s5_ops, w_glu[0].astype(BF16))

    w_r = jnp.concatenate(
        [w_router_group[0].T, w_router_expert[0].transpose(0, 2, 1).reshape(N_EXPERTS, D_MODEL)], axis=0)
    w_r = jnp.pad(w_r, ((0, ROUTE_ROWS - w_r.shape[0]), (0, 0)))
    b_r = jnp.concatenate([b_router_group[0], b_router_expert[0].reshape(-1)])
    b_r = jnp.pad(b_r, (0, ROUTE_ROWS - b_r.shape[0]))[:, None]
    w_r_hi = w_r.astype(BF16)
    w_r = jnp.concatenate([w_r_hi, (w_r - w_r_hi.astype(F32)).astype(BF16)], axis=0)
    tri = jnp.asarray(np.arange(TILE)[:, None] <= np.arange(TILE)[None, :], BF16)

    h, stage, dest, cnt = _proj_call(
        x.reshape(n_tok, D_MODEL), y_lo.reshape(n_tok, LANES), y_hi.reshape(n_tok, LANES),
        y_ret.reshape(n_tok, RET_WIDTH),
        w_out[0], norm_ffn_g[0][None, :], w_r, b_r, tri)
    src, g_step, n_live, back = _sort_tables(cnt[:, :N_GROUPS, 0].astype(I32), n_steps)
    y_sorted = _moe_call(src, g_step, n_live, stage, w_gate[0], w_up[0], w_down[0], n_steps)
    out = _final_call(back, y_sorted, h, dest, norm_final_g[None, :])
    return out.reshape(bsz, seq, D_MODEL)
```

```python
import functools

import jax
import jax.numpy as jnp
import numpy as np
from jax import lax
from jax.experimental import pallas as pl
from jax.experimental.pallas import tpu as pltpu

D_MODEL = 1024
N_META = 16
CHUNK = 64
EPS = 1e-6
SSM_WIDTH = 256
SSM_GROUP = 16
SSM_GROUPS = 16
SSM_STATE = 64
RET_WIDTH = 768
HEAD_DIM = 128
HEADS = 6
HEAD_PAIRS = HEADS // 2
PAIR_DIM = 2 * HEAD_DIM
ROPE_BASE = 10000.0
IN_WIDTH = SSM_WIDTH + 4 * RET_WIDTH
N_GROUPS = 4
EXPERTS_PER_GROUP = 4
N_EXPERTS = 16
EXPERT_FF = 256

S5_BLOCK = 16
S5_LANES = S5_BLOCK * SSM_GROUP
SCAN_ROWS = 8
RET_BLOCK = 256
TILE = 512
MIXER_TILE = 1024
LANES = 128
ROUTE_ROWS = 32
ROW_CHUNK = 256
PROJ_TILES = 2
FINAL_TILES = 4
PIECE = 16
TILE_PIECES = TILE // PIECE
MOE_TILE = 1024
MOE_PIECES = MOE_TILE // PIECE
SORTED_PIECES = TILE_PIECES + N_GROUPS
SORTED_ROWS = SORTED_PIECES * PIECE
EXT_WIDTH = D_MODEL + LANES
INV_LANE = 2 * EXPERTS_PER_GROUP
V7X_VMEM_BYTES = 64 * 1024 * 1024
VMEM_LIMIT = V7X_VMEM_BYTES * 7 // 8

F32 = jnp.float32
BF16 = jnp.bfloat16
FP8 = jnp.float8_e4m3fn
FP8_HEADROOM = 256.0
I32 = jnp.int32


def _dot(a, b):
    return jnp.dot(a, b, preferred_element_type=F32)


def _sigmoid(x):
    return 0.5 * jnp.tanh(0.5 * x) + 0.5


def _sigmoid_exp(x):
    return 1.0 / (1.0 + jnp.exp(-x))


def _rms_norm(x, g):
    return x * lax.rsqrt(jnp.mean(x * x, axis=-1, keepdims=True) + EPS) * g


def _rope(t, cos, sin_signed):
    return t * cos + pltpu.roll(t, HEAD_DIM // 2, axis=1) * sin_signed


def _meta_state(meta_ref, g_ref, w_ref, cos_ref, sin_ref, kdec_ref, blk_ref, u_ref, r0_ref):
    a = _rms_norm(meta_ref[...], g_ref[...]).astype(BF16)
    u_ref[...] = _dot(a, w_ref[:, :SSM_WIDTH])
    k_off = SSM_WIDTH + RET_WIDTH
    v_off = SSM_WIDTH + 2 * RET_WIDTH
    cos = cos_ref[...]
    sin = sin_ref[...]
    for p in range(HEAD_PAIRS):
        k = _dot(a, w_ref[:, k_off + p * PAIR_DIM:k_off + (p + 1) * PAIR_DIM])
        v = _dot(a, w_ref[:, v_off + p * PAIR_DIM:v_off + (p + 1) * PAIR_DIM])
        k = jnp.concatenate([_rope(k[:, :HEAD_DIM], cos, sin), _rope(k[:, HEAD_DIM:], cos, sin)], axis=1)
        kd = (k * kdec_ref[p]).astype(BF16)
        r0_ref[p] = _dot_rows(kd, v.astype(BF16)) * blk_ref[...]


def _dot_rows(a, b):
    return lax.dot_general(a, b, (((0,), (0,)), ((), ())), preferred_element_type=F32)


def _mixer_kernel(x_ref, g_ref, w32_ref, cos_ref, sin_ref, mask_ref, qdec_ref, kdec_ref, bdec_ref, blk_ref,
                  meta_ref, cos_m_ref, sin_m_ref, kdec_m_ref, u_ref, y_ref, um_ref, w_ref, r0_ref, r_ref):
    first_tile = pl.program_id(1) == 0

    @pl.when(jnp.logical_and(pl.program_id(0) == 0, first_tile))
    def _():
        for c in range(0, IN_WIDTH, SSM_WIDTH):
            w_ref[:, c:c + SSM_WIDTH] = w32_ref[:, c:c + SSM_WIDTH].astype(BF16)
        _meta_state(meta_ref, g_ref, w_ref, cos_m_ref, sin_m_ref, kdec_m_ref, blk_ref, um_ref, r0_ref)

    @pl.when(first_tile)
    def _():
        r_ref[...] = r0_ref[...]

    off = SSM_WIDTH
    for b in range(MIXER_TILE // RET_BLOCK):
        bs = slice(b * RET_BLOCK, (b + 1) * RET_BLOCK)
        a = _rms_norm(x_ref[bs, :], g_ref[...]).astype(BF16)
        u_ref[bs, :] = _dot(a, w_ref[:, :SSM_WIDTH])
        q_all = _dot(a, w_ref[:, off:off + RET_WIDTH])
        k_all = _dot(a, w_ref[:, off + RET_WIDTH:off + 2 * RET_WIDTH])
        v_all = _dot(a, w_ref[:, off + 2 * RET_WIDTH:off + 3 * RET_WIDTH])
        gate = _dot(a, w_ref[:, off + 3 * RET_WIDTH:off + 4 * RET_WIDTH])
        cos = cos_ref[bs, :]
        sin = sin_ref[bs, :]
        def scores(h):
            hs = slice(h * HEAD_DIM, (h + 1) * HEAD_DIM)
            q = _rope(q_all[:, hs], cos, sin)
            k = _rope(k_all[:, hs], cos, sin)
            return q, k, _dot_t(q.astype(BF16), k.astype(BF16))

        ahead = scores(0)
        for p in range(HEAD_PAIRS):
            ps = slice(p * PAIR_DIM, (p + 1) * PAIR_DIM)
            pair = [ahead, scores(2 * p + 1)]
            if p + 1 < HEAD_PAIRS:
                ahead = scores(2 * p + 2)
            q = jnp.concatenate([pair[0][0], pair[1][0]], axis=1)
            k = jnp.concatenate([pair[0][1], pair[1][1]], axis=1)
            v = v_all[:, ps].astype(BF16)
            state = r_ref[p]
            cross = _dot((q * qdec_ref[p]).astype(BF16), state.astype(BF16))
            kv = _dot_rows((k * kdec_ref[p]).astype(BF16), v)
            r_ref[p] = state * bdec_ref[p] + kv * blk_ref[...]
            for half, (_, _, s) in enumerate(pair):
                h = 2 * p + half
                hs = slice(h * HEAD_DIM, (h + 1) * HEAD_DIM)
                ls = slice(half * HEAD_DIM, (half + 1) * HEAD_DIM)
                o = _dot((s * mask_ref[h]).astype(BF16), v[:, ls]) + cross[:, ls]
                mu = jnp.mean(o, axis=-1, keepdims=True)
                d = o - mu
                var = jnp.mean(d * d, axis=-1, keepdims=True)
                gt = gate[:, hs]
                y_ref[bs, hs] = (gt * _sigmoid_exp(gt) * d * lax.rsqrt(var + EPS)).astype(BF16)


def _mixer_call(x, g, w_in, cos, sin, mask, qdec, kdec, bdec, blk, meta, cos_m, sin_m, kdec_m):
    bsz, seq, _ = x.shape
    const = lambda a, **kw: pl.BlockSpec(a.shape, lambda b, i: (0,) * a.ndim, **kw)
    return pl.pallas_call(
        _mixer_kernel,
        grid=(bsz, seq // MIXER_TILE),
        in_specs=[
            pl.BlockSpec((None, MIXER_TILE, D_MODEL), lambda b, i: (b, i, 0)),
            const(g),
            const(w_in, pipeline_mode=pl.Buffered(1)),
            pl.BlockSpec((MIXER_TILE, HEAD_DIM), lambda b, i: (i, 0)),
            pl.BlockSpec((MIXER_TILE, HEAD_DIM), lambda b, i: (i, 0)),
            const(mask), const(qdec), const(kdec), const(bdec), const(blk),
            const(meta), const(cos_m), const(sin_m), const(kdec_m),
        ],
        out_specs=(
            pl.BlockSpec((None, MIXER_TILE, SSM_WIDTH), lambda b, i: (b, i, 0)),
            pl.BlockSpec((None, MIXER_TILE, RET_WIDTH), lambda b, i: (b, i, 0)),
            pl.BlockSpec((N_META, SSM_WIDTH), lambda b, i: (0, 0)),
        ),
        out_shape=(jax.ShapeDtypeStruct((bsz, seq, SSM_WIDTH), F32),
                   jax.ShapeDtypeStruct((bsz, seq, RET_WIDTH), BF16),
                   jax.ShapeDtypeStruct((N_META, SSM_WIDTH), F32)),
        scratch_shapes=[pltpu.VMEM((D_MODEL, IN_WIDTH), BF16),
                        pltpu.VMEM((HEAD_PAIRS, PAIR_DIM, PAIR_DIM), F32),
                        pltpu.VMEM((HEAD_PAIRS, PAIR_DIM, PAIR_DIM), F32)],
        compiler_params=pltpu.CompilerParams(
            dimension_semantics=("arbitrary", "arbitrary"), vmem_limit_bytes=VMEM_LIMIT),
        name="mixer_call",
    )(x, g, w_in, cos, sin, mask, qdec, kdec, bdec, blk, meta, cos_m, sin_m, kdec_m)


def _dot_t(a, b):
    return lax.dot_general(a, b, (((1,), (1,)), ((), ())), preferred_element_type=F32)


def _chunk_transpose(arrs):
    n = len(arrs)
    chunk = lax.broadcasted_iota(I32, (1, LANES), 1) // SSM_GROUP
    arrs = list(arrs)
    s = n // 2
    while s:
        keep = (chunk & s) == 0
        nxt = list(arrs)
        for i in range(n):
            if i & s == 0:
                lo, hi = arrs[i], arrs[i + s]
                nxt[i] = jnp.where(keep, lo, pltpu.roll(hi, s * SSM_GROUP, axis=1))
                nxt[i + s] = jnp.where(keep, pltpu.roll(lo, LANES - s * SSM_GROUP, axis=1), hi)
        arrs = nxt
        s //= 2
    return arrs


def _s5_kernel(u_lo_ref, u_hi_ref, um_ref, krow_ref, bmat_ref, cre_ref, cim_ref, ar_ref, ai_ref, wglu_ref,
               y_lo_ref, y_hi_ref, t0_ref, ug_ref, yg_ref):
    n_blocks = u_lo_ref.shape[0] // S5_BLOCK

    @pl.when(pl.program_id(0) == 0)
    def _():
        lane = lax.broadcasted_iota(I32, (SSM_GROUP, S5_LANES), 1)
        for g in range(SSM_GROUPS):
            k0 = krow_ref[g]
            for a in range(S5_BLOCK):
                blk = k0 if a == 0 else jnp.where(lane >= a * SSM_GROUP,
                                                  pltpu.roll(k0, a * SSM_GROUP, axis=1), 0.0)
                t0_ref[g, a * SSM_GROUP:(a + 1) * SSM_GROUP, :] = blk.astype(BF16)

    per_tile = LANES // SSM_GROUP
    for half, uh_ref in enumerate((u_lo_ref, u_hi_ref)):
        for t in range(S5_BLOCK // per_tile):
            words = [pltpu.bitcast(uh_ref[pl.ds(t * per_tile + k, n_blocks, stride=S5_BLOCK), :].astype(BF16),
                                   jnp.uint32) for k in range(per_tile)]
            for m, w in enumerate(_chunk_transpose(words)):
                ug_ref[half * per_tile + m, :, t * LANES:(t + 1) * LANES] = pltpu.bitcast(w, BF16)

    re, im, re0, im0 = [], [], [], []
    for p in range(SSM_GROUPS // 2):
        g0, g1 = 2 * p, 2 * p + 1
        v = _dot(ug_ref[g0], bmat_ref[g0]) + _dot(ug_ref[g1], bmat_ref[g1])
        v0 = (_dot(um_ref[g0].astype(BF16), bmat_ref[g0]) + _dot(um_ref[g1].astype(BF16), bmat_ref[g1]))[0:1]
        re.append(v[:, :LANES])
        im.append(v[:, LANES:])
        re0.append(v0[:, :LANES])
        im0.append(v0[:, LANES:])
    re, im, re0, im0 = (jnp.concatenate(parts, axis=1) for parts in (re, im, re0, im0))

    row = lax.broadcasted_iota(I32, re.shape, 0)
    ar, ai = ar_ref[0:1], ai_ref[0:1]
    re = re + jnp.where(row == 0, ar * re0 - ai * im0, 0.0)
    im = im + jnp.where(row == 0, ar * im0 + ai * re0, 0.0)
    d = 1
    while d < SCAN_ROWS:
        ar, ai = ar_ref[d - 1:d], ai_ref[d - 1:d]
        inside = row % SCAN_ROWS >= d
        sre = jnp.where(inside, pltpu.roll(re, d, axis=0), 0.0)
        sim = jnp.where(inside, pltpu.roll(im, d, axis=0), 0.0)
        re, im = re + ar * sre - ai * sim, im + ar * sim + ai * sre
        d *= 2
    ar, ai = ar_ref[...], ai_ref[...]
    re_tiles, im_tiles = [re[:SCAN_ROWS]], [im[:SCAN_ROWS]]
    for r in range(1, n_blocks // SCAN_ROWS):
        cre_ = re_tiles[-1][SCAN_ROWS - 1:SCAN_ROWS]
        cim_ = im_tiles[-1][SCAN_ROWS - 1:SCAN_ROWS]
        rs = slice(r * SCAN_ROWS, (r + 1) * SCAN_ROWS)
        re_tiles.append(re[rs] + ar * cre_ - ai * cim_)
        im_tiles.append(im[rs] + ar * cim_ + ai * cre_)
    re = jnp.concatenate(re_tiles, axis=0)
    im = jnp.concatenate(im_tiles, axis=0)
    pre = jnp.where(row == 0, re0, pltpu.roll(re, 1, axis=0)).astype(BF16)
    pim = jnp.where(row == 0, im0, pltpu.roll(im, 1, axis=0)).astype(BF16)

    for g in range(SSM_GROUPS):
        ps = slice((g // 2) * LANES, (g // 2 + 1) * LANES)
        yg_ref[g] = (_dot(ug_ref[g], t0_ref[g]) + _dot_t(pre[:, ps], cre_ref[g])
                     + _dot_t(pim[:, ps], cim_ref[g])).astype(BF16)

    wglu = wglu_ref[...]
    for t in range(S5_BLOCK // per_tile):
        ts = slice(t * LANES, (t + 1) * LANES)
        halves = [_chunk_transpose([pltpu.bitcast(yg_ref[half * per_tile + m, :, ts], jnp.uint32)
                                    for m in range(per_tile)])
                  for half in range(SSM_GROUPS // per_tile)]
        for k in range(per_tile):
            y = jnp.concatenate([pltpu.bitcast(h[k], BF16) for h in halves], axis=1).astype(F32)
            y = jax.nn.gelu(y, approximate=True)
            y = y * _sigmoid(_dot(y.astype(BF16), wglu))
            i = t * per_tile + k
            y_lo_ref[pl.ds(i, n_blocks, stride=S5_BLOCK), :] = y[:, :LANES]
            y_hi_ref[pl.ds(i, n_blocks, stride=S5_BLOCK), :] = y[:, LANES:]


def _s5_call(u, um, krow, bmat, cre, cim, ar, ai, wglu):
    bsz, seq, _ = u.shape
    n_blocks = seq // S5_BLOCK
    const = lambda a: pl.BlockSpec(a.shape, lambda b: (0,) * a.ndim)
    return pl.pallas_call(
        _s5_kernel,
        grid=(bsz,),
        in_specs=[pl.BlockSpec((None, seq, LANES), lambda b: (b, 0, 0)),
                  pl.BlockSpec((None, seq, LANES), lambda b: (b, 0, 1)),
                  const(um), const(krow), const(bmat), const(cre), const(cim), const(ar), const(ai),
                  const(wglu)],
        out_specs=(pl.BlockSpec((None, seq, LANES), lambda b: (b, 0, 0)),
                   pl.BlockSpec((None, seq, LANES), lambda b: (b, 0, 0))),
        out_shape=(jax.ShapeDtypeStruct((bsz, seq, LANES), F32),
                   jax.ShapeDtypeStruct((bsz, seq, LANES), F32)),
        scratch_shapes=[pltpu.VMEM((SSM_GROUPS, S5_LANES, S5_LANES), BF16),
                        pltpu.VMEM((SSM_GROUPS, n_blocks, S5_LANES), BF16),
                        pltpu.VMEM((SSM_GROUPS, n_blocks, S5_LANES), BF16)],
        compiler_params=pltpu.CompilerParams(
            dimension_semantics=("arbitrary",), vmem_limit_bytes=VMEM_LIMIT),
        name="s5_call",
    )(u, u, um, krow, bmat, cre, cim, ar, ai, wglu)


def _s5_operators(lam_re, lam_im, log_dt, b_re, b_im, c_re, c_im, d_skip):
    n_groups = lam_re.shape[0]
    lam = lax.complex(lam_re, lam_im)
    lam_dt = lam * jnp.exp(log_dt)[:, None]
    lam_bar = jnp.exp(lam_dt)
    b_bar = ((lam_bar - 1.0) / lam)[..., None] * lax.complex(b_re, b_im)
    c = lax.complex(c_re, c_im)
    tau = jnp.arange(S5_BLOCK + 1, dtype=F32)
    pows = jnp.exp(lam_dt[None] * tau[:, None, None])
    kern = jnp.real(jnp.einsum('ghp,tgp,gpk->gkth', c, pows[:S5_BLOCK], b_bar))
    skip = (jnp.eye(SSM_GROUP, dtype=F32)[None, :, None, :] * d_skip[:, None, None, :]
            * (tau[:S5_BLOCK] == 0).astype(F32)[None, None, :, None])
    krow = (kern + skip).reshape(n_groups, SSM_GROUP, S5_LANES)
    even = (jnp.arange(n_groups) % 2 == 0)[:, None, None]
    pair_pad = lambda m: jnp.concatenate([jnp.where(even, m, 0.0), jnp.where(even, 0.0, m)], axis=-1)
    bm = (pows[S5_BLOCK - 1 - jnp.arange(S5_BLOCK)].transpose(1, 0, 2)[:, :, None, :]
          * b_bar.transpose(0, 2, 1)[:, None, :, :]).reshape(n_groups, S5_LANES, SSM_STATE)
    bmat = jnp.concatenate([pair_pad(jnp.real(bm)), pair_pad(jnp.imag(bm))], axis=-1)
    cm = (pows[1:].transpose(1, 0, 2)[:, :, None, :] * c[:, None, :, :]).reshape(n_groups, S5_LANES, SSM_STATE)
    cre, cim = pair_pad(jnp.real(cm)), pair_pad(-jnp.imag(cm))
    step = S5_BLOCK * (1.0 + jnp.arange(SCAN_ROWS, dtype=F32))
    adec = jnp.exp(lam_dt[None, :, :] * step[:, None, None]).reshape(SCAN_ROWS, n_groups * SSM_STATE)
    return krow, bmat.astype(BF16), cre.astype(BF16), cim.astype(BF16), jnp.real(adec), jnp.imag(adec)


def _first_hit(values, target):
    hits, taken = [], None
    for v in values:
        hit = (v >= target) if taken is None else jnp.logical_and(v >= target, jnp.logical_not(taken))
        taken = hit if taken is None else jnp.logical_or(taken, hit)
        hits.append(hit)
    return hits


def _stack_rows(rows, n_rows):
    idx = lax.broadcasted_iota(I32, (n_rows, rows[0].shape[1]), 0)
    out = jnp.zeros((n_rows, rows[0].shape[1]), F32)
    for k, r in enumerate(rows):
        out = jnp.where(idx == k, r, out)
    return out


def _project(rs, x_ref, ys_lo_ref, ys_hi_ref, yr_ref, wout_ref, h_ref):
    ys = jnp.concatenate([ys_lo_ref[rs, :], ys_hi_ref[rs, :]], axis=1).astype(BF16)
    h = x_ref[rs, :] + _dot(ys, wout_ref[:SSM_WIDTH, :]) + _dot(yr_ref[rs, :], wout_ref[SSM_WIDTH:, :])
    h_ref[rs, :] = h.astype(BF16)
    return h


def _route(lt, tri_ref):
    gl = [lt[g:g + 1, :] for g in range(N_GROUPS)]
    gmax = functools.reduce(jnp.maximum, gl)
    g_w = 1.0 / functools.reduce(lambda a, b: a + b, [jnp.exp(l - gmax) for l in gl])
    sel = _first_hit(gl, gmax)
    ev = []
    for e in range(EXPERTS_PER_GROUP):
        acc = jnp.zeros_like(gmax)
        for g in range(N_GROUPS):
            k = N_GROUPS + g * EXPERTS_PER_GROUP + e
            acc = jnp.where(sel[g], lt[k:k + 1, :], acc)
        ev.append(acc)
    m1 = functools.reduce(jnp.maximum, ev)
    first = _first_hit(ev, m1)
    rest = [jnp.where(f, -jnp.inf, v) for f, v in zip(first, ev)]
    m2 = functools.reduce(jnp.maximum, rest)
    second = _first_hit(rest, m2)
    e2 = jnp.exp(m2 - m1)
    w1 = g_w / (1.0 + e2)
    w2 = e2 * w1
    combine = [jnp.where(f, w1, 0.0) + jnp.where(s, w2, 0.0) for f, s in zip(first, second)]

    sel_f = [jnp.where(s, 1.0, 0.0) for s in sel]
    incl = _dot(_stack_rows(sel_f, 8).astype(BF16), tri_ref[...])
    dest = jnp.zeros_like(gmax)
    seg_start = jnp.zeros((1, 1), F32)
    counts = []
    for g in range(N_GROUPS):
        run = incl[g:g + 1, :]
        cnt = run[:, TILE - 1:TILE]
        counts.append(cnt)
        dest = dest + sel_f[g] * (seg_start + run - 1.0)
        seg_start = seg_start + PIECE * jnp.floor((cnt + (PIECE - 1.0)) * (1.0 / PIECE))
    return combine, dest, counts


def _proj_kernel(x_ref, ys_lo_ref, ys_hi_ref, yr_ref, wout32_ref, g_ref, wr_ref, br_ref,
                 tri_ref, h_ref, stage_ref, dest_ref, cnt_ref, wout_ref):
    @pl.when(pl.program_id(0) == 0)
    def _():
        wout_ref[...] = wout32_ref[...].astype(BF16)

    tiles = range(PROJ_TILES)
    proj_refs = (x_ref, ys_lo_ref, ys_hi_ref, yr_ref, wout_ref, h_ref)
    chunks = [[slice(tile * TILE + b * ROW_CHUNK, tile * TILE + (b + 1) * ROW_CHUNK)
               for b in range(TILE // ROW_CHUNK)] for tile in tiles]
    h_parts = [[_project(rs, *proj_refs) for rs in chunks[tile]] for tile in tiles]

    logits, staged = [], []
    for tile in tiles:
        parts, rows, inv = [], [], []
        for h in h_parts[tile]:
            t = _rms_norm(h, g_ref[...])
            parts.append(t.astype(BF16))
            peak = jnp.maximum(jnp.max(jnp.abs(t), axis=1, keepdims=True), 1e-30)
            rows.append((t * (FP8_HEADROOM / peak)).astype(BF16))
            inv.append(peak * (1.0 / FP8_HEADROOM))
        staged.append((jnp.concatenate(rows, axis=0), jnp.concatenate(inv, axis=0)))
        lt_parts = []
        for t_hi in parts:
            both = _dot_t(wr_ref[...], t_hi)
            lt_parts.append(both[:ROUTE_ROWS] + both[ROUTE_ROWS:])
        logits.append(jnp.concatenate(lt_parts, axis=1) + br_ref[...])

    routed = [_route(logits[tile], tri_ref) for tile in tiles]

    for tile in tiles:
        combine, dest, counts = routed[tile]
        dest_ref[tile] = _stack_rows([dest], 8)
        cnt_ref[tile] = _stack_rows([c + jnp.zeros((1, LANES), F32) for c in counts], 8)
        perm = jnp.where(lax.broadcasted_iota(I32, (SORTED_ROWS, TILE), 0) == dest.astype(I32),
                         1.0, 0.0).astype(BF16)
        c_hi = [c.astype(BF16).astype(F32) for c in combine]
        c_lo = [c - hi for c, hi in zip(combine, c_hi)]
        cw = _stack_rows(c_hi + c_lo, LANES).T
        rows, inv = staged[tile]
        inv_hi = inv.astype(BF16).astype(F32)
        lane = lax.broadcasted_iota(I32, (1, LANES), 1)
        cw = jnp.where(lane == INV_LANE, inv_hi, jnp.where(lane == INV_LANE + 1, inv - inv_hi, cw))
        t_ext = jnp.concatenate([rows, cw.astype(BF16)], axis=1)
        stage_ref[tile * SORTED_ROWS:(tile + 1) * SORTED_ROWS, :] = _dot(perm, t_ext).astype(BF16)


def _proj_call(x, ys_lo, ys_hi, yr, wout, g, wr, br, tri):
    n_tok = x.shape[0]
    n_tiles = n_tok // TILE
    const = lambda *shape: pl.BlockSpec(shape, lambda i: (0,) * len(shape))
    rows = lambda width: pl.BlockSpec((PROJ_TILES * TILE, width), lambda i: (i, 0))
    return pl.pallas_call(
        _proj_kernel,
        grid=(n_tiles // PROJ_TILES,),
        in_specs=[
            rows(D_MODEL), rows(LANES), rows(LANES), rows(RET_WIDTH),
            pl.BlockSpec((D_MODEL, D_MODEL), lambda i: (0, 0), pipeline_mode=pl.Buffered(1)), const(1, D_MODEL),
            const(2 * ROUTE_ROWS, D_MODEL), const(ROUTE_ROWS, 1), const(TILE, TILE),
        ],
        out_specs=(rows(D_MODEL),
                   pl.BlockSpec((PROJ_TILES * SORTED_ROWS, EXT_WIDTH), lambda i: (i, 0)),
                   pl.BlockSpec((PROJ_TILES, 8, TILE), lambda i: (i, 0, 0)),
                   pl.BlockSpec((PROJ_TILES, 8, LANES), lambda i: (i, 0, 0))),
        scratch_shapes=[pltpu.VMEM((D_MODEL, D_MODEL), BF16)],
        out_shape=(jax.ShapeDtypeStruct((n_tok, D_MODEL), BF16),
                   jax.ShapeDtypeStruct((n_tiles * SORTED_ROWS, EXT_WIDTH), BF16),
                   jax.ShapeDtypeStruct((n_tiles, 8, TILE), F32),
                   jax.ShapeDtypeStruct((n_tiles, 8, LANES), F32)),
        compiler_params=pltpu.CompilerParams(
            dimension_semantics=("arbitrary",), vmem_limit_bytes=VMEM_LIMIT),
        name="proj_call",
    )(x, ys_lo, ys_hi, yr, wout, g, wr, br, tri)


def _sort_tables(cnt, n_steps):
    n_tiles = cnt.shape[0]
    npc = (cnt + PIECE - 1) // PIECE
    seg = jnp.cumsum(npc, axis=1) - npc
    before = jnp.cumsum(npc, axis=0) - npc
    n_tile_g = (jnp.sum(npc, axis=0) + MOE_PIECES - 1) // MOE_PIECES
    t_off = jnp.cumsum(n_tile_g) - n_tile_g
    j = jnp.arange(SORTED_PIECES, dtype=I32)[None, :, None]
    in_g = jnp.logical_and(j >= seg[:, None, :], j < (seg + npc)[:, None, :])
    pos = jnp.sum(jnp.where(in_g, MOE_PIECES * t_off[None, None, :] + before[:, None, :] + j - seg[:, None, :], 0),
                  axis=-1)
    valid = jnp.any(in_g, axis=-1)
    steps = jnp.arange(n_steps, dtype=I32)
    g_step = jnp.minimum(jnp.sum(steps[:, None] >= (t_off + n_tile_g)[None, :], axis=1), N_GROUPS - 1)
    slot = jnp.arange(n_steps * MOE_PIECES, dtype=I32)
    g_slot = jnp.repeat(g_step, MOE_PIECES)[:, None]
    p = slot[:, None] - MOE_PIECES * t_off[None, :]
    end = before + npc
    off = seg - before + SORTED_PIECES * jnp.arange(n_tiles, dtype=I32)[:, None]
    passed = end[None, :-1, :] <= p[:, None, :]
    piece = p + off[0][None, :] + jnp.sum(jnp.where(passed, (off[1:] - off[:-1])[None], 0), axis=1)
    live = jnp.logical_and(jnp.arange(N_GROUPS, dtype=I32)[None, :] == g_slot,
                           jnp.logical_and(p >= 0, p < end[-1][None, :]))
    src = jnp.sum(jnp.where(live, piece, 0), axis=1).astype(I32)
    n_live = jnp.sum(n_tile_g).reshape(1)
    back = jnp.where(valid, pos, 0).reshape(-1)
    return src, g_step.astype(I32), n_live.astype(I32), back.astype(I32)


def _piece_copy(src_ref, piece, buf_ref, slot, j, sem_ref):
    start = piece * PIECE if isinstance(piece, int) else pl.multiple_of(piece * PIECE, PIECE)
    return pltpu.make_async_copy(src_ref.at[pl.ds(start, PIECE)],
                                 buf_ref.at[slot, pl.ds(j * PIECE, PIECE)], sem_ref.at[slot])


def _fetch_pieces(table_ref, src_ref, buf_ref, sem_ref, row, slot, n_pieces):
    for j in range(n_pieces):
        _piece_copy(src_ref, table_ref[row * n_pieces + j], buf_ref, slot, j, sem_ref).start()


def _wait_pieces(src_ref, buf_ref, sem_ref, slot, n_pieces):
    for j in range(n_pieces):
        _piece_copy(src_ref, 0, buf_ref, slot, j, sem_ref).wait()


def _gather_pieces(table_ref, src_ref, buf_ref, sem_ref, n_pieces):
    step = pl.program_id(0)
    last = pl.num_programs(0) - 1
    args = (src_ref, buf_ref, sem_ref)

    @pl.when(step == 0)
    def _():
        _fetch_pieces(table_ref, *args, step, 0, n_pieces)

    _wait_pieces(*args, step % 2, n_pieces)

    def fetch_next():
        _fetch_pieces(table_ref, *args, jnp.minimum(step + 1, last), (step + 1) % 2, n_pieces)

    def drain():
        @pl.when(step == last)
        def _():
            _wait_pieces(*args, (step + 1) % 2, n_pieces)

    return fetch_next, drain


def _to_fp8(x, headroom=FP8_HEADROOM):
    peak = jnp.maximum(jnp.max(jnp.abs(x), axis=(0, 1), keepdims=True), 1e-30)
    return (x * (headroom / peak)).astype(FP8), peak * (1.0 / headroom)


def _moe_kernel(src_ref, gstep_ref, nlive_ref, stage_ref, wg_ref, wu_ref, wd_ref, y_ref,
                buf_ref, sem_ref, wgb_ref, wub_ref, wdb_ref, inv_ref):
    step = pl.program_id(0)
    fetch_next, drain = _gather_pieces(src_ref, stage_ref, buf_ref, sem_ref, MOE_PIECES)

    @pl.when(jnp.logical_or(step == 0, gstep_ref[step] != gstep_ref[jnp.maximum(step - 1, 0)]))
    def _():
        ones = jnp.ones((1, LANES), F32)
        for e in range(EXPERTS_PER_GROUP):
            wgb_ref[e], inv = _to_fp8(wg_ref[e])
            inv_ref[e:e + 1, :] = inv * ones
            wub_ref[e], inv = _to_fp8(wu_ref[e])
            inv_ref[EXPERTS_PER_GROUP + e:EXPERTS_PER_GROUP + e + 1, :] = inv * ones
        peak = functools.reduce(jnp.maximum, [jnp.max(jnp.abs(wd_ref[e]), axis=(0, 1), keepdims=True)
                                              for e in range(EXPERTS_PER_GROUP)])
        peak = jnp.maximum(peak, 1e-30)
        for e in range(EXPERTS_PER_GROUP):
            wdb_ref[e * EXPERT_FF:(e + 1) * EXPERT_FF, :] = (wd_ref[e] * (FP8_HEADROOM / peak)).astype(FP8)
        inv_ref[2 * EXPERTS_PER_GROUP:2 * EXPERTS_PER_GROUP + 1, :] = peak * (1.0 / FP8_HEADROOM) * ones

    @pl.when(step < nlive_ref[0])
    def _():
        slot = step % 2
        fetch_next()

        def up(rs):
            t = buf_ref[slot, rs, :D_MODEL].astype(FP8)
            t_inv = (buf_ref[slot, rs, D_MODEL + INV_LANE:D_MODEL + INV_LANE + 1].astype(F32)
                     + buf_ref[slot, rs, D_MODEL + INV_LANE + 1:D_MODEL + INV_LANE + 2].astype(F32))
            out = []
            for e in range(EXPERTS_PER_GROUP):
                g_inv = inv_ref[e:e + 1, 0:1] * t_inv
                u_inv = inv_ref[EXPERTS_PER_GROUP + e:EXPERTS_PER_GROUP + e + 1, 0:1] * t_inv
                out.append((_dot(t, wgb_ref[e]) * g_inv, _dot(t, wub_ref[e]), u_inv))
            return out

        def down(rs, hidden):
            cw = buf_ref[slot, rs, D_MODEL:].astype(F32)
            acts = []
            for e, (hg, hu, u_inv) in enumerate(hidden):
                c = (cw[:, e:e + 1] + cw[:, EXPERTS_PER_GROUP + e:EXPERTS_PER_GROUP + e + 1]) * u_inv
                acts.append(hg * _sigmoid(hg) * hu * c)
            act, a_inv = _to_fp8(jnp.concatenate(acts, axis=1))
            d_inv = inv_ref[2 * EXPERTS_PER_GROUP:2 * EXPERTS_PER_GROUP + 1, 0:1] * a_inv
            y_ref[rs, :] = (_dot(act, wdb_ref[...]) * d_inv).astype(BF16)

        chunks = [slice(b * ROW_CHUNK, (b + 1) * ROW_CHUNK) for b in range(MOE_TILE // ROW_CHUNK)]
        hidden = [up(rs) for rs in chunks]
        for rs, hid in zip(chunks, hidden):
            down(rs, hid)

    @pl.when(step >= nlive_ref[0])
    def _():
        fetch_next()
        y_ref[...] = jnp.zeros_like(y_ref)

    drain()


def _moe_call(src, g_step, n_live, stage, wg, wu, wd, n_steps):
    grp = lambda shape: pl.BlockSpec((EXPERTS_PER_GROUP,) + shape, lambda s, src, gs, nl: (gs[s], 0, 0))
    return pl.pallas_call(
        _moe_kernel,
        grid_spec=pltpu.PrefetchScalarGridSpec(
            num_scalar_prefetch=3,
            grid=(n_steps,),
            in_specs=[pl.BlockSpec(memory_space=pl.ANY),
                      grp((D_MODEL, EXPERT_FF)), grp((D_MODEL, EXPERT_FF)), grp((EXPERT_FF, D_MODEL))],
            out_specs=pl.BlockSpec((MOE_TILE, D_MODEL), lambda s, src, gs, nl: (s, 0)),
            scratch_shapes=[pltpu.VMEM((2, MOE_TILE, EXT_WIDTH), BF16), pltpu.SemaphoreType.DMA((2,)),
                            pltpu.VMEM((EXPERTS_PER_GROUP, D_MODEL, EXPERT_FF), FP8),
                            pltpu.VMEM((EXPERTS_PER_GROUP, D_MODEL, EXPERT_FF), FP8),
                            pltpu.VMEM((EXPERTS_PER_GROUP * EXPERT_FF, D_MODEL), FP8),
                            pltpu.VMEM((16, LANES), F32)],
        ),
        out_shape=jax.ShapeDtypeStruct((n_steps * MOE_TILE, D_MODEL), BF16),
        compiler_params=pltpu.CompilerParams(
            dimension_semantics=("arbitrary",), vmem_limit_bytes=VMEM_LIMIT),
        name="moe_call",
    )(src, g_step, n_live, stage, wg, wu, wd)


def _final_kernel(back_ref, ysort_ref, h_ref, dest_ref, g_ref, o_ref, buf_ref, sem_ref):
    step = pl.program_id(0)
    fetch_next, drain = _gather_pieces(back_ref, ysort_ref, buf_ref, sem_ref, FINAL_TILES * SORTED_PIECES)
    fetch_next()
    for t in range(FINAL_TILES):
        dest = _stack_rows([dest_ref[t, 0:1, :]], LANES).T[:, 0:1].astype(I32)
        unperm = jnp.where(lax.broadcasted_iota(I32, (TILE, SORTED_ROWS), 1) == dest, 1.0, 0.0).astype(BF16)
        rows = buf_ref[step % 2, t * SORTED_ROWS:(t + 1) * SORTED_ROWS, :]
        for b in range(TILE // ROW_CHUNK):
            rs = slice(b * ROW_CHUNK, (b + 1) * ROW_CHUNK)
            os = slice(t * TILE + b * ROW_CHUNK, t * TILE + (b + 1) * ROW_CHUNK)
            o_ref[os, :] = _rms_norm(h_ref[os, :].astype(F32) + _dot(unperm[rs], rows), g_ref[...])
    drain()


def _final_call(back, ysort, h, dest, g):
    n_tok = h.shape[0]
    return pl.pallas_call(
        _final_kernel,
        grid_spec=pltpu.PrefetchScalarGridSpec(
            num_scalar_prefetch=1,
            grid=(n_tok // (FINAL_TILES * TILE),),
            in_specs=[pl.BlockSpec(memory_space=pl.ANY),
                      pl.BlockSpec((FINAL_TILES * TILE, D_MODEL), lambda i, back: (i, 0)),
                      pl.BlockSpec((FINAL_TILES, 8, TILE), lambda i, back: (i, 0, 0)),
                      pl.BlockSpec((1, D_MODEL), lambda i, back: (0, 0))],
            out_specs=pl.BlockSpec((FINAL_TILES * TILE, D_MODEL), lambda i, back: (i, 0)),
            scratch_shapes=[pltpu.VMEM((2, FINAL_TILES * SORTED_ROWS, D_MODEL), BF16),
                            pltpu.SemaphoreType.DMA((2,))],
        ),
        out_shape=jax.ShapeDtypeStruct((n_tok, D_MODEL), F32),
        compiler_params=pltpu.CompilerParams(
            dimension_semantics=("arbitrary",), vmem_limit_bytes=VMEM_LIMIT),
        name="final_call",
    )(back, ysort, h, dest, g)


def _rope_tables(length):
    pos = np.arange(length, dtype=np.float32)
    inv_freq = np.float32(ROPE_BASE) ** (-np.arange(0, HEAD_DIM, 2, dtype=np.float32) / np.float32(HEAD_DIM))
    ang = pos[:, None] * inv_freq[None, :]
    cos, sin = np.cos(ang), np.sin(ang)
    return np.concatenate([cos, cos], axis=-1), np.concatenate([-sin, sin], axis=-1)


def _retention_tables():
    f32 = np.float32
    gamma = f32(1.0) - f32(2.0) ** (f32(-5.0) - np.arange(HEADS, dtype=f32))
    log_g = np.log(gamma)[:, None, None]
    scale = f32(HEAD_DIM ** -0.5)
    idx = np.arange(RET_BLOCK)
    dist = np.abs(idx[:, None] - idx[None, :]).astype(f32)
    visible = (idx[None, :] // CHUNK) <= (idx[:, None] // CHUNK)
    mask = np.where(visible[None], np.exp(log_g * dist[None]), f32(0.0)) * scale
    ones = np.ones((1, 1, HEAD_DIM), f32)
    idx_f = idx.astype(f32)[None, :, None]
    pair = lambda a: a.reshape(HEAD_PAIRS, 2, a.shape[1], HEAD_DIM).transpose(0, 2, 1, 3).reshape(
        HEAD_PAIRS, a.shape[1], PAIR_DIM)
    qdec = pair(np.exp(log_g * (idx_f + f32(1.0))) * ones)
    kdec = pair(np.exp(log_g * (f32(RET_BLOCK - 1.0) - idx_f)) * scale * ones)
    meta_idx = np.arange(N_META, dtype=f32)[None, :, None]
    kdec_meta = pair(np.exp(log_g * (f32(N_META - 1.0) - meta_idx)) * scale * ones)
    blk = np.kron(np.eye(2, dtype=f32), np.ones((HEAD_DIM, HEAD_DIM), f32))
    bdec = np.exp(log_g * f32(RET_BLOCK)).reshape(HEAD_PAIRS, 2)
    bdec = np.stack([np.kron(np.diag(b), np.ones((HEAD_DIM, HEAD_DIM), f32)) for b in bdec])
    return tuple(a.astype(f32) for a in (mask, qdec, kdec, bdec, blk, kdec_meta))


def kernel(x, meta_tokens, norm_mix_g, w_in, ssm_lambda_re, ssm_lambda_im, ssm_log_dt, ssm_b_re, ssm_b_im, ssm_c_re, ssm_c_im, ssm_d, w_glu, w_out, norm_ffn_g, w_router_group, b_router_group, w_router_expert, b_router_expert, w_gate, w_up, w_down, norm_final_g):
    bsz, seq, _ = x.shape
    assert seq % TILE == 0 and seq % MIXER_TILE == 0 and MIXER_TILE % RET_BLOCK == 0 and RET_BLOCK % CHUNK == 0
    n_blocks = seq // S5_BLOCK
    assert n_blocks % SCAN_ROWS == 0
    n_tok = bsz * seq
    assert n_tok % (FINAL_TILES * TILE) == 0 and n_tok % (PROJ_TILES * TILE) == 0
    n_tiles = n_tok // TILE
    n_steps = -(-n_tiles * (TILE_PIECES + N_GROUPS - 1) // MOE_PIECES) + N_GROUPS

    cos, sin = _rope_tables(N_META + seq)
    mask, qdec, kdec, bdec, blk, kdec_meta = _retention_tables()
    g_mix = norm_mix_g[0][None, :]
    u, y_ret, u_meta = _mixer_call(x, g_mix, w_in[0], cos[N_META:], sin[N_META:], mask, qdec, kdec, bdec, blk,
                                   meta_tokens, cos[:N_META], sin[:N_META], kdec_meta)

    s5_ops = _s5_operators(
        ssm_lambda_re[0], ssm_lambda_im[0], ssm_log_dt[0], ssm_b_re[0], ssm_b_im[0],
        ssm_c_re[0], ssm_c_im[0], ssm_d[0])
    um = u_meta.reshape(S5_BLOCK, SSM_GROUPS, SSM_GROUP).transpose(1, 0, 2).reshape(SSM_GROUPS, 1, S5_LANES)
    um = jnp.pad(um, ((0, 0), (0, 7), (0, 0)))
    y_lo, y_hi = _s5_call(u, um, *s5_ops, w_glu[0].astype(BF16))

    w_r = jnp.concatenate(
        [w_router_group[0].T, w_router_expert[0].transpose(0, 2, 1).reshape(N_EXPERTS, D_MODEL)], axis=0)
    w_r = jnp.pad(w_r, ((0, ROUTE_ROWS - w_r.shape[0]), (0, 0)))
    b_r = jnp.concatenate([b_router_group[0], b_router_expert[0].reshape(-1)])
    b_r = jnp.pad(b_r, (0, ROUTE_ROWS - b_r.shape[0]))[:, None]
    w_r_hi = w_r.astype(BF16)
    w_r = jnp.concatenate([w_r_hi, (w_r - w_r_hi.astype(F32)).astype(BF16)], axis=0)
    tri = jnp.asarray(np.arange(TILE)[:, None] <= np.arange(TILE)[None, :], BF16)

    h, stage, dest, cnt = _proj_call(
        x.reshape(n_tok, D_MODEL), y_lo.reshape(n_tok, LANES), y_hi.reshape(n_tok, LANES),
        y_ret.reshape(n_tok, RET_WIDTH),
        w_out[0], norm_ffn_g[0][None, :], w_r, b_r, tri)
    src, g_step, n_live, back = _sort_tables(cnt[:, :N_GROUPS, 0].astype(I32), n_steps)
    y_sorted = _moe_call(src, g_step, n_live, stage, w_gate[0], w_up[0], w_down[0], n_steps)
    out = _final_call(back, y_sorted, h, dest, norm_final_g[None, :])
    return out.reshape(bsz, seq, D_MODEL)
```

```python
import functools

import jax
import jax.numpy as jnp
import numpy as np
from jax import lax
from jax.experimental import pallas as pl
from jax.experimental.pallas import tpu as pltpu

D_MODEL = 1024
N_META = 16
CHUNK = 64
EPS = 1e-6
SSM_WIDTH = 256
SSM_GROUP = 16
SSM_GROUPS = 16
SSM_STATE = 64
RET_WIDTH = 768
HEAD_DIM = 128
HEADS = 6
HEAD_PAIRS = HEADS // 2
PAIR_DIM = 2 * HEAD_DIM
ROPE_BASE = 10000.0
IN_WIDTH = SSM_WIDTH + 4 * RET_WIDTH
N_GROUPS = 4
EXPERTS_PER_GROUP = 4
N_EXPERTS = 16
EXPERT_FF = 256

S5_BLOCK = 16
S5_LANES = S5_BLOCK * SSM_GROUP
SCAN_ROWS = 8
RET_BLOCK = 256
TILE = 512
MIXER_TILE = 1024
LANES = 128
ROUTE_ROWS = 32
ROW_CHUNK = 256
PROJ_TILES = 2
FINAL_TILES = 4
PIECE = 16
TILE_PIECES = TILE // PIECE
MOE_TILE = 1024
MOE_PIECES = MOE_TILE // PIECE
SORTED_PIECES = TILE_PIECES + N_GROUPS
SORTED_ROWS = SORTED_PIECES * PIECE
EXT_WIDTH = D_MODEL + LANES
INV_LANE = 2 * EXPERTS_PER_GROUP
V7X_VMEM_BYTES = 64 * 1024 * 1024
VMEM_LIMIT = V7X_VMEM_BYTES * 7 // 8

F32 = jnp.float32
BF16 = jnp.bfloat16
FP8 = jnp.float8_e4m3fn
FP8_HEADROOM = 256.0
I32 = jnp.int32


def _dot(a, b):
    return jnp.dot(a, b, preferred_element_type=F32)


def _sigmoid(x):
    return 0.5 * jnp.tanh(0.5 * x) + 0.5


def _sigmoid_exp(x):
    return 1.0 / (1.0 + jnp.exp(-x))


def _rms_norm(x, g):
    return x * lax.rsqrt(jnp.mean(x * x, axis=-1, keepdims=True) + EPS) * g


def _rope(t, cos, sin_signed):
    return t * cos + pltpu.roll(t, HEAD_DIM // 2, axis=1) * sin_signed


def _meta_state(meta_ref, g_ref, w_ref, cos_ref, sin_ref, kdec_ref, blk_ref, u_ref, r0_ref):
    a = _rms_norm(meta_ref[...], g_ref[...]).astype(BF16)
    u_ref[...] = _dot(a, w_ref[:, :SSM_WIDTH])
    k_off = SSM_WIDTH + RET_WIDTH
    v_off = SSM_WIDTH + 2 * RET_WIDTH
    cos = cos_ref[...]
    sin = sin_ref[...]
    for p in range(HEAD_PAIRS):
        k = _dot(a, w_ref[:, k_off + p * PAIR_DIM:k_off + (p + 1) * PAIR_DIM])
        v = _dot(a, w_ref[:, v_off + p * PAIR_DIM:v_off + (p + 1) * PAIR_DIM])
        k = jnp.concatenate([_rope(k[:, :HEAD_DIM], cos, sin), _rope(k[:, HEAD_DIM:], cos, sin)], axis=1)
        kd = (k * kdec_ref[p]).astype(BF16)
        r0_ref[p] = _dot_rows(kd, v.astype(BF16)) * blk_ref[...]


def _dot_rows(a, b):
    return lax.dot_general(a, b, (((0,), (0,)), ((), ())), preferred_element_type=F32)


def _mixer_kernel(x_ref, g_ref, w32_ref, cos_ref, sin_ref, mask_ref, qdec_ref, kdec_ref, bdec_ref, blk_ref,
                  meta_ref, cos_m_ref, sin_m_ref, kdec_m_ref, u_ref, y_ref, um_ref, w_ref, r0_ref, r_ref):
    first_tile = pl.program_id(1) == 0

    @pl.when(jnp.logical_and(pl.program_id(0) == 0, first_tile))
    def _():
        for c in range(0, IN_WIDTH, SSM_WIDTH):
            w_ref[:, c:c + SSM_WIDTH] = w32_ref[:, c:c + SSM_WIDTH].astype(BF16)
        _meta_state(meta_ref, g_ref, w_ref, cos_m_ref, sin_m_ref, kdec_m_ref, blk_ref, um_ref, r0_ref)

    @pl.when(first_tile)
    def _():
        r_ref[...] = r0_ref[...]

    off = SSM_WIDTH
    for b in range(MIXER_TILE // RET_BLOCK):
        bs = slice(b * RET_BLOCK, (b + 1) * RET_BLOCK)
        a = _rms_norm(x_ref[bs, :], g_ref[...]).astype(BF16)
        u_ref[bs, :] = _dot(a, w_ref[:, :SSM_WIDTH])
        q_all = _dot(a, w_ref[:, off:off + RET_WIDTH])
        k_all = _dot(a, w_ref[:, off + RET_WIDTH:off + 2 * RET_WIDTH])
        v_all = _dot(a, w_ref[:, off + 2 * RET_WIDTH:off + 3 * RET_WIDTH])
        gate = _dot(a, w_ref[:, off + 3 * RET_WIDTH:off + 4 * RET_WIDTH])
        cos = cos_ref[bs, :]
        sin = sin_ref[bs, :]
        def scores(h):
            hs = slice(h * HEAD_DIM, (h + 1) * HEAD_DIM)
            q = _rope(q_all[:, hs], cos, sin)
            k = _rope(k_all[:, hs], cos, sin)
            return q, k, _dot_t(q.astype(BF16), k.astype(BF16))

        ahead = scores(0)
        for p in range(HEAD_PAIRS):
            ps = slice(p * PAIR_DIM, (p + 1) * PAIR_DIM)
            pair = [ahead, scores(2 * p + 1)]
            if p + 1 < HEAD_PAIRS:
                ahead = scores(2 * p + 2)
            q = jnp.concatenate([pair[0][0], pair[1][0]], axis=1)
            k = jnp.concatenate([pair[0][1], pair[1][1]], axis=1)
            v = v_all[:, ps].astype(BF16)
            state = r_ref[p]
            cross = _dot((q * qdec_ref[p]).astype(BF16), state.astype(BF16))
            kv = _dot_rows((k * kdec_ref[p]).astype(BF16), v)
            r_ref[p] = state * bdec_ref[p] + kv * blk_ref[...]
            for half, (_, _, s) in enumerate(pair):
                h = 2 * p + half
                hs = slice(h * HEAD_DIM, (h + 1) * HEAD_DIM)
                ls = slice(half * HEAD_DIM, (half + 1) * HEAD_DIM)
                o = _dot((s * mask_ref[h]).astype(BF16), v[:, ls]) + cross[:, ls]
                mu = jnp.mean(o, axis=-1, keepdims=True)
                d = o - mu
                var = jnp.mean(d * d, axis=-1, keepdims=True)
                gt = gate[:, hs]
                y_ref[bs, hs] = (gt * _sigmoid_exp(gt) * d * lax.rsqrt(var + EPS)).astype(BF16)


def _mixer_call(x, g, w_in, cos, sin, mask, qdec, kdec, bdec, blk, meta, cos_m, sin_m, kdec_m):
    bsz, seq, _ = x.shape
    const = lambda a, **kw: pl.BlockSpec(a.shape, lambda b, i: (0,) * a.ndim, **kw)
    return pl.pallas_call(
        _mixer_kernel,
        grid=(bsz, seq // MIXER_TILE),
        in_specs=[
            pl.BlockSpec((None, MIXER_TILE, D_MODEL), lambda b, i: (b, i, 0)),
            const(g),
            const(w_in, pipeline_mode=pl.Buffered(1)),
            pl.BlockSpec((MIXER_TILE, HEAD_DIM), lambda b, i: (i, 0)),
            pl.BlockSpec((MIXER_TILE, HEAD_DIM), lambda b, i: (i, 0)),
            const(mask), const(qdec), const(kdec), const(bdec), const(blk),
            const(meta), const(cos_m), const(sin_m), const(kdec_m),
        ],
        out_specs=(
            pl.BlockSpec((None, MIXER_TILE, SSM_WIDTH), lambda b, i: (b, i, 0)),
            pl.BlockSpec((None, MIXER_TILE, RET_WIDTH), lambda b, i: (b, i, 0)),
            pl.BlockSpec((N_META, SSM_WIDTH), lambda b, i: (0, 0)),
        ),
        out_shape=(jax.ShapeDtypeStruct((bsz, seq, SSM_WIDTH), F32),
                   jax.ShapeDtypeStruct((bsz, seq, RET_WIDTH), BF16),
                   jax.ShapeDtypeStruct((N_META, SSM_WIDTH), F32)),
        scratch_shapes=[pltpu.VMEM((D_MODEL, IN_WIDTH), BF16),
                        pltpu.VMEM((HEAD_PAIRS, PAIR_DIM, PAIR_DIM), F32),
                        pltpu.VMEM((HEAD_PAIRS, PAIR_DIM, PAIR_DIM), F32)],
        compiler_params=pltpu.CompilerParams(
            dimension_semantics=("arbitrary", "arbitrary"), vmem_limit_bytes=VMEM_LIMIT),
        name="mixer_call",
    )(x, g, w_in, cos, sin, mask, qdec, kdec, bdec, blk, meta, cos_m, sin_m, kdec_m)


def _dot_t(a, b):
    return lax.dot_general(a, b, (((1,), (1,)), ((), ())), preferred_element_type=F32)


def _chunk_transpose(arrs):
    n = len(arrs)
    chunk = lax.broadcasted_iota(I32, (1, LANES), 1) // SSM_GROUP
    arrs = list(arrs)
    s = n // 2
    while s:
        keep = (chunk & s) == 0
        nxt = list(arrs)
        for i in range(n):
            if i & s == 0:
                lo, hi = arrs[i], arrs[i + s]
                nxt[i] = jnp.where(keep, lo, pltpu.roll(hi, s * SSM_GROUP, axis=1))
                nxt[i + s] = jnp.where(keep, pltpu.roll(lo, LANES - s * SSM_GROUP, axis=1), hi)
        arrs = nxt
        s //= 2
    return arrs


def _s5_kernel(u_lo_ref, u_hi_ref, um_ref, krow_ref, bmat_ref, cre_ref, cim_ref, ar_ref, ai_ref, wglu_ref,
               y_lo_ref, y_hi_ref, t0_ref, ug_ref, yg_ref):
    n_blocks = u_lo_ref.shape[0] // S5_BLOCK

    @pl.when(pl.program_id(0) == 0)
    def _():
        lane = lax.broadcasted_iota(I32, (SSM_GROUP, S5_LANES), 1)
        for g in range(SSM_GROUPS):
            k0 = krow_ref[g]
            for a in range(S5_BLOCK):
                blk = k0 if a == 0 else jnp.where(lane >= a * SSM_GROUP,
                                                  pltpu.roll(k0, a * SSM_GROUP, axis=1), 0.0)
                t0_ref[g, a * SSM_GROUP:(a + 1) * SSM_GROUP, :] = blk.astype(BF16)

    per_tile = LANES // SSM_GROUP
    for half, uh_ref in enumerate((u_lo_ref, u_hi_ref)):
        for t in range(S5_BLOCK // per_tile):
            words = [pltpu.bitcast(uh_ref[pl.ds(t * per_tile + k, n_blocks, stride=S5_BLOCK), :].astype(BF16),
                                   jnp.uint32) for k in range(per_tile)]
            for m, w in enumerate(_chunk_transpose(words)):
                ug_ref[half * per_tile + m, :, t * LANES:(t + 1) * LANES] = pltpu.bitcast(w, BF16)

    re, im, re0, im0 = [], [], [], []
    for p in range(SSM_GROUPS // 2):
        g0, g1 = 2 * p, 2 * p + 1
        v = _dot(ug_ref[g0], bmat_ref[g0]) + _dot(ug_ref[g1], bmat_ref[g1])
        v0 = (_dot(um_ref[g0].astype(BF16), bmat_ref[g0]) + _dot(um_ref[g1].astype(BF16), bmat_ref[g1]))[0:1]
        re.append(v[:, :LANES])
        im.append(v[:, LANES:])
        re0.append(v0[:, :LANES])
        im0.append(v0[:, LANES:])
    re, im, re0, im0 = (jnp.concatenate(parts, axis=1) for parts in (re, im, re0, im0))

    row = lax.broadcasted_iota(I32, re.shape, 0)
    ar, ai = ar_ref[0:1], ai_ref[0:1]
    re = re + jnp.where(row == 0, ar * re0 - ai * im0, 0.0)
    im = im + jnp.where(row == 0, ar * im0 + ai * re0, 0.0)
    d = 1
    while d < SCAN_ROWS:
        ar, ai = ar_ref[d - 1:d], ai_ref[d - 1:d]
        inside = row % SCAN_ROWS >= d
        sre = jnp.where(inside, pltpu.roll(re, d, axis=0), 0.0)
        sim = jnp.where(inside, pltpu.roll(im, d, axis=0), 0.0)
        re, im = re + ar * sre - ai * sim, im + ar * sim + ai * sre
        d *= 2
    ar, ai = ar_ref[...], ai_ref[...]
    re_tiles, im_tiles = [re[:SCAN_ROWS]], [im[:SCAN_ROWS]]
    for r in range(1, n_blocks // SCAN_ROWS):
        cre_ = re_tiles[-1][SCAN_ROWS - 1:SCAN_ROWS]
        cim_ = im_tiles[-1][SCAN_ROWS - 1:SCAN_ROWS]
        rs = slice(r * SCAN_ROWS, (r + 1) * SCAN_ROWS)
        re_tiles.append(re[rs] + ar * cre_ - ai * cim_)
        im_tiles.append(im[rs] + ar * cim_ + ai * cre_)
    re = jnp.concatenate(re_tiles, axis=0)
    im = jnp.concatenate(im_tiles, axis=0)
    pre = jnp.where(row == 0, re0, pltpu.roll(re, 1, axis=0)).astype(BF16)
    pim = jnp.where(row == 0, im0, pltpu.roll(im, 1, axis=0)).astype(BF16)

    for g in range(SSM_GROUPS):
        ps = slice((g // 2) * LANES, (g // 2 + 1) * LANES)
        yg_ref[g] = (_dot(ug_ref[g], t0_ref[g]) + _dot_t(pre[:, ps], cre_ref[g])
                     + _dot_t(pim[:, ps], cim_ref[g])).astype(BF16)

    wglu = wglu_ref[...]
    for t in range(S5_BLOCK // per_tile):
        ts = slice(t * LANES, (t + 1) * LANES)
        halves = [_chunk_transpose([pltpu.bitcast(yg_ref[half * per_tile + m, :, ts], jnp.uint32)
                                    for m in range(per_tile)])
                  for half in range(SSM_GROUPS // per_tile)]
        for k in range(per_tile):
            y = jnp.concatenate([pltpu.bitcast(h[k], BF16) for h in halves], axis=1).astype(F32)
            y = jax.nn.gelu(y, approximate=True)
            y = y * _sigmoid(_dot(y.astype(BF16), wglu))
            i = t * per_tile + k
            y_lo_ref[pl.ds(i, n_blocks, stride=S5_BLOCK), :] = y[:, :LANES]
            y_hi_ref[pl.ds(i, n_blocks, stride=S5_BLOCK), :] = y[:, LANES:]


def _s5_call(u, um, krow, bmat, cre, cim, ar, ai, wglu):
    bsz, seq, _ = u.shape
    n_blocks = seq // S5_BLOCK
    const = lambda a: pl.BlockSpec(a.shape, lambda b: (0,) * a.ndim)
    return pl.pallas_call(
        _s5_kernel,
        grid=(bsz,),
        in_specs=[pl.BlockSpec((None, seq, LANES), lambda b: (b, 0, 0)),
                  pl.BlockSpec((None, seq, LANES), lambda b: (b, 0, 1)),
                  const(um), const(krow), const(bmat), const(cre), const(cim), const(ar), const(ai),
                  const(wglu)],
        out_specs=(pl.BlockSpec((None, seq, LANES), lambda b: (b, 0, 0)),
                   pl.BlockSpec((None, seq, LANES), lambda b: (b, 0, 0))),
        out_shape=(jax.ShapeDtypeStruct((bsz, seq, LANES), F32),
                   jax.ShapeDtypeStruct((bsz, seq, LANES), F32)),
        scratch_shapes=[pltpu.VMEM((SSM_GROUPS, S5_LANES, S5_LANES), BF16),
                        pltpu.VMEM((SSM_GROUPS, n_blocks, S5_LANES), BF16),
                        pltpu.VMEM((SSM_GROUPS, n_blocks, S5_LANES), BF16)],
        compiler_params=pltpu.CompilerParams(
            dimension_semantics=("arbitrary",), vmem_limit_bytes=VMEM_LIMIT),
        name="s5_call",
    )(u, u, um, krow, bmat, cre, cim, ar, ai, wglu)


def _s5_operators(lam_re, lam_im, log_dt, b_re, b_im, c_re, c_im, d_skip):
    n_groups = lam_re.shape[0]
    lam = lax.complex(lam_re, lam_im)
    lam_dt = lam * jnp.exp(log_dt)[:, None]
    lam_bar = jnp.exp(lam_dt)
    b_bar = ((lam_bar - 1.0) / lam)[..., None] * lax.complex(b_re, b_im)
    c = lax.complex(c_re, c_im)
    tau = jnp.arange(S5_BLOCK + 1, dtype=F32)
    pows = jnp.exp(lam_dt[None] * tau[:, None, None])
    kern = jnp.real(jnp.einsum('ghp,tgp,gpk->gkth', c, pows[:S5_BLOCK], b_bar))
    skip = (jnp.eye(SSM_GROUP, dtype=F32)[None, :, None, :] * d_skip[:, None, None, :]
            * (tau[:S5_BLOCK] == 0).astype(F32)[None, None, :, None])
    krow = (kern + skip).reshape(n_groups, SSM_GROUP, S5_LANES)
    even = (jnp.arange(n_groups) % 2 == 0)[:, None, None]
    pair_pad = lambda m: jnp.concatenate([jnp.where(even, m, 0.0), jnp.where(even, 0.0, m)], axis=-1)
    bm = (pows[S5_BLOCK - 1 - jnp.arange(S5_BLOCK)].transpose(1, 0, 2)[:, :, None, :]
          * b_bar.transpose(0, 2, 1)[:, None, :, :]).reshape(n_groups, S5_LANES, SSM_STATE)
    bmat = jnp.concatenate([pair_pad(jnp.real(bm)), pair_pad(jnp.imag(bm))], axis=-1)
    cm = (pows[1:].transpose(1, 0, 2)[:, :, None, :] * c[:, None, :, :]).reshape(n_groups, S5_LANES, SSM_STATE)
    cre, cim = pair_pad(jnp.real(cm)), pair_pad(-jnp.imag(cm))
    step = S5_BLOCK * (1.0 + jnp.arange(SCAN_ROWS, dtype=F32))
    adec = jnp.exp(lam_dt[None, :, :] * step[:, None, None]).reshape(SCAN_ROWS, n_groups * SSM_STATE)
    return krow, bmat.astype(BF16), cre.astype(BF16), cim.astype(BF16), jnp.real(adec), jnp.imag(adec)


def _first_hit(values, target):
    hits, taken = [], None
    for v in values:
        hit = (v >= target) if taken is None else jnp.logical_and(v >= target, jnp.logical_not(taken))
        taken = hit if taken is None else jnp.logical_or(taken, hit)
        hits.append(hit)
    return hits


def _stack_rows(rows, n_rows):
    idx = lax.broadcasted_iota(I32, (n_rows, rows[0].shape[1]), 0)
    out = jnp.zeros((n_rows, rows[0].shape[1]), F32)
    for k, r in enumerate(rows):
        out = jnp.where(idx == k, r, out)
    return out


def _project(rs, x_ref, ys_lo_ref, ys_hi_ref, yr_ref, wout_ref, h_ref):
    ys = jnp.concatenate([ys_lo_ref[rs, :], ys_hi_ref[rs, :]], axis=1).astype(BF16)
    h = x_ref[rs, :] + _dot(ys, wout_ref[:SSM_WIDTH, :]) + _dot(yr_ref[rs, :], wout_ref[SSM_WIDTH:, :])
    h_ref[rs, :] = h.astype(BF16)
    return h


def _route(lt, tri_ref):
    gl = [lt[g:g + 1, :] for g in range(N_GROUPS)]
    gmax = functools.reduce(jnp.maximum, gl)
    g_w = 1.0 / functools.reduce(lambda a, b: a + b, [jnp.exp(l - gmax) for l in gl])
    sel = _first_hit(gl, gmax)
    ev = []
    for e in range(EXPERTS_PER_GROUP):
        acc = jnp.zeros_like(gmax)
        for g in range(N_GROUPS):
            k = N_GROUPS + g * EXPERTS_PER_GROUP + e
            acc = jnp.where(sel[g], lt[k:k + 1, :], acc)
        ev.append(acc)
    m1 = functools.reduce(jnp.maximum, ev)
    first = _first_hit(ev, m1)
    rest = [jnp.where(f, -jnp.inf, v) for f, v in zip(first, ev)]
    m2 = functools.reduce(jnp.maximum, rest)
    second = _first_hit(rest, m2)
    e2 = jnp.exp(m2 - m1)
    w1 = g_w / (1.0 + e2)
    w2 = e2 * w1
    combine = [jnp.where(f, w1, 0.0) + jnp.where(s, w2, 0.0) for f, s in zip(first, second)]

    sel_f = [jnp.where(s, 1.0, 0.0) for s in sel]
    incl = _dot(_stack_rows(sel_f, 8).astype(BF16), tri_ref[...])
    dest = jnp.zeros_like(gmax)
    seg_start = jnp.zeros((1, 1), F32)
    counts = []
    for g in range(N_GROUPS):
        run = incl[g:g + 1, :]
        cnt = run[:, TILE - 1:TILE]
        counts.append(cnt)
        dest = dest + sel_f[g] * (seg_start + run - 1.0)
        seg_start = seg_start + PIECE * jnp.floor((cnt + (PIECE - 1.0)) * (1.0 / PIECE))
    return combine, dest, counts


def _proj_kernel(x_ref, ys_lo_ref, ys_hi_ref, yr_ref, wout32_ref, g_ref, wr_ref, br_ref,
                 tri_ref, h_ref, stage_ref, dest_ref, cnt_ref, wout_ref):
    @pl.when(pl.program_id(0) == 0)
    def _():
        wout_ref[...] = wout32_ref[...].astype(BF16)

    tiles = range(PROJ_TILES)
    proj_refs = (x_ref, ys_lo_ref, ys_hi_ref, yr_ref, wout_ref, h_ref)
    chunks = [[slice(tile * TILE + b * ROW_CHUNK, tile * TILE + (b + 1) * ROW_CHUNK)
               for b in range(TILE // ROW_CHUNK)] for tile in tiles]
    h_parts = [[_project(rs, *proj_refs) for rs in chunks[tile]] for tile in tiles]

    logits, staged = [], []
    for tile in tiles:
        parts, rows, inv = [], [], []
        for h in h_parts[tile]:
            t = _rms_norm(h, g_ref[...])
            parts.append(t.astype(BF16))
            peak = jnp.maximum(jnp.max(jnp.abs(t), axis=1, keepdims=True), 1e-30)
            rows.append((t * (FP8_HEADROOM / peak)).astype(BF16))
            inv.append(peak * (1.0 / FP8_HEADROOM))
        staged.append((jnp.concatenate(rows, axis=0), jnp.concatenate(inv, axis=0)))
        lt_parts = []
        for t_hi in parts:
            both = _dot_t(wr_ref[...], t_hi)
            lt_parts.append(both[:ROUTE_ROWS] + both[ROUTE_ROWS:])
        logits.append(jnp.concatenate(lt_parts, axis=1) + br_ref[...])

    routed = [_route(logits[tile], tri_ref) for tile in tiles]

    for tile in tiles:
        combine, dest, counts = routed[tile]
        dest_ref[tile] = _stack_rows([dest], 8)
        cnt_ref[tile] = _stack_rows([c + jnp.zeros((1, LANES), F32) for c in counts], 8)
        perm = jnp.where(lax.broadcasted_iota(I32, (SORTED_ROWS, TILE), 0) == dest.astype(I32),
                         1.0, 0.0).astype(BF16)
        c_hi = [c.astype(BF16).astype(F32) for c in combine]
        c_lo = [c - hi for c, hi in zip(combine, c_hi)]
        cw = _stack_rows(c_hi + c_lo, LANES).T
        rows, inv = staged[tile]
        inv_hi = inv.astype(BF16).astype(F32)
        lane = lax.broadcasted_iota(I32, (1, LANES), 1)
        cw = jnp.where(lane == INV_LANE, inv_hi, jnp.where(lane == INV_LANE + 1, inv - inv_hi, cw))
        t_ext = jnp.concatenate([rows, cw.astype(BF16)], axis=1)
        stage_ref[tile * SORTED_ROWS:(tile + 1) * SORTED_ROWS, :] = _dot(perm, t_ext).astype(BF16)


def _proj_call(x, ys_lo, ys_hi, yr, wout, g, wr, br, tri):
    n_tok = x.shape[0]
    n_tiles = n_tok // TILE
    const = lambda *shape: pl.BlockSpec(shape, lambda i: (0,) * len(shape))
    rows = lambda width: pl.BlockSpec((PROJ_TILES * TILE, width), lambda i: (i, 0))
    return pl.pallas_call(
        _proj_kernel,
        grid=(n_tiles // PROJ_TILES,),
        in_specs=[
            rows(D_MODEL), rows(LANES), rows(LANES), rows(RET_WIDTH),
            pl.BlockSpec((D_MODEL, D_MODEL), lambda i: (0, 0), pipeline_mode=pl.Buffered(1)), const(1, D_MODEL),
            const(2 * ROUTE_ROWS, D_MODEL), const(ROUTE_ROWS, 1), const(TILE, TILE),
        ],
        out_specs=(rows(D_MODEL),
                   pl.BlockSpec((PROJ_TILES * SORTED_ROWS, EXT_WIDTH), lambda i: (i, 0)),
                   pl.BlockSpec((PROJ_TILES, 8, TILE), lambda i: (i, 0, 0)),
                   pl.BlockSpec((PROJ_TILES, 8, LANES), lambda i: (i, 0, 0))),
        scratch_shapes=[pltpu.VMEM((D_MODEL, D_MODEL), BF16)],
        out_shape=(jax.ShapeDtypeStruct((n_tok, D_MODEL), BF16),
                   jax.ShapeDtypeStruct((n_tiles * SORTED_ROWS, EXT_WIDTH), BF16),
                   jax.ShapeDtypeStruct((n_tiles, 8, TILE), F32),
                   jax.ShapeDtypeStruct((n_tiles, 8, LANES), F32)),
        compiler_params=pltpu.CompilerParams(
            dimension_semantics=("arbitrary",), vmem_limit_bytes=VMEM_LIMIT),
        name="proj_call",
    )(x, ys_lo, ys_hi, yr, wout, g, wr, br, tri)


def _sort_tables(cnt, n_steps):
    n_tiles = cnt.shape[0]
    npc = (cnt + PIECE - 1) // PIECE
    seg = jnp.cumsum(npc, axis=1) - npc
    before = jnp.cumsum(npc, axis=0) - npc
    n_tile_g = (jnp.sum(npc, axis=0) + MOE_PIECES - 1) // MOE_PIECES
    t_off = jnp.cumsum(n_tile_g) - n_tile_g
    j = jnp.arange(SORTED_PIECES, dtype=I32)[None, :, None]
    in_g = jnp.logical_and(j >= seg[:, None, :], j < (seg + npc)[:, None, :])
    pos = jnp.sum(jnp.where(in_g, MOE_PIECES * t_off[None, None, :] + before[:, None, :] + j - seg[:, None, :], 0),
                  axis=-1)
    valid = jnp.any(in_g, axis=-1)
    steps = jnp.arange(n_steps, dtype=I32)
    g_step = jnp.minimum(jnp.sum(steps[:, None] >= (t_off + n_tile_g)[None, :], axis=1), N_GROUPS - 1)
    slot = jnp.arange(n_steps * MOE_PIECES, dtype=I32)
    g_slot = jnp.repeat(g_step, MOE_PIECES)[:, None]
    p = slot[:, None] - MOE_PIECES * t_off[None, :]
    end = before + npc
    off = seg - before + SORTED_PIECES * jnp.arange(n_tiles, dtype=I32)[:, None]
    passed = end[None, :-1, :] <= p[:, None, :]
    piece = p + off[0][None, :] + jnp.sum(jnp.where(passed, (off[1:] - off[:-1])[None], 0), axis=1)
    live = jnp.logical_and(jnp.arange(N_GROUPS, dtype=I32)[None, :] == g_slot,
                           jnp.logical_and(p >= 0, p < end[-1][None, :]))
    src = jnp.sum(jnp.where(live, piece, 0), axis=1).astype(I32)
    n_live = jnp.sum(n_tile_g).reshape(1)
    back = jnp.where(valid, pos, 0).reshape(-1)
    return src, g_step.astype(I32), n_live.astype(I32), back.astype(I32)


def _piece_copy(src_ref, piece, buf_ref, slot, j, sem_ref):
    start = piece * PIECE if isinstance(piece, int) else pl.multiple_of(piece * PIECE, PIECE)
    return pltpu.make_async_copy(src_ref.at[pl.ds(start, PIECE)],
                                 buf_ref.at[slot, pl.ds(j * PIECE, PIECE)], sem_ref.at[slot])


def _fetch_pieces(table_ref, src_ref, buf_ref, sem_ref, row, slot, n_pieces):
    for j in range(n_pieces):
        _piece_copy(src_ref, table_ref[row * n_pieces + j], buf_ref, slot, j, sem_ref).start()


def _wait_pieces(src_ref, buf_ref, sem_ref, slot, n_pieces):
    for j in range(n_pieces):
        _piece_copy(src_ref, 0, buf_ref, slot, j, sem_ref).wait()


def _gather_pieces(table_ref, src_ref, buf_ref, sem_ref, n_pieces, n_live=None):
    step = pl.program_id(0)
    last = pl.num_programs(0) - 1 if n_live is None else n_live - 1
    args = (src_ref, buf_ref, sem_ref)

    @pl.when(step == 0)
    def _():
        _fetch_pieces(table_ref, *args, step, 0, n_pieces)

    if n_live is None:
        _wait_pieces(*args, step % 2, n_pieces)
    else:
        @pl.when(step <= last)
        def _():
            _wait_pieces(*args, step % 2, n_pieces)

    def fetch_next():
        _fetch_pieces(table_ref, *args, jnp.minimum(step + 1, last), (step + 1) % 2, n_pieces)

    def drain():
        @pl.when(step == last)
        def _():
            _wait_pieces(*args, (step + 1) % 2, n_pieces)

    return fetch_next, drain


def _to_fp8(x, headroom=FP8_HEADROOM):
    peak = jnp.maximum(jnp.max(jnp.abs(x), axis=(0, 1), keepdims=True), 1e-30)
    return (x * (headroom / peak)).astype(FP8), peak * (1.0 / headroom)


def _moe_kernel(src_ref, gstep_ref, nlive_ref, stage_ref, wg_ref, wu_ref, wd_ref, y_ref,
                buf_ref, sem_ref, wgb_ref, wub_ref, wdb_ref, inv_ref):
    step = pl.program_id(0)
    fetch_next, drain = _gather_pieces(src_ref, stage_ref, buf_ref, sem_ref, MOE_PIECES, nlive_ref[0])

    @pl.when(jnp.logical_or(step == 0, gstep_ref[step] != gstep_ref[jnp.maximum(step - 1, 0)]))
    def _():
        ones = jnp.ones((1, LANES), F32)
        for e in range(EXPERTS_PER_GROUP):
            wgb_ref[e], inv = _to_fp8(wg_ref[e])
            inv_ref[e:e + 1, :] = inv * ones
            wub_ref[e], inv = _to_fp8(wu_ref[e])
            inv_ref[EXPERTS_PER_GROUP + e:EXPERTS_PER_GROUP + e + 1, :] = inv * ones
        peak = functools.reduce(jnp.maximum, [jnp.max(jnp.abs(wd_ref[e]), axis=(0, 1), keepdims=True)
                                              for e in range(EXPERTS_PER_GROUP)])
        peak = jnp.maximum(peak, 1e-30)
        for e in range(EXPERTS_PER_GROUP):
            wdb_ref[e * EXPERT_FF:(e + 1) * EXPERT_FF, :] = (wd_ref[e] * (FP8_HEADROOM / peak)).astype(FP8)
        inv_ref[2 * EXPERTS_PER_GROUP:2 * EXPERTS_PER_GROUP + 1, :] = peak * (1.0 / FP8_HEADROOM) * ones

    @pl.when(step < nlive_ref[0])
    def _():
        slot = step % 2
        fetch_next()

        def up(rs):
            t = buf_ref[slot, rs, :D_MODEL].astype(FP8)
            t_inv = (buf_ref[slot, rs, D_MODEL + INV_LANE:D_MODEL + INV_LANE + 1].astype(F32)
                     + buf_ref[slot, rs, D_MODEL + INV_LANE + 1:D_MODEL + INV_LANE + 2].astype(F32))
            out = []
            for e in range(EXPERTS_PER_GROUP):
                g_inv = inv_ref[e:e + 1, 0:1] * t_inv
                u_inv = inv_ref[EXPERTS_PER_GROUP + e:EXPERTS_PER_GROUP + e + 1, 0:1] * t_inv
                out.append((_dot(t, wgb_ref[e]) * g_inv, _dot(t, wub_ref[e]), u_inv))
            return out

        def down(rs, hidden):
            cw = buf_ref[slot, rs, D_MODEL:].astype(F32)
            acts = []
            for e, (hg, hu, u_inv) in enumerate(hidden):
                c = (cw[:, e:e + 1] + cw[:, EXPERTS_PER_GROUP + e:EXPERTS_PER_GROUP + e + 1]) * u_inv
                acts.append(hg * _sigmoid(hg) * hu * c)
            act, a_inv = _to_fp8(jnp.concatenate(acts, axis=1))
            d_inv = inv_ref[2 * EXPERTS_PER_GROUP:2 * EXPERTS_PER_GROUP + 1, 0:1] * a_inv
            y_ref[rs, :] = (_dot(act, wdb_ref[...]) * d_inv).astype(BF16)

        chunks = [slice(b * ROW_CHUNK, (b + 1) * ROW_CHUNK) for b in range(MOE_TILE // ROW_CHUNK)]
        hidden = [up(rs) for rs in chunks]
        for rs, hid in zip(chunks, hidden):
            down(rs, hid)

    @pl.when(step >= nlive_ref[0])
    def _():
        y_ref[...] = jnp.zeros_like(y_ref)

    drain()


def _moe_call(src, g_step, n_live, stage, wg, wu, wd, n_steps):
    grp = lambda shape: pl.BlockSpec((EXPERTS_PER_GROUP,) + shape, lambda s, src, gs, nl: (gs[s], 0, 0))
    return pl.pallas_call(
        _moe_kernel,
        grid_spec=pltpu.PrefetchScalarGridSpec(
            num_scalar_prefetch=3,
            grid=(n_steps,),
            in_specs=[pl.BlockSpec(memory_space=pl.ANY),
                      grp((D_MODEL, EXPERT_FF)), grp((D_MODEL, EXPERT_FF)), grp((EXPERT_FF, D_MODEL))],
            out_specs=pl.BlockSpec((MOE_TILE, D_MODEL), lambda s, src, gs, nl: (s, 0)),
            scratch_shapes=[pltpu.VMEM((2, MOE_TILE, EXT_WIDTH), BF16), pltpu.SemaphoreType.DMA((2,)),
                            pltpu.VMEM((EXPERTS_PER_GROUP, D_MODEL, EXPERT_FF), FP8),
                            pltpu.VMEM((EXPERTS_PER_GROUP, D_MODEL, EXPERT_FF), FP8),
                            pltpu.VMEM((EXPERTS_PER_GROUP * EXPERT_FF, D_MODEL), FP8),
                            pltpu.VMEM((16, LANES), F32)],
        ),
        out_shape=jax.ShapeDtypeStruct((n_steps * MOE_TILE, D_MODEL), BF16),
        compiler_params=pltpu.CompilerParams(
            dimension_semantics=("arbitrary",), vmem_limit_bytes=VMEM_LIMIT),
        name="moe_call",
    )(src, g_step, n_live, stage, wg, wu, wd)


def _final_kernel(back_ref, ysort_ref, h_ref, dest_ref, g_ref, o_ref, buf_ref, sem_ref):
    step = pl.program_id(0)
    fetch_next, drain = _gather_pieces(back_ref, ysort_ref, buf_ref, sem_ref, FINAL_TILES * SORTED_PIECES)
    fetch_next()
    for t in range(FINAL_TILES):
        dest = _stack_rows([dest_ref[t, 0:1, :]], LANES).T[:, 0:1].astype(I32)
        unperm = jnp.where(lax.broadcasted_iota(I32, (TILE, SORTED_ROWS), 1) == dest, 1.0, 0.0).astype(BF16)
        rows = buf_ref[step % 2, t * SORTED_ROWS:(t + 1) * SORTED_ROWS, :]
        for b in range(TILE // ROW_CHUNK):
            rs = slice(b * ROW_CHUNK, (b + 1) * ROW_CHUNK)
            os = slice(t * TILE + b * ROW_CHUNK, t * TILE + (b + 1) * ROW_CHUNK)
            o_ref[os, :] = _rms_norm(h_ref[os, :].astype(F32) + _dot(unperm[rs], rows), g_ref[...])
    drain()


def _final_call(back, ysort, h, dest, g):
    n_tok = h.shape[0]
    return pl.pallas_call(
        _final_kernel,
        grid_spec=pltpu.PrefetchScalarGridSpec(
            num_scalar_prefetch=1,
            grid=(n_tok // (FINAL_TILES * TILE),),
            in_specs=[pl.BlockSpec(memory_space=pl.ANY),
                      pl.BlockSpec((FINAL_TILES * TILE, D_MODEL), lambda i, back: (i, 0)),
                      pl.BlockSpec((FINAL_TILES, 8, TILE), lambda i, back: (i, 0, 0)),
                      pl.BlockSpec((1, D_MODEL), lambda i, back: (0, 0))],
            out_specs=pl.BlockSpec((FINAL_TILES * TILE, D_MODEL), lambda i, back: (i, 0)),
            scratch_shapes=[pltpu.VMEM((2, FINAL_TILES * SORTED_ROWS, D_MODEL), BF16),
                            pltpu.SemaphoreType.DMA((2,))],
        ),
        out_shape=jax.ShapeDtypeStruct((n_tok, D_MODEL), F32),
        compiler_params=pltpu.CompilerParams(
            dimension_semantics=("arbitrary",), vmem_limit_bytes=VMEM_LIMIT),
        name="final_call",
    )(back, ysort, h, dest, g)


def _rope_tables(length):
    pos = np.arange(length, dtype=np.float32)
    inv_freq = np.float32(ROPE_BASE) ** (-np.arange(0, HEAD_DIM, 2, dtype=np.float32) / np.float32(HEAD_DIM))
    ang = pos[:, None] * inv_freq[None, :]
    cos, sin = np.cos(ang), np.sin(ang)
    return np.concatenate([cos, cos], axis=-1), np.concatenate([-sin, sin], axis=-1)


def _retention_tables():
    f32 = np.float32
    gamma = f32(1.0) - f32(2.0) ** (f32(-5.0) - np.arange(HEADS, dtype=f32))
    log_g = np.log(gamma)[:, None, None]
    scale = f32(HEAD_DIM ** -0.5)
    idx = np.arange(RET_BLOCK)
    dist = np.abs(idx[:, None] - idx[None, :]).astype(f32)
    visible = (idx[None, :] // CHUNK) <= (idx[:, None] // CHUNK)
    mask = np.where(visible[None], np.exp(log_g * dist[None]), f32(0.0)) * scale
    ones = np.ones((1, 1, HEAD_DIM), f32)
    idx_f = idx.astype(f32)[None, :, None]
    pair = lambda a: a.reshape(HEAD_PAIRS, 2, a.shape[1], HEAD_DIM).transpose(0, 2, 1, 3).reshape(
        HEAD_PAIRS, a.shape[1], PAIR_DIM)
    qdec = pair(np.exp(log_g * (idx_f + f32(1.0))) * ones)
    kdec = pair(np.exp(log_g * (f32(RET_BLOCK - 1.0) - idx_f)) * scale * ones)
    meta_idx = np.arange(N_META, dtype=f32)[None, :, None]
    kdec_meta = pair(np.exp(log_g * (f32(N_META - 1.0) - meta_idx)) * scale * ones)
    blk = np.kron(np.eye(2, dtype=f32), np.ones((HEAD_DIM, HEAD_DIM), f32))
    bdec = np.exp(log_g * f32(RET_BLOCK)).reshape(HEAD_PAIRS, 2)
    bdec = np.stack([np.kron(np.diag(b), np.ones((HEAD_DIM, HEAD_DIM), f32)) for b in bdec])
    return tuple(a.astype(f32) for a in (mask, qdec, kdec, bdec, blk, kdec_meta))


def kernel(x, meta_tokens, norm_mix_g, w_in, ssm_lambda_re, ssm_lambda_im, ssm_log_dt, ssm_b_re, ssm_b_im, ssm_c_re, ssm_c_im, ssm_d, w_glu, w_out, norm_ffn_g, w_router_group, b_router_group, w_router_expert, b_router_expert, w_gate, w_up, w_down, norm_final_g):
    bsz, seq, _ = x.shape
    assert seq % TILE == 0 and seq % MIXER_TILE == 0 and MIXER_TILE % RET_BLOCK == 0 and RET_BLOCK % CHUNK == 0
    n_blocks = seq // S5_BLOCK
    assert n_blocks % SCAN_ROWS == 0
    n_tok = bsz * seq
    assert n_tok % (FINAL_TILES * TILE) == 0 and n_tok % (PROJ_TILES * TILE) == 0
    n_tiles = n_tok // TILE
    n_steps = -(-n_tiles * (TILE_PIECES + N_GROUPS - 1) // MOE_PIECES) + N_GROUPS

    cos, sin = _rope_tables(N_META + seq)
    mask, qdec, kdec, bdec, blk, kdec_meta = _retention_tables()
    g_mix = norm_mix_g[0][None, :]
    u, y_ret, u_meta = _mixer_call(x, g_mix, w_in[0], cos[N_META:], sin[N_META:], mask, qdec, kdec, bdec, blk,
                                   meta_tokens, cos[:N_META], sin[:N_META], kdec_meta)

    s5_ops = _s5_operators(
        ssm_lambda_re[0], ssm_lambda_im[0], ssm_log_dt[0], ssm_b_re[0], ssm_b_im[0],
        ssm_c_re[0], ssm_c_im[0], ssm_d[0])
    um = u_meta.reshape(S5_BLOCK, SSM_GROUPS, SSM_GROUP).transpose(1, 0, 2).reshape(SSM_GROUPS, 1, S5_LANES)
    um = jnp.pad(um, ((0, 0), (0, 7), (0, 0)))
    y_lo, y_hi = _s5_call(u, um, *s5_ops, w_glu[0].astype(BF16))

    w_r = jnp.concatenate(
        [w_router_group[0].T, w_router_expert[0].transpose(0, 2, 1).reshape(N_EXPERTS, D_MODEL)], axis=0)
    w_r = jnp.pad(w_r, ((0, ROUTE_ROWS - w_r.shape[0]), (0, 0)))
    b_r = jnp.concatenate([b_router_group[0], b_router_expert[0].reshape(-1)])
    b_r = jnp.pad(b_r, (0, ROUTE_ROWS - b_r.shape[0]))[:, None]
    w_r_hi = w_r.astype(BF16)
    w_r = jnp.concatenate([w_r_hi, (w_r - w_r_hi.astype(F32)).astype(BF16)], axis=0)
    tri = jnp.asarray(np.arange(TILE)[:, None] <= np.arange(TILE)[None, :], BF16)

    h, stage, dest, cnt = _proj_call(
        x.reshape(n_tok, D_MODEL), y_lo.reshape(n_tok, LANES), y_hi.reshape(n_tok, LANES),
        y_ret.reshape(n_tok, RET_WIDTH),
        w_out[0], norm_ffn_g[0][None, :], w_r, b_r, tri)
    src, g_step, n_live, back = _sort_tables(cnt[:, :N_GROUPS, 0].astype(I32), n_steps)
    y_sorted = _moe_call(src, g_step, n_live, stage, w_gate[0], w_up[0], w_down[0], n_steps)
    out = _final_call(back, y_sorted, h, dest, norm_final_g[None, :])
    return out.reshape(bsz, seq, D_MODEL)
```

```python
import functools

import jax
import jax.numpy as jnp
import numpy as np
from jax import lax
from jax.experimental import pallas as pl
from jax.experimental.pallas import tpu as pltpu

D_MODEL = 1024
N_META = 16
CHUNK = 64
EPS = 1e-6
SSM_WIDTH = 256
SSM_GROUP = 16
SSM_GROUPS = 16
SSM_STATE = 64
RET_WIDTH = 768
HEAD_DIM = 128
HEADS = 6
HEAD_PAIRS = HEADS // 2
PAIR_DIM = 2 * HEAD_DIM
ROPE_BASE = 10000.0
IN_WIDTH = SSM_WIDTH + 4 * RET_WIDTH
N_GROUPS = 4
EXPERTS_PER_GROUP = 4
N_EXPERTS = 16
EXPERT_FF = 256

S5_BLOCK = 16
S5_LANES = S5_BLOCK * SSM_GROUP
SCAN_ROWS = 8
RET_BLOCK = 256
TILE = 512
MIXER_TILE = 1024
LANES = 128
ROUTE_ROWS = 32
ROW_CHUNK = 256
PROJ_TILES = 2
FINAL_TILES = 4
PIECE = 16
TILE_PIECES = TILE // PIECE
MOE_TILE = 1024
MOE_PIECES = MOE_TILE // PIECE
SORTED_PIECES = TILE_PIECES + N_GROUPS
SORTED_ROWS = SORTED_PIECES * PIECE
EXT_WIDTH = D_MODEL + LANES
INV_LANE = 2 * EXPERTS_PER_GROUP
V7X_VMEM_BYTES = 64 * 1024 * 1024
VMEM_LIMIT = V7X_VMEM_BYTES * 7 // 8

F32 = jnp.float32
BF16 = jnp.bfloat16
FP8 = jnp.float8_e4m3fn
FP8_HEADROOM = 256.0
I32 = jnp.int32


def _dot(a, b):
    return jnp.dot(a, b, preferred_element_type=F32)


def _sigmoid(x):
    return 0.5 * jnp.tanh(0.5 * x) + 0.5


def _sigmoid_exp(x):
    return 1.0 / (1.0 + jnp.exp(-x))


def _rms_norm(x, g):
    return x * lax.rsqrt(jnp.mean(x * x, axis=-1, keepdims=True) + EPS) * g


def _rope(t, cos, sin_signed):
    return t * cos + pltpu.roll(t, HEAD_DIM // 2, axis=1) * sin_signed


def _meta_state(meta_ref, g_ref, w_ref, cos_ref, sin_ref, kdec_ref, blk_ref, u_ref, r0_ref):
    a = _rms_norm(meta_ref[...], g_ref[...]).astype(BF16)
    u_ref[...] = _dot(a, w_ref[:, :SSM_WIDTH])
    k_off = SSM_WIDTH + RET_WIDTH
    v_off = SSM_WIDTH + 2 * RET_WIDTH
    cos = cos_ref[...]
    sin = sin_ref[...]
    for p in range(HEAD_PAIRS):
        k = _dot(a, w_ref[:, k_off + p * PAIR_DIM:k_off + (p + 1) * PAIR_DIM])
        v = _dot(a, w_ref[:, v_off + p * PAIR_DIM:v_off + (p + 1) * PAIR_DIM])
        k = jnp.concatenate([_rope(k[:, :HEAD_DIM], cos, sin), _rope(k[:, HEAD_DIM:], cos, sin)], axis=1)
        kd = (k * kdec_ref[p]).astype(BF16)
        r0_ref[p] = _dot_rows(kd, v.astype(BF16)) * blk_ref[...]


def _dot_rows(a, b):
    return lax.dot_general(a, b, (((0,), (0,)), ((), ())), preferred_element_type=F32)


def _mixer_kernel(x_ref, g_ref, w32_ref, cos_ref, sin_ref, mask_ref, qdec_ref, kdec_ref, bdec_ref, blk_ref,
                  meta_ref, cos_m_ref, sin_m_ref, kdec_m_ref, u_ref, y_ref, um_ref, w_ref, r0_ref, r_ref):
    first_tile = pl.program_id(1) == 0

    @pl.when(jnp.logical_and(pl.program_id(0) == 0, first_tile))
    def _():
        for c in range(0, IN_WIDTH, SSM_WIDTH):
            w_ref[:, c:c + SSM_WIDTH] = w32_ref[:, c:c + SSM_WIDTH].astype(BF16)
        _meta_state(meta_ref, g_ref, w_ref, cos_m_ref, sin_m_ref, kdec_m_ref, blk_ref, um_ref, r0_ref)

    @pl.when(first_tile)
    def _():
        r_ref[...] = r0_ref[...]

    off = SSM_WIDTH
    for b in range(MIXER_TILE // RET_BLOCK):
        bs = slice(b * RET_BLOCK, (b + 1) * RET_BLOCK)
        a = _rms_norm(x_ref[bs, :], g_ref[...]).astype(BF16)
        u_ref[bs, :] = _dot(a, w_ref[:, :SSM_WIDTH])
        q_all = _dot(a, w_ref[:, off:off + RET_WIDTH])
        k_all = _dot(a, w_ref[:, off + RET_WIDTH:off + 2 * RET_WIDTH])
        v_all = _dot(a, w_ref[:, off + 2 * RET_WIDTH:off + 3 * RET_WIDTH])
        gate = _dot(a, w_ref[:, off + 3 * RET_WIDTH:off + 4 * RET_WIDTH])
        cos = cos_ref[bs, :]
        sin = sin_ref[bs, :]
        def scores(h):
            hs = slice(h * HEAD_DIM, (h + 1) * HEAD_DIM)
            q = _rope(q_all[:, hs], cos, sin)
            k = _rope(k_all[:, hs], cos, sin)
            return q, k, _dot_t(q.astype(BF16), k.astype(BF16))

        ahead = scores(0)
        for p in range(HEAD_PAIRS):
            ps = slice(p * PAIR_DIM, (p + 1) * PAIR_DIM)
            pair = [ahead, scores(2 * p + 1)]
            if p + 1 < HEAD_PAIRS:
                ahead = scores(2 * p + 2)
            q = jnp.concatenate([pair[0][0], pair[1][0]], axis=1)
            k = jnp.concatenate([pair[0][1], pair[1][1]], axis=1)
            v = v_all[:, ps].astype(BF16)
            state = r_ref[p]
            cross = _dot((q * qdec_ref[p]).astype(BF16), state.astype(BF16))
            kv = _dot_rows((k * kdec_ref[p]).astype(BF16), v)
            r_ref[p] = state * bdec_ref[p] + kv * blk_ref[...]
            for half, (_, _, s) in enumerate(pair):
                h = 2 * p + half
                hs = slice(h * HEAD_DIM, (h + 1) * HEAD_DIM)
                ls = slice(half * HEAD_DIM, (half + 1) * HEAD_DIM)
                o = _dot((s * mask_ref[h]).astype(BF16), v[:, ls]) + cross[:, ls]
                mu = jnp.mean(o, axis=-1, keepdims=True)
                d = o - mu
                var = jnp.mean(d * d, axis=-1, keepdims=True)
                gt = gate[:, hs]
                y_ref[bs, hs] = (gt * _sigmoid_exp(gt) * d * lax.rsqrt(var + EPS)).astype(BF16)


def _mixer_call(x, g, w_in, cos, sin, mask, qdec, kdec, bdec, blk, meta, cos_m, sin_m, kdec_m):
    bsz, seq, _ = x.shape
    const = lambda a, **kw: pl.BlockSpec(a.shape, lambda b, i: (0,) * a.ndim, **kw)
    return pl.pallas_call(
        _mixer_kernel,
        grid=(bsz, seq // MIXER_TILE),
        in_specs=[
            pl.BlockSpec((None, MIXER_TILE, D_MODEL), lambda b, i: (b, i, 0)),
            const(g),
            const(w_in, pipeline_mode=pl.Buffered(1)),
            pl.BlockSpec((MIXER_TILE, HEAD_DIM), lambda b, i: (i, 0)),
            pl.BlockSpec((MIXER_TILE, HEAD_DIM), lambda b, i: (i, 0)),
            const(mask), const(qdec), const(kdec), const(bdec), const(blk),
            const(meta), const(cos_m), const(sin_m), const(kdec_m),
        ],
        out_specs=(
            pl.BlockSpec((None, MIXER_TILE, SSM_WIDTH), lambda b, i: (b, i, 0)),
            pl.BlockSpec((None, MIXER_TILE, RET_WIDTH), lambda b, i: (b, i, 0)),
            pl.BlockSpec((N_META, SSM_WIDTH), lambda b, i: (0, 0)),
        ),
        out_shape=(jax.ShapeDtypeStruct((bsz, seq, SSM_WIDTH), F32),
                   jax.ShapeDtypeStruct((bsz, seq, RET_WIDTH), BF16),
                   jax.ShapeDtypeStruct((N_META, SSM_WIDTH), F32)),
        scratch_shapes=[pltpu.VMEM((D_MODEL, IN_WIDTH), BF16),
                        pltpu.VMEM((HEAD_PAIRS, PAIR_DIM, PAIR_DIM), F32),
                        pltpu.VMEM((HEAD_PAIRS, PAIR_DIM, PAIR_DIM), F32)],
        compiler_params=pltpu.CompilerParams(
            dimension_semantics=("arbitrary", "arbitrary"), vmem_limit_bytes=VMEM_LIMIT),
        name="mixer_call",
    )(x, g, w_in, cos, sin, mask, qdec, kdec, bdec, blk, meta, cos_m, sin_m, kdec_m)


def _dot_t(a, b):
    return lax.dot_general(a, b, (((1,), (1,)), ((), ())), preferred_element_type=F32)


def _chunk_transpose(arrs):
    n = len(arrs)
    chunk = lax.broadcasted_iota(I32, (1, LANES), 1) // SSM_GROUP
    arrs = list(arrs)
    s = n // 2
    while s:
        keep = (chunk & s) == 0
        nxt = list(arrs)
        for i in range(n):
            if i & s == 0:
                lo, hi = arrs[i], arrs[i + s]
                nxt[i] = jnp.where(keep, lo, pltpu.roll(hi, s * SSM_GROUP, axis=1))
                nxt[i + s] = jnp.where(keep, pltpu.roll(lo, LANES - s * SSM_GROUP, axis=1), hi)
        arrs = nxt
        s //= 2
    return arrs


def _s5_kernel(u_lo_ref, u_hi_ref, um_ref, krow_ref, bmat_ref, cre_ref, cim_ref, ar_ref, ai_ref, wglu_ref,
               y_lo_ref, y_hi_ref, t0_ref, ug_ref, yg_ref):
    n_blocks = u_lo_ref.shape[0] // S5_BLOCK

    @pl.when(pl.program_id(0) == 0)
    def _():
        lane = lax.broadcasted_iota(I32, (SSM_GROUP, S5_LANES), 1)
        for g in range(SSM_GROUPS):
            k0 = krow_ref[g]
            for a in range(S5_BLOCK):
                blk = k0 if a == 0 else jnp.where(lane >= a * SSM_GROUP,
                                                  pltpu.roll(k0, a * SSM_GROUP, axis=1), 0.0)
                t0_ref[g, a * SSM_GROUP:(a + 1) * SSM_GROUP, :] = blk.astype(BF16)

    per_tile = LANES // SSM_GROUP
    for half, uh_ref in enumerate((u_lo_ref, u_hi_ref)):
        for t in range(S5_BLOCK // per_tile):
            words = [pltpu.bitcast(uh_ref[pl.ds(t * per_tile + k, n_blocks, stride=S5_BLOCK), :].astype(BF16),
                                   jnp.uint32) for k in range(per_tile)]
            for m, w in enumerate(_chunk_transpose(words)):
                ug_ref[half * per_tile + m, :, t * LANES:(t + 1) * LANES] = pltpu.bitcast(w, BF16)

    re, im, re0, im0 = [], [], [], []
    for p in range(SSM_GROUPS // 2):
        g0, g1 = 2 * p, 2 * p + 1
        v = _dot(ug_ref[g0], bmat_ref[g0]) + _dot(ug_ref[g1], bmat_ref[g1])
        v0 = (_dot(um_ref[g0].astype(BF16), bmat_ref[g0]) + _dot(um_ref[g1].astype(BF16), bmat_ref[g1]))[0:1]
        re.append(v[:, :LANES])
        im.append(v[:, LANES:])
        re0.append(v0[:, :LANES])
        im0.append(v0[:, LANES:])
    re, im, re0, im0 = (jnp.concatenate(parts, axis=1) for parts in (re, im, re0, im0))

    row = lax.broadcasted_iota(I32, re.shape, 0)
    ar, ai = ar_ref[0:1], ai_ref[0:1]
    re = re + jnp.where(row == 0, ar * re0 - ai * im0, 0.0)
    im = im + jnp.where(row == 0, ar * im0 + ai * re0, 0.0)
    d = 1
    while d < SCAN_ROWS:
        ar, ai = ar_ref[d - 1:d], ai_ref[d - 1:d]
        inside = row % SCAN_ROWS >= d
        sre = jnp.where(inside, pltpu.roll(re, d, axis=0), 0.0)
        sim = jnp.where(inside, pltpu.roll(im, d, axis=0), 0.0)
        re, im = re + ar * sre - ai * sim, im + ar * sim + ai * sre
        d *= 2
    ar, ai = ar_ref[...], ai_ref[...]
    re_tiles, im_tiles = [re[:SCAN_ROWS]], [im[:SCAN_ROWS]]
    for r in range(1, n_blocks // SCAN_ROWS):
        cre_ = re_tiles[-1][SCAN_ROWS - 1:SCAN_ROWS]
        cim_ = im_tiles[-1][SCAN_ROWS - 1:SCAN_ROWS]
        rs = slice(r * SCAN_ROWS, (r + 1) * SCAN_ROWS)
        re_tiles.append(re[rs] + ar * cre_ - ai * cim_)
        im_tiles.append(im[rs] + ar * cim_ + ai * cre_)
    re = jnp.concatenate(re_tiles, axis=0)
    im = jnp.concatenate(im_tiles, axis=0)
    pre = jnp.where(row == 0, re0, pltpu.roll(re, 1, axis=0)).astype(BF16)
    pim = jnp.where(row == 0, im0, pltpu.roll(im, 1, axis=0)).astype(BF16)

    for g in range(SSM_GROUPS):
        ps = slice((g // 2) * LANES, (g // 2 + 1) * LANES)
        yg_ref[g] = (_dot(ug_ref[g], t0_ref[g]) + _dot_t(pre[:, ps], cre_ref[g])
                     + _dot_t(pim[:, ps], cim_ref[g])).astype(BF16)

    wglu = wglu_ref[...]
    for t in range(S5_BLOCK // per_tile):
        ts = slice(t * LANES, (t + 1) * LANES)
        halves = [_chunk_transpose([pltpu.bitcast(yg_ref[half * per_tile + m, :, ts], jnp.uint32)
                                    for m in range(per_tile)])
                  for half in range(SSM_GROUPS // per_tile)]
        for k in range(per_tile):
            y = jnp.concatenate([pltpu.bitcast(h[k], BF16) for h in halves], axis=1).astype(F32)
            y = jax.nn.gelu(y, approximate=True)
            y = y * _sigmoid(_dot(y.astype(BF16), wglu))
            i = t * per_tile + k
            y_lo_ref[pl.ds(i, n_blocks, stride=S5_BLOCK), :] = y[:, :LANES]
            y_hi_ref[pl.ds(i, n_blocks, stride=S5_BLOCK), :] = y[:, LANES:]


def _s5_call(u, um, krow, bmat, cre, cim, ar, ai, wglu):
    bsz, seq, _ = u.shape
    n_blocks = seq // S5_BLOCK
    const = lambda a: pl.BlockSpec(a.shape, lambda b: (0,) * a.ndim)
    return pl.pallas_call(
        _s5_kernel,
        grid=(bsz,),
        in_specs=[pl.BlockSpec((None, seq, LANES), lambda b: (b, 0, 0)),
                  pl.BlockSpec((None, seq, LANES), lambda b: (b, 0, 1)),
                  const(um), const(krow), const(bmat), const(cre), const(cim), const(ar), const(ai),
                  const(wglu)],
        out_specs=(pl.BlockSpec((None, seq, LANES), lambda b: (b, 0, 0)),
                   pl.BlockSpec((None, seq, LANES), lambda b: (b, 0, 0))),
        out_shape=(jax.ShapeDtypeStruct((bsz, seq, LANES), F32),
                   jax.ShapeDtypeStruct((bsz, seq, LANES), F32)),
        scratch_shapes=[pltpu.VMEM((SSM_GROUPS, S5_LANES, S5_LANES), BF16),
                        pltpu.VMEM((SSM_GROUPS, n_blocks, S5_LANES), BF16),
                        pltpu.VMEM((SSM_GROUPS, n_blocks, S5_LANES), BF16)],
        compiler_params=pltpu.CompilerParams(
            dimension_semantics=("arbitrary",), vmem_limit_bytes=VMEM_LIMIT),
        name="s5_call",
    )(u, u, um, krow, bmat, cre, cim, ar, ai, wglu)


def _s5_operators(lam_re, lam_im, log_dt, b_re, b_im, c_re, c_im, d_skip):
    n_groups = lam_re.shape[0]
    lam = lax.complex(lam_re, lam_im)
    lam_dt = lam * jnp.exp(log_dt)[:, None]
    lam_bar = jnp.exp(lam_dt)
    b_bar = ((lam_bar - 1.0) / lam)[..., None] * lax.complex(b_re, b_im)
    c = lax.complex(c_re, c_im)
    tau = jnp.arange(S5_BLOCK + 1, dtype=F32)
    pows = jnp.exp(lam_dt[None] * tau[:, None, None])
    kern = jnp.real(jnp.einsum('ghp,tgp,gpk->gkth', c, pows[:S5_BLOCK], b_bar))
    skip = (jnp.eye(SSM_GROUP, dtype=F32)[None, :, None, :] * d_skip[:, None, None, :]
            * (tau[:S5_BLOCK] == 0).astype(F32)[None, None, :, None])
    krow = (kern + skip).reshape(n_groups, SSM_GROUP, S5_LANES)
    even = (jnp.arange(n_groups) % 2 == 0)[:, None, None]
    pair_pad = lambda m: jnp.concatenate([jnp.where(even, m, 0.0), jnp.where(even, 0.0, m)], axis=-1)
    bm = (pows[S5_BLOCK - 1 - jnp.arange(S5_BLOCK)].transpose(1, 0, 2)[:, :, None, :]
          * b_bar.transpose(0, 2, 1)[:, None, :, :]).reshape(n_groups, S5_LANES, SSM_STATE)
    bmat = jnp.concatenate([pair_pad(jnp.real(bm)), pair_pad(jnp.imag(bm))], axis=-1)
    cm = (pows[1:].transpose(1, 0, 2)[:, :, None, :] * c[:, None, :, :]).reshape(n_groups, S5_LANES, SSM_STATE)
    cre, cim = pair_pad(jnp.real(cm)), pair_pad(-jnp.imag(cm))
    step = S5_BLOCK * (1.0 + jnp.arange(SCAN_ROWS, dtype=F32))
    adec = jnp.exp(lam_dt[None, :, :] * step[:, None, None]).reshape(SCAN_ROWS, n_groups * SSM_STATE)
    return krow, bmat.astype(BF16), cre.astype(BF16), cim.astype(BF16), jnp.real(adec), jnp.imag(adec)


def _first_hit(values, target):
    hits, taken = [], None
    for v in values:
        hit = (v >= target) if taken is None else jnp.logical_and(v >= target, jnp.logical_not(taken))
        taken = hit if taken is None else jnp.logical_or(taken, hit)
        hits.append(hit)
    return hits


def _stack_rows(rows, n_rows):
    idx = lax.broadcasted_iota(I32, (n_rows, rows[0].shape[1]), 0)
    out = jnp.zeros((n_rows, rows[0].shape[1]), F32)
    for k, r in enumerate(rows):
        out = jnp.where(idx == k, r, out)
    return out


def _project(rs, x_ref, ys_lo_ref, ys_hi_ref, yr_ref, wout_ref, h_ref):
    ys = jnp.concatenate([ys_lo_ref[rs, :], ys_hi_ref[rs, :]], axis=1).astype(BF16)
    h = x_ref[rs, :] + _dot(ys, wout_ref[:SSM_WIDTH, :]) + _dot(yr_ref[rs, :], wout_ref[SSM_WIDTH:, :])
    h_ref[rs, :] = h.astype(BF16)
    return h


def _route(lt, tri_ref):
    gl = [lt[g:g + 1, :] for g in range(N_GROUPS)]
    gmax = functools.reduce(jnp.maximum, gl)
    g_w = 1.0 / functools.reduce(lambda a, b: a + b, [jnp.exp(l - gmax) for l in gl])
    sel = _first_hit(gl, gmax)
    ev = []
    for e in range(EXPERTS_PER_GROUP):
        acc = jnp.zeros_like(gmax)
        for g in range(N_GROUPS):
            k = N_GROUPS + g * EXPERTS_PER_GROUP + e
            acc = jnp.where(sel[g], lt[k:k + 1, :], acc)
        ev.append(acc)
    m1 = functools.reduce(jnp.maximum, ev)
    first = _first_hit(ev, m1)
    rest = [jnp.where(f, -jnp.inf, v) for f, v in zip(first, ev)]
    m2 = functools.reduce(jnp.maximum, rest)
    second = _first_hit(rest, m2)
    e2 = jnp.exp(m2 - m1)
    w1 = g_w / (1.0 + e2)
    w2 = e2 * w1
    combine = [jnp.where(f, w1, 0.0) + jnp.where(s, w2, 0.0) for f, s in zip(first, second)]

    sel_f = [jnp.where(s, 1.0, 0.0) for s in sel]
    incl = _dot(_stack_rows(sel_f, 8).astype(BF16), tri_ref[...])
    dest = jnp.zeros_like(gmax)
    seg_start = jnp.zeros((1, 1), F32)
    counts = []
    for g in range(N_GROUPS):
        run = incl[g:g + 1, :]
        cnt = run[:, TILE - 1:TILE]
        counts.append(cnt)
        dest = dest + sel_f[g] * (seg_start + run - 1.0)
        seg_start = seg_start + PIECE * jnp.floor((cnt + (PIECE - 1.0)) * (1.0 / PIECE))
    return combine, dest, counts


def _proj_kernel(x_ref, ys_lo_ref, ys_hi_ref, yr_ref, wout32_ref, g_ref, wr_ref, br_ref,
                 tri_ref, h_ref, stage_ref, dest_ref, cnt_ref, wout_ref):
    @pl.when(pl.program_id(0) == 0)
    def _():
        wout_ref[...] = wout32_ref[...].astype(BF16)

    tiles = range(PROJ_TILES)
    proj_refs = (x_ref, ys_lo_ref, ys_hi_ref, yr_ref, wout_ref, h_ref)
    chunks = [[slice(tile * TILE + b * ROW_CHUNK, tile * TILE + (b + 1) * ROW_CHUNK)
               for b in range(TILE // ROW_CHUNK)] for tile in tiles]
    h_parts = [[_project(rs, *proj_refs) for rs in chunks[tile]] for tile in tiles]

    logits, staged = [], []
    for tile in tiles:
        parts, rows, inv = [], [], []
        for h in h_parts[tile]:
            t = _rms_norm(h, g_ref[...])
            parts.append(t.astype(BF16))
            peak = jnp.maximum(jnp.max(jnp.abs(t), axis=1, keepdims=True), 1e-30)
            rows.append((t * (FP8_HEADROOM / peak)).astype(BF16))
            inv.append(peak * (1.0 / FP8_HEADROOM))
        staged.append((jnp.concatenate(rows, axis=0), jnp.concatenate(inv, axis=0)))
        lt_parts = []
        for t_hi in parts:
            both = _dot_t(wr_ref[...], t_hi)
            lt_parts.append(both[:ROUTE_ROWS] + both[ROUTE_ROWS:])
        logits.append(jnp.concatenate(lt_parts, axis=1) + br_ref[...])

    routed = [_route(logits[tile], tri_ref) for tile in tiles]

    for tile in tiles:
        combine, dest, counts = routed[tile]
        dest_ref[tile] = _stack_rows([dest], 8)
        cnt_ref[tile] = _stack_rows([c + jnp.zeros((1, LANES), F32) for c in counts], 8)
        perm = jnp.where(lax.broadcasted_iota(I32, (SORTED_ROWS, TILE), 0) == dest.astype(I32),
                         1.0, 0.0).astype(BF16)
        c_hi = [c.astype(BF16).astype(F32) for c in combine]
        c_lo = [c - hi for c, hi in zip(combine, c_hi)]
        cw = _stack_rows(c_hi + c_lo, LANES).T
        rows, inv = staged[tile]
        inv_hi = inv.astype(BF16).astype(F32)
        lane = lax.broadcasted_iota(I32, (1, LANES), 1)
        cw = jnp.where(lane == INV_LANE, inv_hi, jnp.where(lane == INV_LANE + 1, inv - inv_hi, cw))
        t_ext = jnp.concatenate([rows, cw.astype(BF16)], axis=1)
        stage_ref[tile * SORTED_ROWS:(tile + 1) * SORTED_ROWS, :] = _dot(perm, t_ext).astype(BF16)


def _proj_call(x, ys_lo, ys_hi, yr, wout, g, wr, br, tri):
    n_tok = x.shape[0]
    n_tiles = n_tok // TILE
    const = lambda *shape: pl.BlockSpec(shape, lambda i: (0,) * len(shape))
    rows = lambda width: pl.BlockSpec((PROJ_TILES * TILE, width), lambda i: (i, 0))
    return pl.pallas_call(
        _proj_kernel,
        grid=(n_tiles // PROJ_TILES,),
        in_specs=[
            rows(D_MODEL), rows(LANES), rows(LANES), rows(RET_WIDTH),
            pl.BlockSpec((D_MODEL, D_MODEL), lambda i: (0, 0), pipeline_mode=pl.Buffered(1)), const(1, D_MODEL),
            const(2 * ROUTE_ROWS, D_MODEL), const(ROUTE_ROWS, 1), const(TILE, TILE),
        ],
        out_specs=(rows(D_MODEL),
                   pl.BlockSpec((PROJ_TILES * SORTED_ROWS, EXT_WIDTH), lambda i: (i, 0)),
                   pl.BlockSpec((PROJ_TILES, 8, TILE), lambda i: (i, 0, 0)),
                   pl.BlockSpec((PROJ_TILES, 8, LANES), lambda i: (i, 0, 0))),
        scratch_shapes=[pltpu.VMEM((D_MODEL, D_MODEL), BF16)],
        out_shape=(jax.ShapeDtypeStruct((n_tok, D_MODEL), BF16),
                   jax.ShapeDtypeStruct((n_tiles * SORTED_ROWS, EXT_WIDTH), BF16),
                   jax.ShapeDtypeStruct((n_tiles, 8, TILE), F32),
                   jax.ShapeDtypeStruct((n_tiles, 8, LANES), F32)),
        compiler_params=pltpu.CompilerParams(
            dimension_semantics=("arbitrary",), vmem_limit_bytes=VMEM_LIMIT),
        name="proj_call",
    )(x, ys_lo, ys_hi, yr, wout, g, wr, br, tri)


def _sort_tables(cnt, n_steps):
    n_tiles = cnt.shape[0]
    npc = (cnt + PIECE - 1) // PIECE
    seg = jnp.cumsum(npc, axis=1) - npc
    before = jnp.cumsum(npc, axis=0) - npc
    n_tile_g = (jnp.sum(npc, axis=0) + MOE_PIECES - 1) // MOE_PIECES
    t_off = jnp.cumsum(n_tile_g) - n_tile_g
    j = jnp.arange(SORTED_PIECES, dtype=I32)[None, :, None]
    in_g = jnp.logical_and(j >= seg[:, None, :], j < (seg + npc)[:, None, :])
    pos = jnp.sum(jnp.where(in_g, MOE_PIECES * t_off[None, None, :] + before[:, None, :] + j - seg[:, None, :], 0),
                  axis=-1)
    valid = jnp.any(in_g, axis=-1)
    steps = jnp.arange(n_steps, dtype=I32)
    g_step = jnp.minimum(jnp.sum(steps[:, None] >= (t_off + n_tile_g)[None, :], axis=1), N_GROUPS - 1)
    slot = jnp.arange(n_steps * MOE_PIECES, dtype=I32)
    g_slot = jnp.repeat(g_step, MOE_PIECES)[:, None]
    p = slot[:, None] - MOE_PIECES * t_off[None, :]
    end = before + npc
    off = seg - before + SORTED_PIECES * jnp.arange(n_tiles, dtype=I32)[:, None]
    passed = end[None, :-1, :] <= p[:, None, :]
    piece = p + off[0][None, :] + jnp.sum(jnp.where(passed, (off[1:] - off[:-1])[None], 0), axis=1)
    live = jnp.logical_and(jnp.arange(N_GROUPS, dtype=I32)[None, :] == g_slot,
                           jnp.logical_and(p >= 0, p < end[-1][None, :]))
    n_stage = n_tiles * SORTED_PIECES
    src = jnp.where(jnp.any(live, axis=1), jnp.sum(jnp.where(live, piece, 0), axis=1), slot % n_stage).astype(I32)
    n_live = jnp.sum(n_tile_g).reshape(1)
    back = jnp.where(valid, pos, jnp.arange(n_stage, dtype=I32).reshape(n_tiles, SORTED_PIECES)).reshape(-1)
    return src, g_step.astype(I32), n_live.astype(I32), back.astype(I32)


def _piece_copy(src_ref, piece, buf_ref, slot, j, sem_ref):
    start = piece * PIECE if isinstance(piece, int) else pl.multiple_of(piece * PIECE, PIECE)
    return pltpu.make_async_copy(src_ref.at[pl.ds(start, PIECE)],
                                 buf_ref.at[slot, pl.ds(j * PIECE, PIECE)], sem_ref.at[slot])


def _fetch_pieces(table_ref, src_ref, buf_ref, sem_ref, row, slot, n_pieces):
    for j in range(n_pieces):
        _piece_copy(src_ref, table_ref[row * n_pieces + j], buf_ref, slot, j, sem_ref).start()


def _wait_pieces(src_ref, buf_ref, sem_ref, slot, n_pieces):
    for j in range(n_pieces):
        _piece_copy(src_ref, 0, buf_ref, slot, j, sem_ref).wait()


def _gather_pieces(table_ref, src_ref, buf_ref, sem_ref, n_pieces, n_live=None):
    step = pl.program_id(0)
    last = pl.num_programs(0) - 1 if n_live is None else n_live - 1
    args = (src_ref, buf_ref, sem_ref)

    @pl.when(step == 0)
    def _():
        _fetch_pieces(table_ref, *args, step, 0, n_pieces)

    if n_live is None:
        _wait_pieces(*args, step % 2, n_pieces)
    else:
        @pl.when(step <= last)
        def _():
            _wait_pieces(*args, step % 2, n_pieces)

    def fetch_next():
        _fetch_pieces(table_ref, *args, jnp.minimum(step + 1, last), (step + 1) % 2, n_pieces)

    def drain():
        @pl.when(step == last)
        def _():
            _wait_pieces(*args, (step + 1) % 2, n_pieces)

    return fetch_next, drain


def _to_fp8(x, headroom=FP8_HEADROOM):
    peak = jnp.maximum(jnp.max(jnp.abs(x), axis=(0, 1), keepdims=True), 1e-30)
    return (x * (headroom / peak)).astype(FP8), peak * (1.0 / headroom)


def _moe_kernel(src_ref, gstep_ref, nlive_ref, stage_ref, wg_ref, wu_ref, wd_ref, y_ref,
                buf_ref, sem_ref, wgb_ref, wub_ref, wdb_ref, inv_ref):
    step = pl.program_id(0)
    fetch_next, drain = _gather_pieces(src_ref, stage_ref, buf_ref, sem_ref, MOE_PIECES, nlive_ref[0])

    @pl.when(jnp.logical_or(step == 0, gstep_ref[step] != gstep_ref[jnp.maximum(step - 1, 0)]))
    def _():
        ones = jnp.ones((1, LANES), F32)
        for e in range(EXPERTS_PER_GROUP):
            wgb_ref[e], inv = _to_fp8(wg_ref[e])
            inv_ref[e:e + 1, :] = inv * ones
            wub_ref[e], inv = _to_fp8(wu_ref[e])
            inv_ref[EXPERTS_PER_GROUP + e:EXPERTS_PER_GROUP + e + 1, :] = inv * ones
        peak = functools.reduce(jnp.maximum, [jnp.max(jnp.abs(wd_ref[e]), axis=(0, 1), keepdims=True)
                                              for e in range(EXPERTS_PER_GROUP)])
        peak = jnp.maximum(peak, 1e-30)
        for e in range(EXPERTS_PER_GROUP):
            wdb_ref[e * EXPERT_FF:(e + 1) * EXPERT_FF, :] = (wd_ref[e] * (FP8_HEADROOM / peak)).astype(FP8)
        inv_ref[2 * EXPERTS_PER_GROUP:2 * EXPERTS_PER_GROUP + 1, :] = peak * (1.0 / FP8_HEADROOM) * ones

    @pl.when(step < nlive_ref[0])
    def _():
        slot = step % 2
        fetch_next()

        def up(rs):
            t = buf_ref[slot, rs, :D_MODEL].astype(FP8)
            t_inv = (buf_ref[slot, rs, D_MODEL + INV_LANE:D_MODEL + INV_LANE + 1].astype(F32)
                     + buf_ref[slot, rs, D_MODEL + INV_LANE + 1:D_MODEL + INV_LANE + 2].astype(F32))
            out = []
            for e in range(EXPERTS_PER_GROUP):
                g_inv = inv_ref[e:e + 1, 0:1] * t_inv
                u_inv = inv_ref[EXPERTS_PER_GROUP + e:EXPERTS_PER_GROUP + e + 1, 0:1] * t_inv
                out.append((_dot(t, wgb_ref[e]) * g_inv, _dot(t, wub_ref[e]), u_inv))
            return out

        def down(rs, hidden):
            cw = buf_ref[slot, rs, D_MODEL:].astype(F32)
            acts = []
            for e, (hg, hu, u_inv) in enumerate(hidden):
                c = (cw[:, e:e + 1] + cw[:, EXPERTS_PER_GROUP + e:EXPERTS_PER_GROUP + e + 1]) * u_inv
                acts.append(hg * _sigmoid(hg) * hu * c)
            act, a_inv = _to_fp8(jnp.concatenate(acts, axis=1))
            d_inv = inv_ref[2 * EXPERTS_PER_GROUP:2 * EXPERTS_PER_GROUP + 1, 0:1] * a_inv
            y_ref[rs, :] = (_dot(act, wdb_ref[...]) * d_inv).astype(BF16)

        chunks = [slice(b * ROW_CHUNK, (b + 1) * ROW_CHUNK) for b in range(MOE_TILE // ROW_CHUNK)]
        hidden = [up(rs) for rs in chunks]
        for rs, hid in zip(chunks, hidden):
            down(rs, hid)

    @pl.when(step >= nlive_ref[0])
    def _():
        y_ref[...] = jnp.zeros_like(y_ref)

    drain()


def _moe_call(src, g_step, n_live, stage, wg, wu, wd, n_steps):
    grp = lambda shape: pl.BlockSpec((EXPERTS_PER_GROUP,) + shape, lambda s, src, gs, nl: (gs[s], 0, 0))
    return pl.pallas_call(
        _moe_kernel,
        grid_spec=pltpu.PrefetchScalarGridSpec(
            num_scalar_prefetch=3,
            grid=(n_steps,),
            in_specs=[pl.BlockSpec(memory_space=pl.ANY),
                      grp((D_MODEL, EXPERT_FF)), grp((D_MODEL, EXPERT_FF)), grp((EXPERT_FF, D_MODEL))],
            out_specs=pl.BlockSpec((MOE_TILE, D_MODEL), lambda s, src, gs, nl: (s, 0)),
            scratch_shapes=[pltpu.VMEM((2, MOE_TILE, EXT_WIDTH), BF16), pltpu.SemaphoreType.DMA((2,)),
                            pltpu.VMEM((EXPERTS_PER_GROUP, D_MODEL, EXPERT_FF), FP8),
                            pltpu.VMEM((EXPERTS_PER_GROUP, D_MODEL, EXPERT_FF), FP8),
                            pltpu.VMEM((EXPERTS_PER_GROUP * EXPERT_FF, D_MODEL), FP8),
                            pltpu.VMEM((16, LANES), F32)],
        ),
        out_shape=jax.ShapeDtypeStruct((n_steps * MOE_TILE, D_MODEL), BF16),
        compiler_params=pltpu.CompilerParams(
            dimension_semantics=("arbitrary",), vmem_limit_bytes=VMEM_LIMIT),
        name="moe_call",
    )(src, g_step, n_live, stage, wg, wu, wd)


def _final_kernel(back_ref, ysort_ref, h_ref, dest_ref, g_ref, o_ref, buf_ref, sem_ref):
    step = pl.program_id(0)
    fetch_next, drain = _gather_pieces(back_ref, ysort_ref, buf_ref, sem_ref, FINAL_TILES * SORTED_PIECES)
    fetch_next()
    for t in range(FINAL_TILES):
        dest = _stack_rows([dest_ref[t, 0:1, :]], LANES).T[:, 0:1].astype(I32)
        unperm = jnp.where(lax.broadcasted_iota(I32, (TILE, SORTED_ROWS), 1) == dest, 1.0, 0.0).astype(BF16)
        rows = buf_ref[step % 2, t * SORTED_ROWS:(t + 1) * SORTED_ROWS, :]
        for b in range(TILE // ROW_CHUNK):
            rs = slice(b * ROW_CHUNK, (b + 1) * ROW_CHUNK)
            os = slice(t * TILE + b * ROW_CHUNK, t * TILE + (b + 1) * ROW_CHUNK)
            o_ref[os, :] = _rms_norm(h_ref[os, :].astype(F32) + _dot(unperm[rs], rows), g_ref[...])
    drain()


def _final_call(back, ysort, h, dest, g):
    n_tok = h.shape[0]
    return pl.pallas_call(
        _final_kernel,
        grid_spec=pltpu.PrefetchScalarGridSpec(
            num_scalar_prefetch=1,
            grid=(n_tok // (FINAL_TILES * TILE),),
            in_specs=[pl.BlockSpec(memory_space=pl.ANY),
                      pl.BlockSpec((FINAL_TILES * TILE, D_MODEL), lambda i, back: (i, 0)),
                      pl.BlockSpec((FINAL_TILES, 8, TILE), lambda i, back: (i, 0, 0)),
                      pl.BlockSpec((1, D_MODEL), lambda i, back: (0, 0))],
            out_specs=pl.BlockSpec((FINAL_TILES * TILE, D_MODEL), lambda i, back: (i, 0)),
            scratch_shapes=[pltpu.VMEM((2, FINAL_TILES * SORTED_ROWS, D_MODEL), BF16),
                            pltpu.SemaphoreType.DMA((2,))],
        ),
        out_shape=jax.ShapeDtypeStruct((n_tok, D_MODEL), F32),
        compiler_params=pltpu.CompilerParams(
            dimension_semantics=("arbitrary",), vmem_limit_bytes=VMEM_LIMIT),
        name="final_call",
    )(back, ysort, h, dest, g)


def _rope_tables(length):
    pos = np.arange(length, dtype=np.float32)
    inv_freq = np.float32(ROPE_BASE) ** (-np.arange(0, HEAD_DIM, 2, dtype=np.float32) / np.float32(HEAD_DIM))
    ang = pos[:, None] * inv_freq[None, :]
    cos, sin = np.cos(ang), np.sin(ang)
    return np.concatenate([cos, cos], axis=-1), np.concatenate([-sin, sin], axis=-1)


def _retention_tables():
    f32 = np.float32
    gamma = f32(1.0) - f32(2.0) ** (f32(-5.0) - np.arange(HEADS, dtype=f32))
    log_g = np.log(gamma)[:, None, None]
    scale = f32(HEAD_DIM ** -0.5)
    idx = np.arange(RET_BLOCK)
    dist = np.abs(idx[:, None] - idx[None, :]).astype(f32)
    visible = (idx[None, :] // CHUNK) <= (idx[:, None] // CHUNK)
    mask = np.where(visible[None], np.exp(log_g * dist[None]), f32(0.0)) * scale
    ones = np.ones((1, 1, HEAD_DIM), f32)
    idx_f = idx.astype(f32)[None, :, None]
    pair = lambda a: a.reshape(HEAD_PAIRS, 2, a.shape[1], HEAD_DIM).transpose(0, 2, 1, 3).reshape(
        HEAD_PAIRS, a.shape[1], PAIR_DIM)
    qdec = pair(np.exp(log_g * (idx_f + f32(1.0))) * ones)
    kdec = pair(np.exp(log_g * (f32(RET_BLOCK - 1.0) - idx_f)) * scale * ones)
    meta_idx = np.arange(N_META, dtype=f32)[None, :, None]
    kdec_meta = pair(np.exp(log_g * (f32(N_META - 1.0) - meta_idx)) * scale * ones)
    blk = np.kron(np.eye(2, dtype=f32), np.ones((HEAD_DIM, HEAD_DIM), f32))
    bdec = np.exp(log_g * f32(RET_BLOCK)).reshape(HEAD_PAIRS, 2)
    bdec = np.stack([np.kron(np.diag(b), np.ones((HEAD_DIM, HEAD_DIM), f32)) for b in bdec])
    return tuple(a.astype(f32) for a in (mask, qdec, kdec, bdec, blk, kdec_meta))


def kernel(x, meta_tokens, norm_mix_g, w_in, ssm_lambda_re, ssm_lambda_im, ssm_log_dt, ssm_b_re, ssm_b_im, ssm_c_re, ssm_c_im, ssm_d, w_glu, w_out, norm_ffn_g, w_router_group, b_router_group, w_router_expert, b_router_expert, w_gate, w_up, w_down, norm_final_g):
    bsz, seq, _ = x.shape
    assert seq % TILE == 0 and seq % MIXER_TILE == 0 and MIXER_TILE % RET_BLOCK == 0 and RET_BLOCK % CHUNK == 0
    n_blocks = seq // S5_BLOCK
    assert n_blocks % SCAN_ROWS == 0
    n_tok = bsz * seq
    assert n_tok % (FINAL_TILES * TILE) == 0 and n_tok % (PROJ_TILES * TILE) == 0
    n_tiles = n_tok // TILE
    n_steps = -(-n_tiles * (TILE_PIECES + N_GROUPS - 1) // MOE_PIECES) + N_GROUPS

    cos, sin = _rope_tables(N_META + seq)
    mask, qdec, kdec, bdec, blk, kdec_meta = _retention_tables()
    g_mix = norm_mix_g[0][None, :]
    u, y_ret, u_meta = _mixer_call(x, g_mix, w_in[0], cos[N_META:], sin[N_META:], mask, qdec, kdec, bdec, blk,
                                   meta_tokens, cos[:N_META], sin[:N_META], kdec_meta)

    s5_ops = _s5_operators(
        ssm_lambda_re[0], ssm_lambda_im[0], ssm_log_dt[0], ssm_b_re[0], ssm_b_im[0],
        ssm_c_re[0], ssm_c_im[0], ssm_d[0])
    um = u_meta.reshape(S5_BLOCK, SSM_GROUPS, SSM_GROUP).transpose(1, 0, 2).reshape(SSM_GROUPS, 1, S5_LANES)
    um = jnp.pad(um, ((0, 0), (0, 7), (0, 0)))
    y_lo, y_hi = _s5_call(u, um, *s5_ops, w_glu[0].astype(BF16))

    w_r = jnp.concatenate(
        [w_router_group[0].T, w_router_expert[0].transpose(0, 2, 1).reshape(N_EXPERTS, D_MODEL)], axis=0)
    w_r = jnp.pad(w_r, ((0, ROUTE_ROWS - w_r.shape[0]), (0, 0)))
    b_r = jnp.concatenate([b_router_group[0], b_router_expert[0].reshape(-1)])
    b_r = jnp.pad(b_r, (0, ROUTE_ROWS - b_r.shape[0]))[:, None]
    w_r_hi = w_r.astype(BF16)
    w_r = jnp.concatenate([w_r_hi, (w_r - w_r_hi.astype(F32)).astype(BF16)], axis=0)
    tri = jnp.asarray(np.arange(TILE)[:, None] <= np.arange(TILE)[None, :], BF16)

    h, stage, dest, cnt = _proj_call(
        x.reshape(n_tok, D_MODEL), y_lo.reshape(n_tok, LANES), y_hi.reshape(n_tok, LANES),
        y_ret.reshape(n_tok, RET_WIDTH),
        w_out[0], norm_ffn_g[0][None, :], w_r, b_r, tri)
    src, g_step, n_live, back = _sort_tables(cnt[:, :N_GROUPS, 0].astype(I32), n_steps)
    y_sorted = _moe_call(src, g_step, n_live, stage, w_gate[0], w_up[0], w_down[0], n_steps)
    out = _final_call(back, y_sorted, h, dest, norm_final_g[None, :])
    return out.reshape(bsz, seq, D_MODEL)
```
